```python
import math
import jax, jax.numpy as jnp
from jax import lax
import numpy as np

D_MODEL = 4096
BATCH = 8
SEQ = 4096
DEPTH = 2

ATTN_PATTERNS = ((128, 1), (512, 4), (2048, 16))
N_GROUPS_ATTN = 3
HEADS_PER_GROUP = 16
HEAD_DIM = 128
D_ATTN = HEADS_PER_GROUP * HEAD_DIM
ATTN_BLOCK = 128
QKV_COLS = N_GROUPS_ATTN * 3 * D_ATTN
IN_ATTN_COLS = QKV_COLS + D_ATTN

NUM_BUCKETS = 32
MAX_DISTANCE = 2048
N_BIAS_HEADS = N_GROUPS_ATTN * HEADS_PER_GROUP

EXPAND = 2
D_INNER = EXPAND * D_MODEL
SSM_HEAD_DIM = 64
SSM_HEADS = D_INNER // SSM_HEAD_DIM
SSM_GROUPS = 8
HEADS_PER_SSM_GROUP = SSM_HEADS // SSM_GROUPS
D_STATE = 128
CONV_WIDTH = 4
CONV_DIM = D_INNER + 2 * SSM_GROUPS * D_STATE
IN_SSM_COLS = D_INNER + CONV_DIM + SSM_HEADS
CHUNK = 128

N_MIXERS = 2
N_ATTN_LAYERS = (DEPTH + 1) // 2
N_SSM_LAYERS = DEPTH // 2
DEEPNORM_ALPHA = (2 * DEPTH) ** 0.25
DEEPNORM_BETA = (8 * DEPTH) ** -0.25
LN_EPS = 1e-5
RMS_EPS = 1e-5
NEG_INF = -1e30

kernel_name = 'hybrid_dilated_attn_mamba2_deepnorm'


def t5_causal_bucket(dist):
    max_exact = NUM_BUCKETS // 2
    d_f = jnp.maximum(dist, 1).astype(jnp.float32)
    large = max_exact + (jnp.log(d_f / max_exact) / math.log(MAX_DISTANCE / max_exact)
                         * (NUM_BUCKETS - max_exact)).astype(jnp.int32)
    large = jnp.minimum(large, NUM_BUCKETS - 1)
    return jnp.where(dist < max_exact, dist, large)


def dilated_group_attention(q, k, v, bias_table, window, dilation):
    b, s, h, dh = q.shape
    span = window // dilation
    L = s // dilation
    nb = -(-L // ATTN_BLOCK)
    lp = nb * ATTN_BLOCK

    def to_sub(t):
        t = t.reshape(b, L, dilation, h, dh).transpose(0, 2, 1, 3, 4)
        t = jnp.pad(t, ((0, 0), (0, 0), (0, lp - L), (0, 0), (0, 0)))
        return t.reshape(b, dilation, nb, ATTN_BLOCK, h, dh)

    def with_prev(t):
        prev = jnp.pad(t, ((0, 0), (0, 0), (1, 0), (0, 0), (0, 0), (0, 0)))[:, :, :-1]
        return jnp.concatenate([prev, t], axis=3)

    qb = to_sub(q)
    kk = with_prev(to_sub(k))
    vv = with_prev(to_sub(v))

    qi = jnp.arange(ATTN_BLOCK)[:, None]
    ki = jnp.arange(2 * ATTN_BLOCK)[None, :]
    delta = ATTN_BLOCK + qi - ki
    band = (delta >= 0) & (delta <= span)
    not_first = (jnp.arange(nb) > 0)[:, None, None]
    valid = band[None] & (not_first | (ki >= ATTN_BLOCK)[None])
    bucket = t5_causal_bucket(jnp.clip(delta, 0, None) * dilation)
    bias = bias_table.astype(jnp.float32)[bucket].transpose(2, 0, 1)

    logits = jnp.einsum('brnqhd,brnkhd->brnhqk', qb, kk).astype(jnp.float32)
    logits = logits * (dh ** -0.5) + bias[None, None, None]
    logits = jnp.where(valid[None, None, :, None], logits, NEG_INF)
    m = jnp.max(logits, axis=-1, keepdims=True)
    p = jnp.exp(logits - m)
    denom = jnp.sum(p, axis=-1, keepdims=True)
    o = jnp.einsum('brnhqk,brnkhd->brnqhd', p / denom, vv.astype(jnp.float32))
    lse = (m + jnp.log(denom))[..., 0]

    o = o.reshape(b, dilation, lp, h, dh)[:, :, :L].transpose(0, 2, 1, 3, 4).reshape(b, s, h, dh)
    lse = lse.transpose(0, 1, 2, 4, 3).reshape(b, dilation, lp, h)[:, :, :L]
    lse = lse.transpose(0, 2, 1, 3).reshape(b, s, h)
    return o, lse


def dilated_attention_mixer(x, w_in, w_out, rel_bias):
    b, s, _ = x.shape
    proj = jnp.einsum('bsd,de->bse', x, w_in)
    qkv = proj[..., :QKV_COLS].reshape(b, s, N_GROUPS_ATTN, 3, HEADS_PER_GROUP, HEAD_DIM)
    gate = proj[..., QKV_COLS:]
    outs, lses = [], []
    for g, (window, dilation) in enumerate(ATTN_PATTERNS):
        o, lse = dilated_group_attention(
            qkv[:, :, g, 0], qkv[:, :, g, 1], qkv[:, :, g, 2],
            rel_bias[:, g * HEADS_PER_GROUP:(g + 1) * HEADS_PER_GROUP], window, dilation)
        outs.append(o)
        lses.append(lse)
    w = jax.nn.softmax(jnp.stack(lses), axis=0)
    o = jnp.einsum('gbsh,gbshd->bshd', w, jnp.stack(outs)).reshape(b, s, D_ATTN)
    y = o.astype(x.dtype) * jax.nn.silu(gate)
    return jnp.einsum('bse,ed->bsd', y, w_out)


def ssd_chunked_scan(xs, dt, a, bm, cm):
    b, s, g, hpg, p = xs.shape
    n = bm.shape[-1]
    nc = s // CHUNK

    def chunks(t):
        return t.reshape((b, nc, CHUNK) + t.shape[2:]).swapaxes(0, 1)

    causal = jnp.tril(jnp.ones((CHUNK, CHUNK), dtype=bool))

    def step(state, inp):
        xc, dtc, bc, cc = inp
        bc = bc.astype(jnp.float32)
        cc = cc.astype(jnp.float32)
        a_cum = jnp.cumsum(dtc * a, axis=1)
        seg = a_cum[:, :, None] - a_cum[:, None, :]
        decay = jnp.exp(jnp.where(causal[None, :, :, None, None], seg, -jnp.inf))
        xdt = xc.astype(jnp.float32) * dtc[..., None]
        cb = jnp.einsum('blgn,bsgn->blsg', cc, bc)
        y_diag = jnp.einsum('blsg,blsgh,bsghp->blghp', cb, decay, xdt)
        y_off = jnp.einsum('blgn,bghpn,blgh->blghp', cc, state, jnp.exp(a_cum))
        to_end = jnp.exp(a_cum[:, -1:] - a_cum)
        new_state = (state * jnp.exp(a_cum[:, -1])[..., None, None]
                     + jnp.einsum('bsgn,bsgh,bsghp->bghpn', bc, to_end, xdt))
        return new_state, y_diag + y_off

    state0 = jnp.zeros((b, g, hpg, p, n), jnp.float32)
    _, y = lax.scan(step, state0, (chunks(xs), chunks(dt), chunks(bm), chunks(cm)))
    return y.swapaxes(0, 1).reshape(b, s, g, hpg, p)


def ssd_mixer(x, w_in, conv_w, conv_b, dt_bias, a_log, d_skip, norm_w, w_out):
    b, s, _ = x.shape
    proj = jnp.einsum('bsd,de->bse', x, w_in)
    z = proj[..., :D_INNER]
    xbc = proj[..., D_INNER:D_INNER + CONV_DIM]
    dt_raw = proj[..., D_INNER + CONV_DIM:]
    xbc = lax.conv_general_dilated(
        xbc, conv_w[:, None, :], window_strides=(1,), padding=[(CONV_WIDTH - 1, 0)],
        dimension_numbers=('NWC', 'WIO', 'NWC'), feature_group_count=CONV_DIM)
    xbc = jax.nn.silu(xbc + conv_b)
    gn = SSM_GROUPS * D_STATE
    xs = xbc[..., :D_INNER].reshape(b, s, SSM_GROUPS, HEADS_PER_SSM_GROUP, SSM_HEAD_DIM)
    bm = xbc[..., D_INNER:D_INNER + gn].reshape(b, s, SSM_GROUPS, D_STATE)
    cm = xbc[..., D_INNER + gn:].reshape(b, s, SSM_GROUPS, D_STATE)
    dt = jax.nn.softplus(dt_raw.astype(jnp.float32) + dt_bias.astype(jnp.float32))
    dt = dt.reshape(b, s, SSM_GROUPS, HEADS_PER_SSM_GROUP)
    a = -jnp.exp(a_log.astype(jnp.float32)).reshape(SSM_GROUPS, HEADS_PER_SSM_GROUP)
    y = ssd_chunked_scan(xs, dt, a, bm, cm)
    y = y + d_skip.astype(jnp.float32).reshape(SSM_GROUPS, HEADS_PER_SSM_GROUP)[:, :, None] * xs
    y = y.reshape(b, s, D_INNER) * jax.nn.silu(z.astype(jnp.float32))
    yg = y.reshape(b, s, SSM_GROUPS, D_INNER // SSM_GROUPS)
    yg = yg * lax.rsqrt(jnp.mean(yg * yg, axis=-1, keepdims=True) + RMS_EPS)
    y = yg.reshape(b, s, D_INNER) * norm_w.astype(jnp.float32)
    return jnp.einsum('bse,ed->bsd', y.astype(x.dtype), w_out)


def layer_norm(x, g, b):
    xf = x.astype(jnp.float32)
    mu = jnp.mean(xf, axis=-1, keepdims=True)
    var = jnp.mean(jnp.square(xf - mu), axis=-1, keepdims=True)
    return ((xf - mu) * lax.rsqrt(var + LN_EPS) * g.astype(jnp.float32)
            + b.astype(jnp.float32)).astype(x.dtype)


def _fwd_setup_inputs(seed: int = 0) -> dict:
    key = jax.random.key(seed)
    ks = jax.random.split(key, 15)
    f32 = jnp.float32
    nrm = jax.random.normal
    x = nrm(ks[0], (BATCH, SEQ, D_MODEL), f32)
    w_in_attn = nrm(ks[1], (N_ATTN_LAYERS, D_MODEL, IN_ATTN_COLS), f32) * D_MODEL ** -0.5
    w_out_attn = nrm(ks[2], (N_ATTN_LAYERS, D_ATTN, D_MODEL), f32) * (D_ATTN ** -0.5 * DEEPNORM_BETA)
    rel_bias = nrm(ks[3], (NUM_BUCKETS, N_BIAS_HEADS), f32) * 0.5
    w_in_ssm = nrm(ks[4], (N_SSM_LAYERS, D_MODEL, IN_SSM_COLS), f32) * D_MODEL ** -0.5
    conv_w = nrm(ks[5], (N_SSM_LAYERS, CONV_WIDTH, CONV_DIM), f32) * CONV_WIDTH ** -0.5
    conv_b = nrm(ks[6], (N_SSM_LAYERS, CONV_DIM), f32) * 0.02
    dt0 = jnp.exp(jax.random.uniform(ks[7], (N_SSM_LAYERS, SSM_HEADS), f32,
                                     minval=math.log(1e-3), maxval=math.log(1e-1)))
    dt_bias = dt0 + jnp.log(-jnp.expm1(-dt0))
    a_log = jnp.log(jax.random.uniform(ks[8], (N_SSM_LAYERS, SSM_HEADS), f32, minval=1.0, maxval=16.0))
    d_skip = 1.0 + 0.1 * nrm(ks[9], (N_SSM_LAYERS, SSM_HEADS), f32)
    ssm_norm_w = 1.0 + 0.02 * nrm(ks[10], (N_SSM_LAYERS, D_INNER), f32)
    w_out_ssm = nrm(ks[11], (N_SSM_LAYERS, D_INNER, D_MODEL), f32) * (D_INNER ** -0.5 * DEEPNORM_BETA)
    ln_g = 1.0 + 0.02 * nrm(ks[12], (DEPTH, D_MODEL), f32)
    ln_b = 0.02 * nrm(ks[13], (DEPTH, D_MODEL), f32)
    return {'x': x, 'w_in_attn': w_in_attn, 'w_out_attn': w_out_attn, 'rel_bias': rel_bias,
            'w_in_ssm': w_in_ssm, 'conv_w': conv_w, 'conv_b': conv_b, 'dt_bias': dt_bias,
            'a_log': a_log, 'd_skip': d_skip, 'ssm_norm_w': ssm_norm_w, 'w_out_ssm': w_out_ssm,
            'ln_g': ln_g, 'ln_b': ln_b}


def _fwd_reference(x, w_in_attn, w_out_attn, rel_bias, w_in_ssm, conv_w, conv_b, dt_bias,
              a_log, d_skip, ssm_norm_w, w_out_ssm, ln_g, ln_b):
    for i in range(DEPTH):
        j = i // N_MIXERS
        if i % N_MIXERS == 0:
            h = dilated_attention_mixer(x, w_in_attn[j], w_out_attn[j], rel_bias)
        else:
            h = ssd_mixer(x, w_in_ssm[j], conv_w[j], conv_b[j], dt_bias[j], a_log[j],
                          d_skip[j], ssm_norm_w[j], w_out_ssm[j])
        x = layer_norm(DEEPNORM_ALPHA * x + h, ln_g[i], ln_b[i])
    return x


import jax as _jax
import jax.numpy as _jnp

TWIN_FORMAT = 'train_step'
FWD_PARAMS = ['x', 'w_in_attn', 'w_out_attn', 'rel_bias', 'w_in_ssm', 'conv_w', 'conv_b', 'dt_bias', 'a_log', 'd_skip', 'ssm_norm_w', 'w_out_ssm', 'ln_g', 'ln_b']
TWIN_WEIGHTS = ['w_in_attn', 'w_out_attn', 'rel_bias', 'w_in_ssm', 'conv_w', 'conv_b', 'dt_bias', 'a_log', 'd_skip', 'ssm_norm_w', 'w_out_ssm', 'ln_g', 'ln_b']
TWIN_DIFF_INPUT = 'x'
TWIN_INPUTS = ['x', 'w_in_attn', 'w_out_attn', 'rel_bias', 'w_in_ssm', 'conv_w', 'conv_b', 'dt_bias', 'a_log', 'd_skip', 'ssm_norm_w', 'w_out_ssm', 'ln_g', 'ln_b', 'loss_target', 'm_w_in_attn', 'm_w_out_attn', 'm_rel_bias', 'm_w_in_ssm', 'm_conv_w', 'm_conv_b', 'm_dt_bias', 'm_a_log', 'm_d_skip', 'm_ssm_norm_w', 'm_w_out_ssm', 'm_ln_g', 'm_ln_b', 'v_w_in_attn', 'v_w_out_attn', 'v_rel_bias', 'v_w_in_ssm', 'v_conv_w', 'v_conv_b', 'v_dt_bias', 'v_a_log', 'v_d_skip', 'v_ssm_norm_w', 'v_w_out_ssm', 'v_ln_g', 'v_ln_b']
TWIN_OUTPUTS = ['loss', 'grad_x', 'grad_w_in_attn', 'grad_w_out_attn', 'grad_rel_bias', 'grad_w_in_ssm', 'grad_conv_w', 'grad_conv_b', 'grad_dt_bias', 'grad_a_log', 'grad_d_skip', 'grad_ssm_norm_w', 'grad_w_out_ssm', 'grad_ln_g', 'grad_ln_b', 'delta_w_in_attn', 'delta_w_out_attn', 'delta_rel_bias', 'delta_w_in_ssm', 'delta_conv_w', 'delta_conv_b', 'delta_dt_bias', 'delta_a_log', 'delta_d_skip', 'delta_ssm_norm_w', 'delta_w_out_ssm', 'delta_ln_g', 'delta_ln_b', 'new_m_w_in_attn', 'new_m_w_out_attn', 'new_m_rel_bias', 'new_m_w_in_ssm', 'new_m_conv_w', 'new_m_conv_b', 'new_m_dt_bias', 'new_m_a_log', 'new_m_d_skip', 'new_m_ssm_norm_w', 'new_m_w_out_ssm', 'new_m_ln_g', 'new_m_ln_b', 'new_v_w_in_attn', 'new_v_w_out_attn', 'new_v_rel_bias', 'new_v_w_in_ssm', 'new_v_conv_w', 'new_v_conv_b', 'new_v_dt_bias', 'new_v_a_log', 'new_v_d_skip', 'new_v_ssm_norm_w', 'new_v_w_out_ssm', 'new_v_ln_g', 'new_v_ln_b']
TWIN_LEAF_KINDS = {'loss': 'loss', 'grad_x': 'grad_x', 'grad_w_in_attn': 'grad_w', 'grad_w_out_attn': 'grad_w', 'grad_rel_bias': 'grad_w', 'grad_w_in_ssm': 'grad_w', 'grad_conv_w': 'grad_w', 'grad_conv_b': 'grad_w', 'grad_dt_bias': 'grad_w', 'grad_a_log': 'grad_w', 'grad_d_skip': 'grad_w', 'grad_ssm_norm_w': 'grad_w', 'grad_w_out_ssm': 'grad_w', 'grad_ln_g': 'grad_w', 'grad_ln_b': 'grad_w', 'delta_w_in_attn': 'delta_w', 'delta_w_out_attn': 'delta_w', 'delta_rel_bias': 'delta_w', 'delta_w_in_ssm': 'delta_w', 'delta_conv_w': 'delta_w', 'delta_conv_b': 'delta_w', 'delta_dt_bias': 'delta_w', 'delta_a_log': 'delta_w', 'delta_d_skip': 'delta_w', 'delta_ssm_norm_w': 'delta_w', 'delta_w_out_ssm': 'delta_w', 'delta_ln_g': 'delta_w', 'delta_ln_b': 'delta_w', 'new_m_w_in_attn': 'new_m', 'new_m_w_out_attn': 'new_m', 'new_m_rel_bias': 'new_m', 'new_m_w_in_ssm': 'new_m', 'new_m_conv_w': 'new_m', 'new_m_conv_b': 'new_m', 'new_m_dt_bias': 'new_m', 'new_m_a_log': 'new_m', 'new_m_d_skip': 'new_m', 'new_m_ssm_norm_w': 'new_m', 'new_m_w_out_ssm': 'new_m', 'new_m_ln_g': 'new_m', 'new_m_ln_b': 'new_m', 'new_v_w_in_attn': 'new_v', 'new_v_w_out_attn': 'new_v', 'new_v_rel_bias': 'new_v', 'new_v_w_in_ssm': 'new_v', 'new_v_conv_w': 'new_v', 'new_v_conv_b': 'new_v', 'new_v_dt_bias': 'new_v', 'new_v_a_log': 'new_v', 'new_v_d_skip': 'new_v', 'new_v_ssm_norm_w': 'new_v', 'new_v_w_out_ssm': 'new_v', 'new_v_ln_g': 'new_v', 'new_v_ln_b': 'new_v'}


def _forward(args):
    return _fwd_reference(*[args[k] for k in FWD_PARAMS])


def _output_shape():
    out = _jax.eval_shape(lambda: _forward(_fwd_setup_inputs(0)))
    return out.shape, out.dtype

N_MICROBATCH = 1
ADAM_LR = 0.001
ADAM_B1 = 0.9
ADAM_B2 = 0.999
ADAM_EPS = 1e-08
ADAM_WD = 0.01
ADAM_STEP = 10
PER_EXAMPLE_BATCH_AXIS = {'x': 0, 'loss_target': 0}
SHARED_INPUTS = []
_WEIGHT_DTYPES = {'w_in_attn': _jnp.float32, 'w_out_attn': _jnp.float32, 'rel_bias': _jnp.float32, 'w_in_ssm': _jnp.float32, 'conv_w': _jnp.float32, 'conv_b': _jnp.float32, 'dt_bias': _jnp.float32, 'a_log': _jnp.float32, 'd_skip': _jnp.float32, 'ssm_norm_w': _jnp.float32, 'w_out_ssm': _jnp.float32, 'ln_g': _jnp.float32, 'ln_b': _jnp.float32}
MOMENT_SCALE = {'w_in_attn': 1.694269e-03, 'w_out_attn': 4.035192e-03, 'rel_bias': 2.704425e-03, 'w_in_ssm': 1.032059e-02, 'conv_w': 9.921432e-03, 'conv_b': 1.563445e-02, 'dt_bias': 2.216004e-02, 'a_log': 5.128855e-02, 'd_skip': 6.159520e-02, 'ssm_norm_w': 1.098580e-02, 'w_out_ssm': 3.153780e-02, 'ln_g': 5.651735e+00, 'ln_b': 3.499356e-01}


def _to_microbatches(a, axis):
    t = _jnp.moveaxis(a, axis, 0)
    t = t.reshape((N_MICROBATCH, t.shape[0] // N_MICROBATCH) + t.shape[1:])
    return _jnp.moveaxis(t, 1, axis + 1)


def setup_inputs(seed: int = 0) -> dict:
    inp = _fwd_setup_inputs(seed)
    key = _jax.random.fold_in(_jax.random.key(seed), 7919)
    shape, _ = _output_shape()
    out = dict(inp)
    out["loss_target"] = _jax.random.normal(_jax.random.fold_in(key, 0), shape, _jnp.float32)
    for i, name in enumerate(TWIN_WEIGHTS):
        w = inp[name].astype(_jnp.float32)
        if MOMENT_SCALE is None:
            s = _jnp.sqrt(_jnp.mean(_jnp.square(w)) + 1e-30)
        else:
            s = MOMENT_SCALE[name]
        km, kv = _jax.random.split(_jax.random.fold_in(key, i + 1))
        out[name] = w
        out["m_" + name] = s * _jax.random.normal(km, w.shape, _jnp.float32)
        out["v_" + name] = (s * s) * _jax.random.uniform(kv, w.shape, _jnp.float32, 0.5, 1.5)
    if N_MICROBATCH > 1:
        for name, axis in PER_EXAMPLE_BATCH_AXIS.items():
            out[name] = _to_microbatches(out[name], axis)
    return {'x': out['x'], 'w_in_attn': out['w_in_attn'], 'w_out_attn': out['w_out_attn'], 'rel_bias': out['rel_bias'], 'w_in_ssm': out['w_in_ssm'], 'conv_w': out['conv_w'], 'conv_b': out['conv_b'], 'dt_bias': out['dt_bias'], 'a_log': out['a_log'], 'd_skip': out['d_skip'], 'ssm_norm_w': out['ssm_norm_w'], 'w_out_ssm': out['w_out_ssm'], 'ln_g': out['ln_g'], 'ln_b': out['ln_b'], 'loss_target': out['loss_target'], 'm_w_in_attn': out['m_w_in_attn'], 'm_w_out_attn': out['m_w_out_attn'], 'm_rel_bias': out['m_rel_bias'], 'm_w_in_ssm': out['m_w_in_ssm'], 'm_conv_w': out['m_conv_w'], 'm_conv_b': out['m_conv_b'], 'm_dt_bias': out['m_dt_bias'], 'm_a_log': out['m_a_log'], 'm_d_skip': out['m_d_skip'], 'm_ssm_norm_w': out['m_ssm_norm_w'], 'm_w_out_ssm': out['m_w_out_ssm'], 'm_ln_g': out['m_ln_g'], 'm_ln_b': out['m_ln_b'], 'v_w_in_attn': out['v_w_in_attn'], 'v_w_out_attn': out['v_w_out_attn'], 'v_rel_bias': out['v_rel_bias'], 'v_w_in_ssm': out['v_w_in_ssm'], 'v_conv_w': out['v_conv_w'], 'v_conv_b': out['v_conv_b'], 'v_dt_bias': out['v_dt_bias'], 'v_a_log': out['v_a_log'], 'v_d_skip': out['v_d_skip'], 'v_ssm_norm_w': out['v_ssm_norm_w'], 'v_w_out_ssm': out['v_w_out_ssm'], 'v_ln_g': out['v_ln_g'], 'v_ln_b': out['v_ln_b']}


def _loss(weights, diff, rest, loss_target):
    with _jax.named_scope("forward"):
        args = {**rest, TWIN_DIFF_INPUT: diff, **{k: w.astype(_WEIGHT_DTYPES[k]) for k, w in weights.items()}}
        y = _forward(args)
    with _jax.named_scope("loss_head"):
        err = _jnp.square(y.astype(_jnp.float32) - loss_target)
        return 0.5 * _jnp.sum(_jnp.mean(err, axis=-1)) if err.ndim else 0.5 * err


def _adamw(w, g, m, v):
    m = ADAM_B1 * m + (1.0 - ADAM_B1) * g
    v = ADAM_B2 * v + (1.0 - ADAM_B2) * _jnp.square(g)
    m_hat = m / (1.0 - ADAM_B1 ** ADAM_STEP)
    v_hat = v / (1.0 - ADAM_B2 ** ADAM_STEP)
    delta = -ADAM_LR * (m_hat / (_jnp.sqrt(v_hat) + ADAM_EPS) + ADAM_WD * w)
    return delta, m, v


def reference(x, w_in_attn, w_out_attn, rel_bias, w_in_ssm, conv_w, conv_b, dt_bias, a_log, d_skip, ssm_norm_w, w_out_ssm, ln_g, ln_b, loss_target, m_w_in_attn, m_w_out_attn, m_rel_bias, m_w_in_ssm, m_conv_w, m_conv_b, m_dt_bias, m_a_log, m_d_skip, m_ssm_norm_w, m_w_out_ssm, m_ln_g, m_ln_b, v_w_in_attn, v_w_out_attn, v_rel_bias, v_w_in_ssm, v_conv_w, v_conv_b, v_dt_bias, v_a_log, v_d_skip, v_ssm_norm_w, v_w_out_ssm, v_ln_g, v_ln_b):
    given = dict(x=x, w_in_attn=w_in_attn, w_out_attn=w_out_attn, rel_bias=rel_bias, w_in_ssm=w_in_ssm, conv_w=conv_w, conv_b=conv_b, dt_bias=dt_bias, a_log=a_log, d_skip=d_skip, ssm_norm_w=ssm_norm_w, w_out_ssm=w_out_ssm, ln_g=ln_g, ln_b=ln_b, loss_target=loss_target, m_w_in_attn=m_w_in_attn, m_w_out_attn=m_w_out_attn, m_rel_bias=m_rel_bias, m_w_in_ssm=m_w_in_ssm, m_conv_w=m_conv_w, m_conv_b=m_conv_b, m_dt_bias=m_dt_bias, m_a_log=m_a_log, m_d_skip=m_d_skip, m_ssm_norm_w=m_ssm_norm_w, m_w_out_ssm=m_w_out_ssm, m_ln_g=m_ln_g, m_ln_b=m_ln_b, v_w_in_attn=v_w_in_attn, v_w_out_attn=v_w_out_attn, v_rel_bias=v_rel_bias, v_w_in_ssm=v_w_in_ssm, v_conv_w=v_conv_w, v_conv_b=v_conv_b, v_dt_bias=v_dt_bias, v_a_log=v_a_log, v_d_skip=v_d_skip, v_ssm_norm_w=v_ssm_norm_w, v_w_out_ssm=v_w_out_ssm, v_ln_g=v_ln_g, v_ln_b=v_ln_b)
    weights = {n: given[n] for n in TWIN_WEIGHTS}
    shared = {n: given[n] for n in SHARED_INPUTS}
    per_example = {n: given[n] for n in ['x']}
    grad_fn = _jax.value_and_grad(_loss, argnums=(0, 1))

    def one_microbatch(ex, loss_target):
        ex = dict(ex)
        diff = ex.pop(TWIN_DIFF_INPUT)
        return grad_fn(weights, diff, {**shared, **ex}, loss_target)

    if N_MICROBATCH == 1:
        loss, (grad_w, grad_x) = one_microbatch(per_example, given["loss_target"])
    else:
        def body(carry, xs):
            loss_sum, grad_sum = carry
            l_k, (gw_k, gx_k) = one_microbatch(xs[0], xs[1])
            with _jax.named_scope("update"):
                return (loss_sum + l_k, _jax.tree.map(_jnp.add, grad_sum, gw_k)), gx_k

        init = (_jnp.zeros((), _jnp.float32), _jax.tree.map(_jnp.zeros_like, weights))
        (loss, grad_w), grad_x = _jax.lax.scan(body, init, (per_example, given["loss_target"]))
    with _jax.named_scope("update"):
        delta_w, new_m, new_v = {}, {}, {}
        for n in TWIN_WEIGHTS:
            delta_w[n], new_m[n], new_v[n] = _adamw(weights[n], grad_w[n], given["m_" + n], given["v_" + n])
    return (loss, grad_x, *[grad_w[n] for n in TWIN_WEIGHTS], *[delta_w[n] for n in TWIN_WEIGHTS],
            *[new_m[n] for n in TWIN_WEIGHTS], *[new_v[n] for n in TWIN_WEIGHTS])
```

```python
import functools
import math

import jax
import jax.numpy as jnp
from jax import lax
from jax.experimental import pallas as pl
from jax.experimental.pallas import tpu as pltpu

F32 = jnp.float32
BF16 = jnp.bfloat16
MESH_AXES = ("x", "y", "c")
NDEV = 8
NPEER = NDEV - 1
LANES = 128
SUBLANES = 8
VMEM_LIMIT = 48 * 1024 * 1024

ATTN_PATTERNS = ((128, 1), (512, 4), (2048, 16))
N_GROUPS_ATTN = 3
HEAD_DIM = 128
ATTN_BLOCK = 128
NUM_BUCKETS = 32
MAX_DISTANCE = 2048
SSM_GROUPS = 8
CONV_WIDTH = 4
CHUNK = 128
DEPTH = 2
DEEPNORM_ALPHA = (2 * DEPTH) ** 0.25
LN_EPS = 1e-5
RMS_EPS = 1e-5
NEG_INF = -1e30
ADAM_LR = 0.001
ADAM_B1 = 0.9
ADAM_B2 = 0.999
ADAM_EPS = 1e-08
ADAM_WD = 0.01
ADAM_STEP = 10
HIGHEST = lax.Precision.HIGHEST


def _params(*sem):
    return pltpu.CompilerParams(dimension_semantics=sem, vmem_limit_bytes=VMEM_LIMIT)


def _pick(n, prefs):
    for p in prefs:
        if n % p == 0:
            return p
    return n


def _dot(a, b):
    return jnp.dot(a, b, preferred_element_type=F32)


def _dot_nt(a, b):
    return lax.dot_general(a, b, (((1,), (1,)), ((), ())), preferred_element_type=F32)


def _dot_tn(a, b):
    return lax.dot_general(a, b, (((0,), (0,)), ((), ())), preferred_element_type=F32)


def _sigmoid(x):
    return 1.0 / (1.0 + jnp.exp(-x))


def _mm(a, b, *, name, out_dtype, trans_b=False, slab_out=0, n_off=0, n_out=None,
        res=None, res_scale=1.0, tn_max=1024):
    m, k = a.shape
    slab_b = b.ndim == 3
    if slab_b:
        ns = b.shape[0]
        if trans_b:
            n, kper = b.shape[1], b.shape[2]
            assert ns * kper == k
        else:
            nper = b.shape[2]
            n = ns * nper
            assert b.shape[1] == k
    else:
        n = b.shape[0] if trans_b else b.shape[1]
        assert (b.shape[1] if trans_b else b.shape[0]) == k
    n_out = n if n_out is None else n_out
    tiles = tuple(t for t in (1024, 896, 768, 640, 512, 384, 256, 128) if t <= tn_max)
    tm = _pick(m, (1024, 512, 256, 128))
    nconstraint = math.gcd(n_out, n_off) if n_off else n_out
    if slab_b and not trans_b:
        nconstraint = math.gcd(nconstraint, nper)
    if slab_out:
        nconstraint = math.gcd(nconstraint, n_out // slab_out)
    tn = _pick(nconstraint, tiles)
    kconstraint = kper if (slab_b and trans_b) else k
    tk = _pick(kconstraint, (512, 640, 384, 256, 128))
    nk = k // tk
    nb0 = n_off // tn
    grid = (m // tm, n_out // tn, nk)

    a_spec = pl.BlockSpec((tm, tk), lambda i, j, kk: (i, kk))
    if slab_b and not trans_b:
        nps = nper // tn
        b_spec = pl.BlockSpec((None, tk, tn), lambda i, j, kk: ((j + nb0) // nps, kk, (j + nb0) % nps))
    elif slab_b and trans_b:
        kps = kper // tk
        b_spec = pl.BlockSpec((None, tn, tk), lambda i, j, kk: (kk // kps, j + nb0, kk % kps))
    elif trans_b:
        b_spec = pl.BlockSpec((tn, tk), lambda i, j, kk: (j + nb0, kk))
    else:
        b_spec = pl.BlockSpec((tk, tn), lambda i, j, kk: (kk, j + nb0))
    if slab_out:
        ops = (n_out // slab_out) // tn
        o_spec = pl.BlockSpec((None, tm, tn), lambda i, j, kk: (j // ops, i, j % ops))
        o_shape = jax.ShapeDtypeStruct((slab_out, m, n_out // slab_out), out_dtype)
    else:
        o_spec = pl.BlockSpec((tm, tn), lambda i, j, kk: (i, j))
        o_shape = jax.ShapeDtypeStruct((m, n_out), out_dtype)
    in_specs = [a_spec, b_spec]
    args = [a, b]
    if res is not None:
        in_specs.append(pl.BlockSpec((tm, tn), lambda i, j, kk: (i, j)))
        args.append(res)

    def body(*refs):
        if res is not None:
            a_ref, b_ref, r_ref, o_ref, acc = refs
        else:
            a_ref, b_ref, o_ref, acc = refs
        kk = pl.program_id(2)

        @pl.when(kk == 0)
        def _():
            acc[...] = jnp.zeros_like(acc)

        av = a_ref[...].astype(BF16)
        bv = b_ref[...].astype(BF16)
        acc[...] += _dot_nt(av, bv) if trans_b else _dot(av, bv)

        @pl.when(kk == nk - 1)
        def _():
            r = acc[...]
            if res is not None:
                r = r + res_scale * r_ref[...]
            o_ref[...] = r.astype(out_dtype)

    return pl.pallas_call(
        body, name=name, grid=grid, in_specs=in_specs, out_specs=o_spec, out_shape=o_shape,
        scratch_shapes=[pltpu.VMEM((tm, tn), F32)],
        compiler_params=_params("parallel", "parallel", "arbitrary"),
    )(*args)


def _cast_bf16(w, *, name):
    r, c = w.shape
    tr = _pick(r, (512, 256, 128, 64, 32, 16, 8))

    def body(w_ref, o_ref):
        o_ref[...] = w_ref[...].astype(BF16)

    return pl.pallas_call(
        body, name=name, grid=(r // tr,),
        in_specs=[pl.BlockSpec((tr, c), lambda i: (i, 0))],
        out_specs=pl.BlockSpec((tr, c), lambda i: (i, 0)),
        out_shape=jax.ShapeDtypeStruct((r, c), BF16),
        compiler_params=_params("parallel"),
    )(w)


def _adam_math(w, g, m, v):
    m2 = ADAM_B1 * m + (1.0 - ADAM_B1) * g
    v2 = ADAM_B2 * v + (1.0 - ADAM_B2) * (g * g)
    m_hat = m2 / (1.0 - ADAM_B1 ** ADAM_STEP)
    v_hat = v2 / (1.0 - ADAM_B2 ** ADAM_STEP)
    delta = -ADAM_LR * (m_hat / (jnp.sqrt(v_hat) + ADAM_EPS) + ADAM_WD * w)
    return delta, m2, v2


def _adamw_sum(parts, w, m, v, *, name):
    r, c = w.shape
    tr = _pick(r, (128, 64, 32, 16, 8))
    tc = c if (c % LANES or c <= 2560) else _pick(c, (2048, 1024, 512, 256, 128))

    def body(p_ref, w_ref, m_ref, v_ref, g_out, d_out, m_out, v_out):
        g = p_ref[0].astype(F32)
        for s in range(1, NDEV):
            g = g + p_ref[s].astype(F32)
        d, m2, v2 = _adam_math(w_ref[...], g, m_ref[...], v_ref[...])
        g_out[...] = g
        d_out[...] = d
        m_out[...] = m2
        v_out[...] = v2

    spec = pl.BlockSpec((tr, tc), lambda i, j: (i, j))
    shp = jax.ShapeDtypeStruct((r, c), F32)
    return pl.pallas_call(
        body, name=name, grid=(r // tr, c // tc),
        in_specs=[pl.BlockSpec((NDEV, tr, tc), lambda i, j: (0, i, j)), spec, spec, spec],
        out_specs=(spec, spec, spec, spec), out_shape=(shp, shp, shp, shp),
        compiler_params=_params("parallel", "parallel"),
    )(parts, w, m, v)


def _adamw_small(g, w, m, v, *, name):
    shp = jax.ShapeDtypeStruct(w.shape, F32)

    def body(g_ref, w_ref, m_ref, v_ref, d_out, m_out, v_out):
        d, m2, v2 = _adam_math(w_ref[...], g_ref[...], m_ref[...], v_ref[...])
        d_out[...] = d
        m_out[...] = m2
        v_out[...] = v2

    return pl.pallas_call(body, name=name, out_shape=(shp, shp, shp),
                          compiler_params=pltpu.CompilerParams(vmem_limit_bytes=VMEM_LIMIT))(g, w, m, v)


def _sum_slots(parts, *, name):
    _, r, c = parts.shape

    def body(p_ref, o_ref):
        g = p_ref[0]
        for s in range(1, NDEV):
            g = g + p_ref[s]
        o_ref[...] = g

    return pl.pallas_call(body, name=name, out_shape=jax.ShapeDtypeStruct((r, c), F32),
                          compiler_params=pltpu.CompilerParams(vmem_limit_bytes=VMEM_LIMIT))(parts)


def _ln_parts(u):
    mu = jnp.mean(u, axis=-1, keepdims=True)
    xc = u - mu
    var = jnp.mean(xc * xc, axis=-1, keepdims=True)
    rstd = lax.rsqrt(var + LN_EPS)
    return xc * rstd, rstd


def _ln_fwd(xin, h, g, b, *, name):
    s, d = xin.shape
    tm = _pick(s, (128,))

    def body(x_ref, h_ref, g_ref, b_ref, o_ref, ob_ref):
        xhat, _ = _ln_parts(DEEPNORM_ALPHA * x_ref[...] + h_ref[...])
        o = xhat * g_ref[...] + b_ref[...]
        o_ref[...] = o
        ob_ref[...] = o.astype(BF16)

    row = pl.BlockSpec((tm, d), lambda i: (i, 0))
    vec = pl.BlockSpec((1, d), lambda i: (0, 0))
    return pl.pallas_call(
        body, name=name, grid=(s // tm,), in_specs=[row, row, vec, vec], out_specs=(row, row),
        out_shape=(jax.ShapeDtypeStruct((s, d), F32), jax.ShapeDtypeStruct((s, d), BF16)),
        compiler_params=_params("parallel"),
    )(xin, h, g, b)


def _ln_bwd(xin, h, g, b, cot, *, with_loss, name):
    s, d = xin.shape
    tm = _pick(s, (128,))

    def body(x_ref, h_ref, g_ref, b_ref, c_ref, du_ref, dub_ref, dg_ref, db_ref, *rest):
        i = pl.program_id(0)
        xhat, rstd = _ln_parts(DEEPNORM_ALPHA * x_ref[...] + h_ref[...])
        gv = g_ref[...]
        if with_loss:
            diff = xhat * gv + b_ref[...] - c_ref[...]
            part = 0.5 * jnp.sum(jnp.mean(diff * diff, axis=-1, keepdims=True), axis=0, keepdims=True)
            dout = diff / d
        else:
            dout = c_ref[...]

        @pl.when(i == 0)
        def _():
            dg_ref[...] = jnp.zeros_like(dg_ref)
            db_ref[...] = jnp.zeros_like(db_ref)
            if with_loss:
                rest[0][...] = jnp.zeros_like(rest[0])

        dg_ref[...] += jnp.sum(dout * xhat, axis=0, keepdims=True)
        db_ref[...] += jnp.sum(dout, axis=0, keepdims=True)
        if with_loss:
            rest[0][...] += jnp.broadcast_to(part, rest[0].shape)
        dxh = dout * gv
        du = rstd * (dxh - jnp.mean(dxh, axis=-1, keepdims=True)
                     - xhat * jnp.mean(dxh * xhat, axis=-1, keepdims=True))
        du_ref[...] = du
        dub_ref[...] = du.astype(BF16)

    row = pl.BlockSpec((tm, d), lambda i: (i, 0))
    vec = pl.BlockSpec((1, d), lambda i: (0, 0))
    out_specs = [row, row, vec, vec]
    out_shape = [jax.ShapeDtypeStruct((s, d), F32), jax.ShapeDtypeStruct((s, d), BF16),
                 jax.ShapeDtypeStruct((1, d), F32), jax.ShapeDtypeStruct((1, d), F32)]
    if with_loss:
        out_specs.append(pl.BlockSpec((SUBLANES, LANES), lambda i: (0, 0)))
        out_shape.append(jax.ShapeDtypeStruct((SUBLANES, LANES), F32))
    return pl.pallas_call(
        body, name=name, grid=(s // tm,), in_specs=[row, row, vec, vec, row],
        out_specs=tuple(out_specs), out_shape=tuple(out_shape),
        compiler_params=_params("arbitrary"),
    )(xin, h, g, b, cot)


def t5_causal_bucket(dist):
    max_exact = NUM_BUCKETS // 2
    d_f = jnp.maximum(dist, 1).astype(jnp.float32)
    large = max_exact + (jnp.log(d_f / max_exact) / math.log(MAX_DISTANCE / max_exact)
                         * (NUM_BUCKETS - max_exact)).astype(jnp.int32)
    large = jnp.minimum(large, NUM_BUCKETS - 1)
    return jnp.where(dist < max_exact, dist, large)


def _bias_tables(rel_bias, heads):
    qi = jnp.arange(ATTN_BLOCK)[:, None]
    ki = jnp.arange(2 * ATTN_BLOCK)[None, :]
    delta = ATTN_BLOCK + qi - ki
    biases, buckets = [], []
    for g, (window, dilation) in enumerate(ATTN_PATTERNS):
        span = window // dilation
        assert span == ATTN_BLOCK
        band = (delta >= 0) & (delta <= span)
        bucket = t5_causal_bucket(jnp.clip(delta, 0, None) * dilation)
        bias = rel_bias[:, g * heads:(g + 1) * heads][bucket].transpose(2, 0, 1)
        biases.append(jnp.where(band[None], bias, NEG_INF))
        buckets.append(jnp.where(band, bucket, -1))
    return jnp.stack(biases), jnp.stack(buckets).astype(jnp.int32)


def _attn_logits(q, kc, kp, bias_c, bias_p, j):
    scale = HEAD_DIM ** -0.5
    sc = _dot_nt(q, kc) * scale + bias_c
    sp = _dot_nt(q, kp) * scale + jnp.where(j > 0, bias_p, NEG_INF)
    return sc, sp


def _attn_fwd_group(qkv, bias_g, g, dilation, da):
    s = qkv.shape[0]
    heads = da // HEAD_DIM
    l = s // dilation
    nb = l // ATTN_BLOCK
    cols = qkv.shape[1]
    view = qkv.reshape(l, dilation * cols)
    cpb = cols // HEAD_DIM
    base = g * 3 * heads

    def body(q_ref, k_ref, v_ref, b_ref, o_ref, l_ref):
        bias_p = b_ref[0, :, 0:ATTN_BLOCK]
        bias_c = b_ref[0, :, ATTN_BLOCK:2 * ATTN_BLOCK]

        def step(j, carry):
            r0 = pl.multiple_of(j * ATTN_BLOCK, ATTN_BLOCK)
            rp = pl.multiple_of(jnp.maximum(j - 1, 0) * ATTN_BLOCK, ATTN_BLOCK)
            q = q_ref[pl.ds(r0, ATTN_BLOCK), :]
            sc, sp = _attn_logits(q, k_ref[pl.ds(r0, ATTN_BLOCK), :], k_ref[pl.ds(rp, ATTN_BLOCK), :],
                                  bias_c, bias_p, j)
            mx = jnp.maximum(jnp.max(sc, axis=-1, keepdims=True), jnp.max(sp, axis=-1, keepdims=True))
            pc = jnp.exp(sc - mx)
            pp = jnp.exp(sp - mx)
            den = jnp.sum(pc, axis=-1, keepdims=True) + jnp.sum(pp, axis=-1, keepdims=True)
            inv = 1.0 / den
            o = (_dot((pc * inv).astype(BF16), v_ref[pl.ds(r0, ATTN_BLOCK), :])
                 + _dot((pp * inv).astype(BF16), v_ref[pl.ds(rp, ATTN_BLOCK), :]))
            o_ref[pl.ds(r0, ATTN_BLOCK), :] = o
            l_ref[pl.ds(r0, ATTN_BLOCK), :] = jnp.broadcast_to(mx + jnp.log(den), (ATTN_BLOCK, HEAD_DIM))
            return carry

        lax.fori_loop(0, nb, step, 0)

    def col(t):
        return lambda r, h: (0, r * cpb + base + t * heads + h)

    blk = (l, HEAD_DIM)
    out = pl.BlockSpec(blk, lambda r, h: (0, r * heads + h))
    shp = jax.ShapeDtypeStruct((l, dilation * da), F32)
    o, lse = pl.pallas_call(
        body, name=f"attn_fwd_g{g}", grid=(dilation, heads),
        in_specs=[pl.BlockSpec(blk, col(0)), pl.BlockSpec(blk, col(1)), pl.BlockSpec(blk, col(2)),
                  pl.BlockSpec((1, ATTN_BLOCK, 2 * ATTN_BLOCK), lambda r, h: (h, 0, 0))],
        out_specs=(out, out), out_shape=(shp, shp),
        compiler_params=_params("parallel", "parallel"),
    )(view, view, view, bias_g)
    return o.reshape(s, da), lse.reshape(s, da)


def _attn_combine(os_, ls_, gate):
    s, da = gate.shape
    tm = _pick(s, (512, 256, 128))
    tc = _pick(da, (512, 256, 128))

    def body(o0, o1, o2, l0, l1, l2, g_ref, o_ref, l_ref, y_ref):
        a0, a1, a2 = l0[...], l1[...], l2[...]
        mx = jnp.maximum(jnp.maximum(a0, a1), a2)
        e0, e1, e2 = jnp.exp(a0 - mx), jnp.exp(a1 - mx), jnp.exp(a2 - mx)
        den = e0 + e1 + e2
        o = (e0 * o0[...] + e1 * o1[...] + e2 * o2[...]) / den
        gv = g_ref[...]
        o_ref[...] = o
        l_ref[...] = mx + jnp.log(den)
        y_ref[...] = (o * (gv * _sigmoid(gv))).astype(BF16)

    spec = pl.BlockSpec((tm, tc), lambda i, j: (i, j))
    return pl.pallas_call(
        body, name="attn_combine", grid=(s // tm, da // tc), in_specs=[spec] * 7, out_specs=(spec, spec, spec),
        out_shape=(jax.ShapeDtypeStruct((s, da), F32), jax.ShapeDtypeStruct((s, da), F32),
                   jax.ShapeDtypeStruct((s, da), BF16)),
        compiler_params=_params("parallel", "parallel"),
    )(*os_, *ls_, gate)


def _attn_bwd_prep(dy, o, gate):
    s, da = gate.shape
    tm = _pick(s, (512, 256, 128))

    def body(dy_ref, o_ref, g_ref, do_ref, dg_ref, dd_ref):
        gv = g_ref[...]
        sg = _sigmoid(gv)
        dyv = dy_ref[...]
        ov = o_ref[...]
        do = dyv * (gv * sg)
        do_ref[...] = do.astype(BF16)
        dg_ref[...] = (dyv * ov * (sg * (1.0 + gv * (1.0 - sg)))).astype(BF16)
        dd_ref[...] = jnp.broadcast_to(jnp.sum(do * ov, axis=-1, keepdims=True), (tm, HEAD_DIM))

    spec = pl.BlockSpec((tm, HEAD_DIM), lambda i, j: (i, j))
    return pl.pallas_call(
        body, name="attn_bwd_prep", grid=(s // tm, da // HEAD_DIM), in_specs=[spec] * 3,
        out_specs=(spec, spec, spec),
        out_shape=(jax.ShapeDtypeStruct((s, da), BF16), jax.ShapeDtypeStruct((s, da), BF16),
                   jax.ShapeDtypeStruct((s, da), F32)),
        compiler_params=_params("parallel", "parallel"),
    )(dy, o, gate)


def _attn_bwd_group(qkv, do, lse, dd, bias_g, g, dilation, da):
    s = qkv.shape[0]
    heads = da // HEAD_DIM
    l = s // dilation
    nb = l // ATTN_BLOCK
    cols = qkv.shape[1]
    view = qkv.reshape(l, dilation * cols)
    cpb = cols // HEAD_DIM
    base = g * 3 * heads
    scale = HEAD_DIM ** -0.5

    def body(q_ref, k_ref, v_ref, do_ref, l_ref, dd_ref, b_ref, dq_ref, dk_ref, dv_ref, ds_ref, dk_acc, dv_acc):
        r = pl.program_id(1)
        bias_p = b_ref[0, :, 0:ATTN_BLOCK]
        bias_c = b_ref[0, :, ATTN_BLOCK:2 * ATTN_BLOCK]

        @pl.when(r == 0)
        def _():
            ds_ref[...] = jnp.zeros_like(ds_ref)

        dk_acc[...] = jnp.zeros_like(dk_acc)
        dv_acc[...] = jnp.zeros_like(dv_acc)

        def step(j, carry):
            r0 = pl.multiple_of(j * ATTN_BLOCK, ATTN_BLOCK)
            rp = pl.multiple_of(jnp.maximum(j - 1, 0) * ATTN_BLOCK, ATTN_BLOCK)
            cur = pl.ds(r0, ATTN_BLOCK)
            prev = pl.ds(rp, ATTN_BLOCK)
            q = q_ref[cur, :]
            kc, kp = k_ref[cur, :], k_ref[prev, :]
            dov = do_ref[cur, :]
            sc, sp = _attn_logits(q, kc, kp, bias_c, bias_p, j)
            lrow = l_ref[cur, 0:1]
            drow = dd_ref[cur, 0:1]
            pc = jnp.exp(sc - lrow)
            pp = jnp.exp(sp - lrow)
            dsc = pc * (_dot_nt(dov, v_ref[cur, :]) - drow)
            dsp = pp * (_dot_nt(dov, v_ref[prev, :]) - drow)
            ds_ref[0, :, ATTN_BLOCK:2 * ATTN_BLOCK] += dsc
            ds_ref[0, :, 0:ATTN_BLOCK] += dsp
            dscb, dspb = dsc.astype(BF16), dsp.astype(BF16)
            dq_ref[cur, :] = ((_dot(dscb, kc) + _dot(dspb, kp)) * scale).astype(BF16)
            dk_acc[cur, :] += _dot_tn(dscb, q) * scale
            dk_acc[prev, :] += _dot_tn(dspb, q) * scale
            dv_acc[cur, :] += _dot_tn(pc.astype(BF16), dov)
            dv_acc[prev, :] += _dot_tn(pp.astype(BF16), dov)
            return carry

        lax.fori_loop(0, nb, step, 0)
        dk_ref[...] = dk_acc[...].astype(BF16)
        dv_ref[...] = dv_acc[...].astype(BF16)

    def col(t):
        return lambda h, r: (0, r * cpb + base + t * heads + h)

    blk = (l, HEAD_DIM)
    act = pl.BlockSpec(blk, lambda h, r: (0, r * heads + h))
    shp = jax.ShapeDtypeStruct((l, dilation * da), BF16)
    actv = lambda t: t.reshape(l, dilation * da)
    dq, dk, dv, ds = pl.pallas_call(
        body, name=f"attn_bwd_g{g}", grid=(heads, dilation),
        in_specs=[pl.BlockSpec(blk, col(0)), pl.BlockSpec(blk, col(1)), pl.BlockSpec(blk, col(2)), act, act, act,
                  pl.BlockSpec((1, ATTN_BLOCK, 2 * ATTN_BLOCK), lambda h, r: (h, 0, 0))],
        out_specs=(act, act, act, pl.BlockSpec((1, ATTN_BLOCK, 2 * ATTN_BLOCK), lambda h, r: (h, 0, 0))),
        out_shape=(shp, shp, shp, jax.ShapeDtypeStruct((heads, ATTN_BLOCK, 2 * ATTN_BLOCK), F32)),
        scratch_shapes=[pltpu.VMEM(blk, F32), pltpu.VMEM(blk, F32)],
        compiler_params=_params("parallel", "arbitrary"),
    )(view, view, view, actv(do), actv(lse), actv(dd), bias_g)
    return dq.reshape(s, da), dk.reshape(s, da), dv.reshape(s, da), ds


def _bias_bwd(ds, bucket):
    ng, heads = ds.shape[0], ds.shape[1]

    def body(ds_ref, bk_ref, o_ref):
        bk = bk_ref[...]
        x = ds_ref[...]
        for b in range(NUM_BUCKETS):
            o_ref[:, b:b + 1] = jnp.sum(jnp.where(bk == b, x, 0.0), axis=(0, 1), keepdims=True)

    tile = (None, ATTN_BLOCK, 2 * ATTN_BLOCK)
    out = pl.pallas_call(
        body, name="bias_bwd", grid=(ng, heads),
        in_specs=[pl.BlockSpec((None,) + tile, lambda g, h: (g, h, 0, 0)), pl.BlockSpec(tile, lambda g, h: (g, 0, 0))],
        out_specs=pl.BlockSpec((None, None, 1, NUM_BUCKETS), lambda g, h: (g, h, 0, 0)),
        out_shape=jax.ShapeDtypeStruct((ng, heads, 1, NUM_BUCKETS), F32),
        compiler_params=_params("parallel", "parallel"),
    )(ds, bucket)
    return out.reshape(ng, heads, NUM_BUCKETS)


def _shift_rows(x, halo, s):
    r = pltpu.roll(x, s, axis=0)
    rh = pltpu.roll(halo, s, axis=0)
    row = lax.broadcasted_iota(jnp.int32, halo.shape, 0)
    top = jnp.where(row < s, rh, r[0:SUBLANES])
    return jnp.concatenate([top, r[SUBLANES:]], axis=0)


def _conv_out(x, halo, w, b):
    acc = b + w[CONV_WIDTH - 1:CONV_WIDTH] * x
    for kk in range(CONV_WIDTH - 1):
        acc = acc + w[kk:kk + 1] * _shift_rows(x, halo, CONV_WIDTH - 1 - kk)
    return acc


def _conv_fwd(proj, conv_w, conv_b, col0):
    s = proj.shape[0]
    c = conv_w.shape[1]
    ts = _pick(s, (512, 256, 128))
    tc = _pick(math.gcd(c, col0), (512, 256, 128))
    cb0 = col0 // tc
    hb = ts // SUBLANES

    def body(x_ref, h_ref, w_ref, b_ref, o_ref):
        i = pl.program_id(0)
        halo = jnp.where(i > 0, h_ref[...], 0.0)
        u = _conv_out(x_ref[...], halo, w_ref[...], b_ref[...])
        o_ref[...] = u * _sigmoid(u)

    return pl.pallas_call(
        body, name="conv_fwd", grid=(s // ts, c // tc),
        in_specs=[pl.BlockSpec((ts, tc), lambda i, j: (i, cb0 + j)),
                  pl.BlockSpec((SUBLANES, tc), lambda i, j: (jnp.maximum(i * hb - 1, 0), cb0 + j)),
                  pl.BlockSpec((CONV_WIDTH, tc), lambda i, j: (0, j)),
                  pl.BlockSpec((1, tc), lambda i, j: (0, j))],
        out_specs=pl.BlockSpec((ts, tc), lambda i, j: (i, j)),
        out_shape=jax.ShapeDtypeStruct((s, c), F32),
        compiler_params=_params("parallel", "parallel"),
    )(proj, proj, conv_w, conv_b)


def _conv_bwd(proj, conv_w, conv_b, dact, col0):
    s = proj.shape[0]
    c = conv_w.shape[1]
    ts = _pick(s, (512, 256, 128))
    tc = _pick(math.gcd(c, col0), (512, 256, 128))
    cb0 = col0 // tc
    hb = ts // SUBLANES
    nblk = s // ts
    ext = ts + SUBLANES

    def body(x_ref, xp_ref, xn_ref, d_ref, dn_ref, w_ref, b_ref, dx_ref, dw_ref, db_ref):
        i = pl.program_id(1)
        last = i == nblk - 1
        w = w_ref[...]
        halo = jnp.where(i > 0, xp_ref[...], 0.0)
        x = x_ref[...]
        xe = jnp.concatenate([x, xn_ref[...]], axis=0)
        de = jnp.concatenate([d_ref[...], jnp.where(last, 0.0, dn_ref[...])], axis=0)
        u = _conv_out(xe, halo, w, b_ref[...])
        sg = _sigmoid(u)
        dpre = de * (sg * (1.0 + u * (1.0 - sg)))
        dx = w[CONV_WIDTH - 1:CONV_WIDTH] * dpre[0:ts]
        for kk in range(CONV_WIDTH - 1):
            sh = CONV_WIDTH - 1 - kk
            dx = dx + w[kk:kk + 1] * pltpu.roll(dpre, ext - sh, axis=0)[0:ts]
        dx_ref[...] = dx.astype(BF16)
        dcur = dpre[0:ts]

        @pl.when(i == 0)
        def _():
            dw_ref[...] = jnp.zeros_like(dw_ref)
            db_ref[...] = jnp.zeros_like(db_ref)

        db_ref[...] += jnp.sum(dcur, axis=0, keepdims=True)
        dw_ref[CONV_WIDTH - 1:CONV_WIDTH, :] += jnp.sum(dcur * x, axis=0, keepdims=True)
        for kk in range(CONV_WIDTH - 1):
            xs = _shift_rows(x, halo, CONV_WIDTH - 1 - kk)
            dw_ref[kk:kk + 1, :] += jnp.sum(dcur * xs, axis=0, keepdims=True)

    cur_p = pl.BlockSpec((ts, tc), lambda j, i: (i, cb0 + j))
    prev_p = pl.BlockSpec((SUBLANES, tc), lambda j, i: (jnp.maximum(i * hb - 1, 0), cb0 + j))
    next_p = pl.BlockSpec((SUBLANES, tc), lambda j, i: (jnp.minimum((i + 1) * hb, nblk * hb - 1), cb0 + j))
    cur_d = pl.BlockSpec((ts, tc), lambda j, i: (i, j))
    next_d = pl.BlockSpec((SUBLANES, tc), lambda j, i: (jnp.minimum((i + 1) * hb, nblk * hb - 1), j))
    return pl.pallas_call(
        body, name="conv_bwd", grid=(c // tc, nblk),
        in_specs=[cur_p, prev_p, next_p, cur_d, next_d,
                  pl.BlockSpec((CONV_WIDTH, tc), lambda j, i: (0, j)), pl.BlockSpec((1, tc), lambda j, i: (0, j))],
        out_specs=(cur_d, pl.BlockSpec((CONV_WIDTH, tc), lambda j, i: (0, j)), pl.BlockSpec((1, tc), lambda j, i: (0, j))),
        out_shape=(jax.ShapeDtypeStruct((s, c), BF16), jax.ShapeDtypeStruct((CONV_WIDTH, c), F32),
                   jax.ShapeDtypeStruct((1, c), F32)),
        compiler_params=_params("parallel", "arbitrary"),
    )(proj, proj, proj, dact, dact, conv_w, conv_b)


def _dt_fwd(proj, dt_bias, col0):
    s = proj.shape[0]
    h = dt_bias.shape[1]
    ts = _pick(s, (1024, 512, 256, 128))

    def body(x_ref, b_ref, o_ref):
        v = x_ref[...] + b_ref[...]
        o_ref[...] = jnp.maximum(v, 0.0) + jnp.log1p(jnp.exp(-jnp.abs(v)))

    return pl.pallas_call(
        body, name="dt_fwd", grid=(s // ts,),
        in_specs=[pl.BlockSpec((ts, h), lambda i: (i, col0 // h)), pl.BlockSpec((1, h), lambda i: (0, 0))],
        out_specs=pl.BlockSpec((ts, h), lambda i: (i, 0)), out_shape=jax.ShapeDtypeStruct((s, h), F32),
        compiler_params=_params("parallel"),
    )(proj, dt_bias)


def _dt_bwd(proj, dt_bias, ddt, col0):
    s = proj.shape[0]
    h = dt_bias.shape[1]
    ts = _pick(s, (1024, 512, 256, 128))

    def body(x_ref, b_ref, d_ref, o_ref, db_ref):
        i = pl.program_id(0)
        draw = d_ref[...] * _sigmoid(x_ref[...] + b_ref[...])
        o_ref[...] = draw.astype(BF16)

        @pl.when(i == 0)
        def _():
            db_ref[...] = jnp.zeros_like(db_ref)

        db_ref[...] += jnp.sum(draw, axis=0, keepdims=True)

    return pl.pallas_call(
        body, name="dt_bwd", grid=(s // ts,),
        in_specs=[pl.BlockSpec((ts, h), lambda i: (i, col0 // h)), pl.BlockSpec((1, h), lambda i: (0, 0)),
                  pl.BlockSpec((ts, h), lambda i: (i, 0))],
        out_specs=(pl.BlockSpec((ts, h), lambda i: (i, 0)), pl.BlockSpec((1, h), lambda i: (0, 0))),
        out_shape=(jax.ShapeDtypeStruct((s, h), BF16), jax.ShapeDtypeStruct((1, h), F32)),
        compiler_params=_params("arbitrary"),
    )(proj, dt_bias, ddt)


def _chunk_terms(dt, dt_t, a, a_t):
    li = lax.broadcasted_iota(jnp.int32, (CHUNK, CHUNK), 0)
    si = lax.broadcasted_iota(jnp.int32, (CHUNK, CHUNK), 1)
    lower = (li >= si).astype(F32)
    upper = (li <= si).astype(F32)
    acum = jnp.dot(lower, dt * a, preferred_element_type=F32, precision=HIGHEST)
    acum_t = jnp.dot(dt_t * a_t, upper, preferred_element_type=F32, precision=HIGHEST)
    return acum, acum_t, li, si, upper


def _ssd_fwd(xbc, proj, dt_g, dt_gt, a_g, a_gt, dskip_e, norm_w, d_inner, n_state):
    s = xbc.shape[0]
    hpg = dt_g.shape[2]
    gw = d_inner // SSM_GROUPS
    p = gw // hpg
    nc = s // CHUNK
    n = n_state
    b0 = d_inner // n
    c0 = b0 + SSM_GROUPS

    def body(xs_ref, b_ref, c_ref, dt_ref, dtt_ref, a_ref, at_ref, z_ref, dsk_ref, nw_ref,
             yn_ref, y_ref, st_ref, state):
        c = pl.program_id(1)

        @pl.when(c == 0)
        def _():
            state[...] = jnp.zeros_like(state)

        st_ref[...] = state[...]
        xs = xs_ref[...]
        bm = b_ref[...].astype(BF16)
        cm = c_ref[...].astype(BF16)
        dt = dt_ref[...]
        acum, acum_t, li, si, _ = _chunk_terms(dt, dtt_ref[...], a_ref[...], at_ref[...])
        cb = _dot_nt(cm, bm)
        causal = li >= si
        alast = acum[CHUNK - 1:CHUNK, :]
        e_a = jnp.exp(acum)
        t_e = jnp.exp(alast - acum)
        e_last = jnp.exp(alast)
        for h in range(hpg):
            hs = slice(h * p, (h + 1) * p)
            decay = jnp.exp(jnp.where(causal, acum[:, h:h + 1] - acum_t[h:h + 1, :], NEG_INF))
            xdt = xs[:, hs] * dt[:, h:h + 1]
            sh = state[hs, :]
            y = (_dot((cb * decay).astype(BF16), xdt.astype(BF16))
                 + _dot_nt(cm, sh.astype(BF16)) * e_a[:, h:h + 1])
            y_ref[:, hs] = y
            state[hs, :] = e_last[:, h:h + 1] * sh + _dot_tn((xdt * t_e[:, h:h + 1]).astype(BF16), bm)
        yt = y_ref[...] + xs * dsk_ref[...]
        z = z_ref[...]
        yz = yt * (z * _sigmoid(z))
        r = lax.rsqrt(jnp.mean(yz * yz, axis=-1, keepdims=True) + RMS_EPS)
        yn_ref[...] = (yz * r * nw_ref[...]).astype(BF16)

    wide = pl.BlockSpec((CHUNK, gw), lambda g, c: (c, g))
    return pl.pallas_call(
        body, name="ssd_fwd", grid=(SSM_GROUPS, nc),
        in_specs=[wide,
                  pl.BlockSpec((CHUNK, n), lambda g, c: (c, b0 + g)),
                  pl.BlockSpec((CHUNK, n), lambda g, c: (c, c0 + g)),
                  pl.BlockSpec((None, CHUNK, hpg), lambda g, c: (g, c, 0)),
                  pl.BlockSpec((None, hpg, CHUNK), lambda g, c: (g, 0, c)),
                  pl.BlockSpec((None, 1, hpg), lambda g, c: (g, 0, 0)),
                  pl.BlockSpec((None, hpg, 1), lambda g, c: (g, 0, 0)),
                  wide,
                  pl.BlockSpec((None, 1, gw), lambda g, c: (g, 0, 0)),
                  pl.BlockSpec((1, gw), lambda g, c: (0, g))],
        out_specs=(wide, wide, pl.BlockSpec((None, None, gw, n), lambda g, c: (g, c, 0, 0))),
        out_shape=(jax.ShapeDtypeStruct((s, d_inner), BF16), jax.ShapeDtypeStruct((s, d_inner), F32),
                   jax.ShapeDtypeStruct((SSM_GROUPS, nc, gw, n), F32)),
        scratch_shapes=[pltpu.VMEM((gw, n), F32)],
        compiler_params=_params("parallel", "arbitrary"),
    )(xbc, xbc, xbc, dt_g, dt_gt, a_g, a_gt, proj, dskip_e, norm_w)


def _ssd_epilogue_bwd(dyn, y, xbc, proj, dskip_e, norm_w, hpg):
    s, d_inner = dyn.shape
    gw = d_inner // SSM_GROUPS
    p = gw // hpg
    nc = s // CHUNK

    def body(dyn_ref, y_ref, xs_ref, z_ref, dsk_ref, nw_ref, dy_ref, dz_ref, dnw_ref, ddsk_ref):
        c = pl.program_id(1)
        xs = xs_ref[...]
        z = z_ref[...]
        yt = y_ref[...] + xs * dsk_ref[...]
        sg = _sigmoid(z)
        sz = z * sg
        yz = yt * sz
        r = lax.rsqrt(jnp.mean(yz * yz, axis=-1, keepdims=True) + RMS_EPS)
        dynv = dyn_ref[...]
        dyh = dynv * nw_ref[...]
        dyz = r * (dyh - yz * (r * r) * jnp.mean(dyh * yz, axis=-1, keepdims=True))
        dyt = dyz * sz
        dy_ref[...] = dyt
        dz_ref[...] = (dyz * yt * (sg * (1.0 + z * (1.0 - sg)))).astype(BF16)

        @pl.when(c == 0)
        def _():
            dnw_ref[...] = jnp.zeros_like(dnw_ref)
            ddsk_ref[...] = jnp.zeros_like(ddsk_ref)

        dnw_ref[...] += jnp.sum(dynv * yz * r, axis=0, keepdims=True)
        colsum = jnp.sum(dyt * xs, axis=0, keepdims=True)
        fold = (lax.broadcasted_iota(jnp.int32, (gw, hpg), 0) // p
                == lax.broadcasted_iota(jnp.int32, (gw, hpg), 1)).astype(F32)
        ddsk_ref[...] += jnp.dot(colsum, fold, preferred_element_type=F32, precision=HIGHEST)

    wide = pl.BlockSpec((CHUNK, gw), lambda g, c: (c, g))
    return pl.pallas_call(
        body, name="ssd_epilogue_bwd", grid=(SSM_GROUPS, nc),
        in_specs=[wide, wide, wide, wide, pl.BlockSpec((None, 1, gw), lambda g, c: (g, 0, 0)),
                  pl.BlockSpec((1, gw), lambda g, c: (0, g))],
        out_specs=(wide, wide, pl.BlockSpec((1, gw), lambda g, c: (0, g)),
                   pl.BlockSpec((None, 1, hpg), lambda g, c: (g, 0, 0))),
        out_shape=(jax.ShapeDtypeStruct((s, d_inner), F32), jax.ShapeDtypeStruct((s, d_inner), BF16),
                   jax.ShapeDtypeStruct((1, d_inner), F32), jax.ShapeDtypeStruct((SSM_GROUPS, 1, hpg), F32)),
        compiler_params=_params("parallel", "arbitrary"),
    )(dyn, y, xbc, proj, dskip_e, norm_w)


def _ssd_scan_bwd(xbc, dt_g, dt_gt, a_g, a_gt, states, dy, dskip_e, d_inner, n_state):
    s = xbc.shape[0]
    hpg = dt_g.shape[2]
    gw = d_inner // SSM_GROUPS
    p = gw // hpg
    nc = s // CHUNK
    n = n_state
    b0 = d_inner // n
    c0 = b0 + SSM_GROUPS

    def body(xs_ref, b_ref, c_ref, dt_ref, dtt_ref, a_ref, at_ref, st_ref, dy_ref, dsk_ref,
             dxs_ref, db_ref, dc_ref, ddt_ref, da_ref, dstate):
        c = pl.program_id(1)

        @pl.when(c == 0)
        def _():
            dstate[...] = jnp.zeros_like(dstate)
            da_ref[...] = jnp.zeros_like(da_ref)

        xs = xs_ref[...]
        bm = b_ref[...].astype(BF16)
        cm = c_ref[...].astype(BF16)
        dt = dt_ref[...]
        a = a_ref[...]
        dyv = dy_ref[...]
        dsk = dsk_ref[...]
        acum, acum_t, li, si, upper = _chunk_terms(dt, dtt_ref[...], a, at_ref[...])
        cb = _dot_nt(cm, bm)
        cb_t = _dot_nt(bm, cm)
        lower_mask = li >= si
        upper_mask = si >= li
        alast = acum[CHUNK - 1:CHUNK, :]
        e_a = jnp.exp(acum)
        t_e = jnp.exp(alast - acum)
        e_last = jnp.exp(alast)
        lane_h = lax.broadcasted_iota(jnp.int32, (1, hpg), 1)
        row_l = lax.broadcasted_iota(jnp.int32, (CHUNK, 1), 0)
        dcb = jnp.zeros((CHUNK, CHUNK), F32)
        dcb_t = jnp.zeros((CHUNK, CHUNK), F32)
        d_acum = jnp.zeros((CHUNK, hpg), F32)
        ddt_x = jnp.zeros((CHUNK, hpg), F32)
        d_c = jnp.zeros((CHUNK, n), F32)
        d_b = jnp.zeros((CHUNK, n), F32)
        for h in range(hpg):
            hs = slice(h * p, (h + 1) * p)
            onehot = (lane_h == h).astype(F32)
            acol = acum[:, h:h + 1]
            arow = acum_t[h:h + 1, :]
            decay = jnp.exp(jnp.where(lower_mask, acol - arow, NEG_INF))
            decay_t = jnp.exp(jnp.where(upper_mask, arow - acol, NEG_INF))
            xsh = xs[:, hs]
            dtc = dt[:, h:h + 1]
            xdt = xsh * dtc
            xdtb = xdt.astype(BF16)
            dyh = dyv[:, hs]
            dyb = dyh.astype(BF16)
            sh = st_ref[hs, :]
            shb = sh.astype(BF16)
            dsh = dstate[hs, :]
            dshb = dsh.astype(BF16)
            tec = t_e[:, h:h + 1]
            eac = e_a[:, h:h + 1]
            elh = e_last[:, h:h + 1]
            mm = cb * decay
            mm_t = cb_t * decay_t
            dm = _dot_nt(dyb, xdtb)
            dm_t = _dot_nt(xdtb, dyb)
            dcb = dcb + dm * decay
            dcb_t = dcb_t + dm_t * decay_t
            wv = _dot_nt(bm, dshb)
            dxdt = _dot(mm_t.astype(BF16), dyb) + tec * wv
            dte = jnp.sum(xdt * wv, axis=-1, keepdims=True)
            yo = _dot_nt(cm, shb)
            dah = (jnp.sum(dm * mm, axis=-1, keepdims=True) - jnp.sum(dm_t * mm_t, axis=-1, keepdims=True)
                   - tec * dte + jnp.sum(dyh * yo, axis=-1, keepdims=True) * eac)
            dalast = (jnp.sum(tec * dte, axis=0, keepdims=True)
                      + elh * jnp.sum(dsh * sh, axis=(0, 1), keepdims=True))
            dah = dah + jnp.where(row_l == CHUNK - 1, dalast, 0.0)
            d_acum = d_acum + dah * onehot
            dye = (dyh * eac).astype(BF16)
            d_c = d_c + _dot(dye, shb)
            d_b = d_b + _dot((xdt * tec).astype(BF16), dshb)
            ddt_x = ddt_x + jnp.sum(dxdt * xsh, axis=-1, keepdims=True) * onehot
            dxs_ref[:, hs] = dxdt * dtc + dyh * dsk[:, hs]
            dstate[hs, :] = elh * dsh + _dot_tn(dye, cm)
        dc_ref[...] = d_c + _dot(dcb.astype(BF16), bm)
        db_ref[...] = d_b + _dot(dcb_t.astype(BF16), cm)
        d_da = jnp.dot(upper, d_acum, preferred_element_type=F32, precision=HIGHEST)
        ddt_ref[...] = d_da * a + ddt_x
        da_ref[...] += jnp.sum(d_da * dt, axis=0, keepdims=True)

    rev = lambda c: nc - 1 - c
    wide = pl.BlockSpec((CHUNK, gw), lambda g, c: (rev(c), g))
    return pl.pallas_call(
        body, name="ssd_scan_bwd", grid=(SSM_GROUPS, nc),
        in_specs=[wide,
                  pl.BlockSpec((CHUNK, n), lambda g, c: (rev(c), b0 + g)),
                  pl.BlockSpec((CHUNK, n), lambda g, c: (rev(c), c0 + g)),
                  pl.BlockSpec((None, CHUNK, hpg), lambda g, c: (g, rev(c), 0)),
                  pl.BlockSpec((None, hpg, CHUNK), lambda g, c: (g, 0, rev(c))),
                  pl.BlockSpec((None, 1, hpg), lambda g, c: (g, 0, 0)),
                  pl.BlockSpec((None, hpg, 1), lambda g, c: (g, 0, 0)),
                  pl.BlockSpec((None, None, gw, n), lambda g, c: (g, rev(c), 0, 0)),
                  wide,
                  pl.BlockSpec((None, 1, gw), lambda g, c: (g, 0, 0))],
        out_specs=(wide,
                   pl.BlockSpec((CHUNK, n), lambda g, c: (rev(c), g)),
                   pl.BlockSpec((CHUNK, n), lambda g, c: (rev(c), g)),
                   pl.BlockSpec((None, CHUNK, hpg), lambda g, c: (g, rev(c), 0)),
                   pl.BlockSpec((None, 1, hpg), lambda g, c: (g, 0, 0))),
        out_shape=(jax.ShapeDtypeStruct((s, d_inner), F32),
                   jax.ShapeDtypeStruct((s, SSM_GROUPS * n), F32), jax.ShapeDtypeStruct((s, SSM_GROUPS * n), F32),
                   jax.ShapeDtypeStruct((SSM_GROUPS, s, hpg), F32), jax.ShapeDtypeStruct((SSM_GROUPS, 1, hpg), F32)),
        scratch_shapes=[pltpu.VMEM((gw, n), F32)],
        compiler_params=_params("parallel", "arbitrary"),
    )(xbc, xbc, xbc, dt_g, dt_gt, a_g, a_gt, states, dy, dskip_e)


def _lin(p):
    return 4 * p[0] + 2 * p[1] + p[2]


def _all_gather(arrs, *, name, in_vmem):
    na = len(arrs)
    space = pltpu.VMEM if in_vmem else pl.ANY

    def body(*refs):
        ins, outs = refs[:na], refs[na:2 * na]
        send_sems, recv_sems, local_sems = refs[2 * na:]
        x, y, c = lax.axis_index("x"), lax.axis_index("y"), lax.axis_index("c")
        me, sibling = (x, y, c), (x, y, 1 - c)
        chips = [(1 - x, y), (x, 1 - y), (1 - x, 1 - y)]

        def copy(a, k, block, to, src=None):
            rows = outs[a].at[_lin(block)]
            return pltpu.make_async_remote_copy(
                src_ref=rows if src is None else src, dst_ref=rows,
                send_sem=send_sems.at[a * NPEER + k], recv_sem=recv_sems.at[a * NPEER + k],
                device_id=to, device_id_type=pl.DeviceIdType.MESH)

        mine, first, passed = [], [], []
        for a in range(na):
            cp = pltpu.make_async_copy(ins[a], outs[a].at[_lin(me)], local_sems.at[a])
            cp.start()
            mine.append(cp)
            cps = [copy(a, 0, me, sibling, src=ins[a])]
            cps += [copy(a, 1 + j, me, (*chip, c), src=ins[a]) for j, chip in enumerate(chips)]
            for cp in cps:
                cp.start()
            first += cps
        for j, chip in enumerate(chips):
            for a in range(na):
                copy(a, 1 + j, (*chip, c), me).wait_recv()
                cp = copy(a, 4 + j, (*chip, c), sibling)
                cp.start()
                passed.append(cp)
        for a in range(na):
            copy(a, 0, sibling, me).wait_recv()
            for j, chip in enumerate(chips):
                copy(a, 4 + j, (*chip, 1 - c), me).wait_recv()
        for cp in first + passed:
            cp.wait_send()
        for cp in mine:
            cp.wait()

    spec = pl.BlockSpec(memory_space=space)
    return pl.pallas_call(
        body, name=name,
        out_shape=tuple(jax.ShapeDtypeStruct((NDEV,) + a.shape, a.dtype) for a in arrs),
        in_specs=[spec] * na, out_specs=tuple([spec] * na),
        scratch_shapes=[pltpu.SemaphoreType.DMA((na * NPEER,)), pltpu.SemaphoreType.DMA((na * NPEER,)),
                        pltpu.SemaphoreType.DMA((na,))],
        compiler_params=pltpu.CompilerParams(vmem_limit_bytes=VMEM_LIMIT),
    )(*arrs)


def _all_to_all(arrs, *, name):
    na = len(arrs)

    def body(*refs):
        ins, outs = refs[:na], refs[na:2 * na]
        send_sems, recv_sems, local_sems = refs[2 * na:]
        x, y, c = lax.axis_index("x"), lax.axis_index("y"), lax.axis_index("c")
        me = (x, y, c)

        def peer(k):
            return (1 - x if k & 4 else x, 1 - y if k & 2 else y, 1 - c if k & 1 else c)

        def copy(a, k):
            return pltpu.make_async_remote_copy(
                src_ref=ins[a].at[_lin(peer(k))], dst_ref=outs[a].at[_lin(me)],
                send_sem=send_sems.at[a * NPEER + k - 1], recv_sem=recv_sems.at[a * NPEER + k - 1],
                device_id=peer(k), device_id_type=pl.DeviceIdType.MESH)

        def landed(a, k):
            return pltpu.make_async_remote_copy(
                src_ref=ins[a].at[_lin(peer(k))], dst_ref=outs[a].at[_lin(peer(k))],
                send_sem=send_sems.at[a * NPEER + k - 1], recv_sem=recv_sems.at[a * NPEER + k - 1],
                device_id=peer(k), device_id_type=pl.DeviceIdType.MESH)

        mine = []
        for a in range(na):
            cp = pltpu.make_async_copy(ins[a].at[_lin(me)], outs[a].at[_lin(me)], local_sems.at[a])
            cp.start()
            mine.append(cp)
        for a in range(na):
            for k in range(1, NDEV):
                copy(a, k).start()
        for a in range(na):
            for k in range(1, NDEV):
                landed(a, k).wait_recv()
        for a in range(na):
            for k in range(1, NDEV):
                copy(a, k).wait_send()
        for cp in mine:
            cp.wait()

    spec = pl.BlockSpec(memory_space=pl.ANY)
    return pl.pallas_call(
        body, name=name,
        out_shape=tuple(jax.ShapeDtypeStruct(a.shape, a.dtype) for a in arrs),
        in_specs=[spec] * na, out_specs=tuple([spec] * na),
        scratch_shapes=[pltpu.SemaphoreType.DMA((na * NPEER,)), pltpu.SemaphoreType.DMA((na * NPEER,)),
                        pltpu.SemaphoreType.DMA((na,))],
    )(*arrs)


def _pack(parts):
    flat = jnp.concatenate([p.reshape(-1).astype(F32) for p in parts])
    tile = SUBLANES * LANES
    pad = (-flat.shape[0]) % tile
    return jnp.pad(flat, (0, pad)).reshape(-1, LANES)


def _unpack(buf, shapes):
    flat = buf.reshape(-1)
    out, off = [], 0
    for shp in shapes:
        size = math.prod(shp)
        out.append(flat[off:off + size].reshape(shp))
        off += size
    return out


def _local_step(x, target, wa, wo, ws, wos, rel_bias, conv_w, conv_b, dt_bias, a_log, d_skip, norm_w, ln_g, ln_b):
    s, d = x.shape
    da = wo.shape[1]
    heads = da // HEAD_DIM
    qkv_cols = 3 * N_GROUPS_ATTN * da
    d_inner = wos.shape[0]
    conv_dim = conv_w.shape[1]
    ssm_heads = dt_bias.shape[1]
    hpg = ssm_heads // SSM_GROUPS
    gn = (conv_dim - d_inner) // 2
    n_state = gn // SSM_GROUPS
    gw = d_inner // SSM_GROUPS
    p = gw // hpg

    qkv = _mm(x, wa, name="mm_qkv", out_dtype=BF16, n_out=qkv_cols)
    gate = _mm(x, wa, name="mm_gate", out_dtype=F32, n_off=qkv_cols, n_out=da)
    bias, bucket = _bias_tables(rel_bias, heads)
    os_, ls_ = [], []
    for g, (_, dil) in enumerate(ATTN_PATTERNS):
        o, l = _attn_fwd_group(qkv, bias[g], g, dil, da)
        os_.append(o)
        ls_.append(l)
    o, lse, y = _attn_combine(os_, ls_, gate)
    h1 = _mm(y, wo, name="mm_out_attn", out_dtype=F32)
    x1, x1b = _ln_fwd(x, h1, ln_g[0:1], ln_b[0:1], name="ln1_fwd")

    proj = _mm(x1b, ws, name="mm_in_ssm", out_dtype=F32, tn_max=640)
    xbc = _conv_fwd(proj, conv_w, conv_b, d_inner)
    dt = _dt_fwd(proj, dt_bias, d_inner + conv_dim)
    dt_g = dt.reshape(s, SSM_GROUPS, hpg).transpose(1, 0, 2)
    dt_gt = dt.reshape(s, SSM_GROUPS, hpg).transpose(1, 2, 0)
    a = -jnp.exp(a_log)
    a_g = a.reshape(SSM_GROUPS, 1, hpg)
    a_gt = a.reshape(SSM_GROUPS, hpg, 1)
    dskip_e = jnp.repeat(d_skip.reshape(SSM_GROUPS, 1, hpg), p, axis=2)
    yn, yscan, states = _ssd_fwd(xbc, proj, dt_g, dt_gt, a_g, a_gt, dskip_e, norm_w, d_inner, n_state)
    h2 = _mm(yn, wos, name="mm_out_ssm", out_dtype=F32)

    du2, du2b, dg1, db1, loss_t = _ln_bwd(x1, h2, ln_g[1:2], ln_b[1:2], target, with_loss=True, name="ln2_loss_bwd")
    loss = loss_t[0, 0]
    dyn = _mm(du2b, wos, name="mm_dyn", out_dtype=F32, trans_b=True)
    g_wos = _mm(yn.T, du2b, name="mm_dw_out_ssm", out_dtype=BF16)
    dyscan, dz, g_norm, g_dskip = _ssd_epilogue_bwd(dyn, yscan, xbc, proj, dskip_e, norm_w, hpg)
    dxs, d_bm, d_cm, ddt_g, g_a = _ssd_scan_bwd(xbc, dt_g, dt_gt, a_g, a_gt, states, dyscan, dskip_e,
                                                 d_inner, n_state)
    g_alog = g_a.reshape(1, ssm_heads) * a
    dxbc = jnp.concatenate([dxs, d_bm, d_cm], axis=1)
    dpre, g_conv_w, g_conv_b = _conv_bwd(proj, conv_w, conv_b, dxbc, d_inner)
    ddt = ddt_g.transpose(1, 0, 2).reshape(s, ssm_heads)
    ddt_raw, g_dtb = _dt_bwd(proj, dt_bias, ddt, d_inner + conv_dim)
    dproj_ssm = jnp.concatenate([dz, dpre, ddt_raw], axis=1)
    dx1 = _mm(dproj_ssm, ws, name="mm_dx1", out_dtype=F32, trans_b=True, res=du2, res_scale=DEEPNORM_ALPHA)
    g_ws = _mm(x1b.T, dproj_ssm, name="mm_dw_in_ssm", out_dtype=BF16, tn_max=640)

    du1, du1b, dg0, db0 = _ln_bwd(x, h1, ln_g[0:1], ln_b[0:1], dx1, with_loss=False, name="ln1_bwd")
    dy = _mm(du1b, wo, name="mm_dy", out_dtype=F32, trans_b=True)
    g_wo = _mm(y.T, du1b, name="mm_dw_out_attn", out_dtype=BF16, slab_out=NDEV)
    do, dgate, dd = _attn_bwd_prep(dy, o, gate)
    dparts, dss = [], []
    for g, (_, dil) in enumerate(ATTN_PATTERNS):
        dq, dk, dv, ds = _attn_bwd_group(qkv, do, lse, dd, bias[g], g, dil, da)
        dparts += [dq, dk, dv]
        dss.append(ds)
    g_bias = _bias_bwd(jnp.stack(dss), bucket)
    g_rel_bias = g_bias.transpose(2, 0, 1).reshape(NUM_BUCKETS, N_GROUPS_ATTN * heads)
    dproj_attn = jnp.concatenate(dparts + [dgate], axis=1)
    xb = x.astype(BF16)
    dx = _mm(dproj_attn, wa, name="mm_dx", out_dtype=F32, trans_b=True, res=du1, res_scale=DEEPNORM_ALPHA)
    g_wa = _mm(xb.T, dproj_attn, name="mm_dw_in_attn", out_dtype=BF16, slab_out=NDEV)

    g_ln_g = jnp.concatenate([dg0, dg1], axis=0)
    g_ln_b = jnp.concatenate([db0, db1], axis=0)
    small = dict(rel_bias=g_rel_bias, dt_bias=g_dtb, a_log=g_alog, d_skip=g_dskip.reshape(1, ssm_heads),
                 ln_g=g_ln_g, ln_b=g_ln_b, conv_w=g_conv_w, conv_b=g_conv_b, ssm_norm_w=g_norm)
    return loss, dx, g_wa, g_wo, g_ws, g_wos, small


REPLICATED = ("rel_bias", "dt_bias", "a_log", "d_skip", "ln_g", "ln_b")
SHARDED_SMALL = ("conv_w", "conv_b", "ssm_norm_w")


def kernel(x, w_in_attn, w_out_attn, rel_bias, w_in_ssm, conv_w, conv_b, dt_bias, a_log, d_skip, ssm_norm_w, w_out_ssm, ln_g, ln_b, loss_target, m_w_in_attn, m_w_out_attn, m_rel_bias, m_w_in_ssm, m_conv_w, m_conv_b, m_dt_bias, m_a_log, m_d_skip, m_ssm_norm_w, m_w_out_ssm, m_ln_g, m_ln_b, v_w_in_attn, v_w_out_attn, v_rel_bias, v_w_in_ssm, v_conv_w, v_conv_b, v_dt_bias, v_a_log, v_d_skip, v_ssm_norm_w, v_w_out_ssm, v_ln_g, v_ln_b):
    w = dict(w_in_attn=w_in_attn, w_out_attn=w_out_attn, rel_bias=rel_bias, w_in_ssm=w_in_ssm, conv_w=conv_w,
             conv_b=conv_b, dt_bias=dt_bias, a_log=a_log, d_skip=d_skip, ssm_norm_w=ssm_norm_w,
             w_out_ssm=w_out_ssm, ln_g=ln_g, ln_b=ln_b)
    m = dict(w_in_attn=m_w_in_attn, w_out_attn=m_w_out_attn, rel_bias=m_rel_bias, w_in_ssm=m_w_in_ssm,
             conv_w=m_conv_w, conv_b=m_conv_b, dt_bias=m_dt_bias, a_log=m_a_log, d_skip=m_d_skip,
             ssm_norm_w=m_ssm_norm_w, w_out_ssm=m_w_out_ssm, ln_g=m_ln_g, ln_b=m_ln_b)
    v = dict(w_in_attn=v_w_in_attn, w_out_attn=v_w_out_attn, rel_bias=v_rel_bias, w_in_ssm=v_w_in_ssm,
             conv_w=v_conv_w, conv_b=v_conv_b, dt_bias=v_dt_bias, a_log=v_a_log, d_skip=v_d_skip,
             ssm_norm_w=v_ssm_norm_w, w_out_ssm=v_w_out_ssm, ln_g=v_ln_g, ln_b=v_ln_b)
    me = _lin((lax.axis_index("x"), lax.axis_index("y"), lax.axis_index("c")))
    d = x.shape[2]
    big = ("w_in_attn", "w_out_attn", "w_in_ssm", "w_out_ssm")

    shards = [_cast_bf16(w[k][0], name=f"cast_{k}") for k in big]
    wa, wo, ws_slabs, wos_slabs = _all_gather(shards, name="gather_weights", in_vmem=False)
    in_ssm = NDEV * ws_slabs.shape[2]
    ws = ws_slabs.transpose(1, 0, 2).reshape(d, in_ssm)
    wos = wos_slabs.reshape(NDEV * wos_slabs.shape[1], d)
    cpd = conv_w.shape[2]
    npd = ssm_norm_w.shape[1]
    small_shapes = [(CONV_WIDTH, cpd), (1, cpd), (1, npd)]
    (small_all,) = _all_gather([_pack([conv_w[0], conv_b, ssm_norm_w])], name="gather_small_weights", in_vmem=True)
    parts = [_unpack(small_all[i], small_shapes) for i in range(NDEV)]
    conv_w_full = jnp.concatenate([p[0] for p in parts], axis=1)
    conv_b_full = jnp.concatenate([p[1] for p in parts], axis=1)
    norm_w_full = jnp.concatenate([p[2] for p in parts], axis=1)

    loss, dx, g_wa, g_wo, g_ws, g_wos, small = _local_step(
        x[0], loss_target[0], wa, wo, ws, wos, rel_bias, conv_w_full, conv_b_full, dt_bias[0:1], a_log[0:1],
        d_skip[0:1], norm_w_full, ln_g, ln_b)
    loss = lax.psum(loss, MESH_AXES)

    g_ws_slabs = g_ws.reshape(d, NDEV, in_ssm // NDEV).transpose(1, 0, 2)
    g_wos_slabs = g_wos.reshape(NDEV, wos.shape[0] // NDEV, d)
    got = _all_to_all([g_wa, g_wo, g_ws_slabs, g_wos_slabs], name="exchange_grads")
    out = {}
    for k, parts_k in zip(big, got):
        out[k] = _adamw_sum(parts_k, w[k][0], m[k][0], v[k][0], name=f"adamw_{k}")

    order = REPLICATED + SHARDED_SMALL
    g_shapes = [small[k].shape for k in order]
    (g_all,) = _all_gather([_pack([small[k] for k in order])], name="gather_small_grads", in_vmem=True)
    g_sum = dict(zip(order, _unpack(_sum_slots(g_all, name="sum_small_grads"), g_shapes)))
    g_mine = {k: g_sum[k] for k in REPLICATED}
    g_mine["conv_w"] = lax.dynamic_slice_in_dim(g_sum["conv_w"], me * cpd, cpd, axis=1)
    g_mine["conv_b"] = lax.dynamic_slice_in_dim(g_sum["conv_b"], me * cpd, cpd, axis=1)
    g_mine["ssm_norm_w"] = lax.dynamic_slice_in_dim(g_sum["ssm_norm_w"], me * npd, npd, axis=1)
    w_shapes = [w[k].shape for k in order]
    g_pack = _pack([g_mine[k] for k in order])
    d_p, m_p, v_p = _adamw_small(g_pack, _pack([w[k] for k in order]), _pack([m[k] for k in order]),
                                 _pack([v[k] for k in order]), name="adamw_small")
    for k, gk, dk, mk, vk in zip(order, _unpack(g_pack, w_shapes), _unpack(d_p, w_shapes), _unpack(m_p, w_shapes),
                                 _unpack(v_p, w_shapes)):
        out[k] = (gk, dk, mk, vk)
    for k in big:
        out[k] = tuple(t[None] for t in out[k])

    names = ("w_in_attn", "w_out_attn", "rel_bias", "w_in_ssm", "conv_w", "conv_b", "dt_bias", "a_log", "d_skip",
             "ssm_norm_w", "w_out_ssm", "ln_g", "ln_b")
    res = [loss, dx[None]]
    for i in range(4):
        res += [out[k][i] for k in names]
    return tuple(res)
```

```python
import functools
import math

import jax
import jax.numpy as jnp
from jax import lax
from jax.experimental import pallas as pl
from jax.experimental.pallas import tpu as pltpu

F32 = jnp.float32
BF16 = jnp.bfloat16
MESH_AXES = ("x", "y", "c")
NDEV = 8
NPEER = NDEV - 1
LANES = 128
SUBLANES = 8
VMEM_LIMIT = 52 * 1024 * 1024
MM_VMEM_BUDGET = 40 * 1024 * 1024
MM_TK_MAX = 4096
MM_TN_MAX = 1024

ATTN_PATTERNS = ((128, 1), (512, 4), (2048, 16))
N_GROUPS_ATTN = 3
HEAD_DIM = 128
ATTN_BLOCK = 128
NUM_BUCKETS = 32
MAX_DISTANCE = 2048
SSM_GROUPS = 8
CONV_WIDTH = 4
CHUNK = 128
DEPTH = 2
DEEPNORM_ALPHA = (2 * DEPTH) ** 0.25
LN_EPS = 1e-5
RMS_EPS = 1e-5
NEG_INF = -1e30
ADAM_LR = 0.001
ADAM_B1 = 0.9
ADAM_B2 = 0.999
ADAM_EPS = 1e-08
ADAM_WD = 0.01
ADAM_STEP = 10
HIGHEST = lax.Precision.HIGHEST


def _params(*sem):
    return pltpu.CompilerParams(dimension_semantics=sem, vmem_limit_bytes=VMEM_LIMIT)


def _pick(n, prefs):
    for p in prefs:
        if n % p == 0:
            return p
    return n


def _dot(a, b):
    return jnp.dot(a, b, preferred_element_type=F32)


def _dot_nt(a, b):
    return lax.dot_general(a, b, (((1,), (1,)), ((), ())), preferred_element_type=F32)


def _dot_tn(a, b):
    return lax.dot_general(a, b, (((0,), (0,)), ((), ())), preferred_element_type=F32)


def _sigmoid(x):
    return 1.0 / (1.0 + jnp.exp(-x))


def _mm(a, b, *, name, out_dtype, trans_b=False, slab_out=0, n_off=0, n_out=None,
        res=None, res_scale=1.0, ride=None):
    m, k = a.shape
    slab_b = b.ndim == 3
    if slab_b:
        ns = b.shape[0]
        if trans_b:
            n, kper = b.shape[1], b.shape[2]
            assert ns * kper == k
        else:
            nper = b.shape[2]
            n = ns * nper
            assert b.shape[1] == k
    else:
        n = b.shape[0] if trans_b else b.shape[1]
        assert (b.shape[1] if trans_b else b.shape[0]) == k
    n_out = n if n_out is None else n_out
    tm = _pick(m, (1024, 512, 256, 128))
    nconstraint = math.gcd(n_out, n_off) if n_off else n_out
    if slab_b and not trans_b:
        nconstraint = math.gcd(nconstraint, nper)
    if slab_out:
        nconstraint = math.gcd(nconstraint, n_out // slab_out)
    kconstraint = kper if (slab_b and trans_b) else k
    tk = max(t for t in range(LANES, min(kconstraint, MM_TK_MAX) + 1, LANES) if kconstraint % t == 0)
    nk = k // tk
    out_bytes = jnp.dtype(out_dtype).itemsize

    def vmem_bytes(t):
        return (2 * 2 * tk * (tm + t) + 2 * tm * t * out_bytes + (4 * tm * t if nk > 1 else 0)
                + (2 * 4 * tm * t if res is not None else 0))

    fits = [t for t in range(LANES, min(nconstraint, MM_TN_MAX) + 1, LANES)
            if nconstraint % t == 0 and vmem_bytes(t) <= MM_VMEM_BUDGET]
    tn = max(fits)
    nb0 = n_off // tn
    grid = (m // tm, n_out // tn, nk)

    a_spec = pl.BlockSpec((tm, tk), lambda i, j, kk: (i, kk))
    if slab_b and not trans_b:
        nps = nper // tn
        b_spec = pl.BlockSpec((None, tk, tn), lambda i, j, kk: ((j + nb0) // nps, kk, (j + nb0) % nps))
    elif slab_b and trans_b:
        kps = kper // tk
        b_spec = pl.BlockSpec((None, tn, tk), lambda i, j, kk: (kk // kps, j + nb0, kk % kps))
    elif trans_b:
        b_spec = pl.BlockSpec((tn, tk), lambda i, j, kk: (j + nb0, kk))
    else:
        b_spec = pl.BlockSpec((tk, tn), lambda i, j, kk: (kk, j + nb0))
    if slab_out:
        ops = (n_out // slab_out) // tn
        o_spec = pl.BlockSpec((None, tm, tn), lambda i, j, kk: (j // ops, i, j % ops))
        o_shape = jax.ShapeDtypeStruct((slab_out, m, n_out // slab_out), out_dtype)
    else:
        o_spec = pl.BlockSpec((tm, tn), lambda i, j, kk: (i, j))
        o_shape = jax.ShapeDtypeStruct((m, n_out), out_dtype)
    in_specs = [a_spec, b_spec]
    args = [a, b]
    if res is not None:
        in_specs.append(pl.BlockSpec((tm, tn), lambda i, j, kk: (i, j)))
        args.append(res)

    def body(*refs):
        a_ref, b_ref = refs[0], refs[1]
        r_ref = refs[2] if res is not None else None
        o_ref = refs[3] if res is not None else refs[2]
        av = a_ref[...].astype(BF16)
        bv = b_ref[...].astype(BF16)
        part = _dot_nt(av, bv) if trans_b else _dot(av, bv)

        def finish(r):
            if res is not None:
                r = r + res_scale * r_ref[...]
            o_ref[...] = r.astype(out_dtype)

        if nk == 1:
            finish(part)
            return
        acc = refs[-1]
        kk = pl.program_id(2)

        @pl.when(kk == 0)
        def _():
            acc[...] = part

        @pl.when(kk > 0)
        def _():
            acc[...] += part

        @pl.when(kk == nk - 1)
        def _():
            finish(acc[...])

    outs, rode = _call(
        body, name=name, grid=grid, in_specs=in_specs, out_specs=[o_spec], out_shape=[o_shape], args=args,
        scratch=[pltpu.VMEM((tm, tn), F32)] if nk > 1 else [], ride=ride,
        sem=("parallel", "parallel", "arbitrary"))
    return (outs[0], rode) if ride is not None else outs[0]


def _cast_bf16(w, *, name):
    r, c = w.shape
    tr = _pick(r, (512, 256, 128, 64, 32, 16, 8))

    def body(w_ref, o_ref):
        o_ref[...] = w_ref[...].astype(BF16)

    return pl.pallas_call(
        body, name=name, grid=(r // tr,),
        in_specs=[pl.BlockSpec((tr, c), lambda i: (i, 0))],
        out_specs=pl.BlockSpec((tr, c), lambda i: (i, 0)),
        out_shape=jax.ShapeDtypeStruct((r, c), BF16),
        compiler_params=_params("parallel"),
    )(w)


def _adam_math(w, g, m, v):
    m2 = ADAM_B1 * m + (1.0 - ADAM_B1) * g
    v2 = ADAM_B2 * v + (1.0 - ADAM_B2) * (g * g)
    m_hat = m2 / (1.0 - ADAM_B1 ** ADAM_STEP)
    v_hat = v2 / (1.0 - ADAM_B2 ** ADAM_STEP)
    delta = -ADAM_LR * (m_hat / (jnp.sqrt(v_hat) + ADAM_EPS) + ADAM_WD * w)
    return delta, m2, v2


def _adamw_sum(parts, w, m, v, *, name):
    r, c = w.shape
    tr = _pick(r, (128, 64, 32, 16, 8))
    tc = c if (c % LANES or c <= 2560) else _pick(c, (2048, 1024, 512, 256, 128))
    npart = len(parts)

    def body(*refs):
        p_refs = refs[:npart]
        w_ref, m_ref, v_ref, g_out, d_out, m_out, v_out = refs[npart:]
        g = None
        for p_ref in p_refs:
            for s in range(p_ref.shape[0]):
                t = p_ref[s].astype(F32)
                g = t if g is None else g + t
        d, m2, v2 = _adam_math(w_ref[...], g, m_ref[...], v_ref[...])
        g_out[...] = g
        d_out[...] = d
        m_out[...] = m2
        v_out[...] = v2

    spec = pl.BlockSpec((tr, tc), lambda i, j: (i, j))
    shp = jax.ShapeDtypeStruct((r, c), F32)
    return pl.pallas_call(
        body, name=name, grid=(r // tr, c // tc),
        in_specs=[pl.BlockSpec((p.shape[0], tr, tc), lambda i, j: (0, i, j)) for p in parts] + [spec, spec, spec],
        out_specs=(spec, spec, spec, spec), out_shape=(shp, shp, shp, shp),
        compiler_params=_params("parallel", "parallel"),
    )(*parts, w, m, v)


def _adamw_small(g, w, m, v, *, name):
    shp = jax.ShapeDtypeStruct(w.shape, F32)

    def body(g_ref, w_ref, m_ref, v_ref, d_out, m_out, v_out):
        d, m2, v2 = _adam_math(w_ref[...], g_ref[...], m_ref[...], v_ref[...])
        d_out[...] = d
        m_out[...] = m2
        v_out[...] = v2

    return pl.pallas_call(body, name=name, out_shape=(shp, shp, shp),
                          compiler_params=pltpu.CompilerParams(vmem_limit_bytes=VMEM_LIMIT))(g, w, m, v)


def _sum_slots(parts, *, name):
    _, r, c = parts.shape

    def body(p_ref, o_ref):
        g = p_ref[0]
        for s in range(1, NDEV):
            g = g + p_ref[s]
        o_ref[...] = g

    return pl.pallas_call(body, name=name, out_shape=jax.ShapeDtypeStruct((r, c), F32),
                          compiler_params=pltpu.CompilerParams(vmem_limit_bytes=VMEM_LIMIT))(parts)


def _ln_parts(u):
    mu = jnp.mean(u, axis=-1, keepdims=True)
    xc = u - mu
    var = jnp.mean(xc * xc, axis=-1, keepdims=True)
    rstd = lax.rsqrt(var + LN_EPS)
    return xc * rstd, rstd


def _ln_fwd(xin, h, g, b, *, name):
    s, d = xin.shape
    tm = _pick(s, (128,))

    def body(x_ref, h_ref, g_ref, b_ref, o_ref, ob_ref):
        xhat, _ = _ln_parts(DEEPNORM_ALPHA * x_ref[...] + h_ref[...])
        o = xhat * g_ref[...] + b_ref[...]
        o_ref[...] = o
        ob_ref[...] = o.astype(BF16)

    row = pl.BlockSpec((tm, d), lambda i: (i, 0))
    vec = pl.BlockSpec((1, d), lambda i: (0, 0))
    return pl.pallas_call(
        body, name=name, grid=(s // tm,), in_specs=[row, row, vec, vec], out_specs=(row, row),
        out_shape=(jax.ShapeDtypeStruct((s, d), F32), jax.ShapeDtypeStruct((s, d), BF16)),
        compiler_params=_params("parallel"),
    )(xin, h, g, b)


def _ln_bwd(xin, h, g, b, cot, *, with_loss, name):
    s, d = xin.shape
    tm = _pick(s, (128,))

    def body(x_ref, h_ref, g_ref, b_ref, c_ref, du_ref, dub_ref, dg_ref, db_ref, *rest):
        i = pl.program_id(0)
        xhat, rstd = _ln_parts(DEEPNORM_ALPHA * x_ref[...] + h_ref[...])
        gv = g_ref[...]
        if with_loss:
            diff = xhat * gv + b_ref[...] - c_ref[...]
            part = 0.5 * jnp.sum(jnp.mean(diff * diff, axis=-1, keepdims=True), axis=0, keepdims=True)
            dout = diff / d
        else:
            dout = c_ref[...]

        @pl.when(i == 0)
        def _():
            dg_ref[...] = jnp.zeros_like(dg_ref)
            db_ref[...] = jnp.zeros_like(db_ref)
            if with_loss:
                rest[0][...] = jnp.zeros_like(rest[0])

        dg_ref[...] += jnp.sum(dout * xhat, axis=0, keepdims=True)
        db_ref[...] += jnp.sum(dout, axis=0, keepdims=True)
        if with_loss:
            rest[0][...] += jnp.broadcast_to(part, rest[0].shape)
        dxh = dout * gv
        du = rstd * (dxh - jnp.mean(dxh, axis=-1, keepdims=True)
                     - xhat * jnp.mean(dxh * xhat, axis=-1, keepdims=True))
        du_ref[...] = du
        dub_ref[...] = du.astype(BF16)

    row = pl.BlockSpec((tm, d), lambda i: (i, 0))
    vec = pl.BlockSpec((1, d), lambda i: (0, 0))
    out_specs = [row, row, vec, vec]
    out_shape = [jax.ShapeDtypeStruct((s, d), F32), jax.ShapeDtypeStruct((s, d), BF16),
                 jax.ShapeDtypeStruct((1, d), F32), jax.ShapeDtypeStruct((1, d), F32)]
    if with_loss:
        out_specs.append(pl.BlockSpec((SUBLANES, LANES), lambda i: (0, 0)))
        out_shape.append(jax.ShapeDtypeStruct((SUBLANES, LANES), F32))
    return pl.pallas_call(
        body, name=name, grid=(s // tm,), in_specs=[row, row, vec, vec, row],
        out_specs=tuple(out_specs), out_shape=tuple(out_shape),
        compiler_params=_params("arbitrary"),
    )(xin, h, g, b, cot)


def t5_causal_bucket(dist):
    max_exact = NUM_BUCKETS // 2
    d_f = jnp.maximum(dist, 1).astype(jnp.float32)
    large = max_exact + (jnp.log(d_f / max_exact) / math.log(MAX_DISTANCE / max_exact)
                         * (NUM_BUCKETS - max_exact)).astype(jnp.int32)
    large = jnp.minimum(large, NUM_BUCKETS - 1)
    return jnp.where(dist < max_exact, dist, large)


def _bias_tables(rel_bias, heads):
    qi = lax.broadcasted_iota(jnp.int32, (ATTN_BLOCK, 2 * ATTN_BLOCK), 0)
    ki = lax.broadcasted_iota(jnp.int32, (ATTN_BLOCK, 2 * ATTN_BLOCK), 1)
    delta = ATTN_BLOCK + qi - ki
    buckets = []
    for window, dilation in ATTN_PATTERNS:
        span = window // dilation
        assert span == ATTN_BLOCK
        band = (delta >= 0) & (delta <= span)
        buckets.append(jnp.where(band, t5_causal_bucket(jnp.clip(delta, 0, None) * dilation), -1))
    bucket = jnp.stack(buckets).astype(jnp.int32)

    def body(bk_ref, tbl_ref, o_ref):
        col = pl.program_id(0) * heads + pl.program_id(1)
        bk = bk_ref[...]
        acc = jnp.full(bk.shape, NEG_INF, F32)
        for b in range(NUM_BUCKETS):
            acc = jnp.where(bk == b, tbl_ref[b, col], acc)
        o_ref[...] = acc

    tile = (None, ATTN_BLOCK, 2 * ATTN_BLOCK)
    bias = pl.pallas_call(
        body, name="bias_fwd", grid=(N_GROUPS_ATTN, heads),
        in_specs=[pl.BlockSpec(tile, lambda g, h: (g, 0, 0)), pl.BlockSpec(memory_space=pltpu.SMEM)],
        out_specs=pl.BlockSpec((None,) + tile, lambda g, h: (g, h, 0, 0)),
        out_shape=jax.ShapeDtypeStruct((N_GROUPS_ATTN, heads, ATTN_BLOCK, 2 * ATTN_BLOCK), F32),
        compiler_params=_params("parallel", "parallel"),
    )(bucket, rel_bias)
    return bias, bucket


def _attn_logits(q, kc, kp, bias_c, bias_p, j):
    scale = HEAD_DIM ** -0.5
    sc = _dot_nt(q, kc) * scale + bias_c
    sp = _dot_nt(q, kp) * scale + jnp.where(j > 0, bias_p, NEG_INF)
    return sc, sp


def _attn_fwd_group(qkv, bias_g, g, dilation, da):
    s = qkv.shape[0]
    heads = da // HEAD_DIM
    l = s // dilation
    nb = l // ATTN_BLOCK
    cols = qkv.shape[1]
    view = qkv.reshape(l, dilation * cols)
    cpb = cols // HEAD_DIM
    base = g * 3 * heads

    def body(q_ref, k_ref, v_ref, b_ref, o_ref, l_ref):
        bias_p = b_ref[0, :, 0:ATTN_BLOCK]
        bias_c = b_ref[0, :, ATTN_BLOCK:2 * ATTN_BLOCK]

        def step(j, carry):
            r0 = pl.multiple_of(j * ATTN_BLOCK, ATTN_BLOCK)
            rp = pl.multiple_of(jnp.maximum(j - 1, 0) * ATTN_BLOCK, ATTN_BLOCK)
            q = q_ref[pl.ds(r0, ATTN_BLOCK), :]
            sc, sp = _attn_logits(q, k_ref[pl.ds(r0, ATTN_BLOCK), :], k_ref[pl.ds(rp, ATTN_BLOCK), :],
                                  bias_c, bias_p, j)
            mx = jnp.maximum(jnp.max(sc, axis=-1, keepdims=True), jnp.max(sp, axis=-1, keepdims=True))
            pc = jnp.exp(sc - mx)
            pp = jnp.exp(sp - mx)
            den = jnp.sum(pc, axis=-1, keepdims=True) + jnp.sum(pp, axis=-1, keepdims=True)
            inv = 1.0 / den
            o = (_dot((pc * inv).astype(BF16), v_ref[pl.ds(r0, ATTN_BLOCK), :])
                 + _dot((pp * inv).astype(BF16), v_ref[pl.ds(rp, ATTN_BLOCK), :]))
            o_ref[pl.ds(r0, ATTN_BLOCK), :] = o
            l_ref[pl.ds(r0, ATTN_BLOCK), :] = jnp.broadcast_to(mx + jnp.log(den), (ATTN_BLOCK, HEAD_DIM))
            return carry

        lax.fori_loop(0, nb, step, 0)

    def col(t):
        return lambda r, h: (0, r * cpb + base + t * heads + h)

    blk = (l, HEAD_DIM)
    out = pl.BlockSpec(blk, lambda r, h: (0, r * heads + h))
    shp = jax.ShapeDtypeStruct((l, dilation * da), F32)
    o, lse = pl.pallas_call(
        body, name=f"attn_fwd_g{g}", grid=(dilation, heads),
        in_specs=[pl.BlockSpec(blk, col(0)), pl.BlockSpec(blk, col(1)), pl.BlockSpec(blk, col(2)),
                  pl.BlockSpec((1, ATTN_BLOCK, 2 * ATTN_BLOCK), lambda r, h: (h, 0, 0))],
        out_specs=(out, out), out_shape=(shp, shp),
        compiler_params=_params("parallel", "parallel"),
    )(view, view, view, bias_g)
    return o.reshape(s, da), lse.reshape(s, da)


def _attn_combine(os_, ls_, gate):
    s, da = gate.shape
    tm = _pick(s, (512, 256, 128))
    tc = _pick(da, (512, 256, 128))

    def body(o0, o1, o2, l0, l1, l2, g_ref, o_ref, l_ref, y_ref):
        a0, a1, a2 = l0[...], l1[...], l2[...]
        mx = jnp.maximum(jnp.maximum(a0, a1), a2)
        e0, e1, e2 = jnp.exp(a0 - mx), jnp.exp(a1 - mx), jnp.exp(a2 - mx)
        den = e0 + e1 + e2
        o = (e0 * o0[...] + e1 * o1[...] + e2 * o2[...]) / den
        gv = g_ref[...]
        o_ref[...] = o
        l_ref[...] = mx + jnp.log(den)
        y_ref[...] = (o * (gv * _sigmoid(gv))).astype(BF16)

    spec = pl.BlockSpec((tm, tc), lambda i, j: (i, j))
    return pl.pallas_call(
        body, name="attn_combine", grid=(s // tm, da // tc), in_specs=[spec] * 7, out_specs=(spec, spec, spec),
        out_shape=(jax.ShapeDtypeStruct((s, da), F32), jax.ShapeDtypeStruct((s, da), F32),
                   jax.ShapeDtypeStruct((s, da), BF16)),
        compiler_params=_params("parallel", "parallel"),
    )(*os_, *ls_, gate)


def _attn_bwd_prep(dy, o, gate):
    s, da = gate.shape
    tm = _pick(s, (512, 256, 128))

    def body(dy_ref, o_ref, g_ref, do_ref, dg_ref, dd_ref):
        gv = g_ref[...]
        sg = _sigmoid(gv)
        dyv = dy_ref[...]
        ov = o_ref[...]
        do = dyv * (gv * sg)
        do_ref[...] = do.astype(BF16)
        dg_ref[...] = (dyv * ov * (sg * (1.0 + gv * (1.0 - sg)))).astype(BF16)
        dd_ref[...] = jnp.broadcast_to(jnp.sum(do * ov, axis=-1, keepdims=True), (tm, HEAD_DIM))

    spec = pl.BlockSpec((tm, HEAD_DIM), lambda i, j: (i, j))
    return pl.pallas_call(
        body, name="attn_bwd_prep", grid=(s // tm, da // HEAD_DIM), in_specs=[spec] * 3,
        out_specs=(spec, spec, spec),
        out_shape=(jax.ShapeDtypeStruct((s, da), BF16), jax.ShapeDtypeStruct((s, da), BF16),
                   jax.ShapeDtypeStruct((s, da), F32)),
        compiler_params=_params("parallel", "parallel"),
    )(dy, o, gate)


def _attn_bwd_group(qkv, do, lse, dd, bias_g, g, dilation, da):
    s = qkv.shape[0]
    heads = da // HEAD_DIM
    l = s // dilation
    nb = l // ATTN_BLOCK
    cols = qkv.shape[1]
    view = qkv.reshape(l, dilation * cols)
    cpb = cols // HEAD_DIM
    base = g * 3 * heads
    scale = HEAD_DIM ** -0.5

    def body(q_ref, k_ref, v_ref, do_ref, l_ref, dd_ref, b_ref, dq_ref, dk_ref, dv_ref, ds_ref, dk_acc, dv_acc):
        r = pl.program_id(1)
        bias_p = b_ref[0, :, 0:ATTN_BLOCK]
        bias_c = b_ref[0, :, ATTN_BLOCK:2 * ATTN_BLOCK]

        @pl.when(r == 0)
        def _():
            ds_ref[...] = jnp.zeros_like(ds_ref)

        dk_acc[...] = jnp.zeros_like(dk_acc)
        dv_acc[...] = jnp.zeros_like(dv_acc)

        def step(j, carry):
            r0 = pl.multiple_of(j * ATTN_BLOCK, ATTN_BLOCK)
            rp = pl.multiple_of(jnp.maximum(j - 1, 0) * ATTN_BLOCK, ATTN_BLOCK)
            cur = pl.ds(r0, ATTN_BLOCK)
            prev = pl.ds(rp, ATTN_BLOCK)
            q = q_ref[cur, :]
            kc, kp = k_ref[cur, :], k_ref[prev, :]
            dov = do_ref[cur, :]
            sc, sp = _attn_logits(q, kc, kp, bias_c, bias_p, j)
            lrow = l_ref[cur, 0:1]
            drow = dd_ref[cur, 0:1]
            pc = jnp.exp(sc - lrow)
            pp = jnp.exp(sp - lrow)
            dsc = pc * (_dot_nt(dov, v_ref[cur, :]) - drow)
            dsp = pp * (_dot_nt(dov, v_ref[prev, :]) - drow)
            ds_ref[0, :, ATTN_BLOCK:2 * ATTN_BLOCK] += dsc
            ds_ref[0, :, 0:ATTN_BLOCK] += dsp
            dscb, dspb = dsc.astype(BF16), dsp.astype(BF16)
            dq_ref[cur, :] = ((_dot(dscb, kc) + _dot(dspb, kp)) * scale).astype(BF16)
            dk_acc[cur, :] += _dot_tn(dscb, q) * scale
            dk_acc[prev, :] += _dot_tn(dspb, q) * scale
            dv_acc[cur, :] += _dot_tn(pc.astype(BF16), dov)
            dv_acc[prev, :] += _dot_tn(pp.astype(BF16), dov)
            return carry

        lax.fori_loop(0, nb, step, 0)
        dk_ref[...] = dk_acc[...].astype(BF16)
        dv_ref[...] = dv_acc[...].astype(BF16)

    def col(t):
        return lambda h, r: (0, r * cpb + base + t * heads + h)

    blk = (l, HEAD_DIM)
    act = pl.BlockSpec(blk, lambda h, r: (0, r * heads + h))
    shp = jax.ShapeDtypeStruct((l, dilation * da), BF16)
    actv = lambda t: t.reshape(l, dilation * da)
    dq, dk, dv, ds = pl.pallas_call(
        body, name=f"attn_bwd_g{g}", grid=(heads, dilation),
        in_specs=[pl.BlockSpec(blk, col(0)), pl.BlockSpec(blk, col(1)), pl.BlockSpec(blk, col(2)), act, act, act,
                  pl.BlockSpec((1, ATTN_BLOCK, 2 * ATTN_BLOCK), lambda h, r: (h, 0, 0))],
        out_specs=(act, act, act, pl.BlockSpec((1, ATTN_BLOCK, 2 * ATTN_BLOCK), lambda h, r: (h, 0, 0))),
        out_shape=(shp, shp, shp, jax.ShapeDtypeStruct((heads, ATTN_BLOCK, 2 * ATTN_BLOCK), F32)),
        scratch_shapes=[pltpu.VMEM(blk, F32), pltpu.VMEM(blk, F32)],
        compiler_params=_params("parallel", "arbitrary"),
    )(view, view, view, actv(do), actv(lse), actv(dd), bias_g)
    return dq.reshape(s, da), dk.reshape(s, da), dv.reshape(s, da), ds


def _bias_bwd(ds, bucket):
    ng, heads = ds.shape[0], ds.shape[1]

    def body(ds_ref, bk_ref, o_ref):
        bk = bk_ref[...]
        x = ds_ref[...]
        for b in range(NUM_BUCKETS):
            o_ref[:, b:b + 1] = jnp.sum(jnp.where(bk == b, x, 0.0), axis=(0, 1), keepdims=True)

    tile = (None, ATTN_BLOCK, 2 * ATTN_BLOCK)
    out = pl.pallas_call(
        body, name="bias_bwd", grid=(ng, heads),
        in_specs=[pl.BlockSpec((None,) + tile, lambda g, h: (g, h, 0, 0)), pl.BlockSpec(tile, lambda g, h: (g, 0, 0))],
        out_specs=pl.BlockSpec((None, None, 1, NUM_BUCKETS), lambda g, h: (g, h, 0, 0)),
        out_shape=jax.ShapeDtypeStruct((ng, heads, 1, NUM_BUCKETS), F32),
        compiler_params=_params("parallel", "parallel"),
    )(ds, bucket)
    return out.reshape(ng, heads, NUM_BUCKETS)


def _shift_rows(x, halo, s):
    r = pltpu.roll(x, s, axis=0)
    rh = pltpu.roll(halo, s, axis=0)
    row = lax.broadcasted_iota(jnp.int32, halo.shape, 0)
    top = jnp.where(row < s, rh, r[0:SUBLANES])
    return jnp.concatenate([top, r[SUBLANES:]], axis=0)


def _conv_out(x, halo, w, b):
    acc = b + w[CONV_WIDTH - 1:CONV_WIDTH] * x
    for kk in range(CONV_WIDTH - 1):
        acc = acc + w[kk:kk + 1] * _shift_rows(x, halo, CONV_WIDTH - 1 - kk)
    return acc


def _conv_fwd(proj, conv_w, conv_b, col0):
    s = proj.shape[0]
    c = conv_w.shape[1]
    ts = _pick(s, (512, 256, 128))
    tc = _pick(math.gcd(c, col0), (512, 256, 128))
    cb0 = col0 // tc
    hb = ts // SUBLANES

    def body(x_ref, h_ref, w_ref, b_ref, o_ref):
        i = pl.program_id(0)
        halo = jnp.where(i > 0, h_ref[...], 0.0)
        u = _conv_out(x_ref[...], halo, w_ref[...], b_ref[...])
        o_ref[...] = u * _sigmoid(u)

    return pl.pallas_call(
        body, name="conv_fwd", grid=(s // ts, c // tc),
        in_specs=[pl.BlockSpec((ts, tc), lambda i, j: (i, cb0 + j)),
                  pl.BlockSpec((SUBLANES, tc), lambda i, j: (jnp.maximum(i * hb - 1, 0), cb0 + j)),
                  pl.BlockSpec((CONV_WIDTH, tc), lambda i, j: (0, j)),
                  pl.BlockSpec((1, tc), lambda i, j: (0, j))],
        out_specs=pl.BlockSpec((ts, tc), lambda i, j: (i, j)),
        out_shape=jax.ShapeDtypeStruct((s, c), F32),
        compiler_params=_params("parallel", "parallel"),
    )(proj, proj, conv_w, conv_b)


def _conv_bwd(proj, conv_w, conv_b, dacts, col0, dproj):
    s = proj.shape[0]
    c = conv_w.shape[1]
    widths = [d.shape[1] for d in dacts]
    assert sum(widths) == c
    ts = _pick(s, (512, 256, 128))
    tc = _pick(math.gcd(math.gcd(c, col0), math.gcd(*widths)), (512, 256, 128))
    cb0 = col0 // tc
    hb = ts // SUBLANES
    nblk = s // ts
    ext = ts + SUBLANES
    nb = [wd // tc for wd in widths]
    starts = [0, nb[0], nb[0] + nb[1]]

    def body(x_ref, xp_ref, xn_ref, d0, d1, d2, n0, n1, n2, w_ref, b_ref, _, dx_ref, dw_ref, db_ref):
        j = pl.program_id(0)
        i = pl.program_id(1)
        last = i == nblk - 1
        w = w_ref[...]
        halo = jnp.where(i > 0, xp_ref[...], 0.0)
        x = x_ref[...]
        xe = jnp.concatenate([x, xn_ref[...]], axis=0)
        dcur = jnp.where(j < starts[1], d0[...], jnp.where(j < starts[2], d1[...], d2[...]))
        dnext = jnp.where(j < starts[1], n0[...], jnp.where(j < starts[2], n1[...], n2[...]))
        de = jnp.concatenate([dcur, jnp.where(last, 0.0, dnext)], axis=0)
        u = _conv_out(xe, halo, w, b_ref[...])
        sg = _sigmoid(u)
        dpre = de * (sg * (1.0 + u * (1.0 - sg)))
        dx = w[CONV_WIDTH - 1:CONV_WIDTH] * dpre[0:ts]
        for kk in range(CONV_WIDTH - 1):
            sh = CONV_WIDTH - 1 - kk
            dx = dx + w[kk:kk + 1] * pltpu.roll(dpre, ext - sh, axis=0)[0:ts]
        dx_ref[...] = dx.astype(BF16)
        dcur = dpre[0:ts]

        @pl.when(i == 0)
        def _():
            dw_ref[...] = jnp.zeros_like(dw_ref)
            db_ref[...] = jnp.zeros_like(db_ref)

        db_ref[...] += jnp.sum(dcur, axis=0, keepdims=True)
        dw_ref[CONV_WIDTH - 1:CONV_WIDTH, :] += jnp.sum(dcur * x, axis=0, keepdims=True)
        for kk in range(CONV_WIDTH - 1):
            xs = _shift_rows(x, halo, CONV_WIDTH - 1 - kk)
            dw_ref[kk:kk + 1, :] += jnp.sum(dcur * xs, axis=0, keepdims=True)

    cur_p = pl.BlockSpec((ts, tc), lambda j, i: (i, cb0 + j))
    prev_p = pl.BlockSpec((SUBLANES, tc), lambda j, i: (jnp.maximum(i * hb - 1, 0), cb0 + j))
    nxt = lambda i: jnp.minimum((i + 1) * hb, nblk * hb - 1)
    next_p = pl.BlockSpec((SUBLANES, tc), lambda j, i: (nxt(i), cb0 + j))

    def part(q):
        return lambda j: jnp.clip(j - starts[q], 0, nb[q] - 1)

    cur_d = [pl.BlockSpec((ts, tc), lambda j, i, f=part(q): (i, f(j))) for q in range(3)]
    next_d = [pl.BlockSpec((SUBLANES, tc), lambda j, i, f=part(q): (nxt(i), f(j))) for q in range(3)]
    vec4 = pl.BlockSpec((CONV_WIDTH, tc), lambda j, i: (0, j))
    vec1 = pl.BlockSpec((1, tc), lambda j, i: (0, j))
    return pl.pallas_call(
        body, name="conv_bwd", grid=(c // tc, nblk),
        in_specs=[cur_p, prev_p, next_p, *cur_d, *next_d, vec4, vec1, pl.BlockSpec(memory_space=pl.ANY)],
        out_specs=(cur_p, vec4, vec1),
        out_shape=(jax.ShapeDtypeStruct(dproj.shape, dproj.dtype), jax.ShapeDtypeStruct((CONV_WIDTH, c), F32),
                   jax.ShapeDtypeStruct((1, c), F32)),
        input_output_aliases={11: 0},
        compiler_params=_params("parallel", "arbitrary"),
    )(proj, proj, proj, *dacts, *dacts, conv_w, conv_b, dproj)


def _dt_fwd(proj, dt_bias, col0):
    s = proj.shape[0]
    h = dt_bias.shape[1]
    ts = _pick(s, (1024, 512, 256, 128))

    def body(x_ref, b_ref, o_ref):
        v = x_ref[...] + b_ref[...]
        o_ref[...] = jnp.maximum(v, 0.0) + jnp.log1p(jnp.exp(-jnp.abs(v)))

    return pl.pallas_call(
        body, name="dt_fwd", grid=(s // ts,),
        in_specs=[pl.BlockSpec((ts, h), lambda i: (i, col0 // h)), pl.BlockSpec((1, h), lambda i: (0, 0))],
        out_specs=pl.BlockSpec((ts, h), lambda i: (i, 0)), out_shape=jax.ShapeDtypeStruct((s, h), F32),
        compiler_params=_params("parallel"),
    )(proj, dt_bias)


def _dt_bwd(proj, dt_bias, ddt, col0, dproj):
    s = proj.shape[0]
    h = dt_bias.shape[1]
    ts = _pick(s, (1024, 512, 256, 128))

    def body(x_ref, b_ref, d_ref, _, o_ref, db_ref):
        i = pl.program_id(0)
        draw = d_ref[...] * _sigmoid(x_ref[...] + b_ref[...])
        o_ref[...] = draw.astype(BF16)

        @pl.when(i == 0)
        def _():
            db_ref[...] = jnp.zeros_like(db_ref)

        db_ref[...] += jnp.sum(draw, axis=0, keepdims=True)

    return pl.pallas_call(
        body, name="dt_bwd", grid=(s // ts,),
        in_specs=[pl.BlockSpec((ts, h), lambda i: (i, col0 // h)), pl.BlockSpec((1, h), lambda i: (0, 0)),
                  pl.BlockSpec((ts, h), lambda i: (i, 0)), pl.BlockSpec(memory_space=pl.ANY)],
        out_specs=(pl.BlockSpec((ts, h), lambda i: (i, col0 // h)), pl.BlockSpec((1, h), lambda i: (0, 0))),
        out_shape=(jax.ShapeDtypeStruct(dproj.shape, dproj.dtype), jax.ShapeDtypeStruct((1, h), F32)),
        input_output_aliases={3: 0},
        compiler_params=_params("arbitrary"),
    )(proj, dt_bias, ddt, dproj)


def _chunk_terms(dt, dt_t, a, a_t):
    li = lax.broadcasted_iota(jnp.int32, (CHUNK, CHUNK), 0)
    si = lax.broadcasted_iota(jnp.int32, (CHUNK, CHUNK), 1)
    lower = (li >= si).astype(F32)
    upper = (li <= si).astype(F32)
    acum = jnp.dot(lower, dt * a, preferred_element_type=F32, precision=HIGHEST)
    acum_t = jnp.dot(dt_t * a_t, upper, preferred_element_type=F32, precision=HIGHEST)
    return acum, acum_t, li, si, upper


def _ssd_fwd(xbc, proj, dt_g, dt_gt, a_g, a_gt, dskip_e, norm_w, d_inner, n_state):
    s = xbc.shape[0]
    hpg = dt_g.shape[2]
    gw = d_inner // SSM_GROUPS
    p = gw // hpg
    nc = s // CHUNK
    n = n_state
    b0 = d_inner // n
    c0 = b0 + SSM_GROUPS

    def body(xs_ref, b_ref, c_ref, dt_ref, dtt_ref, a_ref, at_ref, z_ref, dsk_ref, nw_ref,
             yn_ref, y_ref, st_ref, state):
        c = pl.program_id(1)

        @pl.when(c == 0)
        def _():
            state[...] = jnp.zeros_like(state)

        st_ref[...] = state[...]
        xs = xs_ref[...]
        bm = b_ref[...].astype(BF16)
        cm = c_ref[...].astype(BF16)
        dt = dt_ref[...]
        acum, acum_t, li, si, _ = _chunk_terms(dt, dtt_ref[...], a_ref[...], at_ref[...])
        cb = _dot_nt(cm, bm)
        causal = li >= si
        alast = acum[CHUNK - 1:CHUNK, :]
        e_a = jnp.exp(acum)
        t_e = jnp.exp(alast - acum)
        e_last = jnp.exp(alast)
        for h in range(hpg):
            hs = slice(h * p, (h + 1) * p)
            decay = jnp.exp(jnp.where(causal, acum[:, h:h + 1] - acum_t[h:h + 1, :], NEG_INF))
            xdt = xs[:, hs] * dt[:, h:h + 1]
            sh = state[hs, :]
            y = (_dot((cb * decay).astype(BF16), xdt.astype(BF16))
                 + _dot_nt(cm, sh.astype(BF16)) * e_a[:, h:h + 1])
            y_ref[:, hs] = y
            state[hs, :] = e_last[:, h:h + 1] * sh + _dot_tn((xdt * t_e[:, h:h + 1]).astype(BF16), bm)
        yt = y_ref[...] + xs * dsk_ref[...]
        z = z_ref[...]
        yz = yt * (z * _sigmoid(z))
        r = lax.rsqrt(jnp.mean(yz * yz, axis=-1, keepdims=True) + RMS_EPS)
        yn_ref[...] = (yz * r * nw_ref[...]).astype(BF16)

    wide = pl.BlockSpec((CHUNK, gw), lambda g, c: (c, g))
    return pl.pallas_call(
        body, name="ssd_fwd", grid=(SSM_GROUPS, nc),
        in_specs=[wide,
                  pl.BlockSpec((CHUNK, n), lambda g, c: (c, b0 + g)),
                  pl.BlockSpec((CHUNK, n), lambda g, c: (c, c0 + g)),
                  pl.BlockSpec((None, CHUNK, hpg), lambda g, c: (g, c, 0)),
                  pl.BlockSpec((None, hpg, CHUNK), lambda g, c: (g, 0, c)),
                  pl.BlockSpec((None, 1, hpg), lambda g, c: (g, 0, 0)),
                  pl.BlockSpec((None, hpg, 1), lambda g, c: (g, 0, 0)),
                  wide,
                  pl.BlockSpec((None, 1, gw), lambda g, c: (g, 0, 0)),
                  pl.BlockSpec((1, gw), lambda g, c: (0, g))],
        out_specs=(wide, wide, pl.BlockSpec((None, None, gw, n), lambda g, c: (g, c, 0, 0))),
        out_shape=(jax.ShapeDtypeStruct((s, d_inner), BF16), jax.ShapeDtypeStruct((s, d_inner), F32),
                   jax.ShapeDtypeStruct((SSM_GROUPS, nc, gw, n), F32)),
        scratch_shapes=[pltpu.VMEM((gw, n), F32)],
        compiler_params=_params("parallel", "arbitrary"),
    )(xbc, xbc, xbc, dt_g, dt_gt, a_g, a_gt, proj, dskip_e, norm_w)


def _ssd_epilogue_bwd(dyn, y, xbc, proj, dskip_e, norm_w, hpg):
    s, d_inner = dyn.shape
    gw = d_inner // SSM_GROUPS
    p = gw // hpg
    nc = s // CHUNK

    def body(dyn_ref, y_ref, xs_ref, z_ref, dsk_ref, nw_ref, dy_ref, dz_ref, dnw_ref, ddsk_ref):
        c = pl.program_id(1)
        xs = xs_ref[...]
        z = z_ref[...]
        yt = y_ref[...] + xs * dsk_ref[...]
        sg = _sigmoid(z)
        sz = z * sg
        yz = yt * sz
        r = lax.rsqrt(jnp.mean(yz * yz, axis=-1, keepdims=True) + RMS_EPS)
        dynv = dyn_ref[...]
        dyh = dynv * nw_ref[...]
        dyz = r * (dyh - yz * (r * r) * jnp.mean(dyh * yz, axis=-1, keepdims=True))
        dyt = dyz * sz
        dy_ref[...] = dyt
        dz_ref[...] = (dyz * yt * (sg * (1.0 + z * (1.0 - sg)))).astype(BF16)

        @pl.when(c == 0)
        def _():
            dnw_ref[...] = jnp.zeros_like(dnw_ref)
            ddsk_ref[...] = jnp.zeros_like(ddsk_ref)

        dnw_ref[...] += jnp.sum(dynv * yz * r, axis=0, keepdims=True)
        colsum = jnp.sum(dyt * xs, axis=0, keepdims=True)
        fold = (lax.broadcasted_iota(jnp.int32, (gw, hpg), 0) // p
                == lax.broadcasted_iota(jnp.int32, (gw, hpg), 1)).astype(F32)
        ddsk_ref[...] += jnp.dot(colsum, fold, preferred_element_type=F32, precision=HIGHEST)

    wide = pl.BlockSpec((CHUNK, gw), lambda g, c: (c, g))
    return pl.pallas_call(
        body, name="ssd_epilogue_bwd", grid=(SSM_GROUPS, nc),
        in_specs=[wide, wide, wide, wide, pl.BlockSpec((None, 1, gw), lambda g, c: (g, 0, 0)),
                  pl.BlockSpec((1, gw), lambda g, c: (0, g))],
        out_specs=(wide, wide, pl.BlockSpec((1, gw), lambda g, c: (0, g)),
                   pl.BlockSpec((None, 1, hpg), lambda g, c: (g, 0, 0))),
        out_shape=(jax.ShapeDtypeStruct((s, d_inner), F32), jax.ShapeDtypeStruct((s, proj.shape[1]), BF16),
                   jax.ShapeDtypeStruct((1, d_inner), F32), jax.ShapeDtypeStruct((SSM_GROUPS, 1, hpg), F32)),
        compiler_params=_params("parallel", "arbitrary"),
    )(dyn, y, xbc, proj, dskip_e, norm_w)


def _ssd_scan_bwd(xbc, dt_g, dt_gt, a_g, a_gt, states, dy, dskip_e, d_inner, n_state, ride=None):
    s = xbc.shape[0]
    hpg = dt_g.shape[2]
    gw = d_inner // SSM_GROUPS
    p = gw // hpg
    nc = s // CHUNK
    n = n_state
    b0 = d_inner // n
    c0 = b0 + SSM_GROUPS

    def body(xs_ref, b_ref, c_ref, dt_ref, dtt_ref, a_ref, at_ref, st_ref, dy_ref, dsk_ref,
             dxs_ref, db_ref, dc_ref, ddt_ref, da_ref, dstate):
        c = pl.program_id(1)

        @pl.when(c == 0)
        def _():
            dstate[...] = jnp.zeros_like(dstate)
            da_ref[...] = jnp.zeros_like(da_ref)

        xs = xs_ref[...]
        bm = b_ref[...].astype(BF16)
        cm = c_ref[...].astype(BF16)
        dt = dt_ref[...]
        a = a_ref[...]
        dyv = dy_ref[...]
        dsk = dsk_ref[...]
        acum, acum_t, li, si, upper = _chunk_terms(dt, dtt_ref[...], a, at_ref[...])
        cb = _dot_nt(cm, bm)
        cb_t = _dot_nt(bm, cm)
        lower_mask = li >= si
        upper_mask = si >= li
        alast = acum[CHUNK - 1:CHUNK, :]
        e_a = jnp.exp(acum)
        t_e = jnp.exp(alast - acum)
        e_last = jnp.exp(alast)
        lane_h = lax.broadcasted_iota(jnp.int32, (1, hpg), 1)
        row_l = lax.broadcasted_iota(jnp.int32, (CHUNK, 1), 0)
        dcb = jnp.zeros((CHUNK, CHUNK), F32)
        dcb_t = jnp.zeros((CHUNK, CHUNK), F32)
        d_acum = jnp.zeros((CHUNK, hpg), F32)
        ddt_x = jnp.zeros((CHUNK, hpg), F32)
        d_c = jnp.zeros((CHUNK, n), F32)
        d_b = jnp.zeros((CHUNK, n), F32)
        for h in range(hpg):
            hs = slice(h * p, (h + 1) * p)
            onehot = (lane_h == h).astype(F32)
            acol = acum[:, h:h + 1]
            arow = acum_t[h:h + 1, :]
            decay = jnp.exp(jnp.where(lower_mask, acol - arow, NEG_INF))
            decay_t = jnp.exp(jnp.where(upper_mask, arow - acol, NEG_INF))
            xsh = xs[:, hs]
            dtc = dt[:, h:h + 1]
            xdt = xsh * dtc
            xdtb = xdt.astype(BF16)
            dyh = dyv[:, hs]
            dyb = dyh.astype(BF16)
            sh = st_ref[hs, :]
            shb = sh.astype(BF16)
            dsh = dstate[hs, :]
            dshb = dsh.astype(BF16)
            tec = t_e[:, h:h + 1]
            eac = e_a[:, h:h + 1]
            elh = e_last[:, h:h + 1]
            mm = cb * decay
            mm_t = cb_t * decay_t
            dm = _dot_nt(dyb, xdtb)
            dm_t = _dot_nt(xdtb, dyb)
            dcb = dcb + dm * decay
            dcb_t = dcb_t + dm_t * decay_t
            wv = _dot_nt(bm, dshb)
            dxdt = _dot(mm_t.astype(BF16), dyb) + tec * wv
            dte = jnp.sum(xdt * wv, axis=-1, keepdims=True)
            yo = _dot_nt(cm, shb)
            dah = (jnp.sum(dm * mm, axis=-1, keepdims=True) - jnp.sum(dm_t * mm_t, axis=-1, keepdims=True)
                   - tec * dte + jnp.sum(dyh * yo, axis=-1, keepdims=True) * eac)
            dalast = (jnp.sum(tec * dte, axis=0, keepdims=True)
                      + elh * jnp.sum(dsh * sh, axis=(0, 1), keepdims=True))
            dah = dah + jnp.where(row_l == CHUNK - 1, dalast, 0.0)
            d_acum = d_acum + dah * onehot
            dye = (dyh * eac).astype(BF16)
            d_c = d_c + _dot(dye, shb)
            d_b = d_b + _dot((xdt * tec).astype(BF16), dshb)
            ddt_x = ddt_x + jnp.sum(dxdt * xsh, axis=-1, keepdims=True) * onehot
            dxs_ref[:, hs] = dxdt * dtc + dyh * dsk[:, hs]
            dstate[hs, :] = elh * dsh + _dot_tn(dye, cm)
        dc_ref[...] = d_c + _dot(dcb.astype(BF16), bm)
        db_ref[...] = d_b + _dot(dcb_t.astype(BF16), cm)
        d_da = jnp.dot(upper, d_acum, preferred_element_type=F32, precision=HIGHEST)
        ddt_ref[...] = d_da * a + ddt_x
        da_ref[...] += jnp.sum(d_da * dt, axis=0, keepdims=True)

    rev = lambda c: nc - 1 - c
    wide = pl.BlockSpec((CHUNK, gw), lambda g, c: (rev(c), g))
    return _call(
        body, name="ssd_scan_bwd", grid=(SSM_GROUPS, nc),
        in_specs=[wide,
                  pl.BlockSpec((CHUNK, n), lambda g, c: (rev(c), b0 + g)),
                  pl.BlockSpec((CHUNK, n), lambda g, c: (rev(c), c0 + g)),
                  pl.BlockSpec((None, CHUNK, hpg), lambda g, c: (g, rev(c), 0)),
                  pl.BlockSpec((None, hpg, CHUNK), lambda g, c: (g, 0, rev(c))),
                  pl.BlockSpec((None, 1, hpg), lambda g, c: (g, 0, 0)),
                  pl.BlockSpec((None, hpg, 1), lambda g, c: (g, 0, 0)),
                  pl.BlockSpec((None, None, gw, n), lambda g, c: (g, rev(c), 0, 0)),
                  wide,
                  pl.BlockSpec((None, 1, gw), lambda g, c: (g, 0, 0))],
        out_specs=[wide,
                   pl.BlockSpec((CHUNK, n), lambda g, c: (rev(c), g)),
                   pl.BlockSpec((CHUNK, n), lambda g, c: (rev(c), g)),
                   pl.BlockSpec((None, CHUNK, hpg), lambda g, c: (g, rev(c), 0)),
                   pl.BlockSpec((None, 1, hpg), lambda g, c: (g, 0, 0))],
        out_shape=[jax.ShapeDtypeStruct((s, d_inner), F32),
                   jax.ShapeDtypeStruct((s, SSM_GROUPS * n), F32), jax.ShapeDtypeStruct((s, SSM_GROUPS * n), F32),
                   jax.ShapeDtypeStruct((SSM_GROUPS, s, hpg), F32), jax.ShapeDtypeStruct((SSM_GROUPS, 1, hpg), F32)],
        scratch=[pltpu.VMEM((gw, n), F32)], sem=("parallel", "arbitrary"), ride=ride,
        args=(xbc, xbc, xbc, dt_g, dt_gt, a_g, a_gt, states, dy, dskip_e))


def _lin(p):
    return 4 * p[0] + 2 * p[1] + p[2]


class _Gather:
    def __init__(self, arrs):
        self.arrs = list(arrs)

    def out_shape(self):
        return [jax.ShapeDtypeStruct((NDEV,) + a.shape, a.dtype) for a in self.arrs]

    def _copies(self, ins, outs, sems):
        send_sems, recv_sems, local_sems = sems
        x, y, c = lax.axis_index("x"), lax.axis_index("y"), lax.axis_index("c")
        me, sibling = (x, y, c), (x, y, 1 - c)
        chips = [(1 - x, y), (x, 1 - y), (1 - x, 1 - y)]

        def copy(a, k, block, to, src=None):
            rows = outs[a].at[_lin(block)]
            return pltpu.make_async_remote_copy(
                src_ref=rows if src is None else src, dst_ref=rows,
                send_sem=send_sems.at[a * NPEER + k], recv_sem=recv_sems.at[a * NPEER + k],
                device_id=to, device_id_type=pl.DeviceIdType.MESH)

        na = len(ins)
        mine = [pltpu.make_async_copy(ins[a], outs[a].at[_lin(me)], local_sems.at[a]) for a in range(na)]
        first = []
        for a in range(na):
            first.append(copy(a, 0, me, sibling, src=ins[a]))
            first += [copy(a, 1 + j, me, (*chip, c), src=ins[a]) for j, chip in enumerate(chips)]
        return copy, mine, first, me, sibling, chips, c, na

    def start(self, ins, outs, sems):
        _, mine, first, *_ = self._copies(ins, outs, sems)
        for cp in mine + first:
            cp.start()

    def finish(self, ins, outs, sems):
        copy, mine, first, me, sibling, chips, c, na = self._copies(ins, outs, sems)
        passed = []
        for j, chip in enumerate(chips):
            for a in range(na):
                copy(a, 1 + j, (*chip, c), me).wait_recv()
                cp = copy(a, 4 + j, (*chip, c), sibling)
                cp.start()
                passed.append(cp)
        for a in range(na):
            copy(a, 0, sibling, me).wait_recv()
            for j, chip in enumerate(chips):
                copy(a, 4 + j, (*chip, 1 - c), me).wait_recv()
        for cp in first + passed:
            cp.wait_send()
        for cp in mine:
            cp.wait()


class _Scatter:
    def __init__(self, arrs, ks=tuple(range(NDEV))):
        self.arrs = list(arrs)
        self.ks = tuple(ks)

    def out_shape(self):
        return [jax.ShapeDtypeStruct((len(self.ks),) + a.shape[1:], a.dtype) for a in self.arrs]

    def _copies(self, ins, outs, sems):
        send_sems, recv_sems, local_sems = sems
        x, y, c = lax.axis_index("x"), lax.axis_index("y"), lax.axis_index("c")
        me = (x, y, c)

        def peer(k):
            return (1 - x if k & 4 else x, 1 - y if k & 2 else y, 1 - c if k & 1 else c)

        local, remote = [], []
        for a in range(len(ins)):
            for i, k in enumerate(self.ks):
                if k == 0:
                    local.append(pltpu.make_async_copy(ins[a].at[_lin(me)], outs[a].at[i], local_sems.at[a]))
                else:
                    remote.append(pltpu.make_async_remote_copy(
                        src_ref=ins[a].at[_lin(peer(k))], dst_ref=outs[a].at[i],
                        send_sem=send_sems.at[a * NPEER + k - 1], recv_sem=recv_sems.at[a * NPEER + k - 1],
                        device_id=peer(k), device_id_type=pl.DeviceIdType.MESH))
        return local, remote

    def start(self, ins, outs, sems):
        local, remote = self._copies(ins, outs, sems)
        for cp in local + remote:
            cp.start()

    def finish(self, ins, outs, sems):
        local, remote = self._copies(ins, outs, sems)
        for cp in remote:
            cp.wait_recv()
        for cp in remote:
            cp.wait_send()
        for cp in local:
            cp.wait()


def _exchange_scratch(na):
    return [pltpu.SemaphoreType.DMA((na * NPEER,)), pltpu.SemaphoreType.DMA((na * NPEER,)),
            pltpu.SemaphoreType.DMA((na,))]


def _exchange_alone(ex, *, name, in_vmem=False):
    na = len(ex.arrs)

    def body(*refs):
        ins, outs, sems = refs[:na], refs[na:2 * na], refs[2 * na:]
        ex.start(ins, outs, sems)
        ex.finish(ins, outs, sems)

    spec = pl.BlockSpec(memory_space=pltpu.VMEM if in_vmem else pl.ANY)
    return pl.pallas_call(
        body, name=name, out_shape=tuple(ex.out_shape()), in_specs=[spec] * na, out_specs=tuple([spec] * na),
        scratch_shapes=_exchange_scratch(na),
        compiler_params=pltpu.CompilerParams(vmem_limit_bytes=VMEM_LIMIT),
    )(*ex.arrs)


def _call(body, *, name, grid, in_specs, out_specs, out_shape, args, sem, scratch=(), ride=None, aliases=None):
    n_in, n_out, n_scr = len(in_specs), len(out_specs), len(scratch)
    if ride is None:
        outs = pl.pallas_call(
            body, name=name, grid=grid, in_specs=list(in_specs), out_specs=tuple(out_specs),
            out_shape=tuple(out_shape), scratch_shapes=list(scratch), input_output_aliases=aliases or {},
            compiler_params=_params(*sem))(*args)
        return tuple(outs), ()
    nx = len(ride.arrs)
    hbm = pl.BlockSpec(memory_space=pl.ANY)

    def hosted(*refs):
        ins, x_in = refs[:n_in], refs[n_in:n_in + nx]
        o0 = n_in + nx
        outs, x_out = refs[o0:o0 + n_out], refs[o0 + n_out:o0 + n_out + nx]
        s0 = o0 + n_out + nx
        scr, x_sem = refs[s0:s0 + n_scr], refs[s0 + n_scr:]
        ids = [pl.program_id(i) for i in range(len(grid))]
        first = functools.reduce(jnp.logical_and, [i == 0 for i in ids])
        last = functools.reduce(jnp.logical_and, [i == g - 1 for i, g in zip(ids, grid)])

        @pl.when(first)
        def _():
            ride.start(x_in, x_out, x_sem)

        body(*ins, *outs, *scr)

        @pl.when(last)
        def _():
            ride.finish(x_in, x_out, x_sem)

    outs = pl.pallas_call(
        hosted, name=name, grid=grid, in_specs=list(in_specs) + [hbm] * nx,
        out_specs=tuple(list(out_specs) + [hbm] * nx), out_shape=tuple(list(out_shape) + ride.out_shape()),
        scratch_shapes=list(scratch) + _exchange_scratch(nx), input_output_aliases=aliases or {},
        compiler_params=_params(*(("arbitrary",) * len(grid))))(*args, *ride.arrs)
    return tuple(outs[:n_out]), tuple(outs[n_out:])


def _pack(parts):
    flat = jnp.concatenate([p.reshape(-1).astype(F32) for p in parts])
    tile = SUBLANES * LANES
    pad = (-flat.shape[0]) % tile
    return jnp.pad(flat, (0, pad)).reshape(-1, LANES)


def _unpack(buf, shapes):
    flat = buf.reshape(-1)
    out, off = [], 0
    for shp in shapes:
        size = math.prod(shp)
        out.append(flat[off:off + size].reshape(shp))
        off += size
    return out


KS_NEAR = (0, 1, 4, 5)
KS_FAR = (2, 3, 6, 7)


def _local_step(x, target, wa, wo, ws, wos, rel_bias, conv_w, conv_b, dt_bias, a_log, d_skip, norm_w, ln_g, ln_b,
                dist=False):
    s, d = x.shape
    da = wo.shape[-2]
    heads = da // HEAD_DIM
    qkv_cols = 3 * N_GROUPS_ATTN * da
    d_inner = wos.shape[0] * (NDEV if dist else 1)
    conv_dim = conv_w.shape[1]
    ssm_heads = dt_bias.shape[1]
    hpg = ssm_heads // SSM_GROUPS
    gn = (conv_dim - d_inner) // 2
    n_state = gn // SSM_GROUPS
    gw = d_inner // SSM_GROUPS
    p = gw // hpg
    in_ssm = d_inner + conv_dim + ssm_heads
    xb = _cast_bf16(x, name="cast_x")

    def slabs_of_cols(t):
        return t.reshape(t.shape[0], NDEV, t.shape[1] // NDEV).transpose(1, 0, 2)

    if dist:
        qkv, (wo, ws_slabs) = _mm(xb, wa, name="mm_qkv", out_dtype=BF16, n_out=qkv_cols, ride=_Gather([wo, ws]))
        ws = ws_slabs.transpose(1, 0, 2).reshape(d, in_ssm)
    else:
        qkv = _mm(xb, wa, name="mm_qkv", out_dtype=BF16, n_out=qkv_cols)
    gate = _mm(xb, wa, name="mm_gate", out_dtype=F32, n_off=qkv_cols, n_out=da)
    bias, bucket = _bias_tables(rel_bias, heads)
    os_, ls_ = [], []
    for g, (_, dil) in enumerate(ATTN_PATTERNS):
        o, l = _attn_fwd_group(qkv, bias[g], g, dil, da)
        os_.append(o)
        ls_.append(l)
    o, lse, y = _attn_combine(os_, ls_, gate)
    h1 = _mm(y, wo, name="mm_out_attn", out_dtype=F32)
    x1, x1b = _ln_fwd(x, h1, ln_g[0:1], ln_b[0:1], name="ln1_fwd")

    if dist:
        proj, (wos_slabs,) = _mm(x1b, ws, name="mm_in_ssm", out_dtype=F32, ride=_Gather([wos]))
        wos = wos_slabs.reshape(d_inner, d)
    else:
        proj = _mm(x1b, ws, name="mm_in_ssm", out_dtype=F32)
    xbc = _conv_fwd(proj, conv_w, conv_b, d_inner)
    dt = _dt_fwd(proj, dt_bias, d_inner + conv_dim)
    dt_g = dt.reshape(s, SSM_GROUPS, hpg).transpose(1, 0, 2)
    dt_gt = dt.reshape(s, SSM_GROUPS, hpg).transpose(1, 2, 0)
    a = -jnp.exp(a_log)
    a_g = a.reshape(SSM_GROUPS, 1, hpg)
    a_gt = a.reshape(SSM_GROUPS, hpg, 1)
    dskip_e = jnp.repeat(d_skip.reshape(SSM_GROUPS, 1, hpg), p, axis=2)
    yn, yscan, states = _ssd_fwd(xbc, proj, dt_g, dt_gt, a_g, a_gt, dskip_e, norm_w, d_inner, n_state)
    h2 = _mm(yn, wos, name="mm_out_ssm", out_dtype=F32)

    du2, du2b, dg1, db1, loss_t = _ln_bwd(x1, h2, ln_g[1:2], ln_b[1:2], target, with_loss=True, name="ln2_loss_bwd")
    loss = loss_t[0, 0]
    dyn = _mm(du2b, wos, name="mm_dyn", out_dtype=F32, trans_b=True)
    g_wos = _mm(yn.T, du2b, name="mm_dw_out_ssm", out_dtype=BF16)
    parts = {}
    dyscan, dproj_ssm, g_norm, g_dskip = _ssd_epilogue_bwd(dyn, yscan, xbc, proj, dskip_e, norm_w, hpg)
    ride = _Scatter([g_wos.reshape(NDEV, d_inner // NDEV, d)]) if dist else None
    (dxs, d_bm, d_cm, ddt_g, g_a), rode = _ssd_scan_bwd(xbc, dt_g, dt_gt, a_g, a_gt, states, dyscan, dskip_e,
                                                         d_inner, n_state, ride=ride)
    parts["w_out_ssm"] = list(rode)
    g_alog = g_a.reshape(1, ssm_heads) * a
    dproj_ssm, g_conv_w, g_conv_b = _conv_bwd(proj, conv_w, conv_b, (dxs, d_bm, d_cm), d_inner, dproj_ssm)
    ddt = ddt_g.transpose(1, 0, 2).reshape(s, ssm_heads)
    dproj_ssm, g_dtb = _dt_bwd(proj, dt_bias, ddt, d_inner + conv_dim, dproj_ssm)
    g_ws = _mm(x1b.T, dproj_ssm, name="mm_dw_in_ssm", out_dtype=BF16)
    if dist:
        g_ws_slabs = slabs_of_cols(g_ws)
        dx1, near = _mm(dproj_ssm, ws, name="mm_dx1", out_dtype=F32, trans_b=True, res=du2,
                        res_scale=DEEPNORM_ALPHA, ride=_Scatter([g_ws_slabs], KS_NEAR))
    else:
        dx1 = _mm(dproj_ssm, ws, name="mm_dx1", out_dtype=F32, trans_b=True, res=du2, res_scale=DEEPNORM_ALPHA)

    du1, du1b, dg0, db0 = _ln_bwd(x, h1, ln_g[0:1], ln_b[0:1], dx1, with_loss=False, name="ln1_bwd")
    dy = _mm(du1b, wo, name="mm_dy", out_dtype=F32, trans_b=True)
    g_wo = _mm(y.T, du1b, name="mm_dw_out_attn", out_dtype=BF16, slab_out=NDEV)
    do, dgate, dd = _attn_bwd_prep(dy, o, gate)
    dparts, dss = [], []
    for g, (_, dil) in enumerate(ATTN_PATTERNS):
        dq, dk, dv, ds = _attn_bwd_group(qkv, do, lse, dd, bias[g], g, dil, da)
        dparts += [dq, dk, dv]
        dss.append(ds)
    g_bias = _bias_bwd(jnp.stack(dss), bucket)
    g_rel_bias = g_bias.transpose(2, 0, 1).reshape(NUM_BUCKETS, N_GROUPS_ATTN * heads)
    dproj_attn = jnp.concatenate(dparts + [dgate], axis=1)
    if dist:
        g_wa, far = _mm(xb.T, dproj_attn, name="mm_dw_in_attn", out_dtype=BF16, slab_out=NDEV,
                        ride=_Scatter([g_ws_slabs], KS_FAR))
        parts["w_in_ssm"] = [near[0], far[0]]
        dx, rode = _mm(dproj_attn, wa, name="mm_dx", out_dtype=F32, trans_b=True, res=du1,
                       res_scale=DEEPNORM_ALPHA, ride=_Scatter([g_wa, g_wo]))
        parts["w_in_attn"], parts["w_out_attn"] = [rode[0]], [rode[1]]
    else:
        g_wa = _mm(xb.T, dproj_attn, name="mm_dw_in_attn", out_dtype=BF16, slab_out=NDEV)
        dx = _mm(dproj_attn, wa, name="mm_dx", out_dtype=F32, trans_b=True, res=du1, res_scale=DEEPNORM_ALPHA)

    g_ln_g = jnp.concatenate([dg0, dg1], axis=0)
    g_ln_b = jnp.concatenate([db0, db1], axis=0)
    small = dict(rel_bias=g_rel_bias, dt_bias=g_dtb, a_log=g_alog, d_skip=g_dskip.reshape(1, ssm_heads),
                 ln_g=g_ln_g, ln_b=g_ln_b, conv_w=g_conv_w, conv_b=g_conv_b, ssm_norm_w=g_norm)
    if dist:
        return loss, dx, parts, small
    return loss, dx, g_wa, g_wo, g_ws, g_wos, small


REPLICATED = ("rel_bias", "dt_bias", "a_log", "d_skip", "ln_g", "ln_b")
SHARDED_SMALL = ("conv_w", "conv_b", "ssm_norm_w")


def kernel(x, w_in_attn, w_out_attn, rel_bias, w_in_ssm, conv_w, conv_b, dt_bias, a_log, d_skip, ssm_norm_w, w_out_ssm, ln_g, ln_b, loss_target, m_w_in_attn, m_w_out_attn, m_rel_bias, m_w_in_ssm, m_conv_w, m_conv_b, m_dt_bias, m_a_log, m_d_skip, m_ssm_norm_w, m_w_out_ssm, m_ln_g, m_ln_b, v_w_in_attn, v_w_out_attn, v_rel_bias, v_w_in_ssm, v_conv_w, v_conv_b, v_dt_bias, v_a_log, v_d_skip, v_ssm_norm_w, v_w_out_ssm, v_ln_g, v_ln_b):
    w = dict(w_in_attn=w_in_attn, w_out_attn=w_out_attn, rel_bias=rel_bias, w_in_ssm=w_in_ssm, conv_w=conv_w,
             conv_b=conv_b, dt_bias=dt_bias, a_log=a_log, d_skip=d_skip, ssm_norm_w=ssm_norm_w,
             w_out_ssm=w_out_ssm, ln_g=ln_g, ln_b=ln_b)
    m = dict(w_in_attn=m_w_in_attn, w_out_attn=m_w_out_attn, rel_bias=m_rel_bias, w_in_ssm=m_w_in_ssm,
             conv_w=m_conv_w, conv_b=m_conv_b, dt_bias=m_dt_bias, a_log=m_a_log, d_skip=m_d_skip,
             ssm_norm_w=m_ssm_norm_w, w_out_ssm=m_w_out_ssm, ln_g=m_ln_g, ln_b=m_ln_b)
    v = dict(w_in_attn=v_w_in_attn, w_out_attn=v_w_out_attn, rel_bias=v_rel_bias, w_in_ssm=v_w_in_ssm,
             conv_w=v_conv_w, conv_b=v_conv_b, dt_bias=v_dt_bias, a_log=v_a_log, d_skip=v_d_skip,
             ssm_norm_w=v_ssm_norm_w, w_out_ssm=v_w_out_ssm, ln_g=v_ln_g, ln_b=v_ln_b)
    me = _lin((lax.axis_index("x"), lax.axis_index("y"), lax.axis_index("c")))
    d = x.shape[2]
    big = ("w_in_attn", "w_out_attn", "w_in_ssm", "w_out_ssm")

    shards = {k: _cast_bf16(w[k][0], name=f"cast_{k}") for k in big}
    (wa,) = _exchange_alone(_Gather([shards["w_in_attn"]]), name="gather_w_in_attn")
    cpd = conv_w.shape[2]
    npd = ssm_norm_w.shape[1]
    small_shapes = [(CONV_WIDTH, cpd), (1, cpd), (1, npd)]
    (small_all,) = _exchange_alone(_Gather([_pack([conv_w[0], conv_b, ssm_norm_w])]), name="gather_small_weights",
                                   in_vmem=True)
    small_parts = [_unpack(small_all[i], small_shapes) for i in range(NDEV)]
    conv_w_full = jnp.concatenate([p[0] for p in small_parts], axis=1)
    conv_b_full = jnp.concatenate([p[1] for p in small_parts], axis=1)
    norm_w_full = jnp.concatenate([p[2] for p in small_parts], axis=1)

    loss, dx, parts, small = _local_step(
        x[0], loss_target[0], wa, shards["w_out_attn"], shards["w_in_ssm"], shards["w_out_ssm"], rel_bias,
        conv_w_full, conv_b_full, dt_bias[0:1], a_log[0:1], d_skip[0:1], norm_w_full, ln_g, ln_b, dist=True)
    loss = lax.psum(loss, MESH_AXES)
    out = {}
    for k in big:
        out[k] = _adamw_sum(parts[k], w[k][0], m[k][0], v[k][0], name=f"adamw_{k}")

    order = REPLICATED + SHARDED_SMALL
    g_shapes = [small[k].shape for k in order]
    (g_all,) = _exchange_alone(_Gather([_pack([small[k] for k in order])]), name="gather_small_grads", in_vmem=True)
    g_sum = dict(zip(order, _unpack(_sum_slots(g_all, name="sum_small_grads"), g_shapes)))
    g_mine = {k: g_sum[k] for k in REPLICATED}
    g_mine["conv_w"] = lax.dynamic_slice_in_dim(g_sum["conv_w"], me * cpd, cpd, axis=1)
    g_mine["conv_b"] = lax.dynamic_slice_in_dim(g_sum["conv_b"], me * cpd, cpd, axis=1)
    g_mine["ssm_norm_w"] = lax.dynamic_slice_in_dim(g_sum["ssm_norm_w"], me * npd, npd, axis=1)
    w_shapes = [w[k].shape for k in order]
    g_pack = _pack([g_mine[k] for k in order])
    d_p, m_p, v_p = _adamw_small(g_pack, _pack([w[k] for k in order]), _pack([m[k] for k in order]),
                                 _pack([v[k] for k in order]), name="adamw_small")
    for k, gk, dk, mk, vk in zip(order, _unpack(g_pack, w_shapes), _unpack(d_p, w_shapes), _unpack(m_p, w_shapes),
                                 _unpack(v_p, w_shapes)):
        out[k] = (gk, dk, mk, vk)
    for k in big:
        out[k] = tuple(t[None] for t in out[k])

    names = ("w_in_attn", "w_out_attn", "rel_bias", "w_in_ssm", "conv_w", "conv_b", "dt_bias", "a_log", "d_skip",
             "ssm_norm_w", "w_out_ssm", "ln_g", "ln_b")
    res = [loss, dx[None]]
    for i in range(4):
        res += [out[k][i] for k in names]
    return tuple(res)
```

```python
import functools
import math

import jax
import jax.numpy as jnp
from jax import lax
from jax.experimental import pallas as pl
from jax.experimental.pallas import tpu as pltpu

F32 = jnp.float32
BF16 = jnp.bfloat16
MESH_AXES = ("x", "y", "c")
NDEV = 8
NPEER = NDEV - 1
LANES = 128
SUBLANES = 8
VMEM_LIMIT = 52 * 1024 * 1024
MM_VMEM_BUDGET = 40 * 1024 * 1024
MM_TK_MAX = 4096
MM_TN_MAX = 1024

ATTN_PATTERNS = ((128, 1), (512, 4), (2048, 16))
N_GROUPS_ATTN = 3
HEAD_DIM = 128
ATTN_BLOCK = 128
NUM_BUCKETS = 32
MAX_DISTANCE = 2048
SSM_GROUPS = 8
CONV_WIDTH = 4
CHUNK = 128
DEPTH = 2
DEEPNORM_ALPHA = (2 * DEPTH) ** 0.25
LN_EPS = 1e-5
RMS_EPS = 1e-5
NEG_INF = -1e30
ADAM_LR = 0.001
ADAM_B1 = 0.9
ADAM_B2 = 0.999
ADAM_EPS = 1e-08
ADAM_WD = 0.01
ADAM_STEP = 10
HIGHEST = lax.Precision.HIGHEST


def _params(*sem):
    return pltpu.CompilerParams(dimension_semantics=sem, vmem_limit_bytes=VMEM_LIMIT)


def _pick(n, prefs):
    for p in prefs:
        if n % p == 0:
            return p
    return n


def _dot(a, b):
    return jnp.dot(a, b, preferred_element_type=F32)


def _dot_nt(a, b):
    return lax.dot_general(a, b, (((1,), (1,)), ((), ())), preferred_element_type=F32)


def _dot_tn(a, b):
    return lax.dot_general(a, b, (((0,), (0,)), ((), ())), preferred_element_type=F32)


def _sigmoid(x):
    return 1.0 / (1.0 + jnp.exp(-x))


def _mm(a, b, *, name, out_dtype, trans_b=False, slab_out=0, n_off=0, n_out=None,
        res=None, res_scale=1.0, ride=None):
    m, k = a.shape
    slab_b = b.ndim == 3
    if slab_b:
        ns = b.shape[0]
        if trans_b:
            n, kper = b.shape[1], b.shape[2]
            assert ns * kper == k
        else:
            nper = b.shape[2]
            n = ns * nper
            assert b.shape[1] == k
    else:
        n = b.shape[0] if trans_b else b.shape[1]
        assert (b.shape[1] if trans_b else b.shape[0]) == k
    n_out = n if n_out is None else n_out
    tm = _pick(m, (1024, 512, 256, 128))
    nconstraint = math.gcd(n_out, n_off) if n_off else n_out
    if slab_b and not trans_b:
        nconstraint = math.gcd(nconstraint, nper)
    if slab_out:
        nconstraint = math.gcd(nconstraint, n_out // slab_out)
    kconstraint = kper if (slab_b and trans_b) else k
    tk = max(t for t in range(LANES, min(kconstraint, MM_TK_MAX) + 1, LANES) if kconstraint % t == 0)
    nk = k // tk
    out_bytes = jnp.dtype(out_dtype).itemsize

    def vmem_bytes(t):
        return (2 * 2 * tk * (tm + t) + 2 * tm * t * out_bytes + (4 * tm * t if nk > 1 else 0)
                + (2 * 4 * tm * t if res is not None else 0))

    fits = [t for t in range(LANES, min(nconstraint, MM_TN_MAX) + 1, LANES)
            if nconstraint % t == 0 and vmem_bytes(t) <= MM_VMEM_BUDGET]
    tn = max(fits)
    nb0 = n_off // tn
    grid = (m // tm, n_out // tn, nk)

    a_spec = pl.BlockSpec((tm, tk), lambda i, j, kk: (i, kk))
    if slab_b and not trans_b:
        nps = nper // tn
        b_spec = pl.BlockSpec((None, tk, tn), lambda i, j, kk: ((j + nb0) // nps, kk, (j + nb0) % nps))
    elif slab_b and trans_b:
        kps = kper // tk
        b_spec = pl.BlockSpec((None, tn, tk), lambda i, j, kk: (kk // kps, j + nb0, kk % kps))
    elif trans_b:
        b_spec = pl.BlockSpec((tn, tk), lambda i, j, kk: (j + nb0, kk))
    else:
        b_spec = pl.BlockSpec((tk, tn), lambda i, j, kk: (kk, j + nb0))
    if slab_out:
        ops = (n_out // slab_out) // tn
        o_spec = pl.BlockSpec((None, tm, tn), lambda i, j, kk: (j // ops, i, j % ops))
        o_shape = jax.ShapeDtypeStruct((slab_out, m, n_out // slab_out), out_dtype)
    else:
        o_spec = pl.BlockSpec((tm, tn), lambda i, j, kk: (i, j))
        o_shape = jax.ShapeDtypeStruct((m, n_out), out_dtype)
    in_specs = [a_spec, b_spec]
    args = [a, b]
    if res is not None:
        in_specs.append(pl.BlockSpec((tm, tn), lambda i, j, kk: (i, j)))
        args.append(res)

    def body(*refs):
        a_ref, b_ref = refs[0], refs[1]
        r_ref = refs[2] if res is not None else None
        o_ref = refs[3] if res is not None else refs[2]
        av = a_ref[...].astype(BF16)
        bv = b_ref[...].astype(BF16)
        part = _dot_nt(av, bv) if trans_b else _dot(av, bv)

        def finish(r):
            if res is not None:
                r = r + res_scale * r_ref[...]
            o_ref[...] = r.astype(out_dtype)

        if nk == 1:
            finish(part)
            return
        acc = refs[-1]
        kk = pl.program_id(2)

        @pl.when(kk == 0)
        def _():
            acc[...] = part

        @pl.when(kk > 0)
        def _():
            acc[...] += part

        @pl.when(kk == nk - 1)
        def _():
            finish(acc[...])

    outs, rode = _call(
        body, name=name, grid=grid, in_specs=in_specs, out_specs=[o_spec], out_shape=[o_shape], args=args,
        scratch=[pltpu.VMEM((tm, tn), F32)] if nk > 1 else [], ride=ride,
        sem=("parallel", "parallel", "arbitrary"))
    return (outs[0], rode) if ride is not None else outs[0]


def _cast_bf16(w, *, name):
    r, c = w.shape
    tr = _pick(r, (512, 256, 128, 64, 32, 16, 8))

    def body(w_ref, o_ref):
        o_ref[...] = w_ref[...].astype(BF16)

    return pl.pallas_call(
        body, name=name, grid=(r // tr,),
        in_specs=[pl.BlockSpec((tr, c), lambda i: (i, 0))],
        out_specs=pl.BlockSpec((tr, c), lambda i: (i, 0)),
        out_shape=jax.ShapeDtypeStruct((r, c), BF16),
        compiler_params=_params("parallel"),
    )(w)


def _adam_math(w, g, m, v):
    m2 = ADAM_B1 * m + (1.0 - ADAM_B1) * g
    v2 = ADAM_B2 * v + (1.0 - ADAM_B2) * (g * g)
    m_hat = m2 / (1.0 - ADAM_B1 ** ADAM_STEP)
    v_hat = v2 / (1.0 - ADAM_B2 ** ADAM_STEP)
    delta = -ADAM_LR * (m_hat / (jnp.sqrt(v_hat) + ADAM_EPS) + ADAM_WD * w)
    return delta, m2, v2


def _adamw_sum(parts, w, m, v, *, name):
    r, c = w.shape
    tr = _pick(r, (128, 64, 32, 16, 8))
    tc = c if (c % LANES or c <= 2560) else _pick(c, (2048, 1024, 512, 256, 128))
    npart = len(parts)

    def body(*refs):
        p_refs = refs[:npart]
        w_ref, m_ref, v_ref, g_out, d_out, m_out, v_out = refs[npart:]
        g = None
        for p_ref in p_refs:
            for s in range(p_ref.shape[0]):
                t = p_ref[s].astype(F32)
                g = t if g is None else g + t
        d, m2, v2 = _adam_math(w_ref[...], g, m_ref[...], v_ref[...])
        g_out[...] = g
        d_out[...] = d
        m_out[...] = m2
        v_out[...] = v2

    spec = pl.BlockSpec((tr, tc), lambda i, j: (i, j))
    shp = jax.ShapeDtypeStruct((r, c), F32)
    return pl.pallas_call(
        body, name=name, grid=(r // tr, c // tc),
        in_specs=[pl.BlockSpec((p.shape[0], tr, tc), lambda i, j: (0, i, j)) for p in parts] + [spec, spec, spec],
        out_specs=(spec, spec, spec, spec), out_shape=(shp, shp, shp, shp),
        compiler_params=_params("parallel", "parallel"),
    )(*parts, w, m, v)


def _adamw_small(g, w, m, v, *, name):
    shp = jax.ShapeDtypeStruct(w.shape, F32)

    def body(g_ref, w_ref, m_ref, v_ref, d_out, m_out, v_out):
        d, m2, v2 = _adam_math(w_ref[...], g_ref[...], m_ref[...], v_ref[...])
        d_out[...] = d
        m_out[...] = m2
        v_out[...] = v2

    return pl.pallas_call(body, name=name, out_shape=(shp, shp, shp),
                          compiler_params=pltpu.CompilerParams(vmem_limit_bytes=VMEM_LIMIT))(g, w, m, v)


def _sum_slots(parts, *, name):
    _, r, c = parts.shape

    def body(p_ref, o_ref):
        g = p_ref[0]
        for s in range(1, NDEV):
            g = g + p_ref[s]
        o_ref[...] = g

    return pl.pallas_call(body, name=name, out_shape=jax.ShapeDtypeStruct((r, c), F32),
                          compiler_params=pltpu.CompilerParams(vmem_limit_bytes=VMEM_LIMIT))(parts)


def _ln_parts(u):
    mu = jnp.mean(u, axis=-1, keepdims=True)
    xc = u - mu
    var = jnp.mean(xc * xc, axis=-1, keepdims=True)
    rstd = lax.rsqrt(var + LN_EPS)
    return xc * rstd, rstd


def _ln_fwd(xin, h, g, b, *, name):
    s, d = xin.shape
    tm = _pick(s, (128,))

    def body(x_ref, h_ref, g_ref, b_ref, o_ref, ob_ref):
        xhat, _ = _ln_parts(DEEPNORM_ALPHA * x_ref[...] + h_ref[...])
        o = xhat * g_ref[...] + b_ref[...]
        o_ref[...] = o
        ob_ref[...] = o.astype(BF16)

    row = pl.BlockSpec((tm, d), lambda i: (i, 0))
    vec = pl.BlockSpec((1, d), lambda i: (0, 0))
    return pl.pallas_call(
        body, name=name, grid=(s // tm,), in_specs=[row, row, vec, vec], out_specs=(row, row),
        out_shape=(jax.ShapeDtypeStruct((s, d), F32), jax.ShapeDtypeStruct((s, d), BF16)),
        compiler_params=_params("parallel"),
    )(xin, h, g, b)


def _ln_bwd(xin, h, g, b, cot, *, with_loss, name):
    s, d = xin.shape
    tm = _pick(s, (128,))

    def body(x_ref, h_ref, g_ref, b_ref, c_ref, du_ref, dub_ref, dg_ref, db_ref, *rest):
        i = pl.program_id(0)
        xhat, rstd = _ln_parts(DEEPNORM_ALPHA * x_ref[...] + h_ref[...])
        gv = g_ref[...]
        if with_loss:
            diff = xhat * gv + b_ref[...] - c_ref[...]
            part = 0.5 * jnp.sum(jnp.mean(diff * diff, axis=-1, keepdims=True), axis=0, keepdims=True)
            dout = diff / d
        else:
            dout = c_ref[...]

        @pl.when(i == 0)
        def _():
            dg_ref[...] = jnp.zeros_like(dg_ref)
            db_ref[...] = jnp.zeros_like(db_ref)
            if with_loss:
                rest[0][...] = jnp.zeros_like(rest[0])

        dg_ref[...] += jnp.sum(dout * xhat, axis=0, keepdims=True)
        db_ref[...] += jnp.sum(dout, axis=0, keepdims=True)
        if with_loss:
            rest[0][...] += jnp.broadcast_to(part, rest[0].shape)
        dxh = dout * gv
        du = rstd * (dxh - jnp.mean(dxh, axis=-1, keepdims=True)
                     - xhat * jnp.mean(dxh * xhat, axis=-1, keepdims=True))
        du_ref[...] = du
        dub_ref[...] = du.astype(BF16)

    row = pl.BlockSpec((tm, d), lambda i: (i, 0))
    vec = pl.BlockSpec((1, d), lambda i: (0, 0))
    out_specs = [row, row, vec, vec]
    out_shape = [jax.ShapeDtypeStruct((s, d), F32), jax.ShapeDtypeStruct((s, d), BF16),
                 jax.ShapeDtypeStruct((1, d), F32), jax.ShapeDtypeStruct((1, d), F32)]
    if with_loss:
        out_specs.append(pl.BlockSpec((SUBLANES, LANES), lambda i: (0, 0)))
        out_shape.append(jax.ShapeDtypeStruct((SUBLANES, LANES), F32))
    return pl.pallas_call(
        body, name=name, grid=(s // tm,), in_specs=[row, row, vec, vec, row],
        out_specs=tuple(out_specs), out_shape=tuple(out_shape),
        compiler_params=_params("arbitrary"),
    )(xin, h, g, b, cot)


def t5_causal_bucket(dist):
    max_exact = NUM_BUCKETS // 2
    d_f = jnp.maximum(dist, 1).astype(jnp.float32)
    large = max_exact + (jnp.log(d_f / max_exact) / math.log(MAX_DISTANCE / max_exact)
                         * (NUM_BUCKETS - max_exact)).astype(jnp.int32)
    large = jnp.minimum(large, NUM_BUCKETS - 1)
    return jnp.where(dist < max_exact, dist, large)


def _bias_tables(rel_bias, heads):
    qi = lax.broadcasted_iota(jnp.int32, (ATTN_BLOCK, 2 * ATTN_BLOCK), 0)
    ki = lax.broadcasted_iota(jnp.int32, (ATTN_BLOCK, 2 * ATTN_BLOCK), 1)
    delta = ATTN_BLOCK + qi - ki
    buckets = []
    for window, dilation in ATTN_PATTERNS:
        span = window // dilation
        assert span == ATTN_BLOCK
        band = (delta >= 0) & (delta <= span)
        buckets.append(jnp.where(band, t5_causal_bucket(jnp.clip(delta, 0, None) * dilation), -1))
    bucket = jnp.stack(buckets).astype(jnp.int32)

    def body(bk_ref, tbl_ref, o_ref):
        col = pl.program_id(0) * heads + pl.program_id(1)
        bk = bk_ref[...]
        acc = jnp.full(bk.shape, NEG_INF, F32)
        for b in range(NUM_BUCKETS):
            acc = jnp.where(bk == b, tbl_ref[b, col], acc)
        o_ref[...] = acc

    tile = (None, ATTN_BLOCK, 2 * ATTN_BLOCK)
    bias = pl.pallas_call(
        body, name="bias_fwd", grid=(N_GROUPS_ATTN, heads),
        in_specs=[pl.BlockSpec(tile, lambda g, h: (g, 0, 0)), pl.BlockSpec(memory_space=pltpu.SMEM)],
        out_specs=pl.BlockSpec((None,) + tile, lambda g, h: (g, h, 0, 0)),
        out_shape=jax.ShapeDtypeStruct((N_GROUPS_ATTN, heads, ATTN_BLOCK, 2 * ATTN_BLOCK), F32),
        compiler_params=_params("parallel", "parallel"),
    )(bucket, rel_bias)
    return bias, bucket


def _attn_logits(q, kc, kp, bias_c, bias_p, j):
    scale = HEAD_DIM ** -0.5
    sc = _dot_nt(q, kc) * scale + bias_c
    sp = _dot_nt(q, kp) * scale + jnp.where(j > 0, bias_p, NEG_INF)
    return sc, sp


def _dilated_view(qkv, g, dilation, da):
    heads = da // HEAD_DIM
    if dilation == 1:
        return qkv, qkv.shape[1] // HEAD_DIM, g * 3 * heads
    sub = qkv[:, g * 3 * da:(g + 1) * 3 * da]
    return sub.reshape(qkv.shape[0] // dilation, dilation * 3 * da), 3 * heads, 0


def _attn_fwd_group(qkv, bias_g, g, dilation, da):
    s = qkv.shape[0]
    heads = da // HEAD_DIM
    l = s // dilation
    nb = l // ATTN_BLOCK
    view, cpb, base = _dilated_view(qkv, g, dilation, da)

    def body(q_ref, k_ref, v_ref, b_ref, o_ref, l_ref):
        bias_p = b_ref[0, :, 0:ATTN_BLOCK]
        bias_c = b_ref[0, :, ATTN_BLOCK:2 * ATTN_BLOCK]

        def step(j, carry):
            r0 = pl.multiple_of(j * ATTN_BLOCK, ATTN_BLOCK)
            rp = pl.multiple_of(jnp.maximum(j - 1, 0) * ATTN_BLOCK, ATTN_BLOCK)
            q = q_ref[pl.ds(r0, ATTN_BLOCK), :]
            sc, sp = _attn_logits(q, k_ref[pl.ds(r0, ATTN_BLOCK), :], k_ref[pl.ds(rp, ATTN_BLOCK), :],
                                  bias_c, bias_p, j)
            mx = jnp.maximum(jnp.max(sc, axis=-1, keepdims=True), jnp.max(sp, axis=-1, keepdims=True))
            pc = jnp.exp(sc - mx)
            pp = jnp.exp(sp - mx)
            den = jnp.sum(pc, axis=-1, keepdims=True) + jnp.sum(pp, axis=-1, keepdims=True)
            inv = 1.0 / den
            o = (_dot((pc * inv).astype(BF16), v_ref[pl.ds(r0, ATTN_BLOCK), :])
                 + _dot((pp * inv).astype(BF16), v_ref[pl.ds(rp, ATTN_BLOCK), :]))
            o_ref[pl.ds(r0, ATTN_BLOCK), :] = o
            l_ref[pl.ds(r0, ATTN_BLOCK), :] = jnp.broadcast_to(mx + jnp.log(den), (ATTN_BLOCK, HEAD_DIM))
            return carry

        lax.fori_loop(0, nb, step, 0)

    def col(t):
        return lambda r, h: (0, r * cpb + base + t * heads + h)

    blk = (l, HEAD_DIM)
    out = pl.BlockSpec(blk, lambda r, h: (0, r * heads + h))
    shp = jax.ShapeDtypeStruct((l, dilation * da), F32)
    o, lse = pl.pallas_call(
        body, name=f"attn_fwd_g{g}", grid=(dilation, heads),
        in_specs=[pl.BlockSpec(blk, col(0)), pl.BlockSpec(blk, col(1)), pl.BlockSpec(blk, col(2)),
                  pl.BlockSpec((1, ATTN_BLOCK, 2 * ATTN_BLOCK), lambda r, h: (h, 0, 0))],
        out_specs=(out, out), out_shape=(shp, shp),
        compiler_params=_params("parallel", "parallel"),
    )(view, view, view, bias_g)
    return o.reshape(s, da), lse.reshape(s, da)


def _attn_combine(os_, ls_, gate):
    s, da = gate.shape
    tm = _pick(s, (512, 256, 128))
    tc = _pick(da, (512, 256, 128))

    def body(o0, o1, o2, l0, l1, l2, g_ref, o_ref, l_ref, y_ref):
        a0, a1, a2 = l0[...], l1[...], l2[...]
        mx = jnp.maximum(jnp.maximum(a0, a1), a2)
        e0, e1, e2 = jnp.exp(a0 - mx), jnp.exp(a1 - mx), jnp.exp(a2 - mx)
        den = e0 + e1 + e2
        o = (e0 * o0[...] + e1 * o1[...] + e2 * o2[...]) / den
        gv = g_ref[...]
        o_ref[...] = o
        l_ref[...] = mx + jnp.log(den)
        y_ref[...] = (o * (gv * _sigmoid(gv))).astype(BF16)

    spec = pl.BlockSpec((tm, tc), lambda i, j: (i, j))
    return pl.pallas_call(
        body, name="attn_combine", grid=(s // tm, da // tc), in_specs=[spec] * 7, out_specs=(spec, spec, spec),
        out_shape=(jax.ShapeDtypeStruct((s, da), F32), jax.ShapeDtypeStruct((s, da), F32),
                   jax.ShapeDtypeStruct((s, da), BF16)),
        compiler_params=_params("parallel", "parallel"),
    )(*os_, *ls_, gate)


def _attn_bwd_prep(dy, o, gate):
    s, da = gate.shape
    tm = _pick(s, (512, 256, 128))

    def body(dy_ref, o_ref, g_ref, do_ref, dg_ref, dd_ref):
        gv = g_ref[...]
        sg = _sigmoid(gv)
        dyv = dy_ref[...]
        ov = o_ref[...]
        do = dyv * (gv * sg)
        do_ref[...] = do.astype(BF16)
        dg_ref[...] = (dyv * ov * (sg * (1.0 + gv * (1.0 - sg)))).astype(BF16)
        dd_ref[...] = jnp.broadcast_to(jnp.sum(do * ov, axis=-1, keepdims=True), (tm, HEAD_DIM))

    spec = pl.BlockSpec((tm, HEAD_DIM), lambda i, j: (i, j))
    return pl.pallas_call(
        body, name="attn_bwd_prep", grid=(s // tm, da // HEAD_DIM), in_specs=[spec] * 3,
        out_specs=(spec, spec, spec),
        out_shape=(jax.ShapeDtypeStruct((s, da), BF16), jax.ShapeDtypeStruct((s, da), BF16),
                   jax.ShapeDtypeStruct((s, da), F32)),
        compiler_params=_params("parallel", "parallel"),
    )(dy, o, gate)


def _attn_bwd_group(qkv, do, lse, dd, bias_g, g, dilation, da):
    s = qkv.shape[0]
    heads = da // HEAD_DIM
    l = s // dilation
    nb = l // ATTN_BLOCK
    view, cpb, base = _dilated_view(qkv, g, dilation, da)
    scale = HEAD_DIM ** -0.5

    def body(q_ref, k_ref, v_ref, do_ref, l_ref, dd_ref, b_ref, dq_ref, dk_ref, dv_ref, ds_ref, dk_acc, dv_acc):
        r = pl.program_id(1)
        bias_p = b_ref[0, :, 0:ATTN_BLOCK]
        bias_c = b_ref[0, :, ATTN_BLOCK:2 * ATTN_BLOCK]

        @pl.when(r == 0)
        def _():
            ds_ref[...] = jnp.zeros_like(ds_ref)

        dk_acc[...] = jnp.zeros_like(dk_acc)
        dv_acc[...] = jnp.zeros_like(dv_acc)

        def step(j, carry):
            r0 = pl.multiple_of(j * ATTN_BLOCK, ATTN_BLOCK)
            rp = pl.multiple_of(jnp.maximum(j - 1, 0) * ATTN_BLOCK, ATTN_BLOCK)
            cur = pl.ds(r0, ATTN_BLOCK)
            prev = pl.ds(rp, ATTN_BLOCK)
            q = q_ref[cur, :]
            kc, kp = k_ref[cur, :], k_ref[prev, :]
            dov = do_ref[cur, :]
            sc, sp = _attn_logits(q, kc, kp, bias_c, bias_p, j)
            lrow = l_ref[cur, 0:1]
            drow = dd_ref[cur, 0:1]
            pc = jnp.exp(sc - lrow)
            pp = jnp.exp(sp - lrow)
            dsc = pc * (_dot_nt(dov, v_ref[cur, :]) - drow)
            dsp = pp * (_dot_nt(dov, v_ref[prev, :]) - drow)
            ds_ref[0, :, ATTN_BLOCK:2 * ATTN_BLOCK] += dsc
            ds_ref[0, :, 0:ATTN_BLOCK] += dsp
            dscb, dspb = dsc.astype(BF16), dsp.astype(BF16)
            dq_ref[cur, :] = ((_dot(dscb, kc) + _dot(dspb, kp)) * scale).astype(BF16)
            dk_acc[cur, :] += _dot_tn(dscb, q) * scale
            dk_acc[prev, :] += _dot_tn(dspb, q) * scale
            dv_acc[cur, :] += _dot_tn(pc.astype(BF16), dov)
            dv_acc[prev, :] += _dot_tn(pp.astype(BF16), dov)
            return carry

        lax.fori_loop(0, nb, step, 0)
        dk_ref[...] = dk_acc[...].astype(BF16)
        dv_ref[...] = dv_acc[...].astype(BF16)

    def col(t):
        return lambda h, r: (0, r * cpb + base + t * heads + h)

    blk = (l, HEAD_DIM)
    act = pl.BlockSpec(blk, lambda h, r: (0, r * heads + h))
    shp = jax.ShapeDtypeStruct((l, dilation * da), BF16)
    actv = lambda t: t.reshape(l, dilation * da)
    dq, dk, dv, ds = pl.pallas_call(
        body, name=f"attn_bwd_g{g}", grid=(heads, dilation),
        in_specs=[pl.BlockSpec(blk, col(0)), pl.BlockSpec(blk, col(1)), pl.BlockSpec(blk, col(2)), act, act, act,
                  pl.BlockSpec((1, ATTN_BLOCK, 2 * ATTN_BLOCK), lambda h, r: (h, 0, 0))],
        out_specs=(act, act, act, pl.BlockSpec((1, ATTN_BLOCK, 2 * ATTN_BLOCK), lambda h, r: (h, 0, 0))),
        out_shape=(shp, shp, shp, jax.ShapeDtypeStruct((heads, ATTN_BLOCK, 2 * ATTN_BLOCK), F32)),
        scratch_shapes=[pltpu.VMEM(blk, F32), pltpu.VMEM(blk, F32)],
        compiler_params=_params("parallel", "arbitrary"),
    )(view, view, view, actv(do), actv(lse), actv(dd), bias_g)
    return dq.reshape(s, da), dk.reshape(s, da), dv.reshape(s, da), ds


def _bias_bwd(ds, bucket):
    ng, heads = ds.shape[0], ds.shape[1]

    def body(ds_ref, bk_ref, o_ref):
        bk = bk_ref[...]
        x = ds_ref[...]
        for b in range(NUM_BUCKETS):
            o_ref[:, b:b + 1] = jnp.sum(jnp.where(bk == b, x, 0.0), axis=(0, 1), keepdims=True)

    tile = (None, ATTN_BLOCK, 2 * ATTN_BLOCK)
    out = pl.pallas_call(
        body, name="bias_bwd", grid=(ng, heads),
        in_specs=[pl.BlockSpec((None,) + tile, lambda g, h: (g, h, 0, 0)), pl.BlockSpec(tile, lambda g, h: (g, 0, 0))],
        out_specs=pl.BlockSpec((None, None, 1, NUM_BUCKETS), lambda g, h: (g, h, 0, 0)),
        out_shape=jax.ShapeDtypeStruct((ng, heads, 1, NUM_BUCKETS), F32),
        compiler_params=_params("parallel", "parallel"),
    )(ds, bucket)
    return out.reshape(ng, heads, NUM_BUCKETS)


def _shift_rows(x, halo, s):
    r = pltpu.roll(x, s, axis=0)
    rh = pltpu.roll(halo, s, axis=0)
    row = lax.broadcasted_iota(jnp.int32, halo.shape, 0)
    top = jnp.where(row < s, rh, r[0:SUBLANES])
    return jnp.concatenate([top, r[SUBLANES:]], axis=0)


def _conv_out(x, halo, w, b):
    acc = b + w[CONV_WIDTH - 1:CONV_WIDTH] * x
    for kk in range(CONV_WIDTH - 1):
        acc = acc + w[kk:kk + 1] * _shift_rows(x, halo, CONV_WIDTH - 1 - kk)
    return acc


def _conv_fwd(proj, conv_w, conv_b, col0):
    s = proj.shape[0]
    c = conv_w.shape[1]
    ts = _pick(s, (512, 256, 128))
    tc = _pick(math.gcd(c, col0), (512, 256, 128))
    cb0 = col0 // tc
    hb = ts // SUBLANES

    def body(x_ref, h_ref, w_ref, b_ref, o_ref):
        i = pl.program_id(0)
        halo = jnp.where(i > 0, h_ref[...], 0.0)
        u = _conv_out(x_ref[...], halo, w_ref[...], b_ref[...])
        o_ref[...] = u * _sigmoid(u)

    return pl.pallas_call(
        body, name="conv_fwd", grid=(s // ts, c // tc),
        in_specs=[pl.BlockSpec((ts, tc), lambda i, j: (i, cb0 + j)),
                  pl.BlockSpec((SUBLANES, tc), lambda i, j: (jnp.maximum(i * hb - 1, 0), cb0 + j)),
                  pl.BlockSpec((CONV_WIDTH, tc), lambda i, j: (0, j)),
                  pl.BlockSpec((1, tc), lambda i, j: (0, j))],
        out_specs=pl.BlockSpec((ts, tc), lambda i, j: (i, j)),
        out_shape=jax.ShapeDtypeStruct((s, c), F32),
        compiler_params=_params("parallel", "parallel"),
    )(proj, proj, conv_w, conv_b)


def _conv_bwd(proj, conv_w, conv_b, dacts, col0, dproj):
    s = proj.shape[0]
    c = conv_w.shape[1]
    widths = [d.shape[1] for d in dacts]
    assert sum(widths) == c
    ts = _pick(s, (512, 256, 128))
    tc = _pick(math.gcd(math.gcd(c, col0), math.gcd(*widths)), (512, 256, 128))
    cb0 = col0 // tc
    hb = ts // SUBLANES
    nblk = s // ts
    ext = ts + SUBLANES
    nb = [wd // tc for wd in widths]
    starts = [0, nb[0], nb[0] + nb[1]]

    def body(x_ref, xp_ref, xn_ref, d0, d1, d2, n0, n1, n2, w_ref, b_ref, _, dx_ref, dw_ref, db_ref):
        j = pl.program_id(0)
        i = pl.program_id(1)
        last = i == nblk - 1
        w = w_ref[...]
        halo = jnp.where(i > 0, xp_ref[...], 0.0)
        x = x_ref[...]
        xe = jnp.concatenate([x, xn_ref[...]], axis=0)
        dcur = jnp.where(j < starts[1], d0[...], jnp.where(j < starts[2], d1[...], d2[...]))
        dnext = jnp.where(j < starts[1], n0[...], jnp.where(j < starts[2], n1[...], n2[...]))
        de = jnp.concatenate([dcur, jnp.where(last, 0.0, dnext)], axis=0)
        u = _conv_out(xe, halo, w, b_ref[...])
        sg = _sigmoid(u)
        dpre = de * (sg * (1.0 + u * (1.0 - sg)))
        dx = w[CONV_WIDTH - 1:CONV_WIDTH] * dpre[0:ts]
        for kk in range(CONV_WIDTH - 1):
            sh = CONV_WIDTH - 1 - kk
            dx = dx + w[kk:kk + 1] * pltpu.roll(dpre, ext - sh, axis=0)[0:ts]
        dx_ref[...] = dx.astype(BF16)
        dcur = dpre[0:ts]

        @pl.when(i == 0)
        def _():
            dw_ref[...] = jnp.zeros_like(dw_ref)
            db_ref[...] = jnp.zeros_like(db_ref)

        db_ref[...] += jnp.sum(dcur, axis=0, keepdims=True)
        dw_ref[CONV_WIDTH - 1:CONV_WIDTH, :] += jnp.sum(dcur * x, axis=0, keepdims=True)
        for kk in range(CONV_WIDTH - 1):
            xs = _shift_rows(x, halo, CONV_WIDTH - 1 - kk)
            dw_ref[kk:kk + 1, :] += jnp.sum(dcur * xs, axis=0, keepdims=True)

    cur_p = pl.BlockSpec((ts, tc), lambda j, i: (i, cb0 + j))
    prev_p = pl.BlockSpec((SUBLANES, tc), lambda j, i: (jnp.maximum(i * hb - 1, 0), cb0 + j))
    nxt = lambda i: jnp.minimum((i + 1) * hb, nblk * hb - 1)
    next_p = pl.BlockSpec((SUBLANES, tc), lambda j, i: (nxt(i), cb0 + j))

    def part(q):
        return lambda j: jnp.clip(j - starts[q], 0, nb[q] - 1)

    cur_d = [pl.BlockSpec((ts, tc), lambda j, i, f=part(q): (i, f(j))) for q in range(3)]
    next_d = [pl.BlockSpec((SUBLANES, tc), lambda j, i, f=part(q): (nxt(i), f(j))) for q in range(3)]
    vec4 = pl.BlockSpec((CONV_WIDTH, tc), lambda j, i: (0, j))
    vec1 = pl.BlockSpec((1, tc), lambda j, i: (0, j))
    return pl.pallas_call(
        body, name="conv_bwd", grid=(c // tc, nblk),
        in_specs=[cur_p, prev_p, next_p, *cur_d, *next_d, vec4, vec1, pl.BlockSpec(memory_space=pl.ANY)],
        out_specs=(cur_p, vec4, vec1),
        out_shape=(jax.ShapeDtypeStruct(dproj.shape, dproj.dtype), jax.ShapeDtypeStruct((CONV_WIDTH, c), F32),
                   jax.ShapeDtypeStruct((1, c), F32)),
        input_output_aliases={11: 0},
        compiler_params=_params("parallel", "arbitrary"),
    )(proj, proj, proj, *dacts, *dacts, conv_w, conv_b, dproj)


def _dt_fwd(proj, dt_bias, col0):
    s = proj.shape[0]
    h = dt_bias.shape[1]
    ts = _pick(s, (1024, 512, 256, 128))

    def body(x_ref, b_ref, o_ref):
        v = x_ref[...] + b_ref[...]
        o_ref[...] = jnp.maximum(v, 0.0) + jnp.log1p(jnp.exp(-jnp.abs(v)))

    return pl.pallas_call(
        body, name="dt_fwd", grid=(s // ts,),
        in_specs=[pl.BlockSpec((ts, h), lambda i: (i, col0 // h)), pl.BlockSpec((1, h), lambda i: (0, 0))],
        out_specs=pl.BlockSpec((ts, h), lambda i: (i, 0)), out_shape=jax.ShapeDtypeStruct((s, h), F32),
        compiler_params=_params("parallel"),
    )(proj, dt_bias)


def _dt_bwd(proj, dt_bias, ddt, col0, dproj):
    s = proj.shape[0]
    h = dt_bias.shape[1]
    ts = _pick(s, (1024, 512, 256, 128))

    def body(x_ref, b_ref, d_ref, _, o_ref, db_ref):
        i = pl.program_id(0)
        draw = d_ref[...] * _sigmoid(x_ref[...] + b_ref[...])
        o_ref[...] = draw.astype(BF16)

        @pl.when(i == 0)
        def _():
            db_ref[...] = jnp.zeros_like(db_ref)

        db_ref[...] += jnp.sum(draw, axis=0, keepdims=True)

    return pl.pallas_call(
        body, name="dt_bwd", grid=(s // ts,),
        in_specs=[pl.BlockSpec((ts, h), lambda i: (i, col0 // h)), pl.BlockSpec((1, h), lambda i: (0, 0)),
                  pl.BlockSpec((ts, h), lambda i: (i, 0)), pl.BlockSpec(memory_space=pl.ANY)],
        out_specs=(pl.BlockSpec((ts, h), lambda i: (i, col0 // h)), pl.BlockSpec((1, h), lambda i: (0, 0))),
        out_shape=(jax.ShapeDtypeStruct(dproj.shape, dproj.dtype), jax.ShapeDtypeStruct((1, h), F32)),
        input_output_aliases={3: 0},
        compiler_params=_params("arbitrary"),
    )(proj, dt_bias, ddt, dproj)


def _chunk_terms(dt, dt_t, a, a_t):
    li = lax.broadcasted_iota(jnp.int32, (CHUNK, CHUNK), 0)
    si = lax.broadcasted_iota(jnp.int32, (CHUNK, CHUNK), 1)
    lower = (li >= si).astype(F32)
    upper = (li <= si).astype(F32)
    acum = jnp.dot(lower, dt * a, preferred_element_type=F32, precision=HIGHEST)
    acum_t = jnp.dot(dt_t * a_t, upper, preferred_element_type=F32, precision=HIGHEST)
    return acum, acum_t, li, si, upper


def _dot_exact01(t, m01):
    r = t.shape[0]
    hi = t.astype(BF16)
    rest = t - hi.astype(F32)
    mid = rest.astype(BF16)
    lo = (rest - mid.astype(F32)).astype(BF16)
    out = _dot(jnp.concatenate([hi, mid, lo], axis=0), m01.astype(BF16))
    return out[0:r] + out[r:2 * r] + out[2 * r:3 * r]


def _head_lanes(dt, acum, gw):
    hpg = dt.shape[1]
    p = gw // hpg
    spread = (lax.broadcasted_iota(jnp.int32, (hpg, gw), 1) // p
              == lax.broadcasted_iota(jnp.int32, (hpg, gw), 0)).astype(F32)
    both = _dot_exact01(jnp.concatenate([dt, acum], axis=0), spread)
    dt_e, acum_e = both[0:CHUNK], both[CHUNK:2 * CHUNK]
    alast_e = acum_e[CHUNK - 1:CHUNK, :]
    return dt_e, jnp.exp(acum_e), jnp.exp(alast_e - acum_e), jnp.exp(alast_e)


def _fold_heads(t, hpg):
    gw = t.shape[1]
    p = gw // hpg
    fold = (lax.broadcasted_iota(jnp.int32, (gw, hpg), 0) // p
            == lax.broadcasted_iota(jnp.int32, (gw, hpg), 1)).astype(F32)
    return _dot_exact01(t, fold)


def _ssd_fwd(xbc, proj, dt_g, dt_gt, a_g, a_gt, dskip_e, norm_w, d_inner, n_state):
    s = xbc.shape[0]
    hpg = dt_g.shape[2]
    gw = d_inner // SSM_GROUPS
    p = gw // hpg
    nc = s // CHUNK
    n = n_state
    b0 = d_inner // n
    c0 = b0 + SSM_GROUPS
    per_tile = LANES // p

    def body(xs_ref, b_ref, c_ref, dt_ref, dtt_ref, a_ref, at_ref, z_ref, dsk_ref, nw_ref,
             yn_ref, y_ref, st_ref, state):
        c = pl.program_id(1)

        @pl.when(c == 0)
        def _():
            state[...] = jnp.zeros_like(state)

        st = state[...]
        st_ref[...] = st
        xs = xs_ref[...]
        bm = b_ref[...].astype(BF16)
        cm = c_ref[...].astype(BF16)
        dt = dt_ref[...]
        acum, acum_t, li, si, _ = _chunk_terms(dt, dtt_ref[...], a_ref[...], at_ref[...])
        dt_e, e_a, t_e, e_last = _head_lanes(dt, acum, gw)
        xdt = xs * dt_e
        xdtb = xdt.astype(BF16)
        cb = _dot_nt(cm, bm)
        causal = li >= si
        lane = lax.broadcasted_iota(jnp.int32, (1, LANES), 1)
        y_ref[...] = _dot(cm, st.astype(BF16)) * e_a
        for q in range(gw // LANES):
            ql = slice(q * LANES, (q + 1) * LANES)
            xq = xdtb[:, ql]
            ms = []
            for i in range(per_tile):
                h = q * per_tile + i
                decay = jnp.exp(jnp.where(causal, acum[:, h:h + 1] - acum_t[h:h + 1, :], NEG_INF))
                ms.append((cb * decay).astype(BF16))
            y_all = _dot(jnp.concatenate(ms, axis=0), xq)
            yd = y_all[0:CHUNK]
            for i in range(1, per_tile):
                yd = jnp.where(lane >= i * p, y_all[i * CHUNK:(i + 1) * CHUNK], yd)
            y_ref[:, ql] += yd
        state[...] = st * e_last + _dot_tn(bm, (xdt * t_e).astype(BF16))
        yt = y_ref[...] + xs * dsk_ref[...]
        z = z_ref[...]
        yz = yt * (z * _sigmoid(z))
        r = lax.rsqrt(jnp.mean(yz * yz, axis=-1, keepdims=True) + RMS_EPS)
        yn_ref[...] = (yz * r * nw_ref[...]).astype(BF16)

    wide = pl.BlockSpec((CHUNK, gw), lambda g, c: (c, g))
    return pl.pallas_call(
        body, name="ssd_fwd", grid=(SSM_GROUPS, nc),
        in_specs=[wide,
                  pl.BlockSpec((CHUNK, n), lambda g, c: (c, b0 + g)),
                  pl.BlockSpec((CHUNK, n), lambda g, c: (c, c0 + g)),
                  pl.BlockSpec((None, CHUNK, hpg), lambda g, c: (g, c, 0)),
                  pl.BlockSpec((None, hpg, CHUNK), lambda g, c: (g, 0, c)),
                  pl.BlockSpec((None, 1, hpg), lambda g, c: (g, 0, 0)),
                  pl.BlockSpec((None, hpg, 1), lambda g, c: (g, 0, 0)),
                  wide,
                  pl.BlockSpec((None, 1, gw), lambda g, c: (g, 0, 0)),
                  pl.BlockSpec((1, gw), lambda g, c: (0, g))],
        out_specs=(wide, wide, pl.BlockSpec((None, None, n, gw), lambda g, c: (g, c, 0, 0))),
        out_shape=(jax.ShapeDtypeStruct((s, d_inner), BF16), jax.ShapeDtypeStruct((s, d_inner), F32),
                   jax.ShapeDtypeStruct((SSM_GROUPS, nc, n, gw), F32)),
        scratch_shapes=[pltpu.VMEM((n, gw), F32)],
        compiler_params=_params("parallel", "arbitrary"),
    )(xbc, xbc, xbc, dt_g, dt_gt, a_g, a_gt, proj, dskip_e, norm_w)


def _ssd_epilogue_bwd(dyn, y, xbc, proj, dskip_e, norm_w, hpg):
    s, d_inner = dyn.shape
    gw = d_inner // SSM_GROUPS
    p = gw // hpg
    nc = s // CHUNK

    def body(dyn_ref, y_ref, xs_ref, z_ref, dsk_ref, nw_ref, dy_ref, dz_ref, dnw_ref, ddsk_ref):
        c = pl.program_id(1)
        xs = xs_ref[...]
        z = z_ref[...]
        yt = y_ref[...] + xs * dsk_ref[...]
        sg = _sigmoid(z)
        sz = z * sg
        yz = yt * sz
        r = lax.rsqrt(jnp.mean(yz * yz, axis=-1, keepdims=True) + RMS_EPS)
        dynv = dyn_ref[...]
        dyh = dynv * nw_ref[...]
        dyz = r * (dyh - yz * (r * r) * jnp.mean(dyh * yz, axis=-1, keepdims=True))
        dyt = dyz * sz
        dy_ref[...] = dyt
        dz_ref[...] = (dyz * yt * (sg * (1.0 + z * (1.0 - sg)))).astype(BF16)

        @pl.when(c == 0)
        def _():
            dnw_ref[...] = jnp.zeros_like(dnw_ref)
            ddsk_ref[...] = jnp.zeros_like(ddsk_ref)

        dnw_ref[...] += jnp.sum(dynv * yz * r, axis=0, keepdims=True)
        colsum = jnp.sum(dyt * xs, axis=0, keepdims=True)
        fold = (lax.broadcasted_iota(jnp.int32, (gw, hpg), 0) // p
                == lax.broadcasted_iota(jnp.int32, (gw, hpg), 1)).astype(F32)
        ddsk_ref[...] += jnp.dot(colsum, fold, preferred_element_type=F32, precision=HIGHEST)

    wide = pl.BlockSpec((CHUNK, gw), lambda g, c: (c, g))
    return pl.pallas_call(
        body, name="ssd_epilogue_bwd", grid=(SSM_GROUPS, nc),
        in_specs=[wide, wide, wide, wide, pl.BlockSpec((None, 1, gw), lambda g, c: (g, 0, 0)),
                  pl.BlockSpec((1, gw), lambda g, c: (0, g))],
        out_specs=(wide, wide, pl.BlockSpec((1, gw), lambda g, c: (0, g)),
                   pl.BlockSpec((None, 1, hpg), lambda g, c: (g, 0, 0))),
        out_shape=(jax.ShapeDtypeStruct((s, d_inner), F32), jax.ShapeDtypeStruct((s, proj.shape[1]), BF16),
                   jax.ShapeDtypeStruct((1, d_inner), F32), jax.ShapeDtypeStruct((SSM_GROUPS, 1, hpg), F32)),
        compiler_params=_params("parallel", "arbitrary"),
    )(dyn, y, xbc, proj, dskip_e, norm_w)


def _ssd_scan_bwd(xbc, dt_g, dt_gt, a_g, a_gt, states, dy, dskip_e, d_inner, n_state, ride=None):
    s = xbc.shape[0]
    hpg = dt_g.shape[2]
    gw = d_inner // SSM_GROUPS
    p = gw // hpg
    nc = s // CHUNK
    n = n_state
    b0 = d_inner // n
    c0 = b0 + SSM_GROUPS
    per_tile = LANES // p

    def body(xs_ref, b_ref, c_ref, dt_ref, dtt_ref, a_ref, at_ref, st_ref, dy_ref, dsk_ref,
             dxs_ref, db_ref, dc_ref, ddt_ref, da_ref, dstate, ydiag_ref, dxd_ref):
        c = pl.program_id(1)

        @pl.when(c == 0)
        def _():
            dstate[...] = jnp.zeros_like(dstate)
            da_ref[...] = jnp.zeros_like(da_ref)

        xs = xs_ref[...]
        bm = b_ref[...].astype(BF16)
        cm = c_ref[...].astype(BF16)
        dt = dt_ref[...]
        a = a_ref[...]
        dyv = dy_ref[...]
        dsk = dsk_ref[...]
        acum, acum_t, li, si, upper = _chunk_terms(dt, dtt_ref[...], a, at_ref[...])
        dt_e, e_a, t_e, e_last = _head_lanes(dt, acum, gw)
        cb = _dot_nt(cm, bm)
        lower_mask = li >= si
        lane = lax.broadcasted_iota(jnp.int32, (1, LANES), 1)
        row_l = lax.broadcasted_iota(jnp.int32, (CHUNK, 1), 0)
        st = st_ref[...]
        stb = st.astype(BF16)
        dst = dstate[...]
        dstb = dst.astype(BF16)
        xdt = xs * dt_e
        xdtb = xdt.astype(BF16)
        dyb = dyv.astype(BF16)
        dye = dyv * e_a
        dyeb = dye.astype(BF16)
        xte = xdt * t_e
        xteb = xte.astype(BF16)
        wv = _dot(bm, dstb)
        yo = _dot(cm, stb)
        dcb = jnp.zeros((CHUNK, CHUNK), F32)
        for q in range(gw // LANES):
            ql = slice(q * LANES, (q + 1) * LANES)
            xq = xdtb[:, ql]
            dq = dyb[:, ql]
            decays, ms, mts, dqs = [], [], [], []
            for i in range(per_tile):
                h = q * per_tile + i
                decay = jnp.exp(jnp.where(lower_mask, acum[:, h:h + 1] - acum_t[h:h + 1, :], NEG_INF))
                mm = cb * decay
                mine = (lane >= i * p) & (lane < (i + 1) * p)
                decays.append(decay)
                ms.append(mm.astype(BF16))
                mts.append(mm.T.astype(BF16))
                dqs.append(jnp.where(mine, dq, jnp.zeros_like(dq)))
            dm_all = _dot_nt(jnp.concatenate(dqs, axis=0), xq)
            y_all = _dot(jnp.concatenate(ms, axis=0), xq)
            d_all = _dot(jnp.concatenate(mts, axis=0), dq)
            yd = dd = None
            for i in range(per_tile):
                rows = slice(i * CHUNK, (i + 1) * CHUNK)
                dcb = dcb + dm_all[rows] * decays[i]
                yd = y_all[rows] if i == 0 else jnp.where(lane >= i * p, y_all[rows], yd)
                dd = d_all[rows] if i == 0 else jnp.where(lane >= i * p, d_all[rows], dd)
            ydiag_ref[:, ql] = yd
            dxd_ref[:, ql] = dd
        ydiag = ydiag_ref[...]
        dxd = dxd_ref[...]
        dxdt = dxd + t_e * wv
        xw = xte * wv
        last_in = jnp.sum(xw, axis=0, keepdims=True) + e_last * jnp.sum(dst * st, axis=0, keepdims=True)
        folded = _fold_heads(jnp.concatenate(
            [dyb.astype(F32) * ydiag - xdtb.astype(F32) * dxd - xw + dye * yo, dxdt * xs,
             jnp.broadcast_to(last_in, (SUBLANES, gw))],
            axis=0), hpg)
        dalast = folded[2 * CHUNK:2 * CHUNK + 1]
        d_acum = folded[0:CHUNK] + jnp.where(row_l == CHUNK - 1, dalast, 0.0)
        ddt_x = folded[CHUNK:2 * CHUNK]
        dxs_ref[...] = dxdt * dt_e + dyv * dsk
        dbf = dcb.astype(BF16)
        dc_ref[...] = _dot_nt(dyeb, stb) + _dot(dbf, bm)
        db_ref[...] = _dot_nt(xteb, dstb) + _dot_tn(dbf, cm)
        dstate[...] = dst * e_last + _dot_tn(cm, dyeb)
        d_da = jnp.dot(upper, d_acum, preferred_element_type=F32, precision=HIGHEST)
        ddt_ref[...] = d_da * a + ddt_x
        da_ref[...] += jnp.sum(d_da * dt, axis=0, keepdims=True)

    rev = lambda c: nc - 1 - c
    wide = pl.BlockSpec((CHUNK, gw), lambda g, c: (rev(c), g))
    return _call(
        body, name="ssd_scan_bwd", grid=(SSM_GROUPS, nc),
        in_specs=[wide,
                  pl.BlockSpec((CHUNK, n), lambda g, c: (rev(c), b0 + g)),
                  pl.BlockSpec((CHUNK, n), lambda g, c: (rev(c), c0 + g)),
                  pl.BlockSpec((None, CHUNK, hpg), lambda g, c: (g, rev(c), 0)),
                  pl.BlockSpec((None, hpg, CHUNK), lambda g, c: (g, 0, rev(c))),
                  pl.BlockSpec((None, 1, hpg), lambda g, c: (g, 0, 0)),
                  pl.BlockSpec((None, hpg, 1), lambda g, c: (g, 0, 0)),
                  pl.BlockSpec((None, None, n, gw), lambda g, c: (g, rev(c), 0, 0)),
                  wide,
                  pl.BlockSpec((None, 1, gw), lambda g, c: (g, 0, 0))],
        out_specs=[wide,
                   pl.BlockSpec((CHUNK, n), lambda g, c: (rev(c), g)),
                   pl.BlockSpec((CHUNK, n), lambda g, c: (rev(c), g)),
                   pl.BlockSpec((None, CHUNK, hpg), lambda g, c: (g, rev(c), 0)),
                   pl.BlockSpec((None, 1, hpg), lambda g, c: (g, 0, 0))],
        out_shape=[jax.ShapeDtypeStruct((s, d_inner), F32),
                   jax.ShapeDtypeStruct((s, SSM_GROUPS * n), F32), jax.ShapeDtypeStruct((s, SSM_GROUPS * n), F32),
                   jax.ShapeDtypeStruct((SSM_GROUPS, s, hpg), F32), jax.ShapeDtypeStruct((SSM_GROUPS, 1, hpg), F32)],
        scratch=[pltpu.VMEM((n, gw), F32), pltpu.VMEM((CHUNK, gw), F32), pltpu.VMEM((CHUNK, gw), F32)],
        sem=("parallel", "arbitrary"), ride=ride,
        args=(xbc, xbc, xbc, dt_g, dt_gt, a_g, a_gt, states, dy, dskip_e))


def _lin(p):
    return 4 * p[0] + 2 * p[1] + p[2]


class _Gather:
    def __init__(self, arrs):
        self.arrs = list(arrs)

    def out_shape(self):
        return [jax.ShapeDtypeStruct((NDEV,) + a.shape, a.dtype) for a in self.arrs]

    def _copies(self, ins, outs, sems):
        send_sems, recv_sems, local_sems = sems
        x, y, c = lax.axis_index("x"), lax.axis_index("y"), lax.axis_index("c")
        me, sibling = (x, y, c), (x, y, 1 - c)
        chips = [(1 - x, y), (x, 1 - y), (1 - x, 1 - y)]

        def copy(a, k, block, to, src=None):
            rows = outs[a].at[_lin(block)]
            return pltpu.make_async_remote_copy(
                src_ref=rows if src is None else src, dst_ref=rows,
                send_sem=send_sems.at[a * NPEER + k], recv_sem=recv_sems.at[a * NPEER + k],
                device_id=to, device_id_type=pl.DeviceIdType.MESH)

        na = len(ins)
        mine = [pltpu.make_async_copy(ins[a], outs[a].at[_lin(me)], local_sems.at[a]) for a in range(na)]
        first = []
        for a in range(na):
            first.append(copy(a, 0, me, sibling, src=ins[a]))
            first += [copy(a, 1 + j, me, (*chip, c), src=ins[a]) for j, chip in enumerate(chips)]
        return copy, mine, first, me, sibling, chips, c, na

    def start(self, ins, outs, sems):
        _, mine, first, *_ = self._copies(ins, outs, sems)
        for cp in mine + first:
            cp.start()

    def finish(self, ins, outs, sems):
        copy, mine, first, me, sibling, chips, c, na = self._copies(ins, outs, sems)
        passed = []
        for j, chip in enumerate(chips):
            for a in range(na):
                copy(a, 1 + j, (*chip, c), me).wait_recv()
                cp = copy(a, 4 + j, (*chip, c), sibling)
                cp.start()
                passed.append(cp)
        for a in range(na):
            copy(a, 0, sibling, me).wait_recv()
            for j, chip in enumerate(chips):
                copy(a, 4 + j, (*chip, 1 - c), me).wait_recv()
        for cp in first + passed:
            cp.wait_send()
        for cp in mine:
            cp.wait()


class _Scatter:
    def __init__(self, arrs, ks=tuple(range(NDEV))):
        self.arrs = list(arrs)
        self.ks = tuple(ks)

    def out_shape(self):
        return [jax.ShapeDtypeStruct((len(self.ks),) + a.shape[1:], a.dtype) for a in self.arrs]

    def _copies(self, ins, outs, sems):
        send_sems, recv_sems, local_sems = sems
        x, y, c = lax.axis_index("x"), lax.axis_index("y"), lax.axis_index("c")
        me = (x, y, c)

        def peer(k):
            return (1 - x if k & 4 else x, 1 - y if k & 2 else y, 1 - c if k & 1 else c)

        local, remote = [], []
        for a in range(len(ins)):
            for i, k in enumerate(self.ks):
                if k == 0:
                    local.append(pltpu.make_async_copy(ins[a].at[_lin(me)], outs[a].at[i], local_sems.at[a]))
                else:
                    remote.append(pltpu.make_async_remote_copy(
                        src_ref=ins[a].at[_lin(peer(k))], dst_ref=outs[a].at[i],
                        send_sem=send_sems.at[a * NPEER + k - 1], recv_sem=recv_sems.at[a * NPEER + k - 1],
                        device_id=peer(k), device_id_type=pl.DeviceIdType.MESH))
        return local, remote

    def start(self, ins, outs, sems):
        local, remote = self._copies(ins, outs, sems)
        for cp in local + remote:
            cp.start()

    def finish(self, ins, outs, sems):
        local, remote = self._copies(ins, outs, sems)
        for cp in remote:
            cp.wait_recv()
        for cp in remote:
            cp.wait_send()
        for cp in local:
            cp.wait()


def _exchange_scratch(na):
    return [pltpu.SemaphoreType.DMA((na * NPEER,)), pltpu.SemaphoreType.DMA((na * NPEER,)),
            pltpu.SemaphoreType.DMA((na,))]


def _exchange_alone(ex, *, name, in_vmem=False):
    na = len(ex.arrs)

    def body(*refs):
        ins, outs, sems = refs[:na], refs[na:2 * na], refs[2 * na:]
        ex.start(ins, outs, sems)
        ex.finish(ins, outs, sems)

    spec = pl.BlockSpec(memory_space=pltpu.VMEM if in_vmem else pl.ANY)
    return pl.pallas_call(
        body, name=name, out_shape=tuple(ex.out_shape()), in_specs=[spec] * na, out_specs=tuple([spec] * na),
        scratch_shapes=_exchange_scratch(na),
        compiler_params=pltpu.CompilerParams(vmem_limit_bytes=VMEM_LIMIT),
    )(*ex.arrs)


def _call(body, *, name, grid, in_specs, out_specs, out_shape, args, sem, scratch=(), ride=None, aliases=None):
    n_in, n_out, n_scr = len(in_specs), len(out_specs), len(scratch)
    if ride is None:
        outs = pl.pallas_call(
            body, name=name, grid=grid, in_specs=list(in_specs), out_specs=tuple(out_specs),
            out_shape=tuple(out_shape), scratch_shapes=list(scratch), input_output_aliases=aliases or {},
            compiler_params=_params(*sem))(*args)
        return tuple(outs), ()
    nx = len(ride.arrs)
    hbm = pl.BlockSpec(memory_space=pl.ANY)

    def hosted(*refs):
        ins, x_in = refs[:n_in], refs[n_in:n_in + nx]
        o0 = n_in + nx
        outs, x_out = refs[o0:o0 + n_out], refs[o0 + n_out:o0 + n_out + nx]
        s0 = o0 + n_out + nx
        scr, x_sem = refs[s0:s0 + n_scr], refs[s0 + n_scr:]
        ids = [pl.program_id(i) for i in range(len(grid))]
        first = functools.reduce(jnp.logical_and, [i == 0 for i in ids])
        last = functools.reduce(jnp.logical_and, [i == g - 1 for i, g in zip(ids, grid)])

        @pl.when(first)
        def _():
            ride.start(x_in, x_out, x_sem)

        body(*ins, *outs, *scr)

        @pl.when(last)
        def _():
            ride.finish(x_in, x_out, x_sem)

    outs = pl.pallas_call(
        hosted, name=name, grid=grid, in_specs=list(in_specs) + [hbm] * nx,
        out_specs=tuple(list(out_specs) + [hbm] * nx), out_shape=tuple(list(out_shape) + ride.out_shape()),
        scratch_shapes=list(scratch) + _exchange_scratch(nx), input_output_aliases=aliases or {},
        compiler_params=_params(*(("arbitrary",) * len(grid))))(*args, *ride.arrs)
    return tuple(outs[:n_out]), tuple(outs[n_out:])


def _pack(parts):
    flat = jnp.concatenate([p.reshape(-1).astype(F32) for p in parts])
    tile = SUBLANES * LANES
    pad = (-flat.shape[0]) % tile
    return jnp.pad(flat, (0, pad)).reshape(-1, LANES)


def _unpack(buf, shapes):
    flat = buf.reshape(-1)
    out, off = [], 0
    for shp in shapes:
        size = math.prod(shp)
        out.append(flat[off:off + size].reshape(shp))
        off += size
    return out


KS_NEAR = (0, 1, 4, 5)
KS_FAR = (2, 3, 6, 7)


def _local_step(x, target, wa, wo, ws, wos, rel_bias, conv_w, conv_b, dt_bias, a_log, d_skip, norm_w, ln_g, ln_b,
                dist=False):
    s, d = x.shape
    da = wo.shape[-2]
    heads = da // HEAD_DIM
    qkv_cols = 3 * N_GROUPS_ATTN * da
    d_inner = wos.shape[0] * (NDEV if dist else 1)
    conv_dim = conv_w.shape[1]
    ssm_heads = dt_bias.shape[1]
    hpg = ssm_heads // SSM_GROUPS
    gn = (conv_dim - d_inner) // 2
    n_state = gn // SSM_GROUPS
    gw = d_inner // SSM_GROUPS
    p = gw // hpg
    in_ssm = d_inner + conv_dim + ssm_heads
    xb = _cast_bf16(x, name="cast_x")

    def slabs_of_cols(t):
        return t.reshape(t.shape[0], NDEV, t.shape[1] // NDEV).transpose(1, 0, 2)

    if dist:
        qkv, (wo, ws_slabs) = _mm(xb, wa, name="mm_qkv", out_dtype=BF16, n_out=qkv_cols, ride=_Gather([wo, ws]))
        ws = ws_slabs.transpose(1, 0, 2).reshape(d, in_ssm)
    else:
        qkv = _mm(xb, wa, name="mm_qkv", out_dtype=BF16, n_out=qkv_cols)
    gate = _mm(xb, wa, name="mm_gate", out_dtype=F32, n_off=qkv_cols, n_out=da)
    bias, bucket = _bias_tables(rel_bias, heads)
    os_, ls_ = [], []
    for g, (_, dil) in enumerate(ATTN_PATTERNS):
        o, l = _attn_fwd_group(qkv, bias[g], g, dil, da)
        os_.append(o)
        ls_.append(l)
    o, lse, y = _attn_combine(os_, ls_, gate)
    h1 = _mm(y, wo, name="mm_out_attn", out_dtype=F32)
    x1, x1b = _ln_fwd(x, h1, ln_g[0:1], ln_b[0:1], name="ln1_fwd")

    if dist:
        proj, (wos_slabs,) = _mm(x1b, ws, name="mm_in_ssm", out_dtype=F32, ride=_Gather([wos]))
        wos = wos_slabs.reshape(d_inner, d)
    else:
        proj = _mm(x1b, ws, name="mm_in_ssm", out_dtype=F32)
    xbc = _conv_fwd(proj, conv_w, conv_b, d_inner)
    dt = _dt_fwd(proj, dt_bias, d_inner + conv_dim)
    dt_g = dt.reshape(s, SSM_GROUPS, hpg).transpose(1, 0, 2)
    dt_gt = dt.reshape(s, SSM_GROUPS, hpg).transpose(1, 2, 0)
    a = -jnp.exp(a_log)
    a_g = a.reshape(SSM_GROUPS, 1, hpg)
    a_gt = a.reshape(SSM_GROUPS, hpg, 1)
    dskip_e = jnp.repeat(d_skip.reshape(SSM_GROUPS, 1, hpg), p, axis=2)
    yn, yscan, states = _ssd_fwd(xbc, proj, dt_g, dt_gt, a_g, a_gt, dskip_e, norm_w, d_inner, n_state)
    h2 = _mm(yn, wos, name="mm_out_ssm", out_dtype=F32)

    du2, du2b, dg1, db1, loss_t = _ln_bwd(x1, h2, ln_g[1:2], ln_b[1:2], target, with_loss=True, name="ln2_loss_bwd")
    loss = loss_t[0, 0]
    dyn = _mm(du2b, wos, name="mm_dyn", out_dtype=F32, trans_b=True)
    g_wos = _mm(yn.T, du2b, name="mm_dw_out_ssm", out_dtype=BF16)
    parts = {}
    dyscan, dproj_ssm, g_norm, g_dskip = _ssd_epilogue_bwd(dyn, yscan, xbc, proj, dskip_e, norm_w, hpg)
    ride = _Scatter([g_wos.reshape(NDEV, d_inner // NDEV, d)]) if dist else None
    (dxs, d_bm, d_cm, ddt_g, g_a), rode = _ssd_scan_bwd(xbc, dt_g, dt_gt, a_g, a_gt, states, dyscan, dskip_e,
                                                         d_inner, n_state, ride=ride)
    parts["w_out_ssm"] = list(rode)
    g_alog = g_a.reshape(1, ssm_heads) * a
    dproj_ssm, g_conv_w, g_conv_b = _conv_bwd(proj, conv_w, conv_b, (dxs, d_bm, d_cm), d_inner, dproj_ssm)
    ddt = ddt_g.transpose(1, 0, 2).reshape(s, ssm_heads)
    dproj_ssm, g_dtb = _dt_bwd(proj, dt_bias, ddt, d_inner + conv_dim, dproj_ssm)
    g_ws = _mm(x1b.T, dproj_ssm, name="mm_dw_in_ssm", out_dtype=BF16)
    if dist:
        g_ws_slabs = slabs_of_cols(g_ws)
        dx1, near = _mm(dproj_ssm, ws, name="mm_dx1", out_dtype=F32, trans_b=True, res=du2,
                        res_scale=DEEPNORM_ALPHA, ride=_Scatter([g_ws_slabs], KS_NEAR))
    else:
        dx1 = _mm(dproj_ssm, ws, name="mm_dx1", out_dtype=F32, trans_b=True, res=du2, res_scale=DEEPNORM_ALPHA)

    du1, du1b, dg0, db0 = _ln_bwd(x, h1, ln_g[0:1], ln_b[0:1], dx1, with_loss=False, name="ln1_bwd")
    dy = _mm(du1b, wo, name="mm_dy", out_dtype=F32, trans_b=True)
    g_wo = _mm(y.T, du1b, name="mm_dw_out_attn", out_dtype=BF16, slab_out=NDEV)
    do, dgate, dd = _attn_bwd_prep(dy, o, gate)
    dparts, dss = [], []
    for g, (_, dil) in enumerate(ATTN_PATTERNS):
        dq, dk, dv, ds = _attn_bwd_group(qkv, do, lse, dd, bias[g], g, dil, da)
        dparts += [dq, dk, dv]
        dss.append(ds)
    g_bias = _bias_bwd(jnp.stack(dss), bucket)
    g_rel_bias = g_bias.transpose(2, 0, 1).reshape(NUM_BUCKETS, N_GROUPS_ATTN * heads)
    dproj_attn = jnp.concatenate(dparts + [dgate], axis=1)
    if dist:
        g_wa, far = _mm(xb.T, dproj_attn, name="mm_dw_in_attn", out_dtype=BF16, slab_out=NDEV,
                        ride=_Scatter([g_ws_slabs], KS_FAR))
        parts["w_in_ssm"] = [near[0], far[0]]
        dx, rode = _mm(dproj_attn, wa, name="mm_dx", out_dtype=F32, trans_b=True, res=du1,
                       res_scale=DEEPNORM_ALPHA, ride=_Scatter([g_wa, g_wo]))
        parts["w_in_attn"], parts["w_out_attn"] = [rode[0]], [rode[1]]
    else:
        g_wa = _mm(xb.T, dproj_attn, name="mm_dw_in_attn", out_dtype=BF16, slab_out=NDEV)
        dx = _mm(dproj_attn, wa, name="mm_dx", out_dtype=F32, trans_b=True, res=du1, res_scale=DEEPNORM_ALPHA)

    g_ln_g = jnp.concatenate([dg0, dg1], axis=0)
    g_ln_b = jnp.concatenate([db0, db1], axis=0)
    small = dict(rel_bias=g_rel_bias, dt_bias=g_dtb, a_log=g_alog, d_skip=g_dskip.reshape(1, ssm_heads),
                 ln_g=g_ln_g, ln_b=g_ln_b, conv_w=g_conv_w, conv_b=g_conv_b, ssm_norm_w=g_norm)
    if dist:
        return loss, dx, parts, small
    return loss, dx, g_wa, g_wo, g_ws, g_wos, small


REPLICATED = ("rel_bias", "dt_bias", "a_log", "d_skip", "ln_g", "ln_b")
SHARDED_SMALL = ("conv_w", "conv_b", "ssm_norm_w")


def kernel(x, w_in_attn, w_out_attn, rel_bias, w_in_ssm, conv_w, conv_b, dt_bias, a_log, d_skip, ssm_norm_w, w_out_ssm, ln_g, ln_b, loss_target, m_w_in_attn, m_w_out_attn, m_rel_bias, m_w_in_ssm, m_conv_w, m_conv_b, m_dt_bias, m_a_log, m_d_skip, m_ssm_norm_w, m_w_out_ssm, m_ln_g, m_ln_b, v_w_in_attn, v_w_out_attn, v_rel_bias, v_w_in_ssm, v_conv_w, v_conv_b, v_dt_bias, v_a_log, v_d_skip, v_ssm_norm_w, v_w_out_ssm, v_ln_g, v_ln_b):
    w = dict(w_in_attn=w_in_attn, w_out_attn=w_out_attn, rel_bias=rel_bias, w_in_ssm=w_in_ssm, conv_w=conv_w,
             conv_b=conv_b, dt_bias=dt_bias, a_log=a_log, d_skip=d_skip, ssm_norm_w=ssm_norm_w,
             w_out_ssm=w_out_ssm, ln_g=ln_g, ln_b=ln_b)
    m = dict(w_in_attn=m_w_in_attn, w_out_attn=m_w_out_attn, rel_bias=m_rel_bias, w_in_ssm=m_w_in_ssm,
             conv_w=m_conv_w, conv_b=m_conv_b, dt_bias=m_dt_bias, a_log=m_a_log, d_skip=m_d_skip,
             ssm_norm_w=m_ssm_norm_w, w_out_ssm=m_w_out_ssm, ln_g=m_ln_g, ln_b=m_ln_b)
    v = dict(w_in_attn=v_w_in_attn, w_out_attn=v_w_out_attn, rel_bias=v_rel_bias, w_in_ssm=v_w_in_ssm,
             conv_w=v_conv_w, conv_b=v_conv_b, dt_bias=v_dt_bias, a_log=v_a_log, d_skip=v_d_skip,
             ssm_norm_w=v_ssm_norm_w, w_out_ssm=v_w_out_ssm, ln_g=v_ln_g, ln_b=v_ln_b)
    me = _lin((lax.axis_index("x"), lax.axis_index("y"), lax.axis_index("c")))
    d = x.shape[2]
    big = ("w_in_attn", "w_out_attn", "w_in_ssm", "w_out_ssm")

    shards = {k: _cast_bf16(w[k][0], name=f"cast_{k}") for k in big}
    (wa,) = _exchange_alone(_Gather([shards["w_in_attn"]]), name="gather_w_in_attn")
    cpd = conv_w.shape[2]
    npd = ssm_norm_w.shape[1]
    small_shapes = [(CONV_WIDTH, cpd), (1, cpd), (1, npd)]
    (small_all,) = _exchange_alone(_Gather([_pack([conv_w[0], conv_b, ssm_norm_w])]), name="gather_small_weights",
                                   in_vmem=True)
    small_parts = [_unpack(small_all[i], small_shapes) for i in range(NDEV)]
    conv_w_full = jnp.concatenate([p[0] for p in small_parts], axis=1)
    conv_b_full = jnp.concatenate([p[1] for p in small_parts], axis=1)
    norm_w_full = jnp.concatenate([p[2] for p in small_parts], axis=1)

    loss, dx, parts, small = _local_step(
        x[0], loss_target[0], wa, shards["w_out_attn"], shards["w_in_ssm"], shards["w_out_ssm"], rel_bias,
        conv_w_full, conv_b_full, dt_bias[0:1], a_log[0:1], d_skip[0:1], norm_w_full, ln_g, ln_b, dist=True)
    loss = lax.psum(loss, MESH_AXES)
    out = {}
    for k in big:
        out[k] = _adamw_sum(parts[k], w[k][0], m[k][0], v[k][0], name=f"adamw_{k}")

    order = REPLICATED + SHARDED_SMALL
    g_shapes = [small[k].shape for k in order]
    (g_all,) = _exchange_alone(_Gather([_pack([small[k] for k in order])]), name="gather_small_grads", in_vmem=True)
    g_sum = dict(zip(order, _unpack(_sum_slots(g_all, name="sum_small_grads"), g_shapes)))
    g_mine = {k: g_sum[k] for k in REPLICATED}
    g_mine["conv_w"] = lax.dynamic_slice_in_dim(g_sum["conv_w"], me * cpd, cpd, axis=1)
    g_mine["conv_b"] = lax.dynamic_slice_in_dim(g_sum["conv_b"], me * cpd, cpd, axis=1)
    g_mine["ssm_norm_w"] = lax.dynamic_slice_in_dim(g_sum["ssm_norm_w"], me * npd, npd, axis=1)
    w_shapes = [w[k].shape for k in order]
    g_pack = _pack([g_mine[k] for k in order])
    d_p, m_p, v_p = _adamw_small(g_pack, _pack([w[k] for k in order]), _pack([m[k] for k in order]),
                                 _pack([v[k] for k in order]), name="adamw_small")
    for k, gk, dk, mk, vk in zip(order, _unpack(g_pack, w_shapes), _unpack(d_p, w_shapes), _unpack(m_p, w_shapes),
                                 _unpack(v_p, w_shapes)):
        out[k] = (gk, dk, mk, vk)
    for k in big:
        out[k] = tuple(t[None] for t in out[k])

    names = ("w_in_attn", "w_out_attn", "rel_bias", "w_in_ssm", "conv_w", "conv_b", "dt_bias", "a_log", "d_skip",
             "ssm_norm_w", "w_out_ssm", "ln_g", "ln_b")
    res = [loss, dx[None]]
    for i in range(4):
        res += [out[k][i] for k in names]
    return tuple(res)
```

```python
import functools
import math

import jax
import jax.numpy as jnp
from jax import lax
from jax.experimental import pallas as pl
from jax.experimental.pallas import tpu as pltpu

F32 = jnp.float32
BF16 = jnp.bfloat16
MESH_AXES = ("x", "y", "c")
NDEV = 8
NPEER = NDEV - 1
LANES = 128
SUBLANES = 8
VMEM_LIMIT = 52 * 1024 * 1024
MM_VMEM_BUDGET = 40 * 1024 * 1024
MM_TK_MAX = 4096
MM_TN_MAX = 1024

ATTN_PATTERNS = ((128, 1), (512, 4), (2048, 16))
N_GROUPS_ATTN = 3
HEAD_DIM = 128
ATTN_BLOCK = 128
NUM_BUCKETS = 32
MAX_DISTANCE = 2048
SSM_GROUPS = 8
CONV_WIDTH = 4
CHUNK = 128
DEPTH = 2
DEEPNORM_ALPHA = (2 * DEPTH) ** 0.25
LN_EPS = 1e-5
RMS_EPS = 1e-5
NEG_INF = -1e30
ADAM_LR = 0.001
ADAM_B1 = 0.9
ADAM_B2 = 0.999
ADAM_EPS = 1e-08
ADAM_WD = 0.01
ADAM_STEP = 10
HIGHEST = lax.Precision.HIGHEST


def _params(*sem):
    return pltpu.CompilerParams(dimension_semantics=sem, vmem_limit_bytes=VMEM_LIMIT)


def _pick(n, prefs):
    for p in prefs:
        if n % p == 0:
            return p
    return n


def _dot(a, b):
    return jnp.dot(a, b, preferred_element_type=F32)


def _dot_nt(a, b):
    return lax.dot_general(a, b, (((1,), (1,)), ((), ())), preferred_element_type=F32)


def _dot_tn(a, b):
    return lax.dot_general(a, b, (((0,), (0,)), ((), ())), preferred_element_type=F32)


def _sigmoid(x):
    return 1.0 / (1.0 + jnp.exp(-x))


def _mm(a, b, *, name, out_dtype, trans_b=False, slab_out=0, n_off=0, n_out=None,
        res=None, res_scale=1.0, ride=None):
    m, k = a.shape
    slab_b = b.ndim == 3
    if slab_b:
        ns = b.shape[0]
        if trans_b:
            n, kper = b.shape[1], b.shape[2]
            assert ns * kper == k
        else:
            nper = b.shape[2]
            n = ns * nper
            assert b.shape[1] == k
    else:
        n = b.shape[0] if trans_b else b.shape[1]
        assert (b.shape[1] if trans_b else b.shape[0]) == k
    n_out = n if n_out is None else n_out
    tm = _pick(m, (1024, 512, 256, 128))
    nconstraint = math.gcd(n_out, n_off) if n_off else n_out
    if slab_b and not trans_b:
        nconstraint = math.gcd(nconstraint, nper)
    if slab_out:
        nconstraint = math.gcd(nconstraint, n_out // slab_out)
    kconstraint = kper if (slab_b and trans_b) else k
    tk = max(t for t in range(LANES, min(kconstraint, MM_TK_MAX) + 1, LANES) if kconstraint % t == 0)
    nk = k // tk
    out_bytes = jnp.dtype(out_dtype).itemsize

    def vmem_bytes(t):
        return (2 * 2 * tk * (tm + t) + 2 * tm * t * out_bytes + (4 * tm * t if nk > 1 else 0)
                + (2 * 4 * tm * t if res is not None else 0))

    fits = [t for t in range(LANES, min(nconstraint, MM_TN_MAX) + 1, LANES)
            if nconstraint % t == 0 and vmem_bytes(t) <= MM_VMEM_BUDGET]
    tn = max(fits)
    nb0 = n_off // tn
    grid = (m // tm, n_out // tn, nk)

    a_spec = pl.BlockSpec((tm, tk), lambda i, j, kk: (i, kk))
    if slab_b and not trans_b:
        nps = nper // tn
        b_spec = pl.BlockSpec((None, tk, tn), lambda i, j, kk: ((j + nb0) // nps, kk, (j + nb0) % nps))
    elif slab_b and trans_b:
        kps = kper // tk
        b_spec = pl.BlockSpec((None, tn, tk), lambda i, j, kk: (kk // kps, j + nb0, kk % kps))
    elif trans_b:
        b_spec = pl.BlockSpec((tn, tk), lambda i, j, kk: (j + nb0, kk))
    else:
        b_spec = pl.BlockSpec((tk, tn), lambda i, j, kk: (kk, j + nb0))
    if slab_out:
        ops = (n_out // slab_out) // tn
        o_spec = pl.BlockSpec((None, tm, tn), lambda i, j, kk: (j // ops, i, j % ops))
        o_shape = jax.ShapeDtypeStruct((slab_out, m, n_out // slab_out), out_dtype)
    else:
        o_spec = pl.BlockSpec((tm, tn), lambda i, j, kk: (i, j))
        o_shape = jax.ShapeDtypeStruct((m, n_out), out_dtype)
    in_specs = [a_spec, b_spec]
    args = [a, b]
    if res is not None:
        in_specs.append(pl.BlockSpec((tm, tn), lambda i, j, kk: (i, j)))
        args.append(res)

    def body(*refs):
        a_ref, b_ref = refs[0], refs[1]
        r_ref = refs[2] if res is not None else None
        o_ref = refs[3] if res is not None else refs[2]
        av = a_ref[...].astype(BF16)
        bv = b_ref[...].astype(BF16)
        part = _dot_nt(av, bv) if trans_b else _dot(av, bv)

        def finish(r):
            if res is not None:
                r = r + res_scale * r_ref[...]
            o_ref[...] = r.astype(out_dtype)

        if nk == 1:
            finish(part)
            return
        acc = refs[-1]
        kk = pl.program_id(2)

        @pl.when(kk == 0)
        def _():
            acc[...] = part

        @pl.when(kk > 0)
        def _():
            acc[...] += part

        @pl.when(kk == nk - 1)
        def _():
            finish(acc[...])

    outs, rode = _call(
        body, name=name, grid=grid, in_specs=in_specs, out_specs=[o_spec], out_shape=[o_shape], args=args,
        scratch=[pltpu.VMEM((tm, tn), F32)] if nk > 1 else [], ride=ride,
        sem=("parallel", "parallel", "arbitrary"))
    return (outs[0], rode) if ride is not None else outs[0]


def _cast_bf16(w, *, name):
    r, c = w.shape
    tr = _pick(r, (512, 256, 128, 64, 32, 16, 8))

    def body(w_ref, o_ref):
        o_ref[...] = w_ref[...].astype(BF16)

    return pl.pallas_call(
        body, name=name, grid=(r // tr,),
        in_specs=[pl.BlockSpec((tr, c), lambda i: (i, 0))],
        out_specs=pl.BlockSpec((tr, c), lambda i: (i, 0)),
        out_shape=jax.ShapeDtypeStruct((r, c), BF16),
        compiler_params=_params("parallel"),
    )(w)


def _adam_math(w, g, m, v):
    m2 = ADAM_B1 * m + (1.0 - ADAM_B1) * g
    v2 = ADAM_B2 * v + (1.0 - ADAM_B2) * (g * g)
    m_hat = m2 / (1.0 - ADAM_B1 ** ADAM_STEP)
    v_hat = v2 / (1.0 - ADAM_B2 ** ADAM_STEP)
    delta = -ADAM_LR * (m_hat / (jnp.sqrt(v_hat) + ADAM_EPS) + ADAM_WD * w)
    return delta, m2, v2


def _adamw_sum(bands, w, m, v, *, name):
    r, c = w.shape
    nband = len(bands)
    rows = r // nband
    tr = _pick(rows, (128, 64, 32, 16, 8))
    tc = c if (c % LANES or c <= 2560) else _pick(c, (2048, 1024, 512, 256, 128))
    nt = rows // tr
    flat = [p for band in bands for p in band]

    def body(*refs):
        p_refs = refs[:len(flat)]
        w_ref, m_ref, v_ref, g_out, d_out, m_out, v_out = refs[len(flat):]
        i = pl.program_id(0)
        g, at = None, 0
        for q, band in enumerate(bands):
            gq = None
            for p_ref in p_refs[at:at + len(band)]:
                for s in range(p_ref.shape[0]):
                    t = p_ref[s].astype(F32)
                    gq = t if gq is None else gq + t
            at += len(band)
            g = gq if q == 0 else jnp.where(i >= q * nt, gq, g)
        d, m2, v2 = _adam_math(w_ref[...], g, m_ref[...], v_ref[...])
        g_out[...] = g
        d_out[...] = d
        m_out[...] = m2
        v_out[...] = v2

    def band_spec(p, q):
        return pl.BlockSpec((p.shape[0], tr, tc), lambda i, j: (0, jnp.clip(i - q * nt, 0, nt - 1), j))

    spec = pl.BlockSpec((tr, tc), lambda i, j: (i, j))
    shp = jax.ShapeDtypeStruct((r, c), F32)
    return pl.pallas_call(
        body, name=name, grid=(r // tr, c // tc),
        in_specs=[band_spec(p, q) for q, band in enumerate(bands) for p in band] + [spec, spec, spec],
        out_specs=(spec, spec, spec, spec), out_shape=(shp, shp, shp, shp),
        compiler_params=_params("parallel", "parallel"),
    )(*flat, w, m, v)


def _adamw_small(g, w, m, v, *, name):
    shp = jax.ShapeDtypeStruct(w.shape, F32)

    def body(g_ref, w_ref, m_ref, v_ref, d_out, m_out, v_out):
        d, m2, v2 = _adam_math(w_ref[...], g_ref[...], m_ref[...], v_ref[...])
        d_out[...] = d
        m_out[...] = m2
        v_out[...] = v2

    return pl.pallas_call(body, name=name, out_shape=(shp, shp, shp),
                          compiler_params=pltpu.CompilerParams(vmem_limit_bytes=VMEM_LIMIT))(g, w, m, v)


def _sum_slots(parts, *, name):
    _, r, c = parts.shape

    def body(p_ref, o_ref):
        g = p_ref[0]
        for s in range(1, NDEV):
            g = g + p_ref[s]
        o_ref[...] = g

    return pl.pallas_call(body, name=name, out_shape=jax.ShapeDtypeStruct((r, c), F32),
                          compiler_params=pltpu.CompilerParams(vmem_limit_bytes=VMEM_LIMIT))(parts)


def _ln_parts(u):
    mu = jnp.mean(u, axis=-1, keepdims=True)
    xc = u - mu
    var = jnp.mean(xc * xc, axis=-1, keepdims=True)
    rstd = lax.rsqrt(var + LN_EPS)
    return xc * rstd, rstd


def _ln_fwd(xin, h, g, b, *, name):
    s, d = xin.shape
    tm = _pick(s, (128,))

    def body(x_ref, h_ref, g_ref, b_ref, o_ref, ob_ref):
        xhat, _ = _ln_parts(DEEPNORM_ALPHA * x_ref[...] + h_ref[...])
        o = xhat * g_ref[...] + b_ref[...]
        o_ref[...] = o
        ob_ref[...] = o.astype(BF16)

    row = pl.BlockSpec((tm, d), lambda i: (i, 0))
    vec = pl.BlockSpec((1, d), lambda i: (0, 0))
    return pl.pallas_call(
        body, name=name, grid=(s // tm,), in_specs=[row, row, vec, vec], out_specs=(row, row),
        out_shape=(jax.ShapeDtypeStruct((s, d), F32), jax.ShapeDtypeStruct((s, d), BF16)),
        compiler_params=_params("parallel"),
    )(xin, h, g, b)


def _ln_bwd(xin, h, g, b, cot, *, with_loss, name):
    s, d = xin.shape
    tm = _pick(s, (128,))

    def body(x_ref, h_ref, g_ref, b_ref, c_ref, du_ref, dub_ref, dg_ref, db_ref, *rest):
        i = pl.program_id(0)
        xhat, rstd = _ln_parts(DEEPNORM_ALPHA * x_ref[...] + h_ref[...])
        gv = g_ref[...]
        if with_loss:
            diff = xhat * gv + b_ref[...] - c_ref[...]
            part = 0.5 * jnp.sum(jnp.mean(diff * diff, axis=-1, keepdims=True), axis=0, keepdims=True)
            dout = diff / d
        else:
            dout = c_ref[...]

        @pl.when(i == 0)
        def _():
            dg_ref[...] = jnp.zeros_like(dg_ref)
            db_ref[...] = jnp.zeros_like(db_ref)
            if with_loss:
                rest[0][...] = jnp.zeros_like(rest[0])

        dg_ref[...] += jnp.sum(dout * xhat, axis=0, keepdims=True)
        db_ref[...] += jnp.sum(dout, axis=0, keepdims=True)
        if with_loss:
            rest[0][...] += jnp.broadcast_to(part, rest[0].shape)
        dxh = dout * gv
        du = rstd * (dxh - jnp.mean(dxh, axis=-1, keepdims=True)
                     - xhat * jnp.mean(dxh * xhat, axis=-1, keepdims=True))
        du_ref[...] = du
        dub_ref[...] = du.astype(BF16)

    row = pl.BlockSpec((tm, d), lambda i: (i, 0))
    vec = pl.BlockSpec((1, d), lambda i: (0, 0))
    out_specs = [row, row, vec, vec]
    out_shape = [jax.ShapeDtypeStruct((s, d), F32), jax.ShapeDtypeStruct((s, d), BF16),
                 jax.ShapeDtypeStruct((1, d), F32), jax.ShapeDtypeStruct((1, d), F32)]
    if with_loss:
        out_specs.append(pl.BlockSpec((SUBLANES, LANES), lambda i: (0, 0)))
        out_shape.append(jax.ShapeDtypeStruct((SUBLANES, LANES), F32))
    return pl.pallas_call(
        body, name=name, grid=(s // tm,), in_specs=[row, row, vec, vec, row],
        out_specs=tuple(out_specs), out_shape=tuple(out_shape),
        compiler_params=_params("arbitrary"),
    )(xin, h, g, b, cot)


def t5_causal_bucket(dist):
    max_exact = NUM_BUCKETS // 2
    d_f = jnp.maximum(dist, 1).astype(jnp.float32)
    large = max_exact + (jnp.log(d_f / max_exact) / math.log(MAX_DISTANCE / max_exact)
                         * (NUM_BUCKETS - max_exact)).astype(jnp.int32)
    large = jnp.minimum(large, NUM_BUCKETS - 1)
    return jnp.where(dist < max_exact, dist, large)


def _bias_tables(rel_bias, heads):
    qi = lax.broadcasted_iota(jnp.int32, (ATTN_BLOCK, 2 * ATTN_BLOCK), 0)
    ki = lax.broadcasted_iota(jnp.int32, (ATTN_BLOCK, 2 * ATTN_BLOCK), 1)
    delta = ATTN_BLOCK + qi - ki
    buckets = []
    for window, dilation in ATTN_PATTERNS:
        span = window // dilation
        assert span == ATTN_BLOCK
        band = (delta >= 0) & (delta <= span)
        buckets.append(jnp.where(band, t5_causal_bucket(jnp.clip(delta, 0, None) * dilation), -1))
    bucket = jnp.stack(buckets).astype(jnp.int32)

    def body(bk_ref, tbl_ref, o_ref):
        col = pl.program_id(0) * heads + pl.program_id(1)
        bk = bk_ref[...]
        acc = jnp.full(bk.shape, NEG_INF, F32)
        for b in range(NUM_BUCKETS):
            acc = jnp.where(bk == b, tbl_ref[b, col], acc)
        o_ref[...] = acc

    tile = (None, ATTN_BLOCK, 2 * ATTN_BLOCK)
    bias = pl.pallas_call(
        body, name="bias_fwd", grid=(N_GROUPS_ATTN, heads),
        in_specs=[pl.BlockSpec(tile, lambda g, h: (g, 0, 0)), pl.BlockSpec(memory_space=pltpu.SMEM)],
        out_specs=pl.BlockSpec((None,) + tile, lambda g, h: (g, h, 0, 0)),
        out_shape=jax.ShapeDtypeStruct((N_GROUPS_ATTN, heads, ATTN_BLOCK, 2 * ATTN_BLOCK), F32),
        compiler_params=_params("parallel", "parallel"),
    )(bucket, rel_bias)
    return bias, bucket


def _attn_logits(q, kc, kp, bias_c, bias_p, j):
    scale = HEAD_DIM ** -0.5
    sc = _dot_nt(q, kc) * scale + bias_c
    sp = _dot_nt(q, kp) * scale + jnp.where(j > 0, bias_p, NEG_INF)
    return sc, sp


def _dilated_view(qkv, g, dilation, da):
    heads = da // HEAD_DIM
    if dilation == 1:
        return qkv, qkv.shape[1] // HEAD_DIM, g * 3 * heads
    sub = qkv[:, g * 3 * da:(g + 1) * 3 * da]
    return sub.reshape(qkv.shape[0] // dilation, dilation * 3 * da), 3 * heads, 0


def _attn_fwd_group(qkv, bias_g, g, dilation, da):
    s = qkv.shape[0]
    heads = da // HEAD_DIM
    l = s // dilation
    nb = l // ATTN_BLOCK
    view, cpb, base = _dilated_view(qkv, g, dilation, da)

    def body(q_ref, k_ref, v_ref, b_ref, o_ref, l_ref):
        bias_p = b_ref[0, :, 0:ATTN_BLOCK]
        bias_c = b_ref[0, :, ATTN_BLOCK:2 * ATTN_BLOCK]

        def step(j, carry):
            r0 = pl.multiple_of(j * ATTN_BLOCK, ATTN_BLOCK)
            rp = pl.multiple_of(jnp.maximum(j - 1, 0) * ATTN_BLOCK, ATTN_BLOCK)
            q = q_ref[pl.ds(r0, ATTN_BLOCK), :]
            sc, sp = _attn_logits(q, k_ref[pl.ds(r0, ATTN_BLOCK), :], k_ref[pl.ds(rp, ATTN_BLOCK), :],
                                  bias_c, bias_p, j)
            mx = jnp.maximum(jnp.max(sc, axis=-1, keepdims=True), jnp.max(sp, axis=-1, keepdims=True))
            pc = jnp.exp(sc - mx)
            pp = jnp.exp(sp - mx)
            den = jnp.sum(pc, axis=-1, keepdims=True) + jnp.sum(pp, axis=-1, keepdims=True)
            inv = 1.0 / den
            o = (_dot((pc * inv).astype(BF16), v_ref[pl.ds(r0, ATTN_BLOCK), :])
                 + _dot((pp * inv).astype(BF16), v_ref[pl.ds(rp, ATTN_BLOCK), :]))
            o_ref[pl.ds(r0, ATTN_BLOCK), :] = o
            l_ref[pl.ds(r0, ATTN_BLOCK), :] = jnp.broadcast_to(mx + jnp.log(den), (ATTN_BLOCK, HEAD_DIM))
            return carry

        lax.fori_loop(0, nb, step, 0)

    def col(t):
        return lambda r, h: (0, r * cpb + base + t * heads + h)

    blk = (l, HEAD_DIM)
    out = pl.BlockSpec(blk, lambda r, h: (0, r * heads + h))
    shp = jax.ShapeDtypeStruct((l, dilation * da), F32)
    o, lse = pl.pallas_call(
        body, name=f"attn_fwd_g{g}", grid=(dilation, heads),
        in_specs=[pl.BlockSpec(blk, col(0)), pl.BlockSpec(blk, col(1)), pl.BlockSpec(blk, col(2)),
                  pl.BlockSpec((1, ATTN_BLOCK, 2 * ATTN_BLOCK), lambda r, h: (h, 0, 0))],
        out_specs=(out, out), out_shape=(shp, shp),
        compiler_params=_params("parallel", "parallel"),
    )(view, view, view, bias_g)
    return o.reshape(s, da), lse.reshape(s, da)


def _attn_combine(os_, ls_, gate):
    s, da = gate.shape
    tm = _pick(s, (512, 256, 128))
    tc = _pick(da, (512, 256, 128))

    def body(o0, o1, o2, l0, l1, l2, g_ref, o_ref, l_ref, y_ref):
        a0, a1, a2 = l0[...], l1[...], l2[...]
        mx = jnp.maximum(jnp.maximum(a0, a1), a2)
        e0, e1, e2 = jnp.exp(a0 - mx), jnp.exp(a1 - mx), jnp.exp(a2 - mx)
        den = e0 + e1 + e2
        o = (e0 * o0[...] + e1 * o1[...] + e2 * o2[...]) / den
        gv = g_ref[...]
        o_ref[...] = o
        l_ref[...] = mx + jnp.log(den)
        y_ref[...] = (o * (gv * _sigmoid(gv))).astype(BF16)

    spec = pl.BlockSpec((tm, tc), lambda i, j: (i, j))
    return pl.pallas_call(
        body, name="attn_combine", grid=(s // tm, da // tc), in_specs=[spec] * 7, out_specs=(spec, spec, spec),
        out_shape=(jax.ShapeDtypeStruct((s, da), F32), jax.ShapeDtypeStruct((s, da), F32),
                   jax.ShapeDtypeStruct((s, da), BF16)),
        compiler_params=_params("parallel", "parallel"),
    )(*os_, *ls_, gate)


def _attn_bwd_prep(dy, o, gate):
    s, da = gate.shape
    tm = _pick(s, (512, 256, 128))

    def body(dy_ref, o_ref, g_ref, do_ref, dg_ref, dd_ref):
        gv = g_ref[...]
        sg = _sigmoid(gv)
        dyv = dy_ref[...]
        ov = o_ref[...]
        do = dyv * (gv * sg)
        do_ref[...] = do.astype(BF16)
        dg_ref[...] = (dyv * ov * (sg * (1.0 + gv * (1.0 - sg)))).astype(BF16)
        dd_ref[...] = jnp.broadcast_to(jnp.sum(do * ov, axis=-1, keepdims=True), (tm, HEAD_DIM))

    spec = pl.BlockSpec((tm, HEAD_DIM), lambda i, j: (i, j))
    return pl.pallas_call(
        body, name="attn_bwd_prep", grid=(s // tm, da // HEAD_DIM), in_specs=[spec] * 3,
        out_specs=(spec, spec, spec),
        out_shape=(jax.ShapeDtypeStruct((s, da), BF16), jax.ShapeDtypeStruct((s, da), BF16),
                   jax.ShapeDtypeStruct((s, da), F32)),
        compiler_params=_params("parallel", "parallel"),
    )(dy, o, gate)


def _attn_bwd_group(qkv, do, lse, dd, bias_g, g, dilation, da, ride=None):
    s = qkv.shape[0]
    heads = da // HEAD_DIM
    l = s // dilation
    nb = l // ATTN_BLOCK
    view, cpb, base = _dilated_view(qkv, g, dilation, da)
    scale = HEAD_DIM ** -0.5

    def body(q_ref, k_ref, v_ref, do_ref, l_ref, dd_ref, b_ref, dq_ref, dk_ref, dv_ref, ds_ref, dk_acc, dv_acc):
        r = pl.program_id(1)
        bias_p = b_ref[0, :, 0:ATTN_BLOCK]
        bias_c = b_ref[0, :, ATTN_BLOCK:2 * ATTN_BLOCK]

        @pl.when(r == 0)
        def _():
            ds_ref[...] = jnp.zeros_like(ds_ref)

        dk_acc[...] = jnp.zeros_like(dk_acc)
        dv_acc[...] = jnp.zeros_like(dv_acc)

        def step(j, carry):
            r0 = pl.multiple_of(j * ATTN_BLOCK, ATTN_BLOCK)
            rp = pl.multiple_of(jnp.maximum(j - 1, 0) * ATTN_BLOCK, ATTN_BLOCK)
            cur = pl.ds(r0, ATTN_BLOCK)
            prev = pl.ds(rp, ATTN_BLOCK)
            q = q_ref[cur, :]
            kc, kp = k_ref[cur, :], k_ref[prev, :]
            dov = do_ref[cur, :]
            sc, sp = _attn_logits(q, kc, kp, bias_c, bias_p, j)
            lrow = l_ref[cur, 0:1]
            drow = dd_ref[cur, 0:1]
            pc = jnp.exp(sc - lrow)
            pp = jnp.exp(sp - lrow)
            dsc = pc * (_dot_nt(dov, v_ref[cur, :]) - drow)
            dsp = pp * (_dot_nt(dov, v_ref[prev, :]) - drow)
            ds_ref[0, :, ATTN_BLOCK:2 * ATTN_BLOCK] += dsc
            ds_ref[0, :, 0:ATTN_BLOCK] += dsp
            dscb, dspb = dsc.astype(BF16), dsp.astype(BF16)
            dq_ref[cur, :] = ((_dot(dscb, kc) + _dot(dspb, kp)) * scale).astype(BF16)
            dk_acc[cur, :] += _dot_tn(dscb, q) * scale
            dk_acc[prev, :] += _dot_tn(dspb, q) * scale
            dv_acc[cur, :] += _dot_tn(pc.astype(BF16), dov)
            dv_acc[prev, :] += _dot_tn(pp.astype(BF16), dov)
            return carry

        lax.fori_loop(0, nb, step, 0)
        dk_ref[...] = dk_acc[...].astype(BF16)
        dv_ref[...] = dv_acc[...].astype(BF16)

    def col(t):
        return lambda h, r: (0, r * cpb + base + t * heads + h)

    blk = (l, HEAD_DIM)
    act = pl.BlockSpec(blk, lambda h, r: (0, r * heads + h))
    shp = jax.ShapeDtypeStruct((l, dilation * da), BF16)
    actv = lambda t: t.reshape(l, dilation * da)
    (dq, dk, dv, ds), rode = _call(
        body, name=f"attn_bwd_g{g}", grid=(heads, dilation),
        in_specs=[pl.BlockSpec(blk, col(0)), pl.BlockSpec(blk, col(1)), pl.BlockSpec(blk, col(2)), act, act, act,
                  pl.BlockSpec((1, ATTN_BLOCK, 2 * ATTN_BLOCK), lambda h, r: (h, 0, 0))],
        out_specs=[act, act, act, pl.BlockSpec((1, ATTN_BLOCK, 2 * ATTN_BLOCK), lambda h, r: (h, 0, 0))],
        out_shape=[shp, shp, shp, jax.ShapeDtypeStruct((heads, ATTN_BLOCK, 2 * ATTN_BLOCK), F32)],
        scratch=[pltpu.VMEM(blk, F32), pltpu.VMEM(blk, F32)], sem=("parallel", "arbitrary"), ride=ride,
        args=(view, view, view, actv(do), actv(lse), actv(dd), bias_g))
    return (dq.reshape(s, da), dk.reshape(s, da), dv.reshape(s, da), ds), rode


def _bias_bwd(ds, bucket):
    ng, heads = ds.shape[0], ds.shape[1]

    def body(ds_ref, bk_ref, o_ref):
        bk = bk_ref[...]
        x = ds_ref[...]
        for b in range(NUM_BUCKETS):
            o_ref[:, b:b + 1] = jnp.sum(jnp.where(bk == b, x, 0.0), axis=(0, 1), keepdims=True)

    tile = (None, ATTN_BLOCK, 2 * ATTN_BLOCK)
    out = pl.pallas_call(
        body, name="bias_bwd", grid=(ng, heads),
        in_specs=[pl.BlockSpec((None,) + tile, lambda g, h: (g, h, 0, 0)), pl.BlockSpec(tile, lambda g, h: (g, 0, 0))],
        out_specs=pl.BlockSpec((None, None, 1, NUM_BUCKETS), lambda g, h: (g, h, 0, 0)),
        out_shape=jax.ShapeDtypeStruct((ng, heads, 1, NUM_BUCKETS), F32),
        compiler_params=_params("parallel", "parallel"),
    )(ds, bucket)
    return out.reshape(ng, heads, NUM_BUCKETS)


def _shift_rows(x, halo, s):
    r = pltpu.roll(x, s, axis=0)
    rh = pltpu.roll(halo, s, axis=0)
    row = lax.broadcasted_iota(jnp.int32, halo.shape, 0)
    top = jnp.where(row < s, rh, r[0:SUBLANES])
    return jnp.concatenate([top, r[SUBLANES:]], axis=0)


def _conv_out(x, halo, w, b):
    acc = b + w[CONV_WIDTH - 1:CONV_WIDTH] * x
    for kk in range(CONV_WIDTH - 1):
        acc = acc + w[kk:kk + 1] * _shift_rows(x, halo, CONV_WIDTH - 1 - kk)
    return acc


def _conv_fwd(proj, conv_w, conv_b, col0):
    s = proj.shape[0]
    c = conv_w.shape[1]
    ts = _pick(s, (512, 256, 128))
    tc = _pick(math.gcd(c, col0), (512, 256, 128))
    cb0 = col0 // tc
    hb = ts // SUBLANES

    def body(x_ref, h_ref, w_ref, b_ref, o_ref):
        i = pl.program_id(0)
        halo = jnp.where(i > 0, h_ref[...], 0.0)
        u = _conv_out(x_ref[...], halo, w_ref[...], b_ref[...])
        o_ref[...] = u * _sigmoid(u)

    return pl.pallas_call(
        body, name="conv_fwd", grid=(s // ts, c // tc),
        in_specs=[pl.BlockSpec((ts, tc), lambda i, j: (i, cb0 + j)),
                  pl.BlockSpec((SUBLANES, tc), lambda i, j: (jnp.maximum(i * hb - 1, 0), cb0 + j)),
                  pl.BlockSpec((CONV_WIDTH, tc), lambda i, j: (0, j)),
                  pl.BlockSpec((1, tc), lambda i, j: (0, j))],
        out_specs=pl.BlockSpec((ts, tc), lambda i, j: (i, j)),
        out_shape=jax.ShapeDtypeStruct((s, c), F32),
        compiler_params=_params("parallel", "parallel"),
    )(proj, proj, conv_w, conv_b)


def _conv_bwd(proj, conv_w, conv_b, dacts, col0, dproj):
    s = proj.shape[0]
    c = conv_w.shape[1]
    widths = [d.shape[1] for d in dacts]
    assert sum(widths) == c
    ts = _pick(s, (512, 256, 128))
    tc = _pick(math.gcd(math.gcd(c, col0), math.gcd(*widths)), (512, 256, 128))
    cb0 = col0 // tc
    hb = ts // SUBLANES
    nblk = s // ts
    ext = ts + SUBLANES
    nb = [wd // tc for wd in widths]
    starts = [0, nb[0], nb[0] + nb[1]]

    def body(x_ref, xp_ref, xn_ref, d0, d1, d2, n0, n1, n2, w_ref, b_ref, _, dx_ref, dw_ref, db_ref):
        j = pl.program_id(0)
        i = pl.program_id(1)
        last = i == nblk - 1
        w = w_ref[...]
        halo = jnp.where(i > 0, xp_ref[...], 0.0)
        x = x_ref[...]
        xe = jnp.concatenate([x, xn_ref[...]], axis=0)
        dcur = jnp.where(j < starts[1], d0[...], jnp.where(j < starts[2], d1[...], d2[...]))
        dnext = jnp.where(j < starts[1], n0[...], jnp.where(j < starts[2], n1[...], n2[...]))
        de = jnp.concatenate([dcur, jnp.where(last, 0.0, dnext)], axis=0)
        u = _conv_out(xe, halo, w, b_ref[...])
        sg = _sigmoid(u)
        dpre = de * (sg * (1.0 + u * (1.0 - sg)))
        dx = w[CONV_WIDTH - 1:CONV_WIDTH] * dpre[0:ts]
        for kk in range(CONV_WIDTH - 1):
            sh = CONV_WIDTH - 1 - kk
            dx = dx + w[kk:kk + 1] * pltpu.roll(dpre, ext - sh, axis=0)[0:ts]
        dx_ref[...] = dx.astype(BF16)
        dcur = dpre[0:ts]

        @pl.when(i == 0)
        def _():
            dw_ref[...] = jnp.zeros_like(dw_ref)
            db_ref[...] = jnp.zeros_like(db_ref)

        db_ref[...] += jnp.sum(dcur, axis=0, keepdims=True)
        dw_ref[CONV_WIDTH - 1:CONV_WIDTH, :] += jnp.sum(dcur * x, axis=0, keepdims=True)
        for kk in range(CONV_WIDTH - 1):
            xs = _shift_rows(x, halo, CONV_WIDTH - 1 - kk)
            dw_ref[kk:kk + 1, :] += jnp.sum(dcur * xs, axis=0, keepdims=True)

    cur_p = pl.BlockSpec((ts, tc), lambda j, i: (i, cb0 + j))
    prev_p = pl.BlockSpec((SUBLANES, tc), lambda j, i: (jnp.maximum(i * hb - 1, 0), cb0 + j))
    nxt = lambda i: jnp.minimum((i + 1) * hb, nblk * hb - 1)
    next_p = pl.BlockSpec((SUBLANES, tc), lambda j, i: (nxt(i), cb0 + j))

    def part(q):
        return lambda j: jnp.clip(j - starts[q], 0, nb[q] - 1)

    cur_d = [pl.BlockSpec((ts, tc), lambda j, i, f=part(q): (i, f(j))) for q in range(3)]
    next_d = [pl.BlockSpec((SUBLANES, tc), lambda j, i, f=part(q): (nxt(i), f(j))) for q in range(3)]
    vec4 = pl.BlockSpec((CONV_WIDTH, tc), lambda j, i: (0, j))
    vec1 = pl.BlockSpec((1, tc), lambda j, i: (0, j))
    return pl.pallas_call(
        body, name="conv_bwd", grid=(c // tc, nblk),
        in_specs=[cur_p, prev_p, next_p, *cur_d, *next_d, vec4, vec1, pl.BlockSpec(memory_space=pl.ANY)],
        out_specs=(cur_p, vec4, vec1),
        out_shape=(jax.ShapeDtypeStruct(dproj.shape, dproj.dtype), jax.ShapeDtypeStruct((CONV_WIDTH, c), F32),
                   jax.ShapeDtypeStruct((1, c), F32)),
        input_output_aliases={11: 0},
        compiler_params=_params("parallel", "arbitrary"),
    )(proj, proj, proj, *dacts, *dacts, conv_w, conv_b, dproj)


def _dt_fwd(proj, dt_bias, col0):
    s = proj.shape[0]
    h = dt_bias.shape[1]
    ts = _pick(s, (1024, 512, 256, 128))

    def body(x_ref, b_ref, o_ref):
        v = x_ref[...] + b_ref[...]
        o_ref[...] = jnp.maximum(v, 0.0) + jnp.log1p(jnp.exp(-jnp.abs(v)))

    return pl.pallas_call(
        body, name="dt_fwd", grid=(s // ts,),
        in_specs=[pl.BlockSpec((ts, h), lambda i: (i, col0 // h)), pl.BlockSpec((1, h), lambda i: (0, 0))],
        out_specs=pl.BlockSpec((ts, h), lambda i: (i, 0)), out_shape=jax.ShapeDtypeStruct((s, h), F32),
        compiler_params=_params("parallel"),
    )(proj, dt_bias)


def _dt_bwd(proj, dt_bias, ddt, col0, dproj):
    s = proj.shape[0]
    h = dt_bias.shape[1]
    ts = _pick(s, (1024, 512, 256, 128))

    def body(x_ref, b_ref, d_ref, _, o_ref, db_ref):
        i = pl.program_id(0)
        draw = d_ref[...] * _sigmoid(x_ref[...] + b_ref[...])
        o_ref[...] = draw.astype(BF16)

        @pl.when(i == 0)
        def _():
            db_ref[...] = jnp.zeros_like(db_ref)

        db_ref[...] += jnp.sum(draw, axis=0, keepdims=True)

    return pl.pallas_call(
        body, name="dt_bwd", grid=(s // ts,),
        in_specs=[pl.BlockSpec((ts, h), lambda i: (i, col0 // h)), pl.BlockSpec((1, h), lambda i: (0, 0)),
                  pl.BlockSpec((ts, h), lambda i: (i, 0)), pl.BlockSpec(memory_space=pl.ANY)],
        out_specs=(pl.BlockSpec((ts, h), lambda i: (i, col0 // h)), pl.BlockSpec((1, h), lambda i: (0, 0))),
        out_shape=(jax.ShapeDtypeStruct(dproj.shape, dproj.dtype), jax.ShapeDtypeStruct((1, h), F32)),
        input_output_aliases={3: 0},
        compiler_params=_params("arbitrary"),
    )(proj, dt_bias, ddt, dproj)


def _chunk_terms(dt, dt_t, a, a_t):
    li = lax.broadcasted_iota(jnp.int32, (CHUNK, CHUNK), 0)
    si = lax.broadcasted_iota(jnp.int32, (CHUNK, CHUNK), 1)
    lower = (li >= si).astype(F32)
    upper = (li <= si).astype(F32)
    acum = jnp.dot(lower, dt * a, preferred_element_type=F32, precision=HIGHEST)
    acum_t = jnp.dot(dt_t * a_t, upper, preferred_element_type=F32, precision=HIGHEST)
    return acum, acum_t, li, si, upper


def _dot_exact01(t, m01):
    r = t.shape[0]
    hi = t.astype(BF16)
    rest = t - hi.astype(F32)
    mid = rest.astype(BF16)
    lo = (rest - mid.astype(F32)).astype(BF16)
    out = _dot(jnp.concatenate([hi, mid, lo], axis=0), m01.astype(BF16))
    return out[0:r] + out[r:2 * r] + out[2 * r:3 * r]


def _head_lanes(dt, acum, gw):
    hpg = dt.shape[1]
    p = gw // hpg
    spread = (lax.broadcasted_iota(jnp.int32, (hpg, gw), 1) // p
              == lax.broadcasted_iota(jnp.int32, (hpg, gw), 0)).astype(F32)
    both = _dot_exact01(jnp.concatenate([dt, acum], axis=0), spread)
    dt_e, acum_e = both[0:CHUNK], both[CHUNK:2 * CHUNK]
    alast_e = acum_e[CHUNK - 1:CHUNK, :]
    return dt_e, jnp.exp(acum_e), jnp.exp(alast_e - acum_e), jnp.exp(alast_e)


def _fold_heads(t, hpg):
    gw = t.shape[1]
    p = gw // hpg
    fold = (lax.broadcasted_iota(jnp.int32, (gw, hpg), 0) // p
            == lax.broadcasted_iota(jnp.int32, (gw, hpg), 1)).astype(F32)
    return _dot_exact01(t, fold)


def _ssd_fwd(xbc, proj, dt_g, dt_gt, a_g, a_gt, dskip_e, norm_w, d_inner, n_state):
    s = xbc.shape[0]
    hpg = dt_g.shape[2]
    gw = d_inner // SSM_GROUPS
    p = gw // hpg
    nc = s // CHUNK
    n = n_state
    b0 = d_inner // n
    c0 = b0 + SSM_GROUPS
    per_tile = LANES // p

    def body(xs_ref, b_ref, c_ref, dt_ref, dtt_ref, a_ref, at_ref, z_ref, dsk_ref, nw_ref,
             yn_ref, y_ref, st_ref, state):
        c = pl.program_id(1)

        @pl.when(c == 0)
        def _():
            state[...] = jnp.zeros_like(state)

        st = state[...]
        st_ref[...] = st
        xs = xs_ref[...]
        bm = b_ref[...].astype(BF16)
        cm = c_ref[...].astype(BF16)
        dt = dt_ref[...]
        acum, acum_t, li, si, _ = _chunk_terms(dt, dtt_ref[...], a_ref[...], at_ref[...])
        dt_e, e_a, t_e, e_last = _head_lanes(dt, acum, gw)
        xdt = xs * dt_e
        xdtb = xdt.astype(BF16)
        cb = _dot_nt(cm, bm)
        causal = li >= si
        lane = lax.broadcasted_iota(jnp.int32, (1, LANES), 1)
        y_ref[...] = _dot(cm, st.astype(BF16)) * e_a
        for q in range(gw // LANES):
            ql = slice(q * LANES, (q + 1) * LANES)
            xq = xdtb[:, ql]
            ms = []
            for i in range(per_tile):
                h = q * per_tile + i
                decay = jnp.exp(jnp.where(causal, acum[:, h:h + 1] - acum_t[h:h + 1, :], NEG_INF))
                ms.append((cb * decay).astype(BF16))
            y_all = _dot(jnp.concatenate(ms, axis=0), xq)
            yd = y_all[0:CHUNK]
            for i in range(1, per_tile):
                yd = jnp.where(lane >= i * p, y_all[i * CHUNK:(i + 1) * CHUNK], yd)
            y_ref[:, ql] += yd
        state[...] = st * e_last + _dot_tn(bm, (xdt * t_e).astype(BF16))
        yt = y_ref[...] + xs * dsk_ref[...]
        z = z_ref[...]
        yz = yt * (z * _sigmoid(z))
        r = lax.rsqrt(jnp.mean(yz * yz, axis=-1, keepdims=True) + RMS_EPS)
        yn_ref[...] = (yz * r * nw_ref[...]).astype(BF16)

    wide = pl.BlockSpec((CHUNK, gw), lambda g, c: (c, g))
    return pl.pallas_call(
        body, name="ssd_fwd", grid=(SSM_GROUPS, nc),
        in_specs=[wide,
                  pl.BlockSpec((CHUNK, n), lambda g, c: (c, b0 + g)),
                  pl.BlockSpec((CHUNK, n), lambda g, c: (c, c0 + g)),
                  pl.BlockSpec((None, CHUNK, hpg), lambda g, c: (g, c, 0)),
                  pl.BlockSpec((None, hpg, CHUNK), lambda g, c: (g, 0, c)),
                  pl.BlockSpec((None, 1, hpg), lambda g, c: (g, 0, 0)),
                  pl.BlockSpec((None, hpg, 1), lambda g, c: (g, 0, 0)),
                  wide,
                  pl.BlockSpec((None, 1, gw), lambda g, c: (g, 0, 0)),
                  pl.BlockSpec((1, gw), lambda g, c: (0, g))],
        out_specs=(wide, wide, pl.BlockSpec((None, None, n, gw), lambda g, c: (g, c, 0, 0))),
        out_shape=(jax.ShapeDtypeStruct((s, d_inner), BF16), jax.ShapeDtypeStruct((s, d_inner), F32),
                   jax.ShapeDtypeStruct((SSM_GROUPS, nc, n, gw), F32)),
        scratch_shapes=[pltpu.VMEM((n, gw), F32)],
        compiler_params=_params("parallel", "arbitrary"),
    )(xbc, xbc, xbc, dt_g, dt_gt, a_g, a_gt, proj, dskip_e, norm_w)


def _ssd_epilogue_bwd(dyn, y, xbc, proj, dskip_e, norm_w, hpg):
    s, d_inner = dyn.shape
    gw = d_inner // SSM_GROUPS
    p = gw // hpg
    nc = s // CHUNK

    def body(dyn_ref, y_ref, xs_ref, z_ref, dsk_ref, nw_ref, dy_ref, dz_ref, dnw_ref, ddsk_ref):
        c = pl.program_id(1)
        xs = xs_ref[...]
        z = z_ref[...]
        yt = y_ref[...] + xs * dsk_ref[...]
        sg = _sigmoid(z)
        sz = z * sg
        yz = yt * sz
        r = lax.rsqrt(jnp.mean(yz * yz, axis=-1, keepdims=True) + RMS_EPS)
        dynv = dyn_ref[...]
        dyh = dynv * nw_ref[...]
        dyz = r * (dyh - yz * (r * r) * jnp.mean(dyh * yz, axis=-1, keepdims=True))
        dyt = dyz * sz
        dy_ref[...] = dyt
        dz_ref[...] = (dyz * yt * (sg * (1.0 + z * (1.0 - sg)))).astype(BF16)

        @pl.when(c == 0)
        def _():
            dnw_ref[...] = jnp.zeros_like(dnw_ref)
            ddsk_ref[...] = jnp.zeros_like(ddsk_ref)

        dnw_ref[...] += jnp.sum(dynv * yz * r, axis=0, keepdims=True)
        colsum = jnp.sum(dyt * xs, axis=0, keepdims=True)
        fold = (lax.broadcasted_iota(jnp.int32, (gw, hpg), 0) // p
                == lax.broadcasted_iota(jnp.int32, (gw, hpg), 1)).astype(F32)
        ddsk_ref[...] += jnp.dot(colsum, fold, preferred_element_type=F32, precision=HIGHEST)

    wide = pl.BlockSpec((CHUNK, gw), lambda g, c: (c, g))
    return pl.pallas_call(
        body, name="ssd_epilogue_bwd", grid=(SSM_GROUPS, nc),
        in_specs=[wide, wide, wide, wide, pl.BlockSpec((None, 1, gw), lambda g, c: (g, 0, 0)),
                  pl.BlockSpec((1, gw), lambda g, c: (0, g))],
        out_specs=(wide, wide, pl.BlockSpec((1, gw), lambda g, c: (0, g)),
                   pl.BlockSpec((None, 1, hpg), lambda g, c: (g, 0, 0))),
        out_shape=(jax.ShapeDtypeStruct((s, d_inner), F32), jax.ShapeDtypeStruct((s, proj.shape[1]), BF16),
                   jax.ShapeDtypeStruct((1, d_inner), F32), jax.ShapeDtypeStruct((SSM_GROUPS, 1, hpg), F32)),
        compiler_params=_params("parallel", "arbitrary"),
    )(dyn, y, xbc, proj, dskip_e, norm_w)


def _ssd_scan_bwd(xbc, dt_g, dt_gt, a_g, a_gt, states, dy, dskip_e, d_inner, n_state, ride=None):
    s = xbc.shape[0]
    hpg = dt_g.shape[2]
    gw = d_inner // SSM_GROUPS
    p = gw // hpg
    nc = s // CHUNK
    n = n_state
    b0 = d_inner // n
    c0 = b0 + SSM_GROUPS
    per_tile = LANES // p

    def body(xs_ref, b_ref, c_ref, dt_ref, dtt_ref, a_ref, at_ref, st_ref, dy_ref, dsk_ref,
             dxs_ref, db_ref, dc_ref, ddt_ref, da_ref, dstate, ydiag_ref, dxd_ref):
        c = pl.program_id(1)

        @pl.when(c == 0)
        def _():
            dstate[...] = jnp.zeros_like(dstate)
            da_ref[...] = jnp.zeros_like(da_ref)

        xs = xs_ref[...]
        bm = b_ref[...].astype(BF16)
        cm = c_ref[...].astype(BF16)
        dt = dt_ref[...]
        a = a_ref[...]
        dyv = dy_ref[...]
        dsk = dsk_ref[...]
        acum, acum_t, li, si, upper = _chunk_terms(dt, dtt_ref[...], a, at_ref[...])
        dt_e, e_a, t_e, e_last = _head_lanes(dt, acum, gw)
        cb = _dot_nt(cm, bm)
        lower_mask = li >= si
        lane = lax.broadcasted_iota(jnp.int32, (1, LANES), 1)
        row_l = lax.broadcasted_iota(jnp.int32, (CHUNK, 1), 0)
        st = st_ref[...]
        stb = st.astype(BF16)
        dst = dstate[...]
        dstb = dst.astype(BF16)
        xdt = xs * dt_e
        xdtb = xdt.astype(BF16)
        dyb = dyv.astype(BF16)
        dye = dyv * e_a
        dyeb = dye.astype(BF16)
        xte = xdt * t_e
        xteb = xte.astype(BF16)
        wv = _dot(bm, dstb)
        yo = _dot(cm, stb)
        dcb = jnp.zeros((CHUNK, CHUNK), F32)
        for q in range(gw // LANES):
            ql = slice(q * LANES, (q + 1) * LANES)
            xq = xdtb[:, ql]
            dq = dyb[:, ql]
            decays, ms, mts, dqs = [], [], [], []
            for i in range(per_tile):
                h = q * per_tile + i
                decay = jnp.exp(jnp.where(lower_mask, acum[:, h:h + 1] - acum_t[h:h + 1, :], NEG_INF))
                mm = cb * decay
                mine = (lane >= i * p) & (lane < (i + 1) * p)
                decays.append(decay)
                ms.append(mm.astype(BF16))
                mts.append(mm.T.astype(BF16))
                dqs.append(jnp.where(mine, dq, jnp.zeros_like(dq)))
            dm_all = _dot_nt(jnp.concatenate(dqs, axis=0), xq)
            y_all = _dot(jnp.concatenate(ms, axis=0), xq)
            d_all = _dot(jnp.concatenate(mts, axis=0), dq)
            yd = dd = None
            for i in range(per_tile):
                rows = slice(i * CHUNK, (i + 1) * CHUNK)
                dcb = dcb + dm_all[rows] * decays[i]
                yd = y_all[rows] if i == 0 else jnp.where(lane >= i * p, y_all[rows], yd)
                dd = d_all[rows] if i == 0 else jnp.where(lane >= i * p, d_all[rows], dd)
            ydiag_ref[:, ql] = yd
            dxd_ref[:, ql] = dd
        ydiag = ydiag_ref[...]
        dxd = dxd_ref[...]
        dxdt = dxd + t_e * wv
        xw = xte * wv
        last_in = jnp.sum(xw, axis=0, keepdims=True) + e_last * jnp.sum(dst * st, axis=0, keepdims=True)
        folded = _fold_heads(jnp.concatenate(
            [dyb.astype(F32) * ydiag - xdtb.astype(F32) * dxd - xw + dye * yo, dxdt * xs,
             jnp.broadcast_to(last_in, (SUBLANES, gw))],
            axis=0), hpg)
        dalast = folded[2 * CHUNK:2 * CHUNK + 1]
        d_acum = folded[0:CHUNK] + jnp.where(row_l == CHUNK - 1, dalast, 0.0)
        ddt_x = folded[CHUNK:2 * CHUNK]
        dxs_ref[...] = dxdt * dt_e + dyv * dsk
        dbf = dcb.astype(BF16)
        dc_ref[...] = _dot_nt(dyeb, stb) + _dot(dbf, bm)
        db_ref[...] = _dot_nt(xteb, dstb) + _dot_tn(dbf, cm)
        dstate[...] = dst * e_last + _dot_tn(cm, dyeb)
        d_da = jnp.dot(upper, d_acum, preferred_element_type=F32, precision=HIGHEST)
        ddt_ref[...] = d_da * a + ddt_x
        da_ref[...] += jnp.sum(d_da * dt, axis=0, keepdims=True)

    rev = lambda c: nc - 1 - c
    wide = pl.BlockSpec((CHUNK, gw), lambda g, c: (rev(c), g))
    return _call(
        body, name="ssd_scan_bwd", grid=(SSM_GROUPS, nc),
        in_specs=[wide,
                  pl.BlockSpec((CHUNK, n), lambda g, c: (rev(c), b0 + g)),
                  pl.BlockSpec((CHUNK, n), lambda g, c: (rev(c), c0 + g)),
                  pl.BlockSpec((None, CHUNK, hpg), lambda g, c: (g, rev(c), 0)),
                  pl.BlockSpec((None, hpg, CHUNK), lambda g, c: (g, 0, rev(c))),
                  pl.BlockSpec((None, 1, hpg), lambda g, c: (g, 0, 0)),
                  pl.BlockSpec((None, hpg, 1), lambda g, c: (g, 0, 0)),
                  pl.BlockSpec((None, None, n, gw), lambda g, c: (g, rev(c), 0, 0)),
                  wide,
                  pl.BlockSpec((None, 1, gw), lambda g, c: (g, 0, 0))],
        out_specs=[wide,
                   pl.BlockSpec((CHUNK, n), lambda g, c: (rev(c), g)),
                   pl.BlockSpec((CHUNK, n), lambda g, c: (rev(c), g)),
                   pl.BlockSpec((None, CHUNK, hpg), lambda g, c: (g, rev(c), 0)),
                   pl.BlockSpec((None, 1, hpg), lambda g, c: (g, 0, 0))],
        out_shape=[jax.ShapeDtypeStruct((s, d_inner), F32),
                   jax.ShapeDtypeStruct((s, SSM_GROUPS * n), F32), jax.ShapeDtypeStruct((s, SSM_GROUPS * n), F32),
                   jax.ShapeDtypeStruct((SSM_GROUPS, s, hpg), F32), jax.ShapeDtypeStruct((SSM_GROUPS, 1, hpg), F32)],
        scratch=[pltpu.VMEM((n, gw), F32), pltpu.VMEM((CHUNK, gw), F32), pltpu.VMEM((CHUNK, gw), F32)],
        sem=("parallel", "arbitrary"), ride=ride,
        args=(xbc, xbc, xbc, dt_g, dt_gt, a_g, a_gt, states, dy, dskip_e))


def _lin(p):
    return 4 * p[0] + 2 * p[1] + p[2]


class _Gather:
    def __init__(self, arrs):
        self.arrs = list(arrs)

    def out_shape(self):
        return [jax.ShapeDtypeStruct((NDEV,) + a.shape, a.dtype) for a in self.arrs]

    def _copies(self, ins, outs, sems):
        send_sems, recv_sems, local_sems = sems
        x, y, c = lax.axis_index("x"), lax.axis_index("y"), lax.axis_index("c")
        me, sibling = (x, y, c), (x, y, 1 - c)
        chips = [(1 - x, y), (x, 1 - y), (1 - x, 1 - y)]

        def copy(a, k, block, to, src=None):
            rows = outs[a].at[_lin(block)]
            return pltpu.make_async_remote_copy(
                src_ref=rows if src is None else src, dst_ref=rows,
                send_sem=send_sems.at[a * NPEER + k], recv_sem=recv_sems.at[a * NPEER + k],
                device_id=to, device_id_type=pl.DeviceIdType.MESH)

        na = len(ins)
        mine = [pltpu.make_async_copy(ins[a], outs[a].at[_lin(me)], local_sems.at[a]) for a in range(na)]
        first = []
        for a in range(na):
            first.append(copy(a, 0, me, sibling, src=ins[a]))
            first += [copy(a, 1 + j, me, (*chip, c), src=ins[a]) for j, chip in enumerate(chips)]
        return copy, mine, first, me, sibling, chips, c, na

    def start(self, ins, outs, sems):
        _, mine, first, *_ = self._copies(ins, outs, sems)
        for cp in mine + first:
            cp.start()

    def finish(self, ins, outs, sems):
        copy, mine, first, me, sibling, chips, c, na = self._copies(ins, outs, sems)
        passed = []
        for j, chip in enumerate(chips):
            for a in range(na):
                copy(a, 1 + j, (*chip, c), me).wait_recv()
                cp = copy(a, 4 + j, (*chip, c), sibling)
                cp.start()
                passed.append(cp)
        for a in range(na):
            copy(a, 0, sibling, me).wait_recv()
            for j, chip in enumerate(chips):
                copy(a, 4 + j, (*chip, 1 - c), me).wait_recv()
        for cp in first + passed:
            cp.wait_send()
        for cp in mine:
            cp.wait()


class _Scatter:
    def __init__(self, arrs, ks=tuple(range(NDEV))):
        self.arrs = list(arrs)
        self.ks = tuple(ks)

    def out_shape(self):
        return [jax.ShapeDtypeStruct((len(self.ks),) + a.shape[1:], a.dtype) for a in self.arrs]

    def _copies(self, ins, outs, sems):
        send_sems, recv_sems, local_sems = sems
        x, y, c = lax.axis_index("x"), lax.axis_index("y"), lax.axis_index("c")
        me = (x, y, c)

        def peer(k):
            return (1 - x if k & 4 else x, 1 - y if k & 2 else y, 1 - c if k & 1 else c)

        local, remote = [], []
        for a in range(len(ins)):
            for i, k in enumerate(self.ks):
                if k == 0:
                    local.append(pltpu.make_async_copy(ins[a].at[_lin(me)], outs[a].at[i], local_sems.at[a]))
                else:
                    remote.append(pltpu.make_async_remote_copy(
                        src_ref=ins[a].at[_lin(peer(k))], dst_ref=outs[a].at[i],
                        send_sem=send_sems.at[a * NPEER + k - 1], recv_sem=recv_sems.at[a * NPEER + k - 1],
                        device_id=peer(k), device_id_type=pl.DeviceIdType.MESH))
        return local, remote

    def start(self, ins, outs, sems):
        local, remote = self._copies(ins, outs, sems)
        for cp in local + remote:
            cp.start()

    def finish(self, ins, outs, sems):
        local, remote = self._copies(ins, outs, sems)
        for cp in remote:
            cp.wait_recv()
        for cp in remote:
            cp.wait_send()
        for cp in local:
            cp.wait()


def _exchange_scratch(na):
    return [pltpu.SemaphoreType.DMA((na * NPEER,)), pltpu.SemaphoreType.DMA((na * NPEER,)),
            pltpu.SemaphoreType.DMA((na,))]


def _exchange_alone(ex, *, name, in_vmem=False):
    na = len(ex.arrs)

    def body(*refs):
        ins, outs, sems = refs[:na], refs[na:2 * na], refs[2 * na:]
        ex.start(ins, outs, sems)
        ex.finish(ins, outs, sems)

    spec = pl.BlockSpec(memory_space=pltpu.VMEM if in_vmem else pl.ANY)
    return pl.pallas_call(
        body, name=name, out_shape=tuple(ex.out_shape()), in_specs=[spec] * na, out_specs=tuple([spec] * na),
        scratch_shapes=_exchange_scratch(na),
        compiler_params=pltpu.CompilerParams(vmem_limit_bytes=VMEM_LIMIT),
    )(*ex.arrs)


def _call(body, *, name, grid, in_specs, out_specs, out_shape, args, sem, scratch=(), ride=None, aliases=None):
    n_in, n_out, n_scr = len(in_specs), len(out_specs), len(scratch)
    if ride is None:
        outs = pl.pallas_call(
            body, name=name, grid=grid, in_specs=list(in_specs), out_specs=tuple(out_specs),
            out_shape=tuple(out_shape), scratch_shapes=list(scratch), input_output_aliases=aliases or {},
            compiler_params=_params(*sem))(*args)
        return tuple(outs), ()
    nx = len(ride.arrs)
    hbm = pl.BlockSpec(memory_space=pl.ANY)

    def hosted(*refs):
        ins, x_in = refs[:n_in], refs[n_in:n_in + nx]
        o0 = n_in + nx
        outs, x_out = refs[o0:o0 + n_out], refs[o0 + n_out:o0 + n_out + nx]
        s0 = o0 + n_out + nx
        scr, x_sem = refs[s0:s0 + n_scr], refs[s0 + n_scr:]
        ids = [pl.program_id(i) for i in range(len(grid))]
        first = functools.reduce(jnp.logical_and, [i == 0 for i in ids])
        last = functools.reduce(jnp.logical_and, [i == g - 1 for i, g in zip(ids, grid)])

        @pl.when(first)
        def _():
            ride.start(x_in, x_out, x_sem)

        body(*ins, *outs, *scr)

        @pl.when(last)
        def _():
            ride.finish(x_in, x_out, x_sem)

    outs = pl.pallas_call(
        hosted, name=name, grid=grid, in_specs=list(in_specs) + [hbm] * nx,
        out_specs=tuple(list(out_specs) + [hbm] * nx), out_shape=tuple(list(out_shape) + ride.out_shape()),
        scratch_shapes=list(scratch) + _exchange_scratch(nx), input_output_aliases=aliases or {},
        compiler_params=_params(*(("arbitrary",) * len(grid))))(*args, *ride.arrs)
    return tuple(outs[:n_out]), tuple(outs[n_out:])


def _pack(parts):
    flat = jnp.concatenate([p.reshape(-1).astype(F32) for p in parts])
    tile = SUBLANES * LANES
    pad = (-flat.shape[0]) % tile
    return jnp.pad(flat, (0, pad)).reshape(-1, LANES)


def _unpack(buf, shapes):
    flat = buf.reshape(-1)
    out, off = [], 0
    for shp in shapes:
        size = math.prod(shp)
        out.append(flat[off:off + size].reshape(shp))
        off += size
    return out


KS_NEAR = (0, 1, 4, 5)
KS_FAR = ((2, 3), (6, 7))


def _local_step(x, target, wa, wo, ws, wos, rel_bias, conv_w, conv_b, dt_bias, a_log, d_skip, norm_w, ln_g, ln_b,
                dist=False):
    s, d = x.shape
    da = wo.shape[-2]
    heads = da // HEAD_DIM
    qkv_cols = 3 * N_GROUPS_ATTN * da
    d_inner = wos.shape[0] * (NDEV if dist else 1)
    conv_dim = conv_w.shape[1]
    ssm_heads = dt_bias.shape[1]
    hpg = ssm_heads // SSM_GROUPS
    gn = (conv_dim - d_inner) // 2
    n_state = gn // SSM_GROUPS
    gw = d_inner // SSM_GROUPS
    p = gw // hpg
    in_ssm = d_inner + conv_dim + ssm_heads
    xb = _cast_bf16(x, name="cast_x")

    def slabs_of_cols(t):
        return t.reshape(t.shape[0], NDEV, t.shape[1] // NDEV).transpose(1, 0, 2)

    if dist:
        qkv, (wo, ws_slabs) = _mm(xb, wa, name="mm_qkv", out_dtype=BF16, n_out=qkv_cols, ride=_Gather([wo, ws]))
        ws = ws_slabs.transpose(1, 0, 2).reshape(d, in_ssm)
    else:
        qkv = _mm(xb, wa, name="mm_qkv", out_dtype=BF16, n_out=qkv_cols)
    gate = _mm(xb, wa, name="mm_gate", out_dtype=F32, n_off=qkv_cols, n_out=da)
    bias, bucket = _bias_tables(rel_bias, heads)
    os_, ls_ = [], []
    for g, (_, dil) in enumerate(ATTN_PATTERNS):
        o, l = _attn_fwd_group(qkv, bias[g], g, dil, da)
        os_.append(o)
        ls_.append(l)
    o, lse, y = _attn_combine(os_, ls_, gate)
    h1 = _mm(y, wo, name="mm_out_attn", out_dtype=F32)
    x1, x1b = _ln_fwd(x, h1, ln_g[0:1], ln_b[0:1], name="ln1_fwd")

    if dist:
        proj, (wos_slabs,) = _mm(x1b, ws, name="mm_in_ssm", out_dtype=F32, ride=_Gather([wos]))
        wos = wos_slabs.reshape(d_inner, d)
    else:
        proj = _mm(x1b, ws, name="mm_in_ssm", out_dtype=F32)
    xbc = _conv_fwd(proj, conv_w, conv_b, d_inner)
    dt = _dt_fwd(proj, dt_bias, d_inner + conv_dim)
    dt_g = dt.reshape(s, SSM_GROUPS, hpg).transpose(1, 0, 2)
    dt_gt = dt.reshape(s, SSM_GROUPS, hpg).transpose(1, 2, 0)
    a = -jnp.exp(a_log)
    a_g = a.reshape(SSM_GROUPS, 1, hpg)
    a_gt = a.reshape(SSM_GROUPS, hpg, 1)
    dskip_e = jnp.repeat(d_skip.reshape(SSM_GROUPS, 1, hpg), p, axis=2)
    yn, yscan, states = _ssd_fwd(xbc, proj, dt_g, dt_gt, a_g, a_gt, dskip_e, norm_w, d_inner, n_state)
    h2 = _mm(yn, wos, name="mm_out_ssm", out_dtype=F32)

    du2, du2b, dg1, db1, loss_t = _ln_bwd(x1, h2, ln_g[1:2], ln_b[1:2], target, with_loss=True, name="ln2_loss_bwd")
    loss = loss_t[0, 0]
    dyn = _mm(du2b, wos, name="mm_dyn", out_dtype=F32, trans_b=True)
    g_wos = _mm(yn.T, du2b, name="mm_dw_out_ssm", out_dtype=BF16)
    parts = {}
    dyscan, dproj_ssm, g_norm, g_dskip = _ssd_epilogue_bwd(dyn, yscan, xbc, proj, dskip_e, norm_w, hpg)
    ride = _Scatter([g_wos.reshape(NDEV, d_inner // NDEV, d)]) if dist else None
    (dxs, d_bm, d_cm, ddt_g, g_a), rode = _ssd_scan_bwd(xbc, dt_g, dt_gt, a_g, a_gt, states, dyscan, dskip_e,
                                                         d_inner, n_state, ride=ride)
    parts["w_out_ssm"] = [list(rode)]
    g_alog = g_a.reshape(1, ssm_heads) * a
    dproj_ssm, g_conv_w, g_conv_b = _conv_bwd(proj, conv_w, conv_b, (dxs, d_bm, d_cm), d_inner, dproj_ssm)
    ddt = ddt_g.transpose(1, 0, 2).reshape(s, ssm_heads)
    dproj_ssm, g_dtb = _dt_bwd(proj, dt_bias, ddt, d_inner + conv_dim, dproj_ssm)
    g_ws = _mm(x1b.T, dproj_ssm, name="mm_dw_in_ssm", out_dtype=BF16)
    if dist:
        g_ws_slabs = slabs_of_cols(g_ws)
        dx1, near = _mm(dproj_ssm, ws, name="mm_dx1", out_dtype=F32, trans_b=True, res=du2,
                        res_scale=DEEPNORM_ALPHA, ride=_Scatter([g_ws_slabs], KS_NEAR))
    else:
        dx1 = _mm(dproj_ssm, ws, name="mm_dx1", out_dtype=F32, trans_b=True, res=du2, res_scale=DEEPNORM_ALPHA)

    du1, du1b, dg0, db0 = _ln_bwd(x, h1, ln_g[0:1], ln_b[0:1], dx1, with_loss=False, name="ln1_bwd")
    dy = _mm(du1b, wo, name="mm_dy", out_dtype=F32, trans_b=True)
    g_wo = _mm(y.T, du1b, name="mm_dw_out_attn", out_dtype=BF16, slab_out=NDEV)
    do, dgate, dd = _attn_bwd_prep(dy, o, gate)
    dparts, dss, far = [], [], []
    for g, (_, dil) in enumerate(ATTN_PATTERNS):
        ride = _Scatter([g_ws_slabs], KS_FAR[g]) if dist and g < len(KS_FAR) else None
        (dq, dk, dv, ds), rode = _attn_bwd_group(qkv, do, lse, dd, bias[g], g, dil, da, ride=ride)
        dparts += [dq, dk, dv]
        dss.append(ds)
        far += list(rode)
    g_bias = _bias_bwd(jnp.stack(dss), bucket)
    g_rel_bias = g_bias.transpose(2, 0, 1).reshape(NUM_BUCKETS, N_GROUPS_ATTN * heads)
    dproj_attn = jnp.concatenate(dparts + [dgate], axis=1)
    if dist:
        xbt = xb.T
        half = d // 2
        parts["w_in_ssm"] = [[near[0]] + far]
        g_top, rode = _mm(xbt[:half], dproj_attn, name="mm_dw_in_attn_top", out_dtype=BF16, slab_out=NDEV,
                          ride=_Scatter([g_wo]))
        parts["w_out_attn"] = [list(rode)]
        g_bot, top = _mm(xbt[half:], dproj_attn, name="mm_dw_in_attn_bottom", out_dtype=BF16, slab_out=NDEV,
                         ride=_Scatter([g_top]))
        dx, bot = _mm(dproj_attn, wa, name="mm_dx", out_dtype=F32, trans_b=True, res=du1,
                      res_scale=DEEPNORM_ALPHA, ride=_Scatter([g_bot]))
        parts["w_in_attn"] = [list(top), list(bot)]
    else:
        g_wa = _mm(xb.T, dproj_attn, name="mm_dw_in_attn", out_dtype=BF16, slab_out=NDEV)
        dx = _mm(dproj_attn, wa, name="mm_dx", out_dtype=F32, trans_b=True, res=du1, res_scale=DEEPNORM_ALPHA)

    g_ln_g = jnp.concatenate([dg0, dg1], axis=0)
    g_ln_b = jnp.concatenate([db0, db1], axis=0)
    small = dict(rel_bias=g_rel_bias, dt_bias=g_dtb, a_log=g_alog, d_skip=g_dskip.reshape(1, ssm_heads),
                 ln_g=g_ln_g, ln_b=g_ln_b, conv_w=g_conv_w, conv_b=g_conv_b, ssm_norm_w=g_norm)
    if dist:
        return loss, dx, parts, small
    return loss, dx, g_wa, g_wo, g_ws, g_wos, small


REPLICATED = ("rel_bias", "dt_bias", "a_log", "d_skip", "ln_g", "ln_b")
SHARDED_SMALL = ("conv_w", "conv_b", "ssm_norm_w")


def kernel(x, w_in_attn, w_out_attn, rel_bias, w_in_ssm, conv_w, conv_b, dt_bias, a_log, d_skip, ssm_norm_w, w_out_ssm, ln_g, ln_b, loss_target, m_w_in_attn, m_w_out_attn, m_rel_bias, m_w_in_ssm, m_conv_w, m_conv_b, m_dt_bias, m_a_log, m_d_skip, m_ssm_norm_w, m_w_out_ssm, m_ln_g, m_ln_b, v_w_in_attn, v_w_out_attn, v_rel_bias, v_w_in_ssm, v_conv_w, v_conv_b, v_dt_bias, v_a_log, v_d_skip, v_ssm_norm_w, v_w_out_ssm, v_ln_g, v_ln_b):
    w = dict(w_in_attn=w_in_attn, w_out_attn=w_out_attn, rel_bias=rel_bias, w_in_ssm=w_in_ssm, conv_w=conv_w,
             conv_b=conv_b, dt_bias=dt_bias, a_log=a_log, d_skip=d_skip, ssm_norm_w=ssm_norm_w,
             w_out_ssm=w_out_ssm, ln_g=ln_g, ln_b=ln_b)
    m = dict(w_in_attn=m_w_in_attn, w_out_attn=m_w_out_attn, rel_bias=m_rel_bias, w_in_ssm=m_w_in_ssm,
             conv_w=m_conv_w, conv_b=m_conv_b, dt_bias=m_dt_bias, a_log=m_a_log, d_skip=m_d_skip,
             ssm_norm_w=m_ssm_norm_w, w_out_ssm=m_w_out_ssm, ln_g=m_ln_g, ln_b=m_ln_b)
    v = dict(w_in_attn=v_w_in_attn, w_out_attn=v_w_out_attn, rel_bias=v_rel_bias, w_in_ssm=v_w_in_ssm,
             conv_w=v_conv_w, conv_b=v_conv_b, dt_bias=v_dt_bias, a_log=v_a_log, d_skip=v_d_skip,
             ssm_norm_w=v_ssm_norm_w, w_out_ssm=v_w_out_ssm, ln_g=v_ln_g, ln_b=v_ln_b)
    me = _lin((lax.axis_index("x"), lax.axis_index("y"), lax.axis_index("c")))
    d = x.shape[2]
    big = ("w_in_attn", "w_out_attn", "w_in_ssm", "w_out_ssm")

    shards = {k: _cast_bf16(w[k][0], name=f"cast_{k}") for k in big}
    (wa,) = _exchange_alone(_Gather([shards["w_in_attn"]]), name="gather_w_in_attn")
    cpd = conv_w.shape[2]
    npd = ssm_norm_w.shape[1]
    small_shapes = [(CONV_WIDTH, cpd), (1, cpd), (1, npd)]
    (small_all,) = _exchange_alone(_Gather([_pack([conv_w[0], conv_b, ssm_norm_w])]), name="gather_small_weights",
                                   in_vmem=True)
    small_parts = [_unpack(small_all[i], small_shapes) for i in range(NDEV)]
    conv_w_full = jnp.concatenate([p[0] for p in small_parts], axis=1)
    conv_b_full = jnp.concatenate([p[1] for p in small_parts], axis=1)
    norm_w_full = jnp.concatenate([p[2] for p in small_parts], axis=1)

    loss, dx, parts, small = _local_step(
        x[0], loss_target[0], wa, shards["w_out_attn"], shards["w_in_ssm"], shards["w_out_ssm"], rel_bias,
        conv_w_full, conv_b_full, dt_bias[0:1], a_log[0:1], d_skip[0:1], norm_w_full, ln_g, ln_b, dist=True)
    loss = lax.psum(loss, MESH_AXES)
    out = {}
    for k in big:
        out[k] = _adamw_sum(parts[k], w[k][0], m[k][0], v[k][0], name=f"adamw_{k}")

    order = REPLICATED + SHARDED_SMALL
    g_shapes = [small[k].shape for k in order]
    (g_all,) = _exchange_alone(_Gather([_pack([small[k] for k in order])]), name="gather_small_grads", in_vmem=True)
    g_sum = dict(zip(order, _unpack(_sum_slots(g_all, name="sum_small_grads"), g_shapes)))
    g_mine = {k: g_sum[k] for k in REPLICATED}
    g_mine["conv_w"] = lax.dynamic_slice_in_dim(g_sum["conv_w"], me * cpd, cpd, axis=1)
    g_mine["conv_b"] = lax.dynamic_slice_in_dim(g_sum["conv_b"], me * cpd, cpd, axis=1)
    g_mine["ssm_norm_w"] = lax.dynamic_slice_in_dim(g_sum["ssm_norm_w"], me * npd, npd, axis=1)
    w_shapes = [w[k].shape for k in order]
    g_pack = _pack([g_mine[k] for k in order])
    d_p, m_p, v_p = _adamw_small(g_pack, _pack([w[k] for k in order]), _pack([m[k] for k in order]),
                                 _pack([v[k] for k in order]), name="adamw_small")
    for k, gk, dk, mk, vk in zip(order, _unpack(g_pack, w_shapes), _unpack(d_p, w_shapes), _unpack(m_p, w_shapes),
                                 _unpack(v_p, w_shapes)):
        out[k] = (gk, dk, mk, vk)
    for k in big:
        out[k] = tuple(t[None] for t in out[k])

    names = ("w_in_attn", "w_out_attn", "rel_bias", "w_in_ssm", "conv_w", "conv_b", "dt_bias", "a_log", "d_skip",
             "ssm_norm_w", "w_out_ssm", "ln_g", "ln_b")
    res = [loss, dx[None]]
    for i in range(4):
        res += [out[k][i] for k in names]
    return tuple(res)
```

```python
import functools
import math

import jax
import jax.numpy as jnp
from jax import lax
from jax.experimental import pallas as pl
from jax.experimental.pallas import tpu as pltpu

F32 = jnp.float32
BF16 = jnp.bfloat16
MESH_AXES = ("x", "y", "c")
NDEV = 8
NPEER = NDEV - 1
LANES = 128
SUBLANES = 8
VMEM_LIMIT = 52 * 1024 * 1024
MM_VMEM_BUDGET = 40 * 1024 * 1024
MM_TK_MAX = 4096
MM_TN_MAX = 1024

ATTN_PATTERNS = ((128, 1), (512, 4), (2048, 16))
N_GROUPS_ATTN = 3
HEAD_DIM = 128
ATTN_BLOCK = 128
NUM_BUCKETS = 32
MAX_DISTANCE = 2048
SSM_GROUPS = 8
CONV_WIDTH = 4
CHUNK = 128
DEPTH = 2
DEEPNORM_ALPHA = (2 * DEPTH) ** 0.25
LN_EPS = 1e-5
RMS_EPS = 1e-5
NEG_INF = -1e30
ADAM_LR = 0.001
ADAM_B1 = 0.9
ADAM_B2 = 0.999
ADAM_EPS = 1e-08
ADAM_WD = 0.01
ADAM_STEP = 10
HIGHEST = lax.Precision.HIGHEST


def _params(*sem):
    return pltpu.CompilerParams(dimension_semantics=sem, vmem_limit_bytes=VMEM_LIMIT)


def _pick(n, prefs):
    for p in prefs:
        if n % p == 0:
            return p
    return n


def _dot(a, b):
    return jnp.dot(a, b, preferred_element_type=F32)


def _dot_nt(a, b):
    return lax.dot_general(a, b, (((1,), (1,)), ((), ())), preferred_element_type=F32)


def _dot_tn(a, b):
    return lax.dot_general(a, b, (((0,), (0,)), ((), ())), preferred_element_type=F32)


def _sigmoid(x):
    return 1.0 / (1.0 + jnp.exp(-x))


def _mm(a, b, *, name, out_dtype, trans_b=False, slab_out=0, n_off=0, n_out=None,
        res=None, res_scale=1.0, ride=None):
    m, k = a.shape
    slab_b = b.ndim == 3
    if slab_b:
        ns = b.shape[0]
        if trans_b:
            n, kper = b.shape[1], b.shape[2]
            assert ns * kper == k
        else:
            nper = b.shape[2]
            n = ns * nper
            assert b.shape[1] == k
    else:
        n = b.shape[0] if trans_b else b.shape[1]
        assert (b.shape[1] if trans_b else b.shape[0]) == k
    n_out = n if n_out is None else n_out
    tm = _pick(m, (1024, 512, 256, 128))
    nconstraint = math.gcd(n_out, n_off) if n_off else n_out
    if slab_b and not trans_b:
        nconstraint = math.gcd(nconstraint, nper)
    if slab_out:
        nconstraint = math.gcd(nconstraint, n_out // slab_out)
    kconstraint = kper if (slab_b and trans_b) else k
    tk = max(t for t in range(LANES, min(kconstraint, MM_TK_MAX) + 1, LANES) if kconstraint % t == 0)
    nk = k // tk
    out_bytes = jnp.dtype(out_dtype).itemsize

    def vmem_bytes(t):
        return (2 * 2 * tk * (tm + t) + 2 * tm * t * out_bytes + (4 * tm * t if nk > 1 else 0)
                + (2 * 4 * tm * t if res is not None else 0))

    fits = [t for t in range(LANES, min(nconstraint, MM_TN_MAX) + 1, LANES)
            if nconstraint % t == 0 and vmem_bytes(t) <= MM_VMEM_BUDGET]
    tn = max(fits)
    nb0 = n_off // tn
    grid = (m // tm, n_out // tn, nk)

    a_spec = pl.BlockSpec((tm, tk), lambda i, j, kk: (i, kk))
    if slab_b and not trans_b:
        nps = nper // tn
        b_spec = pl.BlockSpec((None, tk, tn), lambda i, j, kk: ((j + nb0) // nps, kk, (j + nb0) % nps))
    elif slab_b and trans_b:
        kps = kper // tk
        b_spec = pl.BlockSpec((None, tn, tk), lambda i, j, kk: (kk // kps, j + nb0, kk % kps))
    elif trans_b:
        b_spec = pl.BlockSpec((tn, tk), lambda i, j, kk: (j + nb0, kk))
    else:
        b_spec = pl.BlockSpec((tk, tn), lambda i, j, kk: (kk, j + nb0))
    if slab_out:
        ops = (n_out // slab_out) // tn
        o_spec = pl.BlockSpec((None, tm, tn), lambda i, j, kk: (j // ops, i, j % ops))
        o_shape = jax.ShapeDtypeStruct((slab_out, m, n_out // slab_out), out_dtype)
    else:
        o_spec = pl.BlockSpec((tm, tn), lambda i, j, kk: (i, j))
        o_shape = jax.ShapeDtypeStruct((m, n_out), out_dtype)
    in_specs = [a_spec, b_spec]
    args = [a, b]
    if res is not None:
        in_specs.append(pl.BlockSpec((tm, tn), lambda i, j, kk: (i, j)))
        args.append(res)

    def body(*refs):
        a_ref, b_ref = refs[0], refs[1]
        r_ref = refs[2] if res is not None else None
        o_ref = refs[3] if res is not None else refs[2]
        av = a_ref[...].astype(BF16)
        bv = b_ref[...].astype(BF16)
        part = _dot_nt(av, bv) if trans_b else _dot(av, bv)

        def finish(r):
            if res is not None:
                r = r + res_scale * r_ref[...]
            o_ref[...] = r.astype(out_dtype)

        if nk == 1:
            finish(part)
            return
        acc = refs[-1]
        kk = pl.program_id(2)

        @pl.when(kk == 0)
        def _():
            acc[...] = part

        @pl.when(kk > 0)
        def _():
            acc[...] += part

        @pl.when(kk == nk - 1)
        def _():
            finish(acc[...])

    outs, rode = _call(
        body, name=name, grid=grid, in_specs=in_specs, out_specs=[o_spec], out_shape=[o_shape], args=args,
        scratch=[pltpu.VMEM((tm, tn), F32)] if nk > 1 else [], ride=ride,
        sem=("parallel", "parallel", "arbitrary"))
    return (outs[0], rode) if ride is not None else outs[0]


def _cast_bf16(w, *, name):
    r, c = w.shape
    tr = _pick(r, (512, 256, 128, 64, 32, 16, 8))

    def body(w_ref, o_ref):
        o_ref[...] = w_ref[...].astype(BF16)

    return pl.pallas_call(
        body, name=name, grid=(r // tr,),
        in_specs=[pl.BlockSpec((tr, c), lambda i: (i, 0))],
        out_specs=pl.BlockSpec((tr, c), lambda i: (i, 0)),
        out_shape=jax.ShapeDtypeStruct((r, c), BF16),
        compiler_params=_params("parallel"),
    )(w)


def _adam_math(w, g, m, v):
    m2 = ADAM_B1 * m + (1.0 - ADAM_B1) * g
    v2 = ADAM_B2 * v + (1.0 - ADAM_B2) * (g * g)
    m_hat = m2 / (1.0 - ADAM_B1 ** ADAM_STEP)
    v_hat = v2 / (1.0 - ADAM_B2 ** ADAM_STEP)
    delta = -ADAM_LR * (m_hat / (jnp.sqrt(v_hat) + ADAM_EPS) + ADAM_WD * w)
    return delta, m2, v2


def _adamw_sum(bands, w, m, v, *, name, ride=None):
    r, c = w.shape
    nband = len(bands)
    rows = r // nband
    tr = _pick(rows, (128, 64, 32, 16, 8))
    tc = c if (c % LANES or c <= 2560) else _pick(c, (2048, 1024, 512, 256, 128))
    nt = rows // tr
    flat = [p for band in bands for p in band]

    def body(*refs):
        p_refs = refs[:len(flat)]
        w_ref, m_ref, v_ref, g_out, d_out, m_out, v_out = refs[len(flat):]
        i = pl.program_id(0)
        g, at = None, 0
        for q, band in enumerate(bands):
            gq = None
            for p_ref in p_refs[at:at + len(band)]:
                for s in range(p_ref.shape[0]):
                    t = p_ref[s].astype(F32)
                    gq = t if gq is None else gq + t
            at += len(band)
            g = gq if q == 0 else jnp.where(i >= q * nt, gq, g)
        d, m2, v2 = _adam_math(w_ref[...], g, m_ref[...], v_ref[...])
        g_out[...] = g
        d_out[...] = d
        m_out[...] = m2
        v_out[...] = v2

    def band_spec(p, q):
        return pl.BlockSpec((p.shape[0], tr, tc), lambda i, j: (0, jnp.clip(i - q * nt, 0, nt - 1), j))

    spec = pl.BlockSpec((tr, tc), lambda i, j: (i, j))
    shp = jax.ShapeDtypeStruct((r, c), F32)
    outs, rode = _call(
        body, name=name, grid=(r // tr, c // tc),
        in_specs=[band_spec(p, q) for q, band in enumerate(bands) for p in band] + [spec, spec, spec],
        out_specs=[spec, spec, spec, spec], out_shape=[shp, shp, shp, shp], args=(*flat, w, m, v),
        sem=("parallel", "parallel"), ride=ride)
    return (outs, rode) if ride is not None else outs


def _adamw_small(g, w, m, v, *, name):
    shp = jax.ShapeDtypeStruct(w.shape, F32)

    def body(g_ref, w_ref, m_ref, v_ref, d_out, m_out, v_out):
        d, m2, v2 = _adam_math(w_ref[...], g_ref[...], m_ref[...], v_ref[...])
        d_out[...] = d
        m_out[...] = m2
        v_out[...] = v2

    return pl.pallas_call(body, name=name, out_shape=(shp, shp, shp),
                          compiler_params=pltpu.CompilerParams(vmem_limit_bytes=VMEM_LIMIT))(g, w, m, v)


def _sum_slots(parts, *, name):
    _, r, c = parts.shape

    def body(p_ref, o_ref):
        g = p_ref[0]
        for s in range(1, NDEV):
            g = g + p_ref[s]
        o_ref[...] = g

    return pl.pallas_call(body, name=name, out_shape=jax.ShapeDtypeStruct((r, c), F32),
                          compiler_params=pltpu.CompilerParams(vmem_limit_bytes=VMEM_LIMIT))(parts)


def _ln_parts(u):
    mu = jnp.mean(u, axis=-1, keepdims=True)
    xc = u - mu
    var = jnp.mean(xc * xc, axis=-1, keepdims=True)
    rstd = lax.rsqrt(var + LN_EPS)
    return xc * rstd, rstd


def _ln_fwd(xin, h, g, b, *, name):
    s, d = xin.shape
    tm = _pick(s, (128,))

    def body(x_ref, h_ref, g_ref, b_ref, o_ref, ob_ref):
        xhat, _ = _ln_parts(DEEPNORM_ALPHA * x_ref[...] + h_ref[...])
        o = xhat * g_ref[...] + b_ref[...]
        o_ref[...] = o
        ob_ref[...] = o.astype(BF16)

    row = pl.BlockSpec((tm, d), lambda i: (i, 0))
    vec = pl.BlockSpec((1, d), lambda i: (0, 0))
    return pl.pallas_call(
        body, name=name, grid=(s // tm,), in_specs=[row, row, vec, vec], out_specs=(row, row),
        out_shape=(jax.ShapeDtypeStruct((s, d), F32), jax.ShapeDtypeStruct((s, d), BF16)),
        compiler_params=_params("parallel"),
    )(xin, h, g, b)


def _ln_bwd(xin, h, g, b, cot, *, with_loss, name):
    s, d = xin.shape
    tm = _pick(s, (128,))

    def body(x_ref, h_ref, g_ref, b_ref, c_ref, du_ref, dub_ref, dg_ref, db_ref, *rest):
        i = pl.program_id(0)
        xhat, rstd = _ln_parts(DEEPNORM_ALPHA * x_ref[...] + h_ref[...])
        gv = g_ref[...]
        if with_loss:
            diff = xhat * gv + b_ref[...] - c_ref[...]
            part = 0.5 * jnp.sum(jnp.mean(diff * diff, axis=-1, keepdims=True), axis=0, keepdims=True)
            dout = diff / d
        else:
            dout = c_ref[...]

        @pl.when(i == 0)
        def _():
            dg_ref[...] = jnp.zeros_like(dg_ref)
            db_ref[...] = jnp.zeros_like(db_ref)
            if with_loss:
                rest[0][...] = jnp.zeros_like(rest[0])

        dg_ref[...] += jnp.sum(dout * xhat, axis=0, keepdims=True)
        db_ref[...] += jnp.sum(dout, axis=0, keepdims=True)
        if with_loss:
            rest[0][...] += jnp.broadcast_to(part, rest[0].shape)
        dxh = dout * gv
        du = rstd * (dxh - jnp.mean(dxh, axis=-1, keepdims=True)
                     - xhat * jnp.mean(dxh * xhat, axis=-1, keepdims=True))
        du_ref[...] = du
        dub_ref[...] = du.astype(BF16)

    row = pl.BlockSpec((tm, d), lambda i: (i, 0))
    vec = pl.BlockSpec((1, d), lambda i: (0, 0))
    out_specs = [row, row, vec, vec]
    out_shape = [jax.ShapeDtypeStruct((s, d), F32), jax.ShapeDtypeStruct((s, d), BF16),
                 jax.ShapeDtypeStruct((1, d), F32), jax.ShapeDtypeStruct((1, d), F32)]
    if with_loss:
        out_specs.append(pl.BlockSpec((SUBLANES, LANES), lambda i: (0, 0)))
        out_shape.append(jax.ShapeDtypeStruct((SUBLANES, LANES), F32))
    return pl.pallas_call(
        body, name=name, grid=(s // tm,), in_specs=[row, row, vec, vec, row],
        out_specs=tuple(out_specs), out_shape=tuple(out_shape),
        compiler_params=_params("arbitrary"),
    )(xin, h, g, b, cot)


def t5_causal_bucket(dist):
    max_exact = NUM_BUCKETS // 2
    d_f = jnp.maximum(dist, 1).astype(jnp.float32)
    large = max_exact + (jnp.log(d_f / max_exact) / math.log(MAX_DISTANCE / max_exact)
                         * (NUM_BUCKETS - max_exact)).astype(jnp.int32)
    large = jnp.minimum(large, NUM_BUCKETS - 1)
    return jnp.where(dist < max_exact, dist, large)


def _bias_tables(rel_bias, heads):
    qi = lax.broadcasted_iota(jnp.int32, (ATTN_BLOCK, 2 * ATTN_BLOCK), 0)
    ki = lax.broadcasted_iota(jnp.int32, (ATTN_BLOCK, 2 * ATTN_BLOCK), 1)
    delta = ATTN_BLOCK + qi - ki
    buckets = []
    for window, dilation in ATTN_PATTERNS:
        span = window // dilation
        assert span == ATTN_BLOCK
        band = (delta >= 0) & (delta <= span)
        buckets.append(jnp.where(band, t5_causal_bucket(jnp.clip(delta, 0, None) * dilation), -1))
    bucket = jnp.stack(buckets).astype(jnp.int32)

    def body(bk_ref, tbl_ref, o_ref):
        col = pl.program_id(0) * heads + pl.program_id(1)
        bk = bk_ref[...]
        acc = jnp.full(bk.shape, NEG_INF, F32)
        for b in range(NUM_BUCKETS):
            acc = jnp.where(bk == b, tbl_ref[b, col], acc)
        o_ref[...] = acc

    tile = (None, ATTN_BLOCK, 2 * ATTN_BLOCK)
    bias = pl.pallas_call(
        body, name="bias_fwd", grid=(N_GROUPS_ATTN, heads),
        in_specs=[pl.BlockSpec(tile, lambda g, h: (g, 0, 0)), pl.BlockSpec(memory_space=pltpu.SMEM)],
        out_specs=pl.BlockSpec((None,) + tile, lambda g, h: (g, h, 0, 0)),
        out_shape=jax.ShapeDtypeStruct((N_GROUPS_ATTN, heads, ATTN_BLOCK, 2 * ATTN_BLOCK), F32),
        compiler_params=_params("parallel", "parallel"),
    )(bucket, rel_bias)
    return bias, bucket


def _attn_logits(q, kc, kp, bias_c, bias_p, j):
    scale = HEAD_DIM ** -0.5
    sc = _dot_nt(q, kc) * scale + bias_c
    sp = _dot_nt(q, kp) * scale + jnp.where(j > 0, bias_p, NEG_INF)
    return sc, sp


def _dilated_view(qkv, g, dilation, da):
    heads = da // HEAD_DIM
    if dilation == 1:
        return qkv, qkv.shape[1] // HEAD_DIM, g * 3 * heads
    sub = qkv[:, g * 3 * da:(g + 1) * 3 * da]
    return sub.reshape(qkv.shape[0] // dilation, dilation * 3 * da), 3 * heads, 0


def _attn_fwd_group(qkv, bias_g, g, dilation, da, ride=None):
    s = qkv.shape[0]
    heads = da // HEAD_DIM
    l = s // dilation
    nb = l // ATTN_BLOCK
    view, cpb, base = _dilated_view(qkv, g, dilation, da)

    def body(q_ref, k_ref, v_ref, b_ref, o_ref, l_ref):
        bias_p = b_ref[0, :, 0:ATTN_BLOCK]
        bias_c = b_ref[0, :, ATTN_BLOCK:2 * ATTN_BLOCK]

        def step(j, carry):
            r0 = pl.multiple_of(j * ATTN_BLOCK, ATTN_BLOCK)
            rp = pl.multiple_of(jnp.maximum(j - 1, 0) * ATTN_BLOCK, ATTN_BLOCK)
            q = q_ref[pl.ds(r0, ATTN_BLOCK), :]
            sc, sp = _attn_logits(q, k_ref[pl.ds(r0, ATTN_BLOCK), :], k_ref[pl.ds(rp, ATTN_BLOCK), :],
                                  bias_c, bias_p, j)
            mx = jnp.maximum(jnp.max(sc, axis=-1, keepdims=True), jnp.max(sp, axis=-1, keepdims=True))
            pc = jnp.exp(sc - mx)
            pp = jnp.exp(sp - mx)
            den = jnp.sum(pc, axis=-1, keepdims=True) + jnp.sum(pp, axis=-1, keepdims=True)
            inv = 1.0 / den
            o = (_dot((pc * inv).astype(BF16), v_ref[pl.ds(r0, ATTN_BLOCK), :])
                 + _dot((pp * inv).astype(BF16), v_ref[pl.ds(rp, ATTN_BLOCK), :]))
            o_ref[pl.ds(r0, ATTN_BLOCK), :] = o
            l_ref[pl.ds(r0, ATTN_BLOCK), :] = jnp.broadcast_to(mx + jnp.log(den), (ATTN_BLOCK, HEAD_DIM))
            return carry

        lax.fori_loop(0, nb, step, 0)

    def col(t):
        return lambda r, h: (0, r * cpb + base + t * heads + h)

    blk = (l, HEAD_DIM)
    out = pl.BlockSpec(blk, lambda r, h: (0, r * heads + h))
    shp = jax.ShapeDtypeStruct((l, dilation * da), F32)
    (o, lse), rode = _call(
        body, name=f"attn_fwd_g{g}", grid=(dilation, heads),
        in_specs=[pl.BlockSpec(blk, col(0)), pl.BlockSpec(blk, col(1)), pl.BlockSpec(blk, col(2)),
                  pl.BlockSpec((1, ATTN_BLOCK, 2 * ATTN_BLOCK), lambda r, h: (h, 0, 0))],
        out_specs=[out, out], out_shape=[shp, shp], args=(view, view, view, bias_g),
        sem=("parallel", "parallel"), ride=ride)
    return o.reshape(s, da), lse.reshape(s, da), rode


def _attn_combine(os_, ls_, gate):
    s, da = gate.shape
    tm = _pick(s, (512, 256, 128))
    tc = _pick(da, (512, 256, 128))

    def body(o0, o1, o2, l0, l1, l2, g_ref, o_ref, l_ref, y_ref):
        a0, a1, a2 = l0[...], l1[...], l2[...]
        mx = jnp.maximum(jnp.maximum(a0, a1), a2)
        e0, e1, e2 = jnp.exp(a0 - mx), jnp.exp(a1 - mx), jnp.exp(a2 - mx)
        den = e0 + e1 + e2
        o = (e0 * o0[...] + e1 * o1[...] + e2 * o2[...]) / den
        gv = g_ref[...]
        o_ref[...] = o
        l_ref[...] = mx + jnp.log(den)
        y_ref[...] = (o * (gv * _sigmoid(gv))).astype(BF16)

    spec = pl.BlockSpec((tm, tc), lambda i, j: (i, j))
    return pl.pallas_call(
        body, name="attn_combine", grid=(s // tm, da // tc), in_specs=[spec] * 7, out_specs=(spec, spec, spec),
        out_shape=(jax.ShapeDtypeStruct((s, da), F32), jax.ShapeDtypeStruct((s, da), F32),
                   jax.ShapeDtypeStruct((s, da), BF16)),
        compiler_params=_params("parallel", "parallel"),
    )(*os_, *ls_, gate)


def _attn_bwd_prep(dy, o, gate):
    s, da = gate.shape
    tm = _pick(s, (512, 256, 128))

    def body(dy_ref, o_ref, g_ref, do_ref, dg_ref, dd_ref):
        gv = g_ref[...]
        sg = _sigmoid(gv)
        dyv = dy_ref[...]
        ov = o_ref[...]
        do = dyv * (gv * sg)
        do_ref[...] = do.astype(BF16)
        dg_ref[...] = (dyv * ov * (sg * (1.0 + gv * (1.0 - sg)))).astype(BF16)
        dd_ref[...] = jnp.broadcast_to(jnp.sum(do * ov, axis=-1, keepdims=True), (tm, HEAD_DIM))

    spec = pl.BlockSpec((tm, HEAD_DIM), lambda i, j: (i, j))
    return pl.pallas_call(
        body, name="attn_bwd_prep", grid=(s // tm, da // HEAD_DIM), in_specs=[spec] * 3,
        out_specs=(spec, spec, spec),
        out_shape=(jax.ShapeDtypeStruct((s, da), BF16), jax.ShapeDtypeStruct((s, da), BF16),
                   jax.ShapeDtypeStruct((s, da), F32)),
        compiler_params=_params("parallel", "parallel"),
    )(dy, o, gate)


def _attn_bwd_group(qkv, do, lse, dd, bias_g, g, dilation, da, ride=None):
    s = qkv.shape[0]
    heads = da // HEAD_DIM
    l = s // dilation
    nb = l // ATTN_BLOCK
    view, cpb, base = _dilated_view(qkv, g, dilation, da)
    scale = HEAD_DIM ** -0.5

    def body(q_ref, k_ref, v_ref, do_ref, l_ref, dd_ref, b_ref, dq_ref, dk_ref, dv_ref, ds_ref, dk_acc, dv_acc):
        r = pl.program_id(1)
        bias_p = b_ref[0, :, 0:ATTN_BLOCK]
        bias_c = b_ref[0, :, ATTN_BLOCK:2 * ATTN_BLOCK]

        @pl.when(r == 0)
        def _():
            ds_ref[...] = jnp.zeros_like(ds_ref)

        dk_acc[...] = jnp.zeros_like(dk_acc)
        dv_acc[...] = jnp.zeros_like(dv_acc)

        def step(j, carry):
            r0 = pl.multiple_of(j * ATTN_BLOCK, ATTN_BLOCK)
            rp = pl.multiple_of(jnp.maximum(j - 1, 0) * ATTN_BLOCK, ATTN_BLOCK)
            cur = pl.ds(r0, ATTN_BLOCK)
            prev = pl.ds(rp, ATTN_BLOCK)
            q = q_ref[cur, :]
            kc, kp = k_ref[cur, :], k_ref[prev, :]
            dov = do_ref[cur, :]
            sc, sp = _attn_logits(q, kc, kp, bias_c, bias_p, j)
            lrow = l_ref[cur, 0:1]
            drow = dd_ref[cur, 0:1]
            pc = jnp.exp(sc - lrow)
            pp = jnp.exp(sp - lrow)
            dsc = pc * (_dot_nt(dov, v_ref[cur, :]) - drow)
            dsp = pp * (_dot_nt(dov, v_ref[prev, :]) - drow)
            ds_ref[0, :, ATTN_BLOCK:2 * ATTN_BLOCK] += dsc
            ds_ref[0, :, 0:ATTN_BLOCK] += dsp
            dscb, dspb = dsc.astype(BF16), dsp.astype(BF16)
            dq_ref[cur, :] = ((_dot(dscb, kc) + _dot(dspb, kp)) * scale).astype(BF16)
            dk_acc[cur, :] += _dot_tn(dscb, q) * scale
            dk_acc[prev, :] += _dot_tn(dspb, q) * scale
            dv_acc[cur, :] += _dot_tn(pc.astype(BF16), dov)
            dv_acc[prev, :] += _dot_tn(pp.astype(BF16), dov)
            return carry

        lax.fori_loop(0, nb, step, 0)
        dk_ref[...] = dk_acc[...].astype(BF16)
        dv_ref[...] = dv_acc[...].astype(BF16)

    def col(t):
        return lambda h, r: (0, r * cpb + base + t * heads + h)

    blk = (l, HEAD_DIM)
    act = pl.BlockSpec(blk, lambda h, r: (0, r * heads + h))
    shp = jax.ShapeDtypeStruct((l, dilation * da), BF16)
    actv = lambda t: t.reshape(l, dilation * da)
    (dq, dk, dv, ds), rode = _call(
        body, name=f"attn_bwd_g{g}", grid=(heads, dilation),
        in_specs=[pl.BlockSpec(blk, col(0)), pl.BlockSpec(blk, col(1)), pl.BlockSpec(blk, col(2)), act, act, act,
                  pl.BlockSpec((1, ATTN_BLOCK, 2 * ATTN_BLOCK), lambda h, r: (h, 0, 0))],
        out_specs=[act, act, act, pl.BlockSpec((1, ATTN_BLOCK, 2 * ATTN_BLOCK), lambda h, r: (h, 0, 0))],
        out_shape=[shp, shp, shp, jax.ShapeDtypeStruct((heads, ATTN_BLOCK, 2 * ATTN_BLOCK), F32)],
        scratch=[pltpu.VMEM(blk, F32), pltpu.VMEM(blk, F32)], sem=("parallel", "arbitrary"), ride=ride,
        args=(view, view, view, actv(do), actv(lse), actv(dd), bias_g))
    return (dq.reshape(s, da), dk.reshape(s, da), dv.reshape(s, da), ds), rode


def _bias_bwd(ds, bucket):
    ng, heads = ds.shape[0], ds.shape[1]

    def body(ds_ref, bk_ref, o_ref):
        bk = bk_ref[...]
        x = ds_ref[...]
        for b in range(NUM_BUCKETS):
            o_ref[:, b:b + 1] = jnp.sum(jnp.where(bk == b, x, 0.0), axis=(0, 1), keepdims=True)

    tile = (None, ATTN_BLOCK, 2 * ATTN_BLOCK)
    out = pl.pallas_call(
        body, name="bias_bwd", grid=(ng, heads),
        in_specs=[pl.BlockSpec((None,) + tile, lambda g, h: (g, h, 0, 0)), pl.BlockSpec(tile, lambda g, h: (g, 0, 0))],
        out_specs=pl.BlockSpec((None, None, 1, NUM_BUCKETS), lambda g, h: (g, h, 0, 0)),
        out_shape=jax.ShapeDtypeStruct((ng, heads, 1, NUM_BUCKETS), F32),
        compiler_params=_params("parallel", "parallel"),
    )(ds, bucket)
    return out.reshape(ng, heads, NUM_BUCKETS)


def _shift_rows(x, halo, s):
    r = pltpu.roll(x, s, axis=0)
    rh = pltpu.roll(halo, s, axis=0)
    row = lax.broadcasted_iota(jnp.int32, halo.shape, 0)
    top = jnp.where(row < s, rh, r[0:SUBLANES])
    return jnp.concatenate([top, r[SUBLANES:]], axis=0)


def _conv_out(x, halo, w, b):
    acc = b + w[CONV_WIDTH - 1:CONV_WIDTH] * x
    for kk in range(CONV_WIDTH - 1):
        acc = acc + w[kk:kk + 1] * _shift_rows(x, halo, CONV_WIDTH - 1 - kk)
    return acc


def _conv_fwd(proj, conv_w, conv_b, col0):
    s = proj.shape[0]
    c = conv_w.shape[1]
    ts = _pick(s, (512, 256, 128))
    tc = _pick(math.gcd(c, col0), (512, 256, 128))
    cb0 = col0 // tc
    hb = ts // SUBLANES

    def body(x_ref, h_ref, w_ref, b_ref, o_ref):
        i = pl.program_id(0)
        halo = jnp.where(i > 0, h_ref[...], 0.0)
        u = _conv_out(x_ref[...], halo, w_ref[...], b_ref[...])
        o_ref[...] = u * _sigmoid(u)

    return pl.pallas_call(
        body, name="conv_fwd", grid=(s // ts, c // tc),
        in_specs=[pl.BlockSpec((ts, tc), lambda i, j: (i, cb0 + j)),
                  pl.BlockSpec((SUBLANES, tc), lambda i, j: (jnp.maximum(i * hb - 1, 0), cb0 + j)),
                  pl.BlockSpec((CONV_WIDTH, tc), lambda i, j: (0, j)),
                  pl.BlockSpec((1, tc), lambda i, j: (0, j))],
        out_specs=pl.BlockSpec((ts, tc), lambda i, j: (i, j)),
        out_shape=jax.ShapeDtypeStruct((s, c), F32),
        compiler_params=_params("parallel", "parallel"),
    )(proj, proj, conv_w, conv_b)


def _conv_bwd(proj, conv_w, conv_b, dacts, col0, dproj):
    s = proj.shape[0]
    c = conv_w.shape[1]
    widths = [d.shape[1] for d in dacts]
    assert sum(widths) == c
    ts = _pick(s, (512, 256, 128))
    tc = _pick(math.gcd(math.gcd(c, col0), math.gcd(*widths)), (512, 256, 128))
    cb0 = col0 // tc
    hb = ts // SUBLANES
    nblk = s // ts
    ext = ts + SUBLANES
    nb = [wd // tc for wd in widths]
    starts = [0, nb[0], nb[0] + nb[1]]

    def body(x_ref, xp_ref, xn_ref, d0, d1, d2, n0, n1, n2, w_ref, b_ref, _, dx_ref, dw_ref, db_ref):
        j = pl.program_id(0)
        i = pl.program_id(1)
        last = i == nblk - 1
        w = w_ref[...]
        halo = jnp.where(i > 0, xp_ref[...], 0.0)
        x = x_ref[...]
        xe = jnp.concatenate([x, xn_ref[...]], axis=0)
        dcur = jnp.where(j < starts[1], d0[...], jnp.where(j < starts[2], d1[...], d2[...]))
        dnext = jnp.where(j < starts[1], n0[...], jnp.where(j < starts[2], n1[...], n2[...]))
        de = jnp.concatenate([dcur, jnp.where(last, 0.0, dnext)], axis=0)
        u = _conv_out(xe, halo, w, b_ref[...])
        sg = _sigmoid(u)
        dpre = de * (sg * (1.0 + u * (1.0 - sg)))
        dx = w[CONV_WIDTH - 1:CONV_WIDTH] * dpre[0:ts]
        for kk in range(CONV_WIDTH - 1):
            sh = CONV_WIDTH - 1 - kk
            dx = dx + w[kk:kk + 1] * pltpu.roll(dpre, ext - sh, axis=0)[0:ts]
        dx_ref[...] = dx.astype(BF16)
        dcur = dpre[0:ts]

        @pl.when(i == 0)
        def _():
            dw_ref[...] = jnp.zeros_like(dw_ref)
            db_ref[...] = jnp.zeros_like(db_ref)

        db_ref[...] += jnp.sum(dcur, axis=0, keepdims=True)
        dw_ref[CONV_WIDTH - 1:CONV_WIDTH, :] += jnp.sum(dcur * x, axis=0, keepdims=True)
        for kk in range(CONV_WIDTH - 1):
            xs = _shift_rows(x, halo, CONV_WIDTH - 1 - kk)
            dw_ref[kk:kk + 1, :] += jnp.sum(dcur * xs, axis=0, keepdims=True)

    cur_p = pl.BlockSpec((ts, tc), lambda j, i: (i, cb0 + j))
    prev_p = pl.BlockSpec((SUBLANES, tc), lambda j, i: (jnp.maximum(i * hb - 1, 0), cb0 + j))
    nxt = lambda i: jnp.minimum((i + 1) * hb, nblk * hb - 1)
    next_p = pl.BlockSpec((SUBLANES, tc), lambda j, i: (nxt(i), cb0 + j))

    def part(q):
        return lambda j: jnp.clip(j - starts[q], 0, nb[q] - 1)

    cur_d = [pl.BlockSpec((ts, tc), lambda j, i, f=part(q): (i, f(j))) for q in range(3)]
    next_d = [pl.BlockSpec((SUBLANES, tc), lambda j, i, f=part(q): (nxt(i), f(j))) for q in range(3)]
    vec4 = pl.BlockSpec((CONV_WIDTH, tc), lambda j, i: (0, j))
    vec1 = pl.BlockSpec((1, tc), lambda j, i: (0, j))
    return pl.pallas_call(
        body, name="conv_bwd", grid=(c // tc, nblk),
        in_specs=[cur_p, prev_p, next_p, *cur_d, *next_d, vec4, vec1, pl.BlockSpec(memory_space=pl.ANY)],
        out_specs=(cur_p, vec4, vec1),
        out_shape=(jax.ShapeDtypeStruct(dproj.shape, dproj.dtype), jax.ShapeDtypeStruct((CONV_WIDTH, c), F32),
                   jax.ShapeDtypeStruct((1, c), F32)),
        input_output_aliases={11: 0},
        compiler_params=_params("parallel", "arbitrary"),
    )(proj, proj, proj, *dacts, *dacts, conv_w, conv_b, dproj)


def _dt_fwd(proj, dt_bias, col0):
    s = proj.shape[0]
    h = dt_bias.shape[1]
    ts = _pick(s, (1024, 512, 256, 128))

    def body(x_ref, b_ref, o_ref):
        v = x_ref[...] + b_ref[...]
        o_ref[...] = jnp.maximum(v, 0.0) + jnp.log1p(jnp.exp(-jnp.abs(v)))

    return pl.pallas_call(
        body, name="dt_fwd", grid=(s // ts,),
        in_specs=[pl.BlockSpec((ts, h), lambda i: (i, col0 // h)), pl.BlockSpec((1, h), lambda i: (0, 0))],
        out_specs=pl.BlockSpec((ts, h), lambda i: (i, 0)), out_shape=jax.ShapeDtypeStruct((s, h), F32),
        compiler_params=_params("parallel"),
    )(proj, dt_bias)


def _dt_bwd(proj, dt_bias, ddt, col0, dproj):
    s = proj.shape[0]
    h = dt_bias.shape[1]
    ts = _pick(s, (1024, 512, 256, 128))

    def body(x_ref, b_ref, d_ref, _, o_ref, db_ref):
        i = pl.program_id(0)
        draw = d_ref[...] * _sigmoid(x_ref[...] + b_ref[...])
        o_ref[...] = draw.astype(BF16)

        @pl.when(i == 0)
        def _():
            db_ref[...] = jnp.zeros_like(db_ref)

        db_ref[...] += jnp.sum(draw, axis=0, keepdims=True)

    return pl.pallas_call(
        body, name="dt_bwd", grid=(s // ts,),
        in_specs=[pl.BlockSpec((ts, h), lambda i: (i, col0 // h)), pl.BlockSpec((1, h), lambda i: (0, 0)),
                  pl.BlockSpec((ts, h), lambda i: (i, 0)), pl.BlockSpec(memory_space=pl.ANY)],
        out_specs=(pl.BlockSpec((ts, h), lambda i: (i, col0 // h)), pl.BlockSpec((1, h), lambda i: (0, 0))),
        out_shape=(jax.ShapeDtypeStruct(dproj.shape, dproj.dtype), jax.ShapeDtypeStruct((1, h), F32)),
        input_output_aliases={3: 0},
        compiler_params=_params("arbitrary"),
    )(proj, dt_bias, ddt, dproj)


def _chunk_terms(dt, dt_t, a, a_t):
    li = lax.broadcasted_iota(jnp.int32, (CHUNK, CHUNK), 0)
    si = lax.broadcasted_iota(jnp.int32, (CHUNK, CHUNK), 1)
    lower = (li >= si).astype(F32)
    upper = (li <= si).astype(F32)
    acum = jnp.dot(lower, dt * a, preferred_element_type=F32, precision=HIGHEST)
    acum_t = jnp.dot(dt_t * a_t, upper, preferred_element_type=F32, precision=HIGHEST)
    return acum, acum_t, li, si, upper


def _dot_exact01(t, m01):
    r = t.shape[0]
    hi = t.astype(BF16)
    rest = t - hi.astype(F32)
    mid = rest.astype(BF16)
    lo = (rest - mid.astype(F32)).astype(BF16)
    out = _dot(jnp.concatenate([hi, mid, lo], axis=0), m01.astype(BF16))
    return out[0:r] + out[r:2 * r] + out[2 * r:3 * r]


def _head_lanes(dt, acum, gw):
    hpg = dt.shape[1]
    p = gw // hpg
    spread = (lax.broadcasted_iota(jnp.int32, (hpg, gw), 1) // p
              == lax.broadcasted_iota(jnp.int32, (hpg, gw), 0)).astype(F32)
    both = _dot_exact01(jnp.concatenate([dt, acum], axis=0), spread)
    dt_e, acum_e = both[0:CHUNK], both[CHUNK:2 * CHUNK]
    alast_e = acum_e[CHUNK - 1:CHUNK, :]
    return dt_e, jnp.exp(acum_e), jnp.exp(alast_e - acum_e), jnp.exp(alast_e)


def _fold_heads(t, hpg):
    gw = t.shape[1]
    p = gw // hpg
    fold = (lax.broadcasted_iota(jnp.int32, (gw, hpg), 0) // p
            == lax.broadcasted_iota(jnp.int32, (gw, hpg), 1)).astype(F32)
    return _dot_exact01(t, fold)


def _ssd_fwd(xbc, proj, dt_g, dt_gt, a_g, a_gt, dskip_e, norm_w, d_inner, n_state):
    s = xbc.shape[0]
    hpg = dt_g.shape[2]
    gw = d_inner // SSM_GROUPS
    p = gw // hpg
    nc = s // CHUNK
    n = n_state
    b0 = d_inner // n
    c0 = b0 + SSM_GROUPS
    per_tile = LANES // p

    def body(xs_ref, b_ref, c_ref, dt_ref, dtt_ref, a_ref, at_ref, z_ref, dsk_ref, nw_ref,
             yn_ref, y_ref, st_ref, state):
        c = pl.program_id(1)

        @pl.when(c == 0)
        def _():
            state[...] = jnp.zeros_like(state)

        st = state[...]
        st_ref[...] = st
        xs = xs_ref[...]
        bm = b_ref[...].astype(BF16)
        cm = c_ref[...].astype(BF16)
        dt = dt_ref[...]
        acum, acum_t, li, si, _ = _chunk_terms(dt, dtt_ref[...], a_ref[...], at_ref[...])
        dt_e, e_a, t_e, e_last = _head_lanes(dt, acum, gw)
        xdt = xs * dt_e
        xdtb = xdt.astype(BF16)
        cb = _dot_nt(cm, bm)
        causal = li >= si
        lane = lax.broadcasted_iota(jnp.int32, (1, LANES), 1)
        y_ref[...] = _dot(cm, st.astype(BF16)) * e_a
        for q in range(gw // LANES):
            ql = slice(q * LANES, (q + 1) * LANES)
            xq = xdtb[:, ql]
            ms = []
            for i in range(per_tile):
                h = q * per_tile + i
                decay = jnp.exp(jnp.where(causal, acum[:, h:h + 1] - acum_t[h:h + 1, :], NEG_INF))
                ms.append((cb * decay).astype(BF16))
            y_all = _dot(jnp.concatenate(ms, axis=0), xq)
            yd = y_all[0:CHUNK]
            for i in range(1, per_tile):
                yd = jnp.where(lane >= i * p, y_all[i * CHUNK:(i + 1) * CHUNK], yd)
            y_ref[:, ql] += yd
        state[...] = st * e_last + _dot_tn(bm, (xdt * t_e).astype(BF16))
        yt = y_ref[...] + xs * dsk_ref[...]
        z = z_ref[...]
        yz = yt * (z * _sigmoid(z))
        r = lax.rsqrt(jnp.mean(yz * yz, axis=-1, keepdims=True) + RMS_EPS)
        yn_ref[...] = (yz * r * nw_ref[...]).astype(BF16)

    wide = pl.BlockSpec((CHUNK, gw), lambda g, c: (c, g))
    return pl.pallas_call(
        body, name="ssd_fwd", grid=(SSM_GROUPS, nc),
        in_specs=[wide,
                  pl.BlockSpec((CHUNK, n), lambda g, c: (c, b0 + g)),
                  pl.BlockSpec((CHUNK, n), lambda g, c: (c, c0 + g)),
                  pl.BlockSpec((None, CHUNK, hpg), lambda g, c: (g, c, 0)),
                  pl.BlockSpec((None, hpg, CHUNK), lambda g, c: (g, 0, c)),
                  pl.BlockSpec((None, 1, hpg), lambda g, c: (g, 0, 0)),
                  pl.BlockSpec((None, hpg, 1), lambda g, c: (g, 0, 0)),
                  wide,
                  pl.BlockSpec((None, 1, gw), lambda g, c: (g, 0, 0)),
                  pl.BlockSpec((1, gw), lambda g, c: (0, g))],
        out_specs=(wide, wide, pl.BlockSpec((None, None, n, gw), lambda g, c: (g, c, 0, 0))),
        out_shape=(jax.ShapeDtypeStruct((s, d_inner), BF16), jax.ShapeDtypeStruct((s, d_inner), F32),
                   jax.ShapeDtypeStruct((SSM_GROUPS, nc, n, gw), F32)),
        scratch_shapes=[pltpu.VMEM((n, gw), F32)],
        compiler_params=_params("parallel", "arbitrary"),
    )(xbc, xbc, xbc, dt_g, dt_gt, a_g, a_gt, proj, dskip_e, norm_w)


def _ssd_epilogue_bwd(dyn, y, xbc, proj, dskip_e, norm_w, hpg):
    s, d_inner = dyn.shape
    gw = d_inner // SSM_GROUPS
    p = gw // hpg
    nc = s // CHUNK

    def body(dyn_ref, y_ref, xs_ref, z_ref, dsk_ref, nw_ref, dy_ref, dz_ref, dnw_ref, ddsk_ref):
        c = pl.program_id(1)
        xs = xs_ref[...]
        z = z_ref[...]
        yt = y_ref[...] + xs * dsk_ref[...]
        sg = _sigmoid(z)
        sz = z * sg
        yz = yt * sz
        r = lax.rsqrt(jnp.mean(yz * yz, axis=-1, keepdims=True) + RMS_EPS)
        dynv = dyn_ref[...]
        dyh = dynv * nw_ref[...]
        dyz = r * (dyh - yz * (r * r) * jnp.mean(dyh * yz, axis=-1, keepdims=True))
        dyt = dyz * sz
        dy_ref[...] = dyt
        dz_ref[...] = (dyz * yt * (sg * (1.0 + z * (1.0 - sg)))).astype(BF16)

        @pl.when(c == 0)
        def _():
            dnw_ref[...] = jnp.zeros_like(dnw_ref)
            ddsk_ref[...] = jnp.zeros_like(ddsk_ref)

        dnw_ref[...] += jnp.sum(dynv * yz * r, axis=0, keepdims=True)
        colsum = jnp.sum(dyt * xs, axis=0, keepdims=True)
        fold = (lax.broadcasted_iota(jnp.int32, (gw, hpg), 0) // p
                == lax.broadcasted_iota(jnp.int32, (gw, hpg), 1)).astype(F32)
        ddsk_ref[...] += jnp.dot(colsum, fold, preferred_element_type=F32, precision=HIGHEST)

    wide = pl.BlockSpec((CHUNK, gw), lambda g, c: (c, g))
    return pl.pallas_call(
        body, name="ssd_epilogue_bwd", grid=(SSM_GROUPS, nc),
        in_specs=[wide, wide, wide, wide, pl.BlockSpec((None, 1, gw), lambda g, c: (g, 0, 0)),
                  pl.BlockSpec((1, gw), lambda g, c: (0, g))],
        out_specs=(wide, wide, pl.BlockSpec((1, gw), lambda g, c: (0, g)),
                   pl.BlockSpec((None, 1, hpg), lambda g, c: (g, 0, 0))),
        out_shape=(jax.ShapeDtypeStruct((s, d_inner), F32), jax.ShapeDtypeStruct((s, proj.shape[1]), BF16),
                   jax.ShapeDtypeStruct((1, d_inner), F32), jax.ShapeDtypeStruct((SSM_GROUPS, 1, hpg), F32)),
        compiler_params=_params("parallel", "arbitrary"),
    )(dyn, y, xbc, proj, dskip_e, norm_w)


def _ssd_scan_bwd(xbc, dt_g, dt_gt, a_g, a_gt, states, dy, dskip_e, d_inner, n_state, ride=None):
    s = xbc.shape[0]
    hpg = dt_g.shape[2]
    gw = d_inner // SSM_GROUPS
    p = gw // hpg
    nc = s // CHUNK
    n = n_state
    b0 = d_inner // n
    c0 = b0 + SSM_GROUPS
    per_tile = LANES // p

    def body(xs_ref, b_ref, c_ref, dt_ref, dtt_ref, a_ref, at_ref, st_ref, dy_ref, dsk_ref,
             dxs_ref, db_ref, dc_ref, ddt_ref, da_ref, dstate, ydiag_ref, dxd_ref):
        c = pl.program_id(1)

        @pl.when(c == 0)
        def _():
            dstate[...] = jnp.zeros_like(dstate)
            da_ref[...] = jnp.zeros_like(da_ref)

        xs = xs_ref[...]
        bm = b_ref[...].astype(BF16)
        cm = c_ref[...].astype(BF16)
        dt = dt_ref[...]
        a = a_ref[...]
        dyv = dy_ref[...]
        dsk = dsk_ref[...]
        acum, acum_t, li, si, upper = _chunk_terms(dt, dtt_ref[...], a, at_ref[...])
        dt_e, e_a, t_e, e_last = _head_lanes(dt, acum, gw)
        cb = _dot_nt(cm, bm)
        lower_mask = li >= si
        lane = lax.broadcasted_iota(jnp.int32, (1, LANES), 1)
        row_l = lax.broadcasted_iota(jnp.int32, (CHUNK, 1), 0)
        st = st_ref[...]
        stb = st.astype(BF16)
        dst = dstate[...]
        dstb = dst.astype(BF16)
        xdt = xs * dt_e
        xdtb = xdt.astype(BF16)
        dyb = dyv.astype(BF16)
        dye = dyv * e_a
        dyeb = dye.astype(BF16)
        xte = xdt * t_e
        xteb = xte.astype(BF16)
        wv = _dot(bm, dstb)
        yo = _dot(cm, stb)
        dcb = jnp.zeros((CHUNK, CHUNK), F32)
        for q in range(gw // LANES):
            ql = slice(q * LANES, (q + 1) * LANES)
            xq = xdtb[:, ql]
            dq = dyb[:, ql]
            decays, ms, mts, dqs = [], [], [], []
            for i in range(per_tile):
                h = q * per_tile + i
                decay = jnp.exp(jnp.where(lower_mask, acum[:, h:h + 1] - acum_t[h:h + 1, :], NEG_INF))
                mm = cb * decay
                mine = (lane >= i * p) & (lane < (i + 1) * p)
                decays.append(decay)
                ms.append(mm.astype(BF16))
                mts.append(mm.T.astype(BF16))
                dqs.append(jnp.where(mine, dq, jnp.zeros_like(dq)))
            dm_all = _dot_nt(jnp.concatenate(dqs, axis=0), xq)
            y_all = _dot(jnp.concatenate(ms, axis=0), xq)
            d_all = _dot(jnp.concatenate(mts, axis=0), dq)
            yd = dd = None
            for i in range(per_tile):
                rows = slice(i * CHUNK, (i + 1) * CHUNK)
                dcb = dcb + dm_all[rows] * decays[i]
                yd = y_all[rows] if i == 0 else jnp.where(lane >= i * p, y_all[rows], yd)
                dd = d_all[rows] if i == 0 else jnp.where(lane >= i * p, d_all[rows], dd)
            ydiag_ref[:, ql] = yd
            dxd_ref[:, ql] = dd
        ydiag = ydiag_ref[...]
        dxd = dxd_ref[...]
        dxdt = dxd + t_e * wv
        xw = xte * wv
        last_in = jnp.sum(xw, axis=0, keepdims=True) + e_last * jnp.sum(dst * st, axis=0, keepdims=True)
        folded = _fold_heads(jnp.concatenate(
            [dyb.astype(F32) * ydiag - xdtb.astype(F32) * dxd - xw + dye * yo, dxdt * xs,
             jnp.broadcast_to(last_in, (SUBLANES, gw))],
            axis=0), hpg)
        dalast = folded[2 * CHUNK:2 * CHUNK + 1]
        d_acum = folded[0:CHUNK] + jnp.where(row_l == CHUNK - 1, dalast, 0.0)
        ddt_x = folded[CHUNK:2 * CHUNK]
        dxs_ref[...] = dxdt * dt_e + dyv * dsk
        dbf = dcb.astype(BF16)
        dc_ref[...] = _dot_nt(dyeb, stb) + _dot(dbf, bm)
        db_ref[...] = _dot_nt(xteb, dstb) + _dot_tn(dbf, cm)
        dstate[...] = dst * e_last + _dot_tn(cm, dyeb)
        d_da = jnp.dot(upper, d_acum, preferred_element_type=F32, precision=HIGHEST)
        ddt_ref[...] = d_da * a + ddt_x
        da_ref[...] += jnp.sum(d_da * dt, axis=0, keepdims=True)

    rev = lambda c: nc - 1 - c
    wide = pl.BlockSpec((CHUNK, gw), lambda g, c: (rev(c), g))
    return _call(
        body, name="ssd_scan_bwd", grid=(SSM_GROUPS, nc),
        in_specs=[wide,
                  pl.BlockSpec((CHUNK, n), lambda g, c: (rev(c), b0 + g)),
                  pl.BlockSpec((CHUNK, n), lambda g, c: (rev(c), c0 + g)),
                  pl.BlockSpec((None, CHUNK, hpg), lambda g, c: (g, rev(c), 0)),
                  pl.BlockSpec((None, hpg, CHUNK), lambda g, c: (g, 0, rev(c))),
                  pl.BlockSpec((None, 1, hpg), lambda g, c: (g, 0, 0)),
                  pl.BlockSpec((None, hpg, 1), lambda g, c: (g, 0, 0)),
                  pl.BlockSpec((None, None, n, gw), lambda g, c: (g, rev(c), 0, 0)),
                  wide,
                  pl.BlockSpec((None, 1, gw), lambda g, c: (g, 0, 0))],
        out_specs=[wide,
                   pl.BlockSpec((CHUNK, n), lambda g, c: (rev(c), g)),
                   pl.BlockSpec((CHUNK, n), lambda g, c: (rev(c), g)),
                   pl.BlockSpec((None, CHUNK, hpg), lambda g, c: (g, rev(c), 0)),
                   pl.BlockSpec((None, 1, hpg), lambda g, c: (g, 0, 0))],
        out_shape=[jax.ShapeDtypeStruct((s, d_inner), F32),
                   jax.ShapeDtypeStruct((s, SSM_GROUPS * n), F32), jax.ShapeDtypeStruct((s, SSM_GROUPS * n), F32),
                   jax.ShapeDtypeStruct((SSM_GROUPS, s, hpg), F32), jax.ShapeDtypeStruct((SSM_GROUPS, 1, hpg), F32)],
        scratch=[pltpu.VMEM((n, gw), F32), pltpu.VMEM((CHUNK, gw), F32), pltpu.VMEM((CHUNK, gw), F32)],
        sem=("parallel", "arbitrary"), ride=ride,
        args=(xbc, xbc, xbc, dt_g, dt_gt, a_g, a_gt, states, dy, dskip_e))


def _lin(p):
    return 4 * p[0] + 2 * p[1] + p[2]


class _Gather:
    def __init__(self, arrs):
        self.arrs = list(arrs)

    def out_shape(self):
        return [jax.ShapeDtypeStruct((NDEV,) + a.shape, a.dtype) for a in self.arrs]

    def _copies(self, ins, outs, sems):
        send_sems, recv_sems, local_sems = sems
        x, y, c = lax.axis_index("x"), lax.axis_index("y"), lax.axis_index("c")
        me, sibling = (x, y, c), (x, y, 1 - c)
        chips = [(1 - x, y), (x, 1 - y), (1 - x, 1 - y)]

        def copy(a, k, block, to, src=None):
            rows = outs[a].at[_lin(block)]
            return pltpu.make_async_remote_copy(
                src_ref=rows if src is None else src, dst_ref=rows,
                send_sem=send_sems.at[a * NPEER + k], recv_sem=recv_sems.at[a * NPEER + k],
                device_id=to, device_id_type=pl.DeviceIdType.MESH)

        na = len(ins)
        mine = [pltpu.make_async_copy(ins[a], outs[a].at[_lin(me)], local_sems.at[a]) for a in range(na)]
        first = []
        for a in range(na):
            first.append(copy(a, 0, me, sibling, src=ins[a]))
            first += [copy(a, 1 + j, me, (*chip, c), src=ins[a]) for j, chip in enumerate(chips)]
        return copy, mine, first, me, sibling, chips, c, na

    def start(self, ins, outs, sems):
        _, mine, first, *_ = self._copies(ins, outs, sems)
        for cp in mine + first:
            cp.start()

    def finish(self, ins, outs, sems):
        copy, mine, first, me, sibling, chips, c, na = self._copies(ins, outs, sems)
        passed = []
        for j, chip in enumerate(chips):
            for a in range(na):
                copy(a, 1 + j, (*chip, c), me).wait_recv()
                cp = copy(a, 4 + j, (*chip, c), sibling)
                cp.start()
                passed.append(cp)
        for a in range(na):
            copy(a, 0, sibling, me).wait_recv()
            for j, chip in enumerate(chips):
                copy(a, 4 + j, (*chip, 1 - c), me).wait_recv()
        for cp in first + passed:
            cp.wait_send()
        for cp in mine:
            cp.wait()


class _Scatter:
    def __init__(self, arrs, ks=tuple(range(NDEV))):
        self.arrs = list(arrs)
        self.ks = [tuple(k) for k in ks] if isinstance(ks[0], (tuple, list)) else [tuple(ks)] * len(self.arrs)
        assert len(self.ks) == len(self.arrs)

    def out_shape(self):
        return [jax.ShapeDtypeStruct((len(k),) + a.shape[1:], a.dtype) for a, k in zip(self.arrs, self.ks)]

    def _copies(self, ins, outs, sems):
        send_sems, recv_sems, local_sems = sems
        x, y, c = lax.axis_index("x"), lax.axis_index("y"), lax.axis_index("c")
        me = (x, y, c)

        def peer(k):
            return (1 - x if k & 4 else x, 1 - y if k & 2 else y, 1 - c if k & 1 else c)

        local, remote = [], []
        for a in range(len(ins)):
            for i, k in enumerate(self.ks[a]):
                if k == 0:
                    local.append(pltpu.make_async_copy(ins[a].at[_lin(me)], outs[a].at[i], local_sems.at[a]))
                else:
                    remote.append(pltpu.make_async_remote_copy(
                        src_ref=ins[a].at[_lin(peer(k))], dst_ref=outs[a].at[i],
                        send_sem=send_sems.at[a * NPEER + k - 1], recv_sem=recv_sems.at[a * NPEER + k - 1],
                        device_id=peer(k), device_id_type=pl.DeviceIdType.MESH))
        return local, remote

    def start(self, ins, outs, sems):
        local, remote = self._copies(ins, outs, sems)
        for cp in local + remote:
            cp.start()

    def finish(self, ins, outs, sems):
        local, remote = self._copies(ins, outs, sems)
        for cp in remote:
            cp.wait_recv()
        for cp in remote:
            cp.wait_send()
        for cp in local:
            cp.wait()


def _exchange_scratch(na):
    return [pltpu.SemaphoreType.DMA((na * NPEER,)), pltpu.SemaphoreType.DMA((na * NPEER,)),
            pltpu.SemaphoreType.DMA((na,))]


def _exchange_alone(ex, *, name, in_vmem=False):
    na = len(ex.arrs)

    def body(*refs):
        ins, outs, sems = refs[:na], refs[na:2 * na], refs[2 * na:]
        ex.start(ins, outs, sems)
        ex.finish(ins, outs, sems)

    spec = pl.BlockSpec(memory_space=pltpu.VMEM if in_vmem else pl.ANY)
    return pl.pallas_call(
        body, name=name, out_shape=tuple(ex.out_shape()), in_specs=[spec] * na, out_specs=tuple([spec] * na),
        scratch_shapes=_exchange_scratch(na),
        compiler_params=pltpu.CompilerParams(vmem_limit_bytes=VMEM_LIMIT),
    )(*ex.arrs)


def _call(body, *, name, grid, in_specs, out_specs, out_shape, args, sem, scratch=(), ride=None, aliases=None):
    n_in, n_out, n_scr = len(in_specs), len(out_specs), len(scratch)
    if ride is None:
        outs = pl.pallas_call(
            body, name=name, grid=grid, in_specs=list(in_specs), out_specs=tuple(out_specs),
            out_shape=tuple(out_shape), scratch_shapes=list(scratch), input_output_aliases=aliases or {},
            compiler_params=_params(*sem))(*args)
        return tuple(outs), ()
    nx = len(ride.arrs)
    hbm = pl.BlockSpec(memory_space=pl.ANY)

    def hosted(*refs):
        ins, x_in = refs[:n_in], refs[n_in:n_in + nx]
        o0 = n_in + nx
        outs, x_out = refs[o0:o0 + n_out], refs[o0 + n_out:o0 + n_out + nx]
        s0 = o0 + n_out + nx
        scr, x_sem = refs[s0:s0 + n_scr], refs[s0 + n_scr:]
        ids = [pl.program_id(i) for i in range(len(grid))]
        first = functools.reduce(jnp.logical_and, [i == 0 for i in ids])
        last = functools.reduce(jnp.logical_and, [i == g - 1 for i, g in zip(ids, grid)])

        @pl.when(first)
        def _():
            ride.start(x_in, x_out, x_sem)

        body(*ins, *outs, *scr)

        @pl.when(last)
        def _():
            ride.finish(x_in, x_out, x_sem)

    outs = pl.pallas_call(
        hosted, name=name, grid=grid, in_specs=list(in_specs) + [hbm] * nx,
        out_specs=tuple(list(out_specs) + [hbm] * nx), out_shape=tuple(list(out_shape) + ride.out_shape()),
        scratch_shapes=list(scratch) + _exchange_scratch(nx), input_output_aliases=aliases or {},
        compiler_params=_params(*(("arbitrary",) * len(grid))))(*args, *ride.arrs)
    return tuple(outs[:n_out]), tuple(outs[n_out:])


def _pack(parts):
    flat = jnp.concatenate([p.reshape(-1).astype(F32) for p in parts])
    tile = SUBLANES * LANES
    pad = (-flat.shape[0]) % tile
    return jnp.pad(flat, (0, pad)).reshape(-1, LANES)


def _unpack(buf, shapes):
    flat = buf.reshape(-1)
    out, off = [], 0
    for shp in shapes:
        size = math.prod(shp)
        out.append(flat[off:off + size].reshape(shp))
        off += size
    return out


KS_FLAT = (0, 1, 4, 5, 2, 3)
KS_DIAG = (6, 7)


def _local_step(x, target, wa, wo, ws, wos, rel_bias, conv_w, conv_b, dt_bias, a_log, d_skip, norm_w, ln_g, ln_b,
                dist=False):
    s, d = x.shape
    da = wo.shape[-2]
    heads = da // HEAD_DIM
    qkv_cols = 3 * N_GROUPS_ATTN * da
    d_inner = wos.shape[0] * (NDEV if dist else 1)
    conv_dim = conv_w.shape[1]
    ssm_heads = dt_bias.shape[1]
    hpg = ssm_heads // SSM_GROUPS
    gn = (conv_dim - d_inner) // 2
    n_state = gn // SSM_GROUPS
    gw = d_inner // SSM_GROUPS
    p = gw // hpg
    in_ssm = d_inner + conv_dim + ssm_heads
    xb = _cast_bf16(x, name="cast_x")

    def slabs_of_cols(t):
        return t.reshape(t.shape[0], NDEV, t.shape[1] // NDEV).transpose(1, 0, 2)

    if dist:
        qkv, (ws_slabs,) = _mm(xb, wa, name="mm_qkv", out_dtype=BF16, n_out=qkv_cols, ride=_Gather([ws]))
        ws = ws_slabs.transpose(1, 0, 2).reshape(d, in_ssm)
    else:
        qkv = _mm(xb, wa, name="mm_qkv", out_dtype=BF16, n_out=qkv_cols)
    gate = _mm(xb, wa, name="mm_gate", out_dtype=F32, n_off=qkv_cols, n_out=da)
    bias, bucket = _bias_tables(rel_bias, heads)
    os_, ls_ = [], []
    for g, (_, dil) in enumerate(ATTN_PATTERNS):
        ride = _Gather([wo]) if dist and g == 0 else None
        o, l, rode = _attn_fwd_group(qkv, bias[g], g, dil, da, ride=ride)
        if rode:
            (wo,) = rode
        os_.append(o)
        ls_.append(l)
    o, lse, y = _attn_combine(os_, ls_, gate)
    h1 = _mm(y, wo, name="mm_out_attn", out_dtype=F32)
    x1, x1b = _ln_fwd(x, h1, ln_g[0:1], ln_b[0:1], name="ln1_fwd")

    if dist:
        proj, (wos_slabs,) = _mm(x1b, ws, name="mm_in_ssm", out_dtype=F32, ride=_Gather([wos]))
        wos = wos_slabs.reshape(d_inner, d)
    else:
        proj = _mm(x1b, ws, name="mm_in_ssm", out_dtype=F32)
    xbc = _conv_fwd(proj, conv_w, conv_b, d_inner)
    dt = _dt_fwd(proj, dt_bias, d_inner + conv_dim)
    dt_g = dt.reshape(s, SSM_GROUPS, hpg).transpose(1, 0, 2)
    dt_gt = dt.reshape(s, SSM_GROUPS, hpg).transpose(1, 2, 0)
    a = -jnp.exp(a_log)
    a_g = a.reshape(SSM_GROUPS, 1, hpg)
    a_gt = a.reshape(SSM_GROUPS, hpg, 1)
    dskip_e = jnp.repeat(d_skip.reshape(SSM_GROUPS, 1, hpg), p, axis=2)
    yn, yscan, states = _ssd_fwd(xbc, proj, dt_g, dt_gt, a_g, a_gt, dskip_e, norm_w, d_inner, n_state)
    h2 = _mm(yn, wos, name="mm_out_ssm", out_dtype=F32)

    du2, du2b, dg1, db1, loss_t = _ln_bwd(x1, h2, ln_g[1:2], ln_b[1:2], target, with_loss=True, name="ln2_loss_bwd")
    loss = loss_t[0, 0]
    dyn = _mm(du2b, wos, name="mm_dyn", out_dtype=F32, trans_b=True)
    g_wos = _mm(yn.T, du2b, name="mm_dw_out_ssm", out_dtype=BF16)
    parts = {}
    dyscan, dproj_ssm, g_norm, g_dskip = _ssd_epilogue_bwd(dyn, yscan, xbc, proj, dskip_e, norm_w, hpg)
    ride = _Scatter([g_wos.reshape(NDEV, d_inner // NDEV, d)]) if dist else None
    (dxs, d_bm, d_cm, ddt_g, g_a), rode = _ssd_scan_bwd(xbc, dt_g, dt_gt, a_g, a_gt, states, dyscan, dskip_e,
                                                         d_inner, n_state, ride=ride)
    parts["w_out_ssm"] = [list(rode)]
    g_alog = g_a.reshape(1, ssm_heads) * a
    dproj_ssm, g_conv_w, g_conv_b = _conv_bwd(proj, conv_w, conv_b, (dxs, d_bm, d_cm), d_inner, dproj_ssm)
    ddt = ddt_g.transpose(1, 0, 2).reshape(s, ssm_heads)
    dproj_ssm, g_dtb = _dt_bwd(proj, dt_bias, ddt, d_inner + conv_dim, dproj_ssm)
    g_ws = _mm(x1b.T, dproj_ssm, name="mm_dw_in_ssm", out_dtype=BF16)
    if dist:
        g_ws_slabs = slabs_of_cols(g_ws)
        dx1, near = _mm(dproj_ssm, ws, name="mm_dx1", out_dtype=F32, trans_b=True, res=du2,
                        res_scale=DEEPNORM_ALPHA, ride=_Scatter([g_ws_slabs], KS_FLAT))
    else:
        dx1 = _mm(dproj_ssm, ws, name="mm_dx1", out_dtype=F32, trans_b=True, res=du2, res_scale=DEEPNORM_ALPHA)

    du1, du1b, dg0, db0 = _ln_bwd(x, h1, ln_g[0:1], ln_b[0:1], dx1, with_loss=False, name="ln1_bwd")
    dy = _mm(du1b, wo, name="mm_dy", out_dtype=F32, trans_b=True)
    g_wo = _mm(y.T, du1b, name="mm_dw_out_attn", out_dtype=BF16, slab_out=NDEV)
    do, dgate, dd = _attn_bwd_prep(dy, o, gate)
    dparts, dss, rode_attn = [], [], []
    for g, (_, dil) in enumerate(ATTN_PATTERNS):
        ride = None
        if dist:
            ride = _Scatter([g_wo]) if g == len(KS_DIAG) else _Scatter([g_ws_slabs], KS_DIAG[g:g + 1])
        (dq, dk, dv, ds), rode = _attn_bwd_group(qkv, do, lse, dd, bias[g], g, dil, da, ride=ride)
        dparts += [dq, dk, dv]
        dss.append(ds)
        rode_attn += list(rode)
    g_bias = _bias_bwd(jnp.stack(dss), bucket)
    g_rel_bias = g_bias.transpose(2, 0, 1).reshape(NUM_BUCKETS, N_GROUPS_ATTN * heads)
    dproj_attn = jnp.concatenate(dparts + [dgate], axis=1)
    pending = None
    if dist:
        parts["w_in_ssm"] = [[near[0]] + rode_attn[:len(KS_DIAG)]]
        parts["w_out_attn"] = [rode_attn[len(KS_DIAG):]]
        xbt = xb.T
        half = d // 2
        g_top = _mm(xbt[:half], dproj_attn, name="mm_dw_in_attn_top", out_dtype=BF16, slab_out=NDEV)
        g_bot, (top_a,) = _mm(xbt[half:], dproj_attn, name="mm_dw_in_attn_bottom", out_dtype=BF16, slab_out=NDEV,
                              ride=_Scatter([g_top], KS_FLAT))
        dx, (top_b, bot_a) = _mm(dproj_attn, wa, name="mm_dx", out_dtype=F32, trans_b=True, res=du1,
                                 res_scale=DEEPNORM_ALPHA, ride=_Scatter([g_top, g_bot], [KS_DIAG, KS_FLAT]))
        parts["w_in_attn"] = [[top_a, top_b], [bot_a]]
        pending = _Scatter([g_bot], KS_DIAG)
    else:
        g_wa = _mm(xb.T, dproj_attn, name="mm_dw_in_attn", out_dtype=BF16, slab_out=NDEV)
        dx = _mm(dproj_attn, wa, name="mm_dx", out_dtype=F32, trans_b=True, res=du1, res_scale=DEEPNORM_ALPHA)

    g_ln_g = jnp.concatenate([dg0, dg1], axis=0)
    g_ln_b = jnp.concatenate([db0, db1], axis=0)
    small = dict(rel_bias=g_rel_bias, dt_bias=g_dtb, a_log=g_alog, d_skip=g_dskip.reshape(1, ssm_heads),
                 ln_g=g_ln_g, ln_b=g_ln_b, conv_w=g_conv_w, conv_b=g_conv_b, ssm_norm_w=g_norm)
    if dist:
        return loss, dx, parts, pending, small
    return loss, dx, g_wa, g_wo, g_ws, g_wos, small


REPLICATED = ("rel_bias", "dt_bias", "a_log", "d_skip", "ln_g", "ln_b")
SHARDED_SMALL = ("conv_w", "conv_b", "ssm_norm_w")


def kernel(x, w_in_attn, w_out_attn, rel_bias, w_in_ssm, conv_w, conv_b, dt_bias, a_log, d_skip, ssm_norm_w, w_out_ssm, ln_g, ln_b, loss_target, m_w_in_attn, m_w_out_attn, m_rel_bias, m_w_in_ssm, m_conv_w, m_conv_b, m_dt_bias, m_a_log, m_d_skip, m_ssm_norm_w, m_w_out_ssm, m_ln_g, m_ln_b, v_w_in_attn, v_w_out_attn, v_rel_bias, v_w_in_ssm, v_conv_w, v_conv_b, v_dt_bias, v_a_log, v_d_skip, v_ssm_norm_w, v_w_out_ssm, v_ln_g, v_ln_b):
    w = dict(w_in_attn=w_in_attn, w_out_attn=w_out_attn, rel_bias=rel_bias, w_in_ssm=w_in_ssm, conv_w=conv_w,
             conv_b=conv_b, dt_bias=dt_bias, a_log=a_log, d_skip=d_skip, ssm_norm_w=ssm_norm_w,
             w_out_ssm=w_out_ssm, ln_g=ln_g, ln_b=ln_b)
    m = dict(w_in_attn=m_w_in_attn, w_out_attn=m_w_out_attn, rel_bias=m_rel_bias, w_in_ssm=m_w_in_ssm,
             conv_w=m_conv_w, conv_b=m_conv_b, dt_bias=m_dt_bias, a_log=m_a_log, d_skip=m_d_skip,
             ssm_norm_w=m_ssm_norm_w, w_out_ssm=m_w_out_ssm, ln_g=m_ln_g, ln_b=m_ln_b)
    v = dict(w_in_attn=v_w_in_attn, w_out_attn=v_w_out_attn, rel_bias=v_rel_bias, w_in_ssm=v_w_in_ssm,
             conv_w=v_conv_w, conv_b=v_conv_b, dt_bias=v_dt_bias, a_log=v_a_log, d_skip=v_d_skip,
             ssm_norm_w=v_ssm_norm_w, w_out_ssm=v_w_out_ssm, ln_g=v_ln_g, ln_b=v_ln_b)
    me = _lin((lax.axis_index("x"), lax.axis_index("y"), lax.axis_index("c")))
    d = x.shape[2]
    big = ("w_in_attn", "w_out_attn", "w_in_ssm", "w_out_ssm")

    shards = {k: _cast_bf16(w[k][0], name=f"cast_{k}") for k in big}
    (wa,) = _exchange_alone(_Gather([shards["w_in_attn"]]), name="gather_w_in_attn")
    cpd = conv_w.shape[2]
    npd = ssm_norm_w.shape[1]
    small_shapes = [(CONV_WIDTH, cpd), (1, cpd), (1, npd)]
    (small_all,) = _exchange_alone(_Gather([_pack([conv_w[0], conv_b, ssm_norm_w])]), name="gather_small_weights",
                                   in_vmem=True)
    small_parts = [_unpack(small_all[i], small_shapes) for i in range(NDEV)]
    conv_w_full = jnp.concatenate([p[0] for p in small_parts], axis=1)
    conv_b_full = jnp.concatenate([p[1] for p in small_parts], axis=1)
    norm_w_full = jnp.concatenate([p[2] for p in small_parts], axis=1)

    loss, dx, parts, pending, small = _local_step(
        x[0], loss_target[0], wa, shards["w_out_attn"], shards["w_in_ssm"], shards["w_out_ssm"], rel_bias,
        conv_w_full, conv_b_full, dt_bias[0:1], a_log[0:1], d_skip[0:1], norm_w_full, ln_g, ln_b, dist=True)
    loss = lax.psum(loss, MESH_AXES)
    out = {}
    out["w_in_ssm"], late = _adamw_sum(parts["w_in_ssm"], w["w_in_ssm"][0], m["w_in_ssm"][0], v["w_in_ssm"][0],
                                       name="adamw_w_in_ssm", ride=pending)
    parts["w_in_attn"][1] += list(late)
    for k in ("w_out_ssm", "w_out_attn", "w_in_attn"):
        out[k] = _adamw_sum(parts[k], w[k][0], m[k][0], v[k][0], name=f"adamw_{k}")

    order = REPLICATED + SHARDED_SMALL
    g_shapes = [small[k].shape for k in order]
    (g_all,) = _exchange_alone(_Gather([_pack([small[k] for k in order])]), name="gather_small_grads", in_vmem=True)
    g_sum = dict(zip(order, _unpack(_sum_slots(g_all, name="sum_small_grads"), g_shapes)))
    g_mine = {k: g_sum[k] for k in REPLICATED}
    g_mine["conv_w"] = lax.dynamic_slice_in_dim(g_sum["conv_w"], me * cpd, cpd, axis=1)
    g_mine["conv_b"] = lax.dynamic_slice_in_dim(g_sum["conv_b"], me * cpd, cpd, axis=1)
    g_mine["ssm_norm_w"] = lax.dynamic_slice_in_dim(g_sum["ssm_norm_w"], me * npd, npd, axis=1)
    w_shapes = [w[k].shape for k in order]
    g_pack = _pack([g_mine[k] for k in order])
    d_p, m_p, v_p = _adamw_small(g_pack, _pack([w[k] for k in order]), _pack([m[k] for k in order]),
                                 _pack([v[k] for k in order]), name="adamw_small")
    for k, gk, dk, mk, vk in zip(order, _unpack(g_pack, w_shapes), _unpack(d_p, w_shapes), _unpack(m_p, w_shapes),
                                 _unpack(v_p, w_shapes)):
        out[k] = (gk, dk, mk, vk)
    for k in big:
        out[k] = tuple(t[None] for t in out[k])

    names = ("w_in_attn", "w_out_attn", "rel_bias", "w_in_ssm", "conv_w", "conv_b", "dt_bias", "a_log", "d_skip",
             "ssm_norm_w", "w_out_ssm", "ln_g", "ln_b")
    res = [loss, dx[None]]
    for i in range(4):
        res += [out[k][i] for k in names]
    return tuple(res)
```

```python
import functools
import math

import jax
import jax.numpy as jnp
from jax import lax
from jax.experimental import pallas as pl
from jax.experimental.pallas import tpu as pltpu

F32 = jnp.float32
BF16 = jnp.bfloat16
MESH_AXES = ("x", "y", "c")
NDEV = 8
NPEER = NDEV - 1
LANES = 128
SUBLANES = 8
VMEM_LIMIT = 52 * 1024 * 1024
MM_VMEM_BUDGET = 40 * 1024 * 1024
MM_TK_MAX = 4096
MM_TN_MAX = 1024

ATTN_PATTERNS = ((128, 1), (512, 4), (2048, 16))
N_GROUPS_ATTN = 3
HEAD_DIM = 128
ATTN_BLOCK = 128
ATTN_UNROLL = 2
ATTN_ROWS_TIMES_HEADS = 8192
NUM_BUCKETS = 32
MAX_DISTANCE = 2048
SSM_GROUPS = 8
CONV_WIDTH = 4
CHUNK = 128
DEPTH = 2
DEEPNORM_ALPHA = (2 * DEPTH) ** 0.25
LN_EPS = 1e-5
RMS_EPS = 1e-5
NEG_INF = -1e30
ADAM_LR = 0.001
ADAM_B1 = 0.9
ADAM_B2 = 0.999
ADAM_EPS = 1e-08
ADAM_WD = 0.01
ADAM_STEP = 10
HIGHEST = lax.Precision.HIGHEST


def _params(*sem):
    return pltpu.CompilerParams(dimension_semantics=sem, vmem_limit_bytes=VMEM_LIMIT)


def _pick(n, prefs):
    for p in prefs:
        if n % p == 0:
            return p
    return n


def _dot(a, b):
    return jnp.dot(a, b, preferred_element_type=F32)


def _dot_nt(a, b):
    return lax.dot_general(a, b, (((1,), (1,)), ((), ())), preferred_element_type=F32)


def _dot_tn(a, b):
    return lax.dot_general(a, b, (((0,), (0,)), ((), ())), preferred_element_type=F32)


def _sigmoid(x):
    return 1.0 / (1.0 + jnp.exp(-x))


def _mm(a, b, *, name, out_dtype, trans_b=False, slab_out=0, n_off=0, n_out=None,
        res=None, res_scale=1.0, ride=None):
    m, k = a.shape
    slab_b = b.ndim == 3
    if slab_b:
        ns = b.shape[0]
        if trans_b:
            n, kper = b.shape[1], b.shape[2]
            assert ns * kper == k
        else:
            nper = b.shape[2]
            n = ns * nper
            assert b.shape[1] == k
    else:
        n = b.shape[0] if trans_b else b.shape[1]
        assert (b.shape[1] if trans_b else b.shape[0]) == k
    n_out = n if n_out is None else n_out
    tm = _pick(m, (1024, 512, 256, 128))
    nconstraint = math.gcd(n_out, n_off) if n_off else n_out
    if slab_b and not trans_b:
        nconstraint = math.gcd(nconstraint, nper)
    if slab_out:
        nconstraint = math.gcd(nconstraint, n_out // slab_out)
    kconstraint = kper if (slab_b and trans_b) else k
    tk = max(t for t in range(LANES, min(kconstraint, MM_TK_MAX) + 1, LANES) if kconstraint % t == 0)
    nk = k // tk
    out_bytes = jnp.dtype(out_dtype).itemsize

    def vmem_bytes(t):
        return (2 * 2 * tk * (tm + t) + 2 * tm * t * out_bytes + (4 * tm * t if nk > 1 else 0)
                + (2 * 4 * tm * t if res is not None else 0))

    fits = [t for t in range(LANES, min(nconstraint, MM_TN_MAX) + 1, LANES)
            if nconstraint % t == 0 and vmem_bytes(t) <= MM_VMEM_BUDGET]
    tn = max(fits)
    nb0 = n_off // tn
    grid = (m // tm, n_out // tn, nk)

    a_spec = pl.BlockSpec((tm, tk), lambda i, j, kk: (i, kk))
    if slab_b and not trans_b:
        nps = nper // tn
        b_spec = pl.BlockSpec((None, tk, tn), lambda i, j, kk: ((j + nb0) // nps, kk, (j + nb0) % nps))
    elif slab_b and trans_b:
        kps = kper // tk
        b_spec = pl.BlockSpec((None, tn, tk), lambda i, j, kk: (kk // kps, j + nb0, kk % kps))
    elif trans_b:
        b_spec = pl.BlockSpec((tn, tk), lambda i, j, kk: (j + nb0, kk))
    else:
        b_spec = pl.BlockSpec((tk, tn), lambda i, j, kk: (kk, j + nb0))
    if slab_out:
        ops = (n_out // slab_out) // tn
        o_spec = pl.BlockSpec((None, tm, tn), lambda i, j, kk: (j // ops, i, j % ops))
        o_shape = jax.ShapeDtypeStruct((slab_out, m, n_out // slab_out), out_dtype)
    else:
        o_spec = pl.BlockSpec((tm, tn), lambda i, j, kk: (i, j))
        o_shape = jax.ShapeDtypeStruct((m, n_out), out_dtype)
    in_specs = [a_spec, b_spec]
    args = [a, b]
    if res is not None:
        in_specs.append(pl.BlockSpec((tm, tn), lambda i, j, kk: (i, j)))
        args.append(res)

    def body(*refs):
        a_ref, b_ref = refs[0], refs[1]
        r_ref = refs[2] if res is not None else None
        o_ref = refs[3] if res is not None else refs[2]
        av = a_ref[...].astype(BF16)
        bv = b_ref[...].astype(BF16)
        part = _dot_nt(av, bv) if trans_b else _dot(av, bv)

        def finish(r):
            if res is not None:
                r = r + res_scale * r_ref[...]
            o_ref[...] = r.astype(out_dtype)

        if nk == 1:
            finish(part)
            return
        acc = refs[-1]
        kk = pl.program_id(2)

        @pl.when(kk == 0)
        def _():
            acc[...] = part

        @pl.when(kk > 0)
        def _():
            acc[...] += part

        @pl.when(kk == nk - 1)
        def _():
            finish(acc[...])

    outs, rode = _call(
        body, name=name, grid=grid, in_specs=in_specs, out_specs=[o_spec], out_shape=[o_shape], args=args,
        scratch=[pltpu.VMEM((tm, tn), F32)] if nk > 1 else [], ride=ride,
        sem=("parallel", "parallel", "arbitrary"))
    return (outs[0], rode) if ride is not None else outs[0]


def _cast_bf16(w, *, name):
    r, c = w.shape
    tr = _pick(r, (512, 256, 128, 64, 32, 16, 8))

    def body(w_ref, o_ref):
        o_ref[...] = w_ref[...].astype(BF16)

    return pl.pallas_call(
        body, name=name, grid=(r // tr,),
        in_specs=[pl.BlockSpec((tr, c), lambda i: (i, 0))],
        out_specs=pl.BlockSpec((tr, c), lambda i: (i, 0)),
        out_shape=jax.ShapeDtypeStruct((r, c), BF16),
        compiler_params=_params("parallel"),
    )(w)


def _adam_math(w, g, m, v):
    m2 = ADAM_B1 * m + (1.0 - ADAM_B1) * g
    v2 = ADAM_B2 * v + (1.0 - ADAM_B2) * (g * g)
    m_hat = m2 / (1.0 - ADAM_B1 ** ADAM_STEP)
    v_hat = v2 / (1.0 - ADAM_B2 ** ADAM_STEP)
    delta = -ADAM_LR * (m_hat / (jnp.sqrt(v_hat) + ADAM_EPS) + ADAM_WD * w)
    return delta, m2, v2


def _adamw_sum(bands, w, m, v, *, name, ride=None):
    r, c = w.shape
    nband = len(bands)
    rows = r // nband
    tr = _pick(rows, (128, 64, 32, 16, 8))
    tc = c if (c % LANES or c <= 2560) else _pick(c, (2048, 1024, 512, 256, 128))
    nt = rows // tr
    flat = [p for band in bands for p in band]

    def body(*refs):
        p_refs = refs[:len(flat)]
        w_ref, m_ref, v_ref, g_out, d_out, m_out, v_out = refs[len(flat):]
        i = pl.program_id(0)
        g, at = None, 0
        for q, band in enumerate(bands):
            gq = None
            for p_ref in p_refs[at:at + len(band)]:
                for s in range(p_ref.shape[0]):
                    t = p_ref[s].astype(F32)
                    gq = t if gq is None else gq + t
            at += len(band)
            g = gq if q == 0 else jnp.where(i >= q * nt, gq, g)
        d, m2, v2 = _adam_math(w_ref[...], g, m_ref[...], v_ref[...])
        g_out[...] = g
        d_out[...] = d
        m_out[...] = m2
        v_out[...] = v2

    def band_spec(p, q):
        return pl.BlockSpec((p.shape[0], tr, tc), lambda i, j: (0, jnp.clip(i - q * nt, 0, nt - 1), j))

    spec = pl.BlockSpec((tr, tc), lambda i, j: (i, j))
    shp = jax.ShapeDtypeStruct((r, c), F32)
    outs, rode = _call(
        body, name=name, grid=(r // tr, c // tc),
        in_specs=[band_spec(p, q) for q, band in enumerate(bands) for p in band] + [spec, spec, spec],
        out_specs=[spec, spec, spec, spec], out_shape=[shp, shp, shp, shp], args=(*flat, w, m, v),
        sem=("parallel", "parallel"), ride=ride)
    return (outs, rode) if ride is not None else outs


def _adamw_small(g, w, m, v, *, name):
    shp = jax.ShapeDtypeStruct(w.shape, F32)

    def body(g_ref, w_ref, m_ref, v_ref, d_out, m_out, v_out):
        d, m2, v2 = _adam_math(w_ref[...], g_ref[...], m_ref[...], v_ref[...])
        d_out[...] = d
        m_out[...] = m2
        v_out[...] = v2

    return pl.pallas_call(body, name=name, out_shape=(shp, shp, shp),
                          compiler_params=pltpu.CompilerParams(vmem_limit_bytes=VMEM_LIMIT))(g, w, m, v)


def _sum_slots(parts, *, name):
    _, r, c = parts.shape

    def body(p_ref, o_ref):
        g = p_ref[0]
        for s in range(1, NDEV):
            g = g + p_ref[s]
        o_ref[...] = g

    return pl.pallas_call(body, name=name, out_shape=jax.ShapeDtypeStruct((r, c), F32),
                          compiler_params=pltpu.CompilerParams(vmem_limit_bytes=VMEM_LIMIT))(parts)


def _ln_parts(u):
    mu = jnp.mean(u, axis=-1, keepdims=True)
    xc = u - mu
    var = jnp.mean(xc * xc, axis=-1, keepdims=True)
    rstd = lax.rsqrt(var + LN_EPS)
    return xc * rstd, rstd


def _ln_fwd(xin, h, g, b, *, name):
    s, d = xin.shape
    tm = _pick(s, (128,))

    def body(x_ref, h_ref, g_ref, b_ref, o_ref, ob_ref):
        xhat, _ = _ln_parts(DEEPNORM_ALPHA * x_ref[...] + h_ref[...])
        o = xhat * g_ref[...] + b_ref[...]
        o_ref[...] = o
        ob_ref[...] = o.astype(BF16)

    row = pl.BlockSpec((tm, d), lambda i: (i, 0))
    vec = pl.BlockSpec((1, d), lambda i: (0, 0))
    return pl.pallas_call(
        body, name=name, grid=(s // tm,), in_specs=[row, row, vec, vec], out_specs=(row, row),
        out_shape=(jax.ShapeDtypeStruct((s, d), F32), jax.ShapeDtypeStruct((s, d), BF16)),
        compiler_params=_params("parallel"),
    )(xin, h, g, b)


def _ln_bwd(xin, h, g, b, cot, *, with_loss, name):
    s, d = xin.shape
    tm = _pick(s, (128,))

    def body(x_ref, h_ref, g_ref, b_ref, c_ref, du_ref, dub_ref, dg_ref, db_ref, *rest):
        i = pl.program_id(0)
        xhat, rstd = _ln_parts(DEEPNORM_ALPHA * x_ref[...] + h_ref[...])
        gv = g_ref[...]
        if with_loss:
            diff = xhat * gv + b_ref[...] - c_ref[...]
            part = 0.5 * jnp.sum(jnp.mean(diff * diff, axis=-1, keepdims=True), axis=0, keepdims=True)
            dout = diff / d
        else:
            dout = c_ref[...]

        @pl.when(i == 0)
        def _():
            dg_ref[...] = jnp.zeros_like(dg_ref)
            db_ref[...] = jnp.zeros_like(db_ref)
            if with_loss:
                rest[0][...] = jnp.zeros_like(rest[0])

        dg_ref[...] += jnp.sum(dout * xhat, axis=0, keepdims=True)
        db_ref[...] += jnp.sum(dout, axis=0, keepdims=True)
        if with_loss:
            rest[0][...] += jnp.broadcast_to(part, rest[0].shape)
        dxh = dout * gv
        du = rstd * (dxh - jnp.mean(dxh, axis=-1, keepdims=True)
                     - xhat * jnp.mean(dxh * xhat, axis=-1, keepdims=True))
        du_ref[...] = du
        dub_ref[...] = du.astype(BF16)

    row = pl.BlockSpec((tm, d), lambda i: (i, 0))
    vec = pl.BlockSpec((1, d), lambda i: (0, 0))
    out_specs = [row, row, vec, vec]
    out_shape = [jax.ShapeDtypeStruct((s, d), F32), jax.ShapeDtypeStruct((s, d), BF16),
                 jax.ShapeDtypeStruct((1, d), F32), jax.ShapeDtypeStruct((1, d), F32)]
    if with_loss:
        out_specs.append(pl.BlockSpec((SUBLANES, LANES), lambda i: (0, 0)))
        out_shape.append(jax.ShapeDtypeStruct((SUBLANES, LANES), F32))
    return pl.pallas_call(
        body, name=name, grid=(s // tm,), in_specs=[row, row, vec, vec, row],
        out_specs=tuple(out_specs), out_shape=tuple(out_shape),
        compiler_params=_params("arbitrary"),
    )(xin, h, g, b, cot)


def t5_causal_bucket(dist):
    max_exact = NUM_BUCKETS // 2
    d_f = jnp.maximum(dist, 1).astype(jnp.float32)
    large = max_exact + (jnp.log(d_f / max_exact) / math.log(MAX_DISTANCE / max_exact)
                         * (NUM_BUCKETS - max_exact)).astype(jnp.int32)
    large = jnp.minimum(large, NUM_BUCKETS - 1)
    return jnp.where(dist < max_exact, dist, large)


def _bias_tables(rel_bias, heads):
    qi = lax.broadcasted_iota(jnp.int32, (ATTN_BLOCK, 2 * ATTN_BLOCK), 0)
    ki = lax.broadcasted_iota(jnp.int32, (ATTN_BLOCK, 2 * ATTN_BLOCK), 1)
    delta = ATTN_BLOCK + qi - ki
    buckets = []
    for window, dilation in ATTN_PATTERNS:
        span = window // dilation
        assert span == ATTN_BLOCK
        band = (delta >= 0) & (delta <= span)
        buckets.append(jnp.where(band, t5_causal_bucket(jnp.clip(delta, 0, None) * dilation), -1))
    bucket = jnp.stack(buckets).astype(jnp.int32)

    def body(bk_ref, tbl_ref, o_ref):
        col = pl.program_id(0) * heads + pl.program_id(1)
        bk = bk_ref[...]
        acc = jnp.full(bk.shape, NEG_INF, F32)
        for b in range(NUM_BUCKETS):
            acc = jnp.where(bk == b, tbl_ref[b, col], acc)
        o_ref[...] = acc

    tile = (None, ATTN_BLOCK, 2 * ATTN_BLOCK)
    bias = pl.pallas_call(
        body, name="bias_fwd", grid=(N_GROUPS_ATTN, heads),
        in_specs=[pl.BlockSpec(tile, lambda g, h: (g, 0, 0)), pl.BlockSpec(memory_space=pltpu.SMEM)],
        out_specs=pl.BlockSpec((None,) + tile, lambda g, h: (g, h, 0, 0)),
        out_shape=jax.ShapeDtypeStruct((N_GROUPS_ATTN, heads, ATTN_BLOCK, 2 * ATTN_BLOCK), F32),
        compiler_params=_params("parallel", "parallel"),
    )(bucket, rel_bias)
    return bias, bucket


def _dilated_view(qkv, g, dilation, da):
    heads = da // HEAD_DIM
    if dilation == 1:
        return qkv, qkv.shape[1] // HEAD_DIM, g * 3 * heads
    sub = qkv[:, g * 3 * da:(g + 1) * 3 * da]
    return sub.reshape(qkv.shape[0] // dilation, dilation * 3 * da), 3 * heads, 0


def _heads_per_step(l, heads):
    for hps in (4, 2, 1):
        if heads % hps == 0 and l * hps <= ATTN_ROWS_TIMES_HEADS:
            return hps
    return 1


def _attn_fwd_group(qkv, bias_g, g, dilation, da, ride=None):
    s = qkv.shape[0]
    heads = da // HEAD_DIM
    l = s // dilation
    nb = l // ATTN_BLOCK
    view, cpb, base = _dilated_view(qkv, g, dilation, da)

    hps = _heads_per_step(l, heads)
    lanes = [slice(i * HEAD_DIM, (i + 1) * HEAD_DIM) for i in range(hps)]

    def body(q_ref, k_ref, v_ref, b_ref, o_ref, l_ref):
        scale = HEAD_DIM ** -0.5

        def block(rows, keys, first):
            q, k, v = q_ref[rows, :], k_ref[keys, :], v_ref[keys, :]
            bias = [b_ref[i, :, ATTN_BLOCK:2 * ATTN_BLOCK] if first else b_ref[i] for i in range(hps)]
            sc = [_dot_nt(q[:, hl], k[:, hl]) * scale + bias[i] for i, hl in enumerate(lanes)]
            mx = [jnp.max(t, axis=-1, keepdims=True) for t in sc]
            p = [jnp.exp(t - m) for t, m in zip(sc, mx)]
            den = [jnp.sum(t, axis=-1, keepdims=True) for t in p]
            for i, hl in enumerate(lanes):
                o_ref[rows, hl] = _dot((p[i] * (1.0 / den[i])).astype(BF16), v[:, hl])
                l_ref[rows, hl] = jnp.broadcast_to(mx[i] + jnp.log(den[i]), (ATTN_BLOCK, HEAD_DIM))

        first = pl.ds(0, ATTN_BLOCK)
        block(first, first, True)

        def step(j, carry):
            r0 = pl.multiple_of(j * ATTN_BLOCK, ATTN_BLOCK)
            rk = pl.multiple_of((j - 1) * ATTN_BLOCK, ATTN_BLOCK)
            block(pl.ds(r0, ATTN_BLOCK), pl.ds(rk, 2 * ATTN_BLOCK), False)
            return carry

        lax.fori_loop(1, nb, step, 0)

    def col(t):
        return lambda r, h: (0, (r * cpb + base + t * heads) // hps + h)

    blk = (l, hps * HEAD_DIM)
    out = pl.BlockSpec(blk, lambda r, h: (0, r * (heads // hps) + h))
    shp = jax.ShapeDtypeStruct((l, dilation * da), F32)
    (o, lse), rode = _call(
        body, name=f"attn_fwd_g{g}", grid=(dilation, heads // hps),
        in_specs=[pl.BlockSpec(blk, col(0)), pl.BlockSpec(blk, col(1)), pl.BlockSpec(blk, col(2)),
                  pl.BlockSpec((hps, ATTN_BLOCK, 2 * ATTN_BLOCK), lambda r, h: (h, 0, 0))],
        out_specs=[out, out], out_shape=[shp, shp], args=(view, view, view, bias_g),
        sem=("parallel", "parallel"), ride=ride)
    return o.reshape(s, da), lse.reshape(s, da), rode


def _attn_combine(os_, ls_, gate):
    s, da = gate.shape
    tm = _pick(s, (512, 256, 128))
    tc = _pick(da, (512, 256, 128))

    assert da // HEAD_DIM <= LANES
    per_step = tc // HEAD_DIM

    def body(o0, o1, o2, l0, l1, l2, g_ref, o_ref, l_ref, y_ref):
        j = pl.program_id(1)
        a0, a1, a2 = l0[...], l1[...], l2[...]
        mx = jnp.maximum(jnp.maximum(a0, a1), a2)
        e0, e1, e2 = jnp.exp(a0 - mx), jnp.exp(a1 - mx), jnp.exp(a2 - mx)
        den = e0 + e1 + e2
        o = (e0 * o0[...] + e1 * o1[...] + e2 * o2[...]) / den
        gv = g_ref[...]
        o_ref[...] = o
        y_ref[...] = (o * (gv * _sigmoid(gv))).astype(BF16)
        lse = mx + jnp.log(den)

        @pl.when(j == 0)
        def _():
            l_ref[...] = jnp.zeros_like(l_ref)

        lane = lax.broadcasted_iota(jnp.int32, (1, LANES), 1)
        acc = l_ref[...]
        for i in range(per_step):
            acc = jnp.where(lane == j * per_step + i, lse[:, i * HEAD_DIM:(i + 1) * HEAD_DIM], acc)
        l_ref[...] = acc

    spec = pl.BlockSpec((tm, tc), lambda i, j: (i, j))
    heads_spec = pl.BlockSpec((tm, LANES), lambda i, j: (i, 0))
    return pl.pallas_call(
        body, name="attn_combine", grid=(s // tm, da // tc), in_specs=[spec] * 7, out_specs=(spec, heads_spec, spec),
        out_shape=(jax.ShapeDtypeStruct((s, da), F32), jax.ShapeDtypeStruct((s, LANES), F32),
                   jax.ShapeDtypeStruct((s, da), BF16)),
        compiler_params=_params("parallel", "arbitrary"),
    )(*os_, *ls_, gate)


def _attn_bwd_prep(dy, o, gate):
    s, da = gate.shape
    tm = _pick(s, (512, 256, 128))

    def body(dy_ref, o_ref, g_ref, do_ref, dg_ref, dd_ref):
        j = pl.program_id(1)
        gv = g_ref[...]
        sg = _sigmoid(gv)
        dyv = dy_ref[...]
        ov = o_ref[...]
        do = dyv * (gv * sg)
        do_ref[...] = do.astype(BF16)
        dg_ref[...] = (dyv * ov * (sg * (1.0 + gv * (1.0 - sg)))).astype(BF16)

        @pl.when(j == 0)
        def _():
            dd_ref[...] = jnp.zeros_like(dd_ref)

        lane = lax.broadcasted_iota(jnp.int32, (1, LANES), 1)
        dd_ref[...] = jnp.where(lane == j, jnp.sum(do * ov, axis=-1, keepdims=True), dd_ref[...])

    spec = pl.BlockSpec((tm, HEAD_DIM), lambda i, j: (i, j))
    return pl.pallas_call(
        body, name="attn_bwd_prep", grid=(s // tm, da // HEAD_DIM), in_specs=[spec] * 3,
        out_specs=(spec, spec, pl.BlockSpec((tm, LANES), lambda i, j: (i, 0))),
        out_shape=(jax.ShapeDtypeStruct((s, da), BF16), jax.ShapeDtypeStruct((s, da), BF16),
                   jax.ShapeDtypeStruct((s, LANES), F32)),
        compiler_params=_params("parallel", "arbitrary"),
    )(dy, o, gate)


def _attn_bwd_group(qkv, do, lse, dd, bias_g, g, dilation, da, ride=None):
    s = qkv.shape[0]
    heads = da // HEAD_DIM
    l = s // dilation
    nb = l // ATTN_BLOCK
    view, cpb, base = _dilated_view(qkv, g, dilation, da)
    scale = HEAD_DIM ** -0.5
    hps = _heads_per_step(l, heads)
    lanes = [slice(i * HEAD_DIM, (i + 1) * HEAD_DIM) for i in range(hps)]

    def body(q_ref, k_ref, v_ref, do_ref, l_ref, dd_ref, b_ref, dq_ref, dk_ref, dv_ref, ds_ref, dk_acc, dv_acc):
        h0 = pl.program_id(0) * hps
        r = pl.program_id(1)
        lane = lax.broadcasted_iota(jnp.int32, (1, LANES), 1)

        @pl.when(r == 0)
        def _():
            ds_ref[...] = jnp.zeros_like(ds_ref)

        dk_acc[...] = jnp.zeros_like(dk_acc)
        dv_acc[...] = jnp.zeros_like(dv_acc)

        def block(rows, keys, first):
            q, k, v, dov = q_ref[rows, :], k_ref[keys, :], v_ref[keys, :], do_ref[rows, :]
            lse_all, dd_all = l_ref[rows, :], dd_ref[rows, :]
            pick = [(lane == h0 + i).astype(F32) for i in range(hps)]
            lrow = [jnp.sum(lse_all * m, axis=-1, keepdims=True) for m in pick]
            drow = [jnp.sum(dd_all * m, axis=-1, keepdims=True) for m in pick]
            bias = [b_ref[i, :, ATTN_BLOCK:2 * ATTN_BLOCK] if first else b_ref[i] for i in range(hps)]
            sc = [_dot_nt(q[:, hl], k[:, hl]) for hl in lanes]
            dp = [_dot_nt(dov[:, hl], v[:, hl]) for hl in lanes]
            p = [jnp.exp(sc[i] * scale + bias[i] - lrow[i]) for i in range(hps)]
            ds = [p[i] * (dp[i] - drow[i]) for i in range(hps)]
            dsb = [t.astype(BF16) for t in ds]
            pb = [t.astype(BF16) for t in p]
            for i, hl in enumerate(lanes):
                dq_ref[rows, hl] = (_dot(dsb[i], k[:, hl]) * scale).astype(BF16)
                dk_acc[keys, hl] += _dot_tn(dsb[i], q[:, hl]) * scale
                dv_acc[keys, hl] += _dot_tn(pb[i], dov[:, hl])
                if first:
                    ds_ref[i, :, ATTN_BLOCK:2 * ATTN_BLOCK] += ds[i]
                else:
                    ds_ref[i] += ds[i]

        first = pl.ds(0, ATTN_BLOCK)
        block(first, first, True)

        def step(j, carry):
            r0 = pl.multiple_of(j * ATTN_BLOCK, ATTN_BLOCK)
            rk = pl.multiple_of((j - 1) * ATTN_BLOCK, ATTN_BLOCK)
            block(pl.ds(r0, ATTN_BLOCK), pl.ds(rk, 2 * ATTN_BLOCK), False)
            return carry

        lax.fori_loop(1, nb, step, 0)
        dk_ref[...] = dk_acc[...].astype(BF16)
        dv_ref[...] = dv_acc[...].astype(BF16)

    def col(t):
        return lambda h, r: (0, (r * cpb + base + t * heads) // hps + h)

    blk = (l, hps * HEAD_DIM)
    act = pl.BlockSpec(blk, lambda h, r: (0, r * (heads // hps) + h))
    per_head = pl.BlockSpec((l, LANES), lambda h, r: (0, r))
    tile = pl.BlockSpec((hps, ATTN_BLOCK, 2 * ATTN_BLOCK), lambda h, r: (h, 0, 0))
    shp = jax.ShapeDtypeStruct((l, dilation * da), BF16)
    (dq, dk, dv, ds), rode = _call(
        body, name=f"attn_bwd_g{g}", grid=(heads // hps, dilation),
        in_specs=[pl.BlockSpec(blk, col(0)), pl.BlockSpec(blk, col(1)), pl.BlockSpec(blk, col(2)), act,
                  per_head, per_head, tile],
        out_specs=[act, act, act, tile],
        out_shape=[shp, shp, shp, jax.ShapeDtypeStruct((heads, ATTN_BLOCK, 2 * ATTN_BLOCK), F32)],
        scratch=[pltpu.VMEM(blk, F32), pltpu.VMEM(blk, F32)], sem=("parallel", "arbitrary"), ride=ride,
        args=(view, view, view, do.reshape(l, dilation * da), lse.reshape(l, dilation * LANES),
              dd.reshape(l, dilation * LANES), bias_g))
    return (dq.reshape(s, da), dk.reshape(s, da), dv.reshape(s, da), ds), rode


def _bias_bwd(ds, bucket):
    ng, heads = ds.shape[0], ds.shape[1]

    def body(ds_ref, bk_ref, o_ref):
        bk = bk_ref[...]
        x = ds_ref[...]
        for b in range(NUM_BUCKETS):
            o_ref[:, b:b + 1] = jnp.sum(jnp.where(bk == b, x, 0.0), axis=(0, 1), keepdims=True)

    tile = (None, ATTN_BLOCK, 2 * ATTN_BLOCK)
    out = pl.pallas_call(
        body, name="bias_bwd", grid=(ng, heads),
        in_specs=[pl.BlockSpec((None,) + tile, lambda g, h: (g, h, 0, 0)), pl.BlockSpec(tile, lambda g, h: (g, 0, 0))],
        out_specs=pl.BlockSpec((None, None, 1, NUM_BUCKETS), lambda g, h: (g, h, 0, 0)),
        out_shape=jax.ShapeDtypeStruct((ng, heads, 1, NUM_BUCKETS), F32),
        compiler_params=_params("parallel", "parallel"),
    )(ds, bucket)
    return out.reshape(ng, heads, NUM_BUCKETS)


def _shift_rows(x, halo, s):
    r = pltpu.roll(x, s, axis=0)
    rh = pltpu.roll(halo, s, axis=0)
    row = lax.broadcasted_iota(jnp.int32, halo.shape, 0)
    top = jnp.where(row < s, rh, r[0:SUBLANES])
    return jnp.concatenate([top, r[SUBLANES:]], axis=0)


def _conv_out(x, halo, w, b):
    acc = b + w[CONV_WIDTH - 1:CONV_WIDTH] * x
    for kk in range(CONV_WIDTH - 1):
        acc = acc + w[kk:kk + 1] * _shift_rows(x, halo, CONV_WIDTH - 1 - kk)
    return acc


def _conv_fwd(proj, conv_w, conv_b, col0):
    s = proj.shape[0]
    c = conv_w.shape[1]
    ts = _pick(s, (512, 256, 128))
    tc = _pick(math.gcd(c, col0), (512, 256, 128))
    cb0 = col0 // tc
    hb = ts // SUBLANES

    def body(x_ref, h_ref, w_ref, b_ref, o_ref):
        i = pl.program_id(0)
        halo = jnp.where(i > 0, h_ref[...], 0.0)
        u = _conv_out(x_ref[...], halo, w_ref[...], b_ref[...])
        o_ref[...] = u * _sigmoid(u)

    return pl.pallas_call(
        body, name="conv_fwd", grid=(s // ts, c // tc),
        in_specs=[pl.BlockSpec((ts, tc), lambda i, j: (i, cb0 + j)),
                  pl.BlockSpec((SUBLANES, tc), lambda i, j: (jnp.maximum(i * hb - 1, 0), cb0 + j)),
                  pl.BlockSpec((CONV_WIDTH, tc), lambda i, j: (0, j)),
                  pl.BlockSpec((1, tc), lambda i, j: (0, j))],
        out_specs=pl.BlockSpec((ts, tc), lambda i, j: (i, j)),
        out_shape=jax.ShapeDtypeStruct((s, c), F32),
        compiler_params=_params("parallel", "parallel"),
    )(proj, proj, conv_w, conv_b)


def _conv_bwd(proj, conv_w, conv_b, dacts, col0, dproj):
    s = proj.shape[0]
    c = conv_w.shape[1]
    widths = [d.shape[1] for d in dacts]
    assert sum(widths) == c
    ts = _pick(s, (512, 256, 128))
    tc = _pick(math.gcd(math.gcd(c, col0), math.gcd(*widths)), (512, 256, 128))
    cb0 = col0 // tc
    hb = ts // SUBLANES
    nblk = s // ts
    ext = ts + SUBLANES
    nb = [wd // tc for wd in widths]
    starts = [0, nb[0], nb[0] + nb[1]]

    def body(x_ref, xp_ref, xn_ref, d0, d1, d2, n0, n1, n2, w_ref, b_ref, _, dx_ref, dw_ref, db_ref):
        j = pl.program_id(0)
        i = pl.program_id(1)
        last = i == nblk - 1
        w = w_ref[...]
        halo = jnp.where(i > 0, xp_ref[...], 0.0)
        x = x_ref[...]
        xe = jnp.concatenate([x, xn_ref[...]], axis=0)
        dcur = jnp.where(j < starts[1], d0[...], jnp.where(j < starts[2], d1[...], d2[...]))
        dnext = jnp.where(j < starts[1], n0[...], jnp.where(j < starts[2], n1[...], n2[...]))
        de = jnp.concatenate([dcur, jnp.where(last, 0.0, dnext)], axis=0)
        u = _conv_out(xe, halo, w, b_ref[...])
        sg = _sigmoid(u)
        dpre = de * (sg * (1.0 + u * (1.0 - sg)))
        dx = w[CONV_WIDTH - 1:CONV_WIDTH] * dpre[0:ts]
        for kk in range(CONV_WIDTH - 1):
            sh = CONV_WIDTH - 1 - kk
            dx = dx + w[kk:kk + 1] * pltpu.roll(dpre, ext - sh, axis=0)[0:ts]
        dx_ref[...] = dx.astype(BF16)
        dcur = dpre[0:ts]

        @pl.when(i == 0)
        def _():
            dw_ref[...] = jnp.zeros_like(dw_ref)
            db_ref[...] = jnp.zeros_like(db_ref)

        db_ref[...] += jnp.sum(dcur, axis=0, keepdims=True)
        dw_ref[CONV_WIDTH - 1:CONV_WIDTH, :] += jnp.sum(dcur * x, axis=0, keepdims=True)
        for kk in range(CONV_WIDTH - 1):
            xs = _shift_rows(x, halo, CONV_WIDTH - 1 - kk)
            dw_ref[kk:kk + 1, :] += jnp.sum(dcur * xs, axis=0, keepdims=True)

    cur_p = pl.BlockSpec((ts, tc), lambda j, i: (i, cb0 + j))
    prev_p = pl.BlockSpec((SUBLANES, tc), lambda j, i: (jnp.maximum(i * hb - 1, 0), cb0 + j))
    nxt = lambda i: jnp.minimum((i + 1) * hb, nblk * hb - 1)
    next_p = pl.BlockSpec((SUBLANES, tc), lambda j, i: (nxt(i), cb0 + j))

    def part(q):
        return lambda j: jnp.clip(j - starts[q], 0, nb[q] - 1)

    cur_d = [pl.BlockSpec((ts, tc), lambda j, i, f=part(q): (i, f(j))) for q in range(3)]
    next_d = [pl.BlockSpec((SUBLANES, tc), lambda j, i, f=part(q): (nxt(i), f(j))) for q in range(3)]
    vec4 = pl.BlockSpec((CONV_WIDTH, tc), lambda j, i: (0, j))
    vec1 = pl.BlockSpec((1, tc), lambda j, i: (0, j))
    return pl.pallas_call(
        body, name="conv_bwd", grid=(c // tc, nblk),
        in_specs=[cur_p, prev_p, next_p, *cur_d, *next_d, vec4, vec1, pl.BlockSpec(memory_space=pl.ANY)],
        out_specs=(cur_p, vec4, vec1),
        out_shape=(jax.ShapeDtypeStruct(dproj.shape, dproj.dtype), jax.ShapeDtypeStruct((CONV_WIDTH, c), F32),
                   jax.ShapeDtypeStruct((1, c), F32)),
        input_output_aliases={11: 0},
        compiler_params=_params("parallel", "arbitrary"),
    )(proj, proj, proj, *dacts, *dacts, conv_w, conv_b, dproj)


def _dt_fwd(proj, dt_bias, col0):
    s = proj.shape[0]
    h = dt_bias.shape[1]
    ts = _pick(s, (1024, 512, 256, 128))

    def body(x_ref, b_ref, o_ref):
        v = x_ref[...] + b_ref[...]
        o_ref[...] = jnp.maximum(v, 0.0) + jnp.log1p(jnp.exp(-jnp.abs(v)))

    return pl.pallas_call(
        body, name="dt_fwd", grid=(s // ts,),
        in_specs=[pl.BlockSpec((ts, h), lambda i: (i, col0 // h)), pl.BlockSpec((1, h), lambda i: (0, 0))],
        out_specs=pl.BlockSpec((ts, h), lambda i: (i, 0)), out_shape=jax.ShapeDtypeStruct((s, h), F32),
        compiler_params=_params("parallel"),
    )(proj, dt_bias)


def _dt_bwd(proj, dt_bias, ddt, col0, dproj):
    s = proj.shape[0]
    h = dt_bias.shape[1]
    ts = _pick(s, (1024, 512, 256, 128))

    def body(x_ref, b_ref, d_ref, _, o_ref, db_ref):
        i = pl.program_id(0)
        draw = d_ref[...] * _sigmoid(x_ref[...] + b_ref[...])
        o_ref[...] = draw.astype(BF16)

        @pl.when(i == 0)
        def _():
            db_ref[...] = jnp.zeros_like(db_ref)

        db_ref[...] += jnp.sum(draw, axis=0, keepdims=True)

    return pl.pallas_call(
        body, name="dt_bwd", grid=(s // ts,),
        in_specs=[pl.BlockSpec((ts, h), lambda i: (i, col0 // h)), pl.BlockSpec((1, h), lambda i: (0, 0)),
                  pl.BlockSpec((ts, h), lambda i: (i, 0)), pl.BlockSpec(memory_space=pl.ANY)],
        out_specs=(pl.BlockSpec((ts, h), lambda i: (i, col0 // h)), pl.BlockSpec((1, h), lambda i: (0, 0))),
        out_shape=(jax.ShapeDtypeStruct(dproj.shape, dproj.dtype), jax.ShapeDtypeStruct((1, h), F32)),
        input_output_aliases={3: 0},
        compiler_params=_params("arbitrary"),
    )(proj, dt_bias, ddt, dproj)


def _chunk_terms(dt, dt_t, a, a_t):
    li = lax.broadcasted_iota(jnp.int32, (CHUNK, CHUNK), 0)
    si = lax.broadcasted_iota(jnp.int32, (CHUNK, CHUNK), 1)
    lower = (li >= si).astype(F32)
    upper = (li <= si).astype(F32)
    acum = jnp.dot(lower, dt * a, preferred_element_type=F32, precision=HIGHEST)
    acum_t = jnp.dot(dt_t * a_t, upper, preferred_element_type=F32, precision=HIGHEST)
    return acum, acum_t, li, si, upper


def _dot_exact01(t, m01):
    r = t.shape[0]
    hi = t.astype(BF16)
    rest = t - hi.astype(F32)
    mid = rest.astype(BF16)
    lo = (rest - mid.astype(F32)).astype(BF16)
    out = _dot(jnp.concatenate([hi, mid, lo], axis=0), m01.astype(BF16))
    return out[0:r] + out[r:2 * r] + out[2 * r:3 * r]


def _head_lanes(dt, acum, gw):
    hpg = dt.shape[1]
    p = gw // hpg
    spread = (lax.broadcasted_iota(jnp.int32, (hpg, gw), 1) // p
              == lax.broadcasted_iota(jnp.int32, (hpg, gw), 0)).astype(F32)
    both = _dot_exact01(jnp.concatenate([dt, acum], axis=0), spread)
    dt_e, acum_e = both[0:CHUNK], both[CHUNK:2 * CHUNK]
    alast_e = acum_e[CHUNK - 1:CHUNK, :]
    return dt_e, jnp.exp(acum_e), jnp.exp(alast_e - acum_e), jnp.exp(alast_e)


def _fold_heads(t, hpg):
    gw = t.shape[1]
    p = gw // hpg
    fold = (lax.broadcasted_iota(jnp.int32, (gw, hpg), 0) // p
            == lax.broadcasted_iota(jnp.int32, (gw, hpg), 1)).astype(F32)
    return _dot_exact01(t, fold)


def _ssd_fwd(xbc, proj, dt_g, dt_gt, a_g, a_gt, dskip_e, norm_w, d_inner, n_state):
    s = xbc.shape[0]
    hpg = dt_g.shape[2]
    gw = d_inner // SSM_GROUPS
    p = gw // hpg
    nc = s // CHUNK
    n = n_state
    b0 = d_inner // n
    c0 = b0 + SSM_GROUPS
    per_tile = LANES // p

    def body(xs_ref, b_ref, c_ref, dt_ref, dtt_ref, a_ref, at_ref, z_ref, dsk_ref, nw_ref,
             yn_ref, y_ref, st_ref, state):
        c = pl.program_id(1)

        @pl.when(c == 0)
        def _():
            state[...] = jnp.zeros_like(state)

        st = state[...]
        st_ref[...] = st
        xs = xs_ref[...]
        bm = b_ref[...].astype(BF16)
        cm = c_ref[...].astype(BF16)
        dt = dt_ref[...]
        acum, acum_t, li, si, _ = _chunk_terms(dt, dtt_ref[...], a_ref[...], at_ref[...])
        dt_e, e_a, t_e, e_last = _head_lanes(dt, acum, gw)
        xdt = xs * dt_e
        xdtb = xdt.astype(BF16)
        cb = _dot_nt(cm, bm)
        causal = li >= si
        lane = lax.broadcasted_iota(jnp.int32, (1, LANES), 1)
        y_ref[...] = _dot(cm, st.astype(BF16)) * e_a
        for q in range(gw // LANES):
            ql = slice(q * LANES, (q + 1) * LANES)
            xq = xdtb[:, ql]
            ms = []
            for i in range(per_tile):
                h = q * per_tile + i
                decay = jnp.exp(jnp.where(causal, acum[:, h:h + 1] - acum_t[h:h + 1, :], NEG_INF))
                ms.append((cb * decay).astype(BF16))
            y_all = _dot(jnp.concatenate(ms, axis=0), xq)
            yd = y_all[0:CHUNK]
            for i in range(1, per_tile):
                yd = jnp.where(lane >= i * p, y_all[i * CHUNK:(i + 1) * CHUNK], yd)
            y_ref[:, ql] += yd
        state[...] = st * e_last + _dot_tn(bm, (xdt * t_e).astype(BF16))
        yt = y_ref[...] + xs * dsk_ref[...]
        z = z_ref[...]
        yz = yt * (z * _sigmoid(z))
        r = lax.rsqrt(jnp.mean(yz * yz, axis=-1, keepdims=True) + RMS_EPS)
        yn_ref[...] = (yz * r * nw_ref[...]).astype(BF16)

    wide = pl.BlockSpec((CHUNK, gw), lambda g, c: (c, g))
    return pl.pallas_call(
        body, name="ssd_fwd", grid=(SSM_GROUPS, nc),
        in_specs=[wide,
                  pl.BlockSpec((CHUNK, n), lambda g, c: (c, b0 + g)),
                  pl.BlockSpec((CHUNK, n), lambda g, c: (c, c0 + g)),
                  pl.BlockSpec((None, CHUNK, hpg), lambda g, c: (g, c, 0)),
                  pl.BlockSpec((None, hpg, CHUNK), lambda g, c: (g, 0, c)),
                  pl.BlockSpec((None, 1, hpg), lambda g, c: (g, 0, 0)),
                  pl.BlockSpec((None, hpg, 1), lambda g, c: (g, 0, 0)),
                  wide,
                  pl.BlockSpec((None, 1, gw), lambda g, c: (g, 0, 0)),
                  pl.BlockSpec((1, gw), lambda g, c: (0, g))],
        out_specs=(wide, wide, pl.BlockSpec((None, None, n, gw), lambda g, c: (g, c, 0, 0))),
        out_shape=(jax.ShapeDtypeStruct((s, d_inner), BF16), jax.ShapeDtypeStruct((s, d_inner), F32),
                   jax.ShapeDtypeStruct((SSM_GROUPS, nc, n, gw), F32)),
        scratch_shapes=[pltpu.VMEM((n, gw), F32)],
        compiler_params=_params("parallel", "arbitrary"),
    )(xbc, xbc, xbc, dt_g, dt_gt, a_g, a_gt, proj, dskip_e, norm_w)


def _ssd_epilogue_bwd(dyn, y, xbc, proj, dskip_e, norm_w, hpg):
    s, d_inner = dyn.shape
    gw = d_inner // SSM_GROUPS
    p = gw // hpg
    nc = s // CHUNK

    def body(dyn_ref, y_ref, xs_ref, z_ref, dsk_ref, nw_ref, dy_ref, dz_ref, dnw_ref, ddsk_ref):
        c = pl.program_id(1)
        xs = xs_ref[...]
        z = z_ref[...]
        yt = y_ref[...] + xs * dsk_ref[...]
        sg = _sigmoid(z)
        sz = z * sg
        yz = yt * sz
        r = lax.rsqrt(jnp.mean(yz * yz, axis=-1, keepdims=True) + RMS_EPS)
        dynv = dyn_ref[...]
        dyh = dynv * nw_ref[...]
        dyz = r * (dyh - yz * (r * r) * jnp.mean(dyh * yz, axis=-1, keepdims=True))
        dyt = dyz * sz
        dy_ref[...] = dyt
        dz_ref[...] = (dyz * yt * (sg * (1.0 + z * (1.0 - sg)))).astype(BF16)

        @pl.when(c == 0)
        def _():
            dnw_ref[...] = jnp.zeros_like(dnw_ref)
            ddsk_ref[...] = jnp.zeros_like(ddsk_ref)

        dnw_ref[...] += jnp.sum(dynv * yz * r, axis=0, keepdims=True)
        colsum = jnp.sum(dyt * xs, axis=0, keepdims=True)
        fold = (lax.broadcasted_iota(jnp.int32, (gw, hpg), 0) // p
                == lax.broadcasted_iota(jnp.int32, (gw, hpg), 1)).astype(F32)
        ddsk_ref[...] += jnp.dot(colsum, fold, preferred_element_type=F32, precision=HIGHEST)

    wide = pl.BlockSpec((CHUNK, gw), lambda g, c: (c, g))
    return pl.pallas_call(
        body, name="ssd_epilogue_bwd", grid=(SSM_GROUPS, nc),
        in_specs=[wide, wide, wide, wide, pl.BlockSpec((None, 1, gw), lambda g, c: (g, 0, 0)),
                  pl.BlockSpec((1, gw), lambda g, c: (0, g))],
        out_specs=(wide, wide, pl.BlockSpec((1, gw), lambda g, c: (0, g)),
                   pl.BlockSpec((None, 1, hpg), lambda g, c: (g, 0, 0))),
        out_shape=(jax.ShapeDtypeStruct((s, d_inner), F32), jax.ShapeDtypeStruct((s, proj.shape[1]), BF16),
                   jax.ShapeDtypeStruct((1, d_inner), F32), jax.ShapeDtypeStruct((SSM_GROUPS, 1, hpg), F32)),
        compiler_params=_params("parallel", "arbitrary"),
    )(dyn, y, xbc, proj, dskip_e, norm_w)


def _ssd_scan_bwd(xbc, dt_g, dt_gt, a_g, a_gt, states, dy, dskip_e, d_inner, n_state, ride=None):
    s = xbc.shape[0]
    hpg = dt_g.shape[2]
    gw = d_inner // SSM_GROUPS
    p = gw // hpg
    nc = s // CHUNK
    n = n_state
    b0 = d_inner // n
    c0 = b0 + SSM_GROUPS
    per_tile = LANES // p

    def body(xs_ref, b_ref, c_ref, dt_ref, dtt_ref, a_ref, at_ref, st_ref, dy_ref, dsk_ref,
             dxs_ref, db_ref, dc_ref, ddt_ref, da_ref, dstate, ydiag_ref, dxd_ref):
        c = pl.program_id(1)

        @pl.when(c == 0)
        def _():
            dstate[...] = jnp.zeros_like(dstate)
            da_ref[...] = jnp.zeros_like(da_ref)

        xs = xs_ref[...]
        bm = b_ref[...].astype(BF16)
        cm = c_ref[...].astype(BF16)
        dt = dt_ref[...]
        a = a_ref[...]
        dyv = dy_ref[...]
        dsk = dsk_ref[...]
        acum, acum_t, li, si, upper = _chunk_terms(dt, dtt_ref[...], a, at_ref[...])
        dt_e, e_a, t_e, e_last = _head_lanes(dt, acum, gw)
        cb = _dot_nt(cm, bm)
        lower_mask = li >= si
        lane = lax.broadcasted_iota(jnp.int32, (1, LANES), 1)
        row_l = lax.broadcasted_iota(jnp.int32, (CHUNK, 1), 0)
        st = st_ref[...]
        stb = st.astype(BF16)
        dst = dstate[...]
        dstb = dst.astype(BF16)
        xdt = xs * dt_e
        xdtb = xdt.astype(BF16)
        dyb = dyv.astype(BF16)
        dye = dyv * e_a
        dyeb = dye.astype(BF16)
        xte = xdt * t_e
        xteb = xte.astype(BF16)
        wv = _dot(bm, dstb)
        yo = _dot(cm, stb)
        dcb = jnp.zeros((CHUNK, CHUNK), F32)
        for q in range(gw // LANES):
            ql = slice(q * LANES, (q + 1) * LANES)
            xq = xdtb[:, ql]
            dq = dyb[:, ql]
            decays, ms, mts, dqs = [], [], [], []
            for i in range(per_tile):
                h = q * per_tile + i
                decay = jnp.exp(jnp.where(lower_mask, acum[:, h:h + 1] - acum_t[h:h + 1, :], NEG_INF))
                mm = cb * decay
                mine = (lane >= i * p) & (lane < (i + 1) * p)
                decays.append(decay)
                ms.append(mm.astype(BF16))
                mts.append(mm.T.astype(BF16))
                dqs.append(jnp.where(mine, dq, jnp.zeros_like(dq)))
            dm_all = _dot_nt(jnp.concatenate(dqs, axis=0), xq)
            y_all = _dot(jnp.concatenate(ms, axis=0), xq)
            d_all = _dot(jnp.concatenate(mts, axis=0), dq)
            yd = dd = None
            for i in range(per_tile):
                rows = slice(i * CHUNK, (i + 1) * CHUNK)
                dcb = dcb + dm_all[rows] * decays[i]
                yd = y_all[rows] if i == 0 else jnp.where(lane >= i * p, y_all[rows], yd)
                dd = d_all[rows] if i == 0 else jnp.where(lane >= i * p, d_all[rows], dd)
            ydiag_ref[:, ql] = yd
            dxd_ref[:, ql] = dd
        ydiag = ydiag_ref[...]
        dxd = dxd_ref[...]
        dxdt = dxd + t_e * wv
        xw = xte * wv
        last_in = jnp.sum(xw, axis=0, keepdims=True) + e_last * jnp.sum(dst * st, axis=0, keepdims=True)
        folded = _fold_heads(jnp.concatenate(
            [dyb.astype(F32) * ydiag - xdtb.astype(F32) * dxd - xw + dye * yo, dxdt * xs,
             jnp.broadcast_to(last_in, (SUBLANES, gw))],
            axis=0), hpg)
        dalast = folded[2 * CHUNK:2 * CHUNK + 1]
        d_acum = folded[0:CHUNK] + jnp.where(row_l == CHUNK - 1, dalast, 0.0)
        ddt_x = folded[CHUNK:2 * CHUNK]
        dxs_ref[...] = dxdt * dt_e + dyv * dsk
        dbf = dcb.astype(BF16)
        dc_ref[...] = _dot_nt(dyeb, stb) + _dot(dbf, bm)
        db_ref[...] = _dot_nt(xteb, dstb) + _dot_tn(dbf, cm)
        dstate[...] = dst * e_last + _dot_tn(cm, dyeb)
        d_da = jnp.dot(upper, d_acum, preferred_element_type=F32, precision=HIGHEST)
        ddt_ref[...] = d_da * a + ddt_x
        da_ref[...] += jnp.sum(d_da * dt, axis=0, keepdims=True)

    rev = lambda c: nc - 1 - c
    wide = pl.BlockSpec((CHUNK, gw), lambda g, c: (rev(c), g))
    return _call(
        body, name="ssd_scan_bwd", grid=(SSM_GROUPS, nc),
        in_specs=[wide,
                  pl.BlockSpec((CHUNK, n), lambda g, c: (rev(c), b0 + g)),
                  pl.BlockSpec((CHUNK, n), lambda g, c: (rev(c), c0 + g)),
                  pl.BlockSpec((None, CHUNK, hpg), lambda g, c: (g, rev(c), 0)),
                  pl.BlockSpec((None, hpg, CHUNK), lambda g, c: (g, 0, rev(c))),
                  pl.BlockSpec((None, 1, hpg), lambda g, c: (g, 0, 0)),
                  pl.BlockSpec((None, hpg, 1), lambda g, c: (g, 0, 0)),
                  pl.BlockSpec((None, None, n, gw), lambda g, c: (g, rev(c), 0, 0)),
                  wide,
                  pl.BlockSpec((None, 1, gw), lambda g, c: (g, 0, 0))],
        out_specs=[wide,
                   pl.BlockSpec((CHUNK, n), lambda g, c: (rev(c), g)),
                   pl.BlockSpec((CHUNK, n), lambda g, c: (rev(c), g)),
                   pl.BlockSpec((None, CHUNK, hpg), lambda g, c: (g, rev(c), 0)),
                   pl.BlockSpec((None, 1, hpg), lambda g, c: (g, 0, 0))],
        out_shape=[jax.ShapeDtypeStruct((s, d_inner), F32),
                   jax.ShapeDtypeStruct((s, SSM_GROUPS * n), F32), jax.ShapeDtypeStruct((s, SSM_GROUPS * n), F32),
                   jax.ShapeDtypeStruct((SSM_GROUPS, s, hpg), F32), jax.ShapeDtypeStruct((SSM_GROUPS, 1, hpg), F32)],
        scratch=[pltpu.VMEM((n, gw), F32), pltpu.VMEM((CHUNK, gw), F32), pltpu.VMEM((CHUNK, gw), F32)],
        sem=("parallel", "arbitrary"), ride=ride,
        args=(xbc, xbc, xbc, dt_g, dt_gt, a_g, a_gt, states, dy, dskip_e))


def _lin(p):
    return 4 * p[0] + 2 * p[1] + p[2]


class _Gather:
    def __init__(self, arrs):
        self.arrs = list(arrs)

    def out_shape(self):
        return [jax.ShapeDtypeStruct((NDEV,) + a.shape, a.dtype) for a in self.arrs]

    def _copies(self, ins, outs, sems):
        send_sems, recv_sems, local_sems = sems
        x, y, c = lax.axis_index("x"), lax.axis_index("y"), lax.axis_index("c")
        me, sibling = (x, y, c), (x, y, 1 - c)
        chips = [(1 - x, y), (x, 1 - y), (1 - x, 1 - y)]

        def copy(a, k, block, to, src=None):
            rows = outs[a].at[_lin(block)]
            return pltpu.make_async_remote_copy(
                src_ref=rows if src is None else src, dst_ref=rows,
                send_sem=send_sems.at[a * NPEER + k], recv_sem=recv_sems.at[a * NPEER + k],
                device_id=to, device_id_type=pl.DeviceIdType.MESH)

        na = len(ins)
        mine = [pltpu.make_async_copy(ins[a], outs[a].at[_lin(me)], local_sems.at[a]) for a in range(na)]
        first = []
        for a in range(na):
            first.append(copy(a, 0, me, sibling, src=ins[a]))
            first += [copy(a, 1 + j, me, (*chip, c), src=ins[a]) for j, chip in enumerate(chips)]
        return copy, mine, first, me, sibling, chips, c, na

    def start(self, ins, outs, sems):
        _, mine, first, *_ = self._copies(ins, outs, sems)
        for cp in mine + first:
            cp.start()

    def finish(self, ins, outs, sems):
        copy, mine, first, me, sibling, chips, c, na = self._copies(ins, outs, sems)
        passed = []
        for j, chip in enumerate(chips):
            for a in range(na):
                copy(a, 1 + j, (*chip, c), me).wait_recv()
                cp = copy(a, 4 + j, (*chip, c), sibling)
                cp.start()
                passed.append(cp)
        for a in range(na):
            copy(a, 0, sibling, me).wait_recv()
            for j, chip in enumerate(chips):
                copy(a, 4 + j, (*chip, 1 - c), me).wait_recv()
        for cp in first + passed:
            cp.wait_send()
        for cp in mine:
            cp.wait()


class _Scatter:
    def __init__(self, arrs, ks=tuple(range(NDEV))):
        self.arrs = list(arrs)
        self.ks = [tuple(k) for k in ks] if isinstance(ks[0], (tuple, list)) else [tuple(ks)] * len(self.arrs)
        assert len(self.ks) == len(self.arrs)

    def out_shape(self):
        return [jax.ShapeDtypeStruct((len(k),) + a.shape[1:], a.dtype) for a, k in zip(self.arrs, self.ks)]

    def _copies(self, ins, outs, sems):
        send_sems, recv_sems, local_sems = sems
        x, y, c = lax.axis_index("x"), lax.axis_index("y"), lax.axis_index("c")
        me = (x, y, c)

        def peer(k):
            return (1 - x if k & 4 else x, 1 - y if k & 2 else y, 1 - c if k & 1 else c)

        local, remote = [], []
        for a in range(len(ins)):
            for i, k in enumerate(self.ks[a]):
                if k == 0:
                    local.append(pltpu.make_async_copy(ins[a].at[_lin(me)], outs[a].at[i], local_sems.at[a]))
                else:
                    remote.append(pltpu.make_async_remote_copy(
                        src_ref=ins[a].at[_lin(peer(k))], dst_ref=outs[a].at[i],
                        send_sem=send_sems.at[a * NPEER + k - 1], recv_sem=recv_sems.at[a * NPEER + k - 1],
                        device_id=peer(k), device_id_type=pl.DeviceIdType.MESH))
        return local, remote

    def start(self, ins, outs, sems):
        local, remote = self._copies(ins, outs, sems)
        for cp in local + remote:
            cp.start()

    def finish(self, ins, outs, sems):
        local, remote = self._copies(ins, outs, sems)
        for cp in remote:
            cp.wait_recv()
        for cp in remote:
            cp.wait_send()
        for cp in local:
            cp.wait()


def _exchange_scratch(na):
    return [pltpu.SemaphoreType.DMA((na * NPEER,)), pltpu.SemaphoreType.DMA((na * NPEER,)),
            pltpu.SemaphoreType.DMA((na,))]


def _exchange_alone(ex, *, name, in_vmem=False):
    na = len(ex.arrs)

    def body(*refs):
        ins, outs, sems = refs[:na], refs[na:2 * na], refs[2 * na:]
        ex.start(ins, outs, sems)
        ex.finish(ins, outs, sems)

    spec = pl.BlockSpec(memory_space=pltpu.VMEM if in_vmem else pl.ANY)
    return pl.pallas_call(
        body, name=name, out_shape=tuple(ex.out_shape()), in_specs=[spec] * na, out_specs=tuple([spec] * na),
        scratch_shapes=_exchange_scratch(na),
        compiler_params=pltpu.CompilerParams(vmem_limit_bytes=VMEM_LIMIT),
    )(*ex.arrs)


def _call(body, *, name, grid, in_specs, out_specs, out_shape, args, sem, scratch=(), ride=None, aliases=None):
    n_in, n_out, n_scr = len(in_specs), len(out_specs), len(scratch)
    if ride is None:
        outs = pl.pallas_call(
            body, name=name, grid=grid, in_specs=list(in_specs), out_specs=tuple(out_specs),
            out_shape=tuple(out_shape), scratch_shapes=list(scratch), input_output_aliases=aliases or {},
            compiler_params=_params(*sem))(*args)
        return tuple(outs), ()
    nx = len(ride.arrs)
    hbm = pl.BlockSpec(memory_space=pl.ANY)

    def hosted(*refs):
        ins, x_in = refs[:n_in], refs[n_in:n_in + nx]
        o0 = n_in + nx
        outs, x_out = refs[o0:o0 + n_out], refs[o0 + n_out:o0 + n_out + nx]
        s0 = o0 + n_out + nx
        scr, x_sem = refs[s0:s0 + n_scr], refs[s0 + n_scr:]
        ids = [pl.program_id(i) for i in range(len(grid))]
        first = functools.reduce(jnp.logical_and, [i == 0 for i in ids])
        last = functools.reduce(jnp.logical_and, [i == g - 1 for i, g in zip(ids, grid)])

        @pl.when(first)
        def _():
            ride.start(x_in, x_out, x_sem)

        body(*ins, *outs, *scr)

        @pl.when(last)
        def _():
            ride.finish(x_in, x_out, x_sem)

    outs = pl.pallas_call(
        hosted, name=name, grid=grid, in_specs=list(in_specs) + [hbm] * nx,
        out_specs=tuple(list(out_specs) + [hbm] * nx), out_shape=tuple(list(out_shape) + ride.out_shape()),
        scratch_shapes=list(scratch) + _exchange_scratch(nx), input_output_aliases=aliases or {},
        compiler_params=_params(*(("arbitrary",) * len(grid))))(*args, *ride.arrs)
    return tuple(outs[:n_out]), tuple(outs[n_out:])


def _pack(parts):
    flat = jnp.concatenate([p.reshape(-1).astype(F32) for p in parts])
    tile = SUBLANES * LANES
    pad = (-flat.shape[0]) % tile
    return jnp.pad(flat, (0, pad)).reshape(-1, LANES)


def _unpack(buf, shapes):
    flat = buf.reshape(-1)
    out, off = [], 0
    for shp in shapes:
        size = math.prod(shp)
        out.append(flat[off:off + size].reshape(shp))
        off += size
    return out


KS_FLAT = (0, 1, 4, 5, 2, 3)
KS_DIAG = (6, 7)


def _local_step(x, target, wa, wo, ws, wos, rel_bias, conv_w, conv_b, dt_bias, a_log, d_skip, norm_w, ln_g, ln_b,
                dist=False):
    s, d = x.shape
    da = wo.shape[-2]
    heads = da // HEAD_DIM
    qkv_cols = 3 * N_GROUPS_ATTN * da
    d_inner = wos.shape[0] * (NDEV if dist else 1)
    conv_dim = conv_w.shape[1]
    ssm_heads = dt_bias.shape[1]
    hpg = ssm_heads // SSM_GROUPS
    gn = (conv_dim - d_inner) // 2
    n_state = gn // SSM_GROUPS
    gw = d_inner // SSM_GROUPS
    p = gw // hpg
    in_ssm = d_inner + conv_dim + ssm_heads
    xb = _cast_bf16(x, name="cast_x")

    def slabs_of_cols(t):
        return t.reshape(t.shape[0], NDEV, t.shape[1] // NDEV).transpose(1, 0, 2)

    if dist:
        qkv, (ws_slabs,) = _mm(xb, wa, name="mm_qkv", out_dtype=BF16, n_out=qkv_cols, ride=_Gather([ws]))
        ws = ws_slabs.transpose(1, 0, 2).reshape(d, in_ssm)
    else:
        qkv = _mm(xb, wa, name="mm_qkv", out_dtype=BF16, n_out=qkv_cols)
    gate = _mm(xb, wa, name="mm_gate", out_dtype=F32, n_off=qkv_cols, n_out=da)
    bias, bucket = _bias_tables(rel_bias, heads)
    os_, ls_ = [], []
    for g, (_, dil) in enumerate(ATTN_PATTERNS):
        ride = _Gather([wo]) if dist and g == 0 else None
        o, l, rode = _attn_fwd_group(qkv, bias[g], g, dil, da, ride=ride)
        if rode:
            (wo,) = rode
        os_.append(o)
        ls_.append(l)
    o, lse, y = _attn_combine(os_, ls_, gate)
    h1 = _mm(y, wo, name="mm_out_attn", out_dtype=F32)
    x1, x1b = _ln_fwd(x, h1, ln_g[0:1], ln_b[0:1], name="ln1_fwd")

    if dist:
        proj, (wos_slabs,) = _mm(x1b, ws, name="mm_in_ssm", out_dtype=F32, ride=_Gather([wos]))
        wos = wos_slabs.reshape(d_inner, d)
    else:
        proj = _mm(x1b, ws, name="mm_in_ssm", out_dtype=F32)
    xbc = _conv_fwd(proj, conv_w, conv_b, d_inner)
    dt = _dt_fwd(proj, dt_bias, d_inner + conv_dim)
    dt_g = dt.reshape(s, SSM_GROUPS, hpg).transpose(1, 0, 2)
    dt_gt = dt.reshape(s, SSM_GROUPS, hpg).transpose(1, 2, 0)
    a = -jnp.exp(a_log)
    a_g = a.reshape(SSM_GROUPS, 1, hpg)
    a_gt = a.reshape(SSM_GROUPS, hpg, 1)
    dskip_e = jnp.repeat(d_skip.reshape(SSM_GROUPS, 1, hpg), p, axis=2)
    yn, yscan, states = _ssd_fwd(xbc, proj, dt_g, dt_gt, a_g, a_gt, dskip_e, norm_w, d_inner, n_state)
    h2 = _mm(yn, wos, name="mm_out_ssm", out_dtype=F32)

    du2, du2b, dg1, db1, loss_t = _ln_bwd(x1, h2, ln_g[1:2], ln_b[1:2], target, with_loss=True, name="ln2_loss_bwd")
    loss = loss_t[0, 0]
    dyn = _mm(du2b, wos, name="mm_dyn", out_dtype=F32, trans_b=True)
    g_wos = _mm(yn.T, du2b, name="mm_dw_out_ssm", out_dtype=BF16)
    parts = {}
    dyscan, dproj_ssm, g_norm, g_dskip = _ssd_epilogue_bwd(dyn, yscan, xbc, proj, dskip_e, norm_w, hpg)
    ride = _Scatter([g_wos.reshape(NDEV, d_inner // NDEV, d)]) if dist else None
    (dxs, d_bm, d_cm, ddt_g, g_a), rode = _ssd_scan_bwd(xbc, dt_g, dt_gt, a_g, a_gt, states, dyscan, dskip_e,
                                                         d_inner, n_state, ride=ride)
    parts["w_out_ssm"] = [list(rode)]
    g_alog = g_a.reshape(1, ssm_heads) * a
    dproj_ssm, g_conv_w, g_conv_b = _conv_bwd(proj, conv_w, conv_b, (dxs, d_bm, d_cm), d_inner, dproj_ssm)
    ddt = ddt_g.transpose(1, 0, 2).reshape(s, ssm_heads)
    dproj_ssm, g_dtb = _dt_bwd(proj, dt_bias, ddt, d_inner + conv_dim, dproj_ssm)
    g_ws = _mm(x1b.T, dproj_ssm, name="mm_dw_in_ssm", out_dtype=BF16)
    if dist:
        g_ws_slabs = slabs_of_cols(g_ws)
        dx1, near = _mm(dproj_ssm, ws, name="mm_dx1", out_dtype=F32, trans_b=True, res=du2,
                        res_scale=DEEPNORM_ALPHA, ride=_Scatter([g_ws_slabs], KS_FLAT))
    else:
        dx1 = _mm(dproj_ssm, ws, name="mm_dx1", out_dtype=F32, trans_b=True, res=du2, res_scale=DEEPNORM_ALPHA)

    du1, du1b, dg0, db0 = _ln_bwd(x, h1, ln_g[0:1], ln_b[0:1], dx1, with_loss=False, name="ln1_bwd")
    dy = _mm(du1b, wo, name="mm_dy", out_dtype=F32, trans_b=True)
    g_wo = _mm(y.T, du1b, name="mm_dw_out_attn", out_dtype=BF16, slab_out=NDEV)
    do, dgate, dd = _attn_bwd_prep(dy, o, gate)
    dparts, dss, rode_attn = [], [], []
    for g, (_, dil) in enumerate(ATTN_PATTERNS):
        ride = None
        if dist:
            ride = _Scatter([g_wo]) if g == len(KS_DIAG) else _Scatter([g_ws_slabs], KS_DIAG[g:g + 1])
        (dq, dk, dv, ds), rode = _attn_bwd_group(qkv, do, lse, dd, bias[g], g, dil, da, ride=ride)
        dparts += [dq, dk, dv]
        dss.append(ds)
        rode_attn += list(rode)
    g_bias = _bias_bwd(jnp.stack(dss), bucket)
    g_rel_bias = g_bias.transpose(2, 0, 1).reshape(NUM_BUCKETS, N_GROUPS_ATTN * heads)
    dproj_attn = jnp.concatenate(dparts + [dgate], axis=1)
    pending = None
    if dist:
        parts["w_in_ssm"] = [[near[0]] + rode_attn[:len(KS_DIAG)]]
        parts["w_out_attn"] = [rode_attn[len(KS_DIAG):]]
        xbt = xb.T
        half = d // 2
        g_top = _mm(xbt[:half], dproj_attn, name="mm_dw_in_attn_top", out_dtype=BF16, slab_out=NDEV)
        g_bot, (top_a,) = _mm(xbt[half:], dproj_attn, name="mm_dw_in_attn_bottom", out_dtype=BF16, slab_out=NDEV,
                              ride=_Scatter([g_top], KS_FLAT))
        dx, (top_b, bot_a) = _mm(dproj_attn, wa, name="mm_dx", out_dtype=F32, trans_b=True, res=du1,
                                 res_scale=DEEPNORM_ALPHA, ride=_Scatter([g_top, g_bot], [KS_DIAG, KS_FLAT]))
        parts["w_in_attn"] = [[top_a, top_b], [bot_a]]
        pending = _Scatter([g_bot], KS_DIAG)
    else:
        g_wa = _mm(xb.T, dproj_attn, name="mm_dw_in_attn", out_dtype=BF16, slab_out=NDEV)
        dx = _mm(dproj_attn, wa, name="mm_dx", out_dtype=F32, trans_b=True, res=du1, res_scale=DEEPNORM_ALPHA)

    g_ln_g = jnp.concatenate([dg0, dg1], axis=0)
    g_ln_b = jnp.concatenate([db0, db1], axis=0)
    small = dict(rel_bias=g_rel_bias, dt_bias=g_dtb, a_log=g_alog, d_skip=g_dskip.reshape(1, ssm_heads),
                 ln_g=g_ln_g, ln_b=g_ln_b, conv_w=g_conv_w, conv_b=g_conv_b, ssm_norm_w=g_norm)
    if dist:
        return loss, dx, parts, pending, small
    return loss, dx, g_wa, g_wo, g_ws, g_wos, small


REPLICATED = ("rel_bias", "dt_bias", "a_log", "d_skip", "ln_g", "ln_b")
SHARDED_SMALL = ("conv_w", "conv_b", "ssm_norm_w")


def kernel(x, w_in_attn, w_out_attn, rel_bias, w_in_ssm, conv_w, conv_b, dt_bias, a_log, d_skip, ssm_norm_w, w_out_ssm, ln_g, ln_b, loss_target, m_w_in_attn, m_w_out_attn, m_rel_bias, m_w_in_ssm, m_conv_w, m_conv_b, m_dt_bias, m_a_log, m_d_skip, m_ssm_norm_w, m_w_out_ssm, m_ln_g, m_ln_b, v_w_in_attn, v_w_out_attn, v_rel_bias, v_w_in_ssm, v_conv_w, v_conv_b, v_dt_bias, v_a_log, v_d_skip, v_ssm_norm_w, v_w_out_ssm, v_ln_g, v_ln_b):
    w = dict(w_in_attn=w_in_attn, w_out_attn=w_out_attn, rel_bias=rel_bias, w_in_ssm=w_in_ssm, conv_w=conv_w,
             conv_b=conv_b, dt_bias=dt_bias, a_log=a_log, d_skip=d_skip, ssm_norm_w=ssm_norm_w,
             w_out_ssm=w_out_ssm, ln_g=ln_g, ln_b=ln_b)
    m = dict(w_in_attn=m_w_in_attn, w_out_attn=m_w_out_attn, rel_bias=m_rel_bias, w_in_ssm=m_w_in_ssm,
             conv_w=m_conv_w, conv_b=m_conv_b, dt_bias=m_dt_bias, a_log=m_a_log, d_skip=m_d_skip,
             ssm_norm_w=m_ssm_norm_w, w_out_ssm=m_w_out_ssm, ln_g=m_ln_g, ln_b=m_ln_b)
    v = dict(w_in_attn=v_w_in_attn, w_out_attn=v_w_out_attn, rel_bias=v_rel_bias, w_in_ssm=v_w_in_ssm,
             conv_w=v_conv_w, conv_b=v_conv_b, dt_bias=v_dt_bias, a_log=v_a_log, d_skip=v_d_skip,
             ssm_norm_w=v_ssm_norm_w, w_out_ssm=v_w_out_ssm, ln_g=v_ln_g, ln_b=v_ln_b)
    me = _lin((lax.axis_index("x"), lax.axis_index("y"), lax.axis_index("c")))
    d = x.shape[2]
    big = ("w_in_attn", "w_out_attn", "w_in_ssm", "w_out_ssm")

    shards = {k: _cast_bf16(w[k][0], name=f"cast_{k}") for k in big}
    (wa,) = _exchange_alone(_Gather([shards["w_in_attn"]]), name="gather_w_in_attn")
    cpd = conv_w.shape[2]
    npd = ssm_norm_w.shape[1]
    small_shapes = [(CONV_WIDTH, cpd), (1, cpd), (1, npd)]
    (small_all,) = _exchange_alone(_Gather([_pack([conv_w[0], conv_b, ssm_norm_w])]), name="gather_small_weights",
                                   in_vmem=True)
    small_parts = [_unpack(small_all[i], small_shapes) for i in range(NDEV)]
    conv_w_full = jnp.concatenate([p[0] for p in small_parts], axis=1)
    conv_b_full = jnp.concatenate([p[1] for p in small_parts], axis=1)
    norm_w_full = jnp.concatenate([p[2] for p in small_parts], axis=1)

    loss, dx, parts, pending, small = _local_step(
        x[0], loss_target[0], wa, shards["w_out_attn"], shards["w_in_ssm"], shards["w_out_ssm"], rel_bias,
        conv_w_full, conv_b_full, dt_bias[0:1], a_log[0:1], d_skip[0:1], norm_w_full, ln_g, ln_b, dist=True)
    loss = lax.psum(loss, MESH_AXES)
    out = {}
    out["w_in_ssm"], late = _adamw_sum(parts["w_in_ssm"], w["w_in_ssm"][0], m["w_in_ssm"][0], v["w_in_ssm"][0],
                                       name="adamw_w_in_ssm", ride=pending)
    parts["w_in_attn"][1] += list(late)
    for k in ("w_out_ssm", "w_out_attn", "w_in_attn"):
        out[k] = _adamw_sum(parts[k], w[k][0], m[k][0], v[k][0], name=f"adamw_{k}")

    order = REPLICATED + SHARDED_SMALL
    g_shapes = [small[k].shape for k in order]
    (g_all,) = _exchange_alone(_Gather([_pack([small[k] for k in order])]), name="gather_small_grads", in_vmem=True)
    g_sum = dict(zip(order, _unpack(_sum_slots(g_all, name="sum_small_grads"), g_shapes)))
    g_mine = {k: g_sum[k] for k in REPLICATED}
    g_mine["conv_w"] = lax.dynamic_slice_in_dim(g_sum["conv_w"], me * cpd, cpd, axis=1)
    g_mine["conv_b"] = lax.dynamic_slice_in_dim(g_sum["conv_b"], me * cpd, cpd, axis=1)
    g_mine["ssm_norm_w"] = lax.dynamic_slice_in_dim(g_sum["ssm_norm_w"], me * npd, npd, axis=1)
    w_shapes = [w[k].shape for k in order]
    g_pack = _pack([g_mine[k] for k in order])
    d_p, m_p, v_p = _adamw_small(g_pack, _pack([w[k] for k in order]), _pack([m[k] for k in order]),
                                 _pack([v[k] for k in order]), name="adamw_small")
    for k, gk, dk, mk, vk in zip(order, _unpack(g_pack, w_shapes), _unpack(d_p, w_shapes), _unpack(m_p, w_shapes),
                                 _unpack(v_p, w_shapes)):
        out[k] = (gk, dk, mk, vk)
    for k in big:
        out[k] = tuple(t[None] for t in out[k])

    names = ("w_in_attn", "w_out_attn", "rel_bias", "w_in_ssm", "conv_w", "conv_b", "dt_bias", "a_log", "d_skip",
             "ssm_norm_w", "w_out_ssm", "ln_g", "ln_b")
    res = [loss, dx[None]]
    for i in range(4):
        res += [out[k][i] for k in names]
    return tuple(res)
```

```python
import functools
import math

import jax
import jax.numpy as jnp
from jax import lax
from jax.experimental import pallas as pl
from jax.experimental.pallas import tpu as pltpu

F32 = jnp.float32
BF16 = jnp.bfloat16
MESH_AXES = ("x", "y", "c")
NDEV = 8
NPEER = NDEV - 1
LANES = 128
SUBLANES = 8
VMEM_LIMIT = 52 * 1024 * 1024
MM_VMEM_BUDGET = 40 * 1024 * 1024
MM_TK_MAX = 4096
MM_TN_MAX = 1024

ATTN_PATTERNS = ((128, 1), (512, 4), (2048, 16))
N_GROUPS_ATTN = 3
HEAD_DIM = 128
ATTN_BLOCK = 128
ATTN_UNROLL = 2
ATTN_ROWS_TIMES_HEADS = 8192
NUM_BUCKETS = 32
MAX_DISTANCE = 2048
SSM_GROUPS = 8
CONV_WIDTH = 4
CHUNK = 128
DEPTH = 2
DEEPNORM_ALPHA = (2 * DEPTH) ** 0.25
LN_EPS = 1e-5
RMS_EPS = 1e-5
NEG_INF = -1e30
ADAM_LR = 0.001
ADAM_B1 = 0.9
ADAM_B2 = 0.999
ADAM_EPS = 1e-08
ADAM_WD = 0.01
ADAM_STEP = 10
HIGHEST = lax.Precision.HIGHEST


def _params(*sem):
    return pltpu.CompilerParams(dimension_semantics=sem, vmem_limit_bytes=VMEM_LIMIT)


def _pick(n, prefs):
    for p in prefs:
        if n % p == 0:
            return p
    return n


def _dot(a, b):
    return jnp.dot(a, b, preferred_element_type=F32)


def _dot_nt(a, b):
    return lax.dot_general(a, b, (((1,), (1,)), ((), ())), preferred_element_type=F32)


def _dot_tn(a, b):
    return lax.dot_general(a, b, (((0,), (0,)), ((), ())), preferred_element_type=F32)


def _sigmoid(x):
    return 1.0 / (1.0 + jnp.exp(-x))


def _mm(a, b, *, name, out_dtype, trans_b=False, slab_out=0, n_off=0, n_out=None,
        res=None, res_scale=1.0, ride=None):
    m, k = a.shape
    slab_b = b.ndim == 3
    if slab_b:
        ns = b.shape[0]
        if trans_b:
            n, kper = b.shape[1], b.shape[2]
            assert ns * kper == k
        else:
            nper = b.shape[2]
            n = ns * nper
            assert b.shape[1] == k
    else:
        n = b.shape[0] if trans_b else b.shape[1]
        assert (b.shape[1] if trans_b else b.shape[0]) == k
    n_out = n if n_out is None else n_out
    tm = _pick(m, (1024, 512, 256, 128))
    nconstraint = math.gcd(n_out, n_off) if n_off else n_out
    if slab_b and not trans_b:
        nconstraint = math.gcd(nconstraint, nper)
    if slab_out:
        nconstraint = math.gcd(nconstraint, n_out // slab_out)
    kconstraint = kper if (slab_b and trans_b) else k
    tk = max(t for t in range(LANES, min(kconstraint, MM_TK_MAX) + 1, LANES) if kconstraint % t == 0)
    nk = k // tk
    out_bytes = jnp.dtype(out_dtype).itemsize

    def vmem_bytes(t):
        return (2 * 2 * tk * (tm + t) + 2 * tm * t * out_bytes + (4 * tm * t if nk > 1 else 0)
                + (2 * 4 * tm * t if res is not None else 0))

    fits = [t for t in range(LANES, min(nconstraint, MM_TN_MAX) + 1, LANES)
            if nconstraint % t == 0 and vmem_bytes(t) <= MM_VMEM_BUDGET]
    tn = max(fits)
    nb0 = n_off // tn
    grid = (m // tm, n_out // tn, nk)

    a_spec = pl.BlockSpec((tm, tk), lambda i, j, kk: (i, kk))
    if slab_b and not trans_b:
        nps = nper // tn
        b_spec = pl.BlockSpec((None, tk, tn), lambda i, j, kk: ((j + nb0) // nps, kk, (j + nb0) % nps))
    elif slab_b and trans_b:
        kps = kper // tk
        b_spec = pl.BlockSpec((None, tn, tk), lambda i, j, kk: (kk // kps, j + nb0, kk % kps))
    elif trans_b:
        b_spec = pl.BlockSpec((tn, tk), lambda i, j, kk: (j + nb0, kk))
    else:
        b_spec = pl.BlockSpec((tk, tn), lambda i, j, kk: (kk, j + nb0))
    if slab_out:
        ops = (n_out // slab_out) // tn
        o_spec = pl.BlockSpec((None, tm, tn), lambda i, j, kk: (j // ops, i, j % ops))
        o_shape = jax.ShapeDtypeStruct((slab_out, m, n_out // slab_out), out_dtype)
    else:
        o_spec = pl.BlockSpec((tm, tn), lambda i, j, kk: (i, j))
        o_shape = jax.ShapeDtypeStruct((m, n_out), out_dtype)
    in_specs = [a_spec, b_spec]
    args = [a, b]
    if res is not None:
        in_specs.append(pl.BlockSpec((tm, tn), lambda i, j, kk: (i, j)))
        args.append(res)

    def body(*refs):
        a_ref, b_ref = refs[0], refs[1]
        r_ref = refs[2] if res is not None else None
        o_ref = refs[3] if res is not None else refs[2]
        av = a_ref[...].astype(BF16)
        bv = b_ref[...].astype(BF16)
        part = _dot_nt(av, bv) if trans_b else _dot(av, bv)

        def finish(r):
            if res is not None:
                r = r + res_scale * r_ref[...]
            o_ref[...] = r.astype(out_dtype)

        if nk == 1:
            finish(part)
            return
        acc = refs[-1]
        kk = pl.program_id(2)

        @pl.when(kk == 0)
        def _():
            acc[...] = part

        @pl.when(kk > 0)
        def _():
            acc[...] += part

        @pl.when(kk == nk - 1)
        def _():
            finish(acc[...])

    outs, rode = _call(
        body, name=name, grid=grid, in_specs=in_specs, out_specs=[o_spec], out_shape=[o_shape], args=args,
        scratch=[pltpu.VMEM((tm, tn), F32)] if nk > 1 else [], ride=ride,
        sem=("parallel", "parallel", "arbitrary"))
    return (outs[0], rode) if ride is not None else outs[0]


def _cast_bf16(w, *, name):
    r, c = w.shape[-2:]
    tr = _pick(r, (512, 256, 128, 64, 32, 16, 8))

    def body(w_ref, o_ref):
        o_ref[...] = w_ref[...].astype(BF16)

    in_spec = (pl.BlockSpec((None, tr, c), lambda i: (0, i, 0)) if w.ndim == 3
               else pl.BlockSpec((tr, c), lambda i: (i, 0)))
    return pl.pallas_call(
        body, name=name, grid=(r // tr,),
        in_specs=[in_spec],
        out_specs=pl.BlockSpec((tr, c), lambda i: (i, 0)),
        out_shape=jax.ShapeDtypeStruct((r, c), BF16),
        compiler_params=_params("parallel"),
    )(w)


def _adam_math(w, g, m, v):
    m2 = ADAM_B1 * m + (1.0 - ADAM_B1) * g
    v2 = ADAM_B2 * v + (1.0 - ADAM_B2) * (g * g)
    m_hat = m2 / (1.0 - ADAM_B1 ** ADAM_STEP)
    v_hat = v2 / (1.0 - ADAM_B2 ** ADAM_STEP)
    delta = -ADAM_LR * (m_hat / (jnp.sqrt(v_hat) + ADAM_EPS) + ADAM_WD * w)
    return delta, m2, v2


def _adamw_sum(bands, w, m, v, *, name, ride=None):
    _, r, c = w.shape
    nband = len(bands)
    rows = r // nband
    tr = _pick(rows, (128, 64, 32, 16, 8))
    tc = c if (c % LANES or c <= 2560) else _pick(c, (2048, 1024, 512, 256, 128))
    nt = rows // tr
    flat = [p for band in bands for p in band]

    def body(*refs):
        p_refs = refs[:len(flat)]
        w_ref, m_ref, v_ref, g_out, d_out, m_out, v_out = refs[len(flat):]
        i = pl.program_id(0)
        g, at = None, 0
        for q, band in enumerate(bands):
            gq = None
            for p_ref in p_refs[at:at + len(band)]:
                for s in range(p_ref.shape[0]):
                    t = p_ref[s].astype(F32)
                    gq = t if gq is None else gq + t
            at += len(band)
            g = gq if q == 0 else jnp.where(i >= q * nt, gq, g)
        d, m2, v2 = _adam_math(w_ref[...], g, m_ref[...], v_ref[...])
        g_out[...] = g
        d_out[...] = d
        m_out[...] = m2
        v_out[...] = v2

    def band_spec(p, q):
        return pl.BlockSpec((p.shape[0], tr, tc), lambda i, j: (0, jnp.clip(i - q * nt, 0, nt - 1), j))

    spec = pl.BlockSpec((None, tr, tc), lambda i, j: (0, i, j))
    shp = jax.ShapeDtypeStruct((1, r, c), F32)
    outs, rode = _call(
        body, name=name, grid=(r // tr, c // tc),
        in_specs=[band_spec(p, q) for q, band in enumerate(bands) for p in band] + [spec, spec, spec],
        out_specs=[spec, spec, spec, spec], out_shape=[shp, shp, shp, shp], args=(*flat, w, m, v),
        sem=("parallel", "parallel"), ride=ride)
    return (outs, rode) if ride is not None else outs


def _adamw_small(g, w, m, v, *, name):
    shp = jax.ShapeDtypeStruct(w.shape, F32)

    def body(g_ref, w_ref, m_ref, v_ref, d_out, m_out, v_out):
        d, m2, v2 = _adam_math(w_ref[...], g_ref[...], m_ref[...], v_ref[...])
        d_out[...] = d
        m_out[...] = m2
        v_out[...] = v2

    return pl.pallas_call(body, name=name, out_shape=(shp, shp, shp),
                          compiler_params=pltpu.CompilerParams(vmem_limit_bytes=VMEM_LIMIT))(g, w, m, v)


def _sum_slots(parts, *, name):
    _, r, c = parts.shape

    def body(p_ref, o_ref):
        g = p_ref[0]
        for s in range(1, NDEV):
            g = g + p_ref[s]
        o_ref[...] = g

    return pl.pallas_call(body, name=name, out_shape=jax.ShapeDtypeStruct((r, c), F32),
                          compiler_params=pltpu.CompilerParams(vmem_limit_bytes=VMEM_LIMIT))(parts)


def _ln_parts(u):
    mu = jnp.mean(u, axis=-1, keepdims=True)
    xc = u - mu
    var = jnp.mean(xc * xc, axis=-1, keepdims=True)
    rstd = lax.rsqrt(var + LN_EPS)
    return xc * rstd, rstd


def _ln_fwd(xin, h, g, b, *, name):
    s, d = xin.shape
    tm = _pick(s, (128,))

    def body(x_ref, h_ref, g_ref, b_ref, o_ref, ob_ref):
        xhat, _ = _ln_parts(DEEPNORM_ALPHA * x_ref[...] + h_ref[...])
        o = xhat * g_ref[...] + b_ref[...]
        o_ref[...] = o
        ob_ref[...] = o.astype(BF16)

    row = pl.BlockSpec((tm, d), lambda i: (i, 0))
    vec = pl.BlockSpec((1, d), lambda i: (0, 0))
    return pl.pallas_call(
        body, name=name, grid=(s // tm,), in_specs=[row, row, vec, vec], out_specs=(row, row),
        out_shape=(jax.ShapeDtypeStruct((s, d), F32), jax.ShapeDtypeStruct((s, d), BF16)),
        compiler_params=_params("parallel"),
    )(xin, h, g, b)


def _ln_bwd(xin, h, g, b, cot, *, with_loss, name):
    s, d = xin.shape
    tm = _pick(s, (128,))

    def body(x_ref, h_ref, g_ref, b_ref, c_ref, du_ref, dub_ref, dg_ref, db_ref, *rest):
        i = pl.program_id(0)
        xhat, rstd = _ln_parts(DEEPNORM_ALPHA * x_ref[...] + h_ref[...])
        gv = g_ref[...]
        if with_loss:
            diff = xhat * gv + b_ref[...] - c_ref[...]
            part = 0.5 * jnp.sum(jnp.mean(diff * diff, axis=-1, keepdims=True), axis=0, keepdims=True)
            dout = diff / d
        else:
            dout = c_ref[...]

        @pl.when(i == 0)
        def _():
            dg_ref[...] = jnp.zeros_like(dg_ref)
            db_ref[...] = jnp.zeros_like(db_ref)
            if with_loss:
                rest[0][...] = jnp.zeros_like(rest[0])

        dg_ref[...] += jnp.sum(dout * xhat, axis=0, keepdims=True)
        db_ref[...] += jnp.sum(dout, axis=0, keepdims=True)
        if with_loss:
            rest[0][...] += jnp.broadcast_to(part, rest[0].shape)
        dxh = dout * gv
        du = rstd * (dxh - jnp.mean(dxh, axis=-1, keepdims=True)
                     - xhat * jnp.mean(dxh * xhat, axis=-1, keepdims=True))
        du_ref[...] = du
        dub_ref[...] = du.astype(BF16)

    row = pl.BlockSpec((tm, d), lambda i: (i, 0))
    vec = pl.BlockSpec((1, d), lambda i: (0, 0))
    out_specs = [row, row, vec, vec]
    out_shape = [jax.ShapeDtypeStruct((s, d), F32), jax.ShapeDtypeStruct((s, d), BF16),
                 jax.ShapeDtypeStruct((1, d), F32), jax.ShapeDtypeStruct((1, d), F32)]
    if with_loss:
        out_specs.append(pl.BlockSpec((SUBLANES, LANES), lambda i: (0, 0)))
        out_shape.append(jax.ShapeDtypeStruct((SUBLANES, LANES), F32))
    return pl.pallas_call(
        body, name=name, grid=(s // tm,), in_specs=[row, row, vec, vec, row],
        out_specs=tuple(out_specs), out_shape=tuple(out_shape),
        compiler_params=_params("arbitrary"),
    )(xin, h, g, b, cot)


def t5_causal_bucket(dist):
    max_exact = NUM_BUCKETS // 2
    d_f = jnp.maximum(dist, 1).astype(jnp.float32)
    large = max_exact + (jnp.log(d_f / max_exact) / math.log(MAX_DISTANCE / max_exact)
                         * (NUM_BUCKETS - max_exact)).astype(jnp.int32)
    large = jnp.minimum(large, NUM_BUCKETS - 1)
    return jnp.where(dist < max_exact, dist, large)


def _bias_tables(rel_bias, heads):
    qi = lax.broadcasted_iota(jnp.int32, (ATTN_BLOCK, 2 * ATTN_BLOCK), 0)
    ki = lax.broadcasted_iota(jnp.int32, (ATTN_BLOCK, 2 * ATTN_BLOCK), 1)
    delta = ATTN_BLOCK + qi - ki
    buckets = []
    for window, dilation in ATTN_PATTERNS:
        span = window // dilation
        assert span == ATTN_BLOCK
        band = (delta >= 0) & (delta <= span)
        buckets.append(jnp.where(band, t5_causal_bucket(jnp.clip(delta, 0, None) * dilation), -1))
    bucket = jnp.stack(buckets).astype(jnp.int32)

    def body(bk_ref, tbl_ref, o_ref):
        col = pl.program_id(0) * heads + pl.program_id(1)
        bk = bk_ref[...]
        acc = jnp.full(bk.shape, NEG_INF, F32)
        for b in range(NUM_BUCKETS):
            acc = jnp.where(bk == b, tbl_ref[b, col], acc)
        o_ref[...] = acc

    tile = (None, ATTN_BLOCK, 2 * ATTN_BLOCK)
    bias = pl.pallas_call(
        body, name="bias_fwd", grid=(N_GROUPS_ATTN, heads),
        in_specs=[pl.BlockSpec(tile, lambda g, h: (g, 0, 0)), pl.BlockSpec(memory_space=pltpu.SMEM)],
        out_specs=pl.BlockSpec((None,) + tile, lambda g, h: (g, h, 0, 0)),
        out_shape=jax.ShapeDtypeStruct((N_GROUPS_ATTN, heads, ATTN_BLOCK, 2 * ATTN_BLOCK), F32),
        compiler_params=_params("parallel", "parallel"),
    )(bucket, rel_bias)
    return bias, bucket


def _dilated_view(qkv, g, dilation, da):
    del g
    return qkv.reshape(qkv.shape[0] // dilation, dilation * 3 * da), 3 * (da // HEAD_DIM), 0


def _heads_per_step(l, heads):
    for hps in (4, 2, 1):
        if heads % hps == 0 and l * hps <= ATTN_ROWS_TIMES_HEADS:
            return hps
    return 1


def _attn_fwd_group(qkv, bias_g, g, dilation, da, ride=None):
    s = qkv.shape[0]
    heads = da // HEAD_DIM
    l = s // dilation
    nb = l // ATTN_BLOCK
    view, cpb, base = _dilated_view(qkv, g, dilation, da)

    hps = _heads_per_step(l, heads)
    lanes = [slice(i * HEAD_DIM, (i + 1) * HEAD_DIM) for i in range(hps)]

    def body(q_ref, k_ref, v_ref, b_ref, o_ref, l_ref):
        scale = HEAD_DIM ** -0.5

        def block(rows, keys, first):
            q, k, v = q_ref[rows, :], k_ref[keys, :], v_ref[keys, :]
            bias = [b_ref[i, :, ATTN_BLOCK:2 * ATTN_BLOCK] if first else b_ref[i] for i in range(hps)]
            sc = [_dot_nt(q[:, hl], k[:, hl]) * scale + bias[i] for i, hl in enumerate(lanes)]
            mx = [jnp.max(t, axis=-1, keepdims=True) for t in sc]
            p = [jnp.exp(t - m) for t, m in zip(sc, mx)]
            den = [jnp.sum(t, axis=-1, keepdims=True) for t in p]
            for i, hl in enumerate(lanes):
                o_ref[rows, hl] = _dot((p[i] * (1.0 / den[i])).astype(BF16), v[:, hl])
                l_ref[rows, hl] = jnp.broadcast_to(mx[i] + jnp.log(den[i]), (ATTN_BLOCK, HEAD_DIM))

        first = pl.ds(0, ATTN_BLOCK)
        block(first, first, True)

        def step(j, carry):
            r0 = pl.multiple_of(j * ATTN_BLOCK, ATTN_BLOCK)
            rk = pl.multiple_of((j - 1) * ATTN_BLOCK, ATTN_BLOCK)
            block(pl.ds(r0, ATTN_BLOCK), pl.ds(rk, 2 * ATTN_BLOCK), False)
            return carry

        if nb > 1:
            lax.fori_loop(1, nb, step, 0)

    def col(t):
        return lambda r, h: (0, (r * cpb + base + t * heads) // hps + h)

    blk = (l, hps * HEAD_DIM)
    out = pl.BlockSpec(blk, lambda r, h: (0, r * (heads // hps) + h))
    shp = jax.ShapeDtypeStruct((l, dilation * da), F32)
    (o, lse), rode = _call(
        body, name=f"attn_fwd_g{g}", grid=(dilation, heads // hps),
        in_specs=[pl.BlockSpec(blk, col(0)), pl.BlockSpec(blk, col(1)), pl.BlockSpec(blk, col(2)),
                  pl.BlockSpec((hps, ATTN_BLOCK, 2 * ATTN_BLOCK), lambda r, h: (h, 0, 0))],
        out_specs=[out, out], out_shape=[shp, shp], args=(view, view, view, bias_g),
        sem=("parallel", "parallel"), ride=ride)
    return o.reshape(s, da), lse.reshape(s, da), rode


def _attn_combine(os_, ls_, gate):
    s, da = gate.shape
    tm = _pick(s, (512, 256, 128))
    tc = _pick(da, (512, 256, 128))

    assert da // HEAD_DIM <= LANES
    per_step = tc // HEAD_DIM

    def body(o0, o1, o2, l0, l1, l2, g_ref, o_ref, l_ref, y_ref):
        j = pl.program_id(1)
        a0, a1, a2 = l0[...], l1[...], l2[...]
        mx = jnp.maximum(jnp.maximum(a0, a1), a2)
        e0, e1, e2 = jnp.exp(a0 - mx), jnp.exp(a1 - mx), jnp.exp(a2 - mx)
        den = e0 + e1 + e2
        o = (e0 * o0[...] + e1 * o1[...] + e2 * o2[...]) / den
        gv = g_ref[...]
        o_ref[...] = o
        y_ref[...] = (o * (gv * _sigmoid(gv))).astype(BF16)
        lse = mx + jnp.log(den)

        @pl.when(j == 0)
        def _():
            l_ref[...] = jnp.zeros_like(l_ref)

        lane = lax.broadcasted_iota(jnp.int32, (1, LANES), 1)
        acc = l_ref[...]
        for i in range(per_step):
            acc = jnp.where(lane == j * per_step + i, lse[:, i * HEAD_DIM:(i + 1) * HEAD_DIM], acc)
        l_ref[...] = acc

    spec = pl.BlockSpec((tm, tc), lambda i, j: (i, j))
    heads_spec = pl.BlockSpec((tm, LANES), lambda i, j: (i, 0))
    return pl.pallas_call(
        body, name="attn_combine", grid=(s // tm, da // tc), in_specs=[spec] * 7, out_specs=(spec, heads_spec, spec),
        out_shape=(jax.ShapeDtypeStruct((s, da), F32), jax.ShapeDtypeStruct((s, LANES), F32),
                   jax.ShapeDtypeStruct((s, da), BF16)),
        compiler_params=_params("parallel", "arbitrary"),
    )(*os_, *ls_, gate)


def _attn_bwd_prep(dy, o, gate):
    s, da = gate.shape
    tm = _pick(s, (512, 256, 128))

    def body(dy_ref, o_ref, g_ref, do_ref, dg_ref, dd_ref):
        j = pl.program_id(1)
        gv = g_ref[...]
        sg = _sigmoid(gv)
        dyv = dy_ref[...]
        ov = o_ref[...]
        do = dyv * (gv * sg)
        do_ref[...] = do.astype(BF16)
        dg_ref[...] = (dyv * ov * (sg * (1.0 + gv * (1.0 - sg)))).astype(BF16)

        @pl.when(j == 0)
        def _():
            dd_ref[...] = jnp.zeros_like(dd_ref)

        lane = lax.broadcasted_iota(jnp.int32, (1, LANES), 1)
        dd_ref[...] = jnp.where(lane == j, jnp.sum(do * ov, axis=-1, keepdims=True), dd_ref[...])

    spec = pl.BlockSpec((tm, HEAD_DIM), lambda i, j: (i, j))
    return pl.pallas_call(
        body, name="attn_bwd_prep", grid=(s // tm, da // HEAD_DIM), in_specs=[spec] * 3,
        out_specs=(spec, spec, pl.BlockSpec((tm, LANES), lambda i, j: (i, 0))),
        out_shape=(jax.ShapeDtypeStruct((s, da), BF16), jax.ShapeDtypeStruct((s, da), BF16),
                   jax.ShapeDtypeStruct((s, LANES), F32)),
        compiler_params=_params("parallel", "arbitrary"),
    )(dy, o, gate)


def _attn_bwd_group(qkv, do, lse, dd, bias_g, g, dilation, da, ride=None):
    s = qkv.shape[0]
    heads = da // HEAD_DIM
    l = s // dilation
    nb = l // ATTN_BLOCK
    view, cpb, base = _dilated_view(qkv, g, dilation, da)
    scale = HEAD_DIM ** -0.5
    hps = _heads_per_step(l, heads)
    lanes = [slice(i * HEAD_DIM, (i + 1) * HEAD_DIM) for i in range(hps)]

    def body(q_ref, k_ref, v_ref, do_ref, l_ref, dd_ref, b_ref, dq_ref, dk_ref, dv_ref, ds_ref, dk_acc, dv_acc):
        h0 = pl.program_id(0) * hps
        r = pl.program_id(1)
        lane = lax.broadcasted_iota(jnp.int32, (1, LANES), 1)

        @pl.when(r == 0)
        def _():
            ds_ref[...] = jnp.zeros_like(ds_ref)

        dk_acc[...] = jnp.zeros_like(dk_acc)
        dv_acc[...] = jnp.zeros_like(dv_acc)

        def block(rows, keys, first):
            q, k, v, dov = q_ref[rows, :], k_ref[keys, :], v_ref[keys, :], do_ref[rows, :]
            lse_all, dd_all = l_ref[rows, :], dd_ref[rows, :]
            pick = [(lane == h0 + i).astype(F32) for i in range(hps)]
            lrow = [jnp.sum(lse_all * m, axis=-1, keepdims=True) for m in pick]
            drow = [jnp.sum(dd_all * m, axis=-1, keepdims=True) for m in pick]
            bias = [b_ref[i, :, ATTN_BLOCK:2 * ATTN_BLOCK] if first else b_ref[i] for i in range(hps)]
            sc = [_dot_nt(q[:, hl], k[:, hl]) for hl in lanes]
            dp = [_dot_nt(dov[:, hl], v[:, hl]) for hl in lanes]
            p = [jnp.exp(sc[i] * scale + bias[i] - lrow[i]) for i in range(hps)]
            ds = [p[i] * (dp[i] - drow[i]) for i in range(hps)]
            dsb = [t.astype(BF16) for t in ds]
            pb = [t.astype(BF16) for t in p]
            for i, hl in enumerate(lanes):
                dq_ref[rows, hl] = (_dot(dsb[i], k[:, hl]) * scale).astype(BF16)
                dk_acc[keys, hl] += _dot_tn(dsb[i], q[:, hl]) * scale
                dv_acc[keys, hl] += _dot_tn(pb[i], dov[:, hl])
                if first:
                    ds_ref[i, :, ATTN_BLOCK:2 * ATTN_BLOCK] += ds[i]
                else:
                    ds_ref[i] += ds[i]

        first = pl.ds(0, ATTN_BLOCK)
        block(first, first, True)

        def step(j, carry):
            r0 = pl.multiple_of(j * ATTN_BLOCK, ATTN_BLOCK)
            rk = pl.multiple_of((j - 1) * ATTN_BLOCK, ATTN_BLOCK)
            block(pl.ds(r0, ATTN_BLOCK), pl.ds(rk, 2 * ATTN_BLOCK), False)
            return carry

        if nb > 1:
            lax.fori_loop(1, nb, step, 0)
        dk_ref[...] = dk_acc[...].astype(BF16)
        dv_ref[...] = dv_acc[...].astype(BF16)

    def col(t):
        return lambda h, r: (0, (r * cpb + base + t * heads) // hps + h)

    blk = (l, hps * HEAD_DIM)
    act = pl.BlockSpec(blk, lambda h, r: (0, r * (heads // hps) + h))
    per_head = pl.BlockSpec((l, LANES), lambda h, r: (0, r))
    tile = pl.BlockSpec((hps, ATTN_BLOCK, 2 * ATTN_BLOCK), lambda h, r: (h, 0, 0))
    shp = jax.ShapeDtypeStruct((l, dilation * da), BF16)
    (dq, dk, dv, ds), rode = _call(
        body, name=f"attn_bwd_g{g}", grid=(heads // hps, dilation),
        in_specs=[pl.BlockSpec(blk, col(0)), pl.BlockSpec(blk, col(1)), pl.BlockSpec(blk, col(2)), act,
                  per_head, per_head, tile],
        out_specs=[act, act, act, tile],
        out_shape=[shp, shp, shp, jax.ShapeDtypeStruct((heads, ATTN_BLOCK, 2 * ATTN_BLOCK), F32)],
        scratch=[pltpu.VMEM(blk, F32), pltpu.VMEM(blk, F32)], sem=("parallel", "arbitrary"), ride=ride,
        args=(view, view, view, do.reshape(l, dilation * da), lse.reshape(l, dilation * LANES),
              dd.reshape(l, dilation * LANES), bias_g))
    return (dq.reshape(s, da), dk.reshape(s, da), dv.reshape(s, da), ds), rode


def _bias_bwd(ds, bucket):
    ng, heads = ds.shape[0], ds.shape[1]

    def body(ds_ref, bk_ref, o_ref):
        bk = bk_ref[...]
        x = ds_ref[...]
        for b in range(NUM_BUCKETS):
            o_ref[:, b:b + 1] = jnp.sum(jnp.where(bk == b, x, 0.0), axis=(0, 1), keepdims=True)

    tile = (None, ATTN_BLOCK, 2 * ATTN_BLOCK)
    out = pl.pallas_call(
        body, name="bias_bwd", grid=(ng, heads),
        in_specs=[pl.BlockSpec((None,) + tile, lambda g, h: (g, h, 0, 0)), pl.BlockSpec(tile, lambda g, h: (g, 0, 0))],
        out_specs=pl.BlockSpec((None, None, 1, NUM_BUCKETS), lambda g, h: (g, h, 0, 0)),
        out_shape=jax.ShapeDtypeStruct((ng, heads, 1, NUM_BUCKETS), F32),
        compiler_params=_params("parallel", "parallel"),
    )(ds, bucket)
    return out.reshape(ng, heads, NUM_BUCKETS)


def _shift_rows(x, halo, s):
    r = pltpu.roll(x, s, axis=0)
    rh = pltpu.roll(halo, s, axis=0)
    row = lax.broadcasted_iota(jnp.int32, halo.shape, 0)
    top = jnp.where(row < s, rh, r[0:SUBLANES])
    return jnp.concatenate([top, r[SUBLANES:]], axis=0)


def _conv_out(x, halo, w, b):
    acc = b + w[CONV_WIDTH - 1:CONV_WIDTH] * x
    for kk in range(CONV_WIDTH - 1):
        acc = acc + w[kk:kk + 1] * _shift_rows(x, halo, CONV_WIDTH - 1 - kk)
    return acc


def _conv_fwd(proj, conv_w, conv_b, col0):
    s = proj.shape[0]
    c = conv_w.shape[1]
    ts = _pick(s, (512, 256, 128))
    tc = _pick(math.gcd(c, col0), (512, 256, 128))
    cb0 = col0 // tc
    hb = ts // SUBLANES

    def body(x_ref, h_ref, w_ref, b_ref, o_ref):
        i = pl.program_id(0)
        halo = jnp.where(i > 0, h_ref[...], 0.0)
        u = _conv_out(x_ref[...], halo, w_ref[...], b_ref[...])
        o_ref[...] = u * _sigmoid(u)

    return pl.pallas_call(
        body, name="conv_fwd", grid=(s // ts, c // tc),
        in_specs=[pl.BlockSpec((ts, tc), lambda i, j: (i, cb0 + j)),
                  pl.BlockSpec((SUBLANES, tc), lambda i, j: (jnp.maximum(i * hb - 1, 0), cb0 + j)),
                  pl.BlockSpec((CONV_WIDTH, tc), lambda i, j: (0, j)),
                  pl.BlockSpec((1, tc), lambda i, j: (0, j))],
        out_specs=pl.BlockSpec((ts, tc), lambda i, j: (i, j)),
        out_shape=jax.ShapeDtypeStruct((s, c), F32),
        compiler_params=_params("parallel", "parallel"),
    )(proj, proj, conv_w, conv_b)


def _conv_bwd(proj, conv_w, conv_b, dacts, col0, dproj):
    s = proj.shape[0]
    c = conv_w.shape[1]
    widths = [d.shape[1] for d in dacts]
    assert sum(widths) == c
    ts = _pick(s, (512, 256, 128))
    tc = _pick(math.gcd(math.gcd(c, col0), math.gcd(*widths)), (512, 256, 128))
    cb0 = col0 // tc
    hb = ts // SUBLANES
    nblk = s // ts
    ext = ts + SUBLANES
    nb = [wd // tc for wd in widths]
    starts = [0, nb[0], nb[0] + nb[1]]

    def body(x_ref, xp_ref, xn_ref, d0, d1, d2, n0, n1, n2, w_ref, b_ref, _, dx_ref, dw_ref, db_ref):
        j = pl.program_id(0)
        i = pl.program_id(1)
        last = i == nblk - 1
        w = w_ref[...]
        halo = jnp.where(i > 0, xp_ref[...], 0.0)
        x = x_ref[...]
        xe = jnp.concatenate([x, xn_ref[...]], axis=0)
        dcur = jnp.where(j < starts[1], d0[...], jnp.where(j < starts[2], d1[...], d2[...]))
        dnext = jnp.where(j < starts[1], n0[...], jnp.where(j < starts[2], n1[...], n2[...]))
        de = jnp.concatenate([dcur, jnp.where(last, 0.0, dnext)], axis=0)
        u = _conv_out(xe, halo, w, b_ref[...])
        sg = _sigmoid(u)
        dpre = de * (sg * (1.0 + u * (1.0 - sg)))
        dx = w[CONV_WIDTH - 1:CONV_WIDTH] * dpre[0:ts]
        for kk in range(CONV_WIDTH - 1):
            sh = CONV_WIDTH - 1 - kk
            dx = dx + w[kk:kk + 1] * pltpu.roll(dpre, ext - sh, axis=0)[0:ts]
        dx_ref[...] = dx.astype(BF16)
        dcur = dpre[0:ts]

        @pl.when(i == 0)
        def _():
            dw_ref[...] = jnp.zeros_like(dw_ref)
            db_ref[...] = jnp.zeros_like(db_ref)

        db_ref[...] += jnp.sum(dcur, axis=0, keepdims=True)
        dw_ref[CONV_WIDTH - 1:CONV_WIDTH, :] += jnp.sum(dcur * x, axis=0, keepdims=True)
        for kk in range(CONV_WIDTH - 1):
            xs = _shift_rows(x, halo, CONV_WIDTH - 1 - kk)
            dw_ref[kk:kk + 1, :] += jnp.sum(dcur * xs, axis=0, keepdims=True)

    cur_p = pl.BlockSpec((ts, tc), lambda j, i: (i, cb0 + j))
    prev_p = pl.BlockSpec((SUBLANES, tc), lambda j, i: (jnp.maximum(i * hb - 1, 0), cb0 + j))
    nxt = lambda i: jnp.minimum((i + 1) * hb, nblk * hb - 1)
    next_p = pl.BlockSpec((SUBLANES, tc), lambda j, i: (nxt(i), cb0 + j))

    def part(q):
        return lambda j: jnp.clip(j - starts[q], 0, nb[q] - 1)

    cur_d = [pl.BlockSpec((ts, tc), lambda j, i, f=part(q): (i, f(j))) for q in range(3)]
    next_d = [pl.BlockSpec((SUBLANES, tc), lambda j, i, f=part(q): (nxt(i), f(j))) for q in range(3)]
    vec4 = pl.BlockSpec((CONV_WIDTH, tc), lambda j, i: (0, j))
    vec1 = pl.BlockSpec((1, tc), lambda j, i: (0, j))
    return pl.pallas_call(
        body, name="conv_bwd", grid=(c // tc, nblk),
        in_specs=[cur_p, prev_p, next_p, *cur_d, *next_d, vec4, vec1, pl.BlockSpec(memory_space=pl.ANY)],
        out_specs=(cur_p, vec4, vec1),
        out_shape=(jax.ShapeDtypeStruct(dproj.shape, dproj.dtype), jax.ShapeDtypeStruct((CONV_WIDTH, c), F32),
                   jax.ShapeDtypeStruct((1, c), F32)),
        input_output_aliases={11: 0},
        compiler_params=_params("parallel", "arbitrary"),
    )(proj, proj, proj, *dacts, *dacts, conv_w, conv_b, dproj)


def _dt_fwd(proj, dt_bias, col0):
    s = proj.shape[0]
    h = dt_bias.shape[1]
    ts = _pick(s, (1024, 512, 256, 128))

    def body(x_ref, b_ref, o_ref):
        v = x_ref[...] + b_ref[...]
        o_ref[...] = jnp.maximum(v, 0.0) + jnp.log1p(jnp.exp(-jnp.abs(v)))

    return pl.pallas_call(
        body, name="dt_fwd", grid=(s // ts,),
        in_specs=[pl.BlockSpec((ts, h), lambda i: (i, col0 // h)), pl.BlockSpec((1, h), lambda i: (0, 0))],
        out_specs=pl.BlockSpec((ts, h), lambda i: (i, 0)), out_shape=jax.ShapeDtypeStruct((s, h), F32),
        compiler_params=_params("parallel"),
    )(proj, dt_bias)


def _dt_bwd(proj, dt_bias, ddt, col0, dproj):
    s = proj.shape[0]
    h = dt_bias.shape[1]
    ts = _pick(s, (1024, 512, 256, 128))

    def body(x_ref, b_ref, d_ref, _, o_ref, db_ref):
        i = pl.program_id(0)
        draw = d_ref[...] * _sigmoid(x_ref[...] + b_ref[...])
        o_ref[...] = draw.astype(BF16)

        @pl.when(i == 0)
        def _():
            db_ref[...] = jnp.zeros_like(db_ref)

        db_ref[...] += jnp.sum(draw, axis=0, keepdims=True)

    return pl.pallas_call(
        body, name="dt_bwd", grid=(s // ts,),
        in_specs=[pl.BlockSpec((ts, h), lambda i: (i, col0 // h)), pl.BlockSpec((1, h), lambda i: (0, 0)),
                  pl.BlockSpec((ts, h), lambda i: (i, 0)), pl.BlockSpec(memory_space=pl.ANY)],
        out_specs=(pl.BlockSpec((ts, h), lambda i: (i, col0 // h)), pl.BlockSpec((1, h), lambda i: (0, 0))),
        out_shape=(jax.ShapeDtypeStruct(dproj.shape, dproj.dtype), jax.ShapeDtypeStruct((1, h), F32)),
        input_output_aliases={3: 0},
        compiler_params=_params("arbitrary"),
    )(proj, dt_bias, ddt, dproj)


def _chunk_terms(dt, dt_t, a, a_t):
    li = lax.broadcasted_iota(jnp.int32, (CHUNK, CHUNK), 0)
    si = lax.broadcasted_iota(jnp.int32, (CHUNK, CHUNK), 1)
    lower = (li >= si).astype(F32)
    upper = (li <= si).astype(F32)
    acum = jnp.dot(lower, dt * a, preferred_element_type=F32, precision=HIGHEST)
    acum_t = jnp.dot(dt_t * a_t, upper, preferred_element_type=F32, precision=HIGHEST)
    return acum, acum_t, li, si, upper


def _dot_exact01(t, m01):
    r = t.shape[0]
    hi = t.astype(BF16)
    rest = t - hi.astype(F32)
    mid = rest.astype(BF16)
    lo = (rest - mid.astype(F32)).astype(BF16)
    out = _dot(jnp.concatenate([hi, mid, lo], axis=0), m01.astype(BF16))
    return out[0:r] + out[r:2 * r] + out[2 * r:3 * r]


def _head_lanes(dt, acum, gw):
    hpg = dt.shape[1]
    p = gw // hpg
    spread = (lax.broadcasted_iota(jnp.int32, (hpg, gw), 1) // p
              == lax.broadcasted_iota(jnp.int32, (hpg, gw), 0)).astype(F32)
    both = _dot_exact01(jnp.concatenate([dt, acum], axis=0), spread)
    dt_e, acum_e = both[0:CHUNK], both[CHUNK:2 * CHUNK]
    alast_e = acum_e[CHUNK - 1:CHUNK, :]
    return dt_e, jnp.exp(acum_e), jnp.exp(alast_e - acum_e), jnp.exp(alast_e)


def _fold_heads(t, hpg):
    gw = t.shape[1]
    p = gw // hpg
    fold = (lax.broadcasted_iota(jnp.int32, (gw, hpg), 0) // p
            == lax.broadcasted_iota(jnp.int32, (gw, hpg), 1)).astype(F32)
    return _dot_exact01(t, fold)


def _ssd_fwd(xbc, proj, dt_g, dt_gt, a_g, a_gt, dskip_e, norm_w, d_inner, n_state):
    s = xbc.shape[0]
    hpg = dt_g.shape[2]
    gw = d_inner // SSM_GROUPS
    p = gw // hpg
    nc = s // CHUNK
    n = n_state
    b0 = d_inner // n
    c0 = b0 + SSM_GROUPS
    per_tile = LANES // p

    def body(xs_ref, b_ref, c_ref, dt_ref, dtt_ref, a_ref, at_ref, z_ref, dsk_ref, nw_ref,
             yn_ref, y_ref, st_ref, state):
        c = pl.program_id(1)

        @pl.when(c == 0)
        def _():
            state[...] = jnp.zeros_like(state)

        st = state[...]
        st_ref[...] = st
        xs = xs_ref[...]
        bm = b_ref[...].astype(BF16)
        cm = c_ref[...].astype(BF16)
        dt = dt_ref[...]
        acum, acum_t, li, si, _ = _chunk_terms(dt, dtt_ref[...], a_ref[...], at_ref[...])
        dt_e, e_a, t_e, e_last = _head_lanes(dt, acum, gw)
        xdt = xs * dt_e
        xdtb = xdt.astype(BF16)
        cb = _dot_nt(cm, bm)
        causal = li >= si
        lane = lax.broadcasted_iota(jnp.int32, (1, LANES), 1)
        y_ref[...] = _dot(cm, st.astype(BF16)) * e_a
        for q in range(gw // LANES):
            ql = slice(q * LANES, (q + 1) * LANES)
            xq = xdtb[:, ql]
            ms = []
            for i in range(per_tile):
                h = q * per_tile + i
                decay = jnp.exp(jnp.where(causal, acum[:, h:h + 1] - acum_t[h:h + 1, :], NEG_INF))
                ms.append((cb * decay).astype(BF16))
            y_all = _dot(jnp.concatenate(ms, axis=0), xq)
            yd = y_all[0:CHUNK]
            for i in range(1, per_tile):
                yd = jnp.where(lane >= i * p, y_all[i * CHUNK:(i + 1) * CHUNK], yd)
            y_ref[:, ql] += yd
        state[...] = st * e_last + _dot_tn(bm, (xdt * t_e).astype(BF16))
        yt = y_ref[...] + xs * dsk_ref[...]
        z = z_ref[...]
        yz = yt * (z * _sigmoid(z))
        r = lax.rsqrt(jnp.mean(yz * yz, axis=-1, keepdims=True) + RMS_EPS)
        yn_ref[...] = (yz * r * nw_ref[...]).astype(BF16)

    wide = pl.BlockSpec((CHUNK, gw), lambda g, c: (c, g))
    return pl.pallas_call(
        body, name="ssd_fwd", grid=(SSM_GROUPS, nc),
        in_specs=[wide,
                  pl.BlockSpec((CHUNK, n), lambda g, c: (c, b0 + g)),
                  pl.BlockSpec((CHUNK, n), lambda g, c: (c, c0 + g)),
                  pl.BlockSpec((None, CHUNK, hpg), lambda g, c: (g, c, 0)),
                  pl.BlockSpec((None, hpg, CHUNK), lambda g, c: (g, 0, c)),
                  pl.BlockSpec((None, 1, hpg), lambda g, c: (g, 0, 0)),
                  pl.BlockSpec((None, hpg, 1), lambda g, c: (g, 0, 0)),
                  wide,
                  pl.BlockSpec((None, 1, gw), lambda g, c: (g, 0, 0)),
                  pl.BlockSpec((1, gw), lambda g, c: (0, g))],
        out_specs=(wide, wide, pl.BlockSpec((None, None, n, gw), lambda g, c: (g, c, 0, 0))),
        out_shape=(jax.ShapeDtypeStruct((s, d_inner), BF16), jax.ShapeDtypeStruct((s, d_inner), F32),
                   jax.ShapeDtypeStruct((SSM_GROUPS, nc, n, gw), F32)),
        scratch_shapes=[pltpu.VMEM((n, gw), F32)],
        compiler_params=_params("parallel", "arbitrary"),
    )(xbc, xbc, xbc, dt_g, dt_gt, a_g, a_gt, proj, dskip_e, norm_w)


def _ssd_epilogue_bwd(dyn, y, xbc, proj, dskip_e, norm_w, hpg):
    s, d_inner = dyn.shape
    gw = d_inner // SSM_GROUPS
    p = gw // hpg
    nc = s // CHUNK

    def body(dyn_ref, y_ref, xs_ref, z_ref, dsk_ref, nw_ref, dy_ref, dz_ref, dnw_ref, ddsk_ref):
        c = pl.program_id(1)
        xs = xs_ref[...]
        z = z_ref[...]
        yt = y_ref[...] + xs * dsk_ref[...]
        sg = _sigmoid(z)
        sz = z * sg
        yz = yt * sz
        r = lax.rsqrt(jnp.mean(yz * yz, axis=-1, keepdims=True) + RMS_EPS)
        dynv = dyn_ref[...]
        dyh = dynv * nw_ref[...]
        dyz = r * (dyh - yz * (r * r) * jnp.mean(dyh * yz, axis=-1, keepdims=True))
        dyt = dyz * sz
        dy_ref[...] = dyt
        dz_ref[...] = (dyz * yt * (sg * (1.0 + z * (1.0 - sg)))).astype(BF16)

        @pl.when(c == 0)
        def _():
            dnw_ref[...] = jnp.zeros_like(dnw_ref)
            ddsk_ref[...] = jnp.zeros_like(ddsk_ref)

        dnw_ref[...] += jnp.sum(dynv * yz * r, axis=0, keepdims=True)
        colsum = jnp.sum(dyt * xs, axis=0, keepdims=True)
        fold = (lax.broadcasted_iota(jnp.int32, (gw, hpg), 0) // p
                == lax.broadcasted_iota(jnp.int32, (gw, hpg), 1)).astype(F32)
        ddsk_ref[...] += jnp.dot(colsum, fold, preferred_element_type=F32, precision=HIGHEST)

    wide = pl.BlockSpec((CHUNK, gw), lambda g, c: (c, g))
    return pl.pallas_call(
        body, name="ssd_epilogue_bwd", grid=(SSM_GROUPS, nc),
        in_specs=[wide, wide, wide, wide, pl.BlockSpec((None, 1, gw), lambda g, c: (g, 0, 0)),
                  pl.BlockSpec((1, gw), lambda g, c: (0, g))],
        out_specs=(wide, wide, pl.BlockSpec((1, gw), lambda g, c: (0, g)),
                   pl.BlockSpec((None, 1, hpg), lambda g, c: (g, 0, 0))),
        out_shape=(jax.ShapeDtypeStruct((s, d_inner), F32), jax.ShapeDtypeStruct((s, proj.shape[1]), BF16),
                   jax.ShapeDtypeStruct((1, d_inner), F32), jax.ShapeDtypeStruct((SSM_GROUPS, 1, hpg), F32)),
        compiler_params=_params("parallel", "arbitrary"),
    )(dyn, y, xbc, proj, dskip_e, norm_w)


def _ssd_scan_bwd(xbc, dt_g, dt_gt, a_g, a_gt, states, dy, dskip_e, d_inner, n_state, ride=None):
    s = xbc.shape[0]
    hpg = dt_g.shape[2]
    gw = d_inner // SSM_GROUPS
    p = gw // hpg
    nc = s // CHUNK
    n = n_state
    b0 = d_inner // n
    c0 = b0 + SSM_GROUPS
    per_tile = LANES // p

    def body(xs_ref, b_ref, c_ref, dt_ref, dtt_ref, a_ref, at_ref, st_ref, dy_ref, dsk_ref,
             dxs_ref, db_ref, dc_ref, ddt_ref, da_ref, dstate, ydiag_ref, dxd_ref):
        c = pl.program_id(1)

        @pl.when(c == 0)
        def _():
            dstate[...] = jnp.zeros_like(dstate)
            da_ref[...] = jnp.zeros_like(da_ref)

        xs = xs_ref[...]
        bm = b_ref[...].astype(BF16)
        cm = c_ref[...].astype(BF16)
        dt = dt_ref[...]
        a = a_ref[...]
        dyv = dy_ref[...]
        dsk = dsk_ref[...]
        acum, acum_t, li, si, upper = _chunk_terms(dt, dtt_ref[...], a, at_ref[...])
        dt_e, e_a, t_e, e_last = _head_lanes(dt, acum, gw)
        cb = _dot_nt(cm, bm)
        lower_mask = li >= si
        lane = lax.broadcasted_iota(jnp.int32, (1, LANES), 1)
        row_l = lax.broadcasted_iota(jnp.int32, (CHUNK, 1), 0)
        st = st_ref[...]
        stb = st.astype(BF16)
        dst = dstate[...]
        dstb = dst.astype(BF16)
        xdt = xs * dt_e
        xdtb = xdt.astype(BF16)
        dyb = dyv.astype(BF16)
        dye = dyv * e_a
        dyeb = dye.astype(BF16)
        xte = xdt * t_e
        xteb = xte.astype(BF16)
        wv = _dot(bm, dstb)
        yo = _dot(cm, stb)
        dcb = jnp.zeros((CHUNK, CHUNK), F32)
        for q in range(gw // LANES):
            ql = slice(q * LANES, (q + 1) * LANES)
            xq = xdtb[:, ql]
            dq = dyb[:, ql]
            decays, ms, mts, dqs = [], [], [], []
            for i in range(per_tile):
                h = q * per_tile + i
                decay = jnp.exp(jnp.where(lower_mask, acum[:, h:h + 1] - acum_t[h:h + 1, :], NEG_INF))
                mm = cb * decay
                mine = (lane >= i * p) & (lane < (i + 1) * p)
                decays.append(decay)
                ms.append(mm.astype(BF16))
                mts.append(mm.T.astype(BF16))
                dqs.append(jnp.where(mine, dq, jnp.zeros_like(dq)))
            dm_all = _dot_nt(jnp.concatenate(dqs, axis=0), xq)
            y_all = _dot(jnp.concatenate(ms, axis=0), xq)
            d_all = _dot(jnp.concatenate(mts, axis=0), dq)
            yd = dd = None
            for i in range(per_tile):
                rows = slice(i * CHUNK, (i + 1) * CHUNK)
                dcb = dcb + dm_all[rows] * decays[i]
                yd = y_all[rows] if i == 0 else jnp.where(lane >= i * p, y_all[rows], yd)
                dd = d_all[rows] if i == 0 else jnp.where(lane >= i * p, d_all[rows], dd)
            ydiag_ref[:, ql] = yd
            dxd_ref[:, ql] = dd
        ydiag = ydiag_ref[...]
        dxd = dxd_ref[...]
        dxdt = dxd + t_e * wv
        xw = xte * wv
        last_in = jnp.sum(xw, axis=0, keepdims=True) + e_last * jnp.sum(dst * st, axis=0, keepdims=True)
        folded = _fold_heads(jnp.concatenate(
            [dyb.astype(F32) * ydiag - xdtb.astype(F32) * dxd - xw + dye * yo, dxdt * xs,
             jnp.broadcast_to(last_in, (SUBLANES, gw))],
            axis=0), hpg)
        dalast = folded[2 * CHUNK:2 * CHUNK + 1]
        d_acum = folded[0:CHUNK] + jnp.where(row_l == CHUNK - 1, dalast, 0.0)
        ddt_x = folded[CHUNK:2 * CHUNK]
        dxs_ref[...] = dxdt * dt_e + dyv * dsk
        dbf = dcb.astype(BF16)
        dc_ref[...] = _dot_nt(dyeb, stb) + _dot(dbf, bm)
        db_ref[...] = _dot_nt(xteb, dstb) + _dot_tn(dbf, cm)
        dstate[...] = dst * e_last + _dot_tn(cm, dyeb)
        d_da = jnp.dot(upper, d_acum, preferred_element_type=F32, precision=HIGHEST)
        ddt_ref[...] = d_da * a + ddt_x
        da_ref[...] += jnp.sum(d_da * dt, axis=0, keepdims=True)

    rev = lambda c: nc - 1 - c
    wide = pl.BlockSpec((CHUNK, gw), lambda g, c: (rev(c), g))
    return _call(
        body, name="ssd_scan_bwd", grid=(SSM_GROUPS, nc),
        in_specs=[wide,
                  pl.BlockSpec((CHUNK, n), lambda g, c: (rev(c), b0 + g)),
                  pl.BlockSpec((CHUNK, n), lambda g, c: (rev(c), c0 + g)),
                  pl.BlockSpec((None, CHUNK, hpg), lambda g, c: (g, rev(c), 0)),
                  pl.BlockSpec((None, hpg, CHUNK), lambda g, c: (g, 0, rev(c))),
                  pl.BlockSpec((None, 1, hpg), lambda g, c: (g, 0, 0)),
                  pl.BlockSpec((None, hpg, 1), lambda g, c: (g, 0, 0)),
                  pl.BlockSpec((None, None, n, gw), lambda g, c: (g, rev(c), 0, 0)),
                  wide,
                  pl.BlockSpec((None, 1, gw), lambda g, c: (g, 0, 0))],
        out_specs=[wide,
                   pl.BlockSpec((CHUNK, n), lambda g, c: (rev(c), g)),
                   pl.BlockSpec((CHUNK, n), lambda g, c: (rev(c), g)),
                   pl.BlockSpec((None, CHUNK, hpg), lambda g, c: (g, rev(c), 0)),
                   pl.BlockSpec((None, 1, hpg), lambda g, c: (g, 0, 0))],
        out_shape=[jax.ShapeDtypeStruct((s, d_inner), F32),
                   jax.ShapeDtypeStruct((s, SSM_GROUPS * n), F32), jax.ShapeDtypeStruct((s, SSM_GROUPS * n), F32),
                   jax.ShapeDtypeStruct((SSM_GROUPS, s, hpg), F32), jax.ShapeDtypeStruct((SSM_GROUPS, 1, hpg), F32)],
        scratch=[pltpu.VMEM((n, gw), F32), pltpu.VMEM((CHUNK, gw), F32), pltpu.VMEM((CHUNK, gw), F32)],
        sem=("parallel", "arbitrary"), ride=ride,
        args=(xbc, xbc, xbc, dt_g, dt_gt, a_g, a_gt, states, dy, dskip_e))


def _lin(p):
    return 4 * p[0] + 2 * p[1] + p[2]


class _Gather:
    def __init__(self, arrs):
        self.arrs = list(arrs)

    def out_shape(self):
        return [jax.ShapeDtypeStruct((NDEV,) + a.shape, a.dtype) for a in self.arrs]

    def _copies(self, ins, outs, sems):
        send_sems, recv_sems, local_sems = sems
        x, y, c = lax.axis_index("x"), lax.axis_index("y"), lax.axis_index("c")
        me, sibling = (x, y, c), (x, y, 1 - c)
        chips = [(1 - x, y), (x, 1 - y), (1 - x, 1 - y)]

        def copy(a, k, block, to, src=None):
            rows = outs[a].at[_lin(block)]
            return pltpu.make_async_remote_copy(
                src_ref=rows if src is None else src, dst_ref=rows,
                send_sem=send_sems.at[a * NPEER + k], recv_sem=recv_sems.at[a * NPEER + k],
                device_id=to, device_id_type=pl.DeviceIdType.MESH)

        na = len(ins)
        mine = [pltpu.make_async_copy(ins[a], outs[a].at[_lin(me)], local_sems.at[a]) for a in range(na)]
        first = []
        for a in range(na):
            first.append(copy(a, 0, me, sibling, src=ins[a]))
            first += [copy(a, 1 + j, me, (*chip, c), src=ins[a]) for j, chip in enumerate(chips)]
        return copy, mine, first, me, sibling, chips, c, na

    def start(self, ins, outs, sems):
        _, mine, first, *_ = self._copies(ins, outs, sems)
        for cp in mine + first:
            cp.start()

    def finish(self, ins, outs, sems):
        copy, mine, first, me, sibling, chips, c, na = self._copies(ins, outs, sems)
        passed = []
        for j, chip in enumerate(chips):
            for a in range(na):
                copy(a, 1 + j, (*chip, c), me).wait_recv()
                cp = copy(a, 4 + j, (*chip, c), sibling)
                cp.start()
                passed.append(cp)
        for a in range(na):
            copy(a, 0, sibling, me).wait_recv()
            for j, chip in enumerate(chips):
                copy(a, 4 + j, (*chip, 1 - c), me).wait_recv()
        for cp in first + passed:
            cp.wait_send()
        for cp in mine:
            cp.wait()


class _Scatter:
    def __init__(self, arrs, ks=tuple(range(NDEV))):
        self.arrs = list(arrs)
        self.ks = [tuple(k) for k in ks] if isinstance(ks[0], (tuple, list)) else [tuple(ks)] * len(self.arrs)
        assert len(self.ks) == len(self.arrs)

    def out_shape(self):
        return [jax.ShapeDtypeStruct((len(k),) + a.shape[1:], a.dtype) for a, k in zip(self.arrs, self.ks)]

    def _copies(self, ins, outs, sems):
        send_sems, recv_sems, local_sems = sems
        x, y, c = lax.axis_index("x"), lax.axis_index("y"), lax.axis_index("c")
        me = (x, y, c)

        def peer(k):
            return (1 - x if k & 4 else x, 1 - y if k & 2 else y, 1 - c if k & 1 else c)

        local, remote = [], []
        for a in range(len(ins)):
            for i, k in enumerate(self.ks[a]):
                if k == 0:
                    local.append(pltpu.make_async_copy(ins[a].at[_lin(me)], outs[a].at[i], local_sems.at[a]))
                else:
                    remote.append(pltpu.make_async_remote_copy(
                        src_ref=ins[a].at[_lin(peer(k))], dst_ref=outs[a].at[i],
                        send_sem=send_sems.at[a * NPEER + k - 1], recv_sem=recv_sems.at[a * NPEER + k - 1],
                        device_id=peer(k), device_id_type=pl.DeviceIdType.MESH))
        return local, remote

    def start(self, ins, outs, sems):
        local, remote = self._copies(ins, outs, sems)
        for cp in local + remote:
            cp.start()

    def finish(self, ins, outs, sems):
        local, remote = self._copies(ins, outs, sems)
        for cp in remote:
            cp.wait_recv()
        for cp in remote:
            cp.wait_send()
        for cp in local:
            cp.wait()


def _exchange_scratch(na):
    return [pltpu.SemaphoreType.DMA((na * NPEER,)), pltpu.SemaphoreType.DMA((na * NPEER,)),
            pltpu.SemaphoreType.DMA((na,))]


def _exchange_alone(ex, *, name, in_vmem=False):
    na = len(ex.arrs)

    def body(*refs):
        ins, outs, sems = refs[:na], refs[na:2 * na], refs[2 * na:]
        ex.start(ins, outs, sems)
        ex.finish(ins, outs, sems)

    spec = pl.BlockSpec(memory_space=pltpu.VMEM if in_vmem else pl.ANY)
    return pl.pallas_call(
        body, name=name, out_shape=tuple(ex.out_shape()), in_specs=[spec] * na, out_specs=tuple([spec] * na),
        scratch_shapes=_exchange_scratch(na),
        compiler_params=pltpu.CompilerParams(vmem_limit_bytes=VMEM_LIMIT),
    )(*ex.arrs)


def _call(body, *, name, grid, in_specs, out_specs, out_shape, args, sem, scratch=(), ride=None, aliases=None):
    n_in, n_out, n_scr = len(in_specs), len(out_specs), len(scratch)
    if ride is None:
        outs = pl.pallas_call(
            body, name=name, grid=grid, in_specs=list(in_specs), out_specs=tuple(out_specs),
            out_shape=tuple(out_shape), scratch_shapes=list(scratch), input_output_aliases=aliases or {},
            compiler_params=_params(*sem))(*args)
        return tuple(outs), ()
    nx = len(ride.arrs)
    hbm = pl.BlockSpec(memory_space=pl.ANY)

    def hosted(*refs):
        ins, x_in = refs[:n_in], refs[n_in:n_in + nx]
        o0 = n_in + nx
        outs, x_out = refs[o0:o0 + n_out], refs[o0 + n_out:o0 + n_out + nx]
        s0 = o0 + n_out + nx
        scr, x_sem = refs[s0:s0 + n_scr], refs[s0 + n_scr:]
        ids = [pl.program_id(i) for i in range(len(grid))]
        first = functools.reduce(jnp.logical_and, [i == 0 for i in ids])
        last = functools.reduce(jnp.logical_and, [i == g - 1 for i, g in zip(ids, grid)])

        @pl.when(first)
        def _():
            ride.start(x_in, x_out, x_sem)

        body(*ins, *outs, *scr)

        @pl.when(last)
        def _():
            ride.finish(x_in, x_out, x_sem)

    outs = pl.pallas_call(
        hosted, name=name, grid=grid, in_specs=list(in_specs) + [hbm] * nx,
        out_specs=tuple(list(out_specs) + [hbm] * nx), out_shape=tuple(list(out_shape) + ride.out_shape()),
        scratch_shapes=list(scratch) + _exchange_scratch(nx), input_output_aliases=aliases or {},
        compiler_params=_params(*(("arbitrary",) * len(grid))))(*args, *ride.arrs)
    return tuple(outs[:n_out]), tuple(outs[n_out:])


def _pack(parts):
    flat = jnp.concatenate([p.reshape(-1).astype(F32) for p in parts])
    tile = SUBLANES * LANES
    pad = (-flat.shape[0]) % tile
    return jnp.pad(flat, (0, pad)).reshape(-1, LANES)


def _unpack(buf, shapes):
    flat = buf.reshape(-1)
    out, off = [], 0
    for shp in shapes:
        size = math.prod(shp)
        out.append(flat[off:off + size].reshape(shp))
        off += size
    return out


KS_FLAT = (0, 1, 4, 5, 2, 3)
KS_DIAG = (6, 7)


def _local_step(x, target, wa, wo, ws, wos, rel_bias, conv_w, conv_b, dt_bias, a_log, d_skip, norm_w, ln_g, ln_b,
                dist=False):
    s, d = x.shape
    da = wo.shape[-2]
    heads = da // HEAD_DIM
    qkv_cols = 3 * N_GROUPS_ATTN * da
    d_inner = wos.shape[0] * (NDEV if dist else 1)
    conv_dim = conv_w.shape[1]
    ssm_heads = dt_bias.shape[1]
    hpg = ssm_heads // SSM_GROUPS
    gn = (conv_dim - d_inner) // 2
    n_state = gn // SSM_GROUPS
    gw = d_inner // SSM_GROUPS
    p = gw // hpg
    in_ssm = d_inner + conv_dim + ssm_heads
    xb = _cast_bf16(x, name="cast_x")

    def slabs_of_cols(t):
        return t.reshape(t.shape[0], NDEV, t.shape[1] // NDEV).transpose(1, 0, 2)

    band_rows = (3 * d // 8, 3 * d // 8, d // 4)
    qkvs, ws_bands, row0 = [], [], 0
    for g in range(N_GROUPS_ATTN):
        ride = _Gather([ws[row0:row0 + band_rows[g]]]) if dist else None
        row0 += band_rows[g]
        got = _mm(xb, wa, name=f"mm_qkv_g{g}", out_dtype=BF16, n_off=g * 3 * da, n_out=3 * da, ride=ride)
        if dist:
            ws_bands.append(got[1][0])
            got = got[0]
        qkvs.append(got)
    if dist:
        ws = jnp.concatenate(ws_bands, axis=1).transpose(1, 0, 2).reshape(d, in_ssm)
    gate = _mm(xb, wa, name="mm_gate", out_dtype=F32, n_off=qkv_cols, n_out=da)
    bias, bucket = _bias_tables(rel_bias, heads)
    os_, ls_ = [], []
    for g, (_, dil) in enumerate(ATTN_PATTERNS):
        ride = _Gather([wo]) if dist and g == 0 else None
        o, l, rode = _attn_fwd_group(qkvs[g], bias[g], g, dil, da, ride=ride)
        if rode:
            (wo,) = rode
        os_.append(o)
        ls_.append(l)
    o, lse, y = _attn_combine(os_, ls_, gate)
    h1 = _mm(y, wo, name="mm_out_attn", out_dtype=F32)
    x1, x1b = _ln_fwd(x, h1, ln_g[0:1], ln_b[0:1], name="ln1_fwd")

    if dist:
        proj, (wos_slabs,) = _mm(x1b, ws, name="mm_in_ssm", out_dtype=F32, ride=_Gather([wos]))
        wos = wos_slabs.reshape(d_inner, d)
    else:
        proj = _mm(x1b, ws, name="mm_in_ssm", out_dtype=F32)
    xbc = _conv_fwd(proj, conv_w, conv_b, d_inner)
    dt = _dt_fwd(proj, dt_bias, d_inner + conv_dim)
    dt_g = dt.reshape(s, SSM_GROUPS, hpg).transpose(1, 0, 2)
    dt_gt = dt.reshape(s, SSM_GROUPS, hpg).transpose(1, 2, 0)
    a = -jnp.exp(a_log)
    a_g = a.reshape(SSM_GROUPS, 1, hpg)
    a_gt = a.reshape(SSM_GROUPS, hpg, 1)
    dskip_e = jnp.repeat(d_skip.reshape(SSM_GROUPS, 1, hpg), p, axis=2)
    yn, yscan, states = _ssd_fwd(xbc, proj, dt_g, dt_gt, a_g, a_gt, dskip_e, norm_w, d_inner, n_state)
    h2 = _mm(yn, wos, name="mm_out_ssm", out_dtype=F32)

    du2, du2b, dg1, db1, loss_t = _ln_bwd(x1, h2, ln_g[1:2], ln_b[1:2], target, with_loss=True, name="ln2_loss_bwd")
    loss = loss_t[0, 0]
    dyn = _mm(du2b, wos, name="mm_dyn", out_dtype=F32, trans_b=True)
    g_wos = _mm(yn.T, du2b, name="mm_dw_out_ssm", out_dtype=BF16)
    parts = {}
    dyscan, dproj_ssm, g_norm, g_dskip = _ssd_epilogue_bwd(dyn, yscan, xbc, proj, dskip_e, norm_w, hpg)
    ride = _Scatter([g_wos.reshape(NDEV, d_inner // NDEV, d)]) if dist else None
    (dxs, d_bm, d_cm, ddt_g, g_a), rode = _ssd_scan_bwd(xbc, dt_g, dt_gt, a_g, a_gt, states, dyscan, dskip_e,
                                                         d_inner, n_state, ride=ride)
    parts["w_out_ssm"] = [list(rode)]
    g_alog = g_a.reshape(1, ssm_heads) * a
    dproj_ssm, g_conv_w, g_conv_b = _conv_bwd(proj, conv_w, conv_b, (dxs, d_bm, d_cm), d_inner, dproj_ssm)
    ddt = ddt_g.transpose(1, 0, 2).reshape(s, ssm_heads)
    dproj_ssm, g_dtb = _dt_bwd(proj, dt_bias, ddt, d_inner + conv_dim, dproj_ssm)
    g_ws = _mm(x1b.T, dproj_ssm, name="mm_dw_in_ssm", out_dtype=BF16)
    if dist:
        g_ws_slabs = slabs_of_cols(g_ws)
        dx1, near = _mm(dproj_ssm, ws, name="mm_dx1", out_dtype=F32, trans_b=True, res=du2,
                        res_scale=DEEPNORM_ALPHA, ride=_Scatter([g_ws_slabs], KS_FLAT))
    else:
        dx1 = _mm(dproj_ssm, ws, name="mm_dx1", out_dtype=F32, trans_b=True, res=du2, res_scale=DEEPNORM_ALPHA)

    du1, du1b, dg0, db0 = _ln_bwd(x, h1, ln_g[0:1], ln_b[0:1], dx1, with_loss=False, name="ln1_bwd")
    dy = _mm(du1b, wo, name="mm_dy", out_dtype=F32, trans_b=True)
    g_wo = _mm(y.T, du1b, name="mm_dw_out_attn", out_dtype=BF16, slab_out=NDEV)
    do, dgate, dd = _attn_bwd_prep(dy, o, gate)
    rides = [_Scatter([g_ws_slabs], KS_DIAG[0:1]), _Scatter([g_wo]), None] if dist else [None] * N_GROUPS_ATTN
    dparts, dss, rode_attn = [], [], []
    for g, (_, dil) in enumerate(ATTN_PATTERNS):
        (dq, dk, dv, ds), rode = _attn_bwd_group(qkvs[g], do, lse, dd, bias[g], g, dil, da, ride=rides[g])
        dparts += [dq, dk, dv]
        dss.append(ds)
        rode_attn += list(rode)
    g_bias = _bias_bwd(jnp.stack(dss), bucket)
    g_rel_bias = g_bias.transpose(2, 0, 1).reshape(NUM_BUCKETS, N_GROUPS_ATTN * heads)
    dproj_attn = jnp.concatenate(dparts + [dgate], axis=1)
    pending = None
    if dist:
        parts["w_out_attn"] = [rode_attn[1:]]
        xbt = xb.T
        half = d // 2
        g_top, (diag_b,) = _mm(xbt[:half], dproj_attn, name="mm_dw_in_attn_top", out_dtype=BF16, slab_out=NDEV,
                               ride=_Scatter([g_ws_slabs], KS_DIAG[1:2]))
        parts["w_in_ssm"] = [[near[0], rode_attn[0], diag_b]]
        g_bot, (top_a,) = _mm(xbt[half:], dproj_attn, name="mm_dw_in_attn_bottom", out_dtype=BF16, slab_out=NDEV,
                              ride=_Scatter([g_top], KS_FLAT))
        dx, (top_b, bot_a) = _mm(dproj_attn, wa, name="mm_dx", out_dtype=F32, trans_b=True, res=du1,
                                 res_scale=DEEPNORM_ALPHA, ride=_Scatter([g_top, g_bot], [KS_DIAG, KS_FLAT]))
        parts["w_in_attn"] = [[top_a, top_b], [bot_a]]
        pending = _Scatter([g_bot], KS_DIAG)
    else:
        g_wa = _mm(xb.T, dproj_attn, name="mm_dw_in_attn", out_dtype=BF16, slab_out=NDEV)
        dx = _mm(dproj_attn, wa, name="mm_dx", out_dtype=F32, trans_b=True, res=du1, res_scale=DEEPNORM_ALPHA)

    g_ln_g = jnp.concatenate([dg0, dg1], axis=0)
    g_ln_b = jnp.concatenate([db0, db1], axis=0)
    small = dict(rel_bias=g_rel_bias, dt_bias=g_dtb, a_log=g_alog, d_skip=g_dskip.reshape(1, ssm_heads),
                 ln_g=g_ln_g, ln_b=g_ln_b, conv_w=g_conv_w, conv_b=g_conv_b, ssm_norm_w=g_norm)
    if dist:
        return loss, dx, parts, pending, small
    return loss, dx, g_wa, g_wo, g_ws, g_wos, small


REPLICATED = ("rel_bias", "dt_bias", "a_log", "d_skip", "ln_g", "ln_b")
SHARDED_SMALL = ("conv_w", "conv_b", "ssm_norm_w")


def kernel(x, w_in_attn, w_out_attn, rel_bias, w_in_ssm, conv_w, conv_b, dt_bias, a_log, d_skip, ssm_norm_w, w_out_ssm, ln_g, ln_b, loss_target, m_w_in_attn, m_w_out_attn, m_rel_bias, m_w_in_ssm, m_conv_w, m_conv_b, m_dt_bias, m_a_log, m_d_skip, m_ssm_norm_w, m_w_out_ssm, m_ln_g, m_ln_b, v_w_in_attn, v_w_out_attn, v_rel_bias, v_w_in_ssm, v_conv_w, v_conv_b, v_dt_bias, v_a_log, v_d_skip, v_ssm_norm_w, v_w_out_ssm, v_ln_g, v_ln_b):
    w = dict(w_in_attn=w_in_attn, w_out_attn=w_out_attn, rel_bias=rel_bias, w_in_ssm=w_in_ssm, conv_w=conv_w,
             conv_b=conv_b, dt_bias=dt_bias, a_log=a_log, d_skip=d_skip, ssm_norm_w=ssm_norm_w,
             w_out_ssm=w_out_ssm, ln_g=ln_g, ln_b=ln_b)
    m = dict(w_in_attn=m_w_in_attn, w_out_attn=m_w_out_attn, rel_bias=m_rel_bias, w_in_ssm=m_w_in_ssm,
             conv_w=m_conv_w, conv_b=m_conv_b, dt_bias=m_dt_bias, a_log=m_a_log, d_skip=m_d_skip,
             ssm_norm_w=m_ssm_norm_w, w_out_ssm=m_w_out_ssm, ln_g=m_ln_g, ln_b=m_ln_b)
    v = dict(w_in_attn=v_w_in_attn, w_out_attn=v_w_out_attn, rel_bias=v_rel_bias, w_in_ssm=v_w_in_ssm,
             conv_w=v_conv_w, conv_b=v_conv_b, dt_bias=v_dt_bias, a_log=v_a_log, d_skip=v_d_skip,
             ssm_norm_w=v_ssm_norm_w, w_out_ssm=v_w_out_ssm, ln_g=v_ln_g, ln_b=v_ln_b)
    me = _lin((lax.axis_index("x"), lax.axis_index("y"), lax.axis_index("c")))
    d = x.shape[2]
    big = ("w_in_attn", "w_out_attn", "w_in_ssm", "w_out_ssm")

    shards = {k: _cast_bf16(w[k], name=f"cast_{k}") for k in big}
    (wa,) = _exchange_alone(_Gather([shards["w_in_attn"]]), name="gather_w_in_attn")
    cpd = conv_w.shape[2]
    npd = ssm_norm_w.shape[1]
    small_shapes = [(CONV_WIDTH, cpd), (1, cpd), (1, npd)]
    (small_all,) = _exchange_alone(_Gather([_pack([conv_w[0], conv_b, ssm_norm_w])]), name="gather_small_weights",
                                   in_vmem=True)
    small_parts = [_unpack(small_all[i], small_shapes) for i in range(NDEV)]
    conv_w_full = jnp.concatenate([p[0] for p in small_parts], axis=1)
    conv_b_full = jnp.concatenate([p[1] for p in small_parts], axis=1)
    norm_w_full = jnp.concatenate([p[2] for p in small_parts], axis=1)

    loss, dx, parts, pending, small = _local_step(
        x[0], loss_target[0], wa, shards["w_out_attn"], shards["w_in_ssm"], shards["w_out_ssm"], rel_bias,
        conv_w_full, conv_b_full, dt_bias[0:1], a_log[0:1], d_skip[0:1], norm_w_full, ln_g, ln_b, dist=True)
    loss = lax.psum(loss, MESH_AXES)
    out = {}
    out["w_in_ssm"], late = _adamw_sum(parts["w_in_ssm"], w["w_in_ssm"], m["w_in_ssm"], v["w_in_ssm"],
                                       name="adamw_w_in_ssm", ride=pending)
    parts["w_in_attn"][1] += list(late)
    for k in ("w_out_ssm", "w_out_attn", "w_in_attn"):
        out[k] = _adamw_sum(parts[k], w[k], m[k], v[k], name=f"adamw_{k}")

    order = REPLICATED + SHARDED_SMALL
    g_shapes = [small[k].shape for k in order]
    (g_all,) = _exchange_alone(_Gather([_pack([small[k] for k in order])]), name="gather_small_grads", in_vmem=True)
    g_sum = dict(zip(order, _unpack(_sum_slots(g_all, name="sum_small_grads"), g_shapes)))
    g_mine = {k: g_sum[k] for k in REPLICATED}
    g_mine["conv_w"] = lax.dynamic_slice_in_dim(g_sum["conv_w"], me * cpd, cpd, axis=1)
    g_mine["conv_b"] = lax.dynamic_slice_in_dim(g_sum["conv_b"], me * cpd, cpd, axis=1)
    g_mine["ssm_norm_w"] = lax.dynamic_slice_in_dim(g_sum["ssm_norm_w"], me * npd, npd, axis=1)
    w_shapes = [w[k].shape for k in order]
    g_pack = _pack([g_mine[k] for k in order])
    d_p, m_p, v_p = _adamw_small(g_pack, _pack([w[k] for k in order]), _pack([m[k] for k in order]),
                                 _pack([v[k] for k in order]), name="adamw_small")
    for k, gk, dk, mk, vk in zip(order, _unpack(g_pack, w_shapes), _unpack(d_p, w_shapes), _unpack(m_p, w_shapes),
                                 _unpack(v_p, w_shapes)):
        out[k] = (gk, dk, mk, vk)

    names = ("w_in_attn", "w_out_attn", "rel_bias", "w_in_ssm", "conv_w", "conv_b", "dt_bias", "a_log", "d_skip",
             "ssm_norm_w", "w_out_ssm", "ln_g", "ln_b")
    res = [loss, dx[None]]
    for i in range(4):
        res += [out[k][i] for k in names]
    return tuple(res)
```

```python
import functools
import math

import jax
import jax.numpy as jnp
from jax import lax
from jax.experimental import pallas as pl
from jax.experimental.pallas import tpu as pltpu

F32 = jnp.float32
BF16 = jnp.bfloat16
MESH_AXES = ("x", "y", "c")
NDEV = 8
NPEER = NDEV - 1
LANES = 128
SUBLANES = 8
VMEM_LIMIT = 52 * 1024 * 1024
MM_VMEM_BUDGET = 40 * 1024 * 1024
MM_TK_MAX = 4096
MM_TN_MAX = 1024

ATTN_PATTERNS = ((128, 1), (512, 4), (2048, 16))
N_GROUPS_ATTN = 3
HEAD_DIM = 128
ATTN_BLOCK = 128
ATTN_UNROLL = 2
ATTN_ROWS_TIMES_HEADS = 8192
NUM_BUCKETS = 32
MAX_DISTANCE = 2048
SSM_GROUPS = 8
CONV_WIDTH = 4
CHUNK = 128
DEPTH = 2
DEEPNORM_ALPHA = (2 * DEPTH) ** 0.25
LN_EPS = 1e-5
RMS_EPS = 1e-5
NEG_INF = -1e30
ADAM_LR = 0.001
ADAM_B1 = 0.9
ADAM_B2 = 0.999
ADAM_EPS = 1e-08
ADAM_WD = 0.01
ADAM_STEP = 10
HIGHEST = lax.Precision.HIGHEST


def _params(*sem):
    return pltpu.CompilerParams(dimension_semantics=sem, vmem_limit_bytes=VMEM_LIMIT)


def _pick(n, prefs):
    for p in prefs:
        if n % p == 0:
            return p
    return n


def _dot(a, b):
    return jnp.dot(a, b, preferred_element_type=F32)


def _dot_nt(a, b):
    return lax.dot_general(a, b, (((1,), (1,)), ((), ())), preferred_element_type=F32)


def _dot_tn(a, b):
    return lax.dot_general(a, b, (((0,), (0,)), ((), ())), preferred_element_type=F32)


def _sigmoid(x):
    return 1.0 / (1.0 + jnp.exp(-x))


def _mm(a, b, *, name, out_dtype, trans_b=False, slab_out=0, n_off=0, n_out=None,
        res=None, res_scale=1.0, ride=None):
    m, k = a.shape
    slab_b = b.ndim == 3
    if slab_b:
        ns = b.shape[0]
        if trans_b:
            n, kper = b.shape[1], b.shape[2]
            assert ns * kper == k
        else:
            nper = b.shape[2]
            n = ns * nper
            assert b.shape[1] == k
    else:
        n = b.shape[0] if trans_b else b.shape[1]
        assert (b.shape[1] if trans_b else b.shape[0]) == k
    n_out = n if n_out is None else n_out
    tm = _pick(m, (1024, 640, 512, 256, 128))
    nconstraint = math.gcd(n_out, n_off) if n_off else n_out
    if slab_b and not trans_b:
        nconstraint = math.gcd(nconstraint, nper)
    if slab_out:
        nconstraint = math.gcd(nconstraint, n_out // slab_out)
    kconstraint = kper if (slab_b and trans_b) else k
    tk = max(t for t in range(LANES, min(kconstraint, MM_TK_MAX) + 1, LANES) if kconstraint % t == 0)
    nk = k // tk
    out_bytes = jnp.dtype(out_dtype).itemsize

    def vmem_bytes(t):
        return (2 * 2 * tk * (tm + t) + 2 * tm * t * out_bytes + (4 * tm * t if nk > 1 else 0)
                + (2 * 4 * tm * t if res is not None else 0))

    fits = [t for t in range(LANES, min(nconstraint, MM_TN_MAX) + 1, LANES)
            if nconstraint % t == 0 and vmem_bytes(t) <= MM_VMEM_BUDGET]
    tn = max(fits)
    nb0 = n_off // tn
    grid = (m // tm, n_out // tn, nk)

    a_spec = pl.BlockSpec((tm, tk), lambda i, j, kk: (i, kk))
    if slab_b and not trans_b:
        nps = nper // tn
        b_spec = pl.BlockSpec((None, tk, tn), lambda i, j, kk: ((j + nb0) // nps, kk, (j + nb0) % nps))
    elif slab_b and trans_b:
        kps = kper // tk
        b_spec = pl.BlockSpec((None, tn, tk), lambda i, j, kk: (kk // kps, j + nb0, kk % kps))
    elif trans_b:
        b_spec = pl.BlockSpec((tn, tk), lambda i, j, kk: (j + nb0, kk))
    else:
        b_spec = pl.BlockSpec((tk, tn), lambda i, j, kk: (kk, j + nb0))
    if slab_out:
        ops = (n_out // slab_out) // tn
        o_spec = pl.BlockSpec((None, tm, tn), lambda i, j, kk: (j // ops, i, j % ops))
        o_shape = jax.ShapeDtypeStruct((slab_out, m, n_out // slab_out), out_dtype)
    else:
        o_spec = pl.BlockSpec((tm, tn), lambda i, j, kk: (i, j))
        o_shape = jax.ShapeDtypeStruct((m, n_out), out_dtype)
    in_specs = [a_spec, b_spec]
    args = [a, b]
    if res is not None:
        in_specs.append(pl.BlockSpec((tm, tn), lambda i, j, kk: (i, j)))
        args.append(res)

    def body(*refs):
        a_ref, b_ref = refs[0], refs[1]
        r_ref = refs[2] if res is not None else None
        o_ref = refs[3] if res is not None else refs[2]
        av = a_ref[...].astype(BF16)
        bv = b_ref[...].astype(BF16)
        part = _dot_nt(av, bv) if trans_b else _dot(av, bv)

        def finish(r):
            if res is not None:
                r = r + res_scale * r_ref[...]
            o_ref[...] = r.astype(out_dtype)

        if nk == 1:
            finish(part)
            return
        acc = refs[-1]
        kk = pl.program_id(2)

        @pl.when(kk == 0)
        def _():
            acc[...] = part

        @pl.when(kk > 0)
        def _():
            acc[...] += part

        @pl.when(kk == nk - 1)
        def _():
            finish(acc[...])

    outs, rode = _call(
        body, name=name, grid=grid, in_specs=in_specs, out_specs=[o_spec], out_shape=[o_shape], args=args,
        scratch=[pltpu.VMEM((tm, tn), F32)] if nk > 1 else [], ride=ride,
        sem=("parallel", "parallel", "arbitrary"))
    return (outs[0], rode) if ride is not None else outs[0]


def _cast_bf16(w, *, name):
    r, c = w.shape[-2:]
    tr = _pick(r, (512, 256, 128, 64, 32, 16, 8))

    def body(w_ref, o_ref):
        o_ref[...] = w_ref[...].astype(BF16)

    in_spec = (pl.BlockSpec((None, tr, c), lambda i: (0, i, 0)) if w.ndim == 3
               else pl.BlockSpec((tr, c), lambda i: (i, 0)))
    return pl.pallas_call(
        body, name=name, grid=(r // tr,),
        in_specs=[in_spec],
        out_specs=pl.BlockSpec((tr, c), lambda i: (i, 0)),
        out_shape=jax.ShapeDtypeStruct((r, c), BF16),
        compiler_params=_params("parallel"),
    )(w)


def _adam_math(w, g, m, v):
    m2 = ADAM_B1 * m + (1.0 - ADAM_B1) * g
    v2 = ADAM_B2 * v + (1.0 - ADAM_B2) * (g * g)
    m_hat = m2 / (1.0 - ADAM_B1 ** ADAM_STEP)
    v_hat = v2 / (1.0 - ADAM_B2 ** ADAM_STEP)
    delta = -ADAM_LR * (m_hat / (jnp.sqrt(v_hat) + ADAM_EPS) + ADAM_WD * w)
    return delta, m2, v2


def _adamw_sum(bands, w, m, v, *, name, ride=None):
    _, r, c = w.shape
    nband = len(bands)
    rows = r // nband
    tr = _pick(rows, (128, 64, 32, 16, 8))
    tc = c if (c % LANES or c <= 2560) else _pick(c, (2048, 1024, 512, 256, 128))
    nt = rows // tr
    flat = [p for band in bands for p in band]

    def body(*refs):
        p_refs = refs[:len(flat)]
        w_ref, m_ref, v_ref, g_out, d_out, m_out, v_out = refs[len(flat):]
        i = pl.program_id(0)
        g, at = None, 0
        for q, band in enumerate(bands):
            gq = None
            for p_ref in p_refs[at:at + len(band)]:
                for s in range(p_ref.shape[0]):
                    t = p_ref[s].astype(F32)
                    gq = t if gq is None else gq + t
            at += len(band)
            g = gq if q == 0 else jnp.where(i >= q * nt, gq, g)
        d, m2, v2 = _adam_math(w_ref[...], g, m_ref[...], v_ref[...])
        g_out[...] = g
        d_out[...] = d
        m_out[...] = m2
        v_out[...] = v2

    def band_spec(p, q):
        return pl.BlockSpec((p.shape[0], tr, tc), lambda i, j: (0, jnp.clip(i - q * nt, 0, nt - 1), j))

    spec = pl.BlockSpec((None, tr, tc), lambda i, j: (0, i, j))
    shp = jax.ShapeDtypeStruct((1, r, c), F32)
    outs, rode = _call(
        body, name=name, grid=(r // tr, c // tc),
        in_specs=[band_spec(p, q) for q, band in enumerate(bands) for p in band] + [spec, spec, spec],
        out_specs=[spec, spec, spec, spec], out_shape=[shp, shp, shp, shp], args=(*flat, w, m, v),
        sem=("parallel", "parallel"), ride=ride)
    return (outs, rode) if ride is not None else outs


def _adamw_small(g, w, m, v, *, name):
    shp = jax.ShapeDtypeStruct(w.shape, F32)

    def body(g_ref, w_ref, m_ref, v_ref, d_out, m_out, v_out):
        d, m2, v2 = _adam_math(w_ref[...], g_ref[...], m_ref[...], v_ref[...])
        d_out[...] = d
        m_out[...] = m2
        v_out[...] = v2

    return pl.pallas_call(body, name=name, out_shape=(shp, shp, shp),
                          compiler_params=pltpu.CompilerParams(vmem_limit_bytes=VMEM_LIMIT))(g, w, m, v)


def _sum_slots(parts, *, name):
    _, r, c = parts.shape

    def body(p_ref, o_ref):
        g = p_ref[0]
        for s in range(1, NDEV):
            g = g + p_ref[s]
        o_ref[...] = g

    return pl.pallas_call(body, name=name, out_shape=jax.ShapeDtypeStruct((r, c), F32),
                          compiler_params=pltpu.CompilerParams(vmem_limit_bytes=VMEM_LIMIT))(parts)


def _ln_parts(u):
    mu = jnp.mean(u, axis=-1, keepdims=True)
    xc = u - mu
    var = jnp.mean(xc * xc, axis=-1, keepdims=True)
    rstd = lax.rsqrt(var + LN_EPS)
    return xc * rstd, rstd


def _ln_fwd(xin, h, g, b, *, name):
    s, d = xin.shape
    tm = _pick(s, (128,))

    def body(x_ref, h_ref, g_ref, b_ref, o_ref, ob_ref):
        xhat, _ = _ln_parts(DEEPNORM_ALPHA * x_ref[...] + h_ref[...])
        o = xhat * g_ref[...] + b_ref[...]
        o_ref[...] = o
        ob_ref[...] = o.astype(BF16)

    row = pl.BlockSpec((tm, d), lambda i: (i, 0))
    vec = pl.BlockSpec((1, d), lambda i: (0, 0))
    return pl.pallas_call(
        body, name=name, grid=(s // tm,), in_specs=[row, row, vec, vec], out_specs=(row, row),
        out_shape=(jax.ShapeDtypeStruct((s, d), F32), jax.ShapeDtypeStruct((s, d), BF16)),
        compiler_params=_params("parallel"),
    )(xin, h, g, b)


def _ln_bwd(xin, h, g, b, cot, *, with_loss, name):
    s, d = xin.shape
    tm = _pick(s, (128,))

    def body(x_ref, h_ref, g_ref, b_ref, c_ref, du_ref, dub_ref, dg_ref, db_ref, *rest):
        i = pl.program_id(0)
        xhat, rstd = _ln_parts(DEEPNORM_ALPHA * x_ref[...] + h_ref[...])
        gv = g_ref[...]
        if with_loss:
            diff = xhat * gv + b_ref[...] - c_ref[...]
            part = 0.5 * jnp.sum(jnp.mean(diff * diff, axis=-1, keepdims=True), axis=0, keepdims=True)
            dout = diff / d
        else:
            dout = c_ref[...]

        @pl.when(i == 0)
        def _():
            dg_ref[...] = jnp.zeros_like(dg_ref)
            db_ref[...] = jnp.zeros_like(db_ref)
            if with_loss:
                rest[0][...] = jnp.zeros_like(rest[0])

        dg_ref[...] += jnp.sum(dout * xhat, axis=0, keepdims=True)
        db_ref[...] += jnp.sum(dout, axis=0, keepdims=True)
        if with_loss:
            rest[0][...] += jnp.broadcast_to(part, rest[0].shape)
        dxh = dout * gv
        du = rstd * (dxh - jnp.mean(dxh, axis=-1, keepdims=True)
                     - xhat * jnp.mean(dxh * xhat, axis=-1, keepdims=True))
        du_ref[...] = du
        dub_ref[...] = du.astype(BF16)

    row = pl.BlockSpec((tm, d), lambda i: (i, 0))
    vec = pl.BlockSpec((1, d), lambda i: (0, 0))
    out_specs = [row, row, vec, vec]
    out_shape = [jax.ShapeDtypeStruct((s, d), F32), jax.ShapeDtypeStruct((s, d), BF16),
                 jax.ShapeDtypeStruct((1, d), F32), jax.ShapeDtypeStruct((1, d), F32)]
    if with_loss:
        out_specs.append(pl.BlockSpec((SUBLANES, LANES), lambda i: (0, 0)))
        out_shape.append(jax.ShapeDtypeStruct((SUBLANES, LANES), F32))
    return pl.pallas_call(
        body, name=name, grid=(s // tm,), in_specs=[row, row, vec, vec, row],
        out_specs=tuple(out_specs), out_shape=tuple(out_shape),
        compiler_params=_params("arbitrary"),
    )(xin, h, g, b, cot)


def t5_causal_bucket(dist):
    max_exact = NUM_BUCKETS // 2
    d_f = jnp.maximum(dist, 1).astype(jnp.float32)
    large = max_exact + (jnp.log(d_f / max_exact) / math.log(MAX_DISTANCE / max_exact)
                         * (NUM_BUCKETS - max_exact)).astype(jnp.int32)
    large = jnp.minimum(large, NUM_BUCKETS - 1)
    return jnp.where(dist < max_exact, dist, large)


def _bias_tables(rel_bias, heads):
    qi = lax.broadcasted_iota(jnp.int32, (ATTN_BLOCK, 2 * ATTN_BLOCK), 0)
    ki = lax.broadcasted_iota(jnp.int32, (ATTN_BLOCK, 2 * ATTN_BLOCK), 1)
    delta = ATTN_BLOCK + qi - ki
    buckets = []
    for window, dilation in ATTN_PATTERNS:
        span = window // dilation
        assert span == ATTN_BLOCK
        band = (delta >= 0) & (delta <= span)
        buckets.append(jnp.where(band, t5_causal_bucket(jnp.clip(delta, 0, None) * dilation), -1))
    bucket = jnp.stack(buckets).astype(jnp.int32)

    def body(bk_ref, tbl_ref, o_ref):
        col = pl.program_id(0) * heads + pl.program_id(1)
        bk = bk_ref[...]
        acc = jnp.full(bk.shape, NEG_INF, F32)
        for b in range(NUM_BUCKETS):
            acc = jnp.where(bk == b, tbl_ref[b, col], acc)
        o_ref[...] = acc

    tile = (None, ATTN_BLOCK, 2 * ATTN_BLOCK)
    bias = pl.pallas_call(
        body, name="bias_fwd", grid=(N_GROUPS_ATTN, heads),
        in_specs=[pl.BlockSpec(tile, lambda g, h: (g, 0, 0)), pl.BlockSpec(memory_space=pltpu.SMEM)],
        out_specs=pl.BlockSpec((None,) + tile, lambda g, h: (g, h, 0, 0)),
        out_shape=jax.ShapeDtypeStruct((N_GROUPS_ATTN, heads, ATTN_BLOCK, 2 * ATTN_BLOCK), F32),
        compiler_params=_params("parallel", "parallel"),
    )(bucket, rel_bias)
    return bias, bucket


def _dilated_view(qkv, g, dilation, da):
    del g
    return qkv.reshape(qkv.shape[0] // dilation, dilation * 3 * da), 3 * (da // HEAD_DIM), 0


def _heads_per_step(l, heads):
    for hps in (4, 2, 1):
        if heads % hps == 0 and l * hps <= ATTN_ROWS_TIMES_HEADS:
            return hps
    return 1


def _attn_fwd_group(qkv, bias_g, g, dilation, da, ride=None):
    s = qkv.shape[0]
    heads = da // HEAD_DIM
    l = s // dilation
    nb = l // ATTN_BLOCK
    view, cpb, base = _dilated_view(qkv, g, dilation, da)

    hps = _heads_per_step(l, heads)
    lanes = [slice(i * HEAD_DIM, (i + 1) * HEAD_DIM) for i in range(hps)]

    def body(q_ref, k_ref, v_ref, b_ref, o_ref, l_ref):
        scale = HEAD_DIM ** -0.5

        def block(rows, keys, first):
            q, k, v = q_ref[rows, :], k_ref[keys, :], v_ref[keys, :]
            bias = [b_ref[i, :, ATTN_BLOCK:2 * ATTN_BLOCK] if first else b_ref[i] for i in range(hps)]
            sc = [_dot_nt(q[:, hl], k[:, hl]) * scale + bias[i] for i, hl in enumerate(lanes)]
            mx = [jnp.max(t, axis=-1, keepdims=True) for t in sc]
            p = [jnp.exp(t - m) for t, m in zip(sc, mx)]
            den = [jnp.sum(t, axis=-1, keepdims=True) for t in p]
            for i, hl in enumerate(lanes):
                o_ref[rows, hl] = _dot((p[i] * (1.0 / den[i])).astype(BF16), v[:, hl])
                l_ref[rows, hl] = jnp.broadcast_to(mx[i] + jnp.log(den[i]), (ATTN_BLOCK, HEAD_DIM))

        first = pl.ds(0, ATTN_BLOCK)
        block(first, first, True)

        def step(j, carry):
            r0 = pl.multiple_of(j * ATTN_BLOCK, ATTN_BLOCK)
            rk = pl.multiple_of((j - 1) * ATTN_BLOCK, ATTN_BLOCK)
            block(pl.ds(r0, ATTN_BLOCK), pl.ds(rk, 2 * ATTN_BLOCK), False)
            return carry

        if nb > 1:
            lax.fori_loop(1, nb, step, 0)

    def col(t):
        return lambda r, h: (0, (r * cpb + base + t * heads) // hps + h)

    blk = (l, hps * HEAD_DIM)
    out = pl.BlockSpec(blk, lambda r, h: (0, r * (heads // hps) + h))
    shp = jax.ShapeDtypeStruct((l, dilation * da), F32)
    (o, lse), rode = _call(
        body, name=f"attn_fwd_g{g}", grid=(dilation, heads // hps),
        in_specs=[pl.BlockSpec(blk, col(0)), pl.BlockSpec(blk, col(1)), pl.BlockSpec(blk, col(2)),
                  pl.BlockSpec((hps, ATTN_BLOCK, 2 * ATTN_BLOCK), lambda r, h: (h, 0, 0))],
        out_specs=[out, out], out_shape=[shp, shp], args=(view, view, view, bias_g),
        sem=("parallel", "parallel"), ride=ride)
    return o.reshape(s, da), lse.reshape(s, da), rode


def _attn_combine(os_, ls_, gate):
    s, da = gate.shape
    tm = _pick(s, (512, 256, 128))
    tc = _pick(da, (512, 256, 128))

    assert da // HEAD_DIM <= LANES
    per_step = tc // HEAD_DIM

    def body(o0, o1, o2, l0, l1, l2, g_ref, o_ref, l_ref, y_ref):
        j = pl.program_id(1)
        a0, a1, a2 = l0[...], l1[...], l2[...]
        mx = jnp.maximum(jnp.maximum(a0, a1), a2)
        e0, e1, e2 = jnp.exp(a0 - mx), jnp.exp(a1 - mx), jnp.exp(a2 - mx)
        den = e0 + e1 + e2
        o = (e0 * o0[...] + e1 * o1[...] + e2 * o2[...]) / den
        gv = g_ref[...]
        o_ref[...] = o
        y_ref[...] = (o * (gv * _sigmoid(gv))).astype(BF16)
        lse = mx + jnp.log(den)

        @pl.when(j == 0)
        def _():
            l_ref[...] = jnp.zeros_like(l_ref)

        lane = lax.broadcasted_iota(jnp.int32, (1, LANES), 1)
        acc = l_ref[...]
        for i in range(per_step):
            acc = jnp.where(lane == j * per_step + i, lse[:, i * HEAD_DIM:(i + 1) * HEAD_DIM], acc)
        l_ref[...] = acc

    spec = pl.BlockSpec((tm, tc), lambda i, j: (i, j))
    heads_spec = pl.BlockSpec((tm, LANES), lambda i, j: (i, 0))
    return pl.pallas_call(
        body, name="attn_combine", grid=(s // tm, da // tc), in_specs=[spec] * 7, out_specs=(spec, heads_spec, spec),
        out_shape=(jax.ShapeDtypeStruct((s, da), F32), jax.ShapeDtypeStruct((s, LANES), F32),
                   jax.ShapeDtypeStruct((s, da), BF16)),
        compiler_params=_params("parallel", "arbitrary"),
    )(*os_, *ls_, gate)


def _attn_bwd_prep(dy, o, gate):
    s, da = gate.shape
    tm = _pick(s, (512, 256, 128))

    def body(dy_ref, o_ref, g_ref, do_ref, dg_ref, dd_ref):
        j = pl.program_id(1)
        gv = g_ref[...]
        sg = _sigmoid(gv)
        dyv = dy_ref[...]
        ov = o_ref[...]
        do = dyv * (gv * sg)
        do_ref[...] = do.astype(BF16)
        dg_ref[...] = (dyv * ov * (sg * (1.0 + gv * (1.0 - sg)))).astype(BF16)

        @pl.when(j == 0)
        def _():
            dd_ref[...] = jnp.zeros_like(dd_ref)

        lane = lax.broadcasted_iota(jnp.int32, (1, LANES), 1)
        dd_ref[...] = jnp.where(lane == j, jnp.sum(do * ov, axis=-1, keepdims=True), dd_ref[...])

    spec = pl.BlockSpec((tm, HEAD_DIM), lambda i, j: (i, j))
    return pl.pallas_call(
        body, name="attn_bwd_prep", grid=(s // tm, da // HEAD_DIM), in_specs=[spec] * 3,
        out_specs=(spec, spec, pl.BlockSpec((tm, LANES), lambda i, j: (i, 0))),
        out_shape=(jax.ShapeDtypeStruct((s, da), BF16), jax.ShapeDtypeStruct((s, da), BF16),
                   jax.ShapeDtypeStruct((s, LANES), F32)),
        compiler_params=_params("parallel", "arbitrary"),
    )(dy, o, gate)


def _attn_bwd_group(qkv, do, lse, dd, bias_g, g, dilation, da, ride=None):
    s = qkv.shape[0]
    heads = da // HEAD_DIM
    l = s // dilation
    nb = l // ATTN_BLOCK
    view, cpb, base = _dilated_view(qkv, g, dilation, da)
    scale = HEAD_DIM ** -0.5
    hps = _heads_per_step(l, heads)
    lanes = [slice(i * HEAD_DIM, (i + 1) * HEAD_DIM) for i in range(hps)]

    def body(q_ref, k_ref, v_ref, do_ref, l_ref, dd_ref, b_ref, dq_ref, dk_ref, dv_ref, ds_ref, dk_acc, dv_acc):
        h0 = pl.program_id(0) * hps
        r = pl.program_id(1)
        lane = lax.broadcasted_iota(jnp.int32, (1, LANES), 1)

        @pl.when(r == 0)
        def _():
            ds_ref[...] = jnp.zeros_like(ds_ref)

        dk_acc[...] = jnp.zeros_like(dk_acc)
        dv_acc[...] = jnp.zeros_like(dv_acc)

        def block(rows, keys, first):
            q, k, v, dov = q_ref[rows, :], k_ref[keys, :], v_ref[keys, :], do_ref[rows, :]
            lse_all, dd_all = l_ref[rows, :], dd_ref[rows, :]
            pick = [(lane == h0 + i).astype(F32) for i in range(hps)]
            lrow = [jnp.sum(lse_all * m, axis=-1, keepdims=True) for m in pick]
            drow = [jnp.sum(dd_all * m, axis=-1, keepdims=True) for m in pick]
            bias = [b_ref[i, :, ATTN_BLOCK:2 * ATTN_BLOCK] if first else b_ref[i] for i in range(hps)]
            sc = [_dot_nt(q[:, hl], k[:, hl]) for hl in lanes]
            dp = [_dot_nt(dov[:, hl], v[:, hl]) for hl in lanes]
            p = [jnp.exp(sc[i] * scale + bias[i] - lrow[i]) for i in range(hps)]
            ds = [p[i] * (dp[i] - drow[i]) for i in range(hps)]
            dsb = [t.astype(BF16) for t in ds]
            pb = [t.astype(BF16) for t in p]
            for i, hl in enumerate(lanes):
                dq_ref[rows, hl] = (_dot(dsb[i], k[:, hl]) * scale).astype(BF16)
                dk_acc[keys, hl] += _dot_tn(dsb[i], q[:, hl]) * scale
                dv_acc[keys, hl] += _dot_tn(pb[i], dov[:, hl])
                if first:
                    ds_ref[i, :, ATTN_BLOCK:2 * ATTN_BLOCK] += ds[i]
                else:
                    ds_ref[i] += ds[i]

        first = pl.ds(0, ATTN_BLOCK)
        block(first, first, True)

        def step(j, carry):
            r0 = pl.multiple_of(j * ATTN_BLOCK, ATTN_BLOCK)
            rk = pl.multiple_of((j - 1) * ATTN_BLOCK, ATTN_BLOCK)
            block(pl.ds(r0, ATTN_BLOCK), pl.ds(rk, 2 * ATTN_BLOCK), False)
            return carry

        if nb > 1:
            lax.fori_loop(1, nb, step, 0)
        dk_ref[...] = dk_acc[...].astype(BF16)
        dv_ref[...] = dv_acc[...].astype(BF16)

    def col(t):
        return lambda h, r: (0, (r * cpb + base + t * heads) // hps + h)

    blk = (l, hps * HEAD_DIM)
    act = pl.BlockSpec(blk, lambda h, r: (0, r * (heads // hps) + h))
    per_head = pl.BlockSpec((l, LANES), lambda h, r: (0, r))
    tile = pl.BlockSpec((hps, ATTN_BLOCK, 2 * ATTN_BLOCK), lambda h, r: (h, 0, 0))
    shp = jax.ShapeDtypeStruct((l, dilation * da), BF16)
    (dq, dk, dv, ds), rode = _call(
        body, name=f"attn_bwd_g{g}", grid=(heads // hps, dilation),
        in_specs=[pl.BlockSpec(blk, col(0)), pl.BlockSpec(blk, col(1)), pl.BlockSpec(blk, col(2)), act,
                  per_head, per_head, tile],
        out_specs=[act, act, act, tile],
        out_shape=[shp, shp, shp, jax.ShapeDtypeStruct((heads, ATTN_BLOCK, 2 * ATTN_BLOCK), F32)],
        scratch=[pltpu.VMEM(blk, F32), pltpu.VMEM(blk, F32)], sem=("parallel", "arbitrary"), ride=ride,
        args=(view, view, view, do.reshape(l, dilation * da), lse.reshape(l, dilation * LANES),
              dd.reshape(l, dilation * LANES), bias_g))
    return (dq.reshape(s, da), dk.reshape(s, da), dv.reshape(s, da), ds), rode


def _bias_bwd(ds, bucket):
    ng, heads = ds.shape[0], ds.shape[1]

    def body(ds_ref, bk_ref, o_ref):
        bk = bk_ref[...]
        x = ds_ref[...]
        for b in range(NUM_BUCKETS):
            o_ref[:, b:b + 1] = jnp.sum(jnp.where(bk == b, x, 0.0), axis=(0, 1), keepdims=True)

    tile = (None, ATTN_BLOCK, 2 * ATTN_BLOCK)
    out = pl.pallas_call(
        body, name="bias_bwd", grid=(ng, heads),
        in_specs=[pl.BlockSpec((None,) + tile, lambda g, h: (g, h, 0, 0)), pl.BlockSpec(tile, lambda g, h: (g, 0, 0))],
        out_specs=pl.BlockSpec((None, None, 1, NUM_BUCKETS), lambda g, h: (g, h, 0, 0)),
        out_shape=jax.ShapeDtypeStruct((ng, heads, 1, NUM_BUCKETS), F32),
        compiler_params=_params("parallel", "parallel"),
    )(ds, bucket)
    return out.reshape(ng, heads, NUM_BUCKETS)


def _shift_rows(x, halo, s):
    r = pltpu.roll(x, s, axis=0)
    rh = pltpu.roll(halo, s, axis=0)
    row = lax.broadcasted_iota(jnp.int32, halo.shape, 0)
    top = jnp.where(row < s, rh, r[0:SUBLANES])
    return jnp.concatenate([top, r[SUBLANES:]], axis=0)


def _conv_out(x, halo, w, b):
    acc = b + w[CONV_WIDTH - 1:CONV_WIDTH] * x
    for kk in range(CONV_WIDTH - 1):
        acc = acc + w[kk:kk + 1] * _shift_rows(x, halo, CONV_WIDTH - 1 - kk)
    return acc


def _conv_fwd(proj, conv_w, conv_b, col0):
    s = proj.shape[0]
    c = conv_w.shape[1]
    ts = _pick(s, (512, 256, 128))
    tc = _pick(math.gcd(c, col0), (512, 256, 128))
    cb0 = col0 // tc
    hb = ts // SUBLANES

    def body(x_ref, h_ref, w_ref, b_ref, o_ref):
        i = pl.program_id(0)
        halo = jnp.where(i > 0, h_ref[...], 0.0)
        u = _conv_out(x_ref[...], halo, w_ref[...], b_ref[...])
        o_ref[...] = u * _sigmoid(u)

    return pl.pallas_call(
        body, name="conv_fwd", grid=(s // ts, c // tc),
        in_specs=[pl.BlockSpec((ts, tc), lambda i, j: (i, cb0 + j)),
                  pl.BlockSpec((SUBLANES, tc), lambda i, j: (jnp.maximum(i * hb - 1, 0), cb0 + j)),
                  pl.BlockSpec((CONV_WIDTH, tc), lambda i, j: (0, j)),
                  pl.BlockSpec((1, tc), lambda i, j: (0, j))],
        out_specs=pl.BlockSpec((ts, tc), lambda i, j: (i, j)),
        out_shape=jax.ShapeDtypeStruct((s, c), F32),
        compiler_params=_params("parallel", "parallel"),
    )(proj, proj, conv_w, conv_b)


def _conv_bwd(proj, conv_w, conv_b, dacts, col0, dproj):
    s = proj.shape[0]
    c = conv_w.shape[1]
    widths = [d.shape[1] for d in dacts]
    assert sum(widths) == c
    ts = _pick(s, (512, 256, 128))
    tc = _pick(math.gcd(math.gcd(c, col0), math.gcd(*widths)), (512, 256, 128))
    cb0 = col0 // tc
    hb = ts // SUBLANES
    nblk = s // ts
    ext = ts + SUBLANES
    nb = [wd // tc for wd in widths]
    starts = [0, nb[0], nb[0] + nb[1]]

    def body(x_ref, xp_ref, xn_ref, d0, d1, d2, n0, n1, n2, w_ref, b_ref, _, dx_ref, dw_ref, db_ref):
        j = pl.program_id(0)
        i = pl.program_id(1)
        last = i == nblk - 1
        w = w_ref[...]
        halo = jnp.where(i > 0, xp_ref[...], 0.0)
        x = x_ref[...]
        xe = jnp.concatenate([x, xn_ref[...]], axis=0)
        dcur = jnp.where(j < starts[1], d0[...], jnp.where(j < starts[2], d1[...], d2[...]))
        dnext = jnp.where(j < starts[1], n0[...], jnp.where(j < starts[2], n1[...], n2[...]))
        de = jnp.concatenate([dcur, jnp.where(last, 0.0, dnext)], axis=0)
        u = _conv_out(xe, halo, w, b_ref[...])
        sg = _sigmoid(u)
        dpre = de * (sg * (1.0 + u * (1.0 - sg)))
        dx = w[CONV_WIDTH - 1:CONV_WIDTH] * dpre[0:ts]
        for kk in range(CONV_WIDTH - 1):
            sh = CONV_WIDTH - 1 - kk
            dx = dx + w[kk:kk + 1] * pltpu.roll(dpre, ext - sh, axis=0)[0:ts]
        dx_ref[...] = dx.astype(BF16)
        dcur = dpre[0:ts]

        @pl.when(i == 0)
        def _():
            dw_ref[...] = jnp.zeros_like(dw_ref)
            db_ref[...] = jnp.zeros_like(db_ref)

        db_ref[...] += jnp.sum(dcur, axis=0, keepdims=True)
        dw_ref[CONV_WIDTH - 1:CONV_WIDTH, :] += jnp.sum(dcur * x, axis=0, keepdims=True)
        for kk in range(CONV_WIDTH - 1):
            xs = _shift_rows(x, halo, CONV_WIDTH - 1 - kk)
            dw_ref[kk:kk + 1, :] += jnp.sum(dcur * xs, axis=0, keepdims=True)

    cur_p = pl.BlockSpec((ts, tc), lambda j, i: (i, cb0 + j))
    prev_p = pl.BlockSpec((SUBLANES, tc), lambda j, i: (jnp.maximum(i * hb - 1, 0), cb0 + j))
    nxt = lambda i: jnp.minimum((i + 1) * hb, nblk * hb - 1)
    next_p = pl.BlockSpec((SUBLANES, tc), lambda j, i: (nxt(i), cb0 + j))

    def part(q):
        return lambda j: jnp.clip(j - starts[q], 0, nb[q] - 1)

    cur_d = [pl.BlockSpec((ts, tc), lambda j, i, f=part(q): (i, f(j))) for q in range(3)]
    next_d = [pl.BlockSpec((SUBLANES, tc), lambda j, i, f=part(q): (nxt(i), f(j))) for q in range(3)]
    vec4 = pl.BlockSpec((CONV_WIDTH, tc), lambda j, i: (0, j))
    vec1 = pl.BlockSpec((1, tc), lambda j, i: (0, j))
    return pl.pallas_call(
        body, name="conv_bwd", grid=(c // tc, nblk),
        in_specs=[cur_p, prev_p, next_p, *cur_d, *next_d, vec4, vec1, pl.BlockSpec(memory_space=pl.ANY)],
        out_specs=(cur_p, vec4, vec1),
        out_shape=(jax.ShapeDtypeStruct(dproj.shape, dproj.dtype), jax.ShapeDtypeStruct((CONV_WIDTH, c), F32),
                   jax.ShapeDtypeStruct((1, c), F32)),
        input_output_aliases={11: 0},
        compiler_params=_params("parallel", "arbitrary"),
    )(proj, proj, proj, *dacts, *dacts, conv_w, conv_b, dproj)


def _dt_fwd(proj, dt_bias, col0):
    s = proj.shape[0]
    h = dt_bias.shape[1]
    ts = _pick(s, (1024, 512, 256, 128))

    def body(x_ref, b_ref, o_ref):
        v = x_ref[...] + b_ref[...]
        o_ref[...] = jnp.maximum(v, 0.0) + jnp.log1p(jnp.exp(-jnp.abs(v)))

    return pl.pallas_call(
        body, name="dt_fwd", grid=(s // ts,),
        in_specs=[pl.BlockSpec((ts, h), lambda i: (i, col0 // h)), pl.BlockSpec((1, h), lambda i: (0, 0))],
        out_specs=pl.BlockSpec((ts, h), lambda i: (i, 0)), out_shape=jax.ShapeDtypeStruct((s, h), F32),
        compiler_params=_params("parallel"),
    )(proj, dt_bias)


def _dt_bwd(proj, dt_bias, ddt, col0, dproj):
    s = proj.shape[0]
    h = dt_bias.shape[1]
    ts = _pick(s, (1024, 512, 256, 128))

    def body(x_ref, b_ref, d_ref, _, o_ref, db_ref):
        i = pl.program_id(0)
        draw = d_ref[...] * _sigmoid(x_ref[...] + b_ref[...])
        o_ref[...] = draw.astype(BF16)

        @pl.when(i == 0)
        def _():
            db_ref[...] = jnp.zeros_like(db_ref)

        db_ref[...] += jnp.sum(draw, axis=0, keepdims=True)

    return pl.pallas_call(
        body, name="dt_bwd", grid=(s // ts,),
        in_specs=[pl.BlockSpec((ts, h), lambda i: (i, col0 // h)), pl.BlockSpec((1, h), lambda i: (0, 0)),
                  pl.BlockSpec((ts, h), lambda i: (i, 0)), pl.BlockSpec(memory_space=pl.ANY)],
        out_specs=(pl.BlockSpec((ts, h), lambda i: (i, col0 // h)), pl.BlockSpec((1, h), lambda i: (0, 0))),
        out_shape=(jax.ShapeDtypeStruct(dproj.shape, dproj.dtype), jax.ShapeDtypeStruct((1, h), F32)),
        input_output_aliases={3: 0},
        compiler_params=_params("arbitrary"),
    )(proj, dt_bias, ddt, dproj)


def _chunk_terms(dt, dt_t, a, a_t):
    li = lax.broadcasted_iota(jnp.int32, (CHUNK, CHUNK), 0)
    si = lax.broadcasted_iota(jnp.int32, (CHUNK, CHUNK), 1)
    lower = (li >= si).astype(F32)
    upper = (li <= si).astype(F32)
    acum = jnp.dot(lower, dt * a, preferred_element_type=F32, precision=HIGHEST)
    acum_t = jnp.dot(dt_t * a_t, upper, preferred_element_type=F32, precision=HIGHEST)
    return acum, acum_t, li, si, upper


def _dot_exact01(t, m01):
    r = t.shape[0]
    hi = t.astype(BF16)
    rest = t - hi.astype(F32)
    mid = rest.astype(BF16)
    lo = (rest - mid.astype(F32)).astype(BF16)
    out = _dot(jnp.concatenate([hi, mid, lo], axis=0), m01.astype(BF16))
    return out[0:r] + out[r:2 * r] + out[2 * r:3 * r]


def _head_lanes(dt, acum, gw):
    hpg = dt.shape[1]
    p = gw // hpg
    spread = (lax.broadcasted_iota(jnp.int32, (hpg, gw), 1) // p
              == lax.broadcasted_iota(jnp.int32, (hpg, gw), 0)).astype(F32)
    both = _dot_exact01(jnp.concatenate([dt, acum], axis=0), spread)
    dt_e, acum_e = both[0:CHUNK], both[CHUNK:2 * CHUNK]
    alast_e = acum_e[CHUNK - 1:CHUNK, :]
    return dt_e, jnp.exp(acum_e), jnp.exp(alast_e - acum_e), jnp.exp(alast_e)


def _fold_heads(t, hpg):
    gw = t.shape[1]
    p = gw // hpg
    fold = (lax.broadcasted_iota(jnp.int32, (gw, hpg), 0) // p
            == lax.broadcasted_iota(jnp.int32, (gw, hpg), 1)).astype(F32)
    return _dot_exact01(t, fold)


def _ssd_fwd(xbc, proj, dt_g, dt_gt, a_g, a_gt, dskip_e, norm_w, d_inner, n_state):
    s = xbc.shape[0]
    hpg = dt_g.shape[2]
    gw = d_inner // SSM_GROUPS
    p = gw // hpg
    nc = s // CHUNK
    n = n_state
    b0 = d_inner // n
    c0 = b0 + SSM_GROUPS
    per_tile = LANES // p

    def body(xs_ref, b_ref, c_ref, dt_ref, dtt_ref, a_ref, at_ref, z_ref, dsk_ref, nw_ref,
             yn_ref, y_ref, st_ref, state):
        c = pl.program_id(1)

        @pl.when(c == 0)
        def _():
            state[...] = jnp.zeros_like(state)

        st = state[...]
        st_ref[...] = st
        xs = xs_ref[...]
        bm = b_ref[...].astype(BF16)
        cm = c_ref[...].astype(BF16)
        dt = dt_ref[...]
        acum, acum_t, li, si, _ = _chunk_terms(dt, dtt_ref[...], a_ref[...], at_ref[...])
        dt_e, e_a, t_e, e_last = _head_lanes(dt, acum, gw)
        xdt = xs * dt_e
        xdtb = xdt.astype(BF16)
        cb = _dot_nt(cm, bm)
        causal = li >= si
        lane = lax.broadcasted_iota(jnp.int32, (1, LANES), 1)
        y_ref[...] = _dot(cm, st.astype(BF16)) * e_a
        for q in range(gw // LANES):
            ql = slice(q * LANES, (q + 1) * LANES)
            xq = xdtb[:, ql]
            ms = []
            for i in range(per_tile):
                h = q * per_tile + i
                decay = jnp.exp(jnp.where(causal, acum[:, h:h + 1] - acum_t[h:h + 1, :], NEG_INF))
                ms.append((cb * decay).astype(BF16))
            y_all = _dot(jnp.concatenate(ms, axis=0), xq)
            yd = y_all[0:CHUNK]
            for i in range(1, per_tile):
                yd = jnp.where(lane >= i * p, y_all[i * CHUNK:(i + 1) * CHUNK], yd)
            y_ref[:, ql] += yd
        state[...] = st * e_last + _dot_tn(bm, (xdt * t_e).astype(BF16))
        yt = y_ref[...] + xs * dsk_ref[...]
        z = z_ref[...]
        yz = yt * (z * _sigmoid(z))
        r = lax.rsqrt(jnp.mean(yz * yz, axis=-1, keepdims=True) + RMS_EPS)
        yn_ref[...] = (yz * r * nw_ref[...]).astype(BF16)

    wide = pl.BlockSpec((CHUNK, gw), lambda g, c: (c, g))
    return pl.pallas_call(
        body, name="ssd_fwd", grid=(SSM_GROUPS, nc),
        in_specs=[wide,
                  pl.BlockSpec((CHUNK, n), lambda g, c: (c, b0 + g)),
                  pl.BlockSpec((CHUNK, n), lambda g, c: (c, c0 + g)),
                  pl.BlockSpec((None, CHUNK, hpg), lambda g, c: (g, c, 0)),
                  pl.BlockSpec((None, hpg, CHUNK), lambda g, c: (g, 0, c)),
                  pl.BlockSpec((None, 1, hpg), lambda g, c: (g, 0, 0)),
                  pl.BlockSpec((None, hpg, 1), lambda g, c: (g, 0, 0)),
                  wide,
                  pl.BlockSpec((None, 1, gw), lambda g, c: (g, 0, 0)),
                  pl.BlockSpec((1, gw), lambda g, c: (0, g))],
        out_specs=(wide, wide, pl.BlockSpec((None, None, n, gw), lambda g, c: (g, c, 0, 0))),
        out_shape=(jax.ShapeDtypeStruct((s, d_inner), BF16), jax.ShapeDtypeStruct((s, d_inner), F32),
                   jax.ShapeDtypeStruct((SSM_GROUPS, nc, n, gw), F32)),
        scratch_shapes=[pltpu.VMEM((n, gw), F32)],
        compiler_params=_params("parallel", "arbitrary"),
    )(xbc, xbc, xbc, dt_g, dt_gt, a_g, a_gt, proj, dskip_e, norm_w)


def _ssd_epilogue_bwd(dyn, y, xbc, proj, dskip_e, norm_w, hpg):
    s, d_inner = dyn.shape
    gw = d_inner // SSM_GROUPS
    p = gw // hpg
    nc = s // CHUNK

    def body(dyn_ref, y_ref, xs_ref, z_ref, dsk_ref, nw_ref, dy_ref, dz_ref, dnw_ref, ddsk_ref):
        c = pl.program_id(1)
        xs = xs_ref[...]
        z = z_ref[...]
        yt = y_ref[...] + xs * dsk_ref[...]
        sg = _sigmoid(z)
        sz = z * sg
        yz = yt * sz
        r = lax.rsqrt(jnp.mean(yz * yz, axis=-1, keepdims=True) + RMS_EPS)
        dynv = dyn_ref[...]
        dyh = dynv * nw_ref[...]
        dyz = r * (dyh - yz * (r * r) * jnp.mean(dyh * yz, axis=-1, keepdims=True))
        dyt = dyz * sz
        dy_ref[...] = dyt
        dz_ref[...] = (dyz * yt * (sg * (1.0 + z * (1.0 - sg)))).astype(BF16)

        @pl.when(c == 0)
        def _():
            dnw_ref[...] = jnp.zeros_like(dnw_ref)
            ddsk_ref[...] = jnp.zeros_like(ddsk_ref)

        dnw_ref[...] += jnp.sum(dynv * yz * r, axis=0, keepdims=True)
        colsum = jnp.sum(dyt * xs, axis=0, keepdims=True)
        fold = (lax.broadcasted_iota(jnp.int32, (gw, hpg), 0) // p
                == lax.broadcasted_iota(jnp.int32, (gw, hpg), 1)).astype(F32)
        ddsk_ref[...] += jnp.dot(colsum, fold, preferred_element_type=F32, precision=HIGHEST)

    wide = pl.BlockSpec((CHUNK, gw), lambda g, c: (c, g))
    return pl.pallas_call(
        body, name="ssd_epilogue_bwd", grid=(SSM_GROUPS, nc),
        in_specs=[wide, wide, wide, wide, pl.BlockSpec((None, 1, gw), lambda g, c: (g, 0, 0)),
                  pl.BlockSpec((1, gw), lambda g, c: (0, g))],
        out_specs=(wide, wide, pl.BlockSpec((1, gw), lambda g, c: (0, g)),
                   pl.BlockSpec((None, 1, hpg), lambda g, c: (g, 0, 0))),
        out_shape=(jax.ShapeDtypeStruct((s, d_inner), F32), jax.ShapeDtypeStruct((s, proj.shape[1]), BF16),
                   jax.ShapeDtypeStruct((1, d_inner), F32), jax.ShapeDtypeStruct((SSM_GROUPS, 1, hpg), F32)),
        compiler_params=_params("parallel", "arbitrary"),
    )(dyn, y, xbc, proj, dskip_e, norm_w)


def _ssd_scan_bwd(xbc, dt_g, dt_gt, a_g, a_gt, states, dy, dskip_e, d_inner, n_state, ride=None):
    s = xbc.shape[0]
    hpg = dt_g.shape[2]
    gw = d_inner // SSM_GROUPS
    p = gw // hpg
    nc = s // CHUNK
    n = n_state
    b0 = d_inner // n
    c0 = b0 + SSM_GROUPS
    per_tile = LANES // p

    def body(xs_ref, b_ref, c_ref, dt_ref, dtt_ref, a_ref, at_ref, st_ref, dy_ref, dsk_ref,
             dxs_ref, db_ref, dc_ref, ddt_ref, da_ref, dstate, ydiag_ref, dxd_ref):
        c = pl.program_id(1)

        @pl.when(c == 0)
        def _():
            dstate[...] = jnp.zeros_like(dstate)
            da_ref[...] = jnp.zeros_like(da_ref)

        xs = xs_ref[...]
        bm = b_ref[...].astype(BF16)
        cm = c_ref[...].astype(BF16)
        dt = dt_ref[...]
        a = a_ref[...]
        dyv = dy_ref[...]
        dsk = dsk_ref[...]
        acum, acum_t, li, si, upper = _chunk_terms(dt, dtt_ref[...], a, at_ref[...])
        dt_e, e_a, t_e, e_last = _head_lanes(dt, acum, gw)
        cb = _dot_nt(cm, bm)
        lower_mask = li >= si
        lane = lax.broadcasted_iota(jnp.int32, (1, LANES), 1)
        row_l = lax.broadcasted_iota(jnp.int32, (CHUNK, 1), 0)
        st = st_ref[...]
        stb = st.astype(BF16)
        dst = dstate[...]
        dstb = dst.astype(BF16)
        xdt = xs * dt_e
        xdtb = xdt.astype(BF16)
        dyb = dyv.astype(BF16)
        dye = dyv * e_a
        dyeb = dye.astype(BF16)
        xte = xdt * t_e
        xteb = xte.astype(BF16)
        wv = _dot(bm, dstb)
        yo = _dot(cm, stb)
        dcb = jnp.zeros((CHUNK, CHUNK), F32)
        for q in range(gw // LANES):
            ql = slice(q * LANES, (q + 1) * LANES)
            xq = xdtb[:, ql]
            dq = dyb[:, ql]
            decays, ms, mts, dqs = [], [], [], []
            for i in range(per_tile):
                h = q * per_tile + i
                decay = jnp.exp(jnp.where(lower_mask, acum[:, h:h + 1] - acum_t[h:h + 1, :], NEG_INF))
                mm = cb * decay
                mine = (lane >= i * p) & (lane < (i + 1) * p)
                decays.append(decay)
                ms.append(mm.astype(BF16))
                mts.append(mm.T.astype(BF16))
                dqs.append(jnp.where(mine, dq, jnp.zeros_like(dq)))
            dm_all = _dot_nt(jnp.concatenate(dqs, axis=0), xq)
            y_all = _dot(jnp.concatenate(ms, axis=0), xq)
            d_all = _dot(jnp.concatenate(mts, axis=0), dq)
            yd = dd = None
            for i in range(per_tile):
                rows = slice(i * CHUNK, (i + 1) * CHUNK)
                dcb = dcb + dm_all[rows] * decays[i]
                yd = y_all[rows] if i == 0 else jnp.where(lane >= i * p, y_all[rows], yd)
                dd = d_all[rows] if i == 0 else jnp.where(lane >= i * p, d_all[rows], dd)
            ydiag_ref[:, ql] = yd
            dxd_ref[:, ql] = dd
        ydiag = ydiag_ref[...]
        dxd = dxd_ref[...]
        dxdt = dxd + t_e * wv
        xw = xte * wv
        last_in = jnp.sum(xw, axis=0, keepdims=True) + e_last * jnp.sum(dst * st, axis=0, keepdims=True)
        folded = _fold_heads(jnp.concatenate(
            [dyb.astype(F32) * ydiag - xdtb.astype(F32) * dxd - xw + dye * yo, dxdt * xs,
             jnp.broadcast_to(last_in, (SUBLANES, gw))],
            axis=0), hpg)
        dalast = folded[2 * CHUNK:2 * CHUNK + 1]
        d_acum = folded[0:CHUNK] + jnp.where(row_l == CHUNK - 1, dalast, 0.0)
        ddt_x = folded[CHUNK:2 * CHUNK]
        dxs_ref[...] = dxdt * dt_e + dyv * dsk
        dbf = dcb.astype(BF16)
        dc_ref[...] = _dot_nt(dyeb, stb) + _dot(dbf, bm)
        db_ref[...] = _dot_nt(xteb, dstb) + _dot_tn(dbf, cm)
        dstate[...] = dst * e_last + _dot_tn(cm, dyeb)
        d_da = jnp.dot(upper, d_acum, preferred_element_type=F32, precision=HIGHEST)
        ddt_ref[...] = d_da * a + ddt_x
        da_ref[...] += jnp.sum(d_da * dt, axis=0, keepdims=True)

    rev = lambda c: nc - 1 - c
    wide = pl.BlockSpec((CHUNK, gw), lambda g, c: (rev(c), g))
    return _call(
        body, name="ssd_scan_bwd", grid=(SSM_GROUPS, nc),
        in_specs=[wide,
                  pl.BlockSpec((CHUNK, n), lambda g, c: (rev(c), b0 + g)),
                  pl.BlockSpec((CHUNK, n), lambda g, c: (rev(c), c0 + g)),
                  pl.BlockSpec((None, CHUNK, hpg), lambda g, c: (g, rev(c), 0)),
                  pl.BlockSpec((None, hpg, CHUNK), lambda g, c: (g, 0, rev(c))),
                  pl.BlockSpec((None, 1, hpg), lambda g, c: (g, 0, 0)),
                  pl.BlockSpec((None, hpg, 1), lambda g, c: (g, 0, 0)),
                  pl.BlockSpec((None, None, n, gw), lambda g, c: (g, rev(c), 0, 0)),
                  wide,
                  pl.BlockSpec((None, 1, gw), lambda g, c: (g, 0, 0))],
        out_specs=[wide,
                   pl.BlockSpec((CHUNK, n), lambda g, c: (rev(c), g)),
                   pl.BlockSpec((CHUNK, n), lambda g, c: (rev(c), g)),
                   pl.BlockSpec((None, CHUNK, hpg), lambda g, c: (g, rev(c), 0)),
                   pl.BlockSpec((None, 1, hpg), lambda g, c: (g, 0, 0))],
        out_shape=[jax.ShapeDtypeStruct((s, d_inner), F32),
                   jax.ShapeDtypeStruct((s, SSM_GROUPS * n), F32), jax.ShapeDtypeStruct((s, SSM_GROUPS * n), F32),
                   jax.ShapeDtypeStruct((SSM_GROUPS, s, hpg), F32), jax.ShapeDtypeStruct((SSM_GROUPS, 1, hpg), F32)],
        scratch=[pltpu.VMEM((n, gw), F32), pltpu.VMEM((CHUNK, gw), F32), pltpu.VMEM((CHUNK, gw), F32)],
        sem=("parallel", "arbitrary"), ride=ride,
        args=(xbc, xbc, xbc, dt_g, dt_gt, a_g, a_gt, states, dy, dskip_e))


def _lin(p):
    return 4 * p[0] + 2 * p[1] + p[2]


class _Gather:
    def __init__(self, arrs):
        self.arrs = list(arrs)

    def out_shape(self):
        return [jax.ShapeDtypeStruct((NDEV,) + a.shape, a.dtype) for a in self.arrs]

    def _copies(self, ins, outs, sems):
        send_sems, recv_sems, local_sems = sems
        x, y, c = lax.axis_index("x"), lax.axis_index("y"), lax.axis_index("c")
        me, sibling = (x, y, c), (x, y, 1 - c)
        chips = [(1 - x, y), (x, 1 - y), (1 - x, 1 - y)]

        def copy(a, k, block, to, src=None):
            rows = outs[a].at[_lin(block)]
            return pltpu.make_async_remote_copy(
                src_ref=rows if src is None else src, dst_ref=rows,
                send_sem=send_sems.at[a * NPEER + k], recv_sem=recv_sems.at[a * NPEER + k],
                device_id=to, device_id_type=pl.DeviceIdType.MESH)

        na = len(ins)
        mine = [pltpu.make_async_copy(ins[a], outs[a].at[_lin(me)], local_sems.at[a]) for a in range(na)]
        first = []
        for a in range(na):
            first.append(copy(a, 0, me, sibling, src=ins[a]))
            first += [copy(a, 1 + j, me, (*chip, c), src=ins[a]) for j, chip in enumerate(chips)]
        return copy, mine, first, me, sibling, chips, c, na

    def start(self, ins, outs, sems):
        _, mine, first, *_ = self._copies(ins, outs, sems)
        for cp in mine + first:
            cp.start()

    def finish(self, ins, outs, sems):
        copy, mine, first, me, sibling, chips, c, na = self._copies(ins, outs, sems)
        passed = []
        for j, chip in enumerate(chips):
            for a in range(na):
                copy(a, 1 + j, (*chip, c), me).wait_recv()
                cp = copy(a, 4 + j, (*chip, c), sibling)
                cp.start()
                passed.append(cp)
        for a in range(na):
            copy(a, 0, sibling, me).wait_recv()
            for j, chip in enumerate(chips):
                copy(a, 4 + j, (*chip, 1 - c), me).wait_recv()
        for cp in first + passed:
            cp.wait_send()
        for cp in mine:
            cp.wait()


class _Scatter:
    def __init__(self, arrs, ks=tuple(range(NDEV))):
        self.arrs = list(arrs)
        self.ks = [tuple(k) for k in ks] if isinstance(ks[0], (tuple, list)) else [tuple(ks)] * len(self.arrs)
        assert len(self.ks) == len(self.arrs)

    def out_shape(self):
        return [jax.ShapeDtypeStruct((len(k),) + a.shape[1:], a.dtype) for a, k in zip(self.arrs, self.ks)]

    def _copies(self, ins, outs, sems):
        send_sems, recv_sems, local_sems = sems
        x, y, c = lax.axis_index("x"), lax.axis_index("y"), lax.axis_index("c")
        me = (x, y, c)

        def peer(k):
            return (1 - x if k & 4 else x, 1 - y if k & 2 else y, 1 - c if k & 1 else c)

        local, remote = [], []
        for a in range(len(ins)):
            for i, k in enumerate(self.ks[a]):
                if k == 0:
                    local.append(pltpu.make_async_copy(ins[a].at[_lin(me)], outs[a].at[i], local_sems.at[a]))
                else:
                    remote.append(pltpu.make_async_remote_copy(
                        src_ref=ins[a].at[_lin(peer(k))], dst_ref=outs[a].at[i],
                        send_sem=send_sems.at[a * NPEER + k - 1], recv_sem=recv_sems.at[a * NPEER + k - 1],
                        device_id=peer(k), device_id_type=pl.DeviceIdType.MESH))
        return local, remote

    def start(self, ins, outs, sems):
        local, remote = self._copies(ins, outs, sems)
        for cp in local + remote:
            cp.start()

    def finish(self, ins, outs, sems):
        local, remote = self._copies(ins, outs, sems)
        for cp in remote:
            cp.wait_recv()
        for cp in remote:
            cp.wait_send()
        for cp in local:
            cp.wait()


def _exchange_scratch(na):
    return [pltpu.SemaphoreType.DMA((na * NPEER,)), pltpu.SemaphoreType.DMA((na * NPEER,)),
            pltpu.SemaphoreType.DMA((na,))]


def _exchange_alone(ex, *, name, in_vmem=False):
    na = len(ex.arrs)

    def body(*refs):
        ins, outs, sems = refs[:na], refs[na:2 * na], refs[2 * na:]
        ex.start(ins, outs, sems)
        ex.finish(ins, outs, sems)

    spec = pl.BlockSpec(memory_space=pltpu.VMEM if in_vmem else pl.ANY)
    return pl.pallas_call(
        body, name=name, out_shape=tuple(ex.out_shape()), in_specs=[spec] * na, out_specs=tuple([spec] * na),
        scratch_shapes=_exchange_scratch(na),
        compiler_params=pltpu.CompilerParams(vmem_limit_bytes=VMEM_LIMIT),
    )(*ex.arrs)


def _call(body, *, name, grid, in_specs, out_specs, out_shape, args, sem, scratch=(), ride=None, aliases=None):
    n_in, n_out, n_scr = len(in_specs), len(out_specs), len(scratch)
    if ride is None:
        outs = pl.pallas_call(
            body, name=name, grid=grid, in_specs=list(in_specs), out_specs=tuple(out_specs),
            out_shape=tuple(out_shape), scratch_shapes=list(scratch), input_output_aliases=aliases or {},
            compiler_params=_params(*sem))(*args)
        return tuple(outs), ()
    nx = len(ride.arrs)
    hbm = pl.BlockSpec(memory_space=pl.ANY)

    def hosted(*refs):
        ins, x_in = refs[:n_in], refs[n_in:n_in + nx]
        o0 = n_in + nx
        outs, x_out = refs[o0:o0 + n_out], refs[o0 + n_out:o0 + n_out + nx]
        s0 = o0 + n_out + nx
        scr, x_sem = refs[s0:s0 + n_scr], refs[s0 + n_scr:]
        ids = [pl.program_id(i) for i in range(len(grid))]
        first = functools.reduce(jnp.logical_and, [i == 0 for i in ids])
        last = functools.reduce(jnp.logical_and, [i == g - 1 for i, g in zip(ids, grid)])

        @pl.when(first)
        def _():
            ride.start(x_in, x_out, x_sem)

        body(*ins, *outs, *scr)

        @pl.when(last)
        def _():
            ride.finish(x_in, x_out, x_sem)

    outs = pl.pallas_call(
        hosted, name=name, grid=grid, in_specs=list(in_specs) + [hbm] * nx,
        out_specs=tuple(list(out_specs) + [hbm] * nx), out_shape=tuple(list(out_shape) + ride.out_shape()),
        scratch_shapes=list(scratch) + _exchange_scratch(nx), input_output_aliases=aliases or {},
        compiler_params=_params(*(("arbitrary",) * len(grid))))(*args, *ride.arrs)
    return tuple(outs[:n_out]), tuple(outs[n_out:])


def _pack(parts):
    flat = jnp.concatenate([p.reshape(-1).astype(F32) for p in parts])
    tile = SUBLANES * LANES
    pad = (-flat.shape[0]) % tile
    return jnp.pad(flat, (0, pad)).reshape(-1, LANES)


def _unpack(buf, shapes):
    flat = buf.reshape(-1)
    out, off = [], 0
    for shp in shapes:
        size = math.prod(shp)
        out.append(flat[off:off + size].reshape(shp))
        off += size
    return out


KS_FLAT = (0, 1, 4, 5, 2, 3)
KS_DIAG = (6, 7)


def _local_step(x, target, wa, wo, ws, wos, rel_bias, conv_w, conv_b, dt_bias, a_log, d_skip, norm_w, ln_g, ln_b,
                dist=False):
    s, d = x.shape
    da = wo.shape[-2]
    heads = da // HEAD_DIM
    qkv_cols = 3 * N_GROUPS_ATTN * da
    d_inner = wos.shape[0] * (NDEV if dist else 1)
    conv_dim = conv_w.shape[1]
    ssm_heads = dt_bias.shape[1]
    hpg = ssm_heads // SSM_GROUPS
    gn = (conv_dim - d_inner) // 2
    n_state = gn // SSM_GROUPS
    gw = d_inner // SSM_GROUPS
    p = gw // hpg
    in_ssm = d_inner + conv_dim + ssm_heads
    xb = _cast_bf16(x, name="cast_x")

    per_dev = in_ssm // NDEV
    third = (per_dev // 3) // 16 * 16
    band_rows = (third, third, per_dev - 2 * third)
    qkvs, ws_bands, row0 = [], [], 0
    for g in range(N_GROUPS_ATTN):
        ride = _Gather([ws[row0:row0 + band_rows[g]]]) if dist else None
        row0 += band_rows[g]
        got = _mm(xb, wa, name=f"mm_qkv_g{g}", out_dtype=BF16, n_off=g * 3 * da, n_out=3 * da, ride=ride)
        if dist:
            ws_bands.append(got[1][0])
            got = got[0]
        qkvs.append(got)
    if dist:
        ws = jnp.concatenate(ws_bands, axis=1).reshape(in_ssm, d)
    gate = _mm(xb, wa, name="mm_gate", out_dtype=F32, n_off=qkv_cols, n_out=da)
    bias, bucket = _bias_tables(rel_bias, heads)
    os_, ls_ = [], []
    for g, (_, dil) in enumerate(ATTN_PATTERNS):
        ride = _Gather([wo]) if dist and g == 0 else None
        o, l, rode = _attn_fwd_group(qkvs[g], bias[g], g, dil, da, ride=ride)
        if rode:
            (wo,) = rode
        os_.append(o)
        ls_.append(l)
    o, lse, y = _attn_combine(os_, ls_, gate)
    h1 = _mm(y, wo, name="mm_out_attn", out_dtype=F32)
    x1, x1b = _ln_fwd(x, h1, ln_g[0:1], ln_b[0:1], name="ln1_fwd")

    if dist:
        proj, (wos_slabs,) = _mm(x1b, ws, name="mm_in_ssm", out_dtype=F32, trans_b=True, ride=_Gather([wos]))
        wos = wos_slabs.reshape(d_inner, d)
    else:
        proj = _mm(x1b, ws, name="mm_in_ssm", out_dtype=F32, trans_b=True)
    xbc = _conv_fwd(proj, conv_w, conv_b, d_inner)
    dt = _dt_fwd(proj, dt_bias, d_inner + conv_dim)
    dt_g = dt.reshape(s, SSM_GROUPS, hpg).transpose(1, 0, 2)
    dt_gt = dt.reshape(s, SSM_GROUPS, hpg).transpose(1, 2, 0)
    a = -jnp.exp(a_log)
    a_g = a.reshape(SSM_GROUPS, 1, hpg)
    a_gt = a.reshape(SSM_GROUPS, hpg, 1)
    dskip_e = jnp.repeat(d_skip.reshape(SSM_GROUPS, 1, hpg), p, axis=2)
    yn, yscan, states = _ssd_fwd(xbc, proj, dt_g, dt_gt, a_g, a_gt, dskip_e, norm_w, d_inner, n_state)
    h2 = _mm(yn, wos, name="mm_out_ssm", out_dtype=F32)

    du2, du2b, dg1, db1, loss_t = _ln_bwd(x1, h2, ln_g[1:2], ln_b[1:2], target, with_loss=True, name="ln2_loss_bwd")
    loss = loss_t[0, 0]
    dyn = _mm(du2b, wos, name="mm_dyn", out_dtype=F32, trans_b=True)
    g_wos = _mm(yn.T, du2b, name="mm_dw_out_ssm", out_dtype=BF16)
    parts = {}
    dyscan, dproj_ssm, g_norm, g_dskip = _ssd_epilogue_bwd(dyn, yscan, xbc, proj, dskip_e, norm_w, hpg)
    ride = _Scatter([g_wos.reshape(NDEV, d_inner // NDEV, d)]) if dist else None
    (dxs, d_bm, d_cm, ddt_g, g_a), rode = _ssd_scan_bwd(xbc, dt_g, dt_gt, a_g, a_gt, states, dyscan, dskip_e,
                                                         d_inner, n_state, ride=ride)
    parts["w_out_ssm"] = [list(rode)]
    g_alog = g_a.reshape(1, ssm_heads) * a
    dproj_ssm, g_conv_w, g_conv_b = _conv_bwd(proj, conv_w, conv_b, (dxs, d_bm, d_cm), d_inner, dproj_ssm)
    ddt = ddt_g.transpose(1, 0, 2).reshape(s, ssm_heads)
    dproj_ssm, g_dtb = _dt_bwd(proj, dt_bias, ddt, d_inner + conv_dim, dproj_ssm)
    g_ws = _mm(dproj_ssm.T, x1b, name="mm_dw_in_ssm", out_dtype=BF16)
    if dist:
        g_ws_slabs = g_ws.reshape(NDEV, per_dev, d)
        dx1, near = _mm(dproj_ssm, ws, name="mm_dx1", out_dtype=F32, res=du2, res_scale=DEEPNORM_ALPHA,
                        ride=_Scatter([g_ws_slabs], KS_FLAT))
    else:
        dx1 = _mm(dproj_ssm, ws, name="mm_dx1", out_dtype=F32, res=du2, res_scale=DEEPNORM_ALPHA)

    du1, du1b, dg0, db0 = _ln_bwd(x, h1, ln_g[0:1], ln_b[0:1], dx1, with_loss=False, name="ln1_bwd")
    dy = _mm(du1b, wo, name="mm_dy", out_dtype=F32, trans_b=True)
    g_wo = _mm(y.T, du1b, name="mm_dw_out_attn", out_dtype=BF16, slab_out=NDEV)
    do, dgate, dd = _attn_bwd_prep(dy, o, gate)
    rides = [_Scatter([g_ws_slabs], KS_DIAG[0:1]), _Scatter([g_wo]), None] if dist else [None] * N_GROUPS_ATTN
    dparts, dss, rode_attn = [], [], []
    for g, (_, dil) in enumerate(ATTN_PATTERNS):
        (dq, dk, dv, ds), rode = _attn_bwd_group(qkvs[g], do, lse, dd, bias[g], g, dil, da, ride=rides[g])
        dparts += [dq, dk, dv]
        dss.append(ds)
        rode_attn += list(rode)
    g_bias = _bias_bwd(jnp.stack(dss), bucket)
    g_rel_bias = g_bias.transpose(2, 0, 1).reshape(NUM_BUCKETS, N_GROUPS_ATTN * heads)
    dproj_attn = jnp.concatenate(dparts + [dgate], axis=1)
    pending = None
    if dist:
        parts["w_out_attn"] = [rode_attn[1:]]
        xbt = xb.T
        half = d // 2
        g_top, (diag_b,) = _mm(xbt[:half], dproj_attn, name="mm_dw_in_attn_top", out_dtype=BF16, slab_out=NDEV,
                               ride=_Scatter([g_ws_slabs], KS_DIAG[1:2]))
        parts["w_in_ssm"] = [[near[0], rode_attn[0], diag_b]]
        g_bot, (top_a,) = _mm(xbt[half:], dproj_attn, name="mm_dw_in_attn_bottom", out_dtype=BF16, slab_out=NDEV,
                              ride=_Scatter([g_top], KS_FLAT))
        dx, (top_b, bot_a) = _mm(dproj_attn, wa, name="mm_dx", out_dtype=F32, trans_b=True, res=du1,
                                 res_scale=DEEPNORM_ALPHA, ride=_Scatter([g_top, g_bot], [KS_DIAG, KS_FLAT]))
        parts["w_in_attn"] = [[top_a, top_b], [bot_a]]
        pending = _Scatter([g_bot], KS_DIAG)
    else:
        g_wa = _mm(xb.T, dproj_attn, name="mm_dw_in_attn", out_dtype=BF16, slab_out=NDEV)
        dx = _mm(dproj_attn, wa, name="mm_dx", out_dtype=F32, trans_b=True, res=du1, res_scale=DEEPNORM_ALPHA)

    g_ln_g = jnp.concatenate([dg0, dg1], axis=0)
    g_ln_b = jnp.concatenate([db0, db1], axis=0)
    small = dict(rel_bias=g_rel_bias, dt_bias=g_dtb, a_log=g_alog, d_skip=g_dskip.reshape(1, ssm_heads),
                 ln_g=g_ln_g, ln_b=g_ln_b, conv_w=g_conv_w, conv_b=g_conv_b, ssm_norm_w=g_norm)
    if dist:
        return loss, dx, parts, pending, small
    return loss, dx, g_wa, g_wo, g_ws, g_wos, small


REPLICATED = ("rel_bias", "dt_bias", "a_log", "d_skip", "ln_g", "ln_b")
SHARDED_SMALL = ("conv_w", "conv_b", "ssm_norm_w")


def kernel(x, w_in_attn, w_out_attn, rel_bias, w_in_ssm, conv_w, conv_b, dt_bias, a_log, d_skip, ssm_norm_w, w_out_ssm, ln_g, ln_b, loss_target, m_w_in_attn, m_w_out_attn, m_rel_bias, m_w_in_ssm, m_conv_w, m_conv_b, m_dt_bias, m_a_log, m_d_skip, m_ssm_norm_w, m_w_out_ssm, m_ln_g, m_ln_b, v_w_in_attn, v_w_out_attn, v_rel_bias, v_w_in_ssm, v_conv_w, v_conv_b, v_dt_bias, v_a_log, v_d_skip, v_ssm_norm_w, v_w_out_ssm, v_ln_g, v_ln_b):
    w = dict(w_in_attn=w_in_attn, w_out_attn=w_out_attn, rel_bias=rel_bias, w_in_ssm=w_in_ssm, conv_w=conv_w,
             conv_b=conv_b, dt_bias=dt_bias, a_log=a_log, d_skip=d_skip, ssm_norm_w=ssm_norm_w,
             w_out_ssm=w_out_ssm, ln_g=ln_g, ln_b=ln_b)
    m = dict(w_in_attn=m_w_in_attn, w_out_attn=m_w_out_attn, rel_bias=m_rel_bias, w_in_ssm=m_w_in_ssm,
             conv_w=m_conv_w, conv_b=m_conv_b, dt_bias=m_dt_bias, a_log=m_a_log, d_skip=m_d_skip,
             ssm_norm_w=m_ssm_norm_w, w_out_ssm=m_w_out_ssm, ln_g=m_ln_g, ln_b=m_ln_b)
    v = dict(w_in_attn=v_w_in_attn, w_out_attn=v_w_out_attn, rel_bias=v_rel_bias, w_in_ssm=v_w_in_ssm,
             conv_w=v_conv_w, conv_b=v_conv_b, dt_bias=v_dt_bias, a_log=v_a_log, d_skip=v_d_skip,
             ssm_norm_w=v_ssm_norm_w, w_out_ssm=v_w_out_ssm, ln_g=v_ln_g, ln_b=v_ln_b)
    me = _lin((lax.axis_index("x"), lax.axis_index("y"), lax.axis_index("c")))
    d = x.shape[2]
    big = ("w_in_attn", "w_out_attn", "w_in_ssm", "w_out_ssm")

    for t in (w, m, v):
        t["w_in_ssm"] = t["w_in_ssm"].transpose(0, 2, 1)
    shards = {k: _cast_bf16(w[k], name=f"cast_{k}") for k in big}
    (wa,) = _exchange_alone(_Gather([shards["w_in_attn"]]), name="gather_w_in_attn")
    cpd = conv_w.shape[2]
    npd = ssm_norm_w.shape[1]
    small_shapes = [(CONV_WIDTH, cpd), (1, cpd), (1, npd)]
    (small_all,) = _exchange_alone(_Gather([_pack([conv_w[0], conv_b, ssm_norm_w])]), name="gather_small_weights",
                                   in_vmem=True)
    small_parts = [_unpack(small_all[i], small_shapes) for i in range(NDEV)]
    conv_w_full = jnp.concatenate([p[0] for p in small_parts], axis=1)
    conv_b_full = jnp.concatenate([p[1] for p in small_parts], axis=1)
    norm_w_full = jnp.concatenate([p[2] for p in small_parts], axis=1)

    loss, dx, parts, pending, small = _local_step(
        x[0], loss_target[0], wa, shards["w_out_attn"], shards["w_in_ssm"], shards["w_out_ssm"], rel_bias,
        conv_w_full, conv_b_full, dt_bias[0:1], a_log[0:1], d_skip[0:1], norm_w_full, ln_g, ln_b, dist=True)
    loss = lax.psum(loss, MESH_AXES)
    out = {}
    out["w_in_ssm"], late = _adamw_sum(parts["w_in_ssm"], w["w_in_ssm"], m["w_in_ssm"], v["w_in_ssm"],
                                       name="adamw_w_in_ssm", ride=pending)
    out["w_in_ssm"] = tuple(t.transpose(0, 2, 1) for t in out["w_in_ssm"])
    parts["w_in_attn"][1] += list(late)
    for k in ("w_out_ssm", "w_out_attn", "w_in_attn"):
        out[k] = _adamw_sum(parts[k], w[k], m[k], v[k], name=f"adamw_{k}")

    order = REPLICATED + SHARDED_SMALL
    g_shapes = [small[k].shape for k in order]
    (g_all,) = _exchange_alone(_Gather([_pack([small[k] for k in order])]), name="gather_small_grads", in_vmem=True)
    g_sum = dict(zip(order, _unpack(_sum_slots(g_all, name="sum_small_grads"), g_shapes)))
    g_mine = {k: g_sum[k] for k in REPLICATED}
    g_mine["conv_w"] = lax.dynamic_slice_in_dim(g_sum["conv_w"], me * cpd, cpd, axis=1)
    g_mine["conv_b"] = lax.dynamic_slice_in_dim(g_sum["conv_b"], me * cpd, cpd, axis=1)
    g_mine["ssm_norm_w"] = lax.dynamic_slice_in_dim(g_sum["ssm_norm_w"], me * npd, npd, axis=1)
    w_shapes = [w[k].shape for k in order]
    g_pack = _pack([g_mine[k] for k in order])
    d_p, m_p, v_p = _adamw_small(g_pack, _pack([w[k] for k in order]), _pack([m[k] for k in order]),
                                 _pack([v[k] for k in order]), name="adamw_small")
    for k, gk, dk, mk, vk in zip(order, _unpack(g_pack, w_shapes), _unpack(d_p, w_shapes), _unpack(m_p, w_shapes),
                                 _unpack(v_p, w_shapes)):
        out[k] = (gk, dk, mk, vk)

    names = ("w_in_attn", "w_out_attn", "rel_bias", "w_in_ssm", "conv_w", "conv_b", "dt_bias", "a_log", "d_skip",
             "ssm_norm_w", "w_out_ssm", "ln_g", "ln_b")
    res = [loss, dx[None]]
    for i in range(4):
        res += [out[k][i] for k in names]
    return tuple(res)
```

```python
import functools
import math

import jax
import jax.numpy as jnp
from jax import lax
from jax.experimental import pallas as pl
from jax.experimental.pallas import tpu as pltpu

F32 = jnp.float32
BF16 = jnp.bfloat16
MESH_AXES = ("x", "y", "c")
NDEV = 8
NPEER = NDEV - 1
LANES = 128
SUBLANES = 8
VMEM_LIMIT = 52 * 1024 * 1024
MM_VMEM_BUDGET = 40 * 1024 * 1024
MM_TK_MAX = 4096
MM_TN_MAX = 1024

ATTN_PATTERNS = ((128, 1), (512, 4), (2048, 16))
N_GROUPS_ATTN = 3
HEAD_DIM = 128
ATTN_BLOCK = 128
ATTN_ROWS_TIMES_HEADS = 8192
NUM_BUCKETS = 32
MAX_DISTANCE = 2048
SSM_GROUPS = 8
CONV_WIDTH = 4
CHUNK = 128
DEPTH = 2
DEEPNORM_ALPHA = (2 * DEPTH) ** 0.25
LN_EPS = 1e-5
RMS_EPS = 1e-5
NEG_INF = -1e30
ADAM_LR = 0.001
ADAM_B1 = 0.9
ADAM_B2 = 0.999
ADAM_EPS = 1e-08
ADAM_WD = 0.01
ADAM_STEP = 10
HIGHEST = lax.Precision.HIGHEST


def _params(*sem):
    return pltpu.CompilerParams(dimension_semantics=sem, vmem_limit_bytes=VMEM_LIMIT)


def _pick(n, prefs):
    for p in prefs:
        if n % p == 0:
            return p
    return n


def _row_tile(r, limit):
    return max(t for t in range(2 * SUBLANES, limit + 1, 2 * SUBLANES) if r % t == 0)


def _dot(a, b):
    return jnp.dot(a, b, preferred_element_type=F32)


def _dot_nt(a, b):
    return lax.dot_general(a, b, (((1,), (1,)), ((), ())), preferred_element_type=F32)


def _dot_tn(a, b):
    return lax.dot_general(a, b, (((0,), (0,)), ((), ())), preferred_element_type=F32)


def _sigmoid(x):
    return 1.0 / (1.0 + jnp.exp(-x))


def _mm(a, b, *, name, out_dtype, trans_b=False, slab_out=0, n_off=0, n_out=None,
        res=None, res_scale=1.0, ride=None):
    m, k = a.shape
    slab_b = b.ndim == 3
    if slab_b:
        ns = b.shape[0]
        if trans_b:
            n, kper = b.shape[1], b.shape[2]
            assert ns * kper == k
        else:
            nper = b.shape[2]
            n = ns * nper
            assert b.shape[1] == k
    else:
        n = b.shape[0] if trans_b else b.shape[1]
        assert (b.shape[1] if trans_b else b.shape[0]) == k
    n_out = n if n_out is None else n_out
    tm = _pick(m, (1024, 640, 512, 256, 128))
    nconstraint = math.gcd(n_out, n_off) if n_off else n_out
    if slab_b and not trans_b:
        nconstraint = math.gcd(nconstraint, nper)
    if slab_out:
        nconstraint = math.gcd(nconstraint, n_out // slab_out)
    kconstraint = kper if (slab_b and trans_b) else k
    tk = max(t for t in range(LANES, min(kconstraint, MM_TK_MAX) + 1, LANES) if kconstraint % t == 0)
    nk = k // tk
    out_bytes = jnp.dtype(out_dtype).itemsize

    def vmem_bytes(t):
        return (2 * 2 * tk * (tm + t) + 2 * tm * t * out_bytes + (4 * tm * t if nk > 1 else 0)
                + (2 * 4 * tm * t if res is not None else 0))

    fits = [t for t in range(LANES, min(nconstraint, MM_TN_MAX) + 1, LANES)
            if nconstraint % t == 0 and vmem_bytes(t) <= MM_VMEM_BUDGET]
    tn = max(fits)
    nb0 = n_off // tn
    grid = (m // tm, n_out // tn, nk)

    a_spec = pl.BlockSpec((tm, tk), lambda i, j, kk: (i, kk))
    if slab_b and not trans_b:
        nps = nper // tn
        b_spec = pl.BlockSpec((None, tk, tn), lambda i, j, kk: ((j + nb0) // nps, kk, (j + nb0) % nps))
    elif slab_b and trans_b:
        kps = kper // tk
        b_spec = pl.BlockSpec((None, tn, tk), lambda i, j, kk: (kk // kps, j + nb0, kk % kps))
    elif trans_b:
        b_spec = pl.BlockSpec((tn, tk), lambda i, j, kk: (j + nb0, kk))
    else:
        b_spec = pl.BlockSpec((tk, tn), lambda i, j, kk: (kk, j + nb0))
    if slab_out:
        ops = (n_out // slab_out) // tn
        o_spec = pl.BlockSpec((None, tm, tn), lambda i, j, kk: (j // ops, i, j % ops))
        o_shape = jax.ShapeDtypeStruct((slab_out, m, n_out // slab_out), out_dtype)
    else:
        o_spec = pl.BlockSpec((tm, tn), lambda i, j, kk: (i, j))
        o_shape = jax.ShapeDtypeStruct((m, n_out), out_dtype)
    in_specs = [a_spec, b_spec]
    args = [a, b]
    if res is not None:
        in_specs.append(pl.BlockSpec((tm, tn), lambda i, j, kk: (i, j)))
        args.append(res)

    def body(*refs):
        a_ref, b_ref = refs[0], refs[1]
        r_ref = refs[2] if res is not None else None
        o_ref = refs[3] if res is not None else refs[2]
        av = a_ref[...].astype(BF16)
        bv = b_ref[...].astype(BF16)
        part = _dot_nt(av, bv) if trans_b else _dot(av, bv)

        def finish(r):
            if res is not None:
                r = r + res_scale * r_ref[...]
            o_ref[...] = r.astype(out_dtype)

        if nk == 1:
            finish(part)
            return
        acc = refs[-1]
        kk = pl.program_id(2)

        @pl.when(kk == 0)
        def _():
            acc[...] = part

        @pl.when(kk > 0)
        def _():
            acc[...] += part

        @pl.when(kk == nk - 1)
        def _():
            finish(acc[...])

    outs, rode = _call(
        body, name=name, grid=grid, in_specs=in_specs, out_specs=[o_spec], out_shape=[o_shape], args=args,
        scratch=[pltpu.VMEM((tm, tn), F32)] if nk > 1 else [], ride=ride,
        sem=("parallel", "parallel", "arbitrary"))
    return (outs[0], rode) if ride is not None else outs[0]


def _cast_bf16(w, *, name):
    r, c = w.shape[-2:]
    tr = _row_tile(r, 512)

    def body(w_ref, o_ref):
        o_ref[...] = w_ref[...].astype(BF16)

    in_spec = (pl.BlockSpec((None, tr, c), lambda i: (0, i, 0)) if w.ndim == 3
               else pl.BlockSpec((tr, c), lambda i: (i, 0)))
    return pl.pallas_call(
        body, name=name, grid=(r // tr,),
        in_specs=[in_spec],
        out_specs=pl.BlockSpec((tr, c), lambda i: (i, 0)),
        out_shape=jax.ShapeDtypeStruct((r, c), BF16),
        compiler_params=_params("parallel"),
    )(w)


def _adam_math(w, g, m, v):
    m2 = ADAM_B1 * m + (1.0 - ADAM_B1) * g
    v2 = ADAM_B2 * v + (1.0 - ADAM_B2) * (g * g)
    m_hat = m2 / (1.0 - ADAM_B1 ** ADAM_STEP)
    v_hat = v2 / (1.0 - ADAM_B2 ** ADAM_STEP)
    delta = -ADAM_LR * (m_hat / (jnp.sqrt(v_hat) + ADAM_EPS) + ADAM_WD * w)
    return delta, m2, v2


def _adamw_sum(bands, w, m, v, *, name, ride=None):
    _, r, c = w.shape
    nband = len(bands)
    rows = r // nband
    tr = _row_tile(rows, 128)
    tc = c if (c % LANES or c <= 2560) else _pick(c, (2048, 1024, 512, 256, 128))
    nt = rows // tr
    flat = [p for band in bands for p in band]

    def body(*refs):
        p_refs = refs[:len(flat)]
        w_ref, m_ref, v_ref, g_out, d_out, m_out, v_out = refs[len(flat):]
        i = pl.program_id(0)
        g, at = None, 0
        for q, band in enumerate(bands):
            gq = None
            for p_ref in p_refs[at:at + len(band)]:
                for s in range(p_ref.shape[0]):
                    t = p_ref[s].astype(F32)
                    gq = t if gq is None else gq + t
            at += len(band)
            g = gq if q == 0 else jnp.where(i >= q * nt, gq, g)
        d, m2, v2 = _adam_math(w_ref[...], g, m_ref[...], v_ref[...])
        g_out[...] = g
        d_out[...] = d
        m_out[...] = m2
        v_out[...] = v2

    def band_spec(p, q):
        return pl.BlockSpec((p.shape[0], tr, tc), lambda i, j: (0, jnp.clip(i - q * nt, 0, nt - 1), j))

    spec = pl.BlockSpec((None, tr, tc), lambda i, j: (0, i, j))
    shp = jax.ShapeDtypeStruct((1, r, c), F32)
    outs, rode = _call(
        body, name=name, grid=(r // tr, c // tc),
        in_specs=[band_spec(p, q) for q, band in enumerate(bands) for p in band] + [spec, spec, spec],
        out_specs=[spec, spec, spec, spec], out_shape=[shp, shp, shp, shp], args=(*flat, w, m, v),
        sem=("parallel", "parallel"), ride=ride)
    return (outs, rode) if ride is not None else outs


def _adamw_small(g, w, m, v, *, name):
    shp = jax.ShapeDtypeStruct(w.shape, F32)

    def body(g_ref, w_ref, m_ref, v_ref, d_out, m_out, v_out):
        d, m2, v2 = _adam_math(w_ref[...], g_ref[...], m_ref[...], v_ref[...])
        d_out[...] = d
        m_out[...] = m2
        v_out[...] = v2

    return pl.pallas_call(body, name=name, out_shape=(shp, shp, shp),
                          compiler_params=pltpu.CompilerParams(vmem_limit_bytes=VMEM_LIMIT))(g, w, m, v)


def _sum_slots(parts, *, name):
    _, r, c = parts.shape

    def body(p_ref, o_ref):
        g = p_ref[0]
        for s in range(1, NDEV):
            g = g + p_ref[s]
        o_ref[...] = g

    return pl.pallas_call(body, name=name, out_shape=jax.ShapeDtypeStruct((r, c), F32),
                          compiler_params=pltpu.CompilerParams(vmem_limit_bytes=VMEM_LIMIT))(parts)


def _ln_parts(u):
    mu = jnp.mean(u, axis=-1, keepdims=True)
    xc = u - mu
    var = jnp.mean(xc * xc, axis=-1, keepdims=True)
    rstd = lax.rsqrt(var + LN_EPS)
    return xc * rstd, rstd


def _ln_fwd(xin, h, g, b, *, name):
    s, d = xin.shape
    tm = _pick(s, (128,))

    def body(x_ref, h_ref, g_ref, b_ref, o_ref, ob_ref):
        xhat, _ = _ln_parts(DEEPNORM_ALPHA * x_ref[...] + h_ref[...])
        o = xhat * g_ref[...] + b_ref[...]
        o_ref[...] = o
        ob_ref[...] = o.astype(BF16)

    row = pl.BlockSpec((tm, d), lambda i: (i, 0))
    vec = pl.BlockSpec((1, d), lambda i: (0, 0))
    return pl.pallas_call(
        body, name=name, grid=(s // tm,), in_specs=[row, row, vec, vec], out_specs=(row, row),
        out_shape=(jax.ShapeDtypeStruct((s, d), F32), jax.ShapeDtypeStruct((s, d), BF16)),
        compiler_params=_params("parallel"),
    )(xin, h, g, b)


def _ln_bwd(xin, h, g, b, cot, *, with_loss, name):
    s, d = xin.shape
    tm = _pick(s, (128,))

    def body(x_ref, h_ref, g_ref, b_ref, c_ref, du_ref, dub_ref, dg_ref, db_ref, *rest):
        i = pl.program_id(0)
        xhat, rstd = _ln_parts(DEEPNORM_ALPHA * x_ref[...] + h_ref[...])
        gv = g_ref[...]
        if with_loss:
            diff = xhat * gv + b_ref[...] - c_ref[...]
            part = 0.5 * jnp.sum(jnp.mean(diff * diff, axis=-1, keepdims=True), axis=0, keepdims=True)
            dout = diff / d
        else:
            dout = c_ref[...]

        @pl.when(i == 0)
        def _():
            dg_ref[...] = jnp.zeros_like(dg_ref)
            db_ref[...] = jnp.zeros_like(db_ref)
            if with_loss:
                rest[0][...] = jnp.zeros_like(rest[0])

        dg_ref[...] += jnp.sum(dout * xhat, axis=0, keepdims=True)
        db_ref[...] += jnp.sum(dout, axis=0, keepdims=True)
        if with_loss:
            rest[0][...] += jnp.broadcast_to(part, rest[0].shape)
        dxh = dout * gv
        du = rstd * (dxh - jnp.mean(dxh, axis=-1, keepdims=True)
                     - xhat * jnp.mean(dxh * xhat, axis=-1, keepdims=True))
        du_ref[...] = du
        dub_ref[...] = du.astype(BF16)

    row = pl.BlockSpec((tm, d), lambda i: (i, 0))
    vec = pl.BlockSpec((1, d), lambda i: (0, 0))
    out_specs = [row, row, vec, vec]
    out_shape = [jax.ShapeDtypeStruct((s, d), F32), jax.ShapeDtypeStruct((s, d), BF16),
                 jax.ShapeDtypeStruct((1, d), F32), jax.ShapeDtypeStruct((1, d), F32)]
    if with_loss:
        out_specs.append(pl.BlockSpec((SUBLANES, LANES), lambda i: (0, 0)))
        out_shape.append(jax.ShapeDtypeStruct((SUBLANES, LANES), F32))
    return pl.pallas_call(
        body, name=name, grid=(s // tm,), in_specs=[row, row, vec, vec, row],
        out_specs=tuple(out_specs), out_shape=tuple(out_shape),
        compiler_params=_params("arbitrary"),
    )(xin, h, g, b, cot)


def t5_causal_bucket(dist):
    max_exact = NUM_BUCKETS // 2
    d_f = jnp.maximum(dist, 1).astype(jnp.float32)
    large = max_exact + (jnp.log(d_f / max_exact) / math.log(MAX_DISTANCE / max_exact)
                         * (NUM_BUCKETS - max_exact)).astype(jnp.int32)
    large = jnp.minimum(large, NUM_BUCKETS - 1)
    return jnp.where(dist < max_exact, dist, large)


def _bias_tables(rel_bias, heads):
    qi = lax.broadcasted_iota(jnp.int32, (ATTN_BLOCK, 2 * ATTN_BLOCK), 0)
    ki = lax.broadcasted_iota(jnp.int32, (ATTN_BLOCK, 2 * ATTN_BLOCK), 1)
    delta = ATTN_BLOCK + qi - ki
    buckets = []
    for window, dilation in ATTN_PATTERNS:
        span = window // dilation
        assert span == ATTN_BLOCK
        band = (delta >= 0) & (delta <= span)
        buckets.append(jnp.where(band, t5_causal_bucket(jnp.clip(delta, 0, None) * dilation), -1))
    bucket = jnp.stack(buckets).astype(jnp.int32)

    def body(bk_ref, tbl_ref, o_ref):
        col = pl.program_id(0) * heads + pl.program_id(1)
        bk = bk_ref[...]
        acc = jnp.full(bk.shape, NEG_INF, F32)
        for b in range(NUM_BUCKETS):
            acc = jnp.where(bk == b, tbl_ref[b, col], acc)
        o_ref[...] = acc

    tile = (None, ATTN_BLOCK, 2 * ATTN_BLOCK)
    bias = pl.pallas_call(
        body, name="bias_fwd", grid=(N_GROUPS_ATTN, heads),
        in_specs=[pl.BlockSpec(tile, lambda g, h: (g, 0, 0)), pl.BlockSpec(memory_space=pltpu.SMEM)],
        out_specs=pl.BlockSpec((None,) + tile, lambda g, h: (g, h, 0, 0)),
        out_shape=jax.ShapeDtypeStruct((N_GROUPS_ATTN, heads, ATTN_BLOCK, 2 * ATTN_BLOCK), F32),
        compiler_params=_params("parallel", "parallel"),
    )(bucket, rel_bias)
    return bias, bucket


def _dilated_view(qkv, g, dilation, da):
    del g
    return qkv.reshape(qkv.shape[0] // dilation, dilation * 3 * da), 3 * (da // HEAD_DIM), 0


def _heads_per_step(l, heads):
    for hps in (4, 2, 1):
        if heads % hps == 0 and l * hps <= ATTN_ROWS_TIMES_HEADS:
            return hps
    return 1


def _attn_fwd_group(qkv, bias_g, g, dilation, da, ride=None):
    s = qkv.shape[0]
    heads = da // HEAD_DIM
    l = s // dilation
    nb = l // ATTN_BLOCK
    view, cpb, base = _dilated_view(qkv, g, dilation, da)

    hps = _heads_per_step(l, heads)
    lanes = [slice(i * HEAD_DIM, (i + 1) * HEAD_DIM) for i in range(hps)]

    def body(q_ref, k_ref, v_ref, b_ref, o_ref, l_ref):
        scale = HEAD_DIM ** -0.5

        def block(rows, keys, first):
            q, k, v = q_ref[rows, :], k_ref[keys, :], v_ref[keys, :]
            bias = [b_ref[i, :, ATTN_BLOCK:2 * ATTN_BLOCK] if first else b_ref[i] for i in range(hps)]
            sc = [_dot_nt(q[:, hl], k[:, hl]) * scale + bias[i] for i, hl in enumerate(lanes)]
            mx = [jnp.max(t, axis=-1, keepdims=True) for t in sc]
            p = [jnp.exp(t - m) for t, m in zip(sc, mx)]
            den = [jnp.sum(t, axis=-1, keepdims=True) for t in p]
            for i, hl in enumerate(lanes):
                o_ref[rows, hl] = _dot((p[i] * (1.0 / den[i])).astype(BF16), v[:, hl])
                l_ref[rows, hl] = jnp.broadcast_to(mx[i] + jnp.log(den[i]), (ATTN_BLOCK, HEAD_DIM))

        first = pl.ds(0, ATTN_BLOCK)
        block(first, first, True)

        def step(j, carry):
            r0 = pl.multiple_of(j * ATTN_BLOCK, ATTN_BLOCK)
            rk = pl.multiple_of((j - 1) * ATTN_BLOCK, ATTN_BLOCK)
            block(pl.ds(r0, ATTN_BLOCK), pl.ds(rk, 2 * ATTN_BLOCK), False)
            return carry

        if nb > 1:
            lax.fori_loop(1, nb, step, 0)

    def col(t):
        return lambda r, h: (0, (r * cpb + base + t * heads) // hps + h)

    blk = (l, hps * HEAD_DIM)
    out = pl.BlockSpec(blk, lambda r, h: (0, r * (heads // hps) + h))
    shp = jax.ShapeDtypeStruct((l, dilation * da), F32)
    (o, lse), rode = _call(
        body, name=f"attn_fwd_g{g}", grid=(dilation, heads // hps),
        in_specs=[pl.BlockSpec(blk, col(0)), pl.BlockSpec(blk, col(1)), pl.BlockSpec(blk, col(2)),
                  pl.BlockSpec((hps, ATTN_BLOCK, 2 * ATTN_BLOCK), lambda r, h: (h, 0, 0))],
        out_specs=[out, out], out_shape=[shp, shp], args=(view, view, view, bias_g),
        sem=("parallel", "parallel"), ride=ride)
    return o.reshape(s, da), lse.reshape(s, da), rode


def _attn_combine(os_, ls_, gate):
    s, da = gate.shape
    tm = _pick(s, (512, 256, 128))
    tc = _pick(da, (512, 256, 128))

    assert da // HEAD_DIM <= LANES
    per_step = tc // HEAD_DIM

    def body(o0, o1, o2, l0, l1, l2, g_ref, o_ref, l_ref, y_ref):
        j = pl.program_id(1)
        a0, a1, a2 = l0[...], l1[...], l2[...]
        mx = jnp.maximum(jnp.maximum(a0, a1), a2)
        e0, e1, e2 = jnp.exp(a0 - mx), jnp.exp(a1 - mx), jnp.exp(a2 - mx)
        den = e0 + e1 + e2
        o = (e0 * o0[...] + e1 * o1[...] + e2 * o2[...]) / den
        gv = g_ref[...]
        o_ref[...] = o
        y_ref[...] = (o * (gv * _sigmoid(gv))).astype(BF16)
        lse = mx + jnp.log(den)

        @pl.when(j == 0)
        def _():
            l_ref[...] = jnp.zeros_like(l_ref)

        lane = lax.broadcasted_iota(jnp.int32, (1, LANES), 1)
        acc = l_ref[...]
        for i in range(per_step):
            acc = jnp.where(lane == j * per_step + i, lse[:, i * HEAD_DIM:(i + 1) * HEAD_DIM], acc)
        l_ref[...] = acc

    spec = pl.BlockSpec((tm, tc), lambda i, j: (i, j))
    heads_spec = pl.BlockSpec((tm, LANES), lambda i, j: (i, 0))
    return pl.pallas_call(
        body, name="attn_combine", grid=(s // tm, da // tc), in_specs=[spec] * 7, out_specs=(spec, heads_spec, spec),
        out_shape=(jax.ShapeDtypeStruct((s, da), F32), jax.ShapeDtypeStruct((s, LANES), F32),
                   jax.ShapeDtypeStruct((s, da), BF16)),
        compiler_params=_params("parallel", "arbitrary"),
    )(*os_, *ls_, gate)


def _attn_bwd_prep(dy, o, gate):
    s, da = gate.shape
    tm = _pick(s, (512, 256, 128))

    def body(dy_ref, o_ref, g_ref, do_ref, dg_ref, dd_ref):
        j = pl.program_id(1)
        gv = g_ref[...]
        sg = _sigmoid(gv)
        dyv = dy_ref[...]
        ov = o_ref[...]
        do = dyv * (gv * sg)
        do_ref[...] = do.astype(BF16)
        dg_ref[...] = (dyv * ov * (sg * (1.0 + gv * (1.0 - sg)))).astype(BF16)

        @pl.when(j == 0)
        def _():
            dd_ref[...] = jnp.zeros_like(dd_ref)

        lane = lax.broadcasted_iota(jnp.int32, (1, LANES), 1)
        dd_ref[...] = jnp.where(lane == j, jnp.sum(do * ov, axis=-1, keepdims=True), dd_ref[...])

    spec = pl.BlockSpec((tm, HEAD_DIM), lambda i, j: (i, j))
    return pl.pallas_call(
        body, name="attn_bwd_prep", grid=(s // tm, da // HEAD_DIM), in_specs=[spec] * 3,
        out_specs=(spec, spec, pl.BlockSpec((tm, LANES), lambda i, j: (i, 0))),
        out_shape=(jax.ShapeDtypeStruct((s, da), BF16), jax.ShapeDtypeStruct((s, da), BF16),
                   jax.ShapeDtypeStruct((s, LANES), F32)),
        compiler_params=_params("parallel", "arbitrary"),
    )(dy, o, gate)


def _attn_bwd_group(qkv, do, lse, dd, bias_g, g, dilation, da, ride=None):
    s = qkv.shape[0]
    heads = da // HEAD_DIM
    l = s // dilation
    nb = l // ATTN_BLOCK
    view, cpb, base = _dilated_view(qkv, g, dilation, da)
    scale = HEAD_DIM ** -0.5
    hps = _heads_per_step(l, heads)
    lanes = [slice(i * HEAD_DIM, (i + 1) * HEAD_DIM) for i in range(hps)]

    def body(q_ref, k_ref, v_ref, do_ref, l_ref, dd_ref, b_ref, dq_ref, dk_ref, dv_ref, ds_ref, dk_acc, dv_acc):
        h0 = pl.program_id(0) * hps
        r = pl.program_id(1)
        lane = lax.broadcasted_iota(jnp.int32, (1, LANES), 1)

        @pl.when(r == 0)
        def _():
            ds_ref[...] = jnp.zeros_like(ds_ref)

        dk_acc[...] = jnp.zeros_like(dk_acc)
        dv_acc[...] = jnp.zeros_like(dv_acc)

        def block(rows, keys, first):
            q, k, v, dov = q_ref[rows, :], k_ref[keys, :], v_ref[keys, :], do_ref[rows, :]
            lse_all, dd_all = l_ref[rows, :], dd_ref[rows, :]
            pick = [(lane == h0 + i).astype(F32) for i in range(hps)]
            lrow = [jnp.sum(lse_all * m, axis=-1, keepdims=True) for m in pick]
            drow = [jnp.sum(dd_all * m, axis=-1, keepdims=True) for m in pick]
            bias = [b_ref[i, :, ATTN_BLOCK:2 * ATTN_BLOCK] if first else b_ref[i] for i in range(hps)]
            sc = [_dot_nt(q[:, hl], k[:, hl]) for hl in lanes]
            dp = [_dot_nt(dov[:, hl], v[:, hl]) for hl in lanes]
            p = [jnp.exp(sc[i] * scale + bias[i] - lrow[i]) for i in range(hps)]
            ds = [p[i] * (dp[i] - drow[i]) for i in range(hps)]
            dsb = [t.astype(BF16) for t in ds]
            pb = [t.astype(BF16) for t in p]
            for i, hl in enumerate(lanes):
                dq_ref[rows, hl] = (_dot(dsb[i], k[:, hl]) * scale).astype(BF16)
                dk_acc[keys, hl] += _dot_tn(dsb[i], q[:, hl]) * scale
                dv_acc[keys, hl] += _dot_tn(pb[i], dov[:, hl])
                if first:
                    ds_ref[i, :, ATTN_BLOCK:2 * ATTN_BLOCK] += ds[i]
                else:
                    ds_ref[i] += ds[i]

        first = pl.ds(0, ATTN_BLOCK)
        block(first, first, True)

        def step(j, carry):
            r0 = pl.multiple_of(j * ATTN_BLOCK, ATTN_BLOCK)
            rk = pl.multiple_of((j - 1) * ATTN_BLOCK, ATTN_BLOCK)
            block(pl.ds(r0, ATTN_BLOCK), pl.ds(rk, 2 * ATTN_BLOCK), False)
            return carry

        if nb > 1:
            lax.fori_loop(1, nb, step, 0)
        dk_ref[...] = dk_acc[...].astype(BF16)
        dv_ref[...] = dv_acc[...].astype(BF16)

    def col(t):
        return lambda h, r: (0, (r * cpb + base + t * heads) // hps + h)

    blk = (l, hps * HEAD_DIM)
    act = pl.BlockSpec(blk, lambda h, r: (0, r * (heads // hps) + h))
    per_head = pl.BlockSpec((l, LANES), lambda h, r: (0, r))
    tile = pl.BlockSpec((hps, ATTN_BLOCK, 2 * ATTN_BLOCK), lambda h, r: (h, 0, 0))
    shp = jax.ShapeDtypeStruct((l, dilation * da), BF16)
    (dq, dk, dv, ds), rode = _call(
        body, name=f"attn_bwd_g{g}", grid=(heads // hps, dilation),
        in_specs=[pl.BlockSpec(blk, col(0)), pl.BlockSpec(blk, col(1)), pl.BlockSpec(blk, col(2)), act,
                  per_head, per_head, tile],
        out_specs=[act, act, act, tile],
        out_shape=[shp, shp, shp, jax.ShapeDtypeStruct((heads, ATTN_BLOCK, 2 * ATTN_BLOCK), F32)],
        scratch=[pltpu.VMEM(blk, F32), pltpu.VMEM(blk, F32)], sem=("parallel", "arbitrary"), ride=ride,
        args=(view, view, view, do.reshape(l, dilation * da), lse.reshape(l, dilation * LANES),
              dd.reshape(l, dilation * LANES), bias_g))
    return (dq.reshape(s, da), dk.reshape(s, da), dv.reshape(s, da), ds), rode


def _bias_bwd(ds, bucket):
    ng, heads = ds.shape[0], ds.shape[1]

    def body(ds_ref, bk_ref, o_ref):
        bk = bk_ref[...]
        x = ds_ref[...]
        for b in range(NUM_BUCKETS):
            o_ref[:, b:b + 1] = jnp.sum(jnp.where(bk == b, x, 0.0), axis=(0, 1), keepdims=True)

    tile = (None, ATTN_BLOCK, 2 * ATTN_BLOCK)
    out = pl.pallas_call(
        body, name="bias_bwd", grid=(ng, heads),
        in_specs=[pl.BlockSpec((None,) + tile, lambda g, h: (g, h, 0, 0)), pl.BlockSpec(tile, lambda g, h: (g, 0, 0))],
        out_specs=pl.BlockSpec((None, None, 1, NUM_BUCKETS), lambda g, h: (g, h, 0, 0)),
        out_shape=jax.ShapeDtypeStruct((ng, heads, 1, NUM_BUCKETS), F32),
        compiler_params=_params("parallel", "parallel"),
    )(ds, bucket)
    return out.reshape(ng, heads, NUM_BUCKETS)


def _shift_rows(x, halo, s):
    r = pltpu.roll(x, s, axis=0)
    rh = pltpu.roll(halo, s, axis=0)
    row = lax.broadcasted_iota(jnp.int32, halo.shape, 0)
    top = jnp.where(row < s, rh, r[0:SUBLANES])
    return jnp.concatenate([top, r[SUBLANES:]], axis=0)


def _conv_out(x, halo, w, b):
    acc = b + w[CONV_WIDTH - 1:CONV_WIDTH] * x
    for kk in range(CONV_WIDTH - 1):
        acc = acc + w[kk:kk + 1] * _shift_rows(x, halo, CONV_WIDTH - 1 - kk)
    return acc


def _conv_fwd(proj, conv_w, conv_b, col0):
    s = proj.shape[0]
    c = conv_w.shape[1]
    ts = _pick(s, (512, 256, 128))
    tc = _pick(math.gcd(c, col0), (512, 256, 128))
    cb0 = col0 // tc
    hb = ts // SUBLANES

    def body(x_ref, h_ref, w_ref, b_ref, o_ref):
        i = pl.program_id(0)
        halo = jnp.where(i > 0, h_ref[...], 0.0)
        u = _conv_out(x_ref[...], halo, w_ref[...], b_ref[...])
        o_ref[...] = u * _sigmoid(u)

    return pl.pallas_call(
        body, name="conv_fwd", grid=(s // ts, c // tc),
        in_specs=[pl.BlockSpec((ts, tc), lambda i, j: (i, cb0 + j)),
                  pl.BlockSpec((SUBLANES, tc), lambda i, j: (jnp.maximum(i * hb - 1, 0), cb0 + j)),
                  pl.BlockSpec((CONV_WIDTH, tc), lambda i, j: (0, j)),
                  pl.BlockSpec((1, tc), lambda i, j: (0, j))],
        out_specs=pl.BlockSpec((ts, tc), lambda i, j: (i, j)),
        out_shape=jax.ShapeDtypeStruct((s, c), F32),
        compiler_params=_params("parallel", "parallel"),
    )(proj, proj, conv_w, conv_b)


def _conv_bwd(proj, conv_w, conv_b, dacts, col0, dproj, dproj_t):
    s = proj.shape[0]
    c = conv_w.shape[1]
    widths = [d.shape[1] for d in dacts]
    assert sum(widths) == c
    ts = _pick(s, (512, 256, 128))
    tc = _pick(math.gcd(math.gcd(c, col0), math.gcd(*widths)), (512, 256, 128))
    cb0 = col0 // tc
    hb = ts // SUBLANES
    nblk = s // ts
    ext = ts + SUBLANES
    nb = [wd // tc for wd in widths]
    starts = [0, nb[0], nb[0] + nb[1]]

    def body(x_ref, xp_ref, xn_ref, d0, d1, d2, n0, n1, n2, w_ref, b_ref, _, __, dx_ref, dw_ref, db_ref, dxt_ref):
        j = pl.program_id(0)
        i = pl.program_id(1)
        last = i == nblk - 1
        w = w_ref[...]
        halo = jnp.where(i > 0, xp_ref[...], 0.0)
        x = x_ref[...]
        xe = jnp.concatenate([x, xn_ref[...]], axis=0)
        dcur = jnp.where(j < starts[1], d0[...], jnp.where(j < starts[2], d1[...], d2[...]))
        dnext = jnp.where(j < starts[1], n0[...], jnp.where(j < starts[2], n1[...], n2[...]))
        de = jnp.concatenate([dcur, jnp.where(last, 0.0, dnext)], axis=0)
        u = _conv_out(xe, halo, w, b_ref[...])
        sg = _sigmoid(u)
        dpre = de * (sg * (1.0 + u * (1.0 - sg)))
        dx = w[CONV_WIDTH - 1:CONV_WIDTH] * dpre[0:ts]
        for kk in range(CONV_WIDTH - 1):
            sh = CONV_WIDTH - 1 - kk
            dx = dx + w[kk:kk + 1] * pltpu.roll(dpre, ext - sh, axis=0)[0:ts]
        dx_ref[...] = dx.astype(BF16)
        dxt_ref[...] = dx.T.astype(BF16)
        dcur = dpre[0:ts]

        @pl.when(i == 0)
        def _():
            dw_ref[...] = jnp.zeros_like(dw_ref)
            db_ref[...] = jnp.zeros_like(db_ref)

        db_ref[...] += jnp.sum(dcur, axis=0, keepdims=True)
        dw_ref[CONV_WIDTH - 1:CONV_WIDTH, :] += jnp.sum(dcur * x, axis=0, keepdims=True)
        for kk in range(CONV_WIDTH - 1):
            xs = _shift_rows(x, halo, CONV_WIDTH - 1 - kk)
            dw_ref[kk:kk + 1, :] += jnp.sum(dcur * xs, axis=0, keepdims=True)

    cur_p = pl.BlockSpec((ts, tc), lambda j, i: (i, cb0 + j))
    prev_p = pl.BlockSpec((SUBLANES, tc), lambda j, i: (jnp.maximum(i * hb - 1, 0), cb0 + j))
    nxt = lambda i: jnp.minimum((i + 1) * hb, nblk * hb - 1)
    next_p = pl.BlockSpec((SUBLANES, tc), lambda j, i: (nxt(i), cb0 + j))

    def part(q):
        return lambda j: jnp.clip(j - starts[q], 0, nb[q] - 1)

    cur_d = [pl.BlockSpec((ts, tc), lambda j, i, f=part(q): (i, f(j))) for q in range(3)]
    next_d = [pl.BlockSpec((SUBLANES, tc), lambda j, i, f=part(q): (nxt(i), f(j))) for q in range(3)]
    vec4 = pl.BlockSpec((CONV_WIDTH, tc), lambda j, i: (0, j))
    vec1 = pl.BlockSpec((1, tc), lambda j, i: (0, j))
    hbm = pl.BlockSpec(memory_space=pl.ANY)
    return pl.pallas_call(
        body, name="conv_bwd", grid=(c // tc, nblk),
        in_specs=[cur_p, prev_p, next_p, *cur_d, *next_d, vec4, vec1, hbm, hbm],
        out_specs=(cur_p, vec4, vec1, pl.BlockSpec((tc, ts), lambda j, i: (cb0 + j, i))),
        out_shape=(jax.ShapeDtypeStruct(dproj.shape, dproj.dtype), jax.ShapeDtypeStruct((CONV_WIDTH, c), F32),
                   jax.ShapeDtypeStruct((1, c), F32), jax.ShapeDtypeStruct(dproj_t.shape, dproj_t.dtype)),
        input_output_aliases={11: 0, 12: 3},
        compiler_params=_params("parallel", "arbitrary"),
    )(proj, proj, proj, *dacts, *dacts, conv_w, conv_b, dproj, dproj_t)


def _dt_fwd(proj, dt_bias, col0):
    s = proj.shape[0]
    h = dt_bias.shape[1]
    ts = _pick(s, (1024, 512, 256, 128))

    def body(x_ref, b_ref, o_ref):
        v = x_ref[...] + b_ref[...]
        o_ref[...] = jnp.maximum(v, 0.0) + jnp.log1p(jnp.exp(-jnp.abs(v)))

    return pl.pallas_call(
        body, name="dt_fwd", grid=(s // ts,),
        in_specs=[pl.BlockSpec((ts, h), lambda i: (i, col0 // h)), pl.BlockSpec((1, h), lambda i: (0, 0))],
        out_specs=pl.BlockSpec((ts, h), lambda i: (i, 0)), out_shape=jax.ShapeDtypeStruct((s, h), F32),
        compiler_params=_params("parallel"),
    )(proj, dt_bias)


def _dt_bwd(proj, dt_bias, ddt, col0, dproj, dproj_t):
    s = proj.shape[0]
    h = dt_bias.shape[1]
    ts = _pick(s, (1024, 512, 256, 128))

    def body(x_ref, b_ref, d_ref, _, __, o_ref, db_ref, ot_ref):
        i = pl.program_id(0)
        draw = d_ref[...] * _sigmoid(x_ref[...] + b_ref[...])
        o_ref[...] = draw.astype(BF16)
        ot_ref[...] = draw.T.astype(BF16)

        @pl.when(i == 0)
        def _():
            db_ref[...] = jnp.zeros_like(db_ref)

        db_ref[...] += jnp.sum(draw, axis=0, keepdims=True)

    hbm = pl.BlockSpec(memory_space=pl.ANY)
    return pl.pallas_call(
        body, name="dt_bwd", grid=(s // ts,),
        in_specs=[pl.BlockSpec((ts, h), lambda i: (i, col0 // h)), pl.BlockSpec((1, h), lambda i: (0, 0)),
                  pl.BlockSpec((ts, h), lambda i: (i, 0)), hbm, hbm],
        out_specs=(pl.BlockSpec((ts, h), lambda i: (i, col0 // h)), pl.BlockSpec((1, h), lambda i: (0, 0)),
                   pl.BlockSpec((h, ts), lambda i: (col0 // h, i))),
        out_shape=(jax.ShapeDtypeStruct(dproj.shape, dproj.dtype), jax.ShapeDtypeStruct((1, h), F32),
                   jax.ShapeDtypeStruct(dproj_t.shape, dproj_t.dtype)),
        input_output_aliases={3: 0, 4: 2},
        compiler_params=_params("arbitrary"),
    )(proj, dt_bias, ddt, dproj, dproj_t)


def _chunk_terms(dt, dt_t, a, a_t):
    li = lax.broadcasted_iota(jnp.int32, (CHUNK, CHUNK), 0)
    si = lax.broadcasted_iota(jnp.int32, (CHUNK, CHUNK), 1)
    lower = (li >= si).astype(F32)
    upper = (li <= si).astype(F32)
    acum = jnp.dot(lower, dt * a, preferred_element_type=F32, precision=HIGHEST)
    acum_t = jnp.dot(dt_t * a_t, upper, preferred_element_type=F32, precision=HIGHEST)
    return acum, acum_t, li, si, upper


def _dot_exact01(t, m01):
    r = t.shape[0]
    hi = t.astype(BF16)
    rest = t - hi.astype(F32)
    mid = rest.astype(BF16)
    lo = (rest - mid.astype(F32)).astype(BF16)
    out = _dot(jnp.concatenate([hi, mid, lo], axis=0), m01.astype(BF16))
    return out[0:r] + out[r:2 * r] + out[2 * r:3 * r]


def _head_lanes(dt, acum, gw):
    hpg = dt.shape[1]
    p = gw // hpg
    spread = (lax.broadcasted_iota(jnp.int32, (hpg, gw), 1) // p
              == lax.broadcasted_iota(jnp.int32, (hpg, gw), 0)).astype(F32)
    both = _dot_exact01(jnp.concatenate([dt, acum], axis=0), spread)
    dt_e, acum_e = both[0:CHUNK], both[CHUNK:2 * CHUNK]
    alast_e = acum_e[CHUNK - 1:CHUNK, :]
    return dt_e, jnp.exp(acum_e), jnp.exp(alast_e - acum_e), jnp.exp(alast_e)


def _fold_heads(t, hpg):
    gw = t.shape[1]
    p = gw // hpg
    fold = (lax.broadcasted_iota(jnp.int32, (gw, hpg), 0) // p
            == lax.broadcasted_iota(jnp.int32, (gw, hpg), 1)).astype(F32)
    return _dot_exact01(t, fold)


def _ssd_fwd(xbc, proj, dt_g, dt_gt, a_g, a_gt, dskip_e, norm_w, d_inner, n_state):
    s = xbc.shape[0]
    hpg = dt_g.shape[2]
    gw = d_inner // SSM_GROUPS
    p = gw // hpg
    nc = s // CHUNK
    n = n_state
    b0 = d_inner // n
    c0 = b0 + SSM_GROUPS
    per_tile = LANES // p

    def body(xs_ref, b_ref, c_ref, dt_ref, dtt_ref, a_ref, at_ref, z_ref, dsk_ref, nw_ref,
             yn_ref, y_ref, st_ref, state):
        c = pl.program_id(1)

        @pl.when(c == 0)
        def _():
            state[...] = jnp.zeros_like(state)

        st = state[...]
        st_ref[...] = st
        xs = xs_ref[...]
        bm = b_ref[...].astype(BF16)
        cm = c_ref[...].astype(BF16)
        dt = dt_ref[...]
        acum, acum_t, li, si, _ = _chunk_terms(dt, dtt_ref[...], a_ref[...], at_ref[...])
        dt_e, e_a, t_e, e_last = _head_lanes(dt, acum, gw)
        xdt = xs * dt_e
        xdtb = xdt.astype(BF16)
        cb = _dot_nt(cm, bm)
        causal = li >= si
        lane = lax.broadcasted_iota(jnp.int32, (1, LANES), 1)
        y_ref[...] = _dot(cm, st.astype(BF16)) * e_a
        for q in range(gw // LANES):
            ql = slice(q * LANES, (q + 1) * LANES)
            xq = xdtb[:, ql]
            ms = []
            for i in range(per_tile):
                h = q * per_tile + i
                decay = jnp.exp(jnp.where(causal, acum[:, h:h + 1] - acum_t[h:h + 1, :], NEG_INF))
                ms.append((cb * decay).astype(BF16))
            y_all = _dot(jnp.concatenate(ms, axis=0), xq)
            yd = y_all[0:CHUNK]
            for i in range(1, per_tile):
                yd = jnp.where(lane >= i * p, y_all[i * CHUNK:(i + 1) * CHUNK], yd)
            y_ref[:, ql] += yd
        state[...] = st * e_last + _dot_tn(bm, (xdt * t_e).astype(BF16))
        yt = y_ref[...] + xs * dsk_ref[...]
        z = z_ref[...]
        yz = yt * (z * _sigmoid(z))
        r = lax.rsqrt(jnp.mean(yz * yz, axis=-1, keepdims=True) + RMS_EPS)
        yn_ref[...] = (yz * r * nw_ref[...]).astype(BF16)

    wide = pl.BlockSpec((CHUNK, gw), lambda g, c: (c, g))
    return pl.pallas_call(
        body, name="ssd_fwd", grid=(SSM_GROUPS, nc),
        in_specs=[wide,
                  pl.BlockSpec((CHUNK, n), lambda g, c: (c, b0 + g)),
                  pl.BlockSpec((CHUNK, n), lambda g, c: (c, c0 + g)),
                  pl.BlockSpec((None, CHUNK, hpg), lambda g, c: (g, c, 0)),
                  pl.BlockSpec((None, hpg, CHUNK), lambda g, c: (g, 0, c)),
                  pl.BlockSpec((None, 1, hpg), lambda g, c: (g, 0, 0)),
                  pl.BlockSpec((None, hpg, 1), lambda g, c: (g, 0, 0)),
                  wide,
                  pl.BlockSpec((None, 1, gw), lambda g, c: (g, 0, 0)),
                  pl.BlockSpec((1, gw), lambda g, c: (0, g))],
        out_specs=(wide, wide, pl.BlockSpec((None, None, n, gw), lambda g, c: (g, c, 0, 0))),
        out_shape=(jax.ShapeDtypeStruct((s, d_inner), BF16), jax.ShapeDtypeStruct((s, d_inner), F32),
                   jax.ShapeDtypeStruct((SSM_GROUPS, nc, n, gw), F32)),
        scratch_shapes=[pltpu.VMEM((n, gw), F32)],
        compiler_params=_params("parallel", "arbitrary"),
    )(xbc, xbc, xbc, dt_g, dt_gt, a_g, a_gt, proj, dskip_e, norm_w)


def _ssd_epilogue_bwd(dyn, y, xbc, proj, dskip_e, norm_w, hpg):
    s, d_inner = dyn.shape
    gw = d_inner // SSM_GROUPS
    p = gw // hpg
    nc = s // CHUNK

    def body(dyn_ref, y_ref, xs_ref, z_ref, dsk_ref, nw_ref, dy_ref, dz_ref, dnw_ref, ddsk_ref, dzt_ref):
        c = pl.program_id(1)
        xs = xs_ref[...]
        z = z_ref[...]
        yt = y_ref[...] + xs * dsk_ref[...]
        sg = _sigmoid(z)
        sz = z * sg
        yz = yt * sz
        r = lax.rsqrt(jnp.mean(yz * yz, axis=-1, keepdims=True) + RMS_EPS)
        dynv = dyn_ref[...]
        dyh = dynv * nw_ref[...]
        dyz = r * (dyh - yz * (r * r) * jnp.mean(dyh * yz, axis=-1, keepdims=True))
        dyt = dyz * sz
        dy_ref[...] = dyt
        dz = dyz * yt * (sg * (1.0 + z * (1.0 - sg)))
        dz_ref[...] = dz.astype(BF16)
        dzt_ref[...] = dz.T.astype(BF16)

        @pl.when(c == 0)
        def _():
            dnw_ref[...] = jnp.zeros_like(dnw_ref)
            ddsk_ref[...] = jnp.zeros_like(ddsk_ref)

        dnw_ref[...] += jnp.sum(dynv * yz * r, axis=0, keepdims=True)
        colsum = jnp.sum(dyt * xs, axis=0, keepdims=True)
        fold = (lax.broadcasted_iota(jnp.int32, (gw, hpg), 0) // p
                == lax.broadcasted_iota(jnp.int32, (gw, hpg), 1)).astype(F32)
        ddsk_ref[...] += jnp.dot(colsum, fold, preferred_element_type=F32, precision=HIGHEST)

    wide = pl.BlockSpec((CHUNK, gw), lambda g, c: (c, g))
    return pl.pallas_call(
        body, name="ssd_epilogue_bwd", grid=(SSM_GROUPS, nc),
        in_specs=[wide, wide, wide, wide, pl.BlockSpec((None, 1, gw), lambda g, c: (g, 0, 0)),
                  pl.BlockSpec((1, gw), lambda g, c: (0, g))],
        out_specs=(wide, wide, pl.BlockSpec((1, gw), lambda g, c: (0, g)),
                   pl.BlockSpec((None, 1, hpg), lambda g, c: (g, 0, 0)),
                   pl.BlockSpec((gw, CHUNK), lambda g, c: (g, c))),
        out_shape=(jax.ShapeDtypeStruct((s, d_inner), F32), jax.ShapeDtypeStruct((s, proj.shape[1]), BF16),
                   jax.ShapeDtypeStruct((1, d_inner), F32), jax.ShapeDtypeStruct((SSM_GROUPS, 1, hpg), F32),
                   jax.ShapeDtypeStruct((proj.shape[1], s), BF16)),
        compiler_params=_params("parallel", "arbitrary"),
    )(dyn, y, xbc, proj, dskip_e, norm_w)


def _ssd_scan_bwd(xbc, dt_g, dt_gt, a_g, a_gt, states, dy, dskip_e, d_inner, n_state, ride=None):
    s = xbc.shape[0]
    hpg = dt_g.shape[2]
    gw = d_inner // SSM_GROUPS
    p = gw // hpg
    nc = s // CHUNK
    n = n_state
    b0 = d_inner // n
    c0 = b0 + SSM_GROUPS
    per_tile = LANES // p

    def body(xs_ref, b_ref, c_ref, dt_ref, dtt_ref, a_ref, at_ref, st_ref, dy_ref, dsk_ref,
             dxs_ref, db_ref, dc_ref, ddt_ref, da_ref, dstate, ydiag_ref, dxd_ref):
        c = pl.program_id(1)

        @pl.when(c == 0)
        def _():
            dstate[...] = jnp.zeros_like(dstate)
            da_ref[...] = jnp.zeros_like(da_ref)

        xs = xs_ref[...]
        bm = b_ref[...].astype(BF16)
        cm = c_ref[...].astype(BF16)
        dt = dt_ref[...]
        a = a_ref[...]
        dyv = dy_ref[...]
        dsk = dsk_ref[...]
        acum, acum_t, li, si, upper = _chunk_terms(dt, dtt_ref[...], a, at_ref[...])
        dt_e, e_a, t_e, e_last = _head_lanes(dt, acum, gw)
        cb = _dot_nt(cm, bm)
        lower_mask = li >= si
        lane = lax.broadcasted_iota(jnp.int32, (1, LANES), 1)
        row_l = lax.broadcasted_iota(jnp.int32, (CHUNK, 1), 0)
        st = st_ref[...]
        stb = st.astype(BF16)
        dst = dstate[...]
        dstb = dst.astype(BF16)
        xdt = xs * dt_e
        xdtb = xdt.astype(BF16)
        dyb = dyv.astype(BF16)
        dye = dyv * e_a
        dyeb = dye.astype(BF16)
        xte = xdt * t_e
        xteb = xte.astype(BF16)
        wv = _dot(bm, dstb)
        yo = _dot(cm, stb)
        dcb = jnp.zeros((CHUNK, CHUNK), F32)
        for q in range(gw // LANES):
            ql = slice(q * LANES, (q + 1) * LANES)
            xq = xdtb[:, ql]
            dq = dyb[:, ql]
            decays, ms, mts, dqs = [], [], [], []
            for i in range(per_tile):
                h = q * per_tile + i
                decay = jnp.exp(jnp.where(lower_mask, acum[:, h:h + 1] - acum_t[h:h + 1, :], NEG_INF))
                mm = cb * decay
                mine = (lane >= i * p) & (lane < (i + 1) * p)
                decays.append(decay)
                ms.append(mm.astype(BF16))
                mts.append(mm.T.astype(BF16))
                dqs.append(jnp.where(mine, dq, jnp.zeros_like(dq)))
            dm_all = _dot_nt(jnp.concatenate(dqs, axis=0), xq)
            y_all = _dot(jnp.concatenate(ms, axis=0), xq)
            d_all = _dot(jnp.concatenate(mts, axis=0), dq)
            yd = dd = None
            for i in range(per_tile):
                rows = slice(i * CHUNK, (i + 1) * CHUNK)
                dcb = dcb + dm_all[rows] * decays[i]
                yd = y_all[rows] if i == 0 else jnp.where(lane >= i * p, y_all[rows], yd)
                dd = d_all[rows] if i == 0 else jnp.where(lane >= i * p, d_all[rows], dd)
            ydiag_ref[:, ql] = yd
            dxd_ref[:, ql] = dd
        ydiag = ydiag_ref[...]
        dxd = dxd_ref[...]
        dxdt = dxd + t_e * wv
        xw = xte * wv
        last_in = jnp.sum(xw, axis=0, keepdims=True) + e_last * jnp.sum(dst * st, axis=0, keepdims=True)
        folded = _fold_heads(jnp.concatenate(
            [dyb.astype(F32) * ydiag - xdtb.astype(F32) * dxd - xw + dye * yo, dxdt * xs,
             jnp.broadcast_to(last_in, (SUBLANES, gw))],
            axis=0), hpg)
        dalast = folded[2 * CHUNK:2 * CHUNK + 1]
        d_acum = folded[0:CHUNK] + jnp.where(row_l == CHUNK - 1, dalast, 0.0)
        ddt_x = folded[CHUNK:2 * CHUNK]
        dxs_ref[...] = dxdt * dt_e + dyv * dsk
        dbf = dcb.astype(BF16)
        dc_ref[...] = _dot_nt(dyeb, stb) + _dot(dbf, bm)
        db_ref[...] = _dot_nt(xteb, dstb) + _dot_tn(dbf, cm)
        dstate[...] = dst * e_last + _dot_tn(cm, dyeb)
        d_da = jnp.dot(upper, d_acum, preferred_element_type=F32, precision=HIGHEST)
        ddt_ref[...] = d_da * a + ddt_x
        da_ref[...] += jnp.sum(d_da * dt, axis=0, keepdims=True)

    rev = lambda c: nc - 1 - c
    wide = pl.BlockSpec((CHUNK, gw), lambda g, c: (rev(c), g))
    return _call(
        body, name="ssd_scan_bwd", grid=(SSM_GROUPS, nc),
        in_specs=[wide,
                  pl.BlockSpec((CHUNK, n), lambda g, c: (rev(c), b0 + g)),
                  pl.BlockSpec((CHUNK, n), lambda g, c: (rev(c), c0 + g)),
                  pl.BlockSpec((None, CHUNK, hpg), lambda g, c: (g, rev(c), 0)),
                  pl.BlockSpec((None, hpg, CHUNK), lambda g, c: (g, 0, rev(c))),
                  pl.BlockSpec((None, 1, hpg), lambda g, c: (g, 0, 0)),
                  pl.BlockSpec((None, hpg, 1), lambda g, c: (g, 0, 0)),
                  pl.BlockSpec((None, None, n, gw), lambda g, c: (g, rev(c), 0, 0)),
                  wide,
                  pl.BlockSpec((None, 1, gw), lambda g, c: (g, 0, 0))],
        out_specs=[wide,
                   pl.BlockSpec((CHUNK, n), lambda g, c: (rev(c), g)),
                   pl.BlockSpec((CHUNK, n), lambda g, c: (rev(c), g)),
                   pl.BlockSpec((None, CHUNK, hpg), lambda g, c: (g, rev(c), 0)),
                   pl.BlockSpec((None, 1, hpg), lambda g, c: (g, 0, 0))],
        out_shape=[jax.ShapeDtypeStruct((s, d_inner), F32),
                   jax.ShapeDtypeStruct((s, SSM_GROUPS * n), F32), jax.ShapeDtypeStruct((s, SSM_GROUPS * n), F32),
                   jax.ShapeDtypeStruct((SSM_GROUPS, s, hpg), F32), jax.ShapeDtypeStruct((SSM_GROUPS, 1, hpg), F32)],
        scratch=[pltpu.VMEM((n, gw), F32), pltpu.VMEM((CHUNK, gw), F32), pltpu.VMEM((CHUNK, gw), F32)],
        sem=("parallel", "arbitrary"), ride=ride,
        args=(xbc, xbc, xbc, dt_g, dt_gt, a_g, a_gt, states, dy, dskip_e))


def _lin(p):
    return 4 * p[0] + 2 * p[1] + p[2]


class _Gather:
    def __init__(self, arrs):
        self.arrs = list(arrs)

    def out_shape(self):
        return [jax.ShapeDtypeStruct((NDEV,) + a.shape, a.dtype) for a in self.arrs]

    def _copies(self, ins, outs, sems):
        send_sems, recv_sems, local_sems = sems
        x, y, c = lax.axis_index("x"), lax.axis_index("y"), lax.axis_index("c")
        me, sibling = (x, y, c), (x, y, 1 - c)
        chips = [(1 - x, y), (x, 1 - y), (1 - x, 1 - y)]

        def copy(a, k, block, to, src=None):
            rows = outs[a].at[_lin(block)]
            return pltpu.make_async_remote_copy(
                src_ref=rows if src is None else src, dst_ref=rows,
                send_sem=send_sems.at[a * NPEER + k], recv_sem=recv_sems.at[a * NPEER + k],
                device_id=to, device_id_type=pl.DeviceIdType.MESH)

        na = len(ins)
        mine = [pltpu.make_async_copy(ins[a], outs[a].at[_lin(me)], local_sems.at[a]) for a in range(na)]
        first = []
        for a in range(na):
            first.append(copy(a, 0, me, sibling, src=ins[a]))
            first += [copy(a, 1 + j, me, (*chip, c), src=ins[a]) for j, chip in enumerate(chips)]
        return copy, mine, first, me, sibling, chips, c, na

    def start(self, ins, outs, sems):
        _, mine, first, *_ = self._copies(ins, outs, sems)
        for cp in mine + first:
            cp.start()

    def finish(self, ins, outs, sems):
        copy, mine, first, me, sibling, chips, c, na = self._copies(ins, outs, sems)
        passed = []
        for j, chip in enumerate(chips):
            for a in range(na):
                copy(a, 1 + j, (*chip, c), me).wait_recv()
                cp = copy(a, 4 + j, (*chip, c), sibling)
                cp.start()
                passed.append(cp)
        for a in range(na):
            copy(a, 0, sibling, me).wait_recv()
            for j, chip in enumerate(chips):
                copy(a, 4 + j, (*chip, 1 - c), me).wait_recv()
        for cp in first + passed:
            cp.wait_send()
        for cp in mine:
            cp.wait()


class _Scatter:
    def __init__(self, arrs, ks=tuple(range(NDEV))):
        self.arrs = list(arrs)
        self.ks = [tuple(k) for k in ks] if isinstance(ks[0], (tuple, list)) else [tuple(ks)] * len(self.arrs)
        assert len(self.ks) == len(self.arrs)

    def out_shape(self):
        return [jax.ShapeDtypeStruct((len(k),) + a.shape[1:], a.dtype) for a, k in zip(self.arrs, self.ks)]

    def _copies(self, ins, outs, sems):
        send_sems, recv_sems, local_sems = sems
        x, y, c = lax.axis_index("x"), lax.axis_index("y"), lax.axis_index("c")
        me = (x, y, c)

        def peer(k):
            return (1 - x if k & 4 else x, 1 - y if k & 2 else y, 1 - c if k & 1 else c)

        local, remote = [], []
        for a in range(len(ins)):
            for i, k in enumerate(self.ks[a]):
                if k == 0:
                    local.append(pltpu.make_async_copy(ins[a].at[_lin(me)], outs[a].at[i], local_sems.at[a]))
                else:
                    remote.append(pltpu.make_async_remote_copy(
                        src_ref=ins[a].at[_lin(peer(k))], dst_ref=outs[a].at[i],
                        send_sem=send_sems.at[a * NPEER + k - 1], recv_sem=recv_sems.at[a * NPEER + k - 1],
                        device_id=peer(k), device_id_type=pl.DeviceIdType.MESH))
        return local, remote

    def start(self, ins, outs, sems):
        local, remote = self._copies(ins, outs, sems)
        for cp in local + remote:
            cp.start()

    def finish(self, ins, outs, sems):
        local, remote = self._copies(ins, outs, sems)
        for cp in remote:
            cp.wait_recv()
        for cp in remote:
            cp.wait_send()
        for cp in local:
            cp.wait()


def _exchange_scratch(na):
    return [pltpu.SemaphoreType.DMA((na * NPEER,)), pltpu.SemaphoreType.DMA((na * NPEER,)),
            pltpu.SemaphoreType.DMA((na,))]


def _exchange_alone(ex, *, name, in_vmem=False):
    na = len(ex.arrs)

    def body(*refs):
        ins, outs, sems = refs[:na], refs[na:2 * na], refs[2 * na:]
        ex.start(ins, outs, sems)
        ex.finish(ins, outs, sems)

    spec = pl.BlockSpec(memory_space=pltpu.VMEM if in_vmem else pl.ANY)
    return pl.pallas_call(
        body, name=name, out_shape=tuple(ex.out_shape()), in_specs=[spec] * na, out_specs=tuple([spec] * na),
        scratch_shapes=_exchange_scratch(na),
        compiler_params=pltpu.CompilerParams(vmem_limit_bytes=VMEM_LIMIT),
    )(*ex.arrs)


def _call(body, *, name, grid, in_specs, out_specs, out_shape, args, sem, scratch=(), ride=None, aliases=None):
    n_in, n_out, n_scr = len(in_specs), len(out_specs), len(scratch)
    if ride is None:
        outs = pl.pallas_call(
            body, name=name, grid=grid, in_specs=list(in_specs), out_specs=tuple(out_specs),
            out_shape=tuple(out_shape), scratch_shapes=list(scratch), input_output_aliases=aliases or {},
            compiler_params=_params(*sem))(*args)
        return tuple(outs), ()
    nx = len(ride.arrs)
    hbm = pl.BlockSpec(memory_space=pl.ANY)

    def hosted(*refs):
        ins, x_in = refs[:n_in], refs[n_in:n_in + nx]
        o0 = n_in + nx
        outs, x_out = refs[o0:o0 + n_out], refs[o0 + n_out:o0 + n_out + nx]
        s0 = o0 + n_out + nx
        scr, x_sem = refs[s0:s0 + n_scr], refs[s0 + n_scr:]
        ids = [pl.program_id(i) for i in range(len(grid))]
        first = functools.reduce(jnp.logical_and, [i == 0 for i in ids])
        last = functools.reduce(jnp.logical_and, [i == g - 1 for i, g in zip(ids, grid)])

        @pl.when(first)
        def _():
            ride.start(x_in, x_out, x_sem)

        body(*ins, *outs, *scr)

        @pl.when(last)
        def _():
            ride.finish(x_in, x_out, x_sem)

    outs = pl.pallas_call(
        hosted, name=name, grid=grid, in_specs=list(in_specs) + [hbm] * nx,
        out_specs=tuple(list(out_specs) + [hbm] * nx), out_shape=tuple(list(out_shape) + ride.out_shape()),
        scratch_shapes=list(scratch) + _exchange_scratch(nx), input_output_aliases=aliases or {},
        compiler_params=_params(*(("arbitrary",) * len(grid))))(*args, *ride.arrs)
    return tuple(outs[:n_out]), tuple(outs[n_out:])


def _pack(parts):
    flat = jnp.concatenate([p.reshape(-1).astype(F32) for p in parts])
    tile = SUBLANES * LANES
    pad = (-flat.shape[0]) % tile
    return jnp.pad(flat, (0, pad)).reshape(-1, LANES)


def _unpack(buf, shapes):
    flat = buf.reshape(-1)
    out, off = [], 0
    for shp in shapes:
        size = math.prod(shp)
        out.append(flat[off:off + size].reshape(shp))
        off += size
    return out


KS_FLAT = (0, 1, 4, 5, 2, 3)
KS_DIAG = (6, 7)


def _local_step(x, target, wa, wo, ws, wos, rel_bias, conv_w, conv_b, dt_bias, a_log, d_skip, norm_w, ln_g, ln_b,
                dist=False):
    s, d = x.shape
    da = wo.shape[-2]
    heads = da // HEAD_DIM
    qkv_cols = 3 * N_GROUPS_ATTN * da
    d_inner = wos.shape[0] * (NDEV if dist else 1)
    conv_dim = conv_w.shape[1]
    ssm_heads = dt_bias.shape[1]
    hpg = ssm_heads // SSM_GROUPS
    gn = (conv_dim - d_inner) // 2
    n_state = gn // SSM_GROUPS
    gw = d_inner // SSM_GROUPS
    p = gw // hpg
    in_ssm = d_inner + conv_dim + ssm_heads
    xb = _cast_bf16(x, name="cast_x")

    per_dev = in_ssm // NDEV
    third = (per_dev // 3) // 16 * 16
    band_rows = (third, third, per_dev - 2 * third)
    qkvs, ws_bands, row0 = [], [], 0
    for g in range(N_GROUPS_ATTN):
        ride = _Gather([ws[row0:row0 + band_rows[g]]]) if dist else None
        row0 += band_rows[g]
        got = _mm(xb, wa, name=f"mm_qkv_g{g}", out_dtype=BF16, n_off=g * 3 * da, n_out=3 * da, ride=ride)
        if dist:
            ws_bands.append(got[1][0])
            got = got[0]
        qkvs.append(got)
    if dist:
        ws = jnp.concatenate(ws_bands, axis=1).reshape(in_ssm, d)
    gate = _mm(xb, wa, name="mm_gate", out_dtype=F32, n_off=qkv_cols, n_out=da)
    bias, bucket = _bias_tables(rel_bias, heads)
    os_, ls_ = [], []
    for g, (_, dil) in enumerate(ATTN_PATTERNS):
        ride = _Gather([wo]) if dist and g == 0 else None
        o, l, rode = _attn_fwd_group(qkvs[g], bias[g], g, dil, da, ride=ride)
        if rode:
            (wo,) = rode
        os_.append(o)
        ls_.append(l)
    o, lse, y = _attn_combine(os_, ls_, gate)
    h1 = _mm(y, wo, name="mm_out_attn", out_dtype=F32)
    x1, x1b = _ln_fwd(x, h1, ln_g[0:1], ln_b[0:1], name="ln1_fwd")

    if dist:
        proj, (wos_slabs,) = _mm(x1b, ws, name="mm_in_ssm", out_dtype=F32, trans_b=True, ride=_Gather([wos]))
        wos = wos_slabs.reshape(d_inner, d)
    else:
        proj = _mm(x1b, ws, name="mm_in_ssm", out_dtype=F32, trans_b=True)
    xbc = _conv_fwd(proj, conv_w, conv_b, d_inner)
    dt = _dt_fwd(proj, dt_bias, d_inner + conv_dim)
    dt_g = dt.reshape(s, SSM_GROUPS, hpg).transpose(1, 0, 2)
    dt_gt = dt.reshape(s, SSM_GROUPS, hpg).transpose(1, 2, 0)
    a = -jnp.exp(a_log)
    a_g = a.reshape(SSM_GROUPS, 1, hpg)
    a_gt = a.reshape(SSM_GROUPS, hpg, 1)
    dskip_e = jnp.repeat(d_skip.reshape(SSM_GROUPS, 1, hpg), p, axis=2)
    yn, yscan, states = _ssd_fwd(xbc, proj, dt_g, dt_gt, a_g, a_gt, dskip_e, norm_w, d_inner, n_state)
    h2 = _mm(yn, wos, name="mm_out_ssm", out_dtype=F32)

    du2, du2b, dg1, db1, loss_t = _ln_bwd(x1, h2, ln_g[1:2], ln_b[1:2], target, with_loss=True, name="ln2_loss_bwd")
    loss = loss_t[0, 0]
    dyn = _mm(du2b, wos, name="mm_dyn", out_dtype=F32, trans_b=True)
    g_wos = _mm(yn.T, du2b, name="mm_dw_out_ssm", out_dtype=BF16)
    parts = {}
    dyscan, dproj_ssm, g_norm, g_dskip, dproj_t = _ssd_epilogue_bwd(dyn, yscan, xbc, proj, dskip_e, norm_w, hpg)
    ride = _Scatter([g_wos.reshape(NDEV, d_inner // NDEV, d)]) if dist else None
    (dxs, d_bm, d_cm, ddt_g, g_a), rode = _ssd_scan_bwd(xbc, dt_g, dt_gt, a_g, a_gt, states, dyscan, dskip_e,
                                                         d_inner, n_state, ride=ride)
    parts["w_out_ssm"] = [list(rode)]
    g_alog = g_a.reshape(1, ssm_heads) * a
    dproj_ssm, g_conv_w, g_conv_b, dproj_t = _conv_bwd(proj, conv_w, conv_b, (dxs, d_bm, d_cm), d_inner, dproj_ssm,
                                                       dproj_t)
    ddt = ddt_g.transpose(1, 0, 2).reshape(s, ssm_heads)
    dproj_ssm, g_dtb, dproj_t = _dt_bwd(proj, dt_bias, ddt, d_inner + conv_dim, dproj_ssm, dproj_t)
    g_ws = _mm(dproj_t, x1b, name="mm_dw_in_ssm", out_dtype=BF16)
    if dist:
        g_ws_slabs = g_ws.reshape(NDEV, per_dev, d)
        dx1, near = _mm(dproj_ssm, ws, name="mm_dx1", out_dtype=F32, res=du2, res_scale=DEEPNORM_ALPHA,
                        ride=_Scatter([g_ws_slabs], KS_FLAT))
    else:
        dx1 = _mm(dproj_ssm, ws, name="mm_dx1", out_dtype=F32, res=du2, res_scale=DEEPNORM_ALPHA)

    du1, du1b, dg0, db0 = _ln_bwd(x, h1, ln_g[0:1], ln_b[0:1], dx1, with_loss=False, name="ln1_bwd")
    dy = _mm(du1b, wo, name="mm_dy", out_dtype=F32, trans_b=True)
    g_wo = _mm(y.T, du1b, name="mm_dw_out_attn", out_dtype=BF16, slab_out=NDEV)
    do, dgate, dd = _attn_bwd_prep(dy, o, gate)
    rides = [_Scatter([g_ws_slabs], KS_DIAG[0:1]), _Scatter([g_wo]), None] if dist else [None] * N_GROUPS_ATTN
    dparts, dss, rode_attn = [], [], []
    for g, (_, dil) in enumerate(ATTN_PATTERNS):
        (dq, dk, dv, ds), rode = _attn_bwd_group(qkvs[g], do, lse, dd, bias[g], g, dil, da, ride=rides[g])
        dparts += [dq, dk, dv]
        dss.append(ds)
        rode_attn += list(rode)
    g_bias = _bias_bwd(jnp.stack(dss), bucket)
    g_rel_bias = g_bias.transpose(2, 0, 1).reshape(NUM_BUCKETS, N_GROUPS_ATTN * heads)
    dproj_attn = jnp.concatenate(dparts + [dgate], axis=1)
    pending = None
    if dist:
        parts["w_out_attn"] = [rode_attn[1:]]
        xbt = xb.T
        half = d // 2
        g_top, (diag_b,) = _mm(xbt[:half], dproj_attn, name="mm_dw_in_attn_top", out_dtype=BF16, slab_out=NDEV,
                               ride=_Scatter([g_ws_slabs], KS_DIAG[1:2]))
        parts["w_in_ssm"] = [[near[0], rode_attn[0], diag_b]]
        g_bot, (top_a,) = _mm(xbt[half:], dproj_attn, name="mm_dw_in_attn_bottom", out_dtype=BF16, slab_out=NDEV,
                              ride=_Scatter([g_top], KS_FLAT))
        dx, (top_b, bot_a) = _mm(dproj_attn, wa, name="mm_dx", out_dtype=F32, trans_b=True, res=du1,
                                 res_scale=DEEPNORM_ALPHA, ride=_Scatter([g_top, g_bot], [KS_DIAG, KS_FLAT]))
        parts["w_in_attn"] = [[top_a, top_b], [bot_a]]
        pending = _Scatter([g_bot], KS_DIAG)
    else:
        g_wa = _mm(xb.T, dproj_attn, name="mm_dw_in_attn", out_dtype=BF16, slab_out=NDEV)
        dx = _mm(dproj_attn, wa, name="mm_dx", out_dtype=F32, trans_b=True, res=du1, res_scale=DEEPNORM_ALPHA)

    g_ln_g = jnp.concatenate([dg0, dg1], axis=0)
    g_ln_b = jnp.concatenate([db0, db1], axis=0)
    small = dict(rel_bias=g_rel_bias, dt_bias=g_dtb, a_log=g_alog, d_skip=g_dskip.reshape(1, ssm_heads),
                 ln_g=g_ln_g, ln_b=g_ln_b, conv_w=g_conv_w, conv_b=g_conv_b, ssm_norm_w=g_norm)
    if dist:
        return loss, dx, parts, pending, small
    return loss, dx, g_wa, g_wo, g_ws, g_wos, small


REPLICATED = ("rel_bias", "dt_bias", "a_log", "d_skip", "ln_g", "ln_b")
SHARDED_SMALL = ("conv_w", "conv_b", "ssm_norm_w")


def kernel(x, w_in_attn, w_out_attn, rel_bias, w_in_ssm, conv_w, conv_b, dt_bias, a_log, d_skip, ssm_norm_w, w_out_ssm, ln_g, ln_b, loss_target, m_w_in_attn, m_w_out_attn, m_rel_bias, m_w_in_ssm, m_conv_w, m_conv_b, m_dt_bias, m_a_log, m_d_skip, m_ssm_norm_w, m_w_out_ssm, m_ln_g, m_ln_b, v_w_in_attn, v_w_out_attn, v_rel_bias, v_w_in_ssm, v_conv_w, v_conv_b, v_dt_bias, v_a_log, v_d_skip, v_ssm_norm_w, v_w_out_ssm, v_ln_g, v_ln_b):
    w = dict(w_in_attn=w_in_attn, w_out_attn=w_out_attn, rel_bias=rel_bias, w_in_ssm=w_in_ssm, conv_w=conv_w,
             conv_b=conv_b, dt_bias=dt_bias, a_log=a_log, d_skip=d_skip, ssm_norm_w=ssm_norm_w,
             w_out_ssm=w_out_ssm, ln_g=ln_g, ln_b=ln_b)
    m = dict(w_in_attn=m_w_in_attn, w_out_attn=m_w_out_attn, rel_bias=m_rel_bias, w_in_ssm=m_w_in_ssm,
             conv_w=m_conv_w, conv_b=m_conv_b, dt_bias=m_dt_bias, a_log=m_a_log, d_skip=m_d_skip,
             ssm_norm_w=m_ssm_norm_w, w_out_ssm=m_w_out_ssm, ln_g=m_ln_g, ln_b=m_ln_b)
    v = dict(w_in_attn=v_w_in_attn, w_out_attn=v_w_out_attn, rel_bias=v_rel_bias, w_in_ssm=v_w_in_ssm,
             conv_w=v_conv_w, conv_b=v_conv_b, dt_bias=v_dt_bias, a_log=v_a_log, d_skip=v_d_skip,
             ssm_norm_w=v_ssm_norm_w, w_out_ssm=v_w_out_ssm, ln_g=v_ln_g, ln_b=v_ln_b)
    me = _lin((lax.axis_index("x"), lax.axis_index("y"), lax.axis_index("c")))
    d = x.shape[2]
    big = ("w_in_attn", "w_out_attn", "w_in_ssm", "w_out_ssm")

    for t in (w, m, v):
        t["w_in_ssm"] = t["w_in_ssm"].transpose(0, 2, 1)
    shards = {k: _cast_bf16(w[k], name=f"cast_{k}") for k in big}
    (wa,) = _exchange_alone(_Gather([shards["w_in_attn"]]), name="gather_w_in_attn")
    cpd = conv_w.shape[2]
    npd = ssm_norm_w.shape[1]
    small_shapes = [(CONV_WIDTH, cpd), (1, cpd), (1, npd)]
    (small_all,) = _exchange_alone(_Gather([_pack([conv_w[0], conv_b, ssm_norm_w])]), name="gather_small_weights",
                                   in_vmem=True)
    small_parts = [_unpack(small_all[i], small_shapes) for i in range(NDEV)]
    conv_w_full = jnp.concatenate([p[0] for p in small_parts], axis=1)
    conv_b_full = jnp.concatenate([p[1] for p in small_parts], axis=1)
    norm_w_full = jnp.concatenate([p[2] for p in small_parts], axis=1)

    loss, dx, parts, pending, small = _local_step(
        x[0], loss_target[0], wa, shards["w_out_attn"], shards["w_in_ssm"], shards["w_out_ssm"], rel_bias,
        conv_w_full, conv_b_full, dt_bias[0:1], a_log[0:1], d_skip[0:1], norm_w_full, ln_g, ln_b, dist=True)
    loss = lax.psum(loss, MESH_AXES)
    out = {}
    out["w_in_ssm"], late = _adamw_sum(parts["w_in_ssm"], w["w_in_ssm"], m["w_in_ssm"], v["w_in_ssm"],
                                       name="adamw_w_in_ssm", ride=pending)
    out["w_in_ssm"] = tuple(t.transpose(0, 2, 1) for t in out["w_in_ssm"])
    parts["w_in_attn"][1] += list(late)
    for k in ("w_out_ssm", "w_out_attn", "w_in_attn"):
        out[k] = _adamw_sum(parts[k], w[k], m[k], v[k], name=f"adamw_{k}")

    order = REPLICATED + SHARDED_SMALL
    g_shapes = [small[k].shape for k in order]
    (g_all,) = _exchange_alone(_Gather([_pack([small[k] for k in order])]), name="gather_small_grads", in_vmem=True)
    g_sum = dict(zip(order, _unpack(_sum_slots(g_all, name="sum_small_grads"), g_shapes)))
    g_mine = {k: g_sum[k] for k in REPLICATED}
    g_mine["conv_w"] = lax.dynamic_slice_in_dim(g_sum["conv_w"], me * cpd, cpd, axis=1)
    g_mine["conv_b"] = lax.dynamic_slice_in_dim(g_sum["conv_b"], me * cpd, cpd, axis=1)
    g_mine["ssm_norm_w"] = lax.dynamic_slice_in_dim(g_sum["ssm_norm_w"], me * npd, npd, axis=1)
    w_shapes = [w[k].shape for k in order]
    g_pack = _pack([g_mine[k] for k in order])
    d_p, m_p, v_p = _adamw_small(g_pack, _pack([w[k] for k in order]), _pack([m[k] for k in order]),
                                 _pack([v[k] for k in order]), name="adamw_small")
    for k, gk, dk, mk, vk in zip(order, _unpack(g_pack, w_shapes), _unpack(d_p, w_shapes), _unpack(m_p, w_shapes),
                                 _unpack(v_p, w_shapes)):
        out[k] = (gk, dk, mk, vk)

    names = ("w_in_attn", "w_out_attn", "rel_bias", "w_in_ssm", "conv_w", "conv_b", "dt_bias", "a_log", "d_skip",
             "ssm_norm_w", "w_out_ssm", "ln_g", "ln_b")
    res = [loss, dx[None]]
    for i in range(4):
        res += [out[k][i] for k in names]
    return tuple(res)
```

```python
import functools
import math

import jax
import jax.numpy as jnp
from jax import lax
from jax.experimental import pallas as pl
from jax.experimental.pallas import tpu as pltpu

F32 = jnp.float32
BF16 = jnp.bfloat16
MESH_AXES = ("x", "y", "c")
NDEV = 8
NPEER = NDEV - 1
LANES = 128
SUBLANES = 8
VMEM_LIMIT = 52 * 1024 * 1024
MM_VMEM_BUDGET = 40 * 1024 * 1024
MM_TK_MAX = 4096
MM_TN_MAX = 1024

ATTN_PATTERNS = ((128, 1), (512, 4), (2048, 16))
N_GROUPS_ATTN = 3
HEAD_DIM = 128
ATTN_BLOCK = 128
ATTN_ROWS_TIMES_HEADS = 8192
NUM_BUCKETS = 32
MAX_DISTANCE = 2048
SSM_GROUPS = 8
CONV_WIDTH = 4
CHUNK = 128
DEPTH = 2
DEEPNORM_ALPHA = (2 * DEPTH) ** 0.25
LN_EPS = 1e-5
RMS_EPS = 1e-5
NEG_INF = -1e30
ADAM_LR = 0.001
ADAM_B1 = 0.9
ADAM_B2 = 0.999
ADAM_EPS = 1e-08
ADAM_WD = 0.01
ADAM_STEP = 10
HIGHEST = lax.Precision.HIGHEST


def _params(*sem):
    return pltpu.CompilerParams(dimension_semantics=sem, vmem_limit_bytes=VMEM_LIMIT)


def _pick(n, prefs):
    for p in prefs:
        if n % p == 0:
            return p
    return n


def _row_tile(r, limit):
    return max(t for t in range(2 * SUBLANES, limit + 1, 2 * SUBLANES) if r % t == 0)


def _dot(a, b):
    return jnp.dot(a, b, preferred_element_type=F32)


def _dot_nt(a, b):
    return lax.dot_general(a, b, (((1,), (1,)), ((), ())), preferred_element_type=F32)


def _dot_tn(a, b):
    return lax.dot_general(a, b, (((0,), (0,)), ((), ())), preferred_element_type=F32)


def _sigmoid(x):
    return 1.0 / (1.0 + jnp.exp(-x))


def _mm(a, b, *, name, out_dtype, trans_b=False, slab_out=0, n_off=0, n_out=None,
        res=None, res_scale=1.0, ride=None):
    m, k = a.shape
    slab_b = b.ndim == 3
    if slab_b:
        ns = b.shape[0]
        if trans_b:
            n, kper = b.shape[1], b.shape[2]
            assert ns * kper == k
        else:
            nper = b.shape[2]
            n = ns * nper
            assert b.shape[1] == k
    else:
        n = b.shape[0] if trans_b else b.shape[1]
        assert (b.shape[1] if trans_b else b.shape[0]) == k
    n_out = n if n_out is None else n_out
    tm = _pick(m, (1024, 640, 512, 256, 128))
    nconstraint = math.gcd(n_out, n_off) if n_off else n_out
    if slab_b and not trans_b:
        nconstraint = math.gcd(nconstraint, nper)
    if slab_out:
        nconstraint = math.gcd(nconstraint, n_out // slab_out)
    kconstraint = kper if (slab_b and trans_b) else k
    tk = max(t for t in range(LANES, min(kconstraint, MM_TK_MAX) + 1, LANES) if kconstraint % t == 0)
    nk = k // tk
    out_bytes = jnp.dtype(out_dtype).itemsize

    def vmem_bytes(t):
        return (2 * 2 * tk * (tm + t) + 2 * tm * t * out_bytes + (4 * tm * t if nk > 1 else 0)
                + (2 * 4 * tm * t if res is not None else 0))

    fits = [t for t in range(LANES, min(nconstraint, MM_TN_MAX) + 1, LANES)
            if nconstraint % t == 0 and vmem_bytes(t) <= MM_VMEM_BUDGET]
    tn = max(fits)
    nb0 = n_off // tn
    grid = (m // tm, n_out // tn, nk)

    a_spec = pl.BlockSpec((tm, tk), lambda i, j, kk: (i, kk))
    if slab_b and not trans_b:
        nps = nper // tn
        b_spec = pl.BlockSpec((None, tk, tn), lambda i, j, kk: ((j + nb0) // nps, kk, (j + nb0) % nps))
    elif slab_b and trans_b:
        kps = kper // tk
        b_spec = pl.BlockSpec((None, tn, tk), lambda i, j, kk: (kk // kps, j + nb0, kk % kps))
    elif trans_b:
        b_spec = pl.BlockSpec((tn, tk), lambda i, j, kk: (j + nb0, kk))
    else:
        b_spec = pl.BlockSpec((tk, tn), lambda i, j, kk: (kk, j + nb0))
    if slab_out:
        ops = (n_out // slab_out) // tn
        o_spec = pl.BlockSpec((None, tm, tn), lambda i, j, kk: (j // ops, i, j % ops))
        o_shape = jax.ShapeDtypeStruct((slab_out, m, n_out // slab_out), out_dtype)
    else:
        o_spec = pl.BlockSpec((tm, tn), lambda i, j, kk: (i, j))
        o_shape = jax.ShapeDtypeStruct((m, n_out), out_dtype)
    in_specs = [a_spec, b_spec]
    args = [a, b]
    if res is not None:
        in_specs.append(pl.BlockSpec((tm, tn), lambda i, j, kk: (i, j)))
        args.append(res)

    def body(*refs):
        a_ref, b_ref = refs[0], refs[1]
        r_ref = refs[2] if res is not None else None
        o_ref = refs[3] if res is not None else refs[2]
        av = a_ref[...].astype(BF16)
        bv = b_ref[...].astype(BF16)
        part = _dot_nt(av, bv) if trans_b else _dot(av, bv)

        def finish(r):
            if res is not None:
                r = r + res_scale * r_ref[...]
            o_ref[...] = r.astype(out_dtype)

        if nk == 1:
            finish(part)
            return
        acc = refs[-1]
        kk = pl.program_id(2)

        @pl.when(kk == 0)
        def _():
            acc[...] = part

        @pl.when(kk > 0)
        def _():
            acc[...] += part

        @pl.when(kk == nk - 1)
        def _():
            finish(acc[...])

    outs, rode = _call(
        body, name=name, grid=grid, in_specs=in_specs, out_specs=[o_spec], out_shape=[o_shape], args=args,
        scratch=[pltpu.VMEM((tm, tn), F32)] if nk > 1 else [], ride=ride,
        sem=("parallel", "parallel", "arbitrary"))
    return (outs[0], rode) if ride is not None else outs[0]


def _cast_bf16(w, *, name, with_transpose=False):
    r, c = w.shape[-2:]
    tr = _row_tile(r, 512)

    def body(w_ref, o_ref, *t_ref):
        v = w_ref[...]
        o_ref[...] = v.astype(BF16)
        if with_transpose:
            t_ref[0][...] = v.T.astype(BF16)

    in_spec = (pl.BlockSpec((None, tr, c), lambda i: (0, i, 0)) if w.ndim == 3
               else pl.BlockSpec((tr, c), lambda i: (i, 0)))
    out_specs = [pl.BlockSpec((tr, c), lambda i: (i, 0))]
    out_shape = [jax.ShapeDtypeStruct((r, c), BF16)]
    if with_transpose:
        out_specs.append(pl.BlockSpec((c, tr), lambda i: (0, i)))
        out_shape.append(jax.ShapeDtypeStruct((c, r), BF16))
    out = pl.pallas_call(
        body, name=name, grid=(r // tr,), in_specs=[in_spec], out_specs=tuple(out_specs),
        out_shape=tuple(out_shape), compiler_params=_params("parallel"),
    )(w)
    return out if with_transpose else out[0]


def _adam_math(w, g, m, v):
    m2 = ADAM_B1 * m + (1.0 - ADAM_B1) * g
    v2 = ADAM_B2 * v + (1.0 - ADAM_B2) * (g * g)
    m_hat = m2 / (1.0 - ADAM_B1 ** ADAM_STEP)
    v_hat = v2 / (1.0 - ADAM_B2 ** ADAM_STEP)
    delta = -ADAM_LR * (m_hat / (jnp.sqrt(v_hat) + ADAM_EPS) + ADAM_WD * w)
    return delta, m2, v2


def _adamw_sum(bands, w, m, v, *, name, ride=None):
    _, r, c = w.shape
    nband = len(bands)
    rows = r // nband
    tr = _row_tile(rows, 128)
    tc = c if (c % LANES or c <= 2560) else _pick(c, (2048, 1024, 512, 256, 128))
    nt = rows // tr
    flat = [p for band in bands for p in band]

    def body(*refs):
        p_refs = refs[:len(flat)]
        w_ref, m_ref, v_ref, g_out, d_out, m_out, v_out = refs[len(flat):]
        i = pl.program_id(0)
        g, at = None, 0
        for q, band in enumerate(bands):
            gq = None
            for p_ref in p_refs[at:at + len(band)]:
                for s in range(p_ref.shape[0]):
                    t = p_ref[s].astype(F32)
                    gq = t if gq is None else gq + t
            at += len(band)
            g = gq if q == 0 else jnp.where(i >= q * nt, gq, g)
        d, m2, v2 = _adam_math(w_ref[...], g, m_ref[...], v_ref[...])
        g_out[...] = g
        d_out[...] = d
        m_out[...] = m2
        v_out[...] = v2

    def band_spec(p, q):
        return pl.BlockSpec((p.shape[0], tr, tc), lambda i, j: (0, jnp.clip(i - q * nt, 0, nt - 1), j))

    spec = pl.BlockSpec((None, tr, tc), lambda i, j: (0, i, j))
    shp = jax.ShapeDtypeStruct((1, r, c), F32)
    outs, rode = _call(
        body, name=name, grid=(r // tr, c // tc),
        in_specs=[band_spec(p, q) for q, band in enumerate(bands) for p in band] + [spec, spec, spec],
        out_specs=[spec, spec, spec, spec], out_shape=[shp, shp, shp, shp], args=(*flat, w, m, v),
        sem=("parallel", "parallel"), ride=ride)
    return (outs, rode) if ride is not None else outs


def _adamw_small(g, w, m, v, *, name):
    shp = jax.ShapeDtypeStruct(w.shape, F32)

    def body(g_ref, w_ref, m_ref, v_ref, d_out, m_out, v_out):
        d, m2, v2 = _adam_math(w_ref[...], g_ref[...], m_ref[...], v_ref[...])
        d_out[...] = d
        m_out[...] = m2
        v_out[...] = v2

    return pl.pallas_call(body, name=name, out_shape=(shp, shp, shp),
                          compiler_params=pltpu.CompilerParams(vmem_limit_bytes=VMEM_LIMIT))(g, w, m, v)


def _sum_slots(parts, *, name):
    _, r, c = parts.shape

    def body(p_ref, o_ref):
        g = p_ref[0]
        for s in range(1, NDEV):
            g = g + p_ref[s]
        o_ref[...] = g

    return pl.pallas_call(body, name=name, out_shape=jax.ShapeDtypeStruct((r, c), F32),
                          compiler_params=pltpu.CompilerParams(vmem_limit_bytes=VMEM_LIMIT))(parts)


def _ln_parts(u):
    mu = jnp.mean(u, axis=-1, keepdims=True)
    xc = u - mu
    var = jnp.mean(xc * xc, axis=-1, keepdims=True)
    rstd = lax.rsqrt(var + LN_EPS)
    return xc * rstd, rstd


def _ln_fwd(xin, h, g, b, *, name):
    s, d = xin.shape
    tm = _pick(s, (128,))

    def body(x_ref, h_ref, g_ref, b_ref, o_ref, ob_ref):
        xhat, _ = _ln_parts(DEEPNORM_ALPHA * x_ref[...] + h_ref[...])
        o = xhat * g_ref[...] + b_ref[...]
        o_ref[...] = o
        ob_ref[...] = o.astype(BF16)

    row = pl.BlockSpec((tm, d), lambda i: (i, 0))
    vec = pl.BlockSpec((1, d), lambda i: (0, 0))
    return pl.pallas_call(
        body, name=name, grid=(s // tm,), in_specs=[row, row, vec, vec], out_specs=(row, row),
        out_shape=(jax.ShapeDtypeStruct((s, d), F32), jax.ShapeDtypeStruct((s, d), BF16)),
        compiler_params=_params("parallel"),
    )(xin, h, g, b)


def _ln_bwd(xin, h, g, b, cot, *, with_loss, name):
    s, d = xin.shape
    tm = _pick(s, (128,))

    def body(x_ref, h_ref, g_ref, b_ref, c_ref, du_ref, dub_ref, dg_ref, db_ref, *rest):
        i = pl.program_id(0)
        xhat, rstd = _ln_parts(DEEPNORM_ALPHA * x_ref[...] + h_ref[...])
        gv = g_ref[...]
        if with_loss:
            diff = xhat * gv + b_ref[...] - c_ref[...]
            part = 0.5 * jnp.sum(jnp.mean(diff * diff, axis=-1, keepdims=True), axis=0, keepdims=True)
            dout = diff / d
        else:
            dout = c_ref[...]

        @pl.when(i == 0)
        def _():
            dg_ref[...] = jnp.zeros_like(dg_ref)
            db_ref[...] = jnp.zeros_like(db_ref)
            if with_loss:
                rest[0][...] = jnp.zeros_like(rest[0])

        dg_ref[...] += jnp.sum(dout * xhat, axis=0, keepdims=True)
        db_ref[...] += jnp.sum(dout, axis=0, keepdims=True)
        if with_loss:
            rest[0][...] += jnp.broadcast_to(part, rest[0].shape)
        dxh = dout * gv
        du = rstd * (dxh - jnp.mean(dxh, axis=-1, keepdims=True)
                     - xhat * jnp.mean(dxh * xhat, axis=-1, keepdims=True))
        du_ref[...] = du
        dub_ref[...] = du.astype(BF16)

    row = pl.BlockSpec((tm, d), lambda i: (i, 0))
    vec = pl.BlockSpec((1, d), lambda i: (0, 0))
    out_specs = [row, row, vec, vec]
    out_shape = [jax.ShapeDtypeStruct((s, d), F32), jax.ShapeDtypeStruct((s, d), BF16),
                 jax.ShapeDtypeStruct((1, d), F32), jax.ShapeDtypeStruct((1, d), F32)]
    if with_loss:
        out_specs.append(pl.BlockSpec((SUBLANES, LANES), lambda i: (0, 0)))
        out_shape.append(jax.ShapeDtypeStruct((SUBLANES, LANES), F32))
    return pl.pallas_call(
        body, name=name, grid=(s // tm,), in_specs=[row, row, vec, vec, row],
        out_specs=tuple(out_specs), out_shape=tuple(out_shape),
        compiler_params=_params("arbitrary"),
    )(xin, h, g, b, cot)


def t5_causal_bucket(dist):
    max_exact = NUM_BUCKETS // 2
    d_f = jnp.maximum(dist, 1).astype(jnp.float32)
    large = max_exact + (jnp.log(d_f / max_exact) / math.log(MAX_DISTANCE / max_exact)
                         * (NUM_BUCKETS - max_exact)).astype(jnp.int32)
    large = jnp.minimum(large, NUM_BUCKETS - 1)
    return jnp.where(dist < max_exact, dist, large)


def _bias_tables(rel_bias, heads):
    qi = lax.broadcasted_iota(jnp.int32, (ATTN_BLOCK, 2 * ATTN_BLOCK), 0)
    ki = lax.broadcasted_iota(jnp.int32, (ATTN_BLOCK, 2 * ATTN_BLOCK), 1)
    delta = ATTN_BLOCK + qi - ki
    buckets = []
    for window, dilation in ATTN_PATTERNS:
        span = window // dilation
        assert span == ATTN_BLOCK
        band = (delta >= 0) & (delta <= span)
        buckets.append(jnp.where(band, t5_causal_bucket(jnp.clip(delta, 0, None) * dilation), -1))
    bucket = jnp.stack(buckets).astype(jnp.int32)

    def body(bk_ref, tbl_ref, o_ref):
        col = pl.program_id(0) * heads + pl.program_id(1)
        bk = bk_ref[...]
        acc = jnp.full(bk.shape, NEG_INF, F32)
        for b in range(NUM_BUCKETS):
            acc = jnp.where(bk == b, tbl_ref[b, col], acc)
        o_ref[...] = acc

    tile = (None, ATTN_BLOCK, 2 * ATTN_BLOCK)
    bias = pl.pallas_call(
        body, name="bias_fwd", grid=(N_GROUPS_ATTN, heads),
        in_specs=[pl.BlockSpec(tile, lambda g, h: (g, 0, 0)), pl.BlockSpec(memory_space=pltpu.SMEM)],
        out_specs=pl.BlockSpec((None,) + tile, lambda g, h: (g, h, 0, 0)),
        out_shape=jax.ShapeDtypeStruct((N_GROUPS_ATTN, heads, ATTN_BLOCK, 2 * ATTN_BLOCK), F32),
        compiler_params=_params("parallel", "parallel"),
    )(bucket, rel_bias)
    return bias, bucket


def _dilated_view(qkv, g, dilation, da):
    del g
    return qkv.reshape(qkv.shape[0] // dilation, dilation * 3 * da), 3 * (da // HEAD_DIM), 0


def _heads_per_step(l, heads):
    for hps in (4, 2, 1):
        if heads % hps == 0 and l * hps <= ATTN_ROWS_TIMES_HEADS:
            return hps
    return 1


def _attn_fwd_group(qkv, bias_g, g, dilation, da, ride=None):
    s = qkv.shape[0]
    heads = da // HEAD_DIM
    l = s // dilation
    nb = l // ATTN_BLOCK
    view, cpb, base = _dilated_view(qkv, g, dilation, da)

    hps = _heads_per_step(l, heads)
    lanes = [slice(i * HEAD_DIM, (i + 1) * HEAD_DIM) for i in range(hps)]

    def body(q_ref, k_ref, v_ref, b_ref, o_ref, l_ref):
        scale = HEAD_DIM ** -0.5

        def block(rows, keys, first):
            q, k, v = q_ref[rows, :], k_ref[keys, :], v_ref[keys, :]
            bias = [b_ref[i, :, ATTN_BLOCK:2 * ATTN_BLOCK] if first else b_ref[i] for i in range(hps)]
            sc = [_dot_nt(q[:, hl], k[:, hl]) * scale + bias[i] for i, hl in enumerate(lanes)]
            mx = [jnp.max(t, axis=-1, keepdims=True) for t in sc]
            p = [jnp.exp(t - m) for t, m in zip(sc, mx)]
            den = [jnp.sum(t, axis=-1, keepdims=True) for t in p]
            for i, hl in enumerate(lanes):
                o_ref[rows, hl] = _dot((p[i] * (1.0 / den[i])).astype(BF16), v[:, hl])
                l_ref[rows, hl] = jnp.broadcast_to(mx[i] + jnp.log(den[i]), (ATTN_BLOCK, HEAD_DIM))

        first = pl.ds(0, ATTN_BLOCK)
        block(first, first, True)

        def step(j, carry):
            r0 = pl.multiple_of(j * ATTN_BLOCK, ATTN_BLOCK)
            rk = pl.multiple_of((j - 1) * ATTN_BLOCK, ATTN_BLOCK)
            block(pl.ds(r0, ATTN_BLOCK), pl.ds(rk, 2 * ATTN_BLOCK), False)
            return carry

        if nb > 1:
            lax.fori_loop(1, nb, step, 0)

    def col(t):
        return lambda r, h: (0, (r * cpb + base + t * heads) // hps + h)

    blk = (l, hps * HEAD_DIM)
    out = pl.BlockSpec(blk, lambda r, h: (0, r * (heads // hps) + h))
    shp = jax.ShapeDtypeStruct((l, dilation * da), F32)
    (o, lse), rode = _call(
        body, name=f"attn_fwd_g{g}", grid=(dilation, heads // hps),
        in_specs=[pl.BlockSpec(blk, col(0)), pl.BlockSpec(blk, col(1)), pl.BlockSpec(blk, col(2)),
                  pl.BlockSpec((hps, ATTN_BLOCK, 2 * ATTN_BLOCK), lambda r, h: (h, 0, 0))],
        out_specs=[out, out], out_shape=[shp, shp], args=(view, view, view, bias_g),
        sem=("parallel", "parallel"), ride=ride)
    return o.reshape(s, da), lse.reshape(s, da), rode


def _attn_combine(os_, ls_, gate):
    s, da = gate.shape
    tm = _pick(s, (512, 256, 128))
    tc = _pick(da, (512, 256, 128))

    assert da // HEAD_DIM <= LANES
    per_step = tc // HEAD_DIM

    def body(o0, o1, o2, l0, l1, l2, g_ref, o_ref, l_ref, y_ref, yt_ref):
        j = pl.program_id(1)
        a0, a1, a2 = l0[...], l1[...], l2[...]
        mx = jnp.maximum(jnp.maximum(a0, a1), a2)
        e0, e1, e2 = jnp.exp(a0 - mx), jnp.exp(a1 - mx), jnp.exp(a2 - mx)
        den = e0 + e1 + e2
        o = (e0 * o0[...] + e1 * o1[...] + e2 * o2[...]) / den
        gv = g_ref[...]
        o_ref[...] = o
        y = o * (gv * _sigmoid(gv))
        y_ref[...] = y.astype(BF16)
        yt_ref[...] = y.T.astype(BF16)
        lse = mx + jnp.log(den)

        @pl.when(j == 0)
        def _():
            l_ref[...] = jnp.zeros_like(l_ref)

        lane = lax.broadcasted_iota(jnp.int32, (1, LANES), 1)
        acc = l_ref[...]
        for i in range(per_step):
            acc = jnp.where(lane == j * per_step + i, lse[:, i * HEAD_DIM:(i + 1) * HEAD_DIM], acc)
        l_ref[...] = acc

    spec = pl.BlockSpec((tm, tc), lambda i, j: (i, j))
    heads_spec = pl.BlockSpec((tm, LANES), lambda i, j: (i, 0))
    return pl.pallas_call(
        body, name="attn_combine", grid=(s // tm, da // tc), in_specs=[spec] * 7,
        out_specs=(spec, heads_spec, spec, pl.BlockSpec((tc, tm), lambda i, j: (j, i))),
        out_shape=(jax.ShapeDtypeStruct((s, da), F32), jax.ShapeDtypeStruct((s, LANES), F32),
                   jax.ShapeDtypeStruct((s, da), BF16), jax.ShapeDtypeStruct((da, s), BF16)),
        compiler_params=_params("parallel", "arbitrary"),
    )(*os_, *ls_, gate)


def _attn_bwd_prep(dy, o, gate):
    s, da = gate.shape
    tm = _pick(s, (512, 256, 128))

    def body(dy_ref, o_ref, g_ref, do_ref, dg_ref, dd_ref):
        j = pl.program_id(1)
        gv = g_ref[...]
        sg = _sigmoid(gv)
        dyv = dy_ref[...]
        ov = o_ref[...]
        do = dyv * (gv * sg)
        do_ref[...] = do.astype(BF16)
        dg_ref[...] = (dyv * ov * (sg * (1.0 + gv * (1.0 - sg)))).astype(BF16)

        @pl.when(j == 0)
        def _():
            dd_ref[...] = jnp.zeros_like(dd_ref)

        lane = lax.broadcasted_iota(jnp.int32, (1, LANES), 1)
        dd_ref[...] = jnp.where(lane == j, jnp.sum(do * ov, axis=-1, keepdims=True), dd_ref[...])

    spec = pl.BlockSpec((tm, HEAD_DIM), lambda i, j: (i, j))
    return pl.pallas_call(
        body, name="attn_bwd_prep", grid=(s // tm, da // HEAD_DIM), in_specs=[spec] * 3,
        out_specs=(spec, spec, pl.BlockSpec((tm, LANES), lambda i, j: (i, 0))),
        out_shape=(jax.ShapeDtypeStruct((s, da), BF16), jax.ShapeDtypeStruct((s, da), BF16),
                   jax.ShapeDtypeStruct((s, LANES), F32)),
        compiler_params=_params("parallel", "arbitrary"),
    )(dy, o, gate)


def _attn_bwd_group(qkv, do, lse, dd, bias_g, g, dilation, da, ride=None):
    s = qkv.shape[0]
    heads = da // HEAD_DIM
    l = s // dilation
    nb = l // ATTN_BLOCK
    view, cpb, base = _dilated_view(qkv, g, dilation, da)
    scale = HEAD_DIM ** -0.5
    hps = _heads_per_step(l, heads)
    lanes = [slice(i * HEAD_DIM, (i + 1) * HEAD_DIM) for i in range(hps)]

    def body(q_ref, k_ref, v_ref, do_ref, l_ref, dd_ref, b_ref, dq_ref, dk_ref, dv_ref, ds_ref, dk_acc, dv_acc):
        h0 = pl.program_id(0) * hps
        r = pl.program_id(1)
        lane = lax.broadcasted_iota(jnp.int32, (1, LANES), 1)

        @pl.when(r == 0)
        def _():
            ds_ref[...] = jnp.zeros_like(ds_ref)

        dk_acc[...] = jnp.zeros_like(dk_acc)
        dv_acc[...] = jnp.zeros_like(dv_acc)

        def block(rows, keys, first):
            q, k, v, dov = q_ref[rows, :], k_ref[keys, :], v_ref[keys, :], do_ref[rows, :]
            lse_all, dd_all = l_ref[rows, :], dd_ref[rows, :]
            pick = [(lane == h0 + i).astype(F32) for i in range(hps)]
            lrow = [jnp.sum(lse_all * m, axis=-1, keepdims=True) for m in pick]
            drow = [jnp.sum(dd_all * m, axis=-1, keepdims=True) for m in pick]
            bias = [b_ref[i, :, ATTN_BLOCK:2 * ATTN_BLOCK] if first else b_ref[i] for i in range(hps)]
            sc = [_dot_nt(q[:, hl], k[:, hl]) for hl in lanes]
            dp = [_dot_nt(dov[:, hl], v[:, hl]) for hl in lanes]
            p = [jnp.exp(sc[i] * scale + bias[i] - lrow[i]) for i in range(hps)]
            ds = [p[i] * (dp[i] - drow[i]) for i in range(hps)]
            dsb = [t.astype(BF16) for t in ds]
            pb = [t.astype(BF16) for t in p]
            for i, hl in enumerate(lanes):
                dq_ref[rows, hl] = (_dot(dsb[i], k[:, hl]) * scale).astype(BF16)
                dk_acc[keys, hl] += _dot_tn(dsb[i], q[:, hl]) * scale
                dv_acc[keys, hl] += _dot_tn(pb[i], dov[:, hl])
                if first:
                    ds_ref[i, :, ATTN_BLOCK:2 * ATTN_BLOCK] += ds[i]
                else:
                    ds_ref[i] += ds[i]

        first = pl.ds(0, ATTN_BLOCK)
        block(first, first, True)

        def step(j, carry):
            r0 = pl.multiple_of(j * ATTN_BLOCK, ATTN_BLOCK)
            rk = pl.multiple_of((j - 1) * ATTN_BLOCK, ATTN_BLOCK)
            block(pl.ds(r0, ATTN_BLOCK), pl.ds(rk, 2 * ATTN_BLOCK), False)
            return carry

        if nb > 1:
            lax.fori_loop(1, nb, step, 0)
        dk_ref[...] = dk_acc[...].astype(BF16)
        dv_ref[...] = dv_acc[...].astype(BF16)

    def col(t):
        return lambda h, r: (0, (r * cpb + base + t * heads) // hps + h)

    blk = (l, hps * HEAD_DIM)
    act = pl.BlockSpec(blk, lambda h, r: (0, r * (heads // hps) + h))
    per_head = pl.BlockSpec((l, LANES), lambda h, r: (0, r))
    tile = pl.BlockSpec((hps, ATTN_BLOCK, 2 * ATTN_BLOCK), lambda h, r: (h, 0, 0))
    shp = jax.ShapeDtypeStruct((l, dilation * da), BF16)
    (dq, dk, dv, ds), rode = _call(
        body, name=f"attn_bwd_g{g}", grid=(heads // hps, dilation),
        in_specs=[pl.BlockSpec(blk, col(0)), pl.BlockSpec(blk, col(1)), pl.BlockSpec(blk, col(2)), act,
                  per_head, per_head, tile],
        out_specs=[act, act, act, tile],
        out_shape=[shp, shp, shp, jax.ShapeDtypeStruct((heads, ATTN_BLOCK, 2 * ATTN_BLOCK), F32)],
        scratch=[pltpu.VMEM(blk, F32), pltpu.VMEM(blk, F32)], sem=("parallel", "arbitrary"), ride=ride,
        args=(view, view, view, do.reshape(l, dilation * da), lse.reshape(l, dilation * LANES),
              dd.reshape(l, dilation * LANES), bias_g))
    return (dq.reshape(s, da), dk.reshape(s, da), dv.reshape(s, da), ds), rode


def _bias_bwd(ds, bucket):
    ng, heads = ds.shape[0], ds.shape[1]

    def body(ds_ref, bk_ref, o_ref):
        bk = bk_ref[...]
        x = ds_ref[...]
        for b in range(NUM_BUCKETS):
            o_ref[:, b:b + 1] = jnp.sum(jnp.where(bk == b, x, 0.0), axis=(0, 1), keepdims=True)

    tile = (None, ATTN_BLOCK, 2 * ATTN_BLOCK)
    out = pl.pallas_call(
        body, name="bias_bwd", grid=(ng, heads),
        in_specs=[pl.BlockSpec((None,) + tile, lambda g, h: (g, h, 0, 0)), pl.BlockSpec(tile, lambda g, h: (g, 0, 0))],
        out_specs=pl.BlockSpec((None, None, 1, NUM_BUCKETS), lambda g, h: (g, h, 0, 0)),
        out_shape=jax.ShapeDtypeStruct((ng, heads, 1, NUM_BUCKETS), F32),
        compiler_params=_params("parallel", "parallel"),
    )(ds, bucket)
    return out.reshape(ng, heads, NUM_BUCKETS)


def _shift_rows(x, halo, s):
    r = pltpu.roll(x, s, axis=0)
    rh = pltpu.roll(halo, s, axis=0)
    row = lax.broadcasted_iota(jnp.int32, halo.shape, 0)
    top = jnp.where(row < s, rh, r[0:SUBLANES])
    return jnp.concatenate([top, r[SUBLANES:]], axis=0)


def _conv_out(x, halo, w, b):
    acc = b + w[CONV_WIDTH - 1:CONV_WIDTH] * x
    for kk in range(CONV_WIDTH - 1):
        acc = acc + w[kk:kk + 1] * _shift_rows(x, halo, CONV_WIDTH - 1 - kk)
    return acc


def _conv_fwd(proj, conv_w, conv_b, col0):
    s = proj.shape[0]
    c = conv_w.shape[1]
    ts = _pick(s, (512, 256, 128))
    tc = _pick(math.gcd(c, col0), (512, 256, 128))
    cb0 = col0 // tc
    hb = ts // SUBLANES

    def body(x_ref, h_ref, w_ref, b_ref, o_ref):
        i = pl.program_id(0)
        halo = jnp.where(i > 0, h_ref[...], 0.0)
        u = _conv_out(x_ref[...], halo, w_ref[...], b_ref[...])
        o_ref[...] = u * _sigmoid(u)

    return pl.pallas_call(
        body, name="conv_fwd", grid=(s // ts, c // tc),
        in_specs=[pl.BlockSpec((ts, tc), lambda i, j: (i, cb0 + j)),
                  pl.BlockSpec((SUBLANES, tc), lambda i, j: (jnp.maximum(i * hb - 1, 0), cb0 + j)),
                  pl.BlockSpec((CONV_WIDTH, tc), lambda i, j: (0, j)),
                  pl.BlockSpec((1, tc), lambda i, j: (0, j))],
        out_specs=pl.BlockSpec((ts, tc), lambda i, j: (i, j)),
        out_shape=jax.ShapeDtypeStruct((s, c), F32),
        compiler_params=_params("parallel", "parallel"),
    )(proj, proj, conv_w, conv_b)


def _conv_bwd(proj, conv_w, conv_b, dacts, col0, dproj, dproj_t):
    s = proj.shape[0]
    c = conv_w.shape[1]
    widths = [d.shape[1] for d in dacts]
    assert sum(widths) == c
    ts = _pick(s, (512, 256, 128))
    tc = _pick(math.gcd(math.gcd(c, col0), math.gcd(*widths)), (512, 256, 128))
    cb0 = col0 // tc
    hb = ts // SUBLANES
    nblk = s // ts
    ext = ts + SUBLANES
    nb = [wd // tc for wd in widths]
    starts = [0, nb[0], nb[0] + nb[1]]

    def body(x_ref, xp_ref, xn_ref, d0, d1, d2, n0, n1, n2, w_ref, b_ref, _, __, dx_ref, dw_ref, db_ref, dxt_ref):
        j = pl.program_id(0)
        i = pl.program_id(1)
        last = i == nblk - 1
        w = w_ref[...]
        halo = jnp.where(i > 0, xp_ref[...], 0.0)
        x = x_ref[...]
        xe = jnp.concatenate([x, xn_ref[...]], axis=0)
        dcur = jnp.where(j < starts[1], d0[...], jnp.where(j < starts[2], d1[...], d2[...]))
        dnext = jnp.where(j < starts[1], n0[...], jnp.where(j < starts[2], n1[...], n2[...]))
        de = jnp.concatenate([dcur, jnp.where(last, 0.0, dnext)], axis=0)
        u = _conv_out(xe, halo, w, b_ref[...])
        sg = _sigmoid(u)
        dpre = de * (sg * (1.0 + u * (1.0 - sg)))
        dx = w[CONV_WIDTH - 1:CONV_WIDTH] * dpre[0:ts]
        for kk in range(CONV_WIDTH - 1):
            sh = CONV_WIDTH - 1 - kk
            dx = dx + w[kk:kk + 1] * pltpu.roll(dpre, ext - sh, axis=0)[0:ts]
        dx_ref[...] = dx.astype(BF16)
        dxt_ref[...] = dx.T.astype(BF16)
        dcur = dpre[0:ts]

        @pl.when(i == 0)
        def _():
            dw_ref[...] = jnp.zeros_like(dw_ref)
            db_ref[...] = jnp.zeros_like(db_ref)

        db_ref[...] += jnp.sum(dcur, axis=0, keepdims=True)
        dw_ref[CONV_WIDTH - 1:CONV_WIDTH, :] += jnp.sum(dcur * x, axis=0, keepdims=True)
        for kk in range(CONV_WIDTH - 1):
            xs = _shift_rows(x, halo, CONV_WIDTH - 1 - kk)
            dw_ref[kk:kk + 1, :] += jnp.sum(dcur * xs, axis=0, keepdims=True)

    cur_p = pl.BlockSpec((ts, tc), lambda j, i: (i, cb0 + j))
    prev_p = pl.BlockSpec((SUBLANES, tc), lambda j, i: (jnp.maximum(i * hb - 1, 0), cb0 + j))
    nxt = lambda i: jnp.minimum((i + 1) * hb, nblk * hb - 1)
    next_p = pl.BlockSpec((SUBLANES, tc), lambda j, i: (nxt(i), cb0 + j))

    def part(q):
        return lambda j: jnp.clip(j - starts[q], 0, nb[q] - 1)

    cur_d = [pl.BlockSpec((ts, tc), lambda j, i, f=part(q): (i, f(j))) for q in range(3)]
    next_d = [pl.BlockSpec((SUBLANES, tc), lambda j, i, f=part(q): (nxt(i), f(j))) for q in range(3)]
    vec4 = pl.BlockSpec((CONV_WIDTH, tc), lambda j, i: (0, j))
    vec1 = pl.BlockSpec((1, tc), lambda j, i: (0, j))
    hbm = pl.BlockSpec(memory_space=pl.ANY)
    return pl.pallas_call(
        body, name="conv_bwd", grid=(c // tc, nblk),
        in_specs=[cur_p, prev_p, next_p, *cur_d, *next_d, vec4, vec1, hbm, hbm],
        out_specs=(cur_p, vec4, vec1, pl.BlockSpec((tc, ts), lambda j, i: (cb0 + j, i))),
        out_shape=(jax.ShapeDtypeStruct(dproj.shape, dproj.dtype), jax.ShapeDtypeStruct((CONV_WIDTH, c), F32),
                   jax.ShapeDtypeStruct((1, c), F32), jax.ShapeDtypeStruct(dproj_t.shape, dproj_t.dtype)),
        input_output_aliases={11: 0, 12: 3},
        compiler_params=_params("parallel", "arbitrary"),
    )(proj, proj, proj, *dacts, *dacts, conv_w, conv_b, dproj, dproj_t)


def _dt_fwd(proj, dt_bias, col0):
    s = proj.shape[0]
    h = dt_bias.shape[1]
    ts = _pick(s, (1024, 512, 256, 128))

    def body(x_ref, b_ref, o_ref):
        v = x_ref[...] + b_ref[...]
        o_ref[...] = jnp.maximum(v, 0.0) + jnp.log1p(jnp.exp(-jnp.abs(v)))

    return pl.pallas_call(
        body, name="dt_fwd", grid=(s // ts,),
        in_specs=[pl.BlockSpec((ts, h), lambda i: (i, col0 // h)), pl.BlockSpec((1, h), lambda i: (0, 0))],
        out_specs=pl.BlockSpec((ts, h), lambda i: (i, 0)), out_shape=jax.ShapeDtypeStruct((s, h), F32),
        compiler_params=_params("parallel"),
    )(proj, dt_bias)


def _dt_bwd(proj, dt_bias, ddt, col0, dproj, dproj_t):
    s = proj.shape[0]
    h = dt_bias.shape[1]
    ts = _pick(s, (1024, 512, 256, 128))

    def body(x_ref, b_ref, d_ref, _, __, o_ref, db_ref, ot_ref):
        i = pl.program_id(0)
        draw = d_ref[...] * _sigmoid(x_ref[...] + b_ref[...])
        o_ref[...] = draw.astype(BF16)
        ot_ref[...] = draw.T.astype(BF16)

        @pl.when(i == 0)
        def _():
            db_ref[...] = jnp.zeros_like(db_ref)

        db_ref[...] += jnp.sum(draw, axis=0, keepdims=True)

    hbm = pl.BlockSpec(memory_space=pl.ANY)
    return pl.pallas_call(
        body, name="dt_bwd", grid=(s // ts,),
        in_specs=[pl.BlockSpec((ts, h), lambda i: (i, col0 // h)), pl.BlockSpec((1, h), lambda i: (0, 0)),
                  pl.BlockSpec((ts, h), lambda i: (i, 0)), hbm, hbm],
        out_specs=(pl.BlockSpec((ts, h), lambda i: (i, col0 // h)), pl.BlockSpec((1, h), lambda i: (0, 0)),
                   pl.BlockSpec((h, ts), lambda i: (col0 // h, i))),
        out_shape=(jax.ShapeDtypeStruct(dproj.shape, dproj.dtype), jax.ShapeDtypeStruct((1, h), F32),
                   jax.ShapeDtypeStruct(dproj_t.shape, dproj_t.dtype)),
        input_output_aliases={3: 0, 4: 2},
        compiler_params=_params("arbitrary"),
    )(proj, dt_bias, ddt, dproj, dproj_t)


def _chunk_terms(dt, dt_t, a, a_t):
    li = lax.broadcasted_iota(jnp.int32, (CHUNK, CHUNK), 0)
    si = lax.broadcasted_iota(jnp.int32, (CHUNK, CHUNK), 1)
    lower = (li >= si).astype(F32)
    upper = (li <= si).astype(F32)
    acum = jnp.dot(lower, dt * a, preferred_element_type=F32, precision=HIGHEST)
    acum_t = jnp.dot(dt_t * a_t, upper, preferred_element_type=F32, precision=HIGHEST)
    return acum, acum_t, li, si, upper


def _dot_exact01(t, m01):
    r = t.shape[0]
    hi = t.astype(BF16)
    rest = t - hi.astype(F32)
    mid = rest.astype(BF16)
    lo = (rest - mid.astype(F32)).astype(BF16)
    out = _dot(jnp.concatenate([hi, mid, lo], axis=0), m01.astype(BF16))
    return out[0:r] + out[r:2 * r] + out[2 * r:3 * r]


def _head_lanes(dt, acum, gw):
    hpg = dt.shape[1]
    p = gw // hpg
    spread = (lax.broadcasted_iota(jnp.int32, (hpg, gw), 1) // p
              == lax.broadcasted_iota(jnp.int32, (hpg, gw), 0)).astype(F32)
    both = _dot_exact01(jnp.concatenate([dt, acum], axis=0), spread)
    dt_e, acum_e = both[0:CHUNK], both[CHUNK:2 * CHUNK]
    alast_e = acum_e[CHUNK - 1:CHUNK, :]
    return dt_e, jnp.exp(acum_e), jnp.exp(alast_e - acum_e), jnp.exp(alast_e)


def _fold_heads(t, hpg):
    gw = t.shape[1]
    p = gw // hpg
    fold = (lax.broadcasted_iota(jnp.int32, (gw, hpg), 0) // p
            == lax.broadcasted_iota(jnp.int32, (gw, hpg), 1)).astype(F32)
    return _dot_exact01(t, fold)


def _ssd_fwd(xbc, proj, dt_g, dt_gt, a_g, a_gt, dskip_e, norm_w, d_inner, n_state):
    s = xbc.shape[0]
    hpg = dt_g.shape[2]
    gw = d_inner // SSM_GROUPS
    p = gw // hpg
    nc = s // CHUNK
    n = n_state
    b0 = d_inner // n
    c0 = b0 + SSM_GROUPS
    per_tile = LANES // p

    def body(xs_ref, b_ref, c_ref, dt_ref, dtt_ref, a_ref, at_ref, z_ref, dsk_ref, nw_ref,
             yn_ref, y_ref, st_ref, ynt_ref, state):
        c = pl.program_id(1)

        @pl.when(c == 0)
        def _():
            state[...] = jnp.zeros_like(state)

        st = state[...]
        st_ref[...] = st
        xs = xs_ref[...]
        bm = b_ref[...].astype(BF16)
        cm = c_ref[...].astype(BF16)
        dt = dt_ref[...]
        acum, acum_t, li, si, _ = _chunk_terms(dt, dtt_ref[...], a_ref[...], at_ref[...])
        dt_e, e_a, t_e, e_last = _head_lanes(dt, acum, gw)
        xdt = xs * dt_e
        xdtb = xdt.astype(BF16)
        cb = _dot_nt(cm, bm)
        causal = li >= si
        lane = lax.broadcasted_iota(jnp.int32, (1, LANES), 1)
        y_ref[...] = _dot(cm, st.astype(BF16)) * e_a
        for q in range(gw // LANES):
            ql = slice(q * LANES, (q + 1) * LANES)
            xq = xdtb[:, ql]
            ms = []
            for i in range(per_tile):
                h = q * per_tile + i
                decay = jnp.exp(jnp.where(causal, acum[:, h:h + 1] - acum_t[h:h + 1, :], NEG_INF))
                ms.append((cb * decay).astype(BF16))
            y_all = _dot(jnp.concatenate(ms, axis=0), xq)
            yd = y_all[0:CHUNK]
            for i in range(1, per_tile):
                yd = jnp.where(lane >= i * p, y_all[i * CHUNK:(i + 1) * CHUNK], yd)
            y_ref[:, ql] += yd
        state[...] = st * e_last + _dot_tn(bm, (xdt * t_e).astype(BF16))
        yt = y_ref[...] + xs * dsk_ref[...]
        z = z_ref[...]
        yz = yt * (z * _sigmoid(z))
        r = lax.rsqrt(jnp.mean(yz * yz, axis=-1, keepdims=True) + RMS_EPS)
        yn = yz * r * nw_ref[...]
        yn_ref[...] = yn.astype(BF16)
        ynt_ref[...] = yn.T.astype(BF16)

    wide = pl.BlockSpec((CHUNK, gw), lambda g, c: (c, g))
    return pl.pallas_call(
        body, name="ssd_fwd", grid=(SSM_GROUPS, nc),
        in_specs=[wide,
                  pl.BlockSpec((CHUNK, n), lambda g, c: (c, b0 + g)),
                  pl.BlockSpec((CHUNK, n), lambda g, c: (c, c0 + g)),
                  pl.BlockSpec((None, CHUNK, hpg), lambda g, c: (g, c, 0)),
                  pl.BlockSpec((None, hpg, CHUNK), lambda g, c: (g, 0, c)),
                  pl.BlockSpec((None, 1, hpg), lambda g, c: (g, 0, 0)),
                  pl.BlockSpec((None, hpg, 1), lambda g, c: (g, 0, 0)),
                  wide,
                  pl.BlockSpec((None, 1, gw), lambda g, c: (g, 0, 0)),
                  pl.BlockSpec((1, gw), lambda g, c: (0, g))],
        out_specs=(wide, wide, pl.BlockSpec((None, None, n, gw), lambda g, c: (g, c, 0, 0)),
                   pl.BlockSpec((gw, CHUNK), lambda g, c: (g, c))),
        out_shape=(jax.ShapeDtypeStruct((s, d_inner), BF16), jax.ShapeDtypeStruct((s, d_inner), F32),
                   jax.ShapeDtypeStruct((SSM_GROUPS, nc, n, gw), F32), jax.ShapeDtypeStruct((d_inner, s), BF16)),
        scratch_shapes=[pltpu.VMEM((n, gw), F32)],
        compiler_params=_params("parallel", "arbitrary"),
    )(xbc, xbc, xbc, dt_g, dt_gt, a_g, a_gt, proj, dskip_e, norm_w)


def _ssd_epilogue_bwd(dyn, y, xbc, proj, dskip_e, norm_w, hpg):
    s, d_inner = dyn.shape
    gw = d_inner // SSM_GROUPS
    p = gw // hpg
    nc = s // CHUNK

    def body(dyn_ref, y_ref, xs_ref, z_ref, dsk_ref, nw_ref, dy_ref, dz_ref, dnw_ref, ddsk_ref, dzt_ref):
        c = pl.program_id(1)
        xs = xs_ref[...]
        z = z_ref[...]
        yt = y_ref[...] + xs * dsk_ref[...]
        sg = _sigmoid(z)
        sz = z * sg
        yz = yt * sz
        r = lax.rsqrt(jnp.mean(yz * yz, axis=-1, keepdims=True) + RMS_EPS)
        dynv = dyn_ref[...]
        dyh = dynv * nw_ref[...]
        dyz = r * (dyh - yz * (r * r) * jnp.mean(dyh * yz, axis=-1, keepdims=True))
        dyt = dyz * sz
        dy_ref[...] = dyt
        dz = dyz * yt * (sg * (1.0 + z * (1.0 - sg)))
        dz_ref[...] = dz.astype(BF16)
        dzt_ref[...] = dz.T.astype(BF16)

        @pl.when(c == 0)
        def _():
            dnw_ref[...] = jnp.zeros_like(dnw_ref)
            ddsk_ref[...] = jnp.zeros_like(ddsk_ref)

        dnw_ref[...] += jnp.sum(dynv * yz * r, axis=0, keepdims=True)
        colsum = jnp.sum(dyt * xs, axis=0, keepdims=True)
        fold = (lax.broadcasted_iota(jnp.int32, (gw, hpg), 0) // p
                == lax.broadcasted_iota(jnp.int32, (gw, hpg), 1)).astype(F32)
        ddsk_ref[...] += jnp.dot(colsum, fold, preferred_element_type=F32, precision=HIGHEST)

    wide = pl.BlockSpec((CHUNK, gw), lambda g, c: (c, g))
    return pl.pallas_call(
        body, name="ssd_epilogue_bwd", grid=(SSM_GROUPS, nc),
        in_specs=[wide, wide, wide, wide, pl.BlockSpec((None, 1, gw), lambda g, c: (g, 0, 0)),
                  pl.BlockSpec((1, gw), lambda g, c: (0, g))],
        out_specs=(wide, wide, pl.BlockSpec((1, gw), lambda g, c: (0, g)),
                   pl.BlockSpec((None, 1, hpg), lambda g, c: (g, 0, 0)),
                   pl.BlockSpec((gw, CHUNK), lambda g, c: (g, c))),
        out_shape=(jax.ShapeDtypeStruct((s, d_inner), F32), jax.ShapeDtypeStruct((s, proj.shape[1]), BF16),
                   jax.ShapeDtypeStruct((1, d_inner), F32), jax.ShapeDtypeStruct((SSM_GROUPS, 1, hpg), F32),
                   jax.ShapeDtypeStruct((proj.shape[1], s), BF16)),
        compiler_params=_params("parallel", "arbitrary"),
    )(dyn, y, xbc, proj, dskip_e, norm_w)


def _ssd_scan_bwd(xbc, dt_g, dt_gt, a_g, a_gt, states, dy, dskip_e, d_inner, n_state, ride=None):
    s = xbc.shape[0]
    hpg = dt_g.shape[2]
    gw = d_inner // SSM_GROUPS
    p = gw // hpg
    nc = s // CHUNK
    n = n_state
    b0 = d_inner // n
    c0 = b0 + SSM_GROUPS
    per_tile = LANES // p

    def body(xs_ref, b_ref, c_ref, dt_ref, dtt_ref, a_ref, at_ref, st_ref, dy_ref, dsk_ref,
             dxs_ref, db_ref, dc_ref, ddt_ref, da_ref, dstate, ydiag_ref, dxd_ref):
        c = pl.program_id(1)

        @pl.when(c == 0)
        def _():
            dstate[...] = jnp.zeros_like(dstate)
            da_ref[...] = jnp.zeros_like(da_ref)

        xs = xs_ref[...]
        bm = b_ref[...].astype(BF16)
        cm = c_ref[...].astype(BF16)
        dt = dt_ref[...]
        a = a_ref[...]
        dyv = dy_ref[...]
        dsk = dsk_ref[...]
        acum, acum_t, li, si, upper = _chunk_terms(dt, dtt_ref[...], a, at_ref[...])
        dt_e, e_a, t_e, e_last = _head_lanes(dt, acum, gw)
        cb = _dot_nt(cm, bm)
        lower_mask = li >= si
        lane = lax.broadcasted_iota(jnp.int32, (1, LANES), 1)
        row_l = lax.broadcasted_iota(jnp.int32, (CHUNK, 1), 0)
        st = st_ref[...]
        stb = st.astype(BF16)
        dst = dstate[...]
        dstb = dst.astype(BF16)
        xdt = xs * dt_e
        xdtb = xdt.astype(BF16)
        dyb = dyv.astype(BF16)
        dye = dyv * e_a
        dyeb = dye.astype(BF16)
        xte = xdt * t_e
        xteb = xte.astype(BF16)
        wv = _dot(bm, dstb)
        yo = _dot(cm, stb)
        dcb = jnp.zeros((CHUNK, CHUNK), F32)
        for q in range(gw // LANES):
            ql = slice(q * LANES, (q + 1) * LANES)
            xq = xdtb[:, ql]
            dq = dyb[:, ql]
            decays, ms, mts, dqs = [], [], [], []
            for i in range(per_tile):
                h = q * per_tile + i
                decay = jnp.exp(jnp.where(lower_mask, acum[:, h:h + 1] - acum_t[h:h + 1, :], NEG_INF))
                mm = cb * decay
                mine = (lane >= i * p) & (lane < (i + 1) * p)
                decays.append(decay)
                ms.append(mm.astype(BF16))
                mts.append(mm.T.astype(BF16))
                dqs.append(jnp.where(mine, dq, jnp.zeros_like(dq)))
            dm_all = _dot_nt(jnp.concatenate(dqs, axis=0), xq)
            y_all = _dot(jnp.concatenate(ms, axis=0), xq)
            d_all = _dot(jnp.concatenate(mts, axis=0), dq)
            yd = dd = None
            for i in range(per_tile):
                rows = slice(i * CHUNK, (i + 1) * CHUNK)
                dcb = dcb + dm_all[rows] * decays[i]
                yd = y_all[rows] if i == 0 else jnp.where(lane >= i * p, y_all[rows], yd)
                dd = d_all[rows] if i == 0 else jnp.where(lane >= i * p, d_all[rows], dd)
            ydiag_ref[:, ql] = yd
            dxd_ref[:, ql] = dd
        ydiag = ydiag_ref[...]
        dxd = dxd_ref[...]
        dxdt = dxd + t_e * wv
        xw = xte * wv
        last_in = jnp.sum(xw, axis=0, keepdims=True) + e_last * jnp.sum(dst * st, axis=0, keepdims=True)
        folded = _fold_heads(jnp.concatenate(
            [dyb.astype(F32) * ydiag - xdtb.astype(F32) * dxd - xw + dye * yo, dxdt * xs,
             jnp.broadcast_to(last_in, (SUBLANES, gw))],
            axis=0), hpg)
        dalast = folded[2 * CHUNK:2 * CHUNK + 1]
        d_acum = folded[0:CHUNK] + jnp.where(row_l == CHUNK - 1, dalast, 0.0)
        ddt_x = folded[CHUNK:2 * CHUNK]
        dxs_ref[...] = dxdt * dt_e + dyv * dsk
        dbf = dcb.astype(BF16)
        dc_ref[...] = _dot_nt(dyeb, stb) + _dot(dbf, bm)
        db_ref[...] = _dot_nt(xteb, dstb) + _dot_tn(dbf, cm)
        dstate[...] = dst * e_last + _dot_tn(cm, dyeb)
        d_da = jnp.dot(upper, d_acum, preferred_element_type=F32, precision=HIGHEST)
        ddt_ref[...] = d_da * a + ddt_x
        da_ref[...] += jnp.sum(d_da * dt, axis=0, keepdims=True)

    rev = lambda c: nc - 1 - c
    wide = pl.BlockSpec((CHUNK, gw), lambda g, c: (rev(c), g))
    return _call(
        body, name="ssd_scan_bwd", grid=(SSM_GROUPS, nc),
        in_specs=[wide,
                  pl.BlockSpec((CHUNK, n), lambda g, c: (rev(c), b0 + g)),
                  pl.BlockSpec((CHUNK, n), lambda g, c: (rev(c), c0 + g)),
                  pl.BlockSpec((None, CHUNK, hpg), lambda g, c: (g, rev(c), 0)),
                  pl.BlockSpec((None, hpg, CHUNK), lambda g, c: (g, 0, rev(c))),
                  pl.BlockSpec((None, 1, hpg), lambda g, c: (g, 0, 0)),
                  pl.BlockSpec((None, hpg, 1), lambda g, c: (g, 0, 0)),
                  pl.BlockSpec((None, None, n, gw), lambda g, c: (g, rev(c), 0, 0)),
                  wide,
                  pl.BlockSpec((None, 1, gw), lambda g, c: (g, 0, 0))],
        out_specs=[wide,
                   pl.BlockSpec((CHUNK, n), lambda g, c: (rev(c), g)),
                   pl.BlockSpec((CHUNK, n), lambda g, c: (rev(c), g)),
                   pl.BlockSpec((None, CHUNK, hpg), lambda g, c: (g, rev(c), 0)),
                   pl.BlockSpec((None, 1, hpg), lambda g, c: (g, 0, 0))],
        out_shape=[jax.ShapeDtypeStruct((s, d_inner), F32),
                   jax.ShapeDtypeStruct((s, SSM_GROUPS * n), F32), jax.ShapeDtypeStruct((s, SSM_GROUPS * n), F32),
                   jax.ShapeDtypeStruct((SSM_GROUPS, s, hpg), F32), jax.ShapeDtypeStruct((SSM_GROUPS, 1, hpg), F32)],
        scratch=[pltpu.VMEM((n, gw), F32), pltpu.VMEM((CHUNK, gw), F32), pltpu.VMEM((CHUNK, gw), F32)],
        sem=("parallel", "arbitrary"), ride=ride,
        args=(xbc, xbc, xbc, dt_g, dt_gt, a_g, a_gt, states, dy, dskip_e))


def _lin(p):
    return 4 * p[0] + 2 * p[1] + p[2]


class _Gather:
    def __init__(self, arrs, rows=None, into=None):
        self.arrs = list(arrs)
        self.rows = rows
        self.into = list(into) if into is not None else []

    def out_shape(self):
        if self.rows is None:
            return [jax.ShapeDtypeStruct((NDEV,) + a.shape, a.dtype) for a in self.arrs]
        return [jax.ShapeDtypeStruct((NDEV, self.rows[1]) + a.shape[1:], a.dtype) for a in self.arrs]

    def _copies(self, ins, outs, sems):
        send_sems, recv_sems, local_sems = sems
        x, y, c = lax.axis_index("x"), lax.axis_index("y"), lax.axis_index("c")
        me, sibling = (x, y, c), (x, y, 1 - c)
        chips = [(1 - x, y), (x, 1 - y), (1 - x, 1 - y)]

        def slot(a, block):
            if self.rows is None:
                return outs[a].at[_lin(block)]
            return outs[a].at[_lin(block), pl.ds(self.rows[0], ins[a].shape[0])]

        def copy(a, k, block, to, src=None):
            rows = slot(a, block)
            return pltpu.make_async_remote_copy(
                src_ref=rows if src is None else src, dst_ref=rows,
                send_sem=send_sems.at[a * NPEER + k], recv_sem=recv_sems.at[a * NPEER + k],
                device_id=to, device_id_type=pl.DeviceIdType.MESH)

        na = len(ins)
        mine = [pltpu.make_async_copy(ins[a], slot(a, me), local_sems.at[a]) for a in range(na)]
        first = []
        for a in range(na):
            first.append(copy(a, 0, me, sibling, src=ins[a]))
            first += [copy(a, 1 + j, me, (*chip, c), src=ins[a]) for j, chip in enumerate(chips)]
        return copy, mine, first, me, sibling, chips, c, na

    def start(self, ins, outs, sems):
        _, mine, first, *_ = self._copies(ins, outs, sems)
        for cp in mine + first:
            cp.start()

    def finish(self, ins, outs, sems):
        copy, mine, first, me, sibling, chips, c, na = self._copies(ins, outs, sems)
        passed = []
        for j, chip in enumerate(chips):
            for a in range(na):
                copy(a, 1 + j, (*chip, c), me).wait_recv()
                cp = copy(a, 4 + j, (*chip, c), sibling)
                cp.start()
                passed.append(cp)
        for a in range(na):
            copy(a, 0, sibling, me).wait_recv()
            for j, chip in enumerate(chips):
                copy(a, 4 + j, (*chip, 1 - c), me).wait_recv()
        for cp in first + passed:
            cp.wait_send()
        for cp in mine:
            cp.wait()


class _Scatter:
    def __init__(self, arrs, ks=tuple(range(NDEV))):
        self.arrs = list(arrs)
        self.ks = [tuple(k) for k in ks] if isinstance(ks[0], (tuple, list)) else [tuple(ks)] * len(self.arrs)
        assert len(self.ks) == len(self.arrs)

    def out_shape(self):
        return [jax.ShapeDtypeStruct((len(k),) + a.shape[1:], a.dtype) for a, k in zip(self.arrs, self.ks)]

    def _copies(self, ins, outs, sems):
        send_sems, recv_sems, local_sems = sems
        x, y, c = lax.axis_index("x"), lax.axis_index("y"), lax.axis_index("c")
        me = (x, y, c)

        def peer(k):
            return (1 - x if k & 4 else x, 1 - y if k & 2 else y, 1 - c if k & 1 else c)

        local, remote = [], []
        for a in range(len(ins)):
            for i, k in enumerate(self.ks[a]):
                if k == 0:
                    local.append(pltpu.make_async_copy(ins[a].at[_lin(me)], outs[a].at[i], local_sems.at[a]))
                else:
                    remote.append(pltpu.make_async_remote_copy(
                        src_ref=ins[a].at[_lin(peer(k))], dst_ref=outs[a].at[i],
                        send_sem=send_sems.at[a * NPEER + k - 1], recv_sem=recv_sems.at[a * NPEER + k - 1],
                        device_id=peer(k), device_id_type=pl.DeviceIdType.MESH))
        return local, remote

    def start(self, ins, outs, sems):
        local, remote = self._copies(ins, outs, sems)
        for cp in local + remote:
            cp.start()

    def finish(self, ins, outs, sems):
        local, remote = self._copies(ins, outs, sems)
        for cp in remote:
            cp.wait_recv()
        for cp in remote:
            cp.wait_send()
        for cp in local:
            cp.wait()


def _exchange_scratch(na):
    return [pltpu.SemaphoreType.DMA((na * NPEER,)), pltpu.SemaphoreType.DMA((na * NPEER,)),
            pltpu.SemaphoreType.DMA((na,))]


def _exchange_alone(ex, *, name, in_vmem=False):
    na = len(ex.arrs)

    def body(*refs):
        ins, outs, sems = refs[:na], refs[na:2 * na], refs[2 * na:]
        ex.start(ins, outs, sems)
        ex.finish(ins, outs, sems)

    spec = pl.BlockSpec(memory_space=pltpu.VMEM if in_vmem else pl.ANY)
    return pl.pallas_call(
        body, name=name, out_shape=tuple(ex.out_shape()), in_specs=[spec] * na, out_specs=tuple([spec] * na),
        scratch_shapes=_exchange_scratch(na),
        compiler_params=pltpu.CompilerParams(vmem_limit_bytes=VMEM_LIMIT),
    )(*ex.arrs)


def _call(body, *, name, grid, in_specs, out_specs, out_shape, args, sem, scratch=(), ride=None, aliases=None):
    n_in, n_out, n_scr = len(in_specs), len(out_specs), len(scratch)
    if ride is None:
        outs = pl.pallas_call(
            body, name=name, grid=grid, in_specs=list(in_specs), out_specs=tuple(out_specs),
            out_shape=tuple(out_shape), scratch_shapes=list(scratch), input_output_aliases=aliases or {},
            compiler_params=_params(*sem))(*args)
        return tuple(outs), ()
    nx = len(ride.arrs)
    into = getattr(ride, "into", [])
    hbm = pl.BlockSpec(memory_space=pl.ANY)
    aliases = dict(aliases or {})
    aliases.update({n_in + nx + i: n_out + i for i in range(len(into))})

    def hosted(*refs):
        ins, x_in = refs[:n_in], refs[n_in:n_in + nx]
        o0 = n_in + nx + len(into)
        outs, x_out = refs[o0:o0 + n_out], refs[o0 + n_out:o0 + n_out + nx]
        s0 = o0 + n_out + nx
        scr, x_sem = refs[s0:s0 + n_scr], refs[s0 + n_scr:]
        ids = [pl.program_id(i) for i in range(len(grid))]
        first = functools.reduce(jnp.logical_and, [i == 0 for i in ids])
        last = functools.reduce(jnp.logical_and, [i == g - 1 for i, g in zip(ids, grid)])

        @pl.when(first)
        def _():
            ride.start(x_in, x_out, x_sem)

        body(*ins, *outs, *scr)

        @pl.when(last)
        def _():
            ride.finish(x_in, x_out, x_sem)

    outs = pl.pallas_call(
        hosted, name=name, grid=grid, in_specs=list(in_specs) + [hbm] * (nx + len(into)),
        out_specs=tuple(list(out_specs) + [hbm] * nx), out_shape=tuple(list(out_shape) + ride.out_shape()),
        scratch_shapes=list(scratch) + _exchange_scratch(nx), input_output_aliases=aliases,
        compiler_params=_params(*(("arbitrary",) * len(grid))))(*args, *ride.arrs, *into)
    return tuple(outs[:n_out]), tuple(outs[n_out:])


def _pack(parts):
    flat = jnp.concatenate([p.reshape(-1).astype(F32) for p in parts])
    tile = SUBLANES * LANES
    pad = (-flat.shape[0]) % tile
    return jnp.pad(flat, (0, pad)).reshape(-1, LANES)


def _unpack(buf, shapes):
    flat = buf.reshape(-1)
    out, off = [], 0
    for shp in shapes:
        size = math.prod(shp)
        out.append(flat[off:off + size].reshape(shp))
        off += size
    return out


KS_FLAT = (0, 1, 4, 5, 2, 3)
KS_DIAG = (6, 7)


def _local_step(x, target, wa, wo, ws, wos, rel_bias, conv_w, conv_b, dt_bias, a_log, d_skip, norm_w, ln_g, ln_b,
                dist=False):
    s, d = x.shape
    da = wo.shape[-2]
    heads = da // HEAD_DIM
    qkv_cols = 3 * N_GROUPS_ATTN * da
    d_inner = wos.shape[0] * (NDEV if dist else 1)
    conv_dim = conv_w.shape[1]
    ssm_heads = dt_bias.shape[1]
    hpg = ssm_heads // SSM_GROUPS
    gn = (conv_dim - d_inner) // 2
    n_state = gn // SSM_GROUPS
    gw = d_inner // SSM_GROUPS
    p = gw // hpg
    in_ssm = d_inner + conv_dim + ssm_heads
    xb, xbt = _cast_bf16(x, name="cast_x", with_transpose=True)

    per_dev = in_ssm // NDEV
    third = (per_dev // 3) // 16 * 16
    band_rows = (third, third, per_dev - 2 * third)
    qkvs, ws_all, row0 = [], None, 0
    for g in range(N_GROUPS_ATTN):
        ride = None
        if dist:
            ride = _Gather([ws[row0:row0 + band_rows[g]]], rows=(row0, per_dev), into=ws_all)
            row0 += band_rows[g]
        got = _mm(xb, wa, name=f"mm_qkv_g{g}", out_dtype=BF16, n_off=g * 3 * da, n_out=3 * da, ride=ride)
        if dist:
            ws_all = list(got[1])
            got = got[0]
        qkvs.append(got)
    if dist:
        ws = ws_all[0].reshape(in_ssm, d)
    gate = _mm(xb, wa, name="mm_gate", out_dtype=F32, n_off=qkv_cols, n_out=da)
    bias, bucket = _bias_tables(rel_bias, heads)
    os_, ls_ = [], []
    for g, (_, dil) in enumerate(ATTN_PATTERNS):
        ride = _Gather([wo]) if dist and g == 0 else None
        o, l, rode = _attn_fwd_group(qkvs[g], bias[g], g, dil, da, ride=ride)
        if rode:
            (wo,) = rode
        os_.append(o)
        ls_.append(l)
    o, lse, y, yt = _attn_combine(os_, ls_, gate)
    h1 = _mm(y, wo, name="mm_out_attn", out_dtype=F32)
    x1, x1b = _ln_fwd(x, h1, ln_g[0:1], ln_b[0:1], name="ln1_fwd")

    if dist:
        proj, (wos_slabs,) = _mm(x1b, ws, name="mm_in_ssm", out_dtype=F32, trans_b=True, ride=_Gather([wos]))
        wos = wos_slabs.reshape(d_inner, d)
    else:
        proj = _mm(x1b, ws, name="mm_in_ssm", out_dtype=F32, trans_b=True)
    xbc = _conv_fwd(proj, conv_w, conv_b, d_inner)
    dt = _dt_fwd(proj, dt_bias, d_inner + conv_dim)
    dt_g = dt.reshape(s, SSM_GROUPS, hpg).transpose(1, 0, 2)
    dt_gt = dt.reshape(s, SSM_GROUPS, hpg).transpose(1, 2, 0)
    a = -jnp.exp(a_log)
    a_g = a.reshape(SSM_GROUPS, 1, hpg)
    a_gt = a.reshape(SSM_GROUPS, hpg, 1)
    dskip_e = jnp.repeat(d_skip.reshape(SSM_GROUPS, 1, hpg), p, axis=2)
    yn, yscan, states, ynt = _ssd_fwd(xbc, proj, dt_g, dt_gt, a_g, a_gt, dskip_e, norm_w, d_inner, n_state)
    h2 = _mm(yn, wos, name="mm_out_ssm", out_dtype=F32)

    du2, du2b, dg1, db1, loss_t = _ln_bwd(x1, h2, ln_g[1:2], ln_b[1:2], target, with_loss=True, name="ln2_loss_bwd")
    loss = loss_t[0, 0]
    dyn = _mm(du2b, wos, name="mm_dyn", out_dtype=F32, trans_b=True)
    g_wos = _mm(ynt, du2b, name="mm_dw_out_ssm", out_dtype=BF16)
    parts = {}
    dyscan, dproj_ssm, g_norm, g_dskip, dproj_t = _ssd_epilogue_bwd(dyn, yscan, xbc, proj, dskip_e, norm_w, hpg)
    ride = _Scatter([g_wos.reshape(NDEV, d_inner // NDEV, d)]) if dist else None
    (dxs, d_bm, d_cm, ddt_g, g_a), rode = _ssd_scan_bwd(xbc, dt_g, dt_gt, a_g, a_gt, states, dyscan, dskip_e,
                                                         d_inner, n_state, ride=ride)
    parts["w_out_ssm"] = [list(rode)]
    g_alog = g_a.reshape(1, ssm_heads) * a
    dproj_ssm, g_conv_w, g_conv_b, dproj_t = _conv_bwd(proj, conv_w, conv_b, (dxs, d_bm, d_cm), d_inner, dproj_ssm,
                                                       dproj_t)
    ddt = ddt_g.transpose(1, 0, 2).reshape(s, ssm_heads)
    dproj_ssm, g_dtb, dproj_t = _dt_bwd(proj, dt_bias, ddt, d_inner + conv_dim, dproj_ssm, dproj_t)
    g_ws = _mm(dproj_t, x1b, name="mm_dw_in_ssm", out_dtype=BF16)
    if dist:
        g_ws_slabs = g_ws.reshape(NDEV, per_dev, d)
        dx1, near = _mm(dproj_ssm, ws, name="mm_dx1", out_dtype=F32, res=du2, res_scale=DEEPNORM_ALPHA,
                        ride=_Scatter([g_ws_slabs], KS_FLAT))
    else:
        dx1 = _mm(dproj_ssm, ws, name="mm_dx1", out_dtype=F32, res=du2, res_scale=DEEPNORM_ALPHA)

    du1, du1b, dg0, db0 = _ln_bwd(x, h1, ln_g[0:1], ln_b[0:1], dx1, with_loss=False, name="ln1_bwd")
    dy = _mm(du1b, wo, name="mm_dy", out_dtype=F32, trans_b=True)
    g_wo = _mm(yt, du1b, name="mm_dw_out_attn", out_dtype=BF16, slab_out=NDEV)
    do, dgate, dd = _attn_bwd_prep(dy, o, gate)
    rides = [_Scatter([g_ws_slabs], KS_DIAG[0:1]), _Scatter([g_wo]), None] if dist else [None] * N_GROUPS_ATTN
    dparts, dss, rode_attn = [], [], []
    for g, (_, dil) in enumerate(ATTN_PATTERNS):
        (dq, dk, dv, ds), rode = _attn_bwd_group(qkvs[g], do, lse, dd, bias[g], g, dil, da, ride=rides[g])
        dparts += [dq, dk, dv]
        dss.append(ds)
        rode_attn += list(rode)
    g_bias = _bias_bwd(jnp.stack(dss), bucket)
    g_rel_bias = g_bias.transpose(2, 0, 1).reshape(NUM_BUCKETS, N_GROUPS_ATTN * heads)
    dproj_attn = jnp.concatenate(dparts + [dgate], axis=1)
    pending = None
    if dist:
        parts["w_out_attn"] = [rode_attn[1:]]
        half = d // 2
        g_top, (diag_b,) = _mm(xbt[:half], dproj_attn, name="mm_dw_in_attn_top", out_dtype=BF16, slab_out=NDEV,
                               ride=_Scatter([g_ws_slabs], KS_DIAG[1:2]))
        parts["w_in_ssm"] = [[near[0], rode_attn[0], diag_b]]
        g_bot, (top_a,) = _mm(xbt[half:], dproj_attn, name="mm_dw_in_attn_bottom", out_dtype=BF16, slab_out=NDEV,
                              ride=_Scatter([g_top], KS_FLAT))
        dx, (top_b, bot_a) = _mm(dproj_attn, wa, name="mm_dx", out_dtype=F32, trans_b=True, res=du1,
                                 res_scale=DEEPNORM_ALPHA, ride=_Scatter([g_top, g_bot], [KS_DIAG, KS_FLAT]))
        parts["w_in_attn"] = [[top_a, top_b], [bot_a]]
        pending = _Scatter([g_bot], KS_DIAG)
    else:
        g_wa = _mm(xbt, dproj_attn, name="mm_dw_in_attn", out_dtype=BF16, slab_out=NDEV)
        dx = _mm(dproj_attn, wa, name="mm_dx", out_dtype=F32, trans_b=True, res=du1, res_scale=DEEPNORM_ALPHA)

    g_ln_g = jnp.concatenate([dg0, dg1], axis=0)
    g_ln_b = jnp.concatenate([db0, db1], axis=0)
    small = dict(rel_bias=g_rel_bias, dt_bias=g_dtb, a_log=g_alog, d_skip=g_dskip.reshape(1, ssm_heads),
                 ln_g=g_ln_g, ln_b=g_ln_b, conv_w=g_conv_w, conv_b=g_conv_b, ssm_norm_w=g_norm)
    if dist:
        return loss, dx, parts, pending, small
    return loss, dx, g_wa, g_wo, g_ws, g_wos, small


REPLICATED = ("rel_bias", "dt_bias", "a_log", "d_skip", "ln_g", "ln_b")
SHARDED_SMALL = ("conv_w", "conv_b", "ssm_norm_w")


def kernel(x, w_in_attn, w_out_attn, rel_bias, w_in_ssm, conv_w, conv_b, dt_bias, a_log, d_skip, ssm_norm_w, w_out_ssm, ln_g, ln_b, loss_target, m_w_in_attn, m_w_out_attn, m_rel_bias, m_w_in_ssm, m_conv_w, m_conv_b, m_dt_bias, m_a_log, m_d_skip, m_ssm_norm_w, m_w_out_ssm, m_ln_g, m_ln_b, v_w_in_attn, v_w_out_attn, v_rel_bias, v_w_in_ssm, v_conv_w, v_conv_b, v_dt_bias, v_a_log, v_d_skip, v_ssm_norm_w, v_w_out_ssm, v_ln_g, v_ln_b):
    w = dict(w_in_attn=w_in_attn, w_out_attn=w_out_attn, rel_bias=rel_bias, w_in_ssm=w_in_ssm, conv_w=conv_w,
             conv_b=conv_b, dt_bias=dt_bias, a_log=a_log, d_skip=d_skip, ssm_norm_w=ssm_norm_w,
             w_out_ssm=w_out_ssm, ln_g=ln_g, ln_b=ln_b)
    m = dict(w_in_attn=m_w_in_attn, w_out_attn=m_w_out_attn, rel_bias=m_rel_bias, w_in_ssm=m_w_in_ssm,
             conv_w=m_conv_w, conv_b=m_conv_b, dt_bias=m_dt_bias, a_log=m_a_log, d_skip=m_d_skip,
             ssm_norm_w=m_ssm_norm_w, w_out_ssm=m_w_out_ssm, ln_g=m_ln_g, ln_b=m_ln_b)
    v = dict(w_in_attn=v_w_in_attn, w_out_attn=v_w_out_attn, rel_bias=v_rel_bias, w_in_ssm=v_w_in_ssm,
             conv_w=v_conv_w, conv_b=v_conv_b, dt_bias=v_dt_bias, a_log=v_a_log, d_skip=v_d_skip,
             ssm_norm_w=v_ssm_norm_w, w_out_ssm=v_w_out_ssm, ln_g=v_ln_g, ln_b=v_ln_b)
    me = _lin((lax.axis_index("x"), lax.axis_index("y"), lax.axis_index("c")))
    d = x.shape[2]
    big = ("w_in_attn", "w_out_attn", "w_in_ssm", "w_out_ssm")

    for t in (w, m, v):
        t["w_in_ssm"] = t["w_in_ssm"].transpose(0, 2, 1)
    shards = {k: _cast_bf16(w[k], name=f"cast_{k}") for k in big}
    (wa,) = _exchange_alone(_Gather([shards["w_in_attn"]]), name="gather_w_in_attn")
    cpd = conv_w.shape[2]
    npd = ssm_norm_w.shape[1]
    small_shapes = [(CONV_WIDTH, cpd), (1, cpd), (1, npd)]
    (small_all,) = _exchange_alone(_Gather([_pack([conv_w[0], conv_b, ssm_norm_w])]), name="gather_small_weights",
                                   in_vmem=True)
    small_parts = [_unpack(small_all[i], small_shapes) for i in range(NDEV)]
    conv_w_full = jnp.concatenate([p[0] for p in small_parts], axis=1)
    conv_b_full = jnp.concatenate([p[1] for p in small_parts], axis=1)
    norm_w_full = jnp.concatenate([p[2] for p in small_parts], axis=1)

    loss, dx, parts, pending, small = _local_step(
        x[0], loss_target[0], wa, shards["w_out_attn"], shards["w_in_ssm"], shards["w_out_ssm"], rel_bias,
        conv_w_full, conv_b_full, dt_bias[0:1], a_log[0:1], d_skip[0:1], norm_w_full, ln_g, ln_b, dist=True)
    loss = lax.psum(loss, MESH_AXES)
    out = {}
    out["w_in_ssm"], late = _adamw_sum(parts["w_in_ssm"], w["w_in_ssm"], m["w_in_ssm"], v["w_in_ssm"],
                                       name="adamw_w_in_ssm", ride=pending)
    out["w_in_ssm"] = tuple(t.transpose(0, 2, 1) for t in out["w_in_ssm"])
    parts["w_in_attn"][1] += list(late)
    for k in ("w_out_ssm", "w_out_attn", "w_in_attn"):
        out[k] = _adamw_sum(parts[k], w[k], m[k], v[k], name=f"adamw_{k}")

    order = REPLICATED + SHARDED_SMALL
    g_shapes = [small[k].shape for k in order]
    (g_all,) = _exchange_alone(_Gather([_pack([small[k] for k in order])]), name="gather_small_grads", in_vmem=True)
    g_sum = dict(zip(order, _unpack(_sum_slots(g_all, name="sum_small_grads"), g_shapes)))
    g_mine = {k: g_sum[k] for k in REPLICATED}
    g_mine["conv_w"] = lax.dynamic_slice_in_dim(g_sum["conv_w"], me * cpd, cpd, axis=1)
    g_mine["conv_b"] = lax.dynamic_slice_in_dim(g_sum["conv_b"], me * cpd, cpd, axis=1)
    g_mine["ssm_norm_w"] = lax.dynamic_slice_in_dim(g_sum["ssm_norm_w"], me * npd, npd, axis=1)
    w_shapes = [w[k].shape for k in order]
    g_pack = _pack([g_mine[k] for k in order])
    d_p, m_p, v_p = _adamw_small(g_pack, _pack([w[k] for k in order]), _pack([m[k] for k in order]),
                                 _pack([v[k] for k in order]), name="adamw_small")
    for k, gk, dk, mk, vk in zip(order, _unpack(g_pack, w_shapes), _unpack(d_p, w_shapes), _unpack(m_p, w_shapes),
                                 _unpack(v_p, w_shapes)):
        out[k] = (gk, dk, mk, vk)

    names = ("w_in_attn", "w_out_attn", "rel_bias", "w_in_ssm", "conv_w", "conv_b", "dt_bias", "a_log", "d_skip",
             "ssm_norm_w", "w_out_ssm", "ln_g", "ln_b")
    res = [loss, dx[None]]
    for i in range(4):
        res += [out[k][i] for k in names]
    return tuple(res)
```

```python
import functools
import math

import jax
import jax.numpy as jnp
from jax import lax
from jax.experimental import pallas as pl
from jax.experimental.pallas import tpu as pltpu

F32 = jnp.float32
BF16 = jnp.bfloat16
MESH_AXES = ("x", "y", "c")
NDEV = 8
NPEER = NDEV - 1
LANES = 128
SUBLANES = 8
VMEM_LIMIT = 52 * 1024 * 1024
MM_VMEM_BUDGET = 40 * 1024 * 1024
MM_TK_MAX = 4096
MM_TN_MAX = 1024

ATTN_PATTERNS = ((128, 1), (512, 4), (2048, 16))
N_GROUPS_ATTN = 3
HEAD_DIM = 128
ATTN_BLOCK = 128
ATTN_ROWS_TIMES_HEADS = 8192
NUM_BUCKETS = 32
MAX_DISTANCE = 2048
SSM_GROUPS = 8
CONV_WIDTH = 4
CHUNK = 128
DEPTH = 2
DEEPNORM_ALPHA = (2 * DEPTH) ** 0.25
LN_EPS = 1e-5
RMS_EPS = 1e-5
NEG_INF = -1e30
ADAM_LR = 0.001
ADAM_B1 = 0.9
ADAM_B2 = 0.999
ADAM_EPS = 1e-08
ADAM_WD = 0.01
ADAM_STEP = 10
HIGHEST = lax.Precision.HIGHEST


def _params(*sem):
    return pltpu.CompilerParams(dimension_semantics=sem, vmem_limit_bytes=VMEM_LIMIT)


def _pick(n, prefs):
    for p in prefs:
        if n % p == 0:
            return p
    return n


def _row_tile(r, limit):
    return max(t for t in range(2 * SUBLANES, limit + 1, 2 * SUBLANES) if r % t == 0)


def _dot(a, b):
    return jnp.dot(a, b, preferred_element_type=F32)


def _dot_nt(a, b):
    return lax.dot_general(a, b, (((1,), (1,)), ((), ())), preferred_element_type=F32)


def _dot_tn(a, b):
    return lax.dot_general(a, b, (((0,), (0,)), ((), ())), preferred_element_type=F32)


def _sigmoid(x):
    return 1.0 / (1.0 + jnp.exp(-x))


def _mm(a, b, *, name, out_dtype, trans_b=False, slab_out=0, n_off=0, n_out=None,
        res=None, res_scale=1.0, ride=None):
    m, k = a.shape
    slab_b = b.ndim == 3
    if slab_b:
        ns = b.shape[0]
        if trans_b:
            n, kper = b.shape[1], b.shape[2]
            assert ns * kper == k
        else:
            nper = b.shape[2]
            n = ns * nper
            assert b.shape[1] == k
    else:
        n = b.shape[0] if trans_b else b.shape[1]
        assert (b.shape[1] if trans_b else b.shape[0]) == k
    n_out = n if n_out is None else n_out
    tm = _pick(m, (1024, 640, 512, 256, 128))
    nconstraint = math.gcd(n_out, n_off) if n_off else n_out
    if slab_b and not trans_b:
        nconstraint = math.gcd(nconstraint, nper)
    if slab_out:
        nconstraint = math.gcd(nconstraint, n_out // slab_out)
    kconstraint = kper if (slab_b and trans_b) else k
    tk = max(t for t in range(LANES, min(kconstraint, MM_TK_MAX) + 1, LANES) if kconstraint % t == 0)
    nk = k // tk
    out_bytes = jnp.dtype(out_dtype).itemsize

    def vmem_bytes(t):
        return (2 * 2 * tk * (tm + t) + 2 * tm * t * out_bytes + (4 * tm * t if nk > 1 else 0)
                + (2 * 4 * tm * t if res is not None else 0))

    fits = [t for t in range(LANES, min(nconstraint, MM_TN_MAX) + 1, LANES)
            if nconstraint % t == 0 and vmem_bytes(t) <= MM_VMEM_BUDGET]
    tn = max(fits)
    nb0 = n_off // tn
    grid = (m // tm, n_out // tn, nk)

    a_spec = pl.BlockSpec((tm, tk), lambda i, j, kk: (i, kk))
    if slab_b and not trans_b:
        nps = nper // tn
        b_spec = pl.BlockSpec((None, tk, tn), lambda i, j, kk: ((j + nb0) // nps, kk, (j + nb0) % nps))
    elif slab_b and trans_b:
        kps = kper // tk
        b_spec = pl.BlockSpec((None, tn, tk), lambda i, j, kk: (kk // kps, j + nb0, kk % kps))
    elif trans_b:
        b_spec = pl.BlockSpec((tn, tk), lambda i, j, kk: (j + nb0, kk))
    else:
        b_spec = pl.BlockSpec((tk, tn), lambda i, j, kk: (kk, j + nb0))
    if slab_out:
        ops = (n_out // slab_out) // tn
        o_spec = pl.BlockSpec((None, tm, tn), lambda i, j, kk: (j // ops, i, j % ops))
        o_shape = jax.ShapeDtypeStruct((slab_out, m, n_out // slab_out), out_dtype)
    else:
        o_spec = pl.BlockSpec((tm, tn), lambda i, j, kk: (i, j))
        o_shape = jax.ShapeDtypeStruct((m, n_out), out_dtype)
    in_specs = [a_spec, b_spec]
    args = [a, b]
    if res is not None:
        in_specs.append(pl.BlockSpec((tm, tn), lambda i, j, kk: (i, j)))
        args.append(res)

    def body(*refs):
        a_ref, b_ref = refs[0], refs[1]
        r_ref = refs[2] if res is not None else None
        o_ref = refs[3] if res is not None else refs[2]
        av = a_ref[...].astype(BF16)
        bv = b_ref[...].astype(BF16)
        part = _dot_nt(av, bv) if trans_b else _dot(av, bv)

        def finish(r):
            if res is not None:
                r = r + res_scale * r_ref[...]
            o_ref[...] = r.astype(out_dtype)

        if nk == 1:
            finish(part)
            return
        acc = refs[-1]
        kk = pl.program_id(2)

        @pl.when(kk == 0)
        def _():
            acc[...] = part

        @pl.when(kk > 0)
        def _():
            acc[...] += part

        @pl.when(kk == nk - 1)
        def _():
            finish(acc[...])

    outs, rode = _call(
        body, name=name, grid=grid, in_specs=in_specs, out_specs=[o_spec], out_shape=[o_shape], args=args,
        scratch=[pltpu.VMEM((tm, tn), F32)] if nk > 1 else [], ride=ride,
        sem=("parallel", "parallel", "arbitrary"))
    return (outs[0], rode) if ride is not None else outs[0]


def _cast_bf16(w, *, name, with_transpose=False):
    r, c = w.shape[-2:]
    tr = _row_tile(r, 512)

    def body(w_ref, o_ref, *t_ref):
        v = w_ref[...]
        o_ref[...] = v.astype(BF16)
        if with_transpose:
            t_ref[0][...] = v.T.astype(BF16)

    in_spec = (pl.BlockSpec((None, tr, c), lambda i: (0, i, 0)) if w.ndim == 3
               else pl.BlockSpec((tr, c), lambda i: (i, 0)))
    out_specs = [pl.BlockSpec((tr, c), lambda i: (i, 0))]
    out_shape = [jax.ShapeDtypeStruct((r, c), BF16)]
    if with_transpose:
        out_specs.append(pl.BlockSpec((c, tr), lambda i: (0, i)))
        out_shape.append(jax.ShapeDtypeStruct((c, r), BF16))
    out = pl.pallas_call(
        body, name=name, grid=(r // tr,), in_specs=[in_spec], out_specs=tuple(out_specs),
        out_shape=tuple(out_shape), compiler_params=_params("parallel"),
    )(w)
    return out if with_transpose else out[0]


def _adam_math(w, g, m, v):
    m2 = ADAM_B1 * m + (1.0 - ADAM_B1) * g
    v2 = ADAM_B2 * v + (1.0 - ADAM_B2) * (g * g)
    m_hat = m2 / (1.0 - ADAM_B1 ** ADAM_STEP)
    v_hat = v2 / (1.0 - ADAM_B2 ** ADAM_STEP)
    delta = -ADAM_LR * (m_hat / (jnp.sqrt(v_hat) + ADAM_EPS) + ADAM_WD * w)
    return delta, m2, v2


def _adamw_sum(bands, w, m, v, *, name, ride=None):
    _, r, c = w.shape
    nband = len(bands)
    rows = r // nband
    tr = _row_tile(rows, 128)
    tc = c if (c % LANES or c <= 2560) else _pick(c, (2048, 1024, 512, 256, 128))
    nt = rows // tr
    flat = [p for band in bands for p in band]

    def body(*refs):
        p_refs = refs[:len(flat)]
        w_ref, m_ref, v_ref, g_out, d_out, m_out, v_out = refs[len(flat):]
        i = pl.program_id(0)
        g, at = None, 0
        for q, band in enumerate(bands):
            gq = None
            for p_ref in p_refs[at:at + len(band)]:
                for s in range(p_ref.shape[0]):
                    t = p_ref[s].astype(F32)
                    gq = t if gq is None else gq + t
            at += len(band)
            g = gq if q == 0 else jnp.where(i >= q * nt, gq, g)
        d, m2, v2 = _adam_math(w_ref[...], g, m_ref[...], v_ref[...])
        g_out[...] = g
        d_out[...] = d
        m_out[...] = m2
        v_out[...] = v2

    def band_spec(p, q):
        return pl.BlockSpec((p.shape[0], tr, tc), lambda i, j: (0, jnp.clip(i - q * nt, 0, nt - 1), j))

    spec = pl.BlockSpec((None, tr, tc), lambda i, j: (0, i, j))
    shp = jax.ShapeDtypeStruct((1, r, c), F32)
    outs, rode = _call(
        body, name=name, grid=(r // tr, c // tc),
        in_specs=[band_spec(p, q) for q, band in enumerate(bands) for p in band] + [spec, spec, spec],
        out_specs=[spec, spec, spec, spec], out_shape=[shp, shp, shp, shp], args=(*flat, w, m, v),
        sem=("parallel", "parallel"), ride=ride)
    return (outs, rode) if ride is not None else outs


def _adamw_small(g, w, m, v, *, name):
    shp = jax.ShapeDtypeStruct(w.shape, F32)

    def body(g_ref, w_ref, m_ref, v_ref, d_out, m_out, v_out):
        d, m2, v2 = _adam_math(w_ref[...], g_ref[...], m_ref[...], v_ref[...])
        d_out[...] = d
        m_out[...] = m2
        v_out[...] = v2

    return pl.pallas_call(body, name=name, out_shape=(shp, shp, shp),
                          compiler_params=pltpu.CompilerParams(vmem_limit_bytes=VMEM_LIMIT))(g, w, m, v)


def _sum_slots(parts, *, name):
    _, r, c = parts.shape

    def body(p_ref, o_ref):
        g = p_ref[0]
        for s in range(1, NDEV):
            g = g + p_ref[s]
        o_ref[...] = g

    return pl.pallas_call(body, name=name, out_shape=jax.ShapeDtypeStruct((r, c), F32),
                          compiler_params=pltpu.CompilerParams(vmem_limit_bytes=VMEM_LIMIT))(parts)


def _ln_parts(u):
    mu = jnp.mean(u, axis=-1, keepdims=True)
    xc = u - mu
    var = jnp.mean(xc * xc, axis=-1, keepdims=True)
    rstd = lax.rsqrt(var + LN_EPS)
    return xc * rstd, rstd


def _ln_fwd(xin, h, g, b, *, name):
    s, d = xin.shape
    tm = _pick(s, (128,))

    def body(x_ref, h_ref, g_ref, b_ref, o_ref, ob_ref):
        xhat, _ = _ln_parts(DEEPNORM_ALPHA * x_ref[...] + h_ref[...])
        o = xhat * g_ref[...] + b_ref[...]
        o_ref[...] = o
        ob_ref[...] = o.astype(BF16)

    row = pl.BlockSpec((tm, d), lambda i: (i, 0))
    vec = pl.BlockSpec((1, d), lambda i: (0, 0))
    return pl.pallas_call(
        body, name=name, grid=(s // tm,), in_specs=[row, row, vec, vec], out_specs=(row, row),
        out_shape=(jax.ShapeDtypeStruct((s, d), F32), jax.ShapeDtypeStruct((s, d), BF16)),
        compiler_params=_params("parallel"),
    )(xin, h, g, b)


def _ln_bwd(xin, h, g, b, cot, *, with_loss, name):
    s, d = xin.shape
    tm = _pick(s, (128,))

    def body(x_ref, h_ref, g_ref, b_ref, c_ref, du_ref, dub_ref, dg_ref, db_ref, *rest):
        i = pl.program_id(0)
        xhat, rstd = _ln_parts(DEEPNORM_ALPHA * x_ref[...] + h_ref[...])
        gv = g_ref[...]
        if with_loss:
            diff = xhat * gv + b_ref[...] - c_ref[...]
            part = 0.5 * jnp.sum(jnp.mean(diff * diff, axis=-1, keepdims=True), axis=0, keepdims=True)
            dout = diff / d
        else:
            dout = c_ref[...]

        @pl.when(i == 0)
        def _():
            dg_ref[...] = jnp.zeros_like(dg_ref)
            db_ref[...] = jnp.zeros_like(db_ref)
            if with_loss:
                rest[0][...] = jnp.zeros_like(rest[0])

        dg_ref[...] += jnp.sum(dout * xhat, axis=0, keepdims=True)
        db_ref[...] += jnp.sum(dout, axis=0, keepdims=True)
        if with_loss:
            rest[0][...] += jnp.broadcast_to(part, rest[0].shape)
        dxh = dout * gv
        du = rstd * (dxh - jnp.mean(dxh, axis=-1, keepdims=True)
                     - xhat * jnp.mean(dxh * xhat, axis=-1, keepdims=True))
        du_ref[...] = du
        dub_ref[...] = du.astype(BF16)

    row = pl.BlockSpec((tm, d), lambda i: (i, 0))
    vec = pl.BlockSpec((1, d), lambda i: (0, 0))
    out_specs = [row, row, vec, vec]
    out_shape = [jax.ShapeDtypeStruct((s, d), F32), jax.ShapeDtypeStruct((s, d), BF16),
                 jax.ShapeDtypeStruct((1, d), F32), jax.ShapeDtypeStruct((1, d), F32)]
    if with_loss:
        out_specs.append(pl.BlockSpec((SUBLANES, LANES), lambda i: (0, 0)))
        out_shape.append(jax.ShapeDtypeStruct((SUBLANES, LANES), F32))
    return pl.pallas_call(
        body, name=name, grid=(s // tm,), in_specs=[row, row, vec, vec, row],
        out_specs=tuple(out_specs), out_shape=tuple(out_shape),
        compiler_params=_params("arbitrary"),
    )(xin, h, g, b, cot)


def t5_causal_bucket(dist):
    max_exact = NUM_BUCKETS // 2
    d_f = jnp.maximum(dist, 1).astype(jnp.float32)
    large = max_exact + (jnp.log(d_f / max_exact) / math.log(MAX_DISTANCE / max_exact)
                         * (NUM_BUCKETS - max_exact)).astype(jnp.int32)
    large = jnp.minimum(large, NUM_BUCKETS - 1)
    return jnp.where(dist < max_exact, dist, large)


def _bias_tables(rel_bias, heads):
    qi = lax.broadcasted_iota(jnp.int32, (ATTN_BLOCK, 2 * ATTN_BLOCK), 0)
    ki = lax.broadcasted_iota(jnp.int32, (ATTN_BLOCK, 2 * ATTN_BLOCK), 1)
    delta = ATTN_BLOCK + qi - ki
    buckets = []
    for window, dilation in ATTN_PATTERNS:
        span = window // dilation
        assert span == ATTN_BLOCK
        band = (delta >= 0) & (delta <= span)
        buckets.append(jnp.where(band, t5_causal_bucket(jnp.clip(delta, 0, None) * dilation), -1))
    bucket = jnp.stack(buckets).astype(jnp.int32)

    def body(bk_ref, tbl_ref, o_ref):
        col = pl.program_id(0) * heads + pl.program_id(1)
        bk = bk_ref[...]
        acc = jnp.full(bk.shape, NEG_INF, F32)
        for b in range(NUM_BUCKETS):
            acc = jnp.where(bk == b, tbl_ref[b, col], acc)
        o_ref[...] = acc

    tile = (None, ATTN_BLOCK, 2 * ATTN_BLOCK)
    bias = pl.pallas_call(
        body, name="bias_fwd", grid=(N_GROUPS_ATTN, heads),
        in_specs=[pl.BlockSpec(tile, lambda g, h: (g, 0, 0)), pl.BlockSpec(memory_space=pltpu.SMEM)],
        out_specs=pl.BlockSpec((None,) + tile, lambda g, h: (g, h, 0, 0)),
        out_shape=jax.ShapeDtypeStruct((N_GROUPS_ATTN, heads, ATTN_BLOCK, 2 * ATTN_BLOCK), F32),
        compiler_params=_params("parallel", "parallel"),
    )(bucket, rel_bias)
    return bias, bucket


def _dilated_view(qkv, g, dilation, da):
    del g
    return qkv.reshape(qkv.shape[0] // dilation, dilation * 3 * da), 3 * (da // HEAD_DIM), 0


def _heads_per_step(l, heads):
    for hps in (4, 2, 1):
        if heads % hps == 0 and l * hps <= ATTN_ROWS_TIMES_HEADS:
            return hps
    return 1


def _attn_fwd_group(qkv, bias_g, g, dilation, da, ride=None):
    s = qkv.shape[0]
    heads = da // HEAD_DIM
    l = s // dilation
    nb = l // ATTN_BLOCK
    view, cpb, base = _dilated_view(qkv, g, dilation, da)

    hps = _heads_per_step(l, heads)
    lanes = [slice(i * HEAD_DIM, (i + 1) * HEAD_DIM) for i in range(hps)]

    def body(q_ref, k_ref, v_ref, b_ref, o_ref, l_ref):
        scale = HEAD_DIM ** -0.5

        def block(rows, keys, first):
            q, k, v = q_ref[rows, :], k_ref[keys, :], v_ref[keys, :]
            bias = [b_ref[i, :, ATTN_BLOCK:2 * ATTN_BLOCK] if first else b_ref[i] for i in range(hps)]
            sc = [_dot_nt(q[:, hl], k[:, hl]) * scale + bias[i] for i, hl in enumerate(lanes)]
            mx = [jnp.max(t, axis=-1, keepdims=True) for t in sc]
            p = [jnp.exp(t - m) for t, m in zip(sc, mx)]
            den = [jnp.sum(t, axis=-1, keepdims=True) for t in p]
            for i, hl in enumerate(lanes):
                o_ref[rows, hl] = _dot((p[i] * (1.0 / den[i])).astype(BF16), v[:, hl])
                l_ref[rows, hl] = jnp.broadcast_to(mx[i] + jnp.log(den[i]), (ATTN_BLOCK, HEAD_DIM))

        first = pl.ds(0, ATTN_BLOCK)
        block(first, first, True)

        def step(j, carry):
            r0 = pl.multiple_of(j * ATTN_BLOCK, ATTN_BLOCK)
            rk = pl.multiple_of((j - 1) * ATTN_BLOCK, ATTN_BLOCK)
            block(pl.ds(r0, ATTN_BLOCK), pl.ds(rk, 2 * ATTN_BLOCK), False)
            return carry

        if nb > 1:
            lax.fori_loop(1, nb, step, 0)

    def col(t):
        return lambda r, h: (0, (r * cpb + base + t * heads) // hps + h)

    blk = (l, hps * HEAD_DIM)
    out = pl.BlockSpec(blk, lambda r, h: (0, r * (heads // hps) + h))
    shp = jax.ShapeDtypeStruct((l, dilation * da), F32)
    (o, lse), rode = _call(
        body, name=f"attn_fwd_g{g}", grid=(dilation, heads // hps),
        in_specs=[pl.BlockSpec(blk, col(0)), pl.BlockSpec(blk, col(1)), pl.BlockSpec(blk, col(2)),
                  pl.BlockSpec((hps, ATTN_BLOCK, 2 * ATTN_BLOCK), lambda r, h: (h, 0, 0))],
        out_specs=[out, out], out_shape=[shp, shp], args=(view, view, view, bias_g),
        sem=("parallel", "parallel"), ride=ride)
    return o.reshape(s, da), lse.reshape(s, da), rode


def _attn_combine(os_, ls_, gate):
    s, da = gate.shape
    tm = _pick(s, (512, 256, 128))
    tc = _pick(da, (512, 256, 128))

    assert da // HEAD_DIM <= LANES
    per_step = tc // HEAD_DIM

    def body(o0, o1, o2, l0, l1, l2, g_ref, o_ref, l_ref, y_ref, yt_ref):
        j = pl.program_id(1)
        a0, a1, a2 = l0[...], l1[...], l2[...]
        mx = jnp.maximum(jnp.maximum(a0, a1), a2)
        e0, e1, e2 = jnp.exp(a0 - mx), jnp.exp(a1 - mx), jnp.exp(a2 - mx)
        den = e0 + e1 + e2
        o = (e0 * o0[...] + e1 * o1[...] + e2 * o2[...]) / den
        gv = g_ref[...]
        o_ref[...] = o
        y = o * (gv * _sigmoid(gv))
        y_ref[...] = y.astype(BF16)
        yt_ref[...] = y.T.astype(BF16)
        lse = mx + jnp.log(den)

        @pl.when(j == 0)
        def _():
            l_ref[...] = jnp.zeros_like(l_ref)

        lane = lax.broadcasted_iota(jnp.int32, (1, LANES), 1)
        acc = l_ref[...]
        for i in range(per_step):
            acc = jnp.where(lane == j * per_step + i, lse[:, i * HEAD_DIM:(i + 1) * HEAD_DIM], acc)
        l_ref[...] = acc

    spec = pl.BlockSpec((tm, tc), lambda i, j: (i, j))
    heads_spec = pl.BlockSpec((tm, LANES), lambda i, j: (i, 0))
    return pl.pallas_call(
        body, name="attn_combine", grid=(s // tm, da // tc), in_specs=[spec] * 7,
        out_specs=(spec, heads_spec, spec, pl.BlockSpec((tc, tm), lambda i, j: (j, i))),
        out_shape=(jax.ShapeDtypeStruct((s, da), F32), jax.ShapeDtypeStruct((s, LANES), F32),
                   jax.ShapeDtypeStruct((s, da), BF16), jax.ShapeDtypeStruct((da, s), BF16)),
        compiler_params=_params("parallel", "arbitrary"),
    )(*os_, *ls_, gate)


def _attn_bwd_prep(dy, o, gate):
    s, da = gate.shape
    tm = _pick(s, (512, 256, 128))

    def body(dy_ref, o_ref, g_ref, do_ref, dg_ref, dd_ref):
        j = pl.program_id(1)
        gv = g_ref[...]
        sg = _sigmoid(gv)
        dyv = dy_ref[...]
        ov = o_ref[...]
        do = dyv * (gv * sg)
        do_ref[...] = do.astype(BF16)
        dg_ref[...] = (dyv * ov * (sg * (1.0 + gv * (1.0 - sg)))).astype(BF16)

        @pl.when(j == 0)
        def _():
            dd_ref[...] = jnp.zeros_like(dd_ref)

        lane = lax.broadcasted_iota(jnp.int32, (1, LANES), 1)
        dd_ref[...] = jnp.where(lane == j, jnp.sum(do * ov, axis=-1, keepdims=True), dd_ref[...])

    spec = pl.BlockSpec((tm, HEAD_DIM), lambda i, j: (i, j))
    return pl.pallas_call(
        body, name="attn_bwd_prep", grid=(s // tm, da // HEAD_DIM), in_specs=[spec] * 3,
        out_specs=(spec, spec, pl.BlockSpec((tm, LANES), lambda i, j: (i, 0))),
        out_shape=(jax.ShapeDtypeStruct((s, da), BF16), jax.ShapeDtypeStruct((s, da), BF16),
                   jax.ShapeDtypeStruct((s, LANES), F32)),
        compiler_params=_params("parallel", "arbitrary"),
    )(dy, o, gate)


def _attn_bwd_group(qkv, do, lse, dd, bias_g, g, dilation, da, ride=None):
    s = qkv.shape[0]
    heads = da // HEAD_DIM
    l = s // dilation
    nb = l // ATTN_BLOCK
    view, cpb, base = _dilated_view(qkv, g, dilation, da)
    scale = HEAD_DIM ** -0.5
    hps = _heads_per_step(l, heads)
    lanes = [slice(i * HEAD_DIM, (i + 1) * HEAD_DIM) for i in range(hps)]

    def body(q_ref, k_ref, v_ref, do_ref, l_ref, dd_ref, b_ref, dq_ref, dk_ref, dv_ref, ds_ref, dk_acc, dv_acc):
        h0 = pl.program_id(0) * hps
        r = pl.program_id(1)
        lane = lax.broadcasted_iota(jnp.int32, (1, LANES), 1)

        @pl.when(r == 0)
        def _():
            ds_ref[...] = jnp.zeros_like(ds_ref)

        dk_acc[...] = jnp.zeros_like(dk_acc)
        dv_acc[...] = jnp.zeros_like(dv_acc)

        def block(rows, keys, first):
            q, k, v, dov = q_ref[rows, :], k_ref[keys, :], v_ref[keys, :], do_ref[rows, :]
            lse_all, dd_all = l_ref[rows, :], dd_ref[rows, :]
            pick = [(lane == h0 + i).astype(F32) for i in range(hps)]
            lrow = [jnp.sum(lse_all * m, axis=-1, keepdims=True) for m in pick]
            drow = [jnp.sum(dd_all * m, axis=-1, keepdims=True) for m in pick]
            bias = [b_ref[i, :, ATTN_BLOCK:2 * ATTN_BLOCK] if first else b_ref[i] for i in range(hps)]
            sc = [_dot_nt(q[:, hl], k[:, hl]) for hl in lanes]
            dp = [_dot_nt(dov[:, hl], v[:, hl]) for hl in lanes]
            p = [jnp.exp(sc[i] * scale + bias[i] - lrow[i]) for i in range(hps)]
            ds = [p[i] * (dp[i] - drow[i]) for i in range(hps)]
            dsb = [t.astype(BF16) for t in ds]
            pb = [t.astype(BF16) for t in p]
            for i, hl in enumerate(lanes):
                dq_ref[rows, hl] = (_dot(dsb[i], k[:, hl]) * scale).astype(BF16)
                dk_acc[keys, hl] += _dot_tn(dsb[i], q[:, hl]) * scale
                dv_acc[keys, hl] += _dot_tn(pb[i], dov[:, hl])
                if first:
                    ds_ref[i, :, ATTN_BLOCK:2 * ATTN_BLOCK] += ds[i]
                else:
                    ds_ref[i] += ds[i]

        first = pl.ds(0, ATTN_BLOCK)
        block(first, first, True)

        def step(j, carry):
            r0 = pl.multiple_of(j * ATTN_BLOCK, ATTN_BLOCK)
            rk = pl.multiple_of((j - 1) * ATTN_BLOCK, ATTN_BLOCK)
            block(pl.ds(r0, ATTN_BLOCK), pl.ds(rk, 2 * ATTN_BLOCK), False)
            return carry

        if nb > 1:
            lax.fori_loop(1, nb, step, 0)
        dk_ref[...] = dk_acc[...].astype(BF16)
        dv_ref[...] = dv_acc[...].astype(BF16)

    def col(t):
        return lambda h, r: (0, (r * cpb + base + t * heads) // hps + h)

    blk = (l, hps * HEAD_DIM)
    act = pl.BlockSpec(blk, lambda h, r: (0, r * (heads // hps) + h))
    per_head = pl.BlockSpec((l, LANES), lambda h, r: (0, r))
    tile = pl.BlockSpec((hps, ATTN_BLOCK, 2 * ATTN_BLOCK), lambda h, r: (h, 0, 0))
    shp = jax.ShapeDtypeStruct((l, dilation * da), BF16)
    (dq, dk, dv, ds), rode = _call(
        body, name=f"attn_bwd_g{g}", grid=(heads // hps, dilation),
        in_specs=[pl.BlockSpec(blk, col(0)), pl.BlockSpec(blk, col(1)), pl.BlockSpec(blk, col(2)), act,
                  per_head, per_head, tile],
        out_specs=[act, act, act, tile],
        out_shape=[shp, shp, shp, jax.ShapeDtypeStruct((heads, ATTN_BLOCK, 2 * ATTN_BLOCK), F32)],
        scratch=[pltpu.VMEM(blk, F32), pltpu.VMEM(blk, F32)], sem=("parallel", "arbitrary"), ride=ride,
        args=(view, view, view, do.reshape(l, dilation * da), lse.reshape(l, dilation * LANES),
              dd.reshape(l, dilation * LANES), bias_g))
    return (dq.reshape(s, da), dk.reshape(s, da), dv.reshape(s, da), ds), rode


def _bias_bwd(ds, bucket):
    ng, heads = ds.shape[0], ds.shape[1]

    def body(ds_ref, bk_ref, o_ref):
        bk = bk_ref[...]
        x = ds_ref[...]
        for b in range(NUM_BUCKETS):
            o_ref[:, b:b + 1] = jnp.sum(jnp.where(bk == b, x, 0.0), axis=(0, 1), keepdims=True)

    tile = (None, ATTN_BLOCK, 2 * ATTN_BLOCK)
    out = pl.pallas_call(
        body, name="bias_bwd", grid=(ng, heads),
        in_specs=[pl.BlockSpec((None,) + tile, lambda g, h: (g, h, 0, 0)), pl.BlockSpec(tile, lambda g, h: (g, 0, 0))],
        out_specs=pl.BlockSpec((None, None, 1, NUM_BUCKETS), lambda g, h: (g, h, 0, 0)),
        out_shape=jax.ShapeDtypeStruct((ng, heads, 1, NUM_BUCKETS), F32),
        compiler_params=_params("parallel", "parallel"),
    )(ds, bucket)
    return out.reshape(ng, heads, NUM_BUCKETS)


def _shift_rows(x, halo, s):
    r = pltpu.roll(x, s, axis=0)
    rh = pltpu.roll(halo, s, axis=0)
    row = lax.broadcasted_iota(jnp.int32, halo.shape, 0)
    top = jnp.where(row < s, rh, r[0:SUBLANES])
    return jnp.concatenate([top, r[SUBLANES:]], axis=0)


def _conv_out(x, halo, w, b):
    acc = b + w[CONV_WIDTH - 1:CONV_WIDTH] * x
    for kk in range(CONV_WIDTH - 1):
        acc = acc + w[kk:kk + 1] * _shift_rows(x, halo, CONV_WIDTH - 1 - kk)
    return acc


def _conv_fwd(proj, conv_w, conv_b, col0):
    s = proj.shape[0]
    c = conv_w.shape[1]
    ts = _pick(s, (1024, 512, 256, 128))
    tc = _pick(math.gcd(c, col0), (512, 256, 128))
    cb0 = col0 // tc
    hb = ts // SUBLANES

    def body(x_ref, h_ref, w_ref, b_ref, o_ref):
        i = pl.program_id(0)
        halo = jnp.where(i > 0, h_ref[...], 0.0)
        u = _conv_out(x_ref[...], halo, w_ref[...], b_ref[...])
        o_ref[...] = u * _sigmoid(u)

    return pl.pallas_call(
        body, name="conv_fwd", grid=(s // ts, c // tc),
        in_specs=[pl.BlockSpec((ts, tc), lambda i, j: (i, cb0 + j)),
                  pl.BlockSpec((SUBLANES, tc), lambda i, j: (jnp.maximum(i * hb - 1, 0), cb0 + j)),
                  pl.BlockSpec((CONV_WIDTH, tc), lambda i, j: (0, j)),
                  pl.BlockSpec((1, tc), lambda i, j: (0, j))],
        out_specs=pl.BlockSpec((ts, tc), lambda i, j: (i, j)),
        out_shape=jax.ShapeDtypeStruct((s, c), F32),
        compiler_params=_params("parallel", "parallel"),
    )(proj, proj, conv_w, conv_b)


def _conv_bwd(proj, conv_w, conv_b, dacts, col0, dproj, dproj_t):
    s = proj.shape[0]
    c = conv_w.shape[1]
    widths = [d.shape[1] for d in dacts]
    assert sum(widths) == c
    ts = _pick(s, (1024, 512, 256, 128))
    tc = _pick(math.gcd(math.gcd(c, col0), math.gcd(*widths)), (512, 256, 128))
    cb0 = col0 // tc
    hb = ts // SUBLANES
    nblk = s // ts
    ext = ts + SUBLANES
    nb = [wd // tc for wd in widths]
    starts = [0, nb[0], nb[0] + nb[1]]

    def body(x_ref, xp_ref, xn_ref, d0, d1, d2, n0, n1, n2, w_ref, b_ref, _, __, dx_ref, dw_ref, db_ref, dxt_ref):
        j = pl.program_id(0)
        i = pl.program_id(1)
        last = i == nblk - 1
        w = w_ref[...]
        halo = jnp.where(i > 0, xp_ref[...], 0.0)
        x = x_ref[...]
        xe = jnp.concatenate([x, xn_ref[...]], axis=0)
        dcur = jnp.where(j < starts[1], d0[...], jnp.where(j < starts[2], d1[...], d2[...]))
        dnext = jnp.where(j < starts[1], n0[...], jnp.where(j < starts[2], n1[...], n2[...]))
        de = jnp.concatenate([dcur, jnp.where(last, 0.0, dnext)], axis=0)
        u = _conv_out(xe, halo, w, b_ref[...])
        sg = _sigmoid(u)
        dpre = de * (sg * (1.0 + u * (1.0 - sg)))
        dx = w[CONV_WIDTH - 1:CONV_WIDTH] * dpre[0:ts]
        for kk in range(CONV_WIDTH - 1):
            sh = CONV_WIDTH - 1 - kk
            dx = dx + w[kk:kk + 1] * pltpu.roll(dpre, ext - sh, axis=0)[0:ts]
        dx_ref[...] = dx.astype(BF16)
        dxt_ref[...] = dx.T.astype(BF16)
        dcur = dpre[0:ts]

        @pl.when(i == 0)
        def _():
            dw_ref[...] = jnp.zeros_like(dw_ref)
            db_ref[...] = jnp.zeros_like(db_ref)

        db_ref[...] += jnp.sum(dcur, axis=0, keepdims=True)
        dw_ref[CONV_WIDTH - 1:CONV_WIDTH, :] += jnp.sum(dcur * x, axis=0, keepdims=True)
        for kk in range(CONV_WIDTH - 1):
            xs = _shift_rows(x, halo, CONV_WIDTH - 1 - kk)
            dw_ref[kk:kk + 1, :] += jnp.sum(dcur * xs, axis=0, keepdims=True)

    cur_p = pl.BlockSpec((ts, tc), lambda j, i: (i, cb0 + j))
    prev_p = pl.BlockSpec((SUBLANES, tc), lambda j, i: (jnp.maximum(i * hb - 1, 0), cb0 + j))
    nxt = lambda i: jnp.minimum((i + 1) * hb, nblk * hb - 1)
    next_p = pl.BlockSpec((SUBLANES, tc), lambda j, i: (nxt(i), cb0 + j))

    def part(q):
        return lambda j: jnp.clip(j - starts[q], 0, nb[q] - 1)

    cur_d = [pl.BlockSpec((ts, tc), lambda j, i, f=part(q): (i, f(j))) for q in range(3)]
    next_d = [pl.BlockSpec((SUBLANES, tc), lambda j, i, f=part(q): (nxt(i), f(j))) for q in range(3)]
    vec4 = pl.BlockSpec((CONV_WIDTH, tc), lambda j, i: (0, j))
    vec1 = pl.BlockSpec((1, tc), lambda j, i: (0, j))
    hbm = pl.BlockSpec(memory_space=pl.ANY)
    return pl.pallas_call(
        body, name="conv_bwd", grid=(c // tc, nblk),
        in_specs=[cur_p, prev_p, next_p, *cur_d, *next_d, vec4, vec1, hbm, hbm],
        out_specs=(cur_p, vec4, vec1, pl.BlockSpec((tc, ts), lambda j, i: (cb0 + j, i))),
        out_shape=(jax.ShapeDtypeStruct(dproj.shape, dproj.dtype), jax.ShapeDtypeStruct((CONV_WIDTH, c), F32),
                   jax.ShapeDtypeStruct((1, c), F32), jax.ShapeDtypeStruct(dproj_t.shape, dproj_t.dtype)),
        input_output_aliases={11: 0, 12: 3},
        compiler_params=_params("parallel", "arbitrary"),
    )(proj, proj, proj, *dacts, *dacts, conv_w, conv_b, dproj, dproj_t)


def _dt_fwd(proj, dt_bias, col0):
    s = proj.shape[0]
    h = dt_bias.shape[1]
    ts = _pick(s, (1024, 512, 256, 128))

    def body(x_ref, b_ref, o_ref):
        v = x_ref[...] + b_ref[...]
        o_ref[...] = jnp.maximum(v, 0.0) + jnp.log1p(jnp.exp(-jnp.abs(v)))

    return pl.pallas_call(
        body, name="dt_fwd", grid=(s // ts,),
        in_specs=[pl.BlockSpec((ts, h), lambda i: (i, col0 // h)), pl.BlockSpec((1, h), lambda i: (0, 0))],
        out_specs=pl.BlockSpec((ts, h), lambda i: (i, 0)), out_shape=jax.ShapeDtypeStruct((s, h), F32),
        compiler_params=_params("parallel"),
    )(proj, dt_bias)


def _dt_bwd(proj, dt_bias, ddt, col0, dproj, dproj_t):
    s = proj.shape[0]
    h = dt_bias.shape[1]
    ts = _pick(s, (1024, 512, 256, 128))

    def body(x_ref, b_ref, d_ref, _, __, o_ref, db_ref, ot_ref):
        i = pl.program_id(0)
        draw = d_ref[...] * _sigmoid(x_ref[...] + b_ref[...])
        o_ref[...] = draw.astype(BF16)
        ot_ref[...] = draw.T.astype(BF16)

        @pl.when(i == 0)
        def _():
            db_ref[...] = jnp.zeros_like(db_ref)

        db_ref[...] += jnp.sum(draw, axis=0, keepdims=True)

    hbm = pl.BlockSpec(memory_space=pl.ANY)
    return pl.pallas_call(
        body, name="dt_bwd", grid=(s // ts,),
        in_specs=[pl.BlockSpec((ts, h), lambda i: (i, col0 // h)), pl.BlockSpec((1, h), lambda i: (0, 0)),
                  pl.BlockSpec((ts, h), lambda i: (i, 0)), hbm, hbm],
        out_specs=(pl.BlockSpec((ts, h), lambda i: (i, col0 // h)), pl.BlockSpec((1, h), lambda i: (0, 0)),
                   pl.BlockSpec((h, ts), lambda i: (col0 // h, i))),
        out_shape=(jax.ShapeDtypeStruct(dproj.shape, dproj.dtype), jax.ShapeDtypeStruct((1, h), F32),
                   jax.ShapeDtypeStruct(dproj_t.shape, dproj_t.dtype)),
        input_output_aliases={3: 0, 4: 2},
        compiler_params=_params("arbitrary"),
    )(proj, dt_bias, ddt, dproj, dproj_t)


def _chunk_terms(dt, dt_t, a, a_t):
    li = lax.broadcasted_iota(jnp.int32, (CHUNK, CHUNK), 0)
    si = lax.broadcasted_iota(jnp.int32, (CHUNK, CHUNK), 1)
    lower = (li >= si).astype(F32)
    upper = (li <= si).astype(F32)
    acum = jnp.dot(lower, dt * a, preferred_element_type=F32, precision=HIGHEST)
    acum_t = jnp.dot(dt_t * a_t, upper, preferred_element_type=F32, precision=HIGHEST)
    return acum, acum_t, li, si, upper


def _dot_exact01(t, m01):
    r = t.shape[0]
    hi = t.astype(BF16)
    rest = t - hi.astype(F32)
    mid = rest.astype(BF16)
    lo = (rest - mid.astype(F32)).astype(BF16)
    out = _dot(jnp.concatenate([hi, mid, lo], axis=0), m01.astype(BF16))
    return out[0:r] + out[r:2 * r] + out[2 * r:3 * r]


def _head_lanes(dt, acum, gw):
    hpg = dt.shape[1]
    p = gw // hpg
    spread = (lax.broadcasted_iota(jnp.int32, (hpg, gw), 1) // p
              == lax.broadcasted_iota(jnp.int32, (hpg, gw), 0)).astype(F32)
    both = _dot_exact01(jnp.concatenate([dt, acum], axis=0), spread)
    dt_e, acum_e = both[0:CHUNK], both[CHUNK:2 * CHUNK]
    alast_e = acum_e[CHUNK - 1:CHUNK, :]
    return dt_e, jnp.exp(acum_e), jnp.exp(alast_e - acum_e), jnp.exp(alast_e)


def _fold_heads(t, hpg):
    gw = t.shape[1]
    p = gw // hpg
    fold = (lax.broadcasted_iota(jnp.int32, (gw, hpg), 0) // p
            == lax.broadcasted_iota(jnp.int32, (gw, hpg), 1)).astype(F32)
    return _dot_exact01(t, fold)


def _ssd_fwd(xbc, proj, dt_g, dt_gt, a_g, a_gt, dskip_e, norm_w, d_inner, n_state):
    s = xbc.shape[0]
    hpg = dt_g.shape[2]
    gw = d_inner // SSM_GROUPS
    p = gw // hpg
    nc = s // CHUNK
    n = n_state
    b0 = d_inner // n
    c0 = b0 + SSM_GROUPS
    per_tile = LANES // p

    def body(xs_ref, b_ref, c_ref, dt_ref, dtt_ref, a_ref, at_ref, z_ref, dsk_ref, nw_ref,
             yn_ref, y_ref, st_ref, ynt_ref, state):
        c = pl.program_id(1)

        @pl.when(c == 0)
        def _():
            state[...] = jnp.zeros_like(state)

        st = state[...]
        st_ref[...] = st
        xs = xs_ref[...]
        bm = b_ref[...].astype(BF16)
        cm = c_ref[...].astype(BF16)
        dt = dt_ref[...]
        acum, acum_t, li, si, _ = _chunk_terms(dt, dtt_ref[...], a_ref[...], at_ref[...])
        dt_e, e_a, t_e, e_last = _head_lanes(dt, acum, gw)
        xdt = xs * dt_e
        xdtb = xdt.astype(BF16)
        cb = _dot_nt(cm, bm)
        causal = li >= si
        lane = lax.broadcasted_iota(jnp.int32, (1, LANES), 1)
        y_ref[...] = _dot(cm, st.astype(BF16)) * e_a
        for q in range(gw // LANES):
            ql = slice(q * LANES, (q + 1) * LANES)
            xq = xdtb[:, ql]
            ms = []
            for i in range(per_tile):
                h = q * per_tile + i
                decay = jnp.exp(jnp.where(causal, acum[:, h:h + 1] - acum_t[h:h + 1, :], NEG_INF))
                ms.append((cb * decay).astype(BF16))
            y_all = _dot(jnp.concatenate(ms, axis=0), xq)
            yd = y_all[0:CHUNK]
            for i in range(1, per_tile):
                yd = jnp.where(lane >= i * p, y_all[i * CHUNK:(i + 1) * CHUNK], yd)
            y_ref[:, ql] += yd
        state[...] = st * e_last + _dot_tn(bm, (xdt * t_e).astype(BF16))
        yt = y_ref[...] + xs * dsk_ref[...]
        z = z_ref[...]
        yz = yt * (z * _sigmoid(z))
        r = lax.rsqrt(jnp.mean(yz * yz, axis=-1, keepdims=True) + RMS_EPS)
        yn = yz * r * nw_ref[...]
        yn_ref[...] = yn.astype(BF16)
        ynt_ref[...] = yn.T.astype(BF16)

    wide = pl.BlockSpec((CHUNK, gw), lambda g, c: (c, g))
    return pl.pallas_call(
        body, name="ssd_fwd", grid=(SSM_GROUPS, nc),
        in_specs=[wide,
                  pl.BlockSpec((CHUNK, n), lambda g, c: (c, b0 + g)),
                  pl.BlockSpec((CHUNK, n), lambda g, c: (c, c0 + g)),
                  pl.BlockSpec((None, CHUNK, hpg), lambda g, c: (g, c, 0)),
                  pl.BlockSpec((None, hpg, CHUNK), lambda g, c: (g, 0, c)),
                  pl.BlockSpec((None, 1, hpg), lambda g, c: (g, 0, 0)),
                  pl.BlockSpec((None, hpg, 1), lambda g, c: (g, 0, 0)),
                  wide,
                  pl.BlockSpec((None, 1, gw), lambda g, c: (g, 0, 0)),
                  pl.BlockSpec((1, gw), lambda g, c: (0, g))],
        out_specs=(wide, wide, pl.BlockSpec((None, None, n, gw), lambda g, c: (g, c, 0, 0)),
                   pl.BlockSpec((gw, CHUNK), lambda g, c: (g, c))),
        out_shape=(jax.ShapeDtypeStruct((s, d_inner), BF16), jax.ShapeDtypeStruct((s, d_inner), F32),
                   jax.ShapeDtypeStruct((SSM_GROUPS, nc, n, gw), F32), jax.ShapeDtypeStruct((d_inner, s), BF16)),
        scratch_shapes=[pltpu.VMEM((n, gw), F32)],
        compiler_params=_params("parallel", "arbitrary"),
    )(xbc, xbc, xbc, dt_g, dt_gt, a_g, a_gt, proj, dskip_e, norm_w)


def _ssd_epilogue_bwd(dyn, y, xbc, proj, dskip_e, norm_w, hpg):
    s, d_inner = dyn.shape
    gw = d_inner // SSM_GROUPS
    p = gw // hpg
    nc = s // CHUNK

    def body(dyn_ref, y_ref, xs_ref, z_ref, dsk_ref, nw_ref, dy_ref, dz_ref, dnw_ref, ddsk_ref, dzt_ref):
        c = pl.program_id(1)
        xs = xs_ref[...]
        z = z_ref[...]
        yt = y_ref[...] + xs * dsk_ref[...]
        sg = _sigmoid(z)
        sz = z * sg
        yz = yt * sz
        r = lax.rsqrt(jnp.mean(yz * yz, axis=-1, keepdims=True) + RMS_EPS)
        dynv = dyn_ref[...]
        dyh = dynv * nw_ref[...]
        dyz = r * (dyh - yz * (r * r) * jnp.mean(dyh * yz, axis=-1, keepdims=True))
        dyt = dyz * sz
        dy_ref[...] = dyt
        dz = dyz * yt * (sg * (1.0 + z * (1.0 - sg)))
        dz_ref[...] = dz.astype(BF16)
        dzt_ref[...] = dz.T.astype(BF16)

        @pl.when(c == 0)
        def _():
            dnw_ref[...] = jnp.zeros_like(dnw_ref)
            ddsk_ref[...] = jnp.zeros_like(ddsk_ref)

        dnw_ref[...] += jnp.sum(dynv * yz * r, axis=0, keepdims=True)
        colsum = jnp.sum(dyt * xs, axis=0, keepdims=True)
        fold = (lax.broadcasted_iota(jnp.int32, (gw, hpg), 0) // p
                == lax.broadcasted_iota(jnp.int32, (gw, hpg), 1)).astype(F32)
        ddsk_ref[...] += jnp.dot(colsum, fold, preferred_element_type=F32, precision=HIGHEST)

    wide = pl.BlockSpec((CHUNK, gw), lambda g, c: (c, g))
    return pl.pallas_call(
        body, name="ssd_epilogue_bwd", grid=(SSM_GROUPS, nc),
        in_specs=[wide, wide, wide, wide, pl.BlockSpec((None, 1, gw), lambda g, c: (g, 0, 0)),
                  pl.BlockSpec((1, gw), lambda g, c: (0, g))],
        out_specs=(wide, wide, pl.BlockSpec((1, gw), lambda g, c: (0, g)),
                   pl.BlockSpec((None, 1, hpg), lambda g, c: (g, 0, 0)),
                   pl.BlockSpec((gw, CHUNK), lambda g, c: (g, c))),
        out_shape=(jax.ShapeDtypeStruct((s, d_inner), F32), jax.ShapeDtypeStruct((s, proj.shape[1]), BF16),
                   jax.ShapeDtypeStruct((1, d_inner), F32), jax.ShapeDtypeStruct((SSM_GROUPS, 1, hpg), F32),
                   jax.ShapeDtypeStruct((proj.shape[1], s), BF16)),
        compiler_params=_params("parallel", "arbitrary"),
    )(dyn, y, xbc, proj, dskip_e, norm_w)


def _ssd_scan_bwd(xbc, dt_g, dt_gt, a_g, a_gt, states, dy, dskip_e, d_inner, n_state, ride=None):
    s = xbc.shape[0]
    hpg = dt_g.shape[2]
    gw = d_inner // SSM_GROUPS
    p = gw // hpg
    nc = s // CHUNK
    n = n_state
    b0 = d_inner // n
    c0 = b0 + SSM_GROUPS
    per_tile = LANES // p

    def body(xs_ref, b_ref, c_ref, dt_ref, dtt_ref, a_ref, at_ref, st_ref, dy_ref, dsk_ref,
             dxs_ref, db_ref, dc_ref, ddt_ref, da_ref, dstate, ydiag_ref, dxd_ref):
        c = pl.program_id(1)

        @pl.when(c == 0)
        def _():
            dstate[...] = jnp.zeros_like(dstate)
            da_ref[...] = jnp.zeros_like(da_ref)

        xs = xs_ref[...]
        bm = b_ref[...].astype(BF16)
        cm = c_ref[...].astype(BF16)
        dt = dt_ref[...]
        a = a_ref[...]
        dyv = dy_ref[...]
        dsk = dsk_ref[...]
        acum, acum_t, li, si, upper = _chunk_terms(dt, dtt_ref[...], a, at_ref[...])
        dt_e, e_a, t_e, e_last = _head_lanes(dt, acum, gw)
        cb = _dot_nt(cm, bm)
        lower_mask = li >= si
        lane = lax.broadcasted_iota(jnp.int32, (1, LANES), 1)
        row_l = lax.broadcasted_iota(jnp.int32, (CHUNK, 1), 0)
        st = st_ref[...]
        stb = st.astype(BF16)
        dst = dstate[...]
        dstb = dst.astype(BF16)
        xdt = xs * dt_e
        xdtb = xdt.astype(BF16)
        dyb = dyv.astype(BF16)
        dye = dyv * e_a
        dyeb = dye.astype(BF16)
        xte = xdt * t_e
        xteb = xte.astype(BF16)
        wv = _dot(bm, dstb)
        yo = _dot(cm, stb)
        dcb = jnp.zeros((CHUNK, CHUNK), F32)
        for q in range(gw // LANES):
            ql = slice(q * LANES, (q + 1) * LANES)
            xq = xdtb[:, ql]
            dq = dyb[:, ql]
            decays, ms, mts, dqs = [], [], [], []
            for i in range(per_tile):
                h = q * per_tile + i
                decay = jnp.exp(jnp.where(lower_mask, acum[:, h:h + 1] - acum_t[h:h + 1, :], NEG_INF))
                mm = cb * decay
                mine = (lane >= i * p) & (lane < (i + 1) * p)
                decays.append(decay)
                ms.append(mm.astype(BF16))
                mts.append(mm.T.astype(BF16))
                dqs.append(jnp.where(mine, dq, jnp.zeros_like(dq)))
            dm_all = _dot_nt(jnp.concatenate(dqs, axis=0), xq)
            y_all = _dot(jnp.concatenate(ms, axis=0), xq)
            d_all = _dot(jnp.concatenate(mts, axis=0), dq)
            yd = dd = None
            for i in range(per_tile):
                rows = slice(i * CHUNK, (i + 1) * CHUNK)
                dcb = dcb + dm_all[rows] * decays[i]
                yd = y_all[rows] if i == 0 else jnp.where(lane >= i * p, y_all[rows], yd)
                dd = d_all[rows] if i == 0 else jnp.where(lane >= i * p, d_all[rows], dd)
            ydiag_ref[:, ql] = yd
            dxd_ref[:, ql] = dd
        ydiag = ydiag_ref[...]
        dxd = dxd_ref[...]
        dxdt = dxd + t_e * wv
        xw = xte * wv
        last_in = jnp.sum(xw, axis=0, keepdims=True) + e_last * jnp.sum(dst * st, axis=0, keepdims=True)
        folded = _fold_heads(jnp.concatenate(
            [dyb.astype(F32) * ydiag - xdtb.astype(F32) * dxd - xw + dye * yo, dxdt * xs,
             jnp.broadcast_to(last_in, (SUBLANES, gw))],
            axis=0), hpg)
        dalast = folded[2 * CHUNK:2 * CHUNK + 1]
        d_acum = folded[0:CHUNK] + jnp.where(row_l == CHUNK - 1, dalast, 0.0)
        ddt_x = folded[CHUNK:2 * CHUNK]
        dxs_ref[...] = dxdt * dt_e + dyv * dsk
        dbf = dcb.astype(BF16)
        dc_ref[...] = _dot_nt(dyeb, stb) + _dot(dbf, bm)
        db_ref[...] = _dot_nt(xteb, dstb) + _dot_tn(dbf, cm)
        dstate[...] = dst * e_last + _dot_tn(cm, dyeb)
        d_da = jnp.dot(upper, d_acum, preferred_element_type=F32, precision=HIGHEST)
        ddt_ref[...] = d_da * a + ddt_x
        da_ref[...] += jnp.sum(d_da * dt, axis=0, keepdims=True)

    rev = lambda c: nc - 1 - c
    wide = pl.BlockSpec((CHUNK, gw), lambda g, c: (rev(c), g))
    return _call(
        body, name="ssd_scan_bwd", grid=(SSM_GROUPS, nc),
        in_specs=[wide,
                  pl.BlockSpec((CHUNK, n), lambda g, c: (rev(c), b0 + g)),
                  pl.BlockSpec((CHUNK, n), lambda g, c: (rev(c), c0 + g)),
                  pl.BlockSpec((None, CHUNK, hpg), lambda g, c: (g, rev(c), 0)),
                  pl.BlockSpec((None, hpg, CHUNK), lambda g, c: (g, 0, rev(c))),
                  pl.BlockSpec((None, 1, hpg), lambda g, c: (g, 0, 0)),
                  pl.BlockSpec((None, hpg, 1), lambda g, c: (g, 0, 0)),
                  pl.BlockSpec((None, None, n, gw), lambda g, c: (g, rev(c), 0, 0)),
                  wide,
                  pl.BlockSpec((None, 1, gw), lambda g, c: (g, 0, 0))],
        out_specs=[wide,
                   pl.BlockSpec((CHUNK, n), lambda g, c: (rev(c), g)),
                   pl.BlockSpec((CHUNK, n), lambda g, c: (rev(c), g)),
                   pl.BlockSpec((None, CHUNK, hpg), lambda g, c: (g, rev(c), 0)),
                   pl.BlockSpec((None, 1, hpg), lambda g, c: (g, 0, 0))],
        out_shape=[jax.ShapeDtypeStruct((s, d_inner), F32),
                   jax.ShapeDtypeStruct((s, SSM_GROUPS * n), F32), jax.ShapeDtypeStruct((s, SSM_GROUPS * n), F32),
                   jax.ShapeDtypeStruct((SSM_GROUPS, s, hpg), F32), jax.ShapeDtypeStruct((SSM_GROUPS, 1, hpg), F32)],
        scratch=[pltpu.VMEM((n, gw), F32), pltpu.VMEM((CHUNK, gw), F32), pltpu.VMEM((CHUNK, gw), F32)],
        sem=("parallel", "arbitrary"), ride=ride,
        args=(xbc, xbc, xbc, dt_g, dt_gt, a_g, a_gt, states, dy, dskip_e))


def _lin(p):
    return 4 * p[0] + 2 * p[1] + p[2]


class _Gather:
    def __init__(self, arrs, rows=None, into=None):
        self.arrs = list(arrs)
        self.rows = rows
        self.into = list(into) if into is not None else []

    def out_shape(self):
        if self.rows is None:
            return [jax.ShapeDtypeStruct((NDEV,) + a.shape, a.dtype) for a in self.arrs]
        return [jax.ShapeDtypeStruct((NDEV, self.rows[1]) + a.shape[1:], a.dtype) for a in self.arrs]

    def _copies(self, ins, outs, sems):
        send_sems, recv_sems, local_sems = sems
        x, y, c = lax.axis_index("x"), lax.axis_index("y"), lax.axis_index("c")
        me, sibling = (x, y, c), (x, y, 1 - c)
        chips = [(1 - x, y), (x, 1 - y), (1 - x, 1 - y)]

        def slot(a, block):
            if self.rows is None:
                return outs[a].at[_lin(block)]
            return outs[a].at[_lin(block), pl.ds(self.rows[0], ins[a].shape[0])]

        def copy(a, k, block, to, src=None):
            rows = slot(a, block)
            return pltpu.make_async_remote_copy(
                src_ref=rows if src is None else src, dst_ref=rows,
                send_sem=send_sems.at[a * NPEER + k], recv_sem=recv_sems.at[a * NPEER + k],
                device_id=to, device_id_type=pl.DeviceIdType.MESH)

        na = len(ins)
        mine = [pltpu.make_async_copy(ins[a], slot(a, me), local_sems.at[a]) for a in range(na)]
        first = []
        for a in range(na):
            first.append(copy(a, 0, me, sibling, src=ins[a]))
            first += [copy(a, 1 + j, me, (*chip, c), src=ins[a]) for j, chip in enumerate(chips)]
        return copy, mine, first, me, sibling, chips, c, na

    def start(self, ins, outs, sems):
        _, mine, first, *_ = self._copies(ins, outs, sems)
        for cp in mine + first:
            cp.start()

    def finish(self, ins, outs, sems):
        copy, mine, first, me, sibling, chips, c, na = self._copies(ins, outs, sems)
        passed = []
        for j, chip in enumerate(chips):
            for a in range(na):
                copy(a, 1 + j, (*chip, c), me).wait_recv()
                cp = copy(a, 4 + j, (*chip, c), sibling)
                cp.start()
                passed.append(cp)
        for a in range(na):
            copy(a, 0, sibling, me).wait_recv()
            for j, chip in enumerate(chips):
                copy(a, 4 + j, (*chip, 1 - c), me).wait_recv()
        for cp in first + passed:
            cp.wait_send()
        for cp in mine:
            cp.wait()


class _Scatter:
    def __init__(self, arrs, ks=tuple(range(NDEV))):
        self.arrs = list(arrs)
        self.ks = [tuple(k) for k in ks] if isinstance(ks[0], (tuple, list)) else [tuple(ks)] * len(self.arrs)
        assert len(self.ks) == len(self.arrs)

    def out_shape(self):
        return [jax.ShapeDtypeStruct((len(k),) + a.shape[1:], a.dtype) for a, k in zip(self.arrs, self.ks)]

    def _copies(self, ins, outs, sems):
        send_sems, recv_sems, local_sems = sems
        x, y, c = lax.axis_index("x"), lax.axis_index("y"), lax.axis_index("c")
        me = (x, y, c)

        def peer(k):
            return (1 - x if k & 4 else x, 1 - y if k & 2 else y, 1 - c if k & 1 else c)

        local, remote = [], []
        for a in range(len(ins)):
            for i, k in enumerate(self.ks[a]):
                if k == 0:
                    local.append(pltpu.make_async_copy(ins[a].at[_lin(me)], outs[a].at[i], local_sems.at[a]))
                else:
                    remote.append(pltpu.make_async_remote_copy(
                        src_ref=ins[a].at[_lin(peer(k))], dst_ref=outs[a].at[i],
                        send_sem=send_sems.at[a * NPEER + k - 1], recv_sem=recv_sems.at[a * NPEER + k - 1],
                        device_id=peer(k), device_id_type=pl.DeviceIdType.MESH))
        return local, remote

    def start(self, ins, outs, sems):
        local, remote = self._copies(ins, outs, sems)
        for cp in local + remote:
            cp.start()

    def finish(self, ins, outs, sems):
        local, remote = self._copies(ins, outs, sems)
        for cp in remote:
            cp.wait_recv()
        for cp in remote:
            cp.wait_send()
        for cp in local:
            cp.wait()


def _exchange_scratch(na):
    return [pltpu.SemaphoreType.DMA((na * NPEER,)), pltpu.SemaphoreType.DMA((na * NPEER,)),
            pltpu.SemaphoreType.DMA((na,))]


def _exchange_alone(ex, *, name, in_vmem=False):
    na = len(ex.arrs)

    def body(*refs):
        ins, outs, sems = refs[:na], refs[na:2 * na], refs[2 * na:]
        ex.start(ins, outs, sems)
        ex.finish(ins, outs, sems)

    spec = pl.BlockSpec(memory_space=pltpu.VMEM if in_vmem else pl.ANY)
    return pl.pallas_call(
        body, name=name, out_shape=tuple(ex.out_shape()), in_specs=[spec] * na, out_specs=tuple([spec] * na),
        scratch_shapes=_exchange_scratch(na),
        compiler_params=pltpu.CompilerParams(vmem_limit_bytes=VMEM_LIMIT),
    )(*ex.arrs)


def _call(body, *, name, grid, in_specs, out_specs, out_shape, args, sem, scratch=(), ride=None, aliases=None):
    n_in, n_out, n_scr = len(in_specs), len(out_specs), len(scratch)
    if ride is None:
        outs = pl.pallas_call(
            body, name=name, grid=grid, in_specs=list(in_specs), out_specs=tuple(out_specs),
            out_shape=tuple(out_shape), scratch_shapes=list(scratch), input_output_aliases=aliases or {},
            compiler_params=_params(*sem))(*args)
        return tuple(outs), ()
    nx = len(ride.arrs)
    into = getattr(ride, "into", [])
    hbm = pl.BlockSpec(memory_space=pl.ANY)
    aliases = dict(aliases or {})
    aliases.update({n_in + nx + i: n_out + i for i in range(len(into))})

    def hosted(*refs):
        ins, x_in = refs[:n_in], refs[n_in:n_in + nx]
        o0 = n_in + nx + len(into)
        outs, x_out = refs[o0:o0 + n_out], refs[o0 + n_out:o0 + n_out + nx]
        s0 = o0 + n_out + nx
        scr, x_sem = refs[s0:s0 + n_scr], refs[s0 + n_scr:]
        ids = [pl.program_id(i) for i in range(len(grid))]
        first = functools.reduce(jnp.logical_and, [i == 0 for i in ids])
        last = functools.reduce(jnp.logical_and, [i == g - 1 for i, g in zip(ids, grid)])

        @pl.when(first)
        def _():
            ride.start(x_in, x_out, x_sem)

        body(*ins, *outs, *scr)

        @pl.when(last)
        def _():
            ride.finish(x_in, x_out, x_sem)

    outs = pl.pallas_call(
        hosted, name=name, grid=grid, in_specs=list(in_specs) + [hbm] * (nx + len(into)),
        out_specs=tuple(list(out_specs) + [hbm] * nx), out_shape=tuple(list(out_shape) + ride.out_shape()),
        scratch_shapes=list(scratch) + _exchange_scratch(nx), input_output_aliases=aliases,
        compiler_params=_params(*(("arbitrary",) * len(grid))))(*args, *ride.arrs, *into)
    return tuple(outs[:n_out]), tuple(outs[n_out:])


def _pack(parts):
    flat = jnp.concatenate([p.reshape(-1).astype(F32) for p in parts])
    tile = SUBLANES * LANES
    pad = (-flat.shape[0]) % tile
    return jnp.pad(flat, (0, pad)).reshape(-1, LANES)


def _unpack(buf, shapes):
    flat = buf.reshape(-1)
    out, off = [], 0
    for shp in shapes:
        size = math.prod(shp)
        out.append(flat[off:off + size].reshape(shp))
        off += size
    return out


KS_FLAT = (0, 1, 4, 5, 2, 3)
KS_DIAG = (6, 7)


def _local_step(x, target, wa, wo, ws, wos, rel_bias, conv_w, conv_b, dt_bias, a_log, d_skip, norm_w, ln_g, ln_b,
                dist=False):
    s, d = x.shape
    da = wo.shape[-2]
    heads = da // HEAD_DIM
    qkv_cols = 3 * N_GROUPS_ATTN * da
    d_inner = wos.shape[0] * (NDEV if dist else 1)
    conv_dim = conv_w.shape[1]
    ssm_heads = dt_bias.shape[1]
    hpg = ssm_heads // SSM_GROUPS
    gn = (conv_dim - d_inner) // 2
    n_state = gn // SSM_GROUPS
    gw = d_inner // SSM_GROUPS
    p = gw // hpg
    in_ssm = d_inner + conv_dim + ssm_heads
    xb, xbt = _cast_bf16(x, name="cast_x", with_transpose=True)

    per_dev = in_ssm // NDEV
    third = (per_dev // 3) // 16 * 16
    band_rows = (third, third, per_dev - 2 * third)
    qkvs, ws_all, row0 = [], None, 0
    for g in range(N_GROUPS_ATTN):
        ride = None
        if dist:
            ride = _Gather([ws[row0:row0 + band_rows[g]]], rows=(row0, per_dev), into=ws_all)
            row0 += band_rows[g]
        got = _mm(xb, wa, name=f"mm_qkv_g{g}", out_dtype=BF16, n_off=g * 3 * da, n_out=3 * da, ride=ride)
        if dist:
            ws_all = list(got[1])
            got = got[0]
        qkvs.append(got)
    if dist:
        ws = ws_all[0].reshape(in_ssm, d)
    gate = _mm(xb, wa, name="mm_gate", out_dtype=F32, n_off=qkv_cols, n_out=da)
    bias, bucket = _bias_tables(rel_bias, heads)
    os_, ls_ = [], []
    for g, (_, dil) in enumerate(ATTN_PATTERNS):
        ride = _Gather([wo]) if dist and g == 0 else None
        o, l, rode = _attn_fwd_group(qkvs[g], bias[g], g, dil, da, ride=ride)
        if rode:
            (wo,) = rode
        os_.append(o)
        ls_.append(l)
    o, lse, y, yt = _attn_combine(os_, ls_, gate)
    h1 = _mm(y, wo, name="mm_out_attn", out_dtype=F32)
    x1, x1b = _ln_fwd(x, h1, ln_g[0:1], ln_b[0:1], name="ln1_fwd")

    if dist:
        proj, (wos_slabs,) = _mm(x1b, ws, name="mm_in_ssm", out_dtype=F32, trans_b=True, ride=_Gather([wos]))
        wos = wos_slabs.reshape(d_inner, d)
    else:
        proj = _mm(x1b, ws, name="mm_in_ssm", out_dtype=F32, trans_b=True)
    xbc = _conv_fwd(proj, conv_w, conv_b, d_inner)
    dt = _dt_fwd(proj, dt_bias, d_inner + conv_dim)
    dt_g = dt.reshape(s, SSM_GROUPS, hpg).transpose(1, 0, 2)
    dt_gt = dt.reshape(s, SSM_GROUPS, hpg).transpose(1, 2, 0)
    a = -jnp.exp(a_log)
    a_g = a.reshape(SSM_GROUPS, 1, hpg)
    a_gt = a.reshape(SSM_GROUPS, hpg, 1)
    dskip_e = jnp.repeat(d_skip.reshape(SSM_GROUPS, 1, hpg), p, axis=2)
    yn, yscan, states, ynt = _ssd_fwd(xbc, proj, dt_g, dt_gt, a_g, a_gt, dskip_e, norm_w, d_inner, n_state)
    h2 = _mm(yn, wos, name="mm_out_ssm", out_dtype=F32)

    du2, du2b, dg1, db1, loss_t = _ln_bwd(x1, h2, ln_g[1:2], ln_b[1:2], target, with_loss=True, name="ln2_loss_bwd")
    loss = loss_t[0, 0]
    dyn = _mm(du2b, wos, name="mm_dyn", out_dtype=F32, trans_b=True)
    g_wos = _mm(ynt, du2b, name="mm_dw_out_ssm", out_dtype=BF16)
    parts = {}
    dyscan, dproj_ssm, g_norm, g_dskip, dproj_t = _ssd_epilogue_bwd(dyn, yscan, xbc, proj, dskip_e, norm_w, hpg)
    ride = _Scatter([g_wos.reshape(NDEV, d_inner // NDEV, d)]) if dist else None
    (dxs, d_bm, d_cm, ddt_g, g_a), rode = _ssd_scan_bwd(xbc, dt_g, dt_gt, a_g, a_gt, states, dyscan, dskip_e,
                                                         d_inner, n_state, ride=ride)
    parts["w_out_ssm"] = [list(rode)]
    g_alog = g_a.reshape(1, ssm_heads) * a
    dproj_ssm, g_conv_w, g_conv_b, dproj_t = _conv_bwd(proj, conv_w, conv_b, (dxs, d_bm, d_cm), d_inner, dproj_ssm,
                                                       dproj_t)
    ddt = ddt_g.transpose(1, 0, 2).reshape(s, ssm_heads)
    dproj_ssm, g_dtb, dproj_t = _dt_bwd(proj, dt_bias, ddt, d_inner + conv_dim, dproj_ssm, dproj_t)
    g_ws = _mm(dproj_t, x1b, name="mm_dw_in_ssm", out_dtype=BF16)
    if dist:
        g_ws_slabs = g_ws.reshape(NDEV, per_dev, d)
        dx1, near = _mm(dproj_ssm, ws, name="mm_dx1", out_dtype=F32, res=du2, res_scale=DEEPNORM_ALPHA,
                        ride=_Scatter([g_ws_slabs], KS_FLAT))
    else:
        dx1 = _mm(dproj_ssm, ws, name="mm_dx1", out_dtype=F32, res=du2, res_scale=DEEPNORM_ALPHA)

    du1, du1b, dg0, db0 = _ln_bwd(x, h1, ln_g[0:1], ln_b[0:1], dx1, with_loss=False, name="ln1_bwd")
    dy = _mm(du1b, wo, name="mm_dy", out_dtype=F32, trans_b=True)
    g_wo = _mm(yt, du1b, name="mm_dw_out_attn", out_dtype=BF16, slab_out=NDEV)
    do, dgate, dd = _attn_bwd_prep(dy, o, gate)
    rides = [_Scatter([g_ws_slabs], KS_DIAG[0:1]), _Scatter([g_wo]), None] if dist else [None] * N_GROUPS_ATTN
    dparts, dss, rode_attn = [], [], []
    for g, (_, dil) in enumerate(ATTN_PATTERNS):
        (dq, dk, dv, ds), rode = _attn_bwd_group(qkvs[g], do, lse, dd, bias[g], g, dil, da, ride=rides[g])
        dparts += [dq, dk, dv]
        dss.append(ds)
        rode_attn += list(rode)
    g_bias = _bias_bwd(jnp.stack(dss), bucket)
    g_rel_bias = g_bias.transpose(2, 0, 1).reshape(NUM_BUCKETS, N_GROUPS_ATTN * heads)
    dproj_attn = jnp.concatenate(dparts + [dgate], axis=1)
    pending = None
    if dist:
        parts["w_out_attn"] = [rode_attn[1:]]
        half = d // 2
        g_top, (diag_b,) = _mm(xbt[:half], dproj_attn, name="mm_dw_in_attn_top", out_dtype=BF16, slab_out=NDEV,
                               ride=_Scatter([g_ws_slabs], KS_DIAG[1:2]))
        parts["w_in_ssm"] = [[near[0], rode_attn[0], diag_b]]
        g_bot, (top_a,) = _mm(xbt[half:], dproj_attn, name="mm_dw_in_attn_bottom", out_dtype=BF16, slab_out=NDEV,
                              ride=_Scatter([g_top], KS_FLAT))
        dx, (top_b, bot_a) = _mm(dproj_attn, wa, name="mm_dx", out_dtype=F32, trans_b=True, res=du1,
                                 res_scale=DEEPNORM_ALPHA, ride=_Scatter([g_top, g_bot], [KS_DIAG, KS_FLAT]))
        parts["w_in_attn"] = [[top_a, top_b], [bot_a]]
        pending = [_Scatter([g_bot], KS_DIAG[0:1]), _Scatter([g_bot], KS_DIAG[1:2])]
    else:
        g_wa = _mm(xbt, dproj_attn, name="mm_dw_in_attn", out_dtype=BF16, slab_out=NDEV)
        dx = _mm(dproj_attn, wa, name="mm_dx", out_dtype=F32, trans_b=True, res=du1, res_scale=DEEPNORM_ALPHA)

    g_ln_g = jnp.concatenate([dg0, dg1], axis=0)
    g_ln_b = jnp.concatenate([db0, db1], axis=0)
    small = dict(rel_bias=g_rel_bias, dt_bias=g_dtb, a_log=g_alog, d_skip=g_dskip.reshape(1, ssm_heads),
                 ln_g=g_ln_g, ln_b=g_ln_b, conv_w=g_conv_w, conv_b=g_conv_b, ssm_norm_w=g_norm)
    if dist:
        return loss, dx, parts, pending, small
    return loss, dx, g_wa, g_wo, g_ws, g_wos, small


REPLICATED = ("rel_bias", "dt_bias", "a_log", "d_skip", "ln_g", "ln_b")
SHARDED_SMALL = ("conv_w", "conv_b", "ssm_norm_w")


def kernel(x, w_in_attn, w_out_attn, rel_bias, w_in_ssm, conv_w, conv_b, dt_bias, a_log, d_skip, ssm_norm_w, w_out_ssm, ln_g, ln_b, loss_target, m_w_in_attn, m_w_out_attn, m_rel_bias, m_w_in_ssm, m_conv_w, m_conv_b, m_dt_bias, m_a_log, m_d_skip, m_ssm_norm_w, m_w_out_ssm, m_ln_g, m_ln_b, v_w_in_attn, v_w_out_attn, v_rel_bias, v_w_in_ssm, v_conv_w, v_conv_b, v_dt_bias, v_a_log, v_d_skip, v_ssm_norm_w, v_w_out_ssm, v_ln_g, v_ln_b):
    w = dict(w_in_attn=w_in_attn, w_out_attn=w_out_attn, rel_bias=rel_bias, w_in_ssm=w_in_ssm, conv_w=conv_w,
             conv_b=conv_b, dt_bias=dt_bias, a_log=a_log, d_skip=d_skip, ssm_norm_w=ssm_norm_w,
             w_out_ssm=w_out_ssm, ln_g=ln_g, ln_b=ln_b)
    m = dict(w_in_attn=m_w_in_attn, w_out_attn=m_w_out_attn, rel_bias=m_rel_bias, w_in_ssm=m_w_in_ssm,
             conv_w=m_conv_w, conv_b=m_conv_b, dt_bias=m_dt_bias, a_log=m_a_log, d_skip=m_d_skip,
             ssm_norm_w=m_ssm_norm_w, w_out_ssm=m_w_out_ssm, ln_g=m_ln_g, ln_b=m_ln_b)
    v = dict(w_in_attn=v_w_in_attn, w_out_attn=v_w_out_attn, rel_bias=v_rel_bias, w_in_ssm=v_w_in_ssm,
             conv_w=v_conv_w, conv_b=v_conv_b, dt_bias=v_dt_bias, a_log=v_a_log, d_skip=v_d_skip,
             ssm_norm_w=v_ssm_norm_w, w_out_ssm=v_w_out_ssm, ln_g=v_ln_g, ln_b=v_ln_b)
    me = _lin((lax.axis_index("x"), lax.axis_index("y"), lax.axis_index("c")))
    d = x.shape[2]
    big = ("w_in_attn", "w_out_attn", "w_in_ssm", "w_out_ssm")

    for t in (w, m, v):
        t["w_in_ssm"] = t["w_in_ssm"].transpose(0, 2, 1)
    shards = {k: _cast_bf16(w[k], name=f"cast_{k}") for k in big}
    (wa,) = _exchange_alone(_Gather([shards["w_in_attn"]]), name="gather_w_in_attn")
    cpd = conv_w.shape[2]
    npd = ssm_norm_w.shape[1]
    small_shapes = [(CONV_WIDTH, cpd), (1, cpd), (1, npd)]
    (small_all,) = _exchange_alone(_Gather([_pack([conv_w[0], conv_b, ssm_norm_w])]), name="gather_small_weights",
                                   in_vmem=True)
    small_parts = [_unpack(small_all[i], small_shapes) for i in range(NDEV)]
    conv_w_full = jnp.concatenate([p[0] for p in small_parts], axis=1)
    conv_b_full = jnp.concatenate([p[1] for p in small_parts], axis=1)
    norm_w_full = jnp.concatenate([p[2] for p in small_parts], axis=1)

    loss, dx, parts, pending, small = _local_step(
        x[0], loss_target[0], wa, shards["w_out_attn"], shards["w_in_ssm"], shards["w_out_ssm"], rel_bias,
        conv_w_full, conv_b_full, dt_bias[0:1], a_log[0:1], d_skip[0:1], norm_w_full, ln_g, ln_b, dist=True)
    loss = lax.psum(loss, MESH_AXES)
    out = {}
    for k, ride in zip(("w_in_ssm", "w_out_ssm"), pending):
        out[k], late = _adamw_sum(parts[k], w[k], m[k], v[k], name=f"adamw_{k}", ride=ride)
        parts["w_in_attn"][1] += list(late)
    out["w_in_ssm"] = tuple(t.transpose(0, 2, 1) for t in out["w_in_ssm"])
    for k in ("w_out_attn", "w_in_attn"):
        out[k] = _adamw_sum(parts[k], w[k], m[k], v[k], name=f"adamw_{k}")

    order = REPLICATED + SHARDED_SMALL
    g_shapes = [small[k].shape for k in order]
    (g_all,) = _exchange_alone(_Gather([_pack([small[k] for k in order])]), name="gather_small_grads", in_vmem=True)
    g_sum = dict(zip(order, _unpack(_sum_slots(g_all, name="sum_small_grads"), g_shapes)))
    g_mine = {k: g_sum[k] for k in REPLICATED}
    g_mine["conv_w"] = lax.dynamic_slice_in_dim(g_sum["conv_w"], me * cpd, cpd, axis=1)
    g_mine["conv_b"] = lax.dynamic_slice_in_dim(g_sum["conv_b"], me * cpd, cpd, axis=1)
    g_mine["ssm_norm_w"] = lax.dynamic_slice_in_dim(g_sum["ssm_norm_w"], me * npd, npd, axis=1)
    w_shapes = [w[k].shape for k in order]
    g_pack = _pack([g_mine[k] for k in order])
    d_p, m_p, v_p = _adamw_small(g_pack, _pack([w[k] for k in order]), _pack([m[k] for k in order]),
                                 _pack([v[k] for k in order]), name="adamw_small")
    for k, gk, dk, mk, vk in zip(order, _unpack(g_pack, w_shapes), _unpack(d_p, w_shapes), _unpack(m_p, w_shapes),
                                 _unpack(v_p, w_shapes)):
        out[k] = (gk, dk, mk, vk)

    names = ("w_in_attn", "w_out_attn", "rel_bias", "w_in_ssm", "conv_w", "conv_b", "dt_bias", "a_log", "d_skip",
             "ssm_norm_w", "w_out_ssm", "ln_g", "ln_b")
    res = [loss, dx[None]]
    for i in range(4):
        res += [out[k][i] for k in names]
    return tuple(res)
```

```python
import functools
import math

import jax
import jax.numpy as jnp
from jax import lax
from jax.experimental import pallas as pl
from jax.experimental.pallas import tpu as pltpu

F32 = jnp.float32
BF16 = jnp.bfloat16
MESH_AXES = ("x", "y", "c")
NDEV = 8
NPEER = NDEV - 1
LANES = 128
SUBLANES = 8
VMEM_LIMIT = 52 * 1024 * 1024
MM_VMEM_BUDGET = 40 * 1024 * 1024
MM_TK_MAX = 4096
MM_TN_MAX = 1024

ATTN_PATTERNS = ((128, 1), (512, 4), (2048, 16))
N_GROUPS_ATTN = 3
HEAD_DIM = 128
ATTN_BLOCK = 128
ATTN_ROWS_TIMES_HEADS = 8192
NUM_BUCKETS = 32
MAX_DISTANCE = 2048
SSM_GROUPS = 8
CONV_WIDTH = 4
CHUNK = 128
DEPTH = 2
DEEPNORM_ALPHA = (2 * DEPTH) ** 0.25
LN_EPS = 1e-5
RMS_EPS = 1e-5
NEG_INF = -1e30
ADAM_LR = 0.001
ADAM_B1 = 0.9
ADAM_B2 = 0.999
ADAM_EPS = 1e-08
ADAM_WD = 0.01
ADAM_STEP = 10
HIGHEST = lax.Precision.HIGHEST


def _params(*sem):
    return pltpu.CompilerParams(dimension_semantics=sem, vmem_limit_bytes=VMEM_LIMIT)


def _pick(n, prefs):
    for p in prefs:
        if n % p == 0:
            return p
    return n


def _row_tile(r, limit):
    return max(t for t in range(2 * SUBLANES, limit + 1, 2 * SUBLANES) if r % t == 0)


def _dot(a, b):
    return jnp.dot(a, b, preferred_element_type=F32)


def _dot_nt(a, b):
    return lax.dot_general(a, b, (((1,), (1,)), ((), ())), preferred_element_type=F32)


def _dot_tn(a, b):
    return lax.dot_general(a, b, (((0,), (0,)), ((), ())), preferred_element_type=F32)


def _sigmoid(x):
    return 1.0 / (1.0 + jnp.exp(-x))


def _mm(a, b, *, name, out_dtype, trans_b=False, slab_out=0, n_off=0, n_out=None,
        res=None, res_scale=1.0, ride=None):
    m, k = a.shape
    slab_b = b.ndim == 3
    if slab_b:
        ns = b.shape[0]
        if trans_b:
            n, kper = b.shape[1], b.shape[2]
            assert ns * kper == k
        else:
            nper = b.shape[2]
            n = ns * nper
            assert b.shape[1] == k
    else:
        n = b.shape[0] if trans_b else b.shape[1]
        assert (b.shape[1] if trans_b else b.shape[0]) == k
    n_out = n if n_out is None else n_out
    tm = _pick(m, (1024, 640, 512, 256, 128))
    nconstraint = math.gcd(n_out, n_off) if n_off else n_out
    if slab_b and not trans_b:
        nconstraint = math.gcd(nconstraint, nper)
    if slab_out:
        nconstraint = math.gcd(nconstraint, n_out // slab_out)
    kconstraint = kper if (slab_b and trans_b) else k
    tk = max(t for t in range(LANES, min(kconstraint, MM_TK_MAX) + 1, LANES) if kconstraint % t == 0)
    nk = k // tk
    out_bytes = jnp.dtype(out_dtype).itemsize

    def vmem_bytes(t):
        return (2 * 2 * tk * (tm + t) + 2 * tm * t * out_bytes + (4 * tm * t if nk > 1 else 0)
                + (2 * 4 * tm * t if res is not None else 0))

    fits = [t for t in range(LANES, min(nconstraint, MM_TN_MAX) + 1, LANES)
            if nconstraint % t == 0 and vmem_bytes(t) <= MM_VMEM_BUDGET]
    tn = max(fits)
    nb0 = n_off // tn
    grid = (m // tm, n_out // tn, nk)

    a_spec = pl.BlockSpec((tm, tk), lambda i, j, kk: (i, kk))
    if slab_b and not trans_b:
        nps = nper // tn
        b_spec = pl.BlockSpec((None, tk, tn), lambda i, j, kk: ((j + nb0) // nps, kk, (j + nb0) % nps))
    elif slab_b and trans_b:
        kps = kper // tk
        b_spec = pl.BlockSpec((None, tn, tk), lambda i, j, kk: (kk // kps, j + nb0, kk % kps))
    elif trans_b:
        b_spec = pl.BlockSpec((tn, tk), lambda i, j, kk: (j + nb0, kk))
    else:
        b_spec = pl.BlockSpec((tk, tn), lambda i, j, kk: (kk, j + nb0))
    if slab_out:
        ops = (n_out // slab_out) // tn
        o_spec = pl.BlockSpec((None, tm, tn), lambda i, j, kk: (j // ops, i, j % ops))
        o_shape = jax.ShapeDtypeStruct((slab_out, m, n_out // slab_out), out_dtype)
    else:
        o_spec = pl.BlockSpec((tm, tn), lambda i, j, kk: (i, j))
        o_shape = jax.ShapeDtypeStruct((m, n_out), out_dtype)
    in_specs = [a_spec, b_spec]
    args = [a, b]
    if res is not None:
        in_specs.append(pl.BlockSpec((tm, tn), lambda i, j, kk: (i, j)))
        args.append(res)

    def body(*refs):
        a_ref, b_ref = refs[0], refs[1]
        r_ref = refs[2] if res is not None else None
        o_ref = refs[3] if res is not None else refs[2]
        av = a_ref[...].astype(BF16)
        bv = b_ref[...].astype(BF16)
        part = _dot_nt(av, bv) if trans_b else _dot(av, bv)

        def finish(r):
            if res is not None:
                r = r + res_scale * r_ref[...]
            o_ref[...] = r.astype(out_dtype)

        if nk == 1:
            finish(part)
            return
        acc = refs[-1]
        kk = pl.program_id(2)

        @pl.when(kk == 0)
        def _():
            acc[...] = part

        @pl.when(kk > 0)
        def _():
            acc[...] += part

        @pl.when(kk == nk - 1)
        def _():
            finish(acc[...])

    outs, rode = _call(
        body, name=name, grid=grid, in_specs=in_specs, out_specs=[o_spec], out_shape=[o_shape], args=args,
        scratch=[pltpu.VMEM((tm, tn), F32)] if nk > 1 else [], ride=ride,
        sem=("parallel", "parallel", "arbitrary"))
    return (outs[0], rode) if ride is not None else outs[0]


def _cast_bf16(w, *, name, with_transpose=False):
    r, c = w.shape[-2:]
    tr = _row_tile(r, 512)

    def body(w_ref, o_ref, *t_ref):
        v = w_ref[...]
        o_ref[...] = v.astype(BF16)
        if with_transpose:
            t_ref[0][...] = v.T.astype(BF16)

    in_spec = (pl.BlockSpec((None, tr, c), lambda i: (0, i, 0)) if w.ndim == 3
               else pl.BlockSpec((tr, c), lambda i: (i, 0)))
    out_specs = [pl.BlockSpec((tr, c), lambda i: (i, 0))]
    out_shape = [jax.ShapeDtypeStruct((r, c), BF16)]
    if with_transpose:
        out_specs.append(pl.BlockSpec((c, tr), lambda i: (0, i)))
        out_shape.append(jax.ShapeDtypeStruct((c, r), BF16))
    out = pl.pallas_call(
        body, name=name, grid=(r // tr,), in_specs=[in_spec], out_specs=tuple(out_specs),
        out_shape=tuple(out_shape), compiler_params=_params("parallel"),
    )(w)
    return out if with_transpose else out[0]


def _adam_math(w, g, m, v):
    m2 = ADAM_B1 * m + (1.0 - ADAM_B1) * g
    v2 = ADAM_B2 * v + (1.0 - ADAM_B2) * (g * g)
    m_hat = m2 / (1.0 - ADAM_B1 ** ADAM_STEP)
    v_hat = v2 / (1.0 - ADAM_B2 ** ADAM_STEP)
    delta = -ADAM_LR * (m_hat / (jnp.sqrt(v_hat) + ADAM_EPS) + ADAM_WD * w)
    return delta, m2, v2


def _adamw_sum(bands, w, m, v, *, name, ride=None):
    _, r, c = w.shape
    nband = len(bands)
    rows = r // nband
    tr = _row_tile(rows, 128)
    tc = c if (c % LANES or c <= 2560) else _pick(c, (2048, 1024, 512, 256, 128))
    nt = rows // tr
    flat = [p for band in bands for p in band]

    def body(*refs):
        p_refs = refs[:len(flat)]
        w_ref, m_ref, v_ref, g_out, d_out, m_out, v_out = refs[len(flat):]
        i = pl.program_id(0)
        g, at = None, 0
        for q, band in enumerate(bands):
            gq = None
            for p_ref in p_refs[at:at + len(band)]:
                for s in range(p_ref.shape[0]):
                    t = p_ref[s].astype(F32)
                    gq = t if gq is None else gq + t
            at += len(band)
            g = gq if q == 0 else jnp.where(i >= q * nt, gq, g)
        d, m2, v2 = _adam_math(w_ref[...], g, m_ref[...], v_ref[...])
        g_out[...] = g
        d_out[...] = d
        m_out[...] = m2
        v_out[...] = v2

    def band_spec(p, q):
        return pl.BlockSpec((p.shape[0], tr, tc), lambda i, j: (0, jnp.clip(i - q * nt, 0, nt - 1), j))

    spec = pl.BlockSpec((None, tr, tc), lambda i, j: (0, i, j))
    shp = jax.ShapeDtypeStruct((1, r, c), F32)
    outs, rode = _call(
        body, name=name, grid=(r // tr, c // tc),
        in_specs=[band_spec(p, q) for q, band in enumerate(bands) for p in band] + [spec, spec, spec],
        out_specs=[spec, spec, spec, spec], out_shape=[shp, shp, shp, shp], args=(*flat, w, m, v),
        sem=("parallel", "parallel"), ride=ride)
    return (outs, rode) if ride is not None else outs


def _adamw_small(g, w, m, v, *, name):
    shp = jax.ShapeDtypeStruct(w.shape, F32)

    def body(g_ref, w_ref, m_ref, v_ref, d_out, m_out, v_out):
        d, m2, v2 = _adam_math(w_ref[...], g_ref[...], m_ref[...], v_ref[...])
        d_out[...] = d
        m_out[...] = m2
        v_out[...] = v2

    return pl.pallas_call(body, name=name, out_shape=(shp, shp, shp),
                          compiler_params=pltpu.CompilerParams(vmem_limit_bytes=VMEM_LIMIT))(g, w, m, v)


def _sum_slots(parts, *, name):
    _, r, c = parts.shape

    def body(p_ref, o_ref):
        g = p_ref[0]
        for s in range(1, NDEV):
            g = g + p_ref[s]
        o_ref[...] = g

    return pl.pallas_call(body, name=name, out_shape=jax.ShapeDtypeStruct((r, c), F32),
                          compiler_params=pltpu.CompilerParams(vmem_limit_bytes=VMEM_LIMIT))(parts)


def _ln_parts(u):
    mu = jnp.mean(u, axis=-1, keepdims=True)
    xc = u - mu
    var = jnp.mean(xc * xc, axis=-1, keepdims=True)
    rstd = lax.rsqrt(var + LN_EPS)
    return xc * rstd, rstd


def _ln_fwd(xin, h, g, b, *, name):
    s, d = xin.shape
    tm = _pick(s, (128,))

    def body(x_ref, h_ref, g_ref, b_ref, o_ref, ob_ref):
        xhat, _ = _ln_parts(DEEPNORM_ALPHA * x_ref[...] + h_ref[...])
        o = xhat * g_ref[...] + b_ref[...]
        o_ref[...] = o
        ob_ref[...] = o.astype(BF16)

    row = pl.BlockSpec((tm, d), lambda i: (i, 0))
    vec = pl.BlockSpec((1, d), lambda i: (0, 0))
    return pl.pallas_call(
        body, name=name, grid=(s // tm,), in_specs=[row, row, vec, vec], out_specs=(row, row),
        out_shape=(jax.ShapeDtypeStruct((s, d), F32), jax.ShapeDtypeStruct((s, d), BF16)),
        compiler_params=_params("parallel"),
    )(xin, h, g, b)


def _ln_bwd(xin, h, g, b, cot, *, with_loss, name):
    s, d = xin.shape
    tm = _pick(s, (128,))

    def body(x_ref, h_ref, g_ref, b_ref, c_ref, du_ref, dub_ref, dg_ref, db_ref, *rest):
        i = pl.program_id(0)
        xhat, rstd = _ln_parts(DEEPNORM_ALPHA * x_ref[...] + h_ref[...])
        gv = g_ref[...]
        if with_loss:
            diff = xhat * gv + b_ref[...] - c_ref[...]
            part = 0.5 * jnp.sum(jnp.mean(diff * diff, axis=-1, keepdims=True), axis=0, keepdims=True)
            dout = diff / d
        else:
            dout = c_ref[...]

        @pl.when(i == 0)
        def _():
            dg_ref[...] = jnp.zeros_like(dg_ref)
            db_ref[...] = jnp.zeros_like(db_ref)
            if with_loss:
                rest[0][...] = jnp.zeros_like(rest[0])

        dg_ref[...] += jnp.sum(dout * xhat, axis=0, keepdims=True)
        db_ref[...] += jnp.sum(dout, axis=0, keepdims=True)
        if with_loss:
            rest[0][...] += jnp.broadcast_to(part, rest[0].shape)
        dxh = dout * gv
        du = rstd * (dxh - jnp.mean(dxh, axis=-1, keepdims=True)
                     - xhat * jnp.mean(dxh * xhat, axis=-1, keepdims=True))
        du_ref[...] = du
        dub_ref[...] = du.astype(BF16)

    row = pl.BlockSpec((tm, d), lambda i: (i, 0))
    vec = pl.BlockSpec((1, d), lambda i: (0, 0))
    out_specs = [row, row, vec, vec]
    out_shape = [jax.ShapeDtypeStruct((s, d), F32), jax.ShapeDtypeStruct((s, d), BF16),
                 jax.ShapeDtypeStruct((1, d), F32), jax.ShapeDtypeStruct((1, d), F32)]
    if with_loss:
        out_specs.append(pl.BlockSpec((SUBLANES, LANES), lambda i: (0, 0)))
        out_shape.append(jax.ShapeDtypeStruct((SUBLANES, LANES), F32))
    return pl.pallas_call(
        body, name=name, grid=(s // tm,), in_specs=[row, row, vec, vec, row],
        out_specs=tuple(out_specs), out_shape=tuple(out_shape),
        compiler_params=_params("arbitrary"),
    )(xin, h, g, b, cot)


def t5_causal_bucket(dist):
    max_exact = NUM_BUCKETS // 2
    d_f = jnp.maximum(dist, 1).astype(jnp.float32)
    large = max_exact + (jnp.log(d_f / max_exact) / math.log(MAX_DISTANCE / max_exact)
                         * (NUM_BUCKETS - max_exact)).astype(jnp.int32)
    large = jnp.minimum(large, NUM_BUCKETS - 1)
    return jnp.where(dist < max_exact, dist, large)


def _bias_tables(rel_bias, heads):
    qi = lax.broadcasted_iota(jnp.int32, (ATTN_BLOCK, 2 * ATTN_BLOCK), 0)
    ki = lax.broadcasted_iota(jnp.int32, (ATTN_BLOCK, 2 * ATTN_BLOCK), 1)
    delta = ATTN_BLOCK + qi - ki
    buckets = []
    for window, dilation in ATTN_PATTERNS:
        span = window // dilation
        assert span == ATTN_BLOCK
        band = (delta >= 0) & (delta <= span)
        buckets.append(jnp.where(band, t5_causal_bucket(jnp.clip(delta, 0, None) * dilation), -1))
    bucket = jnp.stack(buckets).astype(jnp.int32)

    def body(bk_ref, tbl_ref, o_ref):
        col = pl.program_id(0) * heads + pl.program_id(1)
        bk = bk_ref[...]
        acc = jnp.full(bk.shape, NEG_INF, F32)
        for b in range(NUM_BUCKETS):
            acc = jnp.where(bk == b, tbl_ref[b, col], acc)
        o_ref[...] = acc

    tile = (None, ATTN_BLOCK, 2 * ATTN_BLOCK)
    bias = pl.pallas_call(
        body, name="bias_fwd", grid=(N_GROUPS_ATTN, heads),
        in_specs=[pl.BlockSpec(tile, lambda g, h: (g, 0, 0)), pl.BlockSpec(memory_space=pltpu.SMEM)],
        out_specs=pl.BlockSpec((None,) + tile, lambda g, h: (g, h, 0, 0)),
        out_shape=jax.ShapeDtypeStruct((N_GROUPS_ATTN, heads, ATTN_BLOCK, 2 * ATTN_BLOCK), F32),
        compiler_params=_params("parallel", "parallel"),
    )(bucket, rel_bias)
    return bias, bucket


def _dilated_view(qkv, g, dilation, da):
    del g
    return qkv.reshape(qkv.shape[0] // dilation, dilation * 3 * da), 3 * (da // HEAD_DIM), 0


def _heads_per_step(l, heads):
    for hps in (4, 2, 1):
        if heads % hps == 0 and l * hps <= ATTN_ROWS_TIMES_HEADS:
            return hps
    return 1


def _attn_fwd_group(qkv, bias_g, g, dilation, da, ride=None):
    s = qkv.shape[0]
    heads = da // HEAD_DIM
    l = s // dilation
    nb = l // ATTN_BLOCK
    view, cpb, base = _dilated_view(qkv, g, dilation, da)

    hps = _heads_per_step(l, heads)
    lanes = [slice(i * HEAD_DIM, (i + 1) * HEAD_DIM) for i in range(hps)]

    def body(q_ref, k_ref, v_ref, b_ref, o_ref, l_ref):
        scale = HEAD_DIM ** -0.5

        def block(rows, keys, first):
            q, k, v = q_ref[rows, :], k_ref[keys, :], v_ref[keys, :]
            bias = [b_ref[i, :, ATTN_BLOCK:2 * ATTN_BLOCK] if first else b_ref[i] for i in range(hps)]
            sc = [_dot_nt(q[:, hl], k[:, hl]) * scale + bias[i] for i, hl in enumerate(lanes)]
            mx = [jnp.max(t, axis=-1, keepdims=True) for t in sc]
            p = [jnp.exp(t - m) for t, m in zip(sc, mx)]
            den = [jnp.sum(t, axis=-1, keepdims=True) for t in p]
            for i, hl in enumerate(lanes):
                o_ref[rows, hl] = _dot((p[i] * (1.0 / den[i])).astype(BF16), v[:, hl])
                l_ref[rows, hl] = jnp.broadcast_to(mx[i] + jnp.log(den[i]), (ATTN_BLOCK, HEAD_DIM))

        first = pl.ds(0, ATTN_BLOCK)
        block(first, first, True)

        def step(j, carry):
            r0 = pl.multiple_of(j * ATTN_BLOCK, ATTN_BLOCK)
            rk = pl.multiple_of((j - 1) * ATTN_BLOCK, ATTN_BLOCK)
            block(pl.ds(r0, ATTN_BLOCK), pl.ds(rk, 2 * ATTN_BLOCK), False)
            return carry

        if nb > 1:
            lax.fori_loop(1, nb, step, 0)

    def col(t):
        return lambda r, h: (0, (r * cpb + base + t * heads) // hps + h)

    blk = (l, hps * HEAD_DIM)
    out = pl.BlockSpec(blk, lambda r, h: (0, r * (heads // hps) + h))
    shp = jax.ShapeDtypeStruct((l, dilation * da), F32)
    (o, lse), rode = _call(
        body, name=f"attn_fwd_g{g}", grid=(dilation, heads // hps),
        in_specs=[pl.BlockSpec(blk, col(0)), pl.BlockSpec(blk, col(1)), pl.BlockSpec(blk, col(2)),
                  pl.BlockSpec((hps, ATTN_BLOCK, 2 * ATTN_BLOCK), lambda r, h: (h, 0, 0))],
        out_specs=[out, out], out_shape=[shp, shp], args=(view, view, view, bias_g),
        sem=("parallel", "parallel"), ride=ride)
    return o.reshape(s, da), lse.reshape(s, da), rode


def _attn_combine(os_, ls_, gate):
    s, da = gate.shape
    tm = _pick(s, (512, 256, 128))
    tc = _pick(da, (512, 256, 128))

    assert da // HEAD_DIM <= LANES
    per_step = tc // HEAD_DIM

    def body(o0, o1, o2, l0, l1, l2, g_ref, o_ref, l_ref, y_ref, yt_ref):
        j = pl.program_id(1)
        a0, a1, a2 = l0[...], l1[...], l2[...]
        mx = jnp.maximum(jnp.maximum(a0, a1), a2)
        e0, e1, e2 = jnp.exp(a0 - mx), jnp.exp(a1 - mx), jnp.exp(a2 - mx)
        den = e0 + e1 + e2
        o = (e0 * o0[...] + e1 * o1[...] + e2 * o2[...]) / den
        gv = g_ref[...]
        o_ref[...] = o
        y = o * (gv * _sigmoid(gv))
        y_ref[...] = y.astype(BF16)
        yt_ref[...] = y.T.astype(BF16)
        lse = mx + jnp.log(den)

        @pl.when(j == 0)
        def _():
            l_ref[...] = jnp.zeros_like(l_ref)

        lane = lax.broadcasted_iota(jnp.int32, (1, LANES), 1)
        acc = l_ref[...]
        for i in range(per_step):
            acc = jnp.where(lane == j * per_step + i, lse[:, i * HEAD_DIM:(i + 1) * HEAD_DIM], acc)
        l_ref[...] = acc

    spec = pl.BlockSpec((tm, tc), lambda i, j: (i, j))
    heads_spec = pl.BlockSpec((tm, LANES), lambda i, j: (i, 0))
    return pl.pallas_call(
        body, name="attn_combine", grid=(s // tm, da // tc), in_specs=[spec] * 7,
        out_specs=(spec, heads_spec, spec, pl.BlockSpec((tc, tm), lambda i, j: (j, i))),
        out_shape=(jax.ShapeDtypeStruct((s, da), F32), jax.ShapeDtypeStruct((s, LANES), F32),
                   jax.ShapeDtypeStruct((s, da), BF16), jax.ShapeDtypeStruct((da, s), BF16)),
        compiler_params=_params("parallel", "arbitrary"),
    )(*os_, *ls_, gate)


def _attn_bwd_prep(dy, o, gate):
    s, da = gate.shape
    tm = _pick(s, (512, 256, 128))

    def body(dy_ref, o_ref, g_ref, do_ref, dg_ref, dd_ref):
        j = pl.program_id(1)
        gv = g_ref[...]
        sg = _sigmoid(gv)
        dyv = dy_ref[...]
        ov = o_ref[...]
        do = dyv * (gv * sg)
        do_ref[...] = do.astype(BF16)
        dg_ref[...] = (dyv * ov * (sg * (1.0 + gv * (1.0 - sg)))).astype(BF16)

        @pl.when(j == 0)
        def _():
            dd_ref[...] = jnp.zeros_like(dd_ref)

        lane = lax.broadcasted_iota(jnp.int32, (1, LANES), 1)
        dd_ref[...] = jnp.where(lane == j, jnp.sum(do * ov, axis=-1, keepdims=True), dd_ref[...])

    spec = pl.BlockSpec((tm, HEAD_DIM), lambda i, j: (i, j))
    return pl.pallas_call(
        body, name="attn_bwd_prep", grid=(s // tm, da // HEAD_DIM), in_specs=[spec] * 3,
        out_specs=(spec, spec, pl.BlockSpec((tm, LANES), lambda i, j: (i, 0))),
        out_shape=(jax.ShapeDtypeStruct((s, da), BF16), jax.ShapeDtypeStruct((s, da), BF16),
                   jax.ShapeDtypeStruct((s, LANES), F32)),
        compiler_params=_params("parallel", "arbitrary"),
    )(dy, o, gate)


def _attn_bwd_group(qkv, do, lse, dd, bias_g, g, dilation, da, ride=None):
    s = qkv.shape[0]
    heads = da // HEAD_DIM
    l = s // dilation
    nb = l // ATTN_BLOCK
    view, cpb, base = _dilated_view(qkv, g, dilation, da)
    scale = HEAD_DIM ** -0.5
    hps = _heads_per_step(l, heads)
    lanes = [slice(i * HEAD_DIM, (i + 1) * HEAD_DIM) for i in range(hps)]

    def body(q_ref, k_ref, v_ref, do_ref, l_ref, dd_ref, b_ref, dq_ref, dk_ref, dv_ref, ds_ref, dk_acc, dv_acc):
        h0 = pl.program_id(0) * hps
        r = pl.program_id(1)
        lane = lax.broadcasted_iota(jnp.int32, (1, LANES), 1)

        @pl.when(r == 0)
        def _():
            ds_ref[...] = jnp.zeros_like(ds_ref)

        dk_acc[...] = jnp.zeros_like(dk_acc)
        dv_acc[...] = jnp.zeros_like(dv_acc)

        def block(rows, keys, first):
            q, k, v, dov = q_ref[rows, :], k_ref[keys, :], v_ref[keys, :], do_ref[rows, :]
            lse_all, dd_all = l_ref[rows, :], dd_ref[rows, :]
            pick = [(lane == h0 + i).astype(F32) for i in range(hps)]
            lrow = [jnp.sum(lse_all * m, axis=-1, keepdims=True) for m in pick]
            drow = [jnp.sum(dd_all * m, axis=-1, keepdims=True) for m in pick]
            bias = [b_ref[i, :, ATTN_BLOCK:2 * ATTN_BLOCK] if first else b_ref[i] for i in range(hps)]
            sc = [_dot_nt(q[:, hl], k[:, hl]) for hl in lanes]
            dp = [_dot_nt(dov[:, hl], v[:, hl]) for hl in lanes]
            p = [jnp.exp(sc[i] * scale + bias[i] - lrow[i]) for i in range(hps)]
            ds = [p[i] * (dp[i] - drow[i]) for i in range(hps)]
            dsb = [t.astype(BF16) for t in ds]
            pb = [t.astype(BF16) for t in p]
            for i, hl in enumerate(lanes):
                dq_ref[rows, hl] = (_dot(dsb[i], k[:, hl]) * scale).astype(BF16)
                dk_acc[keys, hl] += _dot_tn(dsb[i], q[:, hl]) * scale
                dv_acc[keys, hl] += _dot_tn(pb[i], dov[:, hl])
                if first:
                    ds_ref[i, :, ATTN_BLOCK:2 * ATTN_BLOCK] += ds[i]
                else:
                    ds_ref[i] += ds[i]

        first = pl.ds(0, ATTN_BLOCK)
        block(first, first, True)

        def step(j, carry):
            r0 = pl.multiple_of(j * ATTN_BLOCK, ATTN_BLOCK)
            rk = pl.multiple_of((j - 1) * ATTN_BLOCK, ATTN_BLOCK)
            block(pl.ds(r0, ATTN_BLOCK), pl.ds(rk, 2 * ATTN_BLOCK), False)
            return carry

        if nb > 1:
            lax.fori_loop(1, nb, step, 0)
        dk_ref[...] = dk_acc[...].astype(BF16)
        dv_ref[...] = dv_acc[...].astype(BF16)

    def col(t):
        return lambda h, r: (0, (r * cpb + base + t * heads) // hps + h)

    blk = (l, hps * HEAD_DIM)
    act = pl.BlockSpec(blk, lambda h, r: (0, r * (heads // hps) + h))
    per_head = pl.BlockSpec((l, LANES), lambda h, r: (0, r))
    tile = pl.BlockSpec((hps, ATTN_BLOCK, 2 * ATTN_BLOCK), lambda h, r: (h, 0, 0))
    shp = jax.ShapeDtypeStruct((l, dilation * da), BF16)
    (dq, dk, dv, ds), rode = _call(
        body, name=f"attn_bwd_g{g}", grid=(heads // hps, dilation),
        in_specs=[pl.BlockSpec(blk, col(0)), pl.BlockSpec(blk, col(1)), pl.BlockSpec(blk, col(2)), act,
                  per_head, per_head, tile],
        out_specs=[act, act, act, tile],
        out_shape=[shp, shp, shp, jax.ShapeDtypeStruct((heads, ATTN_BLOCK, 2 * ATTN_BLOCK), F32)],
        scratch=[pltpu.VMEM(blk, F32), pltpu.VMEM(blk, F32)], sem=("parallel", "arbitrary"), ride=ride,
        args=(view, view, view, do.reshape(l, dilation * da), lse.reshape(l, dilation * LANES),
              dd.reshape(l, dilation * LANES), bias_g))
    return (dq.reshape(s, da), dk.reshape(s, da), dv.reshape(s, da), ds), rode


def _bias_bwd(ds, bucket):
    ng, heads = ds.shape[0], ds.shape[1]

    def body(ds_ref, bk_ref, o_ref):
        bk = bk_ref[...]
        x = ds_ref[...]
        for b in range(NUM_BUCKETS):
            o_ref[:, b:b + 1] = jnp.sum(jnp.where(bk == b, x, 0.0), axis=(0, 1), keepdims=True)

    tile = (None, ATTN_BLOCK, 2 * ATTN_BLOCK)
    out = pl.pallas_call(
        body, name="bias_bwd", grid=(ng, heads),
        in_specs=[pl.BlockSpec((None,) + tile, lambda g, h: (g, h, 0, 0)), pl.BlockSpec(tile, lambda g, h: (g, 0, 0))],
        out_specs=pl.BlockSpec((None, None, 1, NUM_BUCKETS), lambda g, h: (g, h, 0, 0)),
        out_shape=jax.ShapeDtypeStruct((ng, heads, 1, NUM_BUCKETS), F32),
        compiler_params=_params("parallel", "parallel"),
    )(ds, bucket)
    return out.reshape(ng, heads, NUM_BUCKETS)


def _shift_rows(x, halo, s):
    r = pltpu.roll(x, s, axis=0)
    rh = pltpu.roll(halo, s, axis=0)
    row = lax.broadcasted_iota(jnp.int32, halo.shape, 0)
    top = jnp.where(row < s, rh, r[0:SUBLANES])
    return jnp.concatenate([top, r[SUBLANES:]], axis=0)


def _conv_out(x, halo, w, b):
    acc = b + w[CONV_WIDTH - 1:CONV_WIDTH] * x
    for kk in range(CONV_WIDTH - 1):
        acc = acc + w[kk:kk + 1] * _shift_rows(x, halo, CONV_WIDTH - 1 - kk)
    return acc


def _conv_fwd(proj, conv_w, conv_b, col0):
    s = proj.shape[0]
    c = conv_w.shape[1]
    ts = _pick(s, (1024, 512, 256, 128))
    tc = _pick(math.gcd(c, col0), (512, 256, 128))
    cb0 = col0 // tc
    hb = ts // SUBLANES

    def body(x_ref, h_ref, w_ref, b_ref, o_ref):
        i = pl.program_id(0)
        halo = jnp.where(i > 0, h_ref[...], 0.0)
        u = _conv_out(x_ref[...], halo, w_ref[...], b_ref[...])
        o_ref[...] = u * _sigmoid(u)

    return pl.pallas_call(
        body, name="conv_fwd", grid=(s // ts, c // tc),
        in_specs=[pl.BlockSpec((ts, tc), lambda i, j: (i, cb0 + j)),
                  pl.BlockSpec((SUBLANES, tc), lambda i, j: (jnp.maximum(i * hb - 1, 0), cb0 + j)),
                  pl.BlockSpec((CONV_WIDTH, tc), lambda i, j: (0, j)),
                  pl.BlockSpec((1, tc), lambda i, j: (0, j))],
        out_specs=pl.BlockSpec((ts, tc), lambda i, j: (i, j)),
        out_shape=jax.ShapeDtypeStruct((s, c), F32),
        compiler_params=_params("parallel", "parallel"),
    )(proj, proj, conv_w, conv_b)


def _conv_bwd(proj, conv_w, conv_b, dacts, col0, dproj, dproj_t):
    s = proj.shape[0]
    c = conv_w.shape[1]
    widths = [d.shape[1] for d in dacts]
    assert sum(widths) == c
    ts = _pick(s, (1024, 512, 256, 128))
    tc = _pick(math.gcd(math.gcd(c, col0), math.gcd(*widths)), (512, 256, 128))
    cb0 = col0 // tc
    hb = ts // SUBLANES
    nblk = s // ts
    ext = ts + SUBLANES
    nb = [wd // tc for wd in widths]
    starts = [0, nb[0], nb[0] + nb[1]]

    def body(x_ref, xp_ref, xn_ref, d0, d1, d2, n0, n1, n2, w_ref, b_ref, _, __, dx_ref, dw_ref, db_ref, dxt_ref):
        j = pl.program_id(0)
        i = pl.program_id(1)
        last = i == nblk - 1
        w = w_ref[...]
        halo = jnp.where(i > 0, xp_ref[...], 0.0)
        x = x_ref[...]
        xe = jnp.concatenate([x, xn_ref[...]], axis=0)
        dcur = jnp.where(j < starts[1], d0[...], jnp.where(j < starts[2], d1[...], d2[...]))
        dnext = jnp.where(j < starts[1], n0[...], jnp.where(j < starts[2], n1[...], n2[...]))
        de = jnp.concatenate([dcur, jnp.where(last, 0.0, dnext)], axis=0)
        u = _conv_out(xe, halo, w, b_ref[...])
        sg = _sigmoid(u)
        dpre = de * (sg * (1.0 + u * (1.0 - sg)))
        dx = w[CONV_WIDTH - 1:CONV_WIDTH] * dpre[0:ts]
        for kk in range(CONV_WIDTH - 1):
            sh = CONV_WIDTH - 1 - kk
            dx = dx + w[kk:kk + 1] * pltpu.roll(dpre, ext - sh, axis=0)[0:ts]
        dx_ref[...] = dx.astype(BF16)
        dxt_ref[...] = dx.T.astype(BF16)
        dcur = dpre[0:ts]

        @pl.when(i == 0)
        def _():
            dw_ref[...] = jnp.zeros_like(dw_ref)
            db_ref[...] = jnp.zeros_like(db_ref)

        db_ref[...] += jnp.sum(dcur, axis=0, keepdims=True)
        dw_ref[CONV_WIDTH - 1:CONV_WIDTH, :] += jnp.sum(dcur * x, axis=0, keepdims=True)
        for kk in range(CONV_WIDTH - 1):
            xs = _shift_rows(x, halo, CONV_WIDTH - 1 - kk)
            dw_ref[kk:kk + 1, :] += jnp.sum(dcur * xs, axis=0, keepdims=True)

    cur_p = pl.BlockSpec((ts, tc), lambda j, i: (i, cb0 + j))
    prev_p = pl.BlockSpec((SUBLANES, tc), lambda j, i: (jnp.maximum(i * hb - 1, 0), cb0 + j))
    nxt = lambda i: jnp.minimum((i + 1) * hb, nblk * hb - 1)
    next_p = pl.BlockSpec((SUBLANES, tc), lambda j, i: (nxt(i), cb0 + j))

    def part(q):
        return lambda j: jnp.clip(j - starts[q], 0, nb[q] - 1)

    cur_d = [pl.BlockSpec((ts, tc), lambda j, i, f=part(q): (i, f(j))) for q in range(3)]
    next_d = [pl.BlockSpec((SUBLANES, tc), lambda j, i, f=part(q): (nxt(i), f(j))) for q in range(3)]
    vec4 = pl.BlockSpec((CONV_WIDTH, tc), lambda j, i: (0, j))
    vec1 = pl.BlockSpec((1, tc), lambda j, i: (0, j))
    hbm = pl.BlockSpec(memory_space=pl.ANY)
    return pl.pallas_call(
        body, name="conv_bwd", grid=(c // tc, nblk),
        in_specs=[cur_p, prev_p, next_p, *cur_d, *next_d, vec4, vec1, hbm, hbm],
        out_specs=(cur_p, vec4, vec1, pl.BlockSpec((tc, ts), lambda j, i: (cb0 + j, i))),
        out_shape=(jax.ShapeDtypeStruct(dproj.shape, dproj.dtype), jax.ShapeDtypeStruct((CONV_WIDTH, c), F32),
                   jax.ShapeDtypeStruct((1, c), F32), jax.ShapeDtypeStruct(dproj_t.shape, dproj_t.dtype)),
        input_output_aliases={11: 0, 12: 3},
        compiler_params=_params("parallel", "arbitrary"),
    )(proj, proj, proj, *dacts, *dacts, conv_w, conv_b, dproj, dproj_t)


def _dt_fwd(proj, dt_bias, col0):
    s = proj.shape[0]
    h = dt_bias.shape[1]
    ts = _pick(s, (1024, 512, 256, 128))

    def body(x_ref, b_ref, o_ref):
        v = x_ref[...] + b_ref[...]
        o_ref[...] = jnp.maximum(v, 0.0) + jnp.log1p(jnp.exp(-jnp.abs(v)))

    return pl.pallas_call(
        body, name="dt_fwd", grid=(s // ts,),
        in_specs=[pl.BlockSpec((ts, h), lambda i: (i, col0 // h)), pl.BlockSpec((1, h), lambda i: (0, 0))],
        out_specs=pl.BlockSpec((ts, h), lambda i: (i, 0)), out_shape=jax.ShapeDtypeStruct((s, h), F32),
        compiler_params=_params("parallel"),
    )(proj, dt_bias)


def _dt_bwd(proj, dt_bias, ddt, col0, dproj, dproj_t):
    s = proj.shape[0]
    h = dt_bias.shape[1]
    ts = _pick(s, (1024, 512, 256, 128))

    def body(x_ref, b_ref, d_ref, _, __, o_ref, db_ref, ot_ref):
        i = pl.program_id(0)
        draw = d_ref[...] * _sigmoid(x_ref[...] + b_ref[...])
        o_ref[...] = draw.astype(BF16)
        ot_ref[...] = draw.T.astype(BF16)

        @pl.when(i == 0)
        def _():
            db_ref[...] = jnp.zeros_like(db_ref)

        db_ref[...] += jnp.sum(draw, axis=0, keepdims=True)

    hbm = pl.BlockSpec(memory_space=pl.ANY)
    return pl.pallas_call(
        body, name="dt_bwd", grid=(s // ts,),
        in_specs=[pl.BlockSpec((ts, h), lambda i: (i, col0 // h)), pl.BlockSpec((1, h), lambda i: (0, 0)),
                  pl.BlockSpec((ts, h), lambda i: (i, 0)), hbm, hbm],
        out_specs=(pl.BlockSpec((ts, h), lambda i: (i, col0 // h)), pl.BlockSpec((1, h), lambda i: (0, 0)),
                   pl.BlockSpec((h, ts), lambda i: (col0 // h, i))),
        out_shape=(jax.ShapeDtypeStruct(dproj.shape, dproj.dtype), jax.ShapeDtypeStruct((1, h), F32),
                   jax.ShapeDtypeStruct(dproj_t.shape, dproj_t.dtype)),
        input_output_aliases={3: 0, 4: 2},
        compiler_params=_params("arbitrary"),
    )(proj, dt_bias, ddt, dproj, dproj_t)


def _chunk_terms(dt, dt_t, a, a_t):
    li = lax.broadcasted_iota(jnp.int32, (CHUNK, CHUNK), 0)
    si = lax.broadcasted_iota(jnp.int32, (CHUNK, CHUNK), 1)
    lower = (li >= si).astype(F32)
    upper = (li <= si).astype(F32)
    acum = jnp.dot(lower, dt * a, preferred_element_type=F32, precision=HIGHEST)
    acum_t = jnp.dot(dt_t * a_t, upper, preferred_element_type=F32, precision=HIGHEST)
    return acum, acum_t, li, si, upper


def _dot_exact01(t, m01):
    r = t.shape[0]
    hi = t.astype(BF16)
    rest = t - hi.astype(F32)
    mid = rest.astype(BF16)
    lo = (rest - mid.astype(F32)).astype(BF16)
    out = _dot(jnp.concatenate([hi, mid, lo], axis=0), m01.astype(BF16))
    return out[0:r] + out[r:2 * r] + out[2 * r:3 * r]


def _head_lanes(dt, acum, gw):
    hpg = dt.shape[1]
    p = gw // hpg
    spread = (lax.broadcasted_iota(jnp.int32, (hpg, gw), 1) // p
              == lax.broadcasted_iota(jnp.int32, (hpg, gw), 0)).astype(F32)
    both = _dot_exact01(jnp.concatenate([dt, acum], axis=0), spread)
    dt_e, acum_e = both[0:CHUNK], both[CHUNK:2 * CHUNK]
    alast_e = acum_e[CHUNK - 1:CHUNK, :]
    return dt_e, jnp.exp(acum_e), jnp.exp(alast_e - acum_e), jnp.exp(alast_e)


def _fold_heads(t, hpg):
    gw = t.shape[1]
    p = gw // hpg
    fold = (lax.broadcasted_iota(jnp.int32, (gw, hpg), 0) // p
            == lax.broadcasted_iota(jnp.int32, (gw, hpg), 1)).astype(F32)
    return _dot_exact01(t, fold)


def _ssd_fwd(xbc, proj, dt_g, dt_gt, a_g, a_gt, dskip_e, norm_w, d_inner, n_state):
    s = xbc.shape[0]
    hpg = dt_g.shape[2]
    gw = d_inner // SSM_GROUPS
    p = gw // hpg
    nc = s // CHUNK
    n = n_state
    b0 = d_inner // n
    c0 = b0 + SSM_GROUPS
    per_tile = LANES // p

    def body(xs_ref, b_ref, c_ref, dt_ref, dtt_ref, a_ref, at_ref, z_ref, dsk_ref, nw_ref,
             yn_ref, y_ref, st_ref, ynt_ref, state):
        c = pl.program_id(1)

        @pl.when(c == 0)
        def _():
            state[...] = jnp.zeros_like(state)

        st = state[...]
        st_ref[...] = st
        xs = xs_ref[...]
        bm = b_ref[...].astype(BF16)
        cm = c_ref[...].astype(BF16)
        dt = dt_ref[...]
        acum, acum_t, li, si, _ = _chunk_terms(dt, dtt_ref[...], a_ref[...], at_ref[...])
        dt_e, e_a, t_e, e_last = _head_lanes(dt, acum, gw)
        xdt = xs * dt_e
        xdtb = xdt.astype(BF16)
        cb = _dot_nt(cm, bm)
        causal = li >= si
        lane = lax.broadcasted_iota(jnp.int32, (1, LANES), 1)
        y_ref[...] = _dot(cm, st.astype(BF16)) * e_a
        for q in range(gw // LANES):
            ql = slice(q * LANES, (q + 1) * LANES)
            xq = xdtb[:, ql]
            ms = []
            for i in range(per_tile):
                h = q * per_tile + i
                decay = jnp.exp(jnp.where(causal, acum[:, h:h + 1] - acum_t[h:h + 1, :], NEG_INF))
                ms.append((cb * decay).astype(BF16))
            y_all = _dot(jnp.concatenate(ms, axis=0), xq)
            yd = y_all[0:CHUNK]
            for i in range(1, per_tile):
                yd = jnp.where(lane >= i * p, y_all[i * CHUNK:(i + 1) * CHUNK], yd)
            y_ref[:, ql] += yd
        state[...] = st * e_last + _dot_tn(bm, (xdt * t_e).astype(BF16))
        yt = y_ref[...] + xs * dsk_ref[...]
        z = z_ref[...]
        yz = yt * (z * _sigmoid(z))
        r = lax.rsqrt(jnp.mean(yz * yz, axis=-1, keepdims=True) + RMS_EPS)
        yn = yz * r * nw_ref[...]
        yn_ref[...] = yn.astype(BF16)
        ynt_ref[...] = yn.T.astype(BF16)

    wide = pl.BlockSpec((CHUNK, gw), lambda g, c: (c, g))
    return pl.pallas_call(
        body, name="ssd_fwd", grid=(SSM_GROUPS, nc),
        in_specs=[wide,
                  pl.BlockSpec((CHUNK, n), lambda g, c: (c, b0 + g)),
                  pl.BlockSpec((CHUNK, n), lambda g, c: (c, c0 + g)),
                  pl.BlockSpec((None, CHUNK, hpg), lambda g, c: (g, c, 0)),
                  pl.BlockSpec((None, hpg, CHUNK), lambda g, c: (g, 0, c)),
                  pl.BlockSpec((None, 1, hpg), lambda g, c: (g, 0, 0)),
                  pl.BlockSpec((None, hpg, 1), lambda g, c: (g, 0, 0)),
                  wide,
                  pl.BlockSpec((None, 1, gw), lambda g, c: (g, 0, 0)),
                  pl.BlockSpec((1, gw), lambda g, c: (0, g))],
        out_specs=(wide, wide, pl.BlockSpec((None, None, n, gw), lambda g, c: (g, c, 0, 0)),
                   pl.BlockSpec((gw, CHUNK), lambda g, c: (g, c))),
        out_shape=(jax.ShapeDtypeStruct((s, d_inner), BF16), jax.ShapeDtypeStruct((s, d_inner), F32),
                   jax.ShapeDtypeStruct((SSM_GROUPS, nc, n, gw), F32), jax.ShapeDtypeStruct((d_inner, s), BF16)),
        scratch_shapes=[pltpu.VMEM((n, gw), F32)],
        compiler_params=_params("parallel", "arbitrary"),
    )(xbc, xbc, xbc, dt_g, dt_gt, a_g, a_gt, proj, dskip_e, norm_w)


def _ssd_epilogue_bwd(dyn, y, xbc, proj, dskip_e, norm_w, hpg):
    s, d_inner = dyn.shape
    gw = d_inner // SSM_GROUPS
    p = gw // hpg
    rows = _pick(s, (2 * CHUNK, CHUNK))
    nc = s // rows

    def body(dyn_ref, y_ref, xs_ref, z_ref, dsk_ref, nw_ref, dy_ref, dz_ref, dnw_ref, ddsk_ref, dzt_ref):
        c = pl.program_id(1)
        xs = xs_ref[...]
        z = z_ref[...]
        yt = y_ref[...] + xs * dsk_ref[...]
        sg = _sigmoid(z)
        sz = z * sg
        yz = yt * sz
        r = lax.rsqrt(jnp.mean(yz * yz, axis=-1, keepdims=True) + RMS_EPS)
        dynv = dyn_ref[...]
        dyh = dynv * nw_ref[...]
        dyz = r * (dyh - yz * (r * r) * jnp.mean(dyh * yz, axis=-1, keepdims=True))
        dyt = dyz * sz
        dy_ref[...] = dyt
        dz = dyz * yt * (sg * (1.0 + z * (1.0 - sg)))
        dz_ref[...] = dz.astype(BF16)
        dzt_ref[...] = dz.T.astype(BF16)

        @pl.when(c == 0)
        def _():
            dnw_ref[...] = jnp.zeros_like(dnw_ref)
            ddsk_ref[...] = jnp.zeros_like(ddsk_ref)

        dnw_ref[...] += jnp.sum(dynv * yz * r, axis=0, keepdims=True)
        colsum = jnp.sum(dyt * xs, axis=0, keepdims=True)
        fold = (lax.broadcasted_iota(jnp.int32, (gw, hpg), 0) // p
                == lax.broadcasted_iota(jnp.int32, (gw, hpg), 1)).astype(F32)
        ddsk_ref[...] += jnp.dot(colsum, fold, preferred_element_type=F32, precision=HIGHEST)

    wide = pl.BlockSpec((rows, gw), lambda g, c: (c, g))
    return pl.pallas_call(
        body, name="ssd_epilogue_bwd", grid=(SSM_GROUPS, nc),
        in_specs=[wide, wide, wide, wide, pl.BlockSpec((None, 1, gw), lambda g, c: (g, 0, 0)),
                  pl.BlockSpec((1, gw), lambda g, c: (0, g))],
        out_specs=(wide, wide, pl.BlockSpec((1, gw), lambda g, c: (0, g)),
                   pl.BlockSpec((None, 1, hpg), lambda g, c: (g, 0, 0)),
                   pl.BlockSpec((gw, rows), lambda g, c: (g, c))),
        out_shape=(jax.ShapeDtypeStruct((s, d_inner), F32), jax.ShapeDtypeStruct((s, proj.shape[1]), BF16),
                   jax.ShapeDtypeStruct((1, d_inner), F32), jax.ShapeDtypeStruct((SSM_GROUPS, 1, hpg), F32),
                   jax.ShapeDtypeStruct((proj.shape[1], s), BF16)),
        compiler_params=_params("parallel", "arbitrary"),
    )(dyn, y, xbc, proj, dskip_e, norm_w)


def _ssd_scan_bwd(xbc, dt_g, dt_gt, a_g, a_gt, states, dy, dskip_e, d_inner, n_state, ride=None):
    s = xbc.shape[0]
    hpg = dt_g.shape[2]
    gw = d_inner // SSM_GROUPS
    p = gw // hpg
    nc = s // CHUNK
    n = n_state
    b0 = d_inner // n
    c0 = b0 + SSM_GROUPS
    per_tile = LANES // p

    def body(xs_ref, b_ref, c_ref, dt_ref, dtt_ref, a_ref, at_ref, st_ref, dy_ref, dsk_ref,
             dxs_ref, db_ref, dc_ref, ddt_ref, da_ref, dstate, ydiag_ref, dxd_ref):
        c = pl.program_id(1)

        @pl.when(c == 0)
        def _():
            dstate[...] = jnp.zeros_like(dstate)
            da_ref[...] = jnp.zeros_like(da_ref)

        xs = xs_ref[...]
        bm = b_ref[...].astype(BF16)
        cm = c_ref[...].astype(BF16)
        dt = dt_ref[...]
        a = a_ref[...]
        dyv = dy_ref[...]
        dsk = dsk_ref[...]
        acum, acum_t, li, si, upper = _chunk_terms(dt, dtt_ref[...], a, at_ref[...])
        dt_e, e_a, t_e, e_last = _head_lanes(dt, acum, gw)
        cb = _dot_nt(cm, bm)
        lower_mask = li >= si
        lane = lax.broadcasted_iota(jnp.int32, (1, LANES), 1)
        row_l = lax.broadcasted_iota(jnp.int32, (CHUNK, 1), 0)
        st = st_ref[...]
        stb = st.astype(BF16)
        dst = dstate[...]
        dstb = dst.astype(BF16)
        xdt = xs * dt_e
        xdtb = xdt.astype(BF16)
        dyb = dyv.astype(BF16)
        dye = dyv * e_a
        dyeb = dye.astype(BF16)
        xte = xdt * t_e
        xteb = xte.astype(BF16)
        wv = _dot(bm, dstb)
        yo = _dot(cm, stb)
        dcb = jnp.zeros((CHUNK, CHUNK), F32)
        for q in range(gw // LANES):
            ql = slice(q * LANES, (q + 1) * LANES)
            xq = xdtb[:, ql]
            dq = dyb[:, ql]
            decays, ms, mts, dqs = [], [], [], []
            for i in range(per_tile):
                h = q * per_tile + i
                decay = jnp.exp(jnp.where(lower_mask, acum[:, h:h + 1] - acum_t[h:h + 1, :], NEG_INF))
                mm = cb * decay
                mine = (lane >= i * p) & (lane < (i + 1) * p)
                decays.append(decay)
                ms.append(mm.astype(BF16))
                mts.append(mm.T.astype(BF16))
                dqs.append(jnp.where(mine, dq, jnp.zeros_like(dq)))
            dm_all = _dot_nt(jnp.concatenate(dqs, axis=0), xq)
            y_all = _dot(jnp.concatenate(ms, axis=0), xq)
            d_all = _dot(jnp.concatenate(mts, axis=0), dq)
            yd = dd = None
            for i in range(per_tile):
                rows = slice(i * CHUNK, (i + 1) * CHUNK)
                dcb = dcb + dm_all[rows] * decays[i]
                yd = y_all[rows] if i == 0 else jnp.where(lane >= i * p, y_all[rows], yd)
                dd = d_all[rows] if i == 0 else jnp.where(lane >= i * p, d_all[rows], dd)
            ydiag_ref[:, ql] = yd
            dxd_ref[:, ql] = dd
        ydiag = ydiag_ref[...]
        dxd = dxd_ref[...]
        dxdt = dxd + t_e * wv
        xw = xte * wv
        last_in = jnp.sum(xw, axis=0, keepdims=True) + e_last * jnp.sum(dst * st, axis=0, keepdims=True)
        folded = _fold_heads(jnp.concatenate(
            [dyb.astype(F32) * ydiag - xdtb.astype(F32) * dxd - xw + dye * yo, dxdt * xs,
             jnp.broadcast_to(last_in, (SUBLANES, gw))],
            axis=0), hpg)
        dalast = folded[2 * CHUNK:2 * CHUNK + 1]
        d_acum = folded[0:CHUNK] + jnp.where(row_l == CHUNK - 1, dalast, 0.0)
        ddt_x = folded[CHUNK:2 * CHUNK]
        dxs_ref[...] = dxdt * dt_e + dyv * dsk
        dbf = dcb.astype(BF16)
        dc_ref[...] = _dot_nt(dyeb, stb) + _dot(dbf, bm)
        db_ref[...] = _dot_nt(xteb, dstb) + _dot_tn(dbf, cm)
        dstate[...] = dst * e_last + _dot_tn(cm, dyeb)
        d_da = jnp.dot(upper, d_acum, preferred_element_type=F32, precision=HIGHEST)
        ddt_ref[...] = d_da * a + ddt_x
        da_ref[...] += jnp.sum(d_da * dt, axis=0, keepdims=True)

    rev = lambda c: nc - 1 - c
    wide = pl.BlockSpec((CHUNK, gw), lambda g, c: (rev(c), g))
    return _call(
        body, name="ssd_scan_bwd", grid=(SSM_GROUPS, nc),
        in_specs=[wide,
                  pl.BlockSpec((CHUNK, n), lambda g, c: (rev(c), b0 + g)),
                  pl.BlockSpec((CHUNK, n), lambda g, c: (rev(c), c0 + g)),
                  pl.BlockSpec((None, CHUNK, hpg), lambda g, c: (g, rev(c), 0)),
                  pl.BlockSpec((None, hpg, CHUNK), lambda g, c: (g, 0, rev(c))),
                  pl.BlockSpec((None, 1, hpg), lambda g, c: (g, 0, 0)),
                  pl.BlockSpec((None, hpg, 1), lambda g, c: (g, 0, 0)),
                  pl.BlockSpec((None, None, n, gw), lambda g, c: (g, rev(c), 0, 0)),
                  wide,
                  pl.BlockSpec((None, 1, gw), lambda g, c: (g, 0, 0))],
        out_specs=[wide,
                   pl.BlockSpec((CHUNK, n), lambda g, c: (rev(c), g)),
                   pl.BlockSpec((CHUNK, n), lambda g, c: (rev(c), g)),
                   pl.BlockSpec((None, CHUNK, hpg), lambda g, c: (g, rev(c), 0)),
                   pl.BlockSpec((None, 1, hpg), lambda g, c: (g, 0, 0))],
        out_shape=[jax.ShapeDtypeStruct((s, d_inner), F32),
                   jax.ShapeDtypeStruct((s, SSM_GROUPS * n), F32), jax.ShapeDtypeStruct((s, SSM_GROUPS * n), F32),
                   jax.ShapeDtypeStruct((SSM_GROUPS, s, hpg), F32), jax.ShapeDtypeStruct((SSM_GROUPS, 1, hpg), F32)],
        scratch=[pltpu.VMEM((n, gw), F32), pltpu.VMEM((CHUNK, gw), F32), pltpu.VMEM((CHUNK, gw), F32)],
        sem=("parallel", "arbitrary"), ride=ride,
        args=(xbc, xbc, xbc, dt_g, dt_gt, a_g, a_gt, states, dy, dskip_e))


def _lin(p):
    return 4 * p[0] + 2 * p[1] + p[2]


class _Gather:
    def __init__(self, arrs, rows=None, into=None):
        self.arrs = list(arrs)
        self.rows = rows
        self.into = list(into) if into is not None else []

    def out_shape(self):
        if self.rows is None:
            return [jax.ShapeDtypeStruct((NDEV,) + a.shape, a.dtype) for a in self.arrs]
        return [jax.ShapeDtypeStruct((NDEV, self.rows[1]) + a.shape[1:], a.dtype) for a in self.arrs]

    def _copies(self, ins, outs, sems):
        send_sems, recv_sems, local_sems = sems
        x, y, c = lax.axis_index("x"), lax.axis_index("y"), lax.axis_index("c")
        me, sibling = (x, y, c), (x, y, 1 - c)
        chips = [(1 - x, y), (x, 1 - y), (1 - x, 1 - y)]

        def slot(a, block):
            if self.rows is None:
                return outs[a].at[_lin(block)]
            return outs[a].at[_lin(block), pl.ds(self.rows[0], ins[a].shape[0])]

        def copy(a, k, block, to, src=None):
            rows = slot(a, block)
            return pltpu.make_async_remote_copy(
                src_ref=rows if src is None else src, dst_ref=rows,
                send_sem=send_sems.at[a * NPEER + k], recv_sem=recv_sems.at[a * NPEER + k],
                device_id=to, device_id_type=pl.DeviceIdType.MESH)

        na = len(ins)
        mine = [pltpu.make_async_copy(ins[a], slot(a, me), local_sems.at[a]) for a in range(na)]
        first = []
        for a in range(na):
            first.append(copy(a, 0, me, sibling, src=ins[a]))
            first += [copy(a, 1 + j, me, (*chip, c), src=ins[a]) for j, chip in enumerate(chips)]
        return copy, mine, first, me, sibling, chips, c, na

    def start(self, ins, outs, sems):
        _, mine, first, *_ = self._copies(ins, outs, sems)
        for cp in mine + first:
            cp.start()

    def finish(self, ins, outs, sems):
        copy, mine, first, me, sibling, chips, c, na = self._copies(ins, outs, sems)
        passed = []
        for j, chip in enumerate(chips):
            for a in range(na):
                copy(a, 1 + j, (*chip, c), me).wait_recv()
                cp = copy(a, 4 + j, (*chip, c), sibling)
                cp.start()
                passed.append(cp)
        for a in range(na):
            copy(a, 0, sibling, me).wait_recv()
            for j, chip in enumerate(chips):
                copy(a, 4 + j, (*chip, 1 - c), me).wait_recv()
        for cp in first + passed:
            cp.wait_send()
        for cp in mine:
            cp.wait()


class _Scatter:
    def __init__(self, arrs, ks=tuple(range(NDEV))):
        self.arrs = list(arrs)
        self.ks = [tuple(k) for k in ks] if isinstance(ks[0], (tuple, list)) else [tuple(ks)] * len(self.arrs)
        assert len(self.ks) == len(self.arrs)

    def out_shape(self):
        return [jax.ShapeDtypeStruct((len(k),) + a.shape[1:], a.dtype) for a, k in zip(self.arrs, self.ks)]

    def _copies(self, ins, outs, sems):
        send_sems, recv_sems, local_sems = sems
        x, y, c = lax.axis_index("x"), lax.axis_index("y"), lax.axis_index("c")
        me = (x, y, c)

        def peer(k):
            return (1 - x if k & 4 else x, 1 - y if k & 2 else y, 1 - c if k & 1 else c)

        local, remote = [], []
        for a in range(len(ins)):
            for i, k in enumerate(self.ks[a]):
                if k == 0:
                    local.append(pltpu.make_async_copy(ins[a].at[_lin(me)], outs[a].at[i], local_sems.at[a]))
                else:
                    remote.append(pltpu.make_async_remote_copy(
                        src_ref=ins[a].at[_lin(peer(k))], dst_ref=outs[a].at[i],
                        send_sem=send_sems.at[a * NPEER + k - 1], recv_sem=recv_sems.at[a * NPEER + k - 1],
                        device_id=peer(k), device_id_type=pl.DeviceIdType.MESH))
        return local, remote

    def start(self, ins, outs, sems):
        local, remote = self._copies(ins, outs, sems)
        for cp in local + remote:
            cp.start()

    def finish(self, ins, outs, sems):
        local, remote = self._copies(ins, outs, sems)
        for cp in remote:
            cp.wait_recv()
        for cp in remote:
            cp.wait_send()
        for cp in local:
            cp.wait()


def _exchange_scratch(na):
    return [pltpu.SemaphoreType.DMA((na * NPEER,)), pltpu.SemaphoreType.DMA((na * NPEER,)),
            pltpu.SemaphoreType.DMA((na,))]


def _exchange_alone(ex, *, name, in_vmem=False):
    na = len(ex.arrs)

    def body(*refs):
        ins, outs, sems = refs[:na], refs[na:2 * na], refs[2 * na:]
        ex.start(ins, outs, sems)
        ex.finish(ins, outs, sems)

    spec = pl.BlockSpec(memory_space=pltpu.VMEM if in_vmem else pl.ANY)
    return pl.pallas_call(
        body, name=name, out_shape=tuple(ex.out_shape()), in_specs=[spec] * na, out_specs=tuple([spec] * na),
        scratch_shapes=_exchange_scratch(na),
        compiler_params=pltpu.CompilerParams(vmem_limit_bytes=VMEM_LIMIT),
    )(*ex.arrs)


def _call(body, *, name, grid, in_specs, out_specs, out_shape, args, sem, scratch=(), ride=None, aliases=None):
    n_in, n_out, n_scr = len(in_specs), len(out_specs), len(scratch)
    if ride is None:
        outs = pl.pallas_call(
            body, name=name, grid=grid, in_specs=list(in_specs), out_specs=tuple(out_specs),
            out_shape=tuple(out_shape), scratch_shapes=list(scratch), input_output_aliases=aliases or {},
            compiler_params=_params(*sem))(*args)
        return tuple(outs), ()
    nx = len(ride.arrs)
    into = getattr(ride, "into", [])
    hbm = pl.BlockSpec(memory_space=pl.ANY)
    aliases = dict(aliases or {})
    aliases.update({n_in + nx + i: n_out + i for i in range(len(into))})

    def hosted(*refs):
        ins, x_in = refs[:n_in], refs[n_in:n_in + nx]
        o0 = n_in + nx + len(into)
        outs, x_out = refs[o0:o0 + n_out], refs[o0 + n_out:o0 + n_out + nx]
        s0 = o0 + n_out + nx
        scr, x_sem = refs[s0:s0 + n_scr], refs[s0 + n_scr:]
        ids = [pl.program_id(i) for i in range(len(grid))]
        first = functools.reduce(jnp.logical_and, [i == 0 for i in ids])
        last = functools.reduce(jnp.logical_and, [i == g - 1 for i, g in zip(ids, grid)])

        @pl.when(first)
        def _():
            ride.start(x_in, x_out, x_sem)

        body(*ins, *outs, *scr)

        @pl.when(last)
        def _():
            ride.finish(x_in, x_out, x_sem)

    outs = pl.pallas_call(
        hosted, name=name, grid=grid, in_specs=list(in_specs) + [hbm] * (nx + len(into)),
        out_specs=tuple(list(out_specs) + [hbm] * nx), out_shape=tuple(list(out_shape) + ride.out_shape()),
        scratch_shapes=list(scratch) + _exchange_scratch(nx), input_output_aliases=aliases,
        compiler_params=_params(*(("arbitrary",) * len(grid))))(*args, *ride.arrs, *into)
    return tuple(outs[:n_out]), tuple(outs[n_out:])


def _pack(parts):
    flat = jnp.concatenate([p.reshape(-1).astype(F32) for p in parts])
    tile = SUBLANES * LANES
    pad = (-flat.shape[0]) % tile
    return jnp.pad(flat, (0, pad)).reshape(-1, LANES)


def _unpack(buf, shapes):
    flat = buf.reshape(-1)
    out, off = [], 0
    for shp in shapes:
        size = math.prod(shp)
        out.append(flat[off:off + size].reshape(shp))
        off += size
    return out


KS_FLAT = (0, 1, 4, 5, 2, 3)
KS_DIAG = (6, 7)


def _local_step(x, target, wa, wo, ws, wos, rel_bias, conv_w, conv_b, dt_bias, a_log, d_skip, norm_w, ln_g, ln_b,
                dist=False):
    s, d = x.shape
    da = wo.shape[-2]
    heads = da // HEAD_DIM
    qkv_cols = 3 * N_GROUPS_ATTN * da
    d_inner = wos.shape[0] * (NDEV if dist else 1)
    conv_dim = conv_w.shape[1]
    ssm_heads = dt_bias.shape[1]
    hpg = ssm_heads // SSM_GROUPS
    gn = (conv_dim - d_inner) // 2
    n_state = gn // SSM_GROUPS
    gw = d_inner // SSM_GROUPS
    p = gw // hpg
    in_ssm = d_inner + conv_dim + ssm_heads
    xb, xbt = _cast_bf16(x, name="cast_x", with_transpose=True)

    per_dev = in_ssm // NDEV
    third = (per_dev // 3) // 16 * 16
    band_rows = (third, third, per_dev - 2 * third)
    qkvs, ws_all, row0 = [], None, 0
    for g in range(N_GROUPS_ATTN):
        ride = None
        if dist:
            ride = _Gather([ws[row0:row0 + band_rows[g]]], rows=(row0, per_dev), into=ws_all)
            row0 += band_rows[g]
        got = _mm(xb, wa, name=f"mm_qkv_g{g}", out_dtype=BF16, n_off=g * 3 * da, n_out=3 * da, ride=ride)
        if dist:
            ws_all = list(got[1])
            got = got[0]
        qkvs.append(got)
    if dist:
        ws = ws_all[0].reshape(in_ssm, d)
    gate = _mm(xb, wa, name="mm_gate", out_dtype=F32, n_off=qkv_cols, n_out=da)
    bias, bucket = _bias_tables(rel_bias, heads)
    os_, ls_ = [], []
    for g, (_, dil) in enumerate(ATTN_PATTERNS):
        ride = _Gather([wo]) if dist and g == 0 else None
        o, l, rode = _attn_fwd_group(qkvs[g], bias[g], g, dil, da, ride=ride)
        if rode:
            (wo,) = rode
        os_.append(o)
        ls_.append(l)
    o, lse, y, yt = _attn_combine(os_, ls_, gate)
    h1 = _mm(y, wo, name="mm_out_attn", out_dtype=F32)
    x1, x1b = _ln_fwd(x, h1, ln_g[0:1], ln_b[0:1], name="ln1_fwd")

    if dist:
        proj, (wos_slabs,) = _mm(x1b, ws, name="mm_in_ssm", out_dtype=F32, trans_b=True, ride=_Gather([wos]))
        wos = wos_slabs.reshape(d_inner, d)
    else:
        proj = _mm(x1b, ws, name="mm_in_ssm", out_dtype=F32, trans_b=True)
    xbc = _conv_fwd(proj, conv_w, conv_b, d_inner)
    dt = _dt_fwd(proj, dt_bias, d_inner + conv_dim)
    dt_g = dt.reshape(s, SSM_GROUPS, hpg).transpose(1, 0, 2)
    dt_gt = dt.reshape(s, SSM_GROUPS, hpg).transpose(1, 2, 0)
    a = -jnp.exp(a_log)
    a_g = a.reshape(SSM_GROUPS, 1, hpg)
    a_gt = a.reshape(SSM_GROUPS, hpg, 1)
    dskip_e = jnp.repeat(d_skip.reshape(SSM_GROUPS, 1, hpg), p, axis=2)
    yn, yscan, states, ynt = _ssd_fwd(xbc, proj, dt_g, dt_gt, a_g, a_gt, dskip_e, norm_w, d_inner, n_state)
    h2 = _mm(yn, wos, name="mm_out_ssm", out_dtype=F32)

    du2, du2b, dg1, db1, loss_t = _ln_bwd(x1, h2, ln_g[1:2], ln_b[1:2], target, with_loss=True, name="ln2_loss_bwd")
    loss = loss_t[0, 0]
    dyn = _mm(du2b, wos, name="mm_dyn", out_dtype=F32, trans_b=True)
    g_wos = _mm(ynt, du2b, name="mm_dw_out_ssm", out_dtype=BF16)
    parts = {}
    dyscan, dproj_ssm, g_norm, g_dskip, dproj_t = _ssd_epilogue_bwd(dyn, yscan, xbc, proj, dskip_e, norm_w, hpg)
    ride = _Scatter([g_wos.reshape(NDEV, d_inner // NDEV, d)]) if dist else None
    (dxs, d_bm, d_cm, ddt_g, g_a), rode = _ssd_scan_bwd(xbc, dt_g, dt_gt, a_g, a_gt, states, dyscan, dskip_e,
                                                         d_inner, n_state, ride=ride)
    parts["w_out_ssm"] = [list(rode)]
    g_alog = g_a.reshape(1, ssm_heads) * a
    dproj_ssm, g_conv_w, g_conv_b, dproj_t = _conv_bwd(proj, conv_w, conv_b, (dxs, d_bm, d_cm), d_inner, dproj_ssm,
                                                       dproj_t)
    ddt = ddt_g.transpose(1, 0, 2).reshape(s, ssm_heads)
    dproj_ssm, g_dtb, dproj_t = _dt_bwd(proj, dt_bias, ddt, d_inner + conv_dim, dproj_ssm, dproj_t)
    g_ws = _mm(dproj_t, x1b, name="mm_dw_in_ssm", out_dtype=BF16)
    if dist:
        g_ws_slabs = g_ws.reshape(NDEV, per_dev, d)
        dx1, near = _mm(dproj_ssm, ws, name="mm_dx1", out_dtype=F32, res=du2, res_scale=DEEPNORM_ALPHA,
                        ride=_Scatter([g_ws_slabs], KS_FLAT))
    else:
        dx1 = _mm(dproj_ssm, ws, name="mm_dx1", out_dtype=F32, res=du2, res_scale=DEEPNORM_ALPHA)

    du1, du1b, dg0, db0 = _ln_bwd(x, h1, ln_g[0:1], ln_b[0:1], dx1, with_loss=False, name="ln1_bwd")
    dy = _mm(du1b, wo, name="mm_dy", out_dtype=F32, trans_b=True)
    g_wo = _mm(yt, du1b, name="mm_dw_out_attn", out_dtype=BF16, slab_out=NDEV)
    do, dgate, dd = _attn_bwd_prep(dy, o, gate)
    rides = [_Scatter([g_ws_slabs], KS_DIAG[0:1]), _Scatter([g_wo]), None] if dist else [None] * N_GROUPS_ATTN
    dparts, dss, rode_attn = [], [], []
    for g, (_, dil) in enumerate(ATTN_PATTERNS):
        (dq, dk, dv, ds), rode = _attn_bwd_group(qkvs[g], do, lse, dd, bias[g], g, dil, da, ride=rides[g])
        dparts += [dq, dk, dv]
        dss.append(ds)
        rode_attn += list(rode)
    g_bias = _bias_bwd(jnp.stack(dss), bucket)
    g_rel_bias = g_bias.transpose(2, 0, 1).reshape(NUM_BUCKETS, N_GROUPS_ATTN * heads)
    dproj_attn = jnp.concatenate(dparts + [dgate], axis=1)
    pending = None
    if dist:
        parts["w_out_attn"] = [rode_attn[1:]]
        half = d // 2
        g_top, (diag_b,) = _mm(xbt[:half], dproj_attn, name="mm_dw_in_attn_top", out_dtype=BF16, slab_out=NDEV,
                               ride=_Scatter([g_ws_slabs], KS_DIAG[1:2]))
        parts["w_in_ssm"] = [[near[0], rode_attn[0], diag_b]]
        g_bot, (top_a,) = _mm(xbt[half:], dproj_attn, name="mm_dw_in_attn_bottom", out_dtype=BF16, slab_out=NDEV,
                              ride=_Scatter([g_top], KS_FLAT))
        dx, (top_b, bot_a) = _mm(dproj_attn, wa, name="mm_dx", out_dtype=F32, trans_b=True, res=du1,
                                 res_scale=DEEPNORM_ALPHA, ride=_Scatter([g_top, g_bot], [KS_DIAG, KS_FLAT]))
        parts["w_in_attn"] = [[top_a, top_b], [bot_a]]
        pending = [_Scatter([g_bot], KS_DIAG[0:1]), _Scatter([g_bot], KS_DIAG[1:2])]
    else:
        g_wa = _mm(xbt, dproj_attn, name="mm_dw_in_attn", out_dtype=BF16, slab_out=NDEV)
        dx = _mm(dproj_attn, wa, name="mm_dx", out_dtype=F32, trans_b=True, res=du1, res_scale=DEEPNORM_ALPHA)

    g_ln_g = jnp.concatenate([dg0, dg1], axis=0)
    g_ln_b = jnp.concatenate([db0, db1], axis=0)
    small = dict(rel_bias=g_rel_bias, dt_bias=g_dtb, a_log=g_alog, d_skip=g_dskip.reshape(1, ssm_heads),
                 ln_g=g_ln_g, ln_b=g_ln_b, conv_w=g_conv_w, conv_b=g_conv_b, ssm_norm_w=g_norm)
    if dist:
        return loss, dx, parts, pending, small
    return loss, dx, g_wa, g_wo, g_ws, g_wos, small


REPLICATED = ("rel_bias", "dt_bias", "a_log", "d_skip", "ln_g", "ln_b")
SHARDED_SMALL = ("conv_w", "conv_b", "ssm_norm_w")


def kernel(x, w_in_attn, w_out_attn, rel_bias, w_in_ssm, conv_w, conv_b, dt_bias, a_log, d_skip, ssm_norm_w, w_out_ssm, ln_g, ln_b, loss_target, m_w_in_attn, m_w_out_attn, m_rel_bias, m_w_in_ssm, m_conv_w, m_conv_b, m_dt_bias, m_a_log, m_d_skip, m_ssm_norm_w, m_w_out_ssm, m_ln_g, m_ln_b, v_w_in_attn, v_w_out_attn, v_rel_bias, v_w_in_ssm, v_conv_w, v_conv_b, v_dt_bias, v_a_log, v_d_skip, v_ssm_norm_w, v_w_out_ssm, v_ln_g, v_ln_b):
    w = dict(w_in_attn=w_in_attn, w_out_attn=w_out_attn, rel_bias=rel_bias, w_in_ssm=w_in_ssm, conv_w=conv_w,
             conv_b=conv_b, dt_bias=dt_bias, a_log=a_log, d_skip=d_skip, ssm_norm_w=ssm_norm_w,
             w_out_ssm=w_out_ssm, ln_g=ln_g, ln_b=ln_b)
    m = dict(w_in_attn=m_w_in_attn, w_out_attn=m_w_out_attn, rel_bias=m_rel_bias, w_in_ssm=m_w_in_ssm,
             conv_w=m_conv_w, conv_b=m_conv_b, dt_bias=m_dt_bias, a_log=m_a_log, d_skip=m_d_skip,
             ssm_norm_w=m_ssm_norm_w, w_out_ssm=m_w_out_ssm, ln_g=m_ln_g, ln_b=m_ln_b)
    v = dict(w_in_attn=v_w_in_attn, w_out_attn=v_w_out_attn, rel_bias=v_rel_bias, w_in_ssm=v_w_in_ssm,
             conv_w=v_conv_w, conv_b=v_conv_b, dt_bias=v_dt_bias, a_log=v_a_log, d_skip=v_d_skip,
             ssm_norm_w=v_ssm_norm_w, w_out_ssm=v_w_out_ssm, ln_g=v_ln_g, ln_b=v_ln_b)
    me = _lin((lax.axis_index("x"), lax.axis_index("y"), lax.axis_index("c")))
    d = x.shape[2]
    big = ("w_in_attn", "w_out_attn", "w_in_ssm", "w_out_ssm")

    for t in (w, m, v):
        t["w_in_ssm"] = t["w_in_ssm"].transpose(0, 2, 1)
    shards = {k: _cast_bf16(w[k], name=f"cast_{k}") for k in big}
    (wa,) = _exchange_alone(_Gather([shards["w_in_attn"]]), name="gather_w_in_attn")
    cpd = conv_w.shape[2]
    npd = ssm_norm_w.shape[1]
    small_shapes = [(CONV_WIDTH, cpd), (1, cpd), (1, npd)]
    (small_all,) = _exchange_alone(_Gather([_pack([conv_w[0], conv_b, ssm_norm_w])]), name="gather_small_weights",
                                   in_vmem=True)
    small_parts = [_unpack(small_all[i], small_shapes) for i in range(NDEV)]
    conv_w_full = jnp.concatenate([p[0] for p in small_parts], axis=1)
    conv_b_full = jnp.concatenate([p[1] for p in small_parts], axis=1)
    norm_w_full = jnp.concatenate([p[2] for p in small_parts], axis=1)

    loss, dx, parts, pending, small = _local_step(
        x[0], loss_target[0], wa, shards["w_out_attn"], shards["w_in_ssm"], shards["w_out_ssm"], rel_bias,
        conv_w_full, conv_b_full, dt_bias[0:1], a_log[0:1], d_skip[0:1], norm_w_full, ln_g, ln_b, dist=True)
    loss = lax.psum(loss, MESH_AXES)
    out = {}
    for k, ride in zip(("w_in_ssm", "w_out_ssm"), pending):
        out[k], late = _adamw_sum(parts[k], w[k], m[k], v[k], name=f"adamw_{k}", ride=ride)
        parts["w_in_attn"][1] += list(late)
    out["w_in_ssm"] = tuple(t.transpose(0, 2, 1) for t in out["w_in_ssm"])
    for k in ("w_out_attn", "w_in_attn"):
        out[k] = _adamw_sum(parts[k], w[k], m[k], v[k], name=f"adamw_{k}")

    order = REPLICATED + SHARDED_SMALL
    g_shapes = [small[k].shape for k in order]
    (g_all,) = _exchange_alone(_Gather([_pack([small[k] for k in order])]), name="gather_small_grads", in_vmem=True)
    g_sum = dict(zip(order, _unpack(_sum_slots(g_all, name="sum_small_grads"), g_shapes)))
    g_mine = {k: g_sum[k] for k in REPLICATED}
    g_mine["conv_w"] = lax.dynamic_slice_in_dim(g_sum["conv_w"], me * cpd, cpd, axis=1)
    g_mine["conv_b"] = lax.dynamic_slice_in_dim(g_sum["conv_b"], me * cpd, cpd, axis=1)
    g_mine["ssm_norm_w"] = lax.dynamic_slice_in_dim(g_sum["ssm_norm_w"], me * npd, npd, axis=1)
    w_shapes = [w[k].shape for k in order]
    g_pack = _pack([g_mine[k] for k in order])
    d_p, m_p, v_p = _adamw_small(g_pack, _pack([w[k] for k in order]), _pack([m[k] for k in order]),
                                 _pack([v[k] for k in order]), name="adamw_small")
    for k, gk, dk, mk, vk in zip(order, _unpack(g_pack, w_shapes), _unpack(d_p, w_shapes), _unpack(m_p, w_shapes),
                                 _unpack(v_p, w_shapes)):
        out[k] = (gk, dk, mk, vk)

    names = ("w_in_attn", "w_out_attn", "rel_bias", "w_in_ssm", "conv_w", "conv_b", "dt_bias", "a_log", "d_skip",
             "ssm_norm_w", "w_out_ssm", "ln_g", "ln_b")
    res = [loss, dx[None]]
    for i in range(4):
        res += [out[k][i] for k in names]
    return tuple(res)
```

```python
import functools
import math

import jax
import jax.numpy as jnp
from jax import lax
from jax.experimental import pallas as pl
from jax.experimental.pallas import tpu as pltpu

F32 = jnp.float32
BF16 = jnp.bfloat16
MESH_AXES = ("x", "y", "c")
NDEV = 8
NPEER = NDEV - 1
LANES = 128
SUBLANES = 8
VMEM_LIMIT = 52 * 1024 * 1024
MM_VMEM_BUDGET = 40 * 1024 * 1024
MM_TK_MAX = 4096
MM_TN_MAX = 1024

ATTN_PATTERNS = ((128, 1), (512, 4), (2048, 16))
N_GROUPS_ATTN = 3
HEAD_DIM = 128
ATTN_BLOCK = 128
ATTN_ROWS_TIMES_HEADS = 8192
NUM_BUCKETS = 32
MAX_DISTANCE = 2048
SSM_GROUPS = 8
CONV_WIDTH = 4
CHUNK = 128
DEPTH = 2
DEEPNORM_ALPHA = (2 * DEPTH) ** 0.25
LN_EPS = 1e-5
RMS_EPS = 1e-5
NEG_INF = -1e30
ADAM_LR = 0.001
ADAM_B1 = 0.9
ADAM_B2 = 0.999
ADAM_EPS = 1e-08
ADAM_WD = 0.01
ADAM_STEP = 10
HIGHEST = lax.Precision.HIGHEST


def _params(*sem):
    return pltpu.CompilerParams(dimension_semantics=sem, vmem_limit_bytes=VMEM_LIMIT)


def _pick(n, prefs):
    for p in prefs:
        if n % p == 0:
            return p
    return n


def _row_tile(r, limit):
    return max(t for t in range(2 * SUBLANES, limit + 1, 2 * SUBLANES) if r % t == 0)


def _dot(a, b):
    return jnp.dot(a, b, preferred_element_type=F32)


def _dot_nt(a, b):
    return lax.dot_general(a, b, (((1,), (1,)), ((), ())), preferred_element_type=F32)


def _dot_tn(a, b):
    return lax.dot_general(a, b, (((0,), (0,)), ((), ())), preferred_element_type=F32)


def _sigmoid(x):
    return 1.0 / (1.0 + jnp.exp(-x))


def _mm(a, b, *, name, out_dtype, trans_b=False, slab_out=0, n_off=0, n_out=None,
        res=None, res_scale=1.0, ride=None):
    m, k = a.shape
    slab_b = b.ndim == 3
    if slab_b:
        ns = b.shape[0]
        if trans_b:
            n, kper = b.shape[1], b.shape[2]
            assert ns * kper == k
        else:
            nper = b.shape[2]
            n = ns * nper
            assert b.shape[1] == k
    else:
        n = b.shape[0] if trans_b else b.shape[1]
        assert (b.shape[1] if trans_b else b.shape[0]) == k
    n_out = n if n_out is None else n_out
    tm = _pick(m, (1024, 640, 512, 256, 128))
    nconstraint = math.gcd(n_out, n_off) if n_off else n_out
    if slab_b and not trans_b:
        nconstraint = math.gcd(nconstraint, nper)
    if slab_out:
        nconstraint = math.gcd(nconstraint, n_out // slab_out)
    kconstraint = kper if (slab_b and trans_b) else k
    tk = max(t for t in range(LANES, min(kconstraint, MM_TK_MAX) + 1, LANES) if kconstraint % t == 0)
    nk = k // tk
    out_bytes = jnp.dtype(out_dtype).itemsize

    def vmem_bytes(t):
        return (2 * 2 * tk * (tm + t) + 2 * tm * t * out_bytes + (4 * tm * t if nk > 1 else 0)
                + (2 * 4 * tm * t if res is not None else 0))

    fits = [t for t in range(LANES, min(nconstraint, MM_TN_MAX) + 1, LANES)
            if nconstraint % t == 0 and vmem_bytes(t) <= MM_VMEM_BUDGET]
    tn = max(fits)
    nb0 = n_off // tn
    grid = (m // tm, n_out // tn, nk)

    a_spec = pl.BlockSpec((tm, tk), lambda i, j, kk: (i, kk))
    if slab_b and not trans_b:
        nps = nper // tn
        b_spec = pl.BlockSpec((None, tk, tn), lambda i, j, kk: ((j + nb0) // nps, kk, (j + nb0) % nps))
    elif slab_b and trans_b:
        kps = kper // tk
        b_spec = pl.BlockSpec((None, tn, tk), lambda i, j, kk: (kk // kps, j + nb0, kk % kps))
    elif trans_b:
        b_spec = pl.BlockSpec((tn, tk), lambda i, j, kk: (j + nb0, kk))
    else:
        b_spec = pl.BlockSpec((tk, tn), lambda i, j, kk: (kk, j + nb0))
    if slab_out:
        ops = (n_out // slab_out) // tn
        o_spec = pl.BlockSpec((None, tm, tn), lambda i, j, kk: (j // ops, i, j % ops))
        o_shape = jax.ShapeDtypeStruct((slab_out, m, n_out // slab_out), out_dtype)
    else:
        o_spec = pl.BlockSpec((tm, tn), lambda i, j, kk: (i, j))
        o_shape = jax.ShapeDtypeStruct((m, n_out), out_dtype)
    in_specs = [a_spec, b_spec]
    args = [a, b]
    if res is not None:
        in_specs.append(pl.BlockSpec((tm, tn), lambda i, j, kk: (i, j)))
        args.append(res)

    def body(*refs):
        a_ref, b_ref = refs[0], refs[1]
        r_ref = refs[2] if res is not None else None
        o_ref = refs[3] if res is not None else refs[2]
        av = a_ref[...].astype(BF16)
        bv = b_ref[...].astype(BF16)
        part = _dot_nt(av, bv) if trans_b else _dot(av, bv)

        def finish(r):
            if res is not None:
                r = r + res_scale * r_ref[...]
            o_ref[...] = r.astype(out_dtype)

        if nk == 1:
            finish(part)
            return
        acc = refs[-1]
        kk = pl.program_id(2)

        @pl.when(kk == 0)
        def _():
            acc[...] = part

        @pl.when(kk > 0)
        def _():
            acc[...] += part

        @pl.when(kk == nk - 1)
        def _():
            finish(acc[...])

    outs, rode = _call(
        body, name=name, grid=grid, in_specs=in_specs, out_specs=[o_spec], out_shape=[o_shape], args=args,
        scratch=[pltpu.VMEM((tm, tn), F32)] if nk > 1 else [], ride=ride,
        sem=("parallel", "parallel", "arbitrary"))
    return (outs[0], rode) if ride is not None else outs[0]


def _cast_bf16(w, *, name, with_transpose=False):
    r, c = w.shape[-2:]
    tr = _row_tile(r, 512)

    def body(w_ref, o_ref, *t_ref):
        v = w_ref[...]
        o_ref[...] = v.astype(BF16)
        if with_transpose:
            t_ref[0][...] = v.T.astype(BF16)

    in_spec = (pl.BlockSpec((None, tr, c), lambda i: (0, i, 0)) if w.ndim == 3
               else pl.BlockSpec((tr, c), lambda i: (i, 0)))
    out_specs = [pl.BlockSpec((tr, c), lambda i: (i, 0))]
    out_shape = [jax.ShapeDtypeStruct((r, c), BF16)]
    if with_transpose:
        out_specs.append(pl.BlockSpec((c, tr), lambda i: (0, i)))
        out_shape.append(jax.ShapeDtypeStruct((c, r), BF16))
    out = pl.pallas_call(
        body, name=name, grid=(r // tr,), in_specs=[in_spec], out_specs=tuple(out_specs),
        out_shape=tuple(out_shape), compiler_params=_params("parallel"),
    )(w)
    return out if with_transpose else out[0]


def _adam_math(w, g, m, v):
    m2 = ADAM_B1 * m + (1.0 - ADAM_B1) * g
    v2 = ADAM_B2 * v + (1.0 - ADAM_B2) * (g * g)
    m_hat = m2 / (1.0 - ADAM_B1 ** ADAM_STEP)
    v_hat = v2 / (1.0 - ADAM_B2 ** ADAM_STEP)
    delta = -ADAM_LR * (m_hat / (jnp.sqrt(v_hat) + ADAM_EPS) + ADAM_WD * w)
    return delta, m2, v2


def _adamw_sum(bands, w, m, v, *, name, ride=None):
    _, r, c = w.shape
    nband = len(bands)
    rows = r // nband
    tr = _row_tile(rows, 128)
    tc = c if (c % LANES or c <= 2560) else _pick(c, (2048, 1024, 512, 256, 128))
    nt = rows // tr
    flat = [p for band in bands for p in band]

    def body(*refs):
        p_refs = refs[:len(flat)]
        w_ref, m_ref, v_ref, g_out, d_out, m_out, v_out = refs[len(flat):]
        i = pl.program_id(0)
        g, at = None, 0
        for q, band in enumerate(bands):
            gq = None
            for p_ref in p_refs[at:at + len(band)]:
                for s in range(p_ref.shape[0]):
                    t = p_ref[s].astype(F32)
                    gq = t if gq is None else gq + t
            at += len(band)
            g = gq if q == 0 else jnp.where(i >= q * nt, gq, g)
        d, m2, v2 = _adam_math(w_ref[...], g, m_ref[...], v_ref[...])
        g_out[...] = g
        d_out[...] = d
        m_out[...] = m2
        v_out[...] = v2

    def band_spec(p, q):
        return pl.BlockSpec((p.shape[0], tr, tc), lambda i, j: (0, jnp.clip(i - q * nt, 0, nt - 1), j))

    spec = pl.BlockSpec((None, tr, tc), lambda i, j: (0, i, j))
    shp = jax.ShapeDtypeStruct((1, r, c), F32)
    outs, rode = _call(
        body, name=name, grid=(r // tr, c // tc),
        in_specs=[band_spec(p, q) for q, band in enumerate(bands) for p in band] + [spec, spec, spec],
        out_specs=[spec, spec, spec, spec], out_shape=[shp, shp, shp, shp], args=(*flat, w, m, v),
        sem=("parallel", "parallel"), ride=ride)
    return (outs, rode) if ride is not None else outs


def _adamw_small(g, w, m, v, *, name):
    shp = jax.ShapeDtypeStruct(w.shape, F32)

    def body(g_ref, w_ref, m_ref, v_ref, d_out, m_out, v_out):
        d, m2, v2 = _adam_math(w_ref[...], g_ref[...], m_ref[...], v_ref[...])
        d_out[...] = d
        m_out[...] = m2
        v_out[...] = v2

    return pl.pallas_call(body, name=name, out_shape=(shp, shp, shp),
                          compiler_params=pltpu.CompilerParams(vmem_limit_bytes=VMEM_LIMIT))(g, w, m, v)


def _sum_slots(parts, *, name):
    _, r, c = parts.shape

    def body(p_ref, o_ref):
        g = p_ref[0]
        for s in range(1, NDEV):
            g = g + p_ref[s]
        o_ref[...] = g

    return pl.pallas_call(body, name=name, out_shape=jax.ShapeDtypeStruct((r, c), F32),
                          compiler_params=pltpu.CompilerParams(vmem_limit_bytes=VMEM_LIMIT))(parts)


def _ln_parts(u):
    mu = jnp.mean(u, axis=-1, keepdims=True)
    xc = u - mu
    var = jnp.mean(xc * xc, axis=-1, keepdims=True)
    rstd = lax.rsqrt(var + LN_EPS)
    return xc * rstd, rstd


def _ln_fwd(xin, h, g, b, *, name):
    s, d = xin.shape
    tm = _pick(s, (256, 128))

    def body(x_ref, h_ref, g_ref, b_ref, o_ref, ob_ref):
        xhat, _ = _ln_parts(DEEPNORM_ALPHA * x_ref[...] + h_ref[...])
        o = xhat * g_ref[...] + b_ref[...]
        o_ref[...] = o
        ob_ref[...] = o.astype(BF16)

    row = pl.BlockSpec((tm, d), lambda i: (i, 0))
    vec = pl.BlockSpec((1, d), lambda i: (0, 0))
    return pl.pallas_call(
        body, name=name, grid=(s // tm,), in_specs=[row, row, vec, vec], out_specs=(row, row),
        out_shape=(jax.ShapeDtypeStruct((s, d), F32), jax.ShapeDtypeStruct((s, d), BF16)),
        compiler_params=_params("parallel"),
    )(xin, h, g, b)


def _ln_bwd(xin, h, g, b, cot, *, with_loss, name):
    s, d = xin.shape
    tm = _pick(s, (128,))

    def body(x_ref, h_ref, g_ref, b_ref, c_ref, du_ref, dub_ref, dg_ref, db_ref, *rest):
        i = pl.program_id(0)
        xhat, rstd = _ln_parts(DEEPNORM_ALPHA * x_ref[...] + h_ref[...])
        gv = g_ref[...]
        if with_loss:
            diff = xhat * gv + b_ref[...] - c_ref[...]
            part = 0.5 * jnp.sum(jnp.mean(diff * diff, axis=-1, keepdims=True), axis=0, keepdims=True)
            dout = diff / d
        else:
            dout = c_ref[...]

        @pl.when(i == 0)
        def _():
            dg_ref[...] = jnp.zeros_like(dg_ref)
            db_ref[...] = jnp.zeros_like(db_ref)
            if with_loss:
                rest[0][...] = jnp.zeros_like(rest[0])

        dg_ref[...] += jnp.sum(dout * xhat, axis=0, keepdims=True)
        db_ref[...] += jnp.sum(dout, axis=0, keepdims=True)
        if with_loss:
            rest[0][...] += jnp.broadcast_to(part, rest[0].shape)
        dxh = dout * gv
        du = rstd * (dxh - jnp.mean(dxh, axis=-1, keepdims=True)
                     - xhat * jnp.mean(dxh * xhat, axis=-1, keepdims=True))
        du_ref[...] = du
        dub_ref[...] = du.astype(BF16)

    row = pl.BlockSpec((tm, d), lambda i: (i, 0))
    vec = pl.BlockSpec((1, d), lambda i: (0, 0))
    out_specs = [row, row, vec, vec]
    out_shape = [jax.ShapeDtypeStruct((s, d), F32), jax.ShapeDtypeStruct((s, d), BF16),
                 jax.ShapeDtypeStruct((1, d), F32), jax.ShapeDtypeStruct((1, d), F32)]
    if with_loss:
        out_specs.append(pl.BlockSpec((SUBLANES, LANES), lambda i: (0, 0)))
        out_shape.append(jax.ShapeDtypeStruct((SUBLANES, LANES), F32))
    return pl.pallas_call(
        body, name=name, grid=(s // tm,), in_specs=[row, row, vec, vec, row],
        out_specs=tuple(out_specs), out_shape=tuple(out_shape),
        compiler_params=_params("arbitrary"),
    )(xin, h, g, b, cot)


def t5_causal_bucket(dist):
    max_exact = NUM_BUCKETS // 2
    d_f = jnp.maximum(dist, 1).astype(jnp.float32)
    large = max_exact + (jnp.log(d_f / max_exact) / math.log(MAX_DISTANCE / max_exact)
                         * (NUM_BUCKETS - max_exact)).astype(jnp.int32)
    large = jnp.minimum(large, NUM_BUCKETS - 1)
    return jnp.where(dist < max_exact, dist, large)


def _bias_tables(rel_bias, heads):
    qi = lax.broadcasted_iota(jnp.int32, (ATTN_BLOCK, 2 * ATTN_BLOCK), 0)
    ki = lax.broadcasted_iota(jnp.int32, (ATTN_BLOCK, 2 * ATTN_BLOCK), 1)
    delta = ATTN_BLOCK + qi - ki
    buckets = []
    for window, dilation in ATTN_PATTERNS:
        span = window // dilation
        assert span == ATTN_BLOCK
        band = (delta >= 0) & (delta <= span)
        buckets.append(jnp.where(band, t5_causal_bucket(jnp.clip(delta, 0, None) * dilation), -1))
    bucket = jnp.stack(buckets).astype(jnp.int32)

    def body(bk_ref, tbl_ref, o_ref):
        col = pl.program_id(0) * heads + pl.program_id(1)
        bk = bk_ref[...]
        acc = jnp.full(bk.shape, NEG_INF, F32)
        for b in range(NUM_BUCKETS):
            acc = jnp.where(bk == b, tbl_ref[b, col], acc)
        o_ref[...] = acc

    tile = (None, ATTN_BLOCK, 2 * ATTN_BLOCK)
    bias = pl.pallas_call(
        body, name="bias_fwd", grid=(N_GROUPS_ATTN, heads),
        in_specs=[pl.BlockSpec(tile, lambda g, h: (g, 0, 0)), pl.BlockSpec(memory_space=pltpu.SMEM)],
        out_specs=pl.BlockSpec((None,) + tile, lambda g, h: (g, h, 0, 0)),
        out_shape=jax.ShapeDtypeStruct((N_GROUPS_ATTN, heads, ATTN_BLOCK, 2 * ATTN_BLOCK), F32),
        compiler_params=_params("parallel", "parallel"),
    )(bucket, rel_bias)
    return bias, bucket


def _dilated_view(qkv, g, dilation, da):
    del g
    return qkv.reshape(qkv.shape[0] // dilation, dilation * 3 * da), 3 * (da // HEAD_DIM), 0


def _heads_per_step(l, heads):
    for hps in (4, 2, 1):
        if heads % hps == 0 and l * hps <= ATTN_ROWS_TIMES_HEADS:
            return hps
    return 1


def _attn_fwd_group(qkv, bias_g, g, dilation, da, ride=None):
    s = qkv.shape[0]
    heads = da // HEAD_DIM
    l = s // dilation
    nb = l // ATTN_BLOCK
    view, cpb, base = _dilated_view(qkv, g, dilation, da)

    hps = _heads_per_step(l, heads)
    lanes = [slice(i * HEAD_DIM, (i + 1) * HEAD_DIM) for i in range(hps)]

    def body(q_ref, k_ref, v_ref, b_ref, o_ref, l_ref):
        scale = HEAD_DIM ** -0.5

        def block(rows, keys, first):
            q, k, v = q_ref[rows, :], k_ref[keys, :], v_ref[keys, :]
            bias = [b_ref[i, :, ATTN_BLOCK:2 * ATTN_BLOCK] if first else b_ref[i] for i in range(hps)]
            sc = [_dot_nt(q[:, hl], k[:, hl]) * scale + bias[i] for i, hl in enumerate(lanes)]
            mx = [jnp.max(t, axis=-1, keepdims=True) for t in sc]
            p = [jnp.exp(t - m) for t, m in zip(sc, mx)]
            den = [jnp.sum(t, axis=-1, keepdims=True) for t in p]
            for i, hl in enumerate(lanes):
                o_ref[rows, hl] = _dot((p[i] * (1.0 / den[i])).astype(BF16), v[:, hl])
                l_ref[rows, hl] = jnp.broadcast_to(mx[i] + jnp.log(den[i]), (ATTN_BLOCK, HEAD_DIM))

        first = pl.ds(0, ATTN_BLOCK)
        block(first, first, True)

        def step(j, carry):
            r0 = pl.multiple_of(j * ATTN_BLOCK, ATTN_BLOCK)
            rk = pl.multiple_of((j - 1) * ATTN_BLOCK, ATTN_BLOCK)
            block(pl.ds(r0, ATTN_BLOCK), pl.ds(rk, 2 * ATTN_BLOCK), False)
            return carry

        if nb > 1:
            lax.fori_loop(1, nb, step, 0)

    def col(t):
        return lambda r, h: (0, (r * cpb + base + t * heads) // hps + h)

    blk = (l, hps * HEAD_DIM)
    out = pl.BlockSpec(blk, lambda r, h: (0, r * (heads // hps) + h))
    shp = jax.ShapeDtypeStruct((l, dilation * da), F32)
    (o, lse), rode = _call(
        body, name=f"attn_fwd_g{g}", grid=(dilation, heads // hps),
        in_specs=[pl.BlockSpec(blk, col(0)), pl.BlockSpec(blk, col(1)), pl.BlockSpec(blk, col(2)),
                  pl.BlockSpec((hps, ATTN_BLOCK, 2 * ATTN_BLOCK), lambda r, h: (h, 0, 0))],
        out_specs=[out, out], out_shape=[shp, shp], args=(view, view, view, bias_g),
        sem=("parallel", "parallel"), ride=ride)
    return o.reshape(s, da), lse.reshape(s, da), rode


def _attn_combine(os_, ls_, gate):
    s, da = gate.shape
    tm = _pick(s, (512, 256, 128))
    tc = _pick(da, (512, 256, 128))

    assert da // HEAD_DIM <= LANES
    per_step = tc // HEAD_DIM

    def body(o0, o1, o2, l0, l1, l2, g_ref, o_ref, l_ref, y_ref, yt_ref):
        j = pl.program_id(1)
        a0, a1, a2 = l0[...], l1[...], l2[...]
        mx = jnp.maximum(jnp.maximum(a0, a1), a2)
        e0, e1, e2 = jnp.exp(a0 - mx), jnp.exp(a1 - mx), jnp.exp(a2 - mx)
        den = e0 + e1 + e2
        o = (e0 * o0[...] + e1 * o1[...] + e2 * o2[...]) / den
        gv = g_ref[...]
        o_ref[...] = o
        y = o * (gv * _sigmoid(gv))
        y_ref[...] = y.astype(BF16)
        yt_ref[...] = y.T.astype(BF16)
        lse = mx + jnp.log(den)

        @pl.when(j == 0)
        def _():
            l_ref[...] = jnp.zeros_like(l_ref)

        lane = lax.broadcasted_iota(jnp.int32, (1, LANES), 1)
        acc = l_ref[...]
        for i in range(per_step):
            acc = jnp.where(lane == j * per_step + i, lse[:, i * HEAD_DIM:(i + 1) * HEAD_DIM], acc)
        l_ref[...] = acc

    spec = pl.BlockSpec((tm, tc), lambda i, j: (i, j))
    heads_spec = pl.BlockSpec((tm, LANES), lambda i, j: (i, 0))
    return pl.pallas_call(
        body, name="attn_combine", grid=(s // tm, da // tc), in_specs=[spec] * 7,
        out_specs=(spec, heads_spec, spec, pl.BlockSpec((tc, tm), lambda i, j: (j, i))),
        out_shape=(jax.ShapeDtypeStruct((s, da), F32), jax.ShapeDtypeStruct((s, LANES), F32),
                   jax.ShapeDtypeStruct((s, da), BF16), jax.ShapeDtypeStruct((da, s), BF16)),
        compiler_params=_params("parallel", "arbitrary"),
    )(*os_, *ls_, gate)


def _attn_bwd_prep(dy, o, gate):
    s, da = gate.shape
    tm = _pick(s, (512, 256, 128))

    def body(dy_ref, o_ref, g_ref, do_ref, dg_ref, dd_ref):
        j = pl.program_id(1)
        gv = g_ref[...]
        sg = _sigmoid(gv)
        dyv = dy_ref[...]
        ov = o_ref[...]
        do = dyv * (gv * sg)
        do_ref[...] = do.astype(BF16)
        dg_ref[...] = (dyv * ov * (sg * (1.0 + gv * (1.0 - sg)))).astype(BF16)

        @pl.when(j == 0)
        def _():
            dd_ref[...] = jnp.zeros_like(dd_ref)

        lane = lax.broadcasted_iota(jnp.int32, (1, LANES), 1)
        dd_ref[...] = jnp.where(lane == j, jnp.sum(do * ov, axis=-1, keepdims=True), dd_ref[...])

    spec = pl.BlockSpec((tm, HEAD_DIM), lambda i, j: (i, j))
    return pl.pallas_call(
        body, name="attn_bwd_prep", grid=(s // tm, da // HEAD_DIM), in_specs=[spec] * 3,
        out_specs=(spec, spec, pl.BlockSpec((tm, LANES), lambda i, j: (i, 0))),
        out_shape=(jax.ShapeDtypeStruct((s, da), BF16), jax.ShapeDtypeStruct((s, da), BF16),
                   jax.ShapeDtypeStruct((s, LANES), F32)),
        compiler_params=_params("parallel", "arbitrary"),
    )(dy, o, gate)


def _attn_bwd_group(qkv, do, lse, dd, bias_g, g, dilation, da, ride=None):
    s = qkv.shape[0]
    heads = da // HEAD_DIM
    l = s // dilation
    nb = l // ATTN_BLOCK
    view, cpb, base = _dilated_view(qkv, g, dilation, da)
    scale = HEAD_DIM ** -0.5
    hps = _heads_per_step(l, heads)
    lanes = [slice(i * HEAD_DIM, (i + 1) * HEAD_DIM) for i in range(hps)]

    def body(q_ref, k_ref, v_ref, do_ref, l_ref, dd_ref, b_ref, dq_ref, dk_ref, dv_ref, ds_ref, dk_acc, dv_acc):
        h0 = pl.program_id(0) * hps
        r = pl.program_id(1)
        lane = lax.broadcasted_iota(jnp.int32, (1, LANES), 1)

        @pl.when(r == 0)
        def _():
            ds_ref[...] = jnp.zeros_like(ds_ref)

        dk_acc[...] = jnp.zeros_like(dk_acc)
        dv_acc[...] = jnp.zeros_like(dv_acc)

        def block(rows, keys, first):
            q, k, v, dov = q_ref[rows, :], k_ref[keys, :], v_ref[keys, :], do_ref[rows, :]
            lse_all, dd_all = l_ref[rows, :], dd_ref[rows, :]
            pick = [(lane == h0 + i).astype(F32) for i in range(hps)]
            lrow = [jnp.sum(lse_all * m, axis=-1, keepdims=True) for m in pick]
            drow = [jnp.sum(dd_all * m, axis=-1, keepdims=True) for m in pick]
            bias = [b_ref[i, :, ATTN_BLOCK:2 * ATTN_BLOCK] if first else b_ref[i] for i in range(hps)]
            sc = [_dot_nt(q[:, hl], k[:, hl]) for hl in lanes]
            dp = [_dot_nt(dov[:, hl], v[:, hl]) for hl in lanes]
            p = [jnp.exp(sc[i] * scale + bias[i] - lrow[i]) for i in range(hps)]
            ds = [p[i] * (dp[i] - drow[i]) for i in range(hps)]
            dsb = [t.astype(BF16) for t in ds]
            pb = [t.astype(BF16) for t in p]
            for i, hl in enumerate(lanes):
                dq_ref[rows, hl] = (_dot(dsb[i], k[:, hl]) * scale).astype(BF16)
                dk_acc[keys, hl] += _dot_tn(dsb[i], q[:, hl]) * scale
                dv_acc[keys, hl] += _dot_tn(pb[i], dov[:, hl])
                if first:
                    ds_ref[i, :, ATTN_BLOCK:2 * ATTN_BLOCK] += ds[i]
                else:
                    ds_ref[i] += ds[i]

        first = pl.ds(0, ATTN_BLOCK)
        block(first, first, True)

        def step(j, carry):
            r0 = pl.multiple_of(j * ATTN_BLOCK, ATTN_BLOCK)
            rk = pl.multiple_of((j - 1) * ATTN_BLOCK, ATTN_BLOCK)
            block(pl.ds(r0, ATTN_BLOCK), pl.ds(rk, 2 * ATTN_BLOCK), False)
            return carry

        if nb > 1:
            lax.fori_loop(1, nb, step, 0)
        dk_ref[...] = dk_acc[...].astype(BF16)
        dv_ref[...] = dv_acc[...].astype(BF16)

    def col(t):
        return lambda h, r: (0, (r * cpb + base + t * heads) // hps + h)

    blk = (l, hps * HEAD_DIM)
    act = pl.BlockSpec(blk, lambda h, r: (0, r * (heads // hps) + h))
    per_head = pl.BlockSpec((l, LANES), lambda h, r: (0, r))
    tile = pl.BlockSpec((hps, ATTN_BLOCK, 2 * ATTN_BLOCK), lambda h, r: (h, 0, 0))
    shp = jax.ShapeDtypeStruct((l, dilation * da), BF16)
    (dq, dk, dv, ds), rode = _call(
        body, name=f"attn_bwd_g{g}", grid=(heads // hps, dilation),
        in_specs=[pl.BlockSpec(blk, col(0)), pl.BlockSpec(blk, col(1)), pl.BlockSpec(blk, col(2)), act,
                  per_head, per_head, tile],
        out_specs=[act, act, act, tile],
        out_shape=[shp, shp, shp, jax.ShapeDtypeStruct((heads, ATTN_BLOCK, 2 * ATTN_BLOCK), F32)],
        scratch=[pltpu.VMEM(blk, F32), pltpu.VMEM(blk, F32)], sem=("parallel", "arbitrary"), ride=ride,
        args=(view, view, view, do.reshape(l, dilation * da), lse.reshape(l, dilation * LANES),
              dd.reshape(l, dilation * LANES), bias_g))
    return (dq.reshape(s, da), dk.reshape(s, da), dv.reshape(s, da), ds), rode


def _bias_bwd(ds, bucket):
    ng, heads = ds.shape[0], ds.shape[1]

    def body(ds_ref, bk_ref, o_ref):
        bk = bk_ref[...]
        x = ds_ref[...]
        for b in range(NUM_BUCKETS):
            o_ref[:, b:b + 1] = jnp.sum(jnp.where(bk == b, x, 0.0), axis=(0, 1), keepdims=True)

    tile = (None, ATTN_BLOCK, 2 * ATTN_BLOCK)
    out = pl.pallas_call(
        body, name="bias_bwd", grid=(ng, heads),
        in_specs=[pl.BlockSpec((None,) + tile, lambda g, h: (g, h, 0, 0)), pl.BlockSpec(tile, lambda g, h: (g, 0, 0))],
        out_specs=pl.BlockSpec((None, None, 1, NUM_BUCKETS), lambda g, h: (g, h, 0, 0)),
        out_shape=jax.ShapeDtypeStruct((ng, heads, 1, NUM_BUCKETS), F32),
        compiler_params=_params("parallel", "parallel"),
    )(ds, bucket)
    return out.reshape(ng, heads, NUM_BUCKETS)


def _shift_rows(x, halo, s):
    r = pltpu.roll(x, s, axis=0)
    rh = pltpu.roll(halo, s, axis=0)
    row = lax.broadcasted_iota(jnp.int32, halo.shape, 0)
    top = jnp.where(row < s, rh, r[0:SUBLANES])
    return jnp.concatenate([top, r[SUBLANES:]], axis=0)


def _conv_out(x, halo, w, b):
    acc = b + w[CONV_WIDTH - 1:CONV_WIDTH] * x
    for kk in range(CONV_WIDTH - 1):
        acc = acc + w[kk:kk + 1] * _shift_rows(x, halo, CONV_WIDTH - 1 - kk)
    return acc


def _conv_fwd(proj, conv_w, conv_b, col0):
    s = proj.shape[0]
    c = conv_w.shape[1]
    ts = _pick(s, (1024, 512, 256, 128))
    tc = _pick(math.gcd(c, col0), (512, 256, 128))
    cb0 = col0 // tc
    hb = ts // SUBLANES

    def body(x_ref, h_ref, w_ref, b_ref, o_ref):
        i = pl.program_id(0)
        halo = jnp.where(i > 0, h_ref[...], 0.0)
        u = _conv_out(x_ref[...], halo, w_ref[...], b_ref[...])
        o_ref[...] = u * _sigmoid(u)

    return pl.pallas_call(
        body, name="conv_fwd", grid=(s // ts, c // tc),
        in_specs=[pl.BlockSpec((ts, tc), lambda i, j: (i, cb0 + j)),
                  pl.BlockSpec((SUBLANES, tc), lambda i, j: (jnp.maximum(i * hb - 1, 0), cb0 + j)),
                  pl.BlockSpec((CONV_WIDTH, tc), lambda i, j: (0, j)),
                  pl.BlockSpec((1, tc), lambda i, j: (0, j))],
        out_specs=pl.BlockSpec((ts, tc), lambda i, j: (i, j)),
        out_shape=jax.ShapeDtypeStruct((s, c), F32),
        compiler_params=_params("parallel", "parallel"),
    )(proj, proj, conv_w, conv_b)


def _conv_bwd(proj, conv_w, conv_b, dacts, col0, dproj, dproj_t):
    s = proj.shape[0]
    c = conv_w.shape[1]
    widths = [d.shape[1] for d in dacts]
    assert sum(widths) == c
    ts = _pick(s, (1024, 512, 256, 128))
    tc = _pick(math.gcd(math.gcd(c, col0), math.gcd(*widths)), (512, 256, 128))
    cb0 = col0 // tc
    hb = ts // SUBLANES
    nblk = s // ts
    ext = ts + SUBLANES
    nb = [wd // tc for wd in widths]
    starts = [0, nb[0], nb[0] + nb[1]]

    def body(x_ref, xp_ref, xn_ref, d0, d1, d2, n0, n1, n2, w_ref, b_ref, _, __, dx_ref, dw_ref, db_ref, dxt_ref):
        j = pl.program_id(0)
        i = pl.program_id(1)
        last = i == nblk - 1
        w = w_ref[...]
        halo = jnp.where(i > 0, xp_ref[...], 0.0)
        x = x_ref[...]
        xe = jnp.concatenate([x, xn_ref[...]], axis=0)
        dcur = jnp.where(j < starts[1], d0[...], jnp.where(j < starts[2], d1[...], d2[...]))
        dnext = jnp.where(j < starts[1], n0[...], jnp.where(j < starts[2], n1[...], n2[...]))
        de = jnp.concatenate([dcur, jnp.where(last, 0.0, dnext)], axis=0)
        u = _conv_out(xe, halo, w, b_ref[...])
        sg = _sigmoid(u)
        dpre = de * (sg * (1.0 + u * (1.0 - sg)))
        dx = w[CONV_WIDTH - 1:CONV_WIDTH] * dpre[0:ts]
        for kk in range(CONV_WIDTH - 1):
            sh = CONV_WIDTH - 1 - kk
            dx = dx + w[kk:kk + 1] * pltpu.roll(dpre, ext - sh, axis=0)[0:ts]
        dx_ref[...] = dx.astype(BF16)
        dxt_ref[...] = dx.T.astype(BF16)
        dcur = dpre[0:ts]

        @pl.when(i == 0)
        def _():
            dw_ref[...] = jnp.zeros_like(dw_ref)
            db_ref[...] = jnp.zeros_like(db_ref)

        db_ref[...] += jnp.sum(dcur, axis=0, keepdims=True)
        dw_ref[CONV_WIDTH - 1:CONV_WIDTH, :] += jnp.sum(dcur * x, axis=0, keepdims=True)
        for kk in range(CONV_WIDTH - 1):
            xs = _shift_rows(x, halo, CONV_WIDTH - 1 - kk)
            dw_ref[kk:kk + 1, :] += jnp.sum(dcur * xs, axis=0, keepdims=True)

    cur_p = pl.BlockSpec((ts, tc), lambda j, i: (i, cb0 + j))
    prev_p = pl.BlockSpec((SUBLANES, tc), lambda j, i: (jnp.maximum(i * hb - 1, 0), cb0 + j))
    nxt = lambda i: jnp.minimum((i + 1) * hb, nblk * hb - 1)
    next_p = pl.BlockSpec((SUBLANES, tc), lambda j, i: (nxt(i), cb0 + j))

    def part(q):
        return lambda j: jnp.clip(j - starts[q], 0, nb[q] - 1)

    cur_d = [pl.BlockSpec((ts, tc), lambda j, i, f=part(q): (i, f(j))) for q in range(3)]
    next_d = [pl.BlockSpec((SUBLANES, tc), lambda j, i, f=part(q): (nxt(i), f(j))) for q in range(3)]
    vec4 = pl.BlockSpec((CONV_WIDTH, tc), lambda j, i: (0, j))
    vec1 = pl.BlockSpec((1, tc), lambda j, i: (0, j))
    hbm = pl.BlockSpec(memory_space=pl.ANY)
    return pl.pallas_call(
        body, name="conv_bwd", grid=(c // tc, nblk),
        in_specs=[cur_p, prev_p, next_p, *cur_d, *next_d, vec4, vec1, hbm, hbm],
        out_specs=(cur_p, vec4, vec1, pl.BlockSpec((tc, ts), lambda j, i: (cb0 + j, i))),
        out_shape=(jax.ShapeDtypeStruct(dproj.shape, dproj.dtype), jax.ShapeDtypeStruct((CONV_WIDTH, c), F32),
                   jax.ShapeDtypeStruct((1, c), F32), jax.ShapeDtypeStruct(dproj_t.shape, dproj_t.dtype)),
        input_output_aliases={11: 0, 12: 3},
        compiler_params=_params("parallel", "arbitrary"),
    )(proj, proj, proj, *dacts, *dacts, conv_w, conv_b, dproj, dproj_t)


def _dt_fwd(proj, dt_bias, col0):
    s = proj.shape[0]
    h = dt_bias.shape[1]
    ts = _pick(s, (1024, 512, 256, 128))

    def body(x_ref, b_ref, o_ref):
        v = x_ref[...] + b_ref[...]
        o_ref[...] = jnp.maximum(v, 0.0) + jnp.log1p(jnp.exp(-jnp.abs(v)))

    return pl.pallas_call(
        body, name="dt_fwd", grid=(s // ts,),
        in_specs=[pl.BlockSpec((ts, h), lambda i: (i, col0 // h)), pl.BlockSpec((1, h), lambda i: (0, 0))],
        out_specs=pl.BlockSpec((ts, h), lambda i: (i, 0)), out_shape=jax.ShapeDtypeStruct((s, h), F32),
        compiler_params=_params("parallel"),
    )(proj, dt_bias)


def _dt_bwd(proj, dt_bias, ddt, col0, dproj, dproj_t):
    s = proj.shape[0]
    h = dt_bias.shape[1]
    ts = _pick(s, (1024, 512, 256, 128))

    def body(x_ref, b_ref, d_ref, _, __, o_ref, db_ref, ot_ref):
        i = pl.program_id(0)
        draw = d_ref[...] * _sigmoid(x_ref[...] + b_ref[...])
        o_ref[...] = draw.astype(BF16)
        ot_ref[...] = draw.T.astype(BF16)

        @pl.when(i == 0)
        def _():
            db_ref[...] = jnp.zeros_like(db_ref)

        db_ref[...] += jnp.sum(draw, axis=0, keepdims=True)

    hbm = pl.BlockSpec(memory_space=pl.ANY)
    return pl.pallas_call(
        body, name="dt_bwd", grid=(s // ts,),
        in_specs=[pl.BlockSpec((ts, h), lambda i: (i, col0 // h)), pl.BlockSpec((1, h), lambda i: (0, 0)),
                  pl.BlockSpec((ts, h), lambda i: (i, 0)), hbm, hbm],
        out_specs=(pl.BlockSpec((ts, h), lambda i: (i, col0 // h)), pl.BlockSpec((1, h), lambda i: (0, 0)),
                   pl.BlockSpec((h, ts), lambda i: (col0 // h, i))),
        out_shape=(jax.ShapeDtypeStruct(dproj.shape, dproj.dtype), jax.ShapeDtypeStruct((1, h), F32),
                   jax.ShapeDtypeStruct(dproj_t.shape, dproj_t.dtype)),
        input_output_aliases={3: 0, 4: 2},
        compiler_params=_params("arbitrary"),
    )(proj, dt_bias, ddt, dproj, dproj_t)


def _chunk_terms(dt, dt_t, a, a_t):
    li = lax.broadcasted_iota(jnp.int32, (CHUNK, CHUNK), 0)
    si = lax.broadcasted_iota(jnp.int32, (CHUNK, CHUNK), 1)
    lower = (li >= si).astype(F32)
    upper = (li <= si).astype(F32)
    acum = jnp.dot(lower, dt * a, preferred_element_type=F32, precision=HIGHEST)
    acum_t = jnp.dot(dt_t * a_t, upper, preferred_element_type=F32, precision=HIGHEST)
    return acum, acum_t, li, si, upper


def _dot_exact01(t, m01):
    r = t.shape[0]
    hi = t.astype(BF16)
    rest = t - hi.astype(F32)
    mid = rest.astype(BF16)
    lo = (rest - mid.astype(F32)).astype(BF16)
    out = _dot(jnp.concatenate([hi, mid, lo], axis=0), m01.astype(BF16))
    return out[0:r] + out[r:2 * r] + out[2 * r:3 * r]


def _head_lanes(dt, acum, gw):
    hpg = dt.shape[1]
    p = gw // hpg
    spread = (lax.broadcasted_iota(jnp.int32, (hpg, gw), 1) // p
              == lax.broadcasted_iota(jnp.int32, (hpg, gw), 0)).astype(F32)
    both = _dot_exact01(jnp.concatenate([dt, acum], axis=0), spread)
    dt_e, acum_e = both[0:CHUNK], both[CHUNK:2 * CHUNK]
    alast_e = acum_e[CHUNK - 1:CHUNK, :]
    return dt_e, jnp.exp(acum_e), jnp.exp(alast_e - acum_e), jnp.exp(alast_e)


def _fold_heads(t, hpg):
    gw = t.shape[1]
    p = gw // hpg
    fold = (lax.broadcasted_iota(jnp.int32, (gw, hpg), 0) // p
            == lax.broadcasted_iota(jnp.int32, (gw, hpg), 1)).astype(F32)
    return _dot_exact01(t, fold)


def _ssd_fwd(xbc, proj, dt_g, dt_gt, a_g, a_gt, dskip_e, norm_w, d_inner, n_state):
    s = xbc.shape[0]
    hpg = dt_g.shape[2]
    gw = d_inner // SSM_GROUPS
    p = gw // hpg
    nc = s // CHUNK
    n = n_state
    b0 = d_inner // n
    c0 = b0 + SSM_GROUPS
    per_tile = LANES // p

    def body(xs_ref, b_ref, c_ref, dt_ref, dtt_ref, a_ref, at_ref, z_ref, dsk_ref, nw_ref,
             yn_ref, y_ref, st_ref, ynt_ref, state):
        c = pl.program_id(1)

        @pl.when(c == 0)
        def _():
            state[...] = jnp.zeros_like(state)

        st = state[...]
        st_ref[...] = st
        xs = xs_ref[...]
        bm = b_ref[...].astype(BF16)
        cm = c_ref[...].astype(BF16)
        dt = dt_ref[...]
        acum, acum_t, li, si, _ = _chunk_terms(dt, dtt_ref[...], a_ref[...], at_ref[...])
        dt_e, e_a, t_e, e_last = _head_lanes(dt, acum, gw)
        xdt = xs * dt_e
        xdtb = xdt.astype(BF16)
        cb = _dot_nt(cm, bm)
        causal = li >= si
        lane = lax.broadcasted_iota(jnp.int32, (1, LANES), 1)
        y_ref[...] = _dot(cm, st.astype(BF16)) * e_a
        for q in range(gw // LANES):
            ql = slice(q * LANES, (q + 1) * LANES)
            xq = xdtb[:, ql]
            ms = []
            for i in range(per_tile):
                h = q * per_tile + i
                decay = jnp.exp(jnp.where(causal, acum[:, h:h + 1] - acum_t[h:h + 1, :], NEG_INF))
                ms.append((cb * decay).astype(BF16))
            y_all = _dot(jnp.concatenate(ms, axis=0), xq)
            yd = y_all[0:CHUNK]
            for i in range(1, per_tile):
                yd = jnp.where(lane >= i * p, y_all[i * CHUNK:(i + 1) * CHUNK], yd)
            y_ref[:, ql] += yd
        state[...] = st * e_last + _dot_tn(bm, (xdt * t_e).astype(BF16))
        yt = y_ref[...] + xs * dsk_ref[...]
        z = z_ref[...]
        yz = yt * (z * _sigmoid(z))
        r = lax.rsqrt(jnp.mean(yz * yz, axis=-1, keepdims=True) + RMS_EPS)
        yn = yz * r * nw_ref[...]
        yn_ref[...] = yn.astype(BF16)
        ynt_ref[...] = yn.T.astype(BF16)

    wide = pl.BlockSpec((CHUNK, gw), lambda g, c: (c, g))
    return pl.pallas_call(
        body, name="ssd_fwd", grid=(SSM_GROUPS, nc),
        in_specs=[wide,
                  pl.BlockSpec((CHUNK, n), lambda g, c: (c, b0 + g)),
                  pl.BlockSpec((CHUNK, n), lambda g, c: (c, c0 + g)),
                  pl.BlockSpec((None, CHUNK, hpg), lambda g, c: (g, c, 0)),
                  pl.BlockSpec((None, hpg, CHUNK), lambda g, c: (g, 0, c)),
                  pl.BlockSpec((None, 1, hpg), lambda g, c: (g, 0, 0)),
                  pl.BlockSpec((None, hpg, 1), lambda g, c: (g, 0, 0)),
                  wide,
                  pl.BlockSpec((None, 1, gw), lambda g, c: (g, 0, 0)),
                  pl.BlockSpec((1, gw), lambda g, c: (0, g))],
        out_specs=(wide, wide, pl.BlockSpec((None, None, n, gw), lambda g, c: (g, c, 0, 0)),
                   pl.BlockSpec((gw, CHUNK), lambda g, c: (g, c))),
        out_shape=(jax.ShapeDtypeStruct((s, d_inner), BF16), jax.ShapeDtypeStruct((s, d_inner), F32),
                   jax.ShapeDtypeStruct((SSM_GROUPS, nc, n, gw), F32), jax.ShapeDtypeStruct((d_inner, s), BF16)),
        scratch_shapes=[pltpu.VMEM((n, gw), F32)],
        compiler_params=_params("parallel", "arbitrary"),
    )(xbc, xbc, xbc, dt_g, dt_gt, a_g, a_gt, proj, dskip_e, norm_w)


def _ssd_epilogue_bwd(dyn, y, xbc, proj, dskip_e, norm_w, hpg):
    s, d_inner = dyn.shape
    gw = d_inner // SSM_GROUPS
    p = gw // hpg
    rows = _pick(s, (4 * CHUNK, 2 * CHUNK, CHUNK))
    nc = s // rows

    def body(dyn_ref, y_ref, xs_ref, z_ref, dsk_ref, nw_ref, dy_ref, dz_ref, dnw_ref, ddsk_ref, dzt_ref):
        c = pl.program_id(1)
        xs = xs_ref[...]
        z = z_ref[...]
        yt = y_ref[...] + xs * dsk_ref[...]
        sg = _sigmoid(z)
        sz = z * sg
        yz = yt * sz
        r = lax.rsqrt(jnp.mean(yz * yz, axis=-1, keepdims=True) + RMS_EPS)
        dynv = dyn_ref[...]
        dyh = dynv * nw_ref[...]
        dyz = r * (dyh - yz * (r * r) * jnp.mean(dyh * yz, axis=-1, keepdims=True))
        dyt = dyz * sz
        dy_ref[...] = dyt
        dz = dyz * yt * (sg * (1.0 + z * (1.0 - sg)))
        dz_ref[...] = dz.astype(BF16)
        dzt_ref[...] = dz.T.astype(BF16)

        @pl.when(c == 0)
        def _():
            dnw_ref[...] = jnp.zeros_like(dnw_ref)
            ddsk_ref[...] = jnp.zeros_like(ddsk_ref)

        dnw_ref[...] += jnp.sum(dynv * yz * r, axis=0, keepdims=True)
        colsum = jnp.sum(dyt * xs, axis=0, keepdims=True)
        fold = (lax.broadcasted_iota(jnp.int32, (gw, hpg), 0) // p
                == lax.broadcasted_iota(jnp.int32, (gw, hpg), 1)).astype(F32)
        ddsk_ref[...] += jnp.dot(colsum, fold, preferred_element_type=F32, precision=HIGHEST)

    wide = pl.BlockSpec((rows, gw), lambda g, c: (c, g))
    return pl.pallas_call(
        body, name="ssd_epilogue_bwd", grid=(SSM_GROUPS, nc),
        in_specs=[wide, wide, wide, wide, pl.BlockSpec((None, 1, gw), lambda g, c: (g, 0, 0)),
                  pl.BlockSpec((1, gw), lambda g, c: (0, g))],
        out_specs=(wide, wide, pl.BlockSpec((1, gw), lambda g, c: (0, g)),
                   pl.BlockSpec((None, 1, hpg), lambda g, c: (g, 0, 0)),
                   pl.BlockSpec((gw, rows), lambda g, c: (g, c))),
        out_shape=(jax.ShapeDtypeStruct((s, d_inner), F32), jax.ShapeDtypeStruct((s, proj.shape[1]), BF16),
                   jax.ShapeDtypeStruct((1, d_inner), F32), jax.ShapeDtypeStruct((SSM_GROUPS, 1, hpg), F32),
                   jax.ShapeDtypeStruct((proj.shape[1], s), BF16)),
        compiler_params=_params("parallel", "arbitrary"),
    )(dyn, y, xbc, proj, dskip_e, norm_w)


def _ssd_scan_bwd(xbc, dt_g, dt_gt, a_g, a_gt, states, dy, dskip_e, d_inner, n_state, ride=None):
    s = xbc.shape[0]
    hpg = dt_g.shape[2]
    gw = d_inner // SSM_GROUPS
    p = gw // hpg
    nc = s // CHUNK
    n = n_state
    b0 = d_inner // n
    c0 = b0 + SSM_GROUPS
    per_tile = LANES // p

    def body(xs_ref, b_ref, c_ref, dt_ref, dtt_ref, a_ref, at_ref, st_ref, dy_ref, dsk_ref,
             dxs_ref, db_ref, dc_ref, ddt_ref, da_ref, dstate, ydiag_ref, dxd_ref):
        c = pl.program_id(1)

        @pl.when(c == 0)
        def _():
            dstate[...] = jnp.zeros_like(dstate)
            da_ref[...] = jnp.zeros_like(da_ref)

        xs = xs_ref[...]
        bm = b_ref[...].astype(BF16)
        cm = c_ref[...].astype(BF16)
        dt = dt_ref[...]
        a = a_ref[...]
        dyv = dy_ref[...]
        dsk = dsk_ref[...]
        acum, acum_t, li, si, upper = _chunk_terms(dt, dtt_ref[...], a, at_ref[...])
        dt_e, e_a, t_e, e_last = _head_lanes(dt, acum, gw)
        cb = _dot_nt(cm, bm)
        lower_mask = li >= si
        lane = lax.broadcasted_iota(jnp.int32, (1, LANES), 1)
        row_l = lax.broadcasted_iota(jnp.int32, (CHUNK, 1), 0)
        st = st_ref[...]
        stb = st.astype(BF16)
        dst = dstate[...]
        dstb = dst.astype(BF16)
        xdt = xs * dt_e
        xdtb = xdt.astype(BF16)
        dyb = dyv.astype(BF16)
        dye = dyv * e_a
        dyeb = dye.astype(BF16)
        xte = xdt * t_e
        xteb = xte.astype(BF16)
        wv = _dot(bm, dstb)
        yo = _dot(cm, stb)
        dcb = jnp.zeros((CHUNK, CHUNK), F32)
        for q in range(gw // LANES):
            ql = slice(q * LANES, (q + 1) * LANES)
            xq = xdtb[:, ql]
            dq = dyb[:, ql]
            decays, ms, mts, dqs = [], [], [], []
            for i in range(per_tile):
                h = q * per_tile + i
                decay = jnp.exp(jnp.where(lower_mask, acum[:, h:h + 1] - acum_t[h:h + 1, :], NEG_INF))
                mm = cb * decay
                mine = (lane >= i * p) & (lane < (i + 1) * p)
                decays.append(decay)
                ms.append(mm.astype(BF16))
                mts.append(mm.T.astype(BF16))
                dqs.append(jnp.where(mine, dq, jnp.zeros_like(dq)))
            dm_all = _dot_nt(jnp.concatenate(dqs, axis=0), xq)
            y_all = _dot(jnp.concatenate(ms, axis=0), xq)
            d_all = _dot(jnp.concatenate(mts, axis=0), dq)
            yd = dd = None
            for i in range(per_tile):
                rows = slice(i * CHUNK, (i + 1) * CHUNK)
                dcb = dcb + dm_all[rows] * decays[i]
                yd = y_all[rows] if i == 0 else jnp.where(lane >= i * p, y_all[rows], yd)
                dd = d_all[rows] if i == 0 else jnp.where(lane >= i * p, d_all[rows], dd)
            ydiag_ref[:, ql] = yd
            dxd_ref[:, ql] = dd
        ydiag = ydiag_ref[...]
        dxd = dxd_ref[...]
        dxdt = dxd + t_e * wv
        xw = xte * wv
        last_in = jnp.sum(xw, axis=0, keepdims=True) + e_last * jnp.sum(dst * st, axis=0, keepdims=True)
        folded = _fold_heads(jnp.concatenate(
            [dyb.astype(F32) * ydiag - xdtb.astype(F32) * dxd - xw + dye * yo, dxdt * xs,
             jnp.broadcast_to(last_in, (SUBLANES, gw))],
            axis=0), hpg)
        dalast = folded[2 * CHUNK:2 * CHUNK + 1]
        d_acum = folded[0:CHUNK] + jnp.where(row_l == CHUNK - 1, dalast, 0.0)
        ddt_x = folded[CHUNK:2 * CHUNK]
        dxs_ref[...] = dxdt * dt_e + dyv * dsk
        dbf = dcb.astype(BF16)
        dc_ref[...] = _dot_nt(dyeb, stb) + _dot(dbf, bm)
        db_ref[...] = _dot_nt(xteb, dstb) + _dot_tn(dbf, cm)
        dstate[...] = dst * e_last + _dot_tn(cm, dyeb)
        d_da = jnp.dot(upper, d_acum, preferred_element_type=F32, precision=HIGHEST)
        ddt_ref[...] = d_da * a + ddt_x
        da_ref[...] += jnp.sum(d_da * dt, axis=0, keepdims=True)

    rev = lambda c: nc - 1 - c
    wide = pl.BlockSpec((CHUNK, gw), lambda g, c: (rev(c), g))
    return _call(
        body, name="ssd_scan_bwd", grid=(SSM_GROUPS, nc),
        in_specs=[wide,
                  pl.BlockSpec((CHUNK, n), lambda g, c: (rev(c), b0 + g)),
                  pl.BlockSpec((CHUNK, n), lambda g, c: (rev(c), c0 + g)),
                  pl.BlockSpec((None, CHUNK, hpg), lambda g, c: (g, rev(c), 0)),
                  pl.BlockSpec((None, hpg, CHUNK), lambda g, c: (g, 0, rev(c))),
                  pl.BlockSpec((None, 1, hpg), lambda g, c: (g, 0, 0)),
                  pl.BlockSpec((None, hpg, 1), lambda g, c: (g, 0, 0)),
                  pl.BlockSpec((None, None, n, gw), lambda g, c: (g, rev(c), 0, 0)),
                  wide,
                  pl.BlockSpec((None, 1, gw), lambda g, c: (g, 0, 0))],
        out_specs=[wide,
                   pl.BlockSpec((CHUNK, n), lambda g, c: (rev(c), g)),
                   pl.BlockSpec((CHUNK, n), lambda g, c: (rev(c), g)),
                   pl.BlockSpec((None, CHUNK, hpg), lambda g, c: (g, rev(c), 0)),
                   pl.BlockSpec((None, 1, hpg), lambda g, c: (g, 0, 0))],
        out_shape=[jax.ShapeDtypeStruct((s, d_inner), F32),
                   jax.ShapeDtypeStruct((s, SSM_GROUPS * n), F32), jax.ShapeDtypeStruct((s, SSM_GROUPS * n), F32),
                   jax.ShapeDtypeStruct((SSM_GROUPS, s, hpg), F32), jax.ShapeDtypeStruct((SSM_GROUPS, 1, hpg), F32)],
        scratch=[pltpu.VMEM((n, gw), F32), pltpu.VMEM((CHUNK, gw), F32), pltpu.VMEM((CHUNK, gw), F32)],
        sem=("parallel", "arbitrary"), ride=ride,
        args=(xbc, xbc, xbc, dt_g, dt_gt, a_g, a_gt, states, dy, dskip_e))


def _lin(p):
    return 4 * p[0] + 2 * p[1] + p[2]


class _Gather:
    def __init__(self, arrs, rows=None, into=None):
        self.arrs = list(arrs)
        self.rows = rows
        self.into = list(into) if into is not None else []

    def out_shape(self):
        if self.rows is None:
            return [jax.ShapeDtypeStruct((NDEV,) + a.shape, a.dtype) for a in self.arrs]
        return [jax.ShapeDtypeStruct((NDEV, self.rows[1]) + a.shape[1:], a.dtype) for a in self.arrs]

    def _copies(self, ins, outs, sems):
        send_sems, recv_sems, local_sems = sems
        x, y, c = lax.axis_index("x"), lax.axis_index("y"), lax.axis_index("c")
        me, sibling = (x, y, c), (x, y, 1 - c)
        chips = [(1 - x, y), (x, 1 - y), (1 - x, 1 - y)]

        def slot(a, block):
            if self.rows is None:
                return outs[a].at[_lin(block)]
            return outs[a].at[_lin(block), pl.ds(self.rows[0], ins[a].shape[0])]

        def copy(a, k, block, to, src=None):
            rows = slot(a, block)
            return pltpu.make_async_remote_copy(
                src_ref=rows if src is None else src, dst_ref=rows,
                send_sem=send_sems.at[a * NPEER + k], recv_sem=recv_sems.at[a * NPEER + k],
                device_id=to, device_id_type=pl.DeviceIdType.MESH)

        na = len(ins)
        mine = [pltpu.make_async_copy(ins[a], slot(a, me), local_sems.at[a]) for a in range(na)]
        first = []
        for a in range(na):
            first.append(copy(a, 0, me, sibling, src=ins[a]))
            first += [copy(a, 1 + j, me, (*chip, c), src=ins[a]) for j, chip in enumerate(chips)]
        return copy, mine, first, me, sibling, chips, c, na

    def start(self, ins, outs, sems):
        _, mine, first, *_ = self._copies(ins, outs, sems)
        for cp in mine + first:
            cp.start()

    def finish(self, ins, outs, sems):
        copy, mine, first, me, sibling, chips, c, na = self._copies(ins, outs, sems)
        passed = []
        for j, chip in enumerate(chips):
            for a in range(na):
                copy(a, 1 + j, (*chip, c), me).wait_recv()
                cp = copy(a, 4 + j, (*chip, c), sibling)
                cp.start()
                passed.append(cp)
        for a in range(na):
            copy(a, 0, sibling, me).wait_recv()
            for j, chip in enumerate(chips):
                copy(a, 4 + j, (*chip, 1 - c), me).wait_recv()
        for cp in first + passed:
            cp.wait_send()
        for cp in mine:
            cp.wait()


class _Scatter:
    def __init__(self, arrs, ks=tuple(range(NDEV))):
        self.arrs = list(arrs)
        self.ks = [tuple(k) for k in ks] if isinstance(ks[0], (tuple, list)) else [tuple(ks)] * len(self.arrs)
        assert len(self.ks) == len(self.arrs)

    def out_shape(self):
        return [jax.ShapeDtypeStruct((len(k),) + a.shape[1:], a.dtype) for a, k in zip(self.arrs, self.ks)]

    def _copies(self, ins, outs, sems):
        send_sems, recv_sems, local_sems = sems
        x, y, c = lax.axis_index("x"), lax.axis_index("y"), lax.axis_index("c")
        me = (x, y, c)

        def peer(k):
            return (1 - x if k & 4 else x, 1 - y if k & 2 else y, 1 - c if k & 1 else c)

        local, remote = [], []
        for a in range(len(ins)):
            for i, k in enumerate(self.ks[a]):
                if k == 0:
                    local.append(pltpu.make_async_copy(ins[a].at[_lin(me)], outs[a].at[i], local_sems.at[a]))
                else:
                    remote.append(pltpu.make_async_remote_copy(
                        src_ref=ins[a].at[_lin(peer(k))], dst_ref=outs[a].at[i],
                        send_sem=send_sems.at[a * NPEER + k - 1], recv_sem=recv_sems.at[a * NPEER + k - 1],
                        device_id=peer(k), device_id_type=pl.DeviceIdType.MESH))
        return local, remote

    def start(self, ins, outs, sems):
        local, remote = self._copies(ins, outs, sems)
        for cp in local + remote:
            cp.start()

    def finish(self, ins, outs, sems):
        local, remote = self._copies(ins, outs, sems)
        for cp in remote:
            cp.wait_recv()
        for cp in remote:
            cp.wait_send()
        for cp in local:
            cp.wait()


def _exchange_scratch(na):
    return [pltpu.SemaphoreType.DMA((na * NPEER,)), pltpu.SemaphoreType.DMA((na * NPEER,)),
            pltpu.SemaphoreType.DMA((na,))]


def _exchange_alone(ex, *, name, in_vmem=False):
    na = len(ex.arrs)

    def body(*refs):
        ins, outs, sems = refs[:na], refs[na:2 * na], refs[2 * na:]
        ex.start(ins, outs, sems)
        ex.finish(ins, outs, sems)

    spec = pl.BlockSpec(memory_space=pltpu.VMEM if in_vmem else pl.ANY)
    return pl.pallas_call(
        body, name=name, out_shape=tuple(ex.out_shape()), in_specs=[spec] * na, out_specs=tuple([spec] * na),
        scratch_shapes=_exchange_scratch(na),
        compiler_params=pltpu.CompilerParams(vmem_limit_bytes=VMEM_LIMIT),
    )(*ex.arrs)


def _call(body, *, name, grid, in_specs, out_specs, out_shape, args, sem, scratch=(), ride=None, aliases=None):
    n_in, n_out, n_scr = len(in_specs), len(out_specs), len(scratch)
    if ride is None:
        outs = pl.pallas_call(
            body, name=name, grid=grid, in_specs=list(in_specs), out_specs=tuple(out_specs),
            out_shape=tuple(out_shape), scratch_shapes=list(scratch), input_output_aliases=aliases or {},
            compiler_params=_params(*sem))(*args)
        return tuple(outs), ()
    nx = len(ride.arrs)
    into = getattr(ride, "into", [])
    hbm = pl.BlockSpec(memory_space=pl.ANY)
    aliases = dict(aliases or {})
    aliases.update({n_in + nx + i: n_out + i for i in range(len(into))})

    def hosted(*refs):
        ins, x_in = refs[:n_in], refs[n_in:n_in + nx]
        o0 = n_in + nx + len(into)
        outs, x_out = refs[o0:o0 + n_out], refs[o0 + n_out:o0 + n_out + nx]
        s0 = o0 + n_out + nx
        scr, x_sem = refs[s0:s0 + n_scr], refs[s0 + n_scr:]
        ids = [pl.program_id(i) for i in range(len(grid))]
        first = functools.reduce(jnp.logical_and, [i == 0 for i in ids])
        last = functools.reduce(jnp.logical_and, [i == g - 1 for i, g in zip(ids, grid)])

        @pl.when(first)
        def _():
            ride.start(x_in, x_out, x_sem)

        body(*ins, *outs, *scr)

        @pl.when(last)
        def _():
            ride.finish(x_in, x_out, x_sem)

    outs = pl.pallas_call(
        hosted, name=name, grid=grid, in_specs=list(in_specs) + [hbm] * (nx + len(into)),
        out_specs=tuple(list(out_specs) + [hbm] * nx), out_shape=tuple(list(out_shape) + ride.out_shape()),
        scratch_shapes=list(scratch) + _exchange_scratch(nx), input_output_aliases=aliases,
        compiler_params=_params(*(("arbitrary",) * len(grid))))(*args, *ride.arrs, *into)
    return tuple(outs[:n_out]), tuple(outs[n_out:])


def _pack(parts):
    flat = jnp.concatenate([p.reshape(-1).astype(F32) for p in parts])
    tile = SUBLANES * LANES
    pad = (-flat.shape[0]) % tile
    return jnp.pad(flat, (0, pad)).reshape(-1, LANES)


def _unpack(buf, shapes):
    flat = buf.reshape(-1)
    out, off = [], 0
    for shp in shapes:
        size = math.prod(shp)
        out.append(flat[off:off + size].reshape(shp))
        off += size
    return out


KS_FLAT = (0, 1, 4, 5, 2, 3)
KS_DIAG = (6, 7)


def _local_step(x, target, wa, wo, ws, wos, rel_bias, conv_w, conv_b, dt_bias, a_log, d_skip, norm_w, ln_g, ln_b,
                dist=False):
    s, d = x.shape
    da = wo.shape[-2]
    heads = da // HEAD_DIM
    qkv_cols = 3 * N_GROUPS_ATTN * da
    d_inner = wos.shape[0] * (NDEV if dist else 1)
    conv_dim = conv_w.shape[1]
    ssm_heads = dt_bias.shape[1]
    hpg = ssm_heads // SSM_GROUPS
    gn = (conv_dim - d_inner) // 2
    n_state = gn // SSM_GROUPS
    gw = d_inner // SSM_GROUPS
    p = gw // hpg
    in_ssm = d_inner + conv_dim + ssm_heads
    xb, xbt = _cast_bf16(x, name="cast_x", with_transpose=True)

    per_dev = in_ssm // NDEV
    third = (per_dev // 3) // 16 * 16
    band_rows = (third, third, per_dev - 2 * third)
    qkvs, ws_all, row0 = [], None, 0
    for g in range(N_GROUPS_ATTN):
        ride = None
        if dist:
            ride = _Gather([ws[row0:row0 + band_rows[g]]], rows=(row0, per_dev), into=ws_all)
            row0 += band_rows[g]
        got = _mm(xb, wa, name=f"mm_qkv_g{g}", out_dtype=BF16, n_off=g * 3 * da, n_out=3 * da, ride=ride)
        if dist:
            ws_all = list(got[1])
            got = got[0]
        qkvs.append(got)
    if dist:
        ws = ws_all[0].reshape(in_ssm, d)
    gate = _mm(xb, wa, name="mm_gate", out_dtype=F32, n_off=qkv_cols, n_out=da)
    bias, bucket = _bias_tables(rel_bias, heads)
    os_, ls_ = [], []
    for g, (_, dil) in enumerate(ATTN_PATTERNS):
        ride = _Gather([wo]) if dist and g == 0 else None
        o, l, rode = _attn_fwd_group(qkvs[g], bias[g], g, dil, da, ride=ride)
        if rode:
            (wo,) = rode
        os_.append(o)
        ls_.append(l)
    o, lse, y, yt = _attn_combine(os_, ls_, gate)
    h1 = _mm(y, wo, name="mm_out_attn", out_dtype=F32)
    x1, x1b = _ln_fwd(x, h1, ln_g[0:1], ln_b[0:1], name="ln1_fwd")

    if dist:
        proj, (wos_slabs,) = _mm(x1b, ws, name="mm_in_ssm", out_dtype=F32, trans_b=True, ride=_Gather([wos]))
        wos = wos_slabs.reshape(d_inner, d)
    else:
        proj = _mm(x1b, ws, name="mm_in_ssm", out_dtype=F32, trans_b=True)
    xbc = _conv_fwd(proj, conv_w, conv_b, d_inner)
    dt = _dt_fwd(proj, dt_bias, d_inner + conv_dim)
    dt_g = dt.reshape(s, SSM_GROUPS, hpg).transpose(1, 0, 2)
    dt_gt = dt.reshape(s, SSM_GROUPS, hpg).transpose(1, 2, 0)
    a = -jnp.exp(a_log)
    a_g = a.reshape(SSM_GROUPS, 1, hpg)
    a_gt = a.reshape(SSM_GROUPS, hpg, 1)
    dskip_e = jnp.repeat(d_skip.reshape(SSM_GROUPS, 1, hpg), p, axis=2)
    yn, yscan, states, ynt = _ssd_fwd(xbc, proj, dt_g, dt_gt, a_g, a_gt, dskip_e, norm_w, d_inner, n_state)
    h2 = _mm(yn, wos, name="mm_out_ssm", out_dtype=F32)

    du2, du2b, dg1, db1, loss_t = _ln_bwd(x1, h2, ln_g[1:2], ln_b[1:2], target, with_loss=True, name="ln2_loss_bwd")
    loss = loss_t[0, 0]
    dyn = _mm(du2b, wos, name="mm_dyn", out_dtype=F32, trans_b=True)
    g_wos = _mm(ynt, du2b, name="mm_dw_out_ssm", out_dtype=BF16)
    parts = {}
    dyscan, dproj_ssm, g_norm, g_dskip, dproj_t = _ssd_epilogue_bwd(dyn, yscan, xbc, proj, dskip_e, norm_w, hpg)
    ride = _Scatter([g_wos.reshape(NDEV, d_inner // NDEV, d)]) if dist else None
    (dxs, d_bm, d_cm, ddt_g, g_a), rode = _ssd_scan_bwd(xbc, dt_g, dt_gt, a_g, a_gt, states, dyscan, dskip_e,
                                                         d_inner, n_state, ride=ride)
    parts["w_out_ssm"] = [list(rode)]
    g_alog = g_a.reshape(1, ssm_heads) * a
    dproj_ssm, g_conv_w, g_conv_b, dproj_t = _conv_bwd(proj, conv_w, conv_b, (dxs, d_bm, d_cm), d_inner, dproj_ssm,
                                                       dproj_t)
    ddt = ddt_g.transpose(1, 0, 2).reshape(s, ssm_heads)
    dproj_ssm, g_dtb, dproj_t = _dt_bwd(proj, dt_bias, ddt, d_inner + conv_dim, dproj_ssm, dproj_t)
    g_ws = _mm(dproj_t, x1b, name="mm_dw_in_ssm", out_dtype=BF16)
    if dist:
        g_ws_slabs = g_ws.reshape(NDEV, per_dev, d)
        dx1, near = _mm(dproj_ssm, ws, name="mm_dx1", out_dtype=F32, res=du2, res_scale=DEEPNORM_ALPHA,
                        ride=_Scatter([g_ws_slabs], KS_FLAT))
    else:
        dx1 = _mm(dproj_ssm, ws, name="mm_dx1", out_dtype=F32, res=du2, res_scale=DEEPNORM_ALPHA)

    du1, du1b, dg0, db0 = _ln_bwd(x, h1, ln_g[0:1], ln_b[0:1], dx1, with_loss=False, name="ln1_bwd")
    dy = _mm(du1b, wo, name="mm_dy", out_dtype=F32, trans_b=True)
    g_wo = _mm(yt, du1b, name="mm_dw_out_attn", out_dtype=BF16, slab_out=NDEV)
    do, dgate, dd = _attn_bwd_prep(dy, o, gate)
    rides = [_Scatter([g_ws_slabs], KS_DIAG[0:1]), _Scatter([g_wo]), None] if dist else [None] * N_GROUPS_ATTN
    dparts, dss, rode_attn = [], [], []
    for g, (_, dil) in enumerate(ATTN_PATTERNS):
        (dq, dk, dv, ds), rode = _attn_bwd_group(qkvs[g], do, lse, dd, bias[g], g, dil, da, ride=rides[g])
        dparts += [dq, dk, dv]
        dss.append(ds)
        rode_attn += list(rode)
    g_bias = _bias_bwd(jnp.stack(dss), bucket)
    g_rel_bias = g_bias.transpose(2, 0, 1).reshape(NUM_BUCKETS, N_GROUPS_ATTN * heads)
    dproj_attn = jnp.concatenate(dparts + [dgate], axis=1)
    pending = None
    if dist:
        parts["w_out_attn"] = [rode_attn[1:]]
        half = d // 2
        g_top, (diag_b,) = _mm(xbt[:half], dproj_attn, name="mm_dw_in_attn_top", out_dtype=BF16, slab_out=NDEV,
                               ride=_Scatter([g_ws_slabs], KS_DIAG[1:2]))
        parts["w_in_ssm"] = [[near[0], rode_attn[0], diag_b]]
        g_bot, (top_a,) = _mm(xbt[half:], dproj_attn, name="mm_dw_in_attn_bottom", out_dtype=BF16, slab_out=NDEV,
                              ride=_Scatter([g_top], KS_FLAT))
        dx, (top_b, bot_a) = _mm(dproj_attn, wa, name="mm_dx", out_dtype=F32, trans_b=True, res=du1,
                                 res_scale=DEEPNORM_ALPHA, ride=_Scatter([g_top, g_bot], [KS_DIAG, KS_FLAT]))
        parts["w_in_attn"] = [[top_a, top_b], [bot_a]]
        pending = [_Scatter([g_bot], KS_DIAG[0:1]), _Scatter([g_bot], KS_DIAG[1:2])]
    else:
        g_wa = _mm(xbt, dproj_attn, name="mm_dw_in_attn", out_dtype=BF16, slab_out=NDEV)
        dx = _mm(dproj_attn, wa, name="mm_dx", out_dtype=F32, trans_b=True, res=du1, res_scale=DEEPNORM_ALPHA)

    g_ln_g = jnp.concatenate([dg0, dg1], axis=0)
    g_ln_b = jnp.concatenate([db0, db1], axis=0)
    small = dict(rel_bias=g_rel_bias, dt_bias=g_dtb, a_log=g_alog, d_skip=g_dskip.reshape(1, ssm_heads),
                 ln_g=g_ln_g, ln_b=g_ln_b, conv_w=g_conv_w, conv_b=g_conv_b, ssm_norm_w=g_norm)
    if dist:
        return loss, dx, parts, pending, small
    return loss, dx, g_wa, g_wo, g_ws, g_wos, small


REPLICATED = ("rel_bias", "dt_bias", "a_log", "d_skip", "ln_g", "ln_b")
SHARDED_SMALL = ("conv_w", "conv_b", "ssm_norm_w")


def kernel(x, w_in_attn, w_out_attn, rel_bias, w_in_ssm, conv_w, conv_b, dt_bias, a_log, d_skip, ssm_norm_w, w_out_ssm, ln_g, ln_b, loss_target, m_w_in_attn, m_w_out_attn, m_rel_bias, m_w_in_ssm, m_conv_w, m_conv_b, m_dt_bias, m_a_log, m_d_skip, m_ssm_norm_w, m_w_out_ssm, m_ln_g, m_ln_b, v_w_in_attn, v_w_out_attn, v_rel_bias, v_w_in_ssm, v_conv_w, v_conv_b, v_dt_bias, v_a_log, v_d_skip, v_ssm_norm_w, v_w_out_ssm, v_ln_g, v_ln_b):
    w = dict(w_in_attn=w_in_attn, w_out_attn=w_out_attn, rel_bias=rel_bias, w_in_ssm=w_in_ssm, conv_w=conv_w,
             conv_b=conv_b, dt_bias=dt_bias, a_log=a_log, d_skip=d_skip, ssm_norm_w=ssm_norm_w,
             w_out_ssm=w_out_ssm, ln_g=ln_g, ln_b=ln_b)
    m = dict(w_in_attn=m_w_in_attn, w_out_attn=m_w_out_attn, rel_bias=m_rel_bias, w_in_ssm=m_w_in_ssm,
             conv_w=m_conv_w, conv_b=m_conv_b, dt_bias=m_dt_bias, a_log=m_a_log, d_skip=m_d_skip,
             ssm_norm_w=m_ssm_norm_w, w_out_ssm=m_w_out_ssm, ln_g=m_ln_g, ln_b=m_ln_b)
    v = dict(w_in_attn=v_w_in_attn, w_out_attn=v_w_out_attn, rel_bias=v_rel_bias, w_in_ssm=v_w_in_ssm,
             conv_w=v_conv_w, conv_b=v_conv_b, dt_bias=v_dt_bias, a_log=v_a_log, d_skip=v_d_skip,
             ssm_norm_w=v_ssm_norm_w, w_out_ssm=v_w_out_ssm, ln_g=v_ln_g, ln_b=v_ln_b)
    me = _lin((lax.axis_index("x"), lax.axis_index("y"), lax.axis_index("c")))
    d = x.shape[2]
    big = ("w_in_attn", "w_out_attn", "w_in_ssm", "w_out_ssm")

    for t in (w, m, v):
        t["w_in_ssm"] = t["w_in_ssm"].transpose(0, 2, 1)
    shards = {k: _cast_bf16(w[k], name=f"cast_{k}") for k in big}
    (wa,) = _exchange_alone(_Gather([shards["w_in_attn"]]), name="gather_w_in_attn")
    cpd = conv_w.shape[2]
    npd = ssm_norm_w.shape[1]
    small_shapes = [(CONV_WIDTH, cpd), (1, cpd), (1, npd)]
    (small_all,) = _exchange_alone(_Gather([_pack([conv_w[0], conv_b, ssm_norm_w])]), name="gather_small_weights",
                                   in_vmem=True)
    small_parts = [_unpack(small_all[i], small_shapes) for i in range(NDEV)]
    conv_w_full = jnp.concatenate([p[0] for p in small_parts], axis=1)
    conv_b_full = jnp.concatenate([p[1] for p in small_parts], axis=1)
    norm_w_full = jnp.concatenate([p[2] for p in small_parts], axis=1)

    loss, dx, parts, pending, small = _local_step(
        x[0], loss_target[0], wa, shards["w_out_attn"], shards["w_in_ssm"], shards["w_out_ssm"], rel_bias,
        conv_w_full, conv_b_full, dt_bias[0:1], a_log[0:1], d_skip[0:1], norm_w_full, ln_g, ln_b, dist=True)
    loss = lax.psum(loss, MESH_AXES)
    out = {}
    for k, ride in zip(("w_in_ssm", "w_out_ssm"), pending):
        out[k], late = _adamw_sum(parts[k], w[k], m[k], v[k], name=f"adamw_{k}", ride=ride)
        parts["w_in_attn"][1] += list(late)
    out["w_in_ssm"] = tuple(t.transpose(0, 2, 1) for t in out["w_in_ssm"])
    for k in ("w_out_attn", "w_in_attn"):
        out[k] = _adamw_sum(parts[k], w[k], m[k], v[k], name=f"adamw_{k}")

    order = REPLICATED + SHARDED_SMALL
    g_shapes = [small[k].shape for k in order]
    (g_all,) = _exchange_alone(_Gather([_pack([small[k] for k in order])]), name="gather_small_grads", in_vmem=True)
    g_sum = dict(zip(order, _unpack(_sum_slots(g_all, name="sum_small_grads"), g_shapes)))
    g_mine = {k: g_sum[k] for k in REPLICATED}
    g_mine["conv_w"] = lax.dynamic_slice_in_dim(g_sum["conv_w"], me * cpd, cpd, axis=1)
    g_mine["conv_b"] = lax.dynamic_slice_in_dim(g_sum["conv_b"], me * cpd, cpd, axis=1)
    g_mine["ssm_norm_w"] = lax.dynamic_slice_in_dim(g_sum["ssm_norm_w"], me * npd, npd, axis=1)
    w_shapes = [w[k].shape for k in order]
    g_pack = _pack([g_mine[k] for k in order])
    d_p, m_p, v_p = _adamw_small(g_pack, _pack([w[k] for k in order]), _pack([m[k] for k in order]),
                                 _pack([v[k] for k in order]), name="adamw_small")
    for k, gk, dk, mk, vk in zip(order, _unpack(g_pack, w_shapes), _unpack(d_p, w_shapes), _unpack(m_p, w_shapes),
                                 _unpack(v_p, w_shapes)):
        out[k] = (gk, dk, mk, vk)

    names = ("w_in_attn", "w_out_attn", "rel_bias", "w_in_ssm", "conv_w", "conv_b", "dt_bias", "a_log", "d_skip",
             "ssm_norm_w", "w_out_ssm", "ln_g", "ln_b")
    res = [loss, dx[None]]
    for i in range(4):
        res += [out[k][i] for k in names]
    return tuple(res)
```

```python
import functools
import math

import jax
import jax.numpy as jnp
from jax import lax
from jax.experimental import pallas as pl
from jax.experimental.pallas import tpu as pltpu

F32 = jnp.float32
BF16 = jnp.bfloat16
MESH_AXES = ("x", "y", "c")
NDEV = 8
NPEER = NDEV - 1
LANES = 128
SUBLANES = 8
VMEM_LIMIT = 52 * 1024 * 1024
MM_VMEM_BUDGET = 40 * 1024 * 1024
MM_TK_MAX = 4096
MM_TN_MAX = 1024

ATTN_PATTERNS = ((128, 1), (512, 4), (2048, 16))
N_GROUPS_ATTN = 3
HEAD_DIM = 128
ATTN_BLOCK = 128
ATTN_ROWS_TIMES_HEADS = 8192
NUM_BUCKETS = 32
MAX_DISTANCE = 2048
SSM_GROUPS = 8
CONV_WIDTH = 4
CHUNK = 128
DEPTH = 2
DEEPNORM_ALPHA = (2 * DEPTH) ** 0.25
LN_EPS = 1e-5
RMS_EPS = 1e-5
NEG_INF = -1e30
ADAM_LR = 0.001
ADAM_B1 = 0.9
ADAM_B2 = 0.999
ADAM_EPS = 1e-08
ADAM_WD = 0.01
ADAM_STEP = 10
HIGHEST = lax.Precision.HIGHEST


def _params(*sem):
    return pltpu.CompilerParams(dimension_semantics=sem, vmem_limit_bytes=VMEM_LIMIT)


def _pick(n, prefs):
    for p in prefs:
        if n % p == 0:
            return p
    return n


def _row_tile(r, limit):
    return max(t for t in range(2 * SUBLANES, limit + 1, 2 * SUBLANES) if r % t == 0)


def _dot(a, b):
    return jnp.dot(a, b, preferred_element_type=F32)


def _dot_nt(a, b):
    return lax.dot_general(a, b, (((1,), (1,)), ((), ())), preferred_element_type=F32)


def _dot_tn(a, b):
    return lax.dot_general(a, b, (((0,), (0,)), ((), ())), preferred_element_type=F32)


def _sigmoid(x):
    return 1.0 / (1.0 + jnp.exp(-x))


def _mm(a, b, *, name, out_dtype, trans_b=False, slab_out=0, n_off=0, n_out=None,
        res=None, res_scale=1.0, ride=None):
    m, k = a.shape
    slab_b = b.ndim == 3
    if slab_b:
        ns = b.shape[0]
        if trans_b:
            n, kper = b.shape[1], b.shape[2]
            assert ns * kper == k
        else:
            nper = b.shape[2]
            n = ns * nper
            assert b.shape[1] == k
    else:
        n = b.shape[0] if trans_b else b.shape[1]
        assert (b.shape[1] if trans_b else b.shape[0]) == k
    n_out = n if n_out is None else n_out
    tm = _pick(m, (1024, 640, 512, 256, 128))
    nconstraint = math.gcd(n_out, n_off) if n_off else n_out
    if slab_b and not trans_b:
        nconstraint = math.gcd(nconstraint, nper)
    if slab_out:
        nconstraint = math.gcd(nconstraint, n_out // slab_out)
    kconstraint = kper if (slab_b and trans_b) else k
    tk = max(t for t in range(LANES, min(kconstraint, MM_TK_MAX) + 1, LANES) if kconstraint % t == 0)
    nk = k // tk
    out_bytes = jnp.dtype(out_dtype).itemsize

    def vmem_bytes(t):
        return (2 * 2 * tk * (tm + t) + 2 * tm * t * out_bytes + (4 * tm * t if nk > 1 else 0)
                + (2 * 4 * tm * t if res is not None else 0))

    fits = [t for t in range(LANES, min(nconstraint, MM_TN_MAX) + 1, LANES)
            if nconstraint % t == 0 and vmem_bytes(t) <= MM_VMEM_BUDGET]
    tn = max(fits)
    nb0 = n_off // tn
    grid = (m // tm, n_out // tn, nk)

    a_spec = pl.BlockSpec((tm, tk), lambda i, j, kk: (i, kk))
    if slab_b and not trans_b:
        nps = nper // tn
        b_spec = pl.BlockSpec((None, tk, tn), lambda i, j, kk: ((j + nb0) // nps, kk, (j + nb0) % nps))
    elif slab_b and trans_b:
        kps = kper // tk
        b_spec = pl.BlockSpec((None, tn, tk), lambda i, j, kk: (kk // kps, j + nb0, kk % kps))
    elif trans_b:
        b_spec = pl.BlockSpec((tn, tk), lambda i, j, kk: (j + nb0, kk))
    else:
        b_spec = pl.BlockSpec((tk, tn), lambda i, j, kk: (kk, j + nb0))
    if slab_out:
        ops = (n_out // slab_out) // tn
        o_spec = pl.BlockSpec((None, tm, tn), lambda i, j, kk: (j // ops, i, j % ops))
        o_shape = jax.ShapeDtypeStruct((slab_out, m, n_out // slab_out), out_dtype)
    else:
        o_spec = pl.BlockSpec((tm, tn), lambda i, j, kk: (i, j))
        o_shape = jax.ShapeDtypeStruct((m, n_out), out_dtype)
    in_specs = [a_spec, b_spec]
    args = [a, b]
    if res is not None:
        in_specs.append(pl.BlockSpec((tm, tn), lambda i, j, kk: (i, j)))
        args.append(res)

    def body(*refs):
        a_ref, b_ref = refs[0], refs[1]
        r_ref = refs[2] if res is not None else None
        o_ref = refs[3] if res is not None else refs[2]
        av = a_ref[...].astype(BF16)
        bv = b_ref[...].astype(BF16)
        part = _dot_nt(av, bv) if trans_b else _dot(av, bv)

        def finish(r):
            if res is not None:
                r = r + res_scale * r_ref[...]
            o_ref[...] = r.astype(out_dtype)

        if nk == 1:
            finish(part)
            return
        acc = refs[-1]
        kk = pl.program_id(2)

        @pl.when(kk == 0)
        def _():
            acc[...] = part

        @pl.when(kk > 0)
        def _():
            acc[...] += part

        @pl.when(kk == nk - 1)
        def _():
            finish(acc[...])

    outs, rode = _call(
        body, name=name, grid=grid, in_specs=in_specs, out_specs=[o_spec], out_shape=[o_shape], args=args,
        scratch=[pltpu.VMEM((tm, tn), F32)] if nk > 1 else [], ride=ride,
        sem=("parallel", "parallel", "arbitrary"))
    return (outs[0], rode) if ride is not None else outs[0]


def _cast_bf16(w, *, name, with_transpose=False):
    r, c = w.shape[-2:]
    tr = _row_tile(r, 512)

    def body(w_ref, o_ref, *t_ref):
        v = w_ref[...]
        o_ref[...] = v.astype(BF16)
        if with_transpose:
            t_ref[0][...] = v.T.astype(BF16)

    in_spec = (pl.BlockSpec((None, tr, c), lambda i: (0, i, 0)) if w.ndim == 3
               else pl.BlockSpec((tr, c), lambda i: (i, 0)))
    out_specs = [pl.BlockSpec((tr, c), lambda i: (i, 0))]
    out_shape = [jax.ShapeDtypeStruct((r, c), BF16)]
    if with_transpose:
        out_specs.append(pl.BlockSpec((c, tr), lambda i: (0, i)))
        out_shape.append(jax.ShapeDtypeStruct((c, r), BF16))
    out = pl.pallas_call(
        body, name=name, grid=(r // tr,), in_specs=[in_spec], out_specs=tuple(out_specs),
        out_shape=tuple(out_shape), compiler_params=_params("parallel"),
    )(w)
    return out if with_transpose else out[0]


def _adam_math(w, g, m, v):
    m2 = ADAM_B1 * m + (1.0 - ADAM_B1) * g
    v2 = ADAM_B2 * v + (1.0 - ADAM_B2) * (g * g)
    m_hat = m2 / (1.0 - ADAM_B1 ** ADAM_STEP)
    v_hat = v2 / (1.0 - ADAM_B2 ** ADAM_STEP)
    delta = -ADAM_LR * (m_hat / (jnp.sqrt(v_hat) + ADAM_EPS) + ADAM_WD * w)
    return delta, m2, v2


def _adamw_sum(bands, w, m, v, *, name, ride=None):
    _, r, c = w.shape
    nband = len(bands)
    rows = r // nband
    tr = _row_tile(rows, 128)
    tc = c if (c % LANES or c <= 2560) else _pick(c, (2048, 1024, 512, 256, 128))
    nt = rows // tr
    flat = [p for band in bands for p in band]

    def body(*refs):
        p_refs = refs[:len(flat)]
        w_ref, m_ref, v_ref, g_out, d_out, m_out, v_out = refs[len(flat):]
        i = pl.program_id(0)
        g, at = None, 0
        for q, band in enumerate(bands):
            gq = None
            for p_ref in p_refs[at:at + len(band)]:
                for s in range(p_ref.shape[0]):
                    t = p_ref[s].astype(F32)
                    gq = t if gq is None else gq + t
            at += len(band)
            g = gq if q == 0 else jnp.where(i >= q * nt, gq, g)
        d, m2, v2 = _adam_math(w_ref[...], g, m_ref[...], v_ref[...])
        g_out[...] = g
        d_out[...] = d
        m_out[...] = m2
        v_out[...] = v2

    def band_spec(p, q):
        return pl.BlockSpec((p.shape[0], tr, tc), lambda i, j: (0, jnp.clip(i - q * nt, 0, nt - 1), j))

    spec = pl.BlockSpec((None, tr, tc), lambda i, j: (0, i, j))
    shp = jax.ShapeDtypeStruct((1, r, c), F32)
    outs, rode = _call(
        body, name=name, grid=(r // tr, c // tc),
        in_specs=[band_spec(p, q) for q, band in enumerate(bands) for p in band] + [spec, spec, spec],
        out_specs=[spec, spec, spec, spec], out_shape=[shp, shp, shp, shp], args=(*flat, w, m, v),
        sem=("parallel", "parallel"), ride=ride)
    return (outs, rode) if ride is not None else outs


def _adamw_small(g, w, m, v, *, name):
    shp = jax.ShapeDtypeStruct(w.shape, F32)

    def body(g_ref, w_ref, m_ref, v_ref, d_out, m_out, v_out):
        d, m2, v2 = _adam_math(w_ref[...], g_ref[...], m_ref[...], v_ref[...])
        d_out[...] = d
        m_out[...] = m2
        v_out[...] = v2

    return pl.pallas_call(body, name=name, out_shape=(shp, shp, shp),
                          compiler_params=pltpu.CompilerParams(vmem_limit_bytes=VMEM_LIMIT))(g, w, m, v)


def _sum_slots(parts, *, name):
    _, r, c = parts.shape

    def body(p_ref, o_ref):
        g = p_ref[0]
        for s in range(1, NDEV):
            g = g + p_ref[s]
        o_ref[...] = g

    return pl.pallas_call(body, name=name, out_shape=jax.ShapeDtypeStruct((r, c), F32),
                          compiler_params=pltpu.CompilerParams(vmem_limit_bytes=VMEM_LIMIT))(parts)


def _ln_parts(u):
    mu = jnp.mean(u, axis=-1, keepdims=True)
    xc = u - mu
    var = jnp.mean(xc * xc, axis=-1, keepdims=True)
    rstd = lax.rsqrt(var + LN_EPS)
    return xc * rstd, rstd


def _ln_fwd(xin, h, g, b, *, name):
    s, d = xin.shape
    tm = _pick(s, (256, 128))

    def body(x_ref, h_ref, g_ref, b_ref, o_ref, ob_ref):
        xhat, _ = _ln_parts(DEEPNORM_ALPHA * x_ref[...] + h_ref[...])
        o = xhat * g_ref[...] + b_ref[...]
        o_ref[...] = o
        ob_ref[...] = o.astype(BF16)

    row = pl.BlockSpec((tm, d), lambda i: (i, 0))
    vec = pl.BlockSpec((1, d), lambda i: (0, 0))
    return pl.pallas_call(
        body, name=name, grid=(s // tm,), in_specs=[row, row, vec, vec], out_specs=(row, row),
        out_shape=(jax.ShapeDtypeStruct((s, d), F32), jax.ShapeDtypeStruct((s, d), BF16)),
        compiler_params=_params("parallel"),
    )(xin, h, g, b)


def _ln_bwd(xin, h, g, b, cot, *, with_loss, name):
    s, d = xin.shape
    tm = _pick(s, (128,))

    def body(x_ref, h_ref, g_ref, b_ref, c_ref, du_ref, dub_ref, dg_ref, db_ref, *rest):
        i = pl.program_id(0)
        xhat, rstd = _ln_parts(DEEPNORM_ALPHA * x_ref[...] + h_ref[...])
        gv = g_ref[...]
        if with_loss:
            diff = xhat * gv + b_ref[...] - c_ref[...]
            part = 0.5 * jnp.sum(jnp.mean(diff * diff, axis=-1, keepdims=True), axis=0, keepdims=True)
            dout = diff / d
        else:
            dout = c_ref[...]

        @pl.when(i == 0)
        def _():
            dg_ref[...] = jnp.zeros_like(dg_ref)
            db_ref[...] = jnp.zeros_like(db_ref)
            if with_loss:
                rest[0][...] = jnp.zeros_like(rest[0])

        dg_ref[...] += jnp.sum(dout * xhat, axis=0, keepdims=True)
        db_ref[...] += jnp.sum(dout, axis=0, keepdims=True)
        if with_loss:
            rest[0][...] += jnp.broadcast_to(part, rest[0].shape)
        dxh = dout * gv
        du = rstd * (dxh - jnp.mean(dxh, axis=-1, keepdims=True)
                     - xhat * jnp.mean(dxh * xhat, axis=-1, keepdims=True))
        du_ref[...] = du
        dub_ref[...] = du.astype(BF16)

    row = pl.BlockSpec((tm, d), lambda i: (i, 0))
    vec = pl.BlockSpec((1, d), lambda i: (0, 0))
    out_specs = [row, row, vec, vec]
    out_shape = [jax.ShapeDtypeStruct((s, d), F32), jax.ShapeDtypeStruct((s, d), BF16),
                 jax.ShapeDtypeStruct((1, d), F32), jax.ShapeDtypeStruct((1, d), F32)]
    if with_loss:
        out_specs.append(pl.BlockSpec((SUBLANES, LANES), lambda i: (0, 0)))
        out_shape.append(jax.ShapeDtypeStruct((SUBLANES, LANES), F32))
    return pl.pallas_call(
        body, name=name, grid=(s // tm,), in_specs=[row, row, vec, vec, row],
        out_specs=tuple(out_specs), out_shape=tuple(out_shape),
        compiler_params=_params("arbitrary"),
    )(xin, h, g, b, cot)


def t5_causal_bucket(dist):
    max_exact = NUM_BUCKETS // 2
    d_f = jnp.maximum(dist, 1).astype(jnp.float32)
    large = max_exact + (jnp.log(d_f / max_exact) / math.log(MAX_DISTANCE / max_exact)
                         * (NUM_BUCKETS - max_exact)).astype(jnp.int32)
    large = jnp.minimum(large, NUM_BUCKETS - 1)
    return jnp.where(dist < max_exact, dist, large)


def _bias_tables(rel_bias, heads):
    qi = lax.broadcasted_iota(jnp.int32, (ATTN_BLOCK, 2 * ATTN_BLOCK), 0)
    ki = lax.broadcasted_iota(jnp.int32, (ATTN_BLOCK, 2 * ATTN_BLOCK), 1)
    delta = ATTN_BLOCK + qi - ki
    buckets = []
    for window, dilation in ATTN_PATTERNS:
        span = window // dilation
        assert span == ATTN_BLOCK
        band = (delta >= 0) & (delta <= span)
        buckets.append(jnp.where(band, t5_causal_bucket(jnp.clip(delta, 0, None) * dilation), -1))
    bucket = jnp.stack(buckets).astype(jnp.int32)

    def body(bk_ref, tbl_ref, o_ref):
        col = pl.program_id(0) * heads + pl.program_id(1)
        bk = bk_ref[...]
        acc = jnp.full(bk.shape, NEG_INF, F32)
        for b in range(NUM_BUCKETS):
            acc = jnp.where(bk == b, tbl_ref[b, col], acc)
        o_ref[...] = acc

    tile = (None, ATTN_BLOCK, 2 * ATTN_BLOCK)
    bias = pl.pallas_call(
        body, name="bias_fwd", grid=(N_GROUPS_ATTN, heads),
        in_specs=[pl.BlockSpec(tile, lambda g, h: (g, 0, 0)), pl.BlockSpec(memory_space=pltpu.SMEM)],
        out_specs=pl.BlockSpec((None,) + tile, lambda g, h: (g, h, 0, 0)),
        out_shape=jax.ShapeDtypeStruct((N_GROUPS_ATTN, heads, ATTN_BLOCK, 2 * ATTN_BLOCK), F32),
        compiler_params=_params("parallel", "parallel"),
    )(bucket, rel_bias)
    return bias, bucket


def _dilated_view(qkv, g, dilation, da):
    del g
    return qkv.reshape(qkv.shape[0] // dilation, dilation * 3 * da), 3 * (da // HEAD_DIM), 0


def _heads_per_step(l, heads):
    for hps in (4, 2, 1):
        if heads % hps == 0 and l * hps <= ATTN_ROWS_TIMES_HEADS:
            return hps
    return 1


def _attn_fwd_group(qkv, bias_g, g, dilation, da, ride=None):
    s = qkv.shape[0]
    heads = da // HEAD_DIM
    l = s // dilation
    nb = l // ATTN_BLOCK
    view, cpb, base = _dilated_view(qkv, g, dilation, da)

    hps = _heads_per_step(l, heads)
    lanes = [slice(i * HEAD_DIM, (i + 1) * HEAD_DIM) for i in range(hps)]

    def body(q_ref, k_ref, v_ref, b_ref, o_ref, l_ref):
        scale = HEAD_DIM ** -0.5

        def block(rows, keys, first):
            q, k, v = q_ref[rows, :], k_ref[keys, :], v_ref[keys, :]
            bias = [b_ref[i, :, ATTN_BLOCK:2 * ATTN_BLOCK] if first else b_ref[i] for i in range(hps)]
            sc = [_dot_nt(q[:, hl], k[:, hl]) * scale + bias[i] for i, hl in enumerate(lanes)]
            mx = [jnp.max(t, axis=-1, keepdims=True) for t in sc]
            p = [jnp.exp(t - m) for t, m in zip(sc, mx)]
            den = [jnp.sum(t, axis=-1, keepdims=True) for t in p]
            for i, hl in enumerate(lanes):
                o_ref[rows, hl] = _dot((p[i] * (1.0 / den[i])).astype(BF16), v[:, hl])
                l_ref[rows, hl] = jnp.broadcast_to(mx[i] + jnp.log(den[i]), (ATTN_BLOCK, HEAD_DIM))

        first = pl.ds(0, ATTN_BLOCK)
        block(first, first, True)

        def step(j, carry):
            r0 = pl.multiple_of(j * ATTN_BLOCK, ATTN_BLOCK)
            rk = pl.multiple_of((j - 1) * ATTN_BLOCK, ATTN_BLOCK)
            block(pl.ds(r0, ATTN_BLOCK), pl.ds(rk, 2 * ATTN_BLOCK), False)
            return carry

        if nb > 1:
            lax.fori_loop(1, nb, step, 0)

    def col(t):
        return lambda r, h: (0, (r * cpb + base + t * heads) // hps + h)

    blk = (l, hps * HEAD_DIM)
    out = pl.BlockSpec(blk, lambda r, h: (0, r * (heads // hps) + h))
    shp = jax.ShapeDtypeStruct((l, dilation * da), F32)
    (o, lse), rode = _call(
        body, name=f"attn_fwd_g{g}", grid=(dilation, heads // hps),
        in_specs=[pl.BlockSpec(blk, col(0)), pl.BlockSpec(blk, col(1)), pl.BlockSpec(blk, col(2)),
                  pl.BlockSpec((hps, ATTN_BLOCK, 2 * ATTN_BLOCK), lambda r, h: (h, 0, 0))],
        out_specs=[out, out], out_shape=[shp, shp], args=(view, view, view, bias_g),
        sem=("parallel", "parallel"), ride=ride)
    return o.reshape(s, da), lse.reshape(s, da), rode


def _attn_combine(os_, ls_, gate):
    s, da = gate.shape
    tm = _pick(s, (512, 256, 128))
    tc = _pick(da, (512, 256, 128))

    assert da // HEAD_DIM <= LANES
    per_step = tc // HEAD_DIM

    def body(o0, o1, o2, l0, l1, l2, g_ref, o_ref, l_ref, y_ref, yt_ref):
        j = pl.program_id(1)
        a0, a1, a2 = l0[...], l1[...], l2[...]
        mx = jnp.maximum(jnp.maximum(a0, a1), a2)
        e0, e1, e2 = jnp.exp(a0 - mx), jnp.exp(a1 - mx), jnp.exp(a2 - mx)
        den = e0 + e1 + e2
        o = (e0 * o0[...] + e1 * o1[...] + e2 * o2[...]) / den
        gv = g_ref[...]
        o_ref[...] = o
        y = o * (gv * _sigmoid(gv))
        y_ref[...] = y.astype(BF16)
        yt_ref[...] = y.T.astype(BF16)
        lse = mx + jnp.log(den)

        @pl.when(j == 0)
        def _():
            l_ref[...] = jnp.zeros_like(l_ref)

        lane = lax.broadcasted_iota(jnp.int32, (1, LANES), 1)
        acc = l_ref[...]
        for i in range(per_step):
            acc = jnp.where(lane == j * per_step + i, lse[:, i * HEAD_DIM:(i + 1) * HEAD_DIM], acc)
        l_ref[...] = acc

    spec = pl.BlockSpec((tm, tc), lambda i, j: (i, j))
    heads_spec = pl.BlockSpec((tm, LANES), lambda i, j: (i, 0))
    return pl.pallas_call(
        body, name="attn_combine", grid=(s // tm, da // tc), in_specs=[spec] * 7,
        out_specs=(spec, heads_spec, spec, pl.BlockSpec((tc, tm), lambda i, j: (j, i))),
        out_shape=(jax.ShapeDtypeStruct((s, da), F32), jax.ShapeDtypeStruct((s, LANES), F32),
                   jax.ShapeDtypeStruct((s, da), BF16), jax.ShapeDtypeStruct((da, s), BF16)),
        compiler_params=_params("parallel", "arbitrary"),
    )(*os_, *ls_, gate)


def _attn_bwd_prep(dy, o, gate):
    s, da = gate.shape
    tm = _pick(s, (512, 256, 128))

    def body(dy_ref, o_ref, g_ref, do_ref, dg_ref, dd_ref):
        j = pl.program_id(1)
        gv = g_ref[...]
        sg = _sigmoid(gv)
        dyv = dy_ref[...]
        ov = o_ref[...]
        do = dyv * (gv * sg)
        do_ref[...] = do.astype(BF16)
        dg_ref[...] = (dyv * ov * (sg * (1.0 + gv * (1.0 - sg)))).astype(BF16)

        @pl.when(j == 0)
        def _():
            dd_ref[...] = jnp.zeros_like(dd_ref)

        lane = lax.broadcasted_iota(jnp.int32, (1, LANES), 1)
        dd_ref[...] = jnp.where(lane == j, jnp.sum(do * ov, axis=-1, keepdims=True), dd_ref[...])

    spec = pl.BlockSpec((tm, HEAD_DIM), lambda i, j: (i, j))
    return pl.pallas_call(
        body, name="attn_bwd_prep", grid=(s // tm, da // HEAD_DIM), in_specs=[spec] * 3,
        out_specs=(spec, spec, pl.BlockSpec((tm, LANES), lambda i, j: (i, 0))),
        out_shape=(jax.ShapeDtypeStruct((s, da), BF16), jax.ShapeDtypeStruct((s, da), BF16),
                   jax.ShapeDtypeStruct((s, LANES), F32)),
        compiler_params=_params("parallel", "arbitrary"),
    )(dy, o, gate)


def _attn_bwd_group(qkv, do, lse, dd, bias_g, g, dilation, da, ride=None):
    s = qkv.shape[0]
    heads = da // HEAD_DIM
    l = s // dilation
    nb = l // ATTN_BLOCK
    view, cpb, base = _dilated_view(qkv, g, dilation, da)
    scale = HEAD_DIM ** -0.5
    hps = _heads_per_step(l, heads)
    lanes = [slice(i * HEAD_DIM, (i + 1) * HEAD_DIM) for i in range(hps)]

    def body(q_ref, k_ref, v_ref, do_ref, l_ref, dd_ref, b_ref, dq_ref, dk_ref, dv_ref, ds_ref, dk_acc, dv_acc):
        h0 = pl.program_id(0) * hps
        r = pl.program_id(1)
        lane = lax.broadcasted_iota(jnp.int32, (1, LANES), 1)

        @pl.when(r == 0)
        def _():
            ds_ref[...] = jnp.zeros_like(ds_ref)

        dk_acc[...] = jnp.zeros_like(dk_acc)
        dv_acc[...] = jnp.zeros_like(dv_acc)

        def block(rows, keys, first):
            q, k, v, dov = q_ref[rows, :], k_ref[keys, :], v_ref[keys, :], do_ref[rows, :]
            lse_all, dd_all = l_ref[rows, :], dd_ref[rows, :]
            pick = [(lane == h0 + i).astype(F32) for i in range(hps)]
            lrow = [jnp.sum(lse_all * m, axis=-1, keepdims=True) for m in pick]
            drow = [jnp.sum(dd_all * m, axis=-1, keepdims=True) for m in pick]
            bias = [b_ref[i, :, ATTN_BLOCK:2 * ATTN_BLOCK] if first else b_ref[i] for i in range(hps)]
            sc = [_dot_nt(q[:, hl], k[:, hl]) for hl in lanes]
            dp = [_dot_nt(dov[:, hl], v[:, hl]) for hl in lanes]
            p = [jnp.exp(sc[i] * scale + bias[i] - lrow[i]) for i in range(hps)]
            ds = [p[i] * (dp[i] - drow[i]) for i in range(hps)]
            dsb = [t.astype(BF16) for t in ds]
            pb = [t.astype(BF16) for t in p]
            for i, hl in enumerate(lanes):
                dq_ref[rows, hl] = (_dot(dsb[i], k[:, hl]) * scale).astype(BF16)
                dk_acc[keys, hl] += _dot_tn(dsb[i], q[:, hl]) * scale
                dv_acc[keys, hl] += _dot_tn(pb[i], dov[:, hl])
                if first:
                    ds_ref[i, :, ATTN_BLOCK:2 * ATTN_BLOCK] += ds[i]
                else:
                    ds_ref[i] += ds[i]

        first = pl.ds(0, ATTN_BLOCK)
        block(first, first, True)

        def step(j, carry):
            r0 = pl.multiple_of(j * ATTN_BLOCK, ATTN_BLOCK)
            rk = pl.multiple_of((j - 1) * ATTN_BLOCK, ATTN_BLOCK)
            block(pl.ds(r0, ATTN_BLOCK), pl.ds(rk, 2 * ATTN_BLOCK), False)
            return carry

        if nb > 1:
            lax.fori_loop(1, nb, step, 0)
        dk_ref[...] = dk_acc[...].astype(BF16)
        dv_ref[...] = dv_acc[...].astype(BF16)

    def col(t):
        return lambda h, r: (0, (r * cpb + base + t * heads) // hps + h)

    blk = (l, hps * HEAD_DIM)
    act = pl.BlockSpec(blk, lambda h, r: (0, r * (heads // hps) + h))
    per_head = pl.BlockSpec((l, LANES), lambda h, r: (0, r))
    tile = pl.BlockSpec((hps, ATTN_BLOCK, 2 * ATTN_BLOCK), lambda h, r: (h, 0, 0))
    shp = jax.ShapeDtypeStruct((l, dilation * da), BF16)
    (dq, dk, dv, ds), rode = _call(
        body, name=f"attn_bwd_g{g}", grid=(heads // hps, dilation),
        in_specs=[pl.BlockSpec(blk, col(0)), pl.BlockSpec(blk, col(1)), pl.BlockSpec(blk, col(2)), act,
                  per_head, per_head, tile],
        out_specs=[act, act, act, tile],
        out_shape=[shp, shp, shp, jax.ShapeDtypeStruct((heads, ATTN_BLOCK, 2 * ATTN_BLOCK), F32)],
        scratch=[pltpu.VMEM(blk, F32), pltpu.VMEM(blk, F32)], sem=("parallel", "arbitrary"), ride=ride,
        args=(view, view, view, do.reshape(l, dilation * da), lse.reshape(l, dilation * LANES),
              dd.reshape(l, dilation * LANES), bias_g))
    return (dq.reshape(s, da), dk.reshape(s, da), dv.reshape(s, da), ds), rode


def _bias_bwd(ds, bucket):
    ng, heads = ds.shape[0], ds.shape[1]

    def body(ds_ref, bk_ref, o_ref):
        bk = bk_ref[...]
        x = ds_ref[...]
        for b in range(NUM_BUCKETS):
            o_ref[:, b:b + 1] = jnp.sum(jnp.where(bk == b, x, 0.0), axis=(0, 1), keepdims=True)

    tile = (None, ATTN_BLOCK, 2 * ATTN_BLOCK)
    out = pl.pallas_call(
        body, name="bias_bwd", grid=(ng, heads),
        in_specs=[pl.BlockSpec((None,) + tile, lambda g, h: (g, h, 0, 0)), pl.BlockSpec(tile, lambda g, h: (g, 0, 0))],
        out_specs=pl.BlockSpec((None, None, 1, NUM_BUCKETS), lambda g, h: (g, h, 0, 0)),
        out_shape=jax.ShapeDtypeStruct((ng, heads, 1, NUM_BUCKETS), F32),
        compiler_params=_params("parallel", "parallel"),
    )(ds, bucket)
    return out.reshape(ng, heads, NUM_BUCKETS)


def _shift_rows(x, halo, s):
    r = pltpu.roll(x, s, axis=0)
    rh = pltpu.roll(halo, s, axis=0)
    row = lax.broadcasted_iota(jnp.int32, halo.shape, 0)
    top = jnp.where(row < s, rh, r[0:SUBLANES])
    return jnp.concatenate([top, r[SUBLANES:]], axis=0)


def _conv_out(x, halo, w, b):
    acc = b + w[CONV_WIDTH - 1:CONV_WIDTH] * x
    for kk in range(CONV_WIDTH - 1):
        acc = acc + w[kk:kk + 1] * _shift_rows(x, halo, CONV_WIDTH - 1 - kk)
    return acc


def _conv_fwd(proj, conv_w, conv_b, col0):
    s = proj.shape[0]
    c = conv_w.shape[1]
    ts = _pick(s, (1024, 512, 256, 128))
    tc = _pick(math.gcd(c, col0), (512, 256, 128))
    cb0 = col0 // tc
    hb = ts // SUBLANES

    def body(x_ref, h_ref, w_ref, b_ref, o_ref):
        i = pl.program_id(0)
        halo = jnp.where(i > 0, h_ref[...], 0.0)
        u = _conv_out(x_ref[...], halo, w_ref[...], b_ref[...])
        o_ref[...] = u * _sigmoid(u)

    return pl.pallas_call(
        body, name="conv_fwd", grid=(s // ts, c // tc),
        in_specs=[pl.BlockSpec((ts, tc), lambda i, j: (i, cb0 + j)),
                  pl.BlockSpec((SUBLANES, tc), lambda i, j: (jnp.maximum(i * hb - 1, 0), cb0 + j)),
                  pl.BlockSpec((CONV_WIDTH, tc), lambda i, j: (0, j)),
                  pl.BlockSpec((1, tc), lambda i, j: (0, j))],
        out_specs=pl.BlockSpec((ts, tc), lambda i, j: (i, j)),
        out_shape=jax.ShapeDtypeStruct((s, c), F32),
        compiler_params=_params("parallel", "parallel"),
    )(proj, proj, conv_w, conv_b)


def _conv_bwd(proj, conv_w, conv_b, dacts, col0, dproj, dproj_t):
    s = proj.shape[0]
    c = conv_w.shape[1]
    widths = [d.shape[1] for d in dacts]
    assert sum(widths) == c
    ts = _pick(s, (1024, 512, 256, 128))
    tc = _pick(math.gcd(math.gcd(c, col0), math.gcd(*widths)), (512, 256, 128))
    cb0 = col0 // tc
    hb = ts // SUBLANES
    nblk = s // ts
    ext = ts + SUBLANES
    nb = [wd // tc for wd in widths]
    starts = [0, nb[0], nb[0] + nb[1]]

    def body(x_ref, xp_ref, xn_ref, d0, d1, d2, n0, n1, n2, w_ref, b_ref, _, __, dx_ref, dw_ref, db_ref, dxt_ref):
        j = pl.program_id(0)
        i = pl.program_id(1)
        last = i == nblk - 1
        w = w_ref[...]
        halo = jnp.where(i > 0, xp_ref[...], 0.0)
        x = x_ref[...]
        xe = jnp.concatenate([x, xn_ref[...]], axis=0)
        dcur = jnp.where(j < starts[1], d0[...], jnp.where(j < starts[2], d1[...], d2[...]))
        dnext = jnp.where(j < starts[1], n0[...], jnp.where(j < starts[2], n1[...], n2[...]))
        de = jnp.concatenate([dcur, jnp.where(last, 0.0, dnext)], axis=0)
        u = _conv_out(xe, halo, w, b_ref[...])
        sg = _sigmoid(u)
        dpre = de * (sg * (1.0 + u * (1.0 - sg)))
        dx = w[CONV_WIDTH - 1:CONV_WIDTH] * dpre[0:ts]
        for kk in range(CONV_WIDTH - 1):
            sh = CONV_WIDTH - 1 - kk
            dx = dx + w[kk:kk + 1] * pltpu.roll(dpre, ext - sh, axis=0)[0:ts]
        dx_ref[...] = dx.astype(BF16)
        dxt_ref[...] = dx.T.astype(BF16)
        dcur = dpre[0:ts]

        @pl.when(i == 0)
        def _():
            dw_ref[...] = jnp.zeros_like(dw_ref)
            db_ref[...] = jnp.zeros_like(db_ref)

        db_ref[...] += jnp.sum(dcur, axis=0, keepdims=True)
        dw_ref[CONV_WIDTH - 1:CONV_WIDTH, :] += jnp.sum(dcur * x, axis=0, keepdims=True)
        for kk in range(CONV_WIDTH - 1):
            xs = _shift_rows(x, halo, CONV_WIDTH - 1 - kk)
            dw_ref[kk:kk + 1, :] += jnp.sum(dcur * xs, axis=0, keepdims=True)

    cur_p = pl.BlockSpec((ts, tc), lambda j, i: (i, cb0 + j))
    prev_p = pl.BlockSpec((SUBLANES, tc), lambda j, i: (jnp.maximum(i * hb - 1, 0), cb0 + j))
    nxt = lambda i: jnp.minimum((i + 1) * hb, nblk * hb - 1)
    next_p = pl.BlockSpec((SUBLANES, tc), lambda j, i: (nxt(i), cb0 + j))

    def part(q):
        return lambda j: jnp.clip(j - starts[q], 0, nb[q] - 1)

    cur_d = [pl.BlockSpec((ts, tc), lambda j, i, f=part(q): (i, f(j))) for q in range(3)]
    next_d = [pl.BlockSpec((SUBLANES, tc), lambda j, i, f=part(q): (nxt(i), f(j))) for q in range(3)]
    vec4 = pl.BlockSpec((CONV_WIDTH, tc), lambda j, i: (0, j))
    vec1 = pl.BlockSpec((1, tc), lambda j, i: (0, j))
    hbm = pl.BlockSpec(memory_space=pl.ANY)
    return pl.pallas_call(
        body, name="conv_bwd", grid=(c // tc, nblk),
        in_specs=[cur_p, prev_p, next_p, *cur_d, *next_d, vec4, vec1, hbm, hbm],
        out_specs=(cur_p, vec4, vec1, pl.BlockSpec((tc, ts), lambda j, i: (cb0 + j, i))),
        out_shape=(jax.ShapeDtypeStruct(dproj.shape, dproj.dtype), jax.ShapeDtypeStruct((CONV_WIDTH, c), F32),
                   jax.ShapeDtypeStruct((1, c), F32), jax.ShapeDtypeStruct(dproj_t.shape, dproj_t.dtype)),
        input_output_aliases={11: 0, 12: 3},
        compiler_params=_params("parallel", "arbitrary"),
    )(proj, proj, proj, *dacts, *dacts, conv_w, conv_b, dproj, dproj_t)


def _dt_fwd(proj, dt_bias, col0):
    s = proj.shape[0]
    h = dt_bias.shape[1]
    ts = _pick(s, (1024, 512, 256, 128))

    def body(x_ref, b_ref, o_ref):
        v = x_ref[...] + b_ref[...]
        o_ref[...] = jnp.maximum(v, 0.0) + jnp.log1p(jnp.exp(-jnp.abs(v)))

    return pl.pallas_call(
        body, name="dt_fwd", grid=(s // ts,),
        in_specs=[pl.BlockSpec((ts, h), lambda i: (i, col0 // h)), pl.BlockSpec((1, h), lambda i: (0, 0))],
        out_specs=pl.BlockSpec((ts, h), lambda i: (i, 0)), out_shape=jax.ShapeDtypeStruct((s, h), F32),
        compiler_params=_params("parallel"),
    )(proj, dt_bias)


def _dt_bwd(proj, dt_bias, ddt, col0, dproj, dproj_t):
    s = proj.shape[0]
    h = dt_bias.shape[1]
    ts = _pick(s, (1024, 512, 256, 128))

    def body(x_ref, b_ref, d_ref, _, __, o_ref, db_ref, ot_ref):
        i = pl.program_id(0)
        draw = d_ref[...] * _sigmoid(x_ref[...] + b_ref[...])
        o_ref[...] = draw.astype(BF16)
        ot_ref[...] = draw.T.astype(BF16)

        @pl.when(i == 0)
        def _():
            db_ref[...] = jnp.zeros_like(db_ref)

        db_ref[...] += jnp.sum(draw, axis=0, keepdims=True)

    hbm = pl.BlockSpec(memory_space=pl.ANY)
    return pl.pallas_call(
        body, name="dt_bwd", grid=(s // ts,),
        in_specs=[pl.BlockSpec((ts, h), lambda i: (i, col0 // h)), pl.BlockSpec((1, h), lambda i: (0, 0)),
                  pl.BlockSpec((ts, h), lambda i: (i, 0)), hbm, hbm],
        out_specs=(pl.BlockSpec((ts, h), lambda i: (i, col0 // h)), pl.BlockSpec((1, h), lambda i: (0, 0)),
                   pl.BlockSpec((h, ts), lambda i: (col0 // h, i))),
        out_shape=(jax.ShapeDtypeStruct(dproj.shape, dproj.dtype), jax.ShapeDtypeStruct((1, h), F32),
                   jax.ShapeDtypeStruct(dproj_t.shape, dproj_t.dtype)),
        input_output_aliases={3: 0, 4: 2},
        compiler_params=_params("arbitrary"),
    )(proj, dt_bias, ddt, dproj, dproj_t)


def _chunk_terms(dt, dt_t, a, a_t):
    li = lax.broadcasted_iota(jnp.int32, (CHUNK, CHUNK), 0)
    si = lax.broadcasted_iota(jnp.int32, (CHUNK, CHUNK), 1)
    lower = (li >= si).astype(F32)
    upper = (li <= si).astype(F32)
    acum = jnp.dot(lower, dt * a, preferred_element_type=F32, precision=HIGHEST)
    acum_t = jnp.dot(dt_t * a_t, upper, preferred_element_type=F32, precision=HIGHEST)
    return acum, acum_t, li, si, upper


def _dot_exact01(t, m01):
    r = t.shape[0]
    hi = t.astype(BF16)
    rest = t - hi.astype(F32)
    mid = rest.astype(BF16)
    lo = (rest - mid.astype(F32)).astype(BF16)
    out = _dot(jnp.concatenate([hi, mid, lo], axis=0), m01.astype(BF16))
    return out[0:r] + out[r:2 * r] + out[2 * r:3 * r]


def _head_lanes(dt, acum, gw):
    hpg = dt.shape[1]
    p = gw // hpg
    spread = (lax.broadcasted_iota(jnp.int32, (hpg, gw), 1) // p
              == lax.broadcasted_iota(jnp.int32, (hpg, gw), 0)).astype(F32)
    both = _dot_exact01(jnp.concatenate([dt, acum], axis=0), spread)
    dt_e, acum_e = both[0:CHUNK], both[CHUNK:2 * CHUNK]
    alast_e = acum_e[CHUNK - 1:CHUNK, :]
    return dt_e, jnp.exp(acum_e), jnp.exp(alast_e - acum_e), jnp.exp(alast_e)


def _fold_heads(t, hpg):
    gw = t.shape[1]
    p = gw // hpg
    fold = (lax.broadcasted_iota(jnp.int32, (gw, hpg), 0) // p
            == lax.broadcasted_iota(jnp.int32, (gw, hpg), 1)).astype(F32)
    return _dot_exact01(t, fold)


def _ssd_fwd(xbc, proj, dt_g, dt_gt, a_g, a_gt, dskip_e, norm_w, d_inner, n_state):
    s = xbc.shape[0]
    hpg = dt_g.shape[2]
    gw = d_inner // SSM_GROUPS
    p = gw // hpg
    nc = s // CHUNK
    n = n_state
    b0 = d_inner // n
    c0 = b0 + SSM_GROUPS
    per_tile = LANES // p

    def body(xs_ref, b_ref, c_ref, dt_ref, dtt_ref, a_ref, at_ref, z_ref, dsk_ref, nw_ref,
             yn_ref, y_ref, st_ref, ynt_ref, state):
        c = pl.program_id(1)

        @pl.when(c == 0)
        def _():
            state[...] = jnp.zeros_like(state)

        st = state[...]
        st_ref[...] = st
        xs = xs_ref[...]
        bm = b_ref[...].astype(BF16)
        cm = c_ref[...].astype(BF16)
        dt = dt_ref[...]
        acum, acum_t, li, si, _ = _chunk_terms(dt, dtt_ref[...], a_ref[...], at_ref[...])
        dt_e, e_a, t_e, e_last = _head_lanes(dt, acum, gw)
        xdt = xs * dt_e
        xdtb = xdt.astype(BF16)
        cb = _dot_nt(cm, bm)
        causal = li >= si
        lane = lax.broadcasted_iota(jnp.int32, (1, LANES), 1)
        y_ref[...] = _dot(cm, st.astype(BF16)) * e_a
        for q in range(gw // LANES):
            ql = slice(q * LANES, (q + 1) * LANES)
            xq = xdtb[:, ql]
            ms = []
            for i in range(per_tile):
                h = q * per_tile + i
                decay = jnp.exp(jnp.where(causal, acum[:, h:h + 1] - acum_t[h:h + 1, :], NEG_INF))
                ms.append((cb * decay).astype(BF16))
            y_all = _dot(jnp.concatenate(ms, axis=0), xq)
            yd = y_all[0:CHUNK]
            for i in range(1, per_tile):
                yd = jnp.where(lane >= i * p, y_all[i * CHUNK:(i + 1) * CHUNK], yd)
            y_ref[:, ql] += yd
        state[...] = st * e_last + _dot_tn(bm, (xdt * t_e).astype(BF16))
        yt = y_ref[...] + xs * dsk_ref[...]
        z = z_ref[...]
        yz = yt * (z * _sigmoid(z))
        r = lax.rsqrt(jnp.mean(yz * yz, axis=-1, keepdims=True) + RMS_EPS)
        yn = yz * r * nw_ref[...]
        yn_ref[...] = yn.astype(BF16)
        ynt_ref[...] = yn.T.astype(BF16)

    wide = pl.BlockSpec((CHUNK, gw), lambda g, c: (c, g))
    return pl.pallas_call(
        body, name="ssd_fwd", grid=(SSM_GROUPS, nc),
        in_specs=[wide,
                  pl.BlockSpec((CHUNK, n), lambda g, c: (c, b0 + g)),
                  pl.BlockSpec((CHUNK, n), lambda g, c: (c, c0 + g)),
                  pl.BlockSpec((None, CHUNK, hpg), lambda g, c: (g, c, 0)),
                  pl.BlockSpec((None, hpg, CHUNK), lambda g, c: (g, 0, c)),
                  pl.BlockSpec((None, 1, hpg), lambda g, c: (g, 0, 0)),
                  pl.BlockSpec((None, hpg, 1), lambda g, c: (g, 0, 0)),
                  wide,
                  pl.BlockSpec((None, 1, gw), lambda g, c: (g, 0, 0)),
                  pl.BlockSpec((1, gw), lambda g, c: (0, g))],
        out_specs=(wide, wide, pl.BlockSpec((None, None, n, gw), lambda g, c: (g, c, 0, 0)),
                   pl.BlockSpec((gw, CHUNK), lambda g, c: (g, c))),
        out_shape=(jax.ShapeDtypeStruct((s, d_inner), BF16), jax.ShapeDtypeStruct((s, d_inner), F32),
                   jax.ShapeDtypeStruct((SSM_GROUPS, nc, n, gw), F32), jax.ShapeDtypeStruct((d_inner, s), BF16)),
        scratch_shapes=[pltpu.VMEM((n, gw), F32)],
        compiler_params=_params("parallel", "arbitrary"),
    )(xbc, xbc, xbc, dt_g, dt_gt, a_g, a_gt, proj, dskip_e, norm_w)


def _ssd_epilogue_bwd(dyn, y, xbc, proj, dskip_e, norm_w, hpg):
    s, d_inner = dyn.shape
    gw = d_inner // SSM_GROUPS
    p = gw // hpg
    rows = _pick(s, (4 * CHUNK, 2 * CHUNK, CHUNK))
    nc = s // rows

    def body(dyn_ref, y_ref, xs_ref, z_ref, dsk_ref, nw_ref, dy_ref, dz_ref, dnw_ref, ddsk_ref, dzt_ref):
        c = pl.program_id(1)
        xs = xs_ref[...]
        z = z_ref[...]
        yt = y_ref[...] + xs * dsk_ref[...]
        sg = _sigmoid(z)
        sz = z * sg
        yz = yt * sz
        r = lax.rsqrt(jnp.mean(yz * yz, axis=-1, keepdims=True) + RMS_EPS)
        dynv = dyn_ref[...]
        dyh = dynv * nw_ref[...]
        dyz = r * (dyh - yz * (r * r) * jnp.mean(dyh * yz, axis=-1, keepdims=True))
        dyt = dyz * sz
        dy_ref[...] = dyt
        dz = dyz * yt * (sg * (1.0 + z * (1.0 - sg)))
        dz_ref[...] = dz.astype(BF16)
        dzt_ref[...] = dz.T.astype(BF16)

        @pl.when(c == 0)
        def _():
            dnw_ref[...] = jnp.zeros_like(dnw_ref)
            ddsk_ref[...] = jnp.zeros_like(ddsk_ref)

        dnw_ref[...] += jnp.sum(dynv * yz * r, axis=0, keepdims=True)
        colsum = jnp.sum(dyt * xs, axis=0, keepdims=True)
        fold = (lax.broadcasted_iota(jnp.int32, (gw, hpg), 0) // p
                == lax.broadcasted_iota(jnp.int32, (gw, hpg), 1)).astype(F32)
        ddsk_ref[...] += jnp.dot(colsum, fold, preferred_element_type=F32, precision=HIGHEST)

    wide = pl.BlockSpec((rows, gw), lambda g, c: (c, g))
    return pl.pallas_call(
        body, name="ssd_epilogue_bwd", grid=(SSM_GROUPS, nc),
        in_specs=[wide, wide, wide, wide, pl.BlockSpec((None, 1, gw), lambda g, c: (g, 0, 0)),
                  pl.BlockSpec((1, gw), lambda g, c: (0, g))],
        out_specs=(wide, wide, pl.BlockSpec((1, gw), lambda g, c: (0, g)),
                   pl.BlockSpec((None, 1, hpg), lambda g, c: (g, 0, 0)),
                   pl.BlockSpec((gw, rows), lambda g, c: (g, c))),
        out_shape=(jax.ShapeDtypeStruct((s, d_inner), F32), jax.ShapeDtypeStruct((s, proj.shape[1]), BF16),
                   jax.ShapeDtypeStruct((1, d_inner), F32), jax.ShapeDtypeStruct((SSM_GROUPS, 1, hpg), F32),
                   jax.ShapeDtypeStruct((proj.shape[1], s), BF16)),
        compiler_params=_params("parallel", "arbitrary"),
    )(dyn, y, xbc, proj, dskip_e, norm_w)


def _ssd_scan_bwd(xbc, dt_g, dt_gt, a_g, a_gt, states, dy, dskip_e, d_inner, n_state, ride=None):
    s = xbc.shape[0]
    hpg = dt_g.shape[2]
    gw = d_inner // SSM_GROUPS
    p = gw // hpg
    nc = s // CHUNK
    n = n_state
    b0 = d_inner // n
    c0 = b0 + SSM_GROUPS
    per_tile = LANES // p

    def body(xs_ref, b_ref, c_ref, dt_ref, dtt_ref, a_ref, at_ref, st_ref, dy_ref, dsk_ref,
             dxs_ref, db_ref, dc_ref, ddt_ref, da_ref, dstate, ydiag_ref, dxd_ref):
        c = pl.program_id(1)

        @pl.when(c == 0)
        def _():
            dstate[...] = jnp.zeros_like(dstate)
            da_ref[...] = jnp.zeros_like(da_ref)

        xs = xs_ref[...]
        bm = b_ref[...].astype(BF16)
        cm = c_ref[...].astype(BF16)
        dt = dt_ref[...]
        a = a_ref[...]
        dyv = dy_ref[...]
        dsk = dsk_ref[...]
        acum, acum_t, li, si, upper = _chunk_terms(dt, dtt_ref[...], a, at_ref[...])
        dt_e, e_a, t_e, e_last = _head_lanes(dt, acum, gw)
        cb = _dot_nt(cm, bm)
        lower_mask = li >= si
        lane = lax.broadcasted_iota(jnp.int32, (1, LANES), 1)
        row_l = lax.broadcasted_iota(jnp.int32, (CHUNK, 1), 0)
        st = st_ref[...]
        stb = st.astype(BF16)
        dst = dstate[...]
        dstb = dst.astype(BF16)
        xdt = xs * dt_e
        xdtb = xdt.astype(BF16)
        dyb = dyv.astype(BF16)
        dye = dyv * e_a
        dyeb = dye.astype(BF16)
        xte = xdt * t_e
        xteb = xte.astype(BF16)
        wv = _dot(bm, dstb)
        yo = _dot(cm, stb)
        dcb = jnp.zeros((CHUNK, CHUNK), F32)
        for q in range(gw // LANES):
            ql = slice(q * LANES, (q + 1) * LANES)
            xq = xdtb[:, ql]
            dq = dyb[:, ql]
            decays, ms, mts, dqs = [], [], [], []
            for i in range(per_tile):
                h = q * per_tile + i
                decay = jnp.exp(jnp.where(lower_mask, acum[:, h:h + 1] - acum_t[h:h + 1, :], NEG_INF))
                mm = cb * decay
                mine = (lane >= i * p) & (lane < (i + 1) * p)
                decays.append(decay)
                ms.append(mm.astype(BF16))
                mts.append(mm.T.astype(BF16))
                dqs.append(jnp.where(mine, dq, jnp.zeros_like(dq)))
            dm_all = _dot_nt(jnp.concatenate(dqs, axis=0), xq)
            y_all = _dot(jnp.concatenate(ms, axis=0), xq)
            d_all = _dot(jnp.concatenate(mts, axis=0), dq)
            yd = dd = None
            for i in range(per_tile):
                rows = slice(i * CHUNK, (i + 1) * CHUNK)
                dcb = dcb + dm_all[rows] * decays[i]
                yd = y_all[rows] if i == 0 else jnp.where(lane >= i * p, y_all[rows], yd)
                dd = d_all[rows] if i == 0 else jnp.where(lane >= i * p, d_all[rows], dd)
            ydiag_ref[:, ql] = yd
            dxd_ref[:, ql] = dd
        ydiag = ydiag_ref[...]
        dxd = dxd_ref[...]
        dxdt = dxd + t_e * wv
        xw = xte * wv
        last_in = jnp.sum(xw, axis=0, keepdims=True) + e_last * jnp.sum(dst * st, axis=0, keepdims=True)
        folded = _fold_heads(jnp.concatenate(
            [dyb.astype(F32) * ydiag - xdtb.astype(F32) * dxd - xw + dye * yo, dxdt * xs,
             jnp.broadcast_to(last_in, (SUBLANES, gw))],
            axis=0), hpg)
        dalast = folded[2 * CHUNK:2 * CHUNK + 1]
        d_acum = folded[0:CHUNK] + jnp.where(row_l == CHUNK - 1, dalast, 0.0)
        ddt_x = folded[CHUNK:2 * CHUNK]
        dxs_ref[...] = dxdt * dt_e + dyv * dsk
        dbf = dcb.astype(BF16)
        dc_ref[...] = _dot_nt(dyeb, stb) + _dot(dbf, bm)
        db_ref[...] = _dot_nt(xteb, dstb) + _dot_tn(dbf, cm)
        dstate[...] = dst * e_last + _dot_tn(cm, dyeb)
        d_da = jnp.dot(upper, d_acum, preferred_element_type=F32, precision=HIGHEST)
        ddt_ref[...] = d_da * a + ddt_x
        da_ref[...] += jnp.sum(d_da * dt, axis=0, keepdims=True)

    rev = lambda c: nc - 1 - c
    wide = pl.BlockSpec((CHUNK, gw), lambda g, c: (rev(c), g))
    return _call(
        body, name="ssd_scan_bwd", grid=(SSM_GROUPS, nc),
        in_specs=[wide,
                  pl.BlockSpec((CHUNK, n), lambda g, c: (rev(c), b0 + g)),
                  pl.BlockSpec((CHUNK, n), lambda g, c: (rev(c), c0 + g)),
                  pl.BlockSpec((None, CHUNK, hpg), lambda g, c: (g, rev(c), 0)),
                  pl.BlockSpec((None, hpg, CHUNK), lambda g, c: (g, 0, rev(c))),
                  pl.BlockSpec((None, 1, hpg), lambda g, c: (g, 0, 0)),
                  pl.BlockSpec((None, hpg, 1), lambda g, c: (g, 0, 0)),
                  pl.BlockSpec((None, None, n, gw), lambda g, c: (g, rev(c), 0, 0)),
                  wide,
                  pl.BlockSpec((None, 1, gw), lambda g, c: (g, 0, 0))],
        out_specs=[wide,
                   pl.BlockSpec((CHUNK, n), lambda g, c: (rev(c), g)),
                   pl.BlockSpec((CHUNK, n), lambda g, c: (rev(c), g)),
                   pl.BlockSpec((None, CHUNK, hpg), lambda g, c: (g, rev(c), 0)),
                   pl.BlockSpec((None, 1, hpg), lambda g, c: (g, 0, 0))],
        out_shape=[jax.ShapeDtypeStruct((s, d_inner), F32),
                   jax.ShapeDtypeStruct((s, SSM_GROUPS * n), F32), jax.ShapeDtypeStruct((s, SSM_GROUPS * n), F32),
                   jax.ShapeDtypeStruct((SSM_GROUPS, s, hpg), F32), jax.ShapeDtypeStruct((SSM_GROUPS, 1, hpg), F32)],
        scratch=[pltpu.VMEM((n, gw), F32), pltpu.VMEM((CHUNK, gw), F32), pltpu.VMEM((CHUNK, gw), F32)],
        sem=("parallel", "arbitrary"), ride=ride,
        args=(xbc, xbc, xbc, dt_g, dt_gt, a_g, a_gt, states, dy, dskip_e))


def _lin(p):
    return 4 * p[0] + 2 * p[1] + p[2]


class _Gather:
    def __init__(self, arrs, rows=None, into=None):
        self.arrs = list(arrs)
        self.rows = rows
        self.into = list(into) if into is not None else []

    def out_shape(self):
        if self.rows is None:
            return [jax.ShapeDtypeStruct((NDEV,) + a.shape, a.dtype) for a in self.arrs]
        return [jax.ShapeDtypeStruct((NDEV, self.rows[1]) + a.shape[1:], a.dtype) for a in self.arrs]

    def _copies(self, ins, outs, sems):
        send_sems, recv_sems, local_sems = sems
        x, y, c = lax.axis_index("x"), lax.axis_index("y"), lax.axis_index("c")
        me, sibling = (x, y, c), (x, y, 1 - c)
        chips = [(1 - x, y), (x, 1 - y), (1 - x, 1 - y)]

        def slot(a, block):
            if self.rows is None:
                return outs[a].at[_lin(block)]
            return outs[a].at[_lin(block), pl.ds(self.rows[0], ins[a].shape[0])]

        def copy(a, k, block, to, src=None):
            rows = slot(a, block)
            return pltpu.make_async_remote_copy(
                src_ref=rows if src is None else src, dst_ref=rows,
                send_sem=send_sems.at[a * NPEER + k], recv_sem=recv_sems.at[a * NPEER + k],
                device_id=to, device_id_type=pl.DeviceIdType.MESH)

        na = len(ins)
        mine = [pltpu.make_async_copy(ins[a], slot(a, me), local_sems.at[a]) for a in range(na)]
        first = []
        for a in range(na):
            first.append(copy(a, 0, me, sibling, src=ins[a]))
            first += [copy(a, 1 + j, me, (*chip, c), src=ins[a]) for j, chip in enumerate(chips[:2])]
        return copy, mine, first, me, sibling, chips, c, na

    def start(self, ins, outs, sems):
        _, mine, first, *_ = self._copies(ins, outs, sems)
        for cp in mine + first:
            cp.start()

    def finish(self, ins, outs, sems):
        copy, mine, first, me, sibling, chips, c, na = self._copies(ins, outs, sems)
        x, y = me[0], me[1]
        xn, yn, dg = chips
        zero = c == 0
        via = (jnp.where(zero, 1 - x, x), jnp.where(zero, y, 1 - y))
        nxt = (jnp.where(zero, x, 1 - x), jnp.where(zero, 1 - y, y))
        passed = []
        for a in range(na):
            copy(a, 1, (*xn, c), me).wait_recv()
            copy(a, 2, (*yn, c), me).wait_recv()
            passed += [copy(a, 3, (*via, c), (*nxt, c)), copy(a, 4, (*xn, c), sibling), copy(a, 5, (*yn, c), sibling)]
            for cp in passed[-3:]:
                cp.start()
        for a in range(na):
            copy(a, 3, (*dg, c), me).wait_recv()
            passed.append(copy(a, 6, (*dg, c), sibling))
            passed[-1].start()
        for a in range(na):
            copy(a, 0, sibling, me).wait_recv()
            for j, chip in enumerate(chips):
                copy(a, 4 + j, (*chip, 1 - c), me).wait_recv()
        for cp in first + passed:
            cp.wait_send()
        for cp in mine:
            cp.wait()


class _Scatter:
    def __init__(self, arrs, ks=tuple(range(NDEV))):
        self.arrs = list(arrs)
        self.ks = [tuple(k) for k in ks] if isinstance(ks[0], (tuple, list)) else [tuple(ks)] * len(self.arrs)
        assert len(self.ks) == len(self.arrs)

    def out_shape(self):
        return [jax.ShapeDtypeStruct((len(k),) + a.shape[1:], a.dtype) for a, k in zip(self.arrs, self.ks)]

    def _copies(self, ins, outs, sems):
        send_sems, recv_sems, local_sems = sems
        x, y, c = lax.axis_index("x"), lax.axis_index("y"), lax.axis_index("c")
        me = (x, y, c)

        def peer(k):
            return (1 - x if k & 4 else x, 1 - y if k & 2 else y, 1 - c if k & 1 else c)

        local, remote = [], []
        for a in range(len(ins)):
            for i, k in enumerate(self.ks[a]):
                if k == 0:
                    local.append(pltpu.make_async_copy(ins[a].at[_lin(me)], outs[a].at[i], local_sems.at[a]))
                else:
                    remote.append(pltpu.make_async_remote_copy(
                        src_ref=ins[a].at[_lin(peer(k))], dst_ref=outs[a].at[i],
                        send_sem=send_sems.at[a * NPEER + k - 1], recv_sem=recv_sems.at[a * NPEER + k - 1],
                        device_id=peer(k), device_id_type=pl.DeviceIdType.MESH))
        return local, remote

    def start(self, ins, outs, sems):
        local, remote = self._copies(ins, outs, sems)
        for cp in local + remote:
            cp.start()

    def finish(self, ins, outs, sems):
        local, remote = self._copies(ins, outs, sems)
        for cp in remote:
            cp.wait_recv()
        for cp in remote:
            cp.wait_send()
        for cp in local:
            cp.wait()


def _exchange_scratch(na):
    return [pltpu.SemaphoreType.DMA((na * NPEER,)), pltpu.SemaphoreType.DMA((na * NPEER,)),
            pltpu.SemaphoreType.DMA((na,))]


def _exchange_alone(ex, *, name, in_vmem=False):
    na = len(ex.arrs)

    def body(*refs):
        ins, outs, sems = refs[:na], refs[na:2 * na], refs[2 * na:]
        ex.start(ins, outs, sems)
        ex.finish(ins, outs, sems)

    spec = pl.BlockSpec(memory_space=pltpu.VMEM if in_vmem else pl.ANY)
    return pl.pallas_call(
        body, name=name, out_shape=tuple(ex.out_shape()), in_specs=[spec] * na, out_specs=tuple([spec] * na),
        scratch_shapes=_exchange_scratch(na),
        compiler_params=pltpu.CompilerParams(vmem_limit_bytes=VMEM_LIMIT),
    )(*ex.arrs)


def _call(body, *, name, grid, in_specs, out_specs, out_shape, args, sem, scratch=(), ride=None, aliases=None):
    n_in, n_out, n_scr = len(in_specs), len(out_specs), len(scratch)
    if ride is None:
        outs = pl.pallas_call(
            body, name=name, grid=grid, in_specs=list(in_specs), out_specs=tuple(out_specs),
            out_shape=tuple(out_shape), scratch_shapes=list(scratch), input_output_aliases=aliases or {},
            compiler_params=_params(*sem))(*args)
        return tuple(outs), ()
    nx = len(ride.arrs)
    into = getattr(ride, "into", [])
    hbm = pl.BlockSpec(memory_space=pl.ANY)
    aliases = dict(aliases or {})
    aliases.update({n_in + nx + i: n_out + i for i in range(len(into))})

    def hosted(*refs):
        ins, x_in = refs[:n_in], refs[n_in:n_in + nx]
        o0 = n_in + nx + len(into)
        outs, x_out = refs[o0:o0 + n_out], refs[o0 + n_out:o0 + n_out + nx]
        s0 = o0 + n_out + nx
        scr, x_sem = refs[s0:s0 + n_scr], refs[s0 + n_scr:]
        ids = [pl.program_id(i) for i in range(len(grid))]
        first = functools.reduce(jnp.logical_and, [i == 0 for i in ids])
        last = functools.reduce(jnp.logical_and, [i == g - 1 for i, g in zip(ids, grid)])

        @pl.when(first)
        def _():
            ride.start(x_in, x_out, x_sem)

        body(*ins, *outs, *scr)

        @pl.when(last)
        def _():
            ride.finish(x_in, x_out, x_sem)

    outs = pl.pallas_call(
        hosted, name=name, grid=grid, in_specs=list(in_specs) + [hbm] * (nx + len(into)),
        out_specs=tuple(list(out_specs) + [hbm] * nx), out_shape=tuple(list(out_shape) + ride.out_shape()),
        scratch_shapes=list(scratch) + _exchange_scratch(nx), input_output_aliases=aliases,
        compiler_params=_params(*(("arbitrary",) * len(grid))))(*args, *ride.arrs, *into)
    return tuple(outs[:n_out]), tuple(outs[n_out:])


def _pack(parts):
    flat = jnp.concatenate([p.reshape(-1).astype(F32) for p in parts])
    tile = SUBLANES * LANES
    pad = (-flat.shape[0]) % tile
    return jnp.pad(flat, (0, pad)).reshape(-1, LANES)


def _unpack(buf, shapes):
    flat = buf.reshape(-1)
    out, off = [], 0
    for shp in shapes:
        size = math.prod(shp)
        out.append(flat[off:off + size].reshape(shp))
        off += size
    return out


KS_FLAT = (0, 1, 4, 5, 2, 3)
KS_DIAG = (6, 7)


def _local_step(x, target, wa, wo, ws, wos, rel_bias, conv_w, conv_b, dt_bias, a_log, d_skip, norm_w, ln_g, ln_b,
                dist=False):
    s, d = x.shape
    da = wo.shape[-2]
    heads = da // HEAD_DIM
    qkv_cols = 3 * N_GROUPS_ATTN * da
    d_inner = wos.shape[0] * (NDEV if dist else 1)
    conv_dim = conv_w.shape[1]
    ssm_heads = dt_bias.shape[1]
    hpg = ssm_heads // SSM_GROUPS
    gn = (conv_dim - d_inner) // 2
    n_state = gn // SSM_GROUPS
    gw = d_inner // SSM_GROUPS
    p = gw // hpg
    in_ssm = d_inner + conv_dim + ssm_heads
    xb, xbt = _cast_bf16(x, name="cast_x", with_transpose=True)

    per_dev = in_ssm // NDEV
    third = (per_dev // 3) // 16 * 16
    band_rows = (third, third, per_dev - 2 * third)
    qkvs, ws_all, row0 = [], None, 0
    for g in range(N_GROUPS_ATTN):
        ride = None
        if dist:
            ride = _Gather([ws[row0:row0 + band_rows[g]]], rows=(row0, per_dev), into=ws_all)
            row0 += band_rows[g]
        got = _mm(xb, wa, name=f"mm_qkv_g{g}", out_dtype=BF16, n_off=g * 3 * da, n_out=3 * da, ride=ride)
        if dist:
            ws_all = list(got[1])
            got = got[0]
        qkvs.append(got)
    if dist:
        ws = ws_all[0].reshape(in_ssm, d)
    gate = _mm(xb, wa, name="mm_gate", out_dtype=F32, n_off=qkv_cols, n_out=da)
    bias, bucket = _bias_tables(rel_bias, heads)
    os_, ls_ = [], []
    for g, (_, dil) in enumerate(ATTN_PATTERNS):
        ride = _Gather([wo]) if dist and g == 0 else None
        o, l, rode = _attn_fwd_group(qkvs[g], bias[g], g, dil, da, ride=ride)
        if rode:
            (wo,) = rode
        os_.append(o)
        ls_.append(l)
    o, lse, y, yt = _attn_combine(os_, ls_, gate)
    h1 = _mm(y, wo, name="mm_out_attn", out_dtype=F32)
    x1, x1b = _ln_fwd(x, h1, ln_g[0:1], ln_b[0:1], name="ln1_fwd")

    if dist:
        proj, (wos_slabs,) = _mm(x1b, ws, name="mm_in_ssm", out_dtype=F32, trans_b=True, ride=_Gather([wos]))
        wos = wos_slabs.reshape(d_inner, d)
    else:
        proj = _mm(x1b, ws, name="mm_in_ssm", out_dtype=F32, trans_b=True)
    xbc = _conv_fwd(proj, conv_w, conv_b, d_inner)
    dt = _dt_fwd(proj, dt_bias, d_inner + conv_dim)
    dt_g = dt.reshape(s, SSM_GROUPS, hpg).transpose(1, 0, 2)
    dt_gt = dt.reshape(s, SSM_GROUPS, hpg).transpose(1, 2, 0)
    a = -jnp.exp(a_log)
    a_g = a.reshape(SSM_GROUPS, 1, hpg)
    a_gt = a.reshape(SSM_GROUPS, hpg, 1)
    dskip_e = jnp.repeat(d_skip.reshape(SSM_GROUPS, 1, hpg), p, axis=2)
    yn, yscan, states, ynt = _ssd_fwd(xbc, proj, dt_g, dt_gt, a_g, a_gt, dskip_e, norm_w, d_inner, n_state)
    h2 = _mm(yn, wos, name="mm_out_ssm", out_dtype=F32)

    du2, du2b, dg1, db1, loss_t = _ln_bwd(x1, h2, ln_g[1:2], ln_b[1:2], target, with_loss=True, name="ln2_loss_bwd")
    loss = loss_t[0, 0]
    dyn = _mm(du2b, wos, name="mm_dyn", out_dtype=F32, trans_b=True)
    g_wos = _mm(ynt, du2b, name="mm_dw_out_ssm", out_dtype=BF16)
    parts = {}
    dyscan, dproj_ssm, g_norm, g_dskip, dproj_t = _ssd_epilogue_bwd(dyn, yscan, xbc, proj, dskip_e, norm_w, hpg)
    ride = _Scatter([g_wos.reshape(NDEV, d_inner // NDEV, d)]) if dist else None
    (dxs, d_bm, d_cm, ddt_g, g_a), rode = _ssd_scan_bwd(xbc, dt_g, dt_gt, a_g, a_gt, states, dyscan, dskip_e,
                                                         d_inner, n_state, ride=ride)
    parts["w_out_ssm"] = [list(rode)]
    g_alog = g_a.reshape(1, ssm_heads) * a
    dproj_ssm, g_conv_w, g_conv_b, dproj_t = _conv_bwd(proj, conv_w, conv_b, (dxs, d_bm, d_cm), d_inner, dproj_ssm,
                                                       dproj_t)
    ddt = ddt_g.transpose(1, 0, 2).reshape(s, ssm_heads)
    dproj_ssm, g_dtb, dproj_t = _dt_bwd(proj, dt_bias, ddt, d_inner + conv_dim, dproj_ssm, dproj_t)
    g_ws = _mm(dproj_t, x1b, name="mm_dw_in_ssm", out_dtype=BF16)
    if dist:
        g_ws_slabs = g_ws.reshape(NDEV, per_dev, d)
        dx1, near = _mm(dproj_ssm, ws, name="mm_dx1", out_dtype=F32, res=du2, res_scale=DEEPNORM_ALPHA,
                        ride=_Scatter([g_ws_slabs], KS_FLAT))
    else:
        dx1 = _mm(dproj_ssm, ws, name="mm_dx1", out_dtype=F32, res=du2, res_scale=DEEPNORM_ALPHA)

    du1, du1b, dg0, db0 = _ln_bwd(x, h1, ln_g[0:1], ln_b[0:1], dx1, with_loss=False, name="ln1_bwd")
    dy = _mm(du1b, wo, name="mm_dy", out_dtype=F32, trans_b=True)
    g_wo = _mm(yt, du1b, name="mm_dw_out_attn", out_dtype=BF16, slab_out=NDEV)
    do, dgate, dd = _attn_bwd_prep(dy, o, gate)
    rides = [_Scatter([g_ws_slabs], KS_DIAG[0:1]), _Scatter([g_wo]), None] if dist else [None] * N_GROUPS_ATTN
    dparts, dss, rode_attn = [], [], []
    for g, (_, dil) in enumerate(ATTN_PATTERNS):
        (dq, dk, dv, ds), rode = _attn_bwd_group(qkvs[g], do, lse, dd, bias[g], g, dil, da, ride=rides[g])
        dparts += [dq, dk, dv]
        dss.append(ds)
        rode_attn += list(rode)
    g_bias = _bias_bwd(jnp.stack(dss), bucket)
    g_rel_bias = g_bias.transpose(2, 0, 1).reshape(NUM_BUCKETS, N_GROUPS_ATTN * heads)
    dproj_attn = jnp.concatenate(dparts + [dgate], axis=1)
    pending = None
    if dist:
        parts["w_out_attn"] = [rode_attn[1:]]
        half = d // 2
        g_top, (diag_b,) = _mm(xbt[:half], dproj_attn, name="mm_dw_in_attn_top", out_dtype=BF16, slab_out=NDEV,
                               ride=_Scatter([g_ws_slabs], KS_DIAG[1:2]))
        parts["w_in_ssm"] = [[near[0], rode_attn[0], diag_b]]
        g_bot, (top_a,) = _mm(xbt[half:], dproj_attn, name="mm_dw_in_attn_bottom", out_dtype=BF16, slab_out=NDEV,
                              ride=_Scatter([g_top], KS_FLAT))
        dx, (top_b, bot_a) = _mm(dproj_attn, wa, name="mm_dx", out_dtype=F32, trans_b=True, res=du1,
                                 res_scale=DEEPNORM_ALPHA, ride=_Scatter([g_top, g_bot], [KS_DIAG, KS_FLAT]))
        parts["w_in_attn"] = [[top_a, top_b], [bot_a]]
        pending = [_Scatter([g_bot], KS_DIAG[0:1]), _Scatter([g_bot], KS_DIAG[1:2])]
    else:
        g_wa = _mm(xbt, dproj_attn, name="mm_dw_in_attn", out_dtype=BF16, slab_out=NDEV)
        dx = _mm(dproj_attn, wa, name="mm_dx", out_dtype=F32, trans_b=True, res=du1, res_scale=DEEPNORM_ALPHA)

    g_ln_g = jnp.concatenate([dg0, dg1], axis=0)
    g_ln_b = jnp.concatenate([db0, db1], axis=0)
    small = dict(rel_bias=g_rel_bias, dt_bias=g_dtb, a_log=g_alog, d_skip=g_dskip.reshape(1, ssm_heads),
                 ln_g=g_ln_g, ln_b=g_ln_b, conv_w=g_conv_w, conv_b=g_conv_b, ssm_norm_w=g_norm)
    if dist:
        return loss, dx, parts, pending, small
    return loss, dx, g_wa, g_wo, g_ws, g_wos, small


REPLICATED = ("rel_bias", "dt_bias", "a_log", "d_skip", "ln_g", "ln_b")
SHARDED_SMALL = ("conv_w", "conv_b", "ssm_norm_w")


def kernel(x, w_in_attn, w_out_attn, rel_bias, w_in_ssm, conv_w, conv_b, dt_bias, a_log, d_skip, ssm_norm_w, w_out_ssm, ln_g, ln_b, loss_target, m_w_in_attn, m_w_out_attn, m_rel_bias, m_w_in_ssm, m_conv_w, m_conv_b, m_dt_bias, m_a_log, m_d_skip, m_ssm_norm_w, m_w_out_ssm, m_ln_g, m_ln_b, v_w_in_attn, v_w_out_attn, v_rel_bias, v_w_in_ssm, v_conv_w, v_conv_b, v_dt_bias, v_a_log, v_d_skip, v_ssm_norm_w, v_w_out_ssm, v_ln_g, v_ln_b):
    w = dict(w_in_attn=w_in_attn, w_out_attn=w_out_attn, rel_bias=rel_bias, w_in_ssm=w_in_ssm, conv_w=conv_w,
             conv_b=conv_b, dt_bias=dt_bias, a_log=a_log, d_skip=d_skip, ssm_norm_w=ssm_norm_w,
             w_out_ssm=w_out_ssm, ln_g=ln_g, ln_b=ln_b)
    m = dict(w_in_attn=m_w_in_attn, w_out_attn=m_w_out_attn, rel_bias=m_rel_bias, w_in_ssm=m_w_in_ssm,
             conv_w=m_conv_w, conv_b=m_conv_b, dt_bias=m_dt_bias, a_log=m_a_log, d_skip=m_d_skip,
             ssm_norm_w=m_ssm_norm_w, w_out_ssm=m_w_out_ssm, ln_g=m_ln_g, ln_b=m_ln_b)
    v = dict(w_in_attn=v_w_in_attn, w_out_attn=v_w_out_attn, rel_bias=v_rel_bias, w_in_ssm=v_w_in_ssm,
             conv_w=v_conv_w, conv_b=v_conv_b, dt_bias=v_dt_bias, a_log=v_a_log, d_skip=v_d_skip,
             ssm_norm_w=v_ssm_norm_w, w_out_ssm=v_w_out_ssm, ln_g=v_ln_g, ln_b=v_ln_b)
    me = _lin((lax.axis_index("x"), lax.axis_index("y"), lax.axis_index("c")))
    d = x.shape[2]
    big = ("w_in_attn", "w_out_attn", "w_in_ssm", "w_out_ssm")

    for t in (w, m, v):
        t["w_in_ssm"] = t["w_in_ssm"].transpose(0, 2, 1)
    shards = {k: _cast_bf16(w[k], name=f"cast_{k}") for k in big}
    (wa,) = _exchange_alone(_Gather([shards["w_in_attn"]]), name="gather_w_in_attn")
    cpd = conv_w.shape[2]
    npd = ssm_norm_w.shape[1]
    small_shapes = [(CONV_WIDTH, cpd), (1, cpd), (1, npd)]
    (small_all,) = _exchange_alone(_Gather([_pack([conv_w[0], conv_b, ssm_norm_w])]), name="gather_small_weights",
                                   in_vmem=True)
    small_parts = [_unpack(small_all[i], small_shapes) for i in range(NDEV)]
    conv_w_full = jnp.concatenate([p[0] for p in small_parts], axis=1)
    conv_b_full = jnp.concatenate([p[1] for p in small_parts], axis=1)
    norm_w_full = jnp.concatenate([p[2] for p in small_parts], axis=1)

    loss, dx, parts, pending, small = _local_step(
        x[0], loss_target[0], wa, shards["w_out_attn"], shards["w_in_ssm"], shards["w_out_ssm"], rel_bias,
        conv_w_full, conv_b_full, dt_bias[0:1], a_log[0:1], d_skip[0:1], norm_w_full, ln_g, ln_b, dist=True)
    loss = lax.psum(loss, MESH_AXES)
    out = {}
    for k, ride in zip(("w_in_ssm", "w_out_ssm"), pending):
        out[k], late = _adamw_sum(parts[k], w[k], m[k], v[k], name=f"adamw_{k}", ride=ride)
        parts["w_in_attn"][1] += list(late)
    out["w_in_ssm"] = tuple(t.transpose(0, 2, 1) for t in out["w_in_ssm"])
    for k in ("w_out_attn", "w_in_attn"):
        out[k] = _adamw_sum(parts[k], w[k], m[k], v[k], name=f"adamw_{k}")

    order = REPLICATED + SHARDED_SMALL
    g_shapes = [small[k].shape for k in order]
    (g_all,) = _exchange_alone(_Gather([_pack([small[k] for k in order])]), name="gather_small_grads", in_vmem=True)
    g_sum = dict(zip(order, _unpack(_sum_slots(g_all, name="sum_small_grads"), g_shapes)))
    g_mine = {k: g_sum[k] for k in REPLICATED}
    g_mine["conv_w"] = lax.dynamic_slice_in_dim(g_sum["conv_w"], me * cpd, cpd, axis=1)
    g_mine["conv_b"] = lax.dynamic_slice_in_dim(g_sum["conv_b"], me * cpd, cpd, axis=1)
    g_mine["ssm_norm_w"] = lax.dynamic_slice_in_dim(g_sum["ssm_norm_w"], me * npd, npd, axis=1)
    w_shapes = [w[k].shape for k in order]
    g_pack = _pack([g_mine[k] for k in order])
    d_p, m_p, v_p = _adamw_small(g_pack, _pack([w[k] for k in order]), _pack([m[k] for k in order]),
                                 _pack([v[k] for k in order]), name="adamw_small")
    for k, gk, dk, mk, vk in zip(order, _unpack(g_pack, w_shapes), _unpack(d_p, w_shapes), _unpack(m_p, w_shapes),
                                 _unpack(v_p, w_shapes)):
        out[k] = (gk, dk, mk, vk)

    names = ("w_in_attn", "w_out_attn", "rel_bias", "w_in_ssm", "conv_w", "conv_b", "dt_bias", "a_log", "d_skip",
             "ssm_norm_w", "w_out_ssm", "ln_g", "ln_b")
    res = [loss, dx[None]]
    for i in range(4):
        res += [out[k][i] for k in names]
    return tuple(res)
```
